```python
import math
import jax, jax.numpy as jnp
from jax import lax
import numpy as np

D_MODEL = 1024
BATCH = 8
SEQ = 16384
DEPTH = 2

GRID_W = 64
CTX_LEN = 256
N_MIXERS = 2
EXPAND = 2
E_CONV = EXPAND * D_MODEL
CONV_W = 3
E_SSM = EXPAND * D_MODEL
SSM_GROUP = 16
N_GROUPS = E_SSM // SSM_GROUP
SSM_STATE = 64
SCAN_CHUNK = 128
LN_EPS = 1e-5
DN_ALPHA = (2 * DEPTH) ** 0.25
DN_BETA = (8 * DEPTH) ** -0.25
N_CONV_LAYERS = (DEPTH + 1) // 2
N_SSM_LAYERS = DEPTH // 2

kernel_name = "hybrid_shortconv_s5_prefix_dit"


def _layernorm(x, g, b):
    xf = x.astype(jnp.float32)
    mu = jnp.mean(xf, axis=-1, keepdims=True)
    var = jnp.mean(jnp.square(xf - mu), axis=-1, keepdims=True)
    return ((xf - mu) * lax.rsqrt(var + LN_EPS) * g.astype(jnp.float32) + b.astype(jnp.float32)).astype(x.dtype)


def _ada(cvec, w, b):
    m = jax.nn.silu(cvec) @ w + b
    shift, scale, gate = jnp.split(m, 3, axis=-1)
    return shift[..., None, :], scale[..., None, :], gate[..., None, :]


def _shift_conv(u, w, axis):
    n = u.shape[axis]
    pad = [(0, 0)] * u.ndim
    pad[axis] = (1, 1)
    up = jnp.pad(u, pad)
    sl = lambda k: lax.slice_in_dim(up, k, k + n, axis=axis)
    return sl(0) * w[0] + sl(1) * w[1] + sl(2) * w[2]


def _conv_grid(u, w):
    bsz, L, E = u.shape
    rows = L // GRID_W
    half = E // 2
    ug = u.reshape(bsz, rows, GRID_W, E)
    yh = _shift_conv(ug[..., :half], w[:, :half], axis=2)
    yv = _shift_conv(ug[..., half:], w[:, half:], axis=1)
    return jnp.concatenate([yh, yv], axis=-1).reshape(bsz, L, E)


def _conv_mixer(h, w_in, w_conv, w_out, grid):
    bg, cg, v, z = jnp.split(h @ w_in, 4, axis=-1)
    u = cg * v
    yc = _conv_grid(u, w_conv) if grid else _shift_conv(u, w_conv, axis=1)
    return (bg * yc * jax.nn.silu(z)) @ w_out


def _zoh(lam_re, lam_im, log_step, b_re, b_im):
    dt = jnp.exp(log_step)[:, None]
    mag = jnp.exp(lam_re * dt)
    ar = mag * jnp.cos(lam_im * dt)
    ai = mag * jnp.sin(lam_im * dt)
    qr, qi = ar - 1.0, ai
    den = lam_re * lam_re + lam_im * lam_im
    fr = (qr * lam_re + qi * lam_im) / den
    fi = (qi * lam_re - qr * lam_im) / den
    bbr = fr[..., None] * b_re - fi[..., None] * b_im
    bbi = fr[..., None] * b_im + fi[..., None] * b_re
    return ar, ai, bbr, bbi


def _binop(e1, e2):
    a1r, a1i, b1r, b1i = e1
    a2r, a2i, b2r, b2i = e2
    return (a2r * a1r - a2i * a1i,
            a2r * a1i + a2i * a1r,
            a2r * b1r - a2i * b1i + b2r,
            a2r * b1i + a2i * b1r + b2i)


def _s5_scan(u, h0r, h0i, ar, ai, bbr, bbi, c_re, c_im, with_output):
    bsz, L, _ = u.shape
    n_blk = L // SCAN_CHUNK
    ub = u.reshape(bsz, n_blk, SCAN_CHUNK, N_GROUPS, SSM_GROUP).transpose(1, 0, 2, 3, 4)

    def step(carry, u_blk):
        hr, hi = carry
        bur = jnp.einsum('btgp,gnp->btgn', u_blk, bbr)
        bui = jnp.einsum('btgp,gnp->btgn', u_blk, bbi)
        bur = bur.at[:, 0].add(ar * hr - ai * hi)
        bui = bui.at[:, 0].add(ar * hi + ai * hr)
        a_r = jnp.broadcast_to(ar, bur.shape)
        a_i = jnp.broadcast_to(ai, bur.shape)
        _, _, sr, si = lax.associative_scan(_binop, (a_r, a_i, bur, bui), axis=1)
        new = (sr[:, -1], si[:, -1])
        if with_output:
            y = jnp.einsum('btgn,gpn->btgp', sr, c_re) - jnp.einsum('btgn,gpn->btgp', si, c_im)
            return new, y
        return new, None

    h_final, ys = lax.scan(step, (h0r, h0i), ub)
    if with_output:
        ys = ys.transpose(1, 0, 2, 3, 4).reshape(bsz, L, E_SSM)
    return ys, h_final


def _glu_gate_out(y, z, w_glu, b_glu, w_out):
    g = jax.nn.gelu(y)
    g = g * jax.nn.sigmoid(g @ w_glu + b_glu)
    return (g * jax.nn.silu(z)) @ w_out


def _s5_mixer(h_lat, h_ctx, w_in, lam_re, lam_im, log_step, b_re, b_im, c_re, c_im, d,
              w_glu, b_glu, w_out, ctx_out):
    u_l, z_l = jnp.split(h_lat @ w_in, 2, axis=-1)
    pc = h_ctx @ w_in
    u_c = pc[..., :E_SSM]
    y_l = d * u_l
    y_c = d * u_c if ctx_out else None
    for r in range(2):
        ar, ai, bbr, bbi = _zoh(lam_re[r], lam_im[r], log_step[r], b_re[r], b_im[r])
        seq = (lambda t: t[:, ::-1]) if r == 1 else (lambda t: t)
        dtype = jnp.result_type(u_c.dtype, bbr.dtype)
        h0 = jnp.zeros((u_c.shape[0], N_GROUPS, SSM_STATE), dtype)
        yc, hc = _s5_scan(seq(u_c), h0, h0, ar, ai, bbr, bbi, c_re[r], c_im[r], ctx_out)
        yl, _ = _s5_scan(seq(u_l), hc[0], hc[1], ar, ai, bbr, bbi, c_re[r], c_im[r], True)
        y_l = y_l + seq(yl)
        if ctx_out:
            y_c = y_c + seq(yc)
    out_l = _glu_gate_out(y_l, z_l, w_glu, b_glu, w_out)
    out_c = _glu_gate_out(y_c, pc[..., E_SSM:], w_glu, b_glu, w_out) if ctx_out else None
    return out_l, out_c


def _fwd_setup_inputs(seed: int = 0) -> dict:
    key = jax.random.key(seed)
    ks = jax.random.split(key, 24)
    f32 = jnp.float32
    nrm = lambda k, shape, s: jax.random.normal(k, shape, f32) * s
    nA, nB = N_CONV_LAYERS, N_SSM_LAYERS
    G, N, P = N_GROUPS, SSM_STATE, SSM_GROUP
    return {
        "x": nrm(ks[0], (BATCH, SEQ, D_MODEL), 1.0),
        "c": nrm(ks[1], (BATCH, D_MODEL), 1.0),
        "ctx": nrm(ks[2], (BATCH, CTX_LEN, D_MODEL), 1.0),
        "c_ctx": nrm(ks[3], (D_MODEL,), 1.0),
        "ada_w": nrm(ks[4], (DEPTH, D_MODEL, 3 * D_MODEL), D_MODEL ** -0.5),
        "ada_b": nrm(ks[5], (DEPTH, 3 * D_MODEL), 0.02),
        "ln_g": 1.0 + nrm(ks[6], (DEPTH, D_MODEL), 0.02),
        "ln_b": nrm(ks[7], (DEPTH, D_MODEL), 0.02),
        "conv_w_in": nrm(ks[8], (nA, D_MODEL, 4 * E_CONV), D_MODEL ** -0.5),
        "conv_w": nrm(ks[9], (nA, CONV_W, E_CONV), CONV_W ** -0.5),
        "conv_w_out": nrm(ks[10], (nA, E_CONV, D_MODEL), DN_BETA * E_CONV ** -0.5),
        "ssm_w_in": nrm(ks[11], (nB, D_MODEL, 2 * E_SSM), D_MODEL ** -0.5),
        "ssm_lam_re": -0.5 * jnp.exp(nrm(ks[12], (nB, 2, G, N), 0.05)),
        "ssm_lam_im": jnp.pi * jnp.arange(N, dtype=f32) + nrm(ks[13], (nB, 2, G, N), 0.05),
        "ssm_log_step": jax.random.uniform(ks[14], (nB, 2, G), f32, math.log(1e-3), math.log(1e-1)),
        "ssm_b_re": nrm(ks[15], (nB, 2, G, N, P), (2 * P) ** -0.5),
        "ssm_b_im": nrm(ks[16], (nB, 2, G, N, P), (2 * P) ** -0.5),
        "ssm_c_re": nrm(ks[17], (nB, 2, G, P, N), N ** -0.5),
        "ssm_c_im": nrm(ks[18], (nB, 2, G, P, N), N ** -0.5),
        "ssm_d": nrm(ks[19], (nB, E_SSM), 1.0),
        "ssm_w_glu": nrm(ks[20], (nB, E_SSM, E_SSM), E_SSM ** -0.5),
        "ssm_b_glu": nrm(ks[21], (nB, E_SSM), 0.02),
        "ssm_w_out": nrm(ks[22], (nB, E_SSM, D_MODEL), DN_BETA * E_SSM ** -0.5),
    }


def _fwd_reference(x, c, ctx, c_ctx, ada_w, ada_b, ln_g, ln_b, conv_w_in, conv_w, conv_w_out,
              ssm_w_in, ssm_lam_re, ssm_lam_im, ssm_log_step, ssm_b_re, ssm_b_im,
              ssm_c_re, ssm_c_im, ssm_d, ssm_w_glu, ssm_b_glu, ssm_w_out):
    for i in range(DEPTH):
        last = i == DEPTH - 1
        is_conv = (i % N_MIXERS) == 0
        j = i // N_MIXERS
        need_ctx_out = not last
        need_ctx_in = need_ctx_out or not is_conv
        sh, sc, gt = _ada(c, ada_w[i], ada_b[i])
        hx = x * (1.0 + sc) + sh
        if need_ctx_in:
            sh_c, sc_c, gt_c = _ada(c_ctx, ada_w[i], ada_b[i])
            hc = ctx * (1.0 + sc_c) + sh_c
        if is_conv:
            fx = _conv_mixer(hx, conv_w_in[j], conv_w[j], conv_w_out[j], True)
            fc = _conv_mixer(hc, conv_w_in[j], conv_w[j], conv_w_out[j], False) if need_ctx_out else None
        else:
            fx, fc = _s5_mixer(hx, hc, ssm_w_in[j], ssm_lam_re[j], ssm_lam_im[j], ssm_log_step[j],
                               ssm_b_re[j], ssm_b_im[j], ssm_c_re[j], ssm_c_im[j], ssm_d[j],
                               ssm_w_glu[j], ssm_b_glu[j], ssm_w_out[j], need_ctx_out)
        x = _layernorm(DN_ALPHA * x + gt * fx, ln_g[i], ln_b[i])
        if need_ctx_out:
            ctx = _layernorm(DN_ALPHA * ctx + gt_c * fc, ln_g[i], ln_b[i])
    return x


import jax as _jax
import jax.numpy as _jnp

TWIN_FORMAT = 'train_step'
FWD_PARAMS = ['x', 'c', 'ctx', 'c_ctx', 'ada_w', 'ada_b', 'ln_g', 'ln_b', 'conv_w_in', 'conv_w', 'conv_w_out', 'ssm_w_in', 'ssm_lam_re', 'ssm_lam_im', 'ssm_log_step', 'ssm_b_re', 'ssm_b_im', 'ssm_c_re', 'ssm_c_im', 'ssm_d', 'ssm_w_glu', 'ssm_b_glu', 'ssm_w_out']
TWIN_WEIGHTS = ['c_ctx', 'ada_w', 'ada_b', 'ln_g', 'ln_b', 'conv_w_in', 'conv_w', 'conv_w_out', 'ssm_w_in', 'ssm_lam_re', 'ssm_lam_im', 'ssm_log_step', 'ssm_b_re', 'ssm_b_im', 'ssm_c_re', 'ssm_c_im', 'ssm_d', 'ssm_w_glu', 'ssm_b_glu', 'ssm_w_out']
TWIN_DIFF_INPUT = 'x'
TWIN_INPUTS = ['x', 'c', 'ctx', 'c_ctx', 'ada_w', 'ada_b', 'ln_g', 'ln_b', 'conv_w_in', 'conv_w', 'conv_w_out', 'ssm_w_in', 'ssm_lam_re', 'ssm_lam_im', 'ssm_log_step', 'ssm_b_re', 'ssm_b_im', 'ssm_c_re', 'ssm_c_im', 'ssm_d', 'ssm_w_glu', 'ssm_b_glu', 'ssm_w_out', 'loss_target', 'm_c_ctx', 'm_ada_w', 'm_ada_b', 'm_ln_g', 'm_ln_b', 'm_conv_w_in', 'm_conv_w', 'm_conv_w_out', 'm_ssm_w_in', 'm_ssm_lam_re', 'm_ssm_lam_im', 'm_ssm_log_step', 'm_ssm_b_re', 'm_ssm_b_im', 'm_ssm_c_re', 'm_ssm_c_im', 'm_ssm_d', 'm_ssm_w_glu', 'm_ssm_b_glu', 'm_ssm_w_out', 'v_c_ctx', 'v_ada_w', 'v_ada_b', 'v_ln_g', 'v_ln_b', 'v_conv_w_in', 'v_conv_w', 'v_conv_w_out', 'v_ssm_w_in', 'v_ssm_lam_re', 'v_ssm_lam_im', 'v_ssm_log_step', 'v_ssm_b_re', 'v_ssm_b_im', 'v_ssm_c_re', 'v_ssm_c_im', 'v_ssm_d', 'v_ssm_w_glu', 'v_ssm_b_glu', 'v_ssm_w_out']
TWIN_OUTPUTS = ['loss', 'grad_x', 'grad_c_ctx', 'grad_ada_w', 'grad_ada_b', 'grad_ln_g', 'grad_ln_b', 'grad_conv_w_in', 'grad_conv_w', 'grad_conv_w_out', 'grad_ssm_w_in', 'grad_ssm_lam_re', 'grad_ssm_lam_im', 'grad_ssm_log_step', 'grad_ssm_b_re', 'grad_ssm_b_im', 'grad_ssm_c_re', 'grad_ssm_c_im', 'grad_ssm_d', 'grad_ssm_w_glu', 'grad_ssm_b_glu', 'grad_ssm_w_out', 'delta_c_ctx', 'delta_ada_w', 'delta_ada_b', 'delta_ln_g', 'delta_ln_b', 'delta_conv_w_in', 'delta_conv_w', 'delta_conv_w_out', 'delta_ssm_w_in', 'delta_ssm_lam_re', 'delta_ssm_lam_im', 'delta_ssm_log_step', 'delta_ssm_b_re', 'delta_ssm_b_im', 'delta_ssm_c_re', 'delta_ssm_c_im', 'delta_ssm_d', 'delta_ssm_w_glu', 'delta_ssm_b_glu', 'delta_ssm_w_out', 'new_m_c_ctx', 'new_m_ada_w', 'new_m_ada_b', 'new_m_ln_g', 'new_m_ln_b', 'new_m_conv_w_in', 'new_m_conv_w', 'new_m_conv_w_out', 'new_m_ssm_w_in', 'new_m_ssm_lam_re', 'new_m_ssm_lam_im', 'new_m_ssm_log_step', 'new_m_ssm_b_re', 'new_m_ssm_b_im', 'new_m_ssm_c_re', 'new_m_ssm_c_im', 'new_m_ssm_d', 'new_m_ssm_w_glu', 'new_m_ssm_b_glu', 'new_m_ssm_w_out', 'new_v_c_ctx', 'new_v_ada_w', 'new_v_ada_b', 'new_v_ln_g', 'new_v_ln_b', 'new_v_conv_w_in', 'new_v_conv_w', 'new_v_conv_w_out', 'new_v_ssm_w_in', 'new_v_ssm_lam_re', 'new_v_ssm_lam_im', 'new_v_ssm_log_step', 'new_v_ssm_b_re', 'new_v_ssm_b_im', 'new_v_ssm_c_re', 'new_v_ssm_c_im', 'new_v_ssm_d', 'new_v_ssm_w_glu', 'new_v_ssm_b_glu', 'new_v_ssm_w_out']
TWIN_LEAF_KINDS = {'loss': 'loss', 'grad_x': 'grad_x', 'grad_c_ctx': 'grad_w', 'grad_ada_w': 'grad_w', 'grad_ada_b': 'grad_w', 'grad_ln_g': 'grad_w', 'grad_ln_b': 'grad_w', 'grad_conv_w_in': 'grad_w', 'grad_conv_w': 'grad_w', 'grad_conv_w_out': 'grad_w', 'grad_ssm_w_in': 'grad_w', 'grad_ssm_lam_re': 'grad_w', 'grad_ssm_lam_im': 'grad_w', 'grad_ssm_log_step': 'grad_w', 'grad_ssm_b_re': 'grad_w', 'grad_ssm_b_im': 'grad_w', 'grad_ssm_c_re': 'grad_w', 'grad_ssm_c_im': 'grad_w', 'grad_ssm_d': 'grad_w', 'grad_ssm_w_glu': 'grad_w', 'grad_ssm_b_glu': 'grad_w', 'grad_ssm_w_out': 'grad_w', 'delta_c_ctx': 'delta_w', 'delta_ada_w': 'delta_w', 'delta_ada_b': 'delta_w', 'delta_ln_g': 'delta_w', 'delta_ln_b': 'delta_w', 'delta_conv_w_in': 'delta_w', 'delta_conv_w': 'delta_w', 'delta_conv_w_out': 'delta_w', 'delta_ssm_w_in': 'delta_w', 'delta_ssm_lam_re': 'delta_w', 'delta_ssm_lam_im': 'delta_w', 'delta_ssm_log_step': 'delta_w', 'delta_ssm_b_re': 'delta_w', 'delta_ssm_b_im': 'delta_w', 'delta_ssm_c_re': 'delta_w', 'delta_ssm_c_im': 'delta_w', 'delta_ssm_d': 'delta_w', 'delta_ssm_w_glu': 'delta_w', 'delta_ssm_b_glu': 'delta_w', 'delta_ssm_w_out': 'delta_w', 'new_m_c_ctx': 'new_m', 'new_m_ada_w': 'new_m', 'new_m_ada_b': 'new_m', 'new_m_ln_g': 'new_m', 'new_m_ln_b': 'new_m', 'new_m_conv_w_in': 'new_m', 'new_m_conv_w': 'new_m', 'new_m_conv_w_out': 'new_m', 'new_m_ssm_w_in': 'new_m', 'new_m_ssm_lam_re': 'new_m', 'new_m_ssm_lam_im': 'new_m', 'new_m_ssm_log_step': 'new_m', 'new_m_ssm_b_re': 'new_m', 'new_m_ssm_b_im': 'new_m', 'new_m_ssm_c_re': 'new_m', 'new_m_ssm_c_im': 'new_m', 'new_m_ssm_d': 'new_m', 'new_m_ssm_w_glu': 'new_m', 'new_m_ssm_b_glu': 'new_m', 'new_m_ssm_w_out': 'new_m', 'new_v_c_ctx': 'new_v', 'new_v_ada_w': 'new_v', 'new_v_ada_b': 'new_v', 'new_v_ln_g': 'new_v', 'new_v_ln_b': 'new_v', 'new_v_conv_w_in': 'new_v', 'new_v_conv_w': 'new_v', 'new_v_conv_w_out': 'new_v', 'new_v_ssm_w_in': 'new_v', 'new_v_ssm_lam_re': 'new_v', 'new_v_ssm_lam_im': 'new_v', 'new_v_ssm_log_step': 'new_v', 'new_v_ssm_b_re': 'new_v', 'new_v_ssm_b_im': 'new_v', 'new_v_ssm_c_re': 'new_v', 'new_v_ssm_c_im': 'new_v', 'new_v_ssm_d': 'new_v', 'new_v_ssm_w_glu': 'new_v', 'new_v_ssm_b_glu': 'new_v', 'new_v_ssm_w_out': 'new_v'}


def _forward(args):
    return _fwd_reference(*[args[k] for k in FWD_PARAMS])


def _output_shape():
    def fwd():
        inp = _fwd_setup_inputs(0)
        return _fwd_reference(*[inp[k] for k in FWD_PARAMS])
    out = _jax.eval_shape(fwd)
    return out.shape, out.dtype

N_MICROBATCH = 1
ADAM_LR = 0.001
ADAM_B1 = 0.9
ADAM_B2 = 0.999
ADAM_EPS = 1e-08
ADAM_WD = 0.01
ADAM_STEP = 10
PER_EXAMPLE_BATCH_AXIS = {'x': 0, 'c': 0, 'ctx': 0, 'loss_target': 0}
SHARED_INPUTS = []
_WEIGHT_DTYPES = {'c_ctx': _jnp.float32, 'ada_w': _jnp.float32, 'ada_b': _jnp.float32, 'ln_g': _jnp.float32, 'ln_b': _jnp.float32, 'conv_w_in': _jnp.float32, 'conv_w': _jnp.float32, 'conv_w_out': _jnp.float32, 'ssm_w_in': _jnp.float32, 'ssm_lam_re': _jnp.float32, 'ssm_lam_im': _jnp.float32, 'ssm_log_step': _jnp.float32, 'ssm_b_re': _jnp.float32, 'ssm_b_im': _jnp.float32, 'ssm_c_re': _jnp.float32, 'ssm_c_im': _jnp.float32, 'ssm_d': _jnp.float32, 'ssm_w_glu': _jnp.float32, 'ssm_b_glu': _jnp.float32, 'ssm_w_out': _jnp.float32}
MOMENT_SCALE = {'c_ctx': 2.272789e-03, 'ada_w': 1.063205e-01, 'ada_b': 1.971440e-01, 'ln_g': 9.087309e+01, 'ln_b': 2.158541e+00, 'conv_w_in': 1.011239e-01, 'conv_w': 1.045331e-01, 'conv_w_out': 2.864180e-01, 'ssm_w_in': 2.222736e-02, 'ssm_lam_re': 2.600570e-03, 'ssm_lam_im': 3.199035e-03, 'ssm_log_step': 1.288140e+00, 'ssm_b_re': 1.778036e-03, 'ssm_b_im': 1.619161e-03, 'ssm_c_re': 2.272913e-03, 'ssm_c_im': 2.525693e-03, 'ssm_d': 2.551597e-02, 'ssm_w_glu': 7.686185e-03, 'ssm_b_glu': 8.692718e-03, 'ssm_w_out': 6.235746e-02}


def _to_microbatches(a, axis):
    t = _jnp.moveaxis(a, axis, 0)
    t = t.reshape((N_MICROBATCH, t.shape[0] // N_MICROBATCH) + t.shape[1:])
    return _jnp.moveaxis(t, 1, axis + 1)


def setup_inputs(seed: int = 0) -> dict:
    inp = _fwd_setup_inputs(seed)
    key = _jax.random.fold_in(_jax.random.key(seed), 7919)
    shape, _ = _output_shape()
    out = dict(inp)
    out["loss_target"] = _jax.random.normal(_jax.random.fold_in(key, 0), shape, _jnp.float32)
    for i, name in enumerate(TWIN_WEIGHTS):
        w = inp[name].astype(_jnp.float32)
        if MOMENT_SCALE is None:
            s = _jnp.sqrt(_jnp.mean(_jnp.square(w)) + 1e-30)
        else:
            s = MOMENT_SCALE[name]
        km, kv = _jax.random.split(_jax.random.fold_in(key, i + 1))
        out[name] = w
        out["m_" + name] = s * _jax.random.normal(km, w.shape, _jnp.float32)
        out["v_" + name] = (s * s) * _jax.random.uniform(kv, w.shape, _jnp.float32, 0.5, 1.5)
    if N_MICROBATCH > 1:
        for name, axis in PER_EXAMPLE_BATCH_AXIS.items():
            out[name] = _to_microbatches(out[name], axis)
    return {'x': out['x'], 'c': out['c'], 'ctx': out['ctx'], 'c_ctx': out['c_ctx'], 'ada_w': out['ada_w'], 'ada_b': out['ada_b'], 'ln_g': out['ln_g'], 'ln_b': out['ln_b'], 'conv_w_in': out['conv_w_in'], 'conv_w': out['conv_w'], 'conv_w_out': out['conv_w_out'], 'ssm_w_in': out['ssm_w_in'], 'ssm_lam_re': out['ssm_lam_re'], 'ssm_lam_im': out['ssm_lam_im'], 'ssm_log_step': out['ssm_log_step'], 'ssm_b_re': out['ssm_b_re'], 'ssm_b_im': out['ssm_b_im'], 'ssm_c_re': out['ssm_c_re'], 'ssm_c_im': out['ssm_c_im'], 'ssm_d': out['ssm_d'], 'ssm_w_glu': out['ssm_w_glu'], 'ssm_b_glu': out['ssm_b_glu'], 'ssm_w_out': out['ssm_w_out'], 'loss_target': out['loss_target'], 'm_c_ctx': out['m_c_ctx'], 'm_ada_w': out['m_ada_w'], 'm_ada_b': out['m_ada_b'], 'm_ln_g': out['m_ln_g'], 'm_ln_b': out['m_ln_b'], 'm_conv_w_in': out['m_conv_w_in'], 'm_conv_w': out['m_conv_w'], 'm_conv_w_out': out['m_conv_w_out'], 'm_ssm_w_in': out['m_ssm_w_in'], 'm_ssm_lam_re': out['m_ssm_lam_re'], 'm_ssm_lam_im': out['m_ssm_lam_im'], 'm_ssm_log_step': out['m_ssm_log_step'], 'm_ssm_b_re': out['m_ssm_b_re'], 'm_ssm_b_im': out['m_ssm_b_im'], 'm_ssm_c_re': out['m_ssm_c_re'], 'm_ssm_c_im': out['m_ssm_c_im'], 'm_ssm_d': out['m_ssm_d'], 'm_ssm_w_glu': out['m_ssm_w_glu'], 'm_ssm_b_glu': out['m_ssm_b_glu'], 'm_ssm_w_out': out['m_ssm_w_out'], 'v_c_ctx': out['v_c_ctx'], 'v_ada_w': out['v_ada_w'], 'v_ada_b': out['v_ada_b'], 'v_ln_g': out['v_ln_g'], 'v_ln_b': out['v_ln_b'], 'v_conv_w_in': out['v_conv_w_in'], 'v_conv_w': out['v_conv_w'], 'v_conv_w_out': out['v_conv_w_out'], 'v_ssm_w_in': out['v_ssm_w_in'], 'v_ssm_lam_re': out['v_ssm_lam_re'], 'v_ssm_lam_im': out['v_ssm_lam_im'], 'v_ssm_log_step': out['v_ssm_log_step'], 'v_ssm_b_re': out['v_ssm_b_re'], 'v_ssm_b_im': out['v_ssm_b_im'], 'v_ssm_c_re': out['v_ssm_c_re'], 'v_ssm_c_im': out['v_ssm_c_im'], 'v_ssm_d': out['v_ssm_d'], 'v_ssm_w_glu': out['v_ssm_w_glu'], 'v_ssm_b_glu': out['v_ssm_b_glu'], 'v_ssm_w_out': out['v_ssm_w_out']}


def _loss(weights, diff, rest, loss_target):
    with _jax.named_scope("forward"):
        args = {**rest, TWIN_DIFF_INPUT: diff, **{k: w.astype(_WEIGHT_DTYPES[k]) for k, w in weights.items()}}
        y = _forward(args)
    with _jax.named_scope("loss_head"):
        err = _jnp.square(y.astype(_jnp.float32) - loss_target)
        return 0.5 * _jnp.sum(_jnp.mean(err, axis=-1)) if err.ndim else 0.5 * err


def _adamw(w, g, m, v):
    m = ADAM_B1 * m + (1.0 - ADAM_B1) * g
    v = ADAM_B2 * v + (1.0 - ADAM_B2) * _jnp.square(g)
    m_hat = m / (1.0 - ADAM_B1 ** ADAM_STEP)
    v_hat = v / (1.0 - ADAM_B2 ** ADAM_STEP)
    delta = -ADAM_LR * (m_hat / (_jnp.sqrt(v_hat) + ADAM_EPS) + ADAM_WD * w)
    return delta, m, v


def reference(x, c, ctx, c_ctx, ada_w, ada_b, ln_g, ln_b, conv_w_in, conv_w, conv_w_out, ssm_w_in, ssm_lam_re, ssm_lam_im, ssm_log_step, ssm_b_re, ssm_b_im, ssm_c_re, ssm_c_im, ssm_d, ssm_w_glu, ssm_b_glu, ssm_w_out, loss_target, m_c_ctx, m_ada_w, m_ada_b, m_ln_g, m_ln_b, m_conv_w_in, m_conv_w, m_conv_w_out, m_ssm_w_in, m_ssm_lam_re, m_ssm_lam_im, m_ssm_log_step, m_ssm_b_re, m_ssm_b_im, m_ssm_c_re, m_ssm_c_im, m_ssm_d, m_ssm_w_glu, m_ssm_b_glu, m_ssm_w_out, v_c_ctx, v_ada_w, v_ada_b, v_ln_g, v_ln_b, v_conv_w_in, v_conv_w, v_conv_w_out, v_ssm_w_in, v_ssm_lam_re, v_ssm_lam_im, v_ssm_log_step, v_ssm_b_re, v_ssm_b_im, v_ssm_c_re, v_ssm_c_im, v_ssm_d, v_ssm_w_glu, v_ssm_b_glu, v_ssm_w_out):
    given = dict(x=x, c=c, ctx=ctx, c_ctx=c_ctx, ada_w=ada_w, ada_b=ada_b, ln_g=ln_g, ln_b=ln_b, conv_w_in=conv_w_in, conv_w=conv_w, conv_w_out=conv_w_out, ssm_w_in=ssm_w_in, ssm_lam_re=ssm_lam_re, ssm_lam_im=ssm_lam_im, ssm_log_step=ssm_log_step, ssm_b_re=ssm_b_re, ssm_b_im=ssm_b_im, ssm_c_re=ssm_c_re, ssm_c_im=ssm_c_im, ssm_d=ssm_d, ssm_w_glu=ssm_w_glu, ssm_b_glu=ssm_b_glu, ssm_w_out=ssm_w_out, loss_target=loss_target, m_c_ctx=m_c_ctx, m_ada_w=m_ada_w, m_ada_b=m_ada_b, m_ln_g=m_ln_g, m_ln_b=m_ln_b, m_conv_w_in=m_conv_w_in, m_conv_w=m_conv_w, m_conv_w_out=m_conv_w_out, m_ssm_w_in=m_ssm_w_in, m_ssm_lam_re=m_ssm_lam_re, m_ssm_lam_im=m_ssm_lam_im, m_ssm_log_step=m_ssm_log_step, m_ssm_b_re=m_ssm_b_re, m_ssm_b_im=m_ssm_b_im, m_ssm_c_re=m_ssm_c_re, m_ssm_c_im=m_ssm_c_im, m_ssm_d=m_ssm_d, m_ssm_w_glu=m_ssm_w_glu, m_ssm_b_glu=m_ssm_b_glu, m_ssm_w_out=m_ssm_w_out, v_c_ctx=v_c_ctx, v_ada_w=v_ada_w, v_ada_b=v_ada_b, v_ln_g=v_ln_g, v_ln_b=v_ln_b, v_conv_w_in=v_conv_w_in, v_conv_w=v_conv_w, v_conv_w_out=v_conv_w_out, v_ssm_w_in=v_ssm_w_in, v_ssm_lam_re=v_ssm_lam_re, v_ssm_lam_im=v_ssm_lam_im, v_ssm_log_step=v_ssm_log_step, v_ssm_b_re=v_ssm_b_re, v_ssm_b_im=v_ssm_b_im, v_ssm_c_re=v_ssm_c_re, v_ssm_c_im=v_ssm_c_im, v_ssm_d=v_ssm_d, v_ssm_w_glu=v_ssm_w_glu, v_ssm_b_glu=v_ssm_b_glu, v_ssm_w_out=v_ssm_w_out)
    weights = {n: given[n] for n in TWIN_WEIGHTS}
    shared = {n: given[n] for n in SHARED_INPUTS}
    per_example = {n: given[n] for n in ['x', 'c', 'ctx']}
    grad_fn = _jax.value_and_grad(_loss, argnums=(0, 1))

    def one_microbatch(ex, loss_target):
        ex = dict(ex)
        diff = ex.pop(TWIN_DIFF_INPUT)
        return grad_fn(weights, diff, {**shared, **ex}, loss_target)

    if N_MICROBATCH == 1:
        loss, (grad_w, grad_x) = one_microbatch(per_example, given["loss_target"])
    else:
        def body(carry, xs):
            loss_sum, grad_sum = carry
            l_k, (gw_k, gx_k) = one_microbatch(xs[0], xs[1])
            with _jax.named_scope("update"):
                return (loss_sum + l_k, _jax.tree.map(_jnp.add, grad_sum, gw_k)), gx_k

        init = (_jnp.zeros((), _jnp.float32), _jax.tree.map(_jnp.zeros_like, weights))
        (loss, grad_w), grad_x = _jax.lax.scan(body, init, (per_example, given["loss_target"]))
    with _jax.named_scope("update"):
        delta_w, new_m, new_v = {}, {}, {}
        for n in TWIN_WEIGHTS:
            delta_w[n], new_m[n], new_v[n] = _adamw(weights[n], grad_w[n], given["m_" + n], given["v_" + n])
    return (loss, grad_x, *[grad_w[n] for n in TWIN_WEIGHTS], *[delta_w[n] for n in TWIN_WEIGHTS],
            *[new_m[n] for n in TWIN_WEIGHTS], *[new_v[n] for n in TWIN_WEIGHTS])
```

```python
import functools
import math

import jax
import jax.numpy as jnp
from jax import lax
from jax.experimental import pallas as pl
from jax.experimental.pallas import tpu as pltpu

F32 = jnp.float32
BF16 = jnp.bfloat16
LANES = 128
SUBLANES = 8
VMEM_LIMIT = 56 * 1024 * 1024
MESH_AXES = ("x", "y", "c")
HIGHEST = lax.Precision.HIGHEST

GRID_W = 64
LN_EPS = 1e-5
DEPTH = 2
DN_ALPHA = (2 * DEPTH) ** 0.25
ADAM_LR, ADAM_B1, ADAM_B2, ADAM_EPS, ADAM_WD, ADAM_STEP = 0.001, 0.9, 0.999, 1e-08, 0.01, 10
GELU_K = math.sqrt(2.0 / math.pi)
GELU_C = 0.044715


def _params(n_grid_axes):
    return pltpu.CompilerParams(dimension_semantics=("arbitrary",) * n_grid_axes, vmem_limit_bytes=VMEM_LIMIT)


def _dot(a, b):
    return jnp.dot(a, b, preferred_element_type=F32)


def _dot_nt(a, b):
    return lax.dot_general(a, b, (((1,), (1,)), ((), ())), preferred_element_type=F32)


def _dot_tn(a, b):
    return lax.dot_general(a, b, (((0,), (0,)), ((), ())), preferred_element_type=F32)


def _sigmoid(x):
    return 1.0 / (1.0 + jnp.exp(-x))


def _colsum(x):
    return jnp.sum(x, axis=0, keepdims=True)


def _rowmean(x):
    return jnp.mean(x, axis=-1, keepdims=True)


def _acc(ref, first, value):
    @pl.when(first)
    def _():
        ref[...] = value

    @pl.when(jnp.logical_not(first))
    def _():
        ref[...] += value


def _exchange(src, axes, all_to_all, name):
    n_peers = 2 ** len(axes)
    block = tuple(src.shape[1:] if all_to_all else src.shape)

    def body(src_ref, out_ref, send_sems, recv_sems, own_sem):
        pos = {a: lax.axis_index(a) for a in MESH_AXES}

        def index(p):
            return sum(p[a] * (2 ** (len(axes) - 1 - i)) for i, a in enumerate(axes))

        me = index(pos)
        own = pltpu.make_async_copy(src_ref.at[me] if all_to_all else src_ref, out_ref.at[me], own_sem)
        own.start()
        copies = []
        for k in range(1, n_peers):
            peer = dict(pos)
            for i, a in enumerate(axes):
                if (k >> (len(axes) - 1 - i)) & 1:
                    peer[a] = 1 - pos[a]
            cp = pltpu.make_async_remote_copy(
                src_ref=src_ref.at[index(peer)] if all_to_all else src_ref,
                dst_ref=out_ref.at[me],
                send_sem=send_sems.at[k - 1],
                recv_sem=recv_sems.at[k - 1],
                device_id=tuple(peer[a] for a in MESH_AXES),
                device_id_type=pl.DeviceIdType.MESH,
            )
            cp.start()
            copies.append(cp)
        for cp in copies:
            cp.wait()
        own.wait()

    return pl.pallas_call(
        body,
        name=name,
        out_shape=jax.ShapeDtypeStruct((n_peers,) + block, src.dtype),
        in_specs=[pl.BlockSpec(memory_space=pltpu.HBM)],
        out_specs=pl.BlockSpec(memory_space=pltpu.HBM),
        scratch_shapes=[
            pltpu.SemaphoreType.DMA((n_peers - 1,)),
            pltpu.SemaphoreType.DMA((n_peers - 1,)),
            pltpu.SemaphoreType.DMA,
        ],
    )(src)


def _sum_parts(parts, name):
    n_parts, rows, cols = parts.shape
    tr = rows
    while n_parts * tr * cols * 4 > 8 * 1024 * 1024 and tr % 16 == 0:
        tr //= 2

    def body(p_ref, o_ref):
        total = p_ref[0]
        for k in range(1, n_parts):
            total = total + p_ref[k]
        o_ref[...] = total

    return pl.pallas_call(
        body,
        name=name,
        grid=(rows // tr,),
        out_shape=jax.ShapeDtypeStruct((rows, cols), F32),
        in_specs=[pl.BlockSpec((n_parts, tr, cols), lambda i: (0, i, 0))],
        out_specs=pl.BlockSpec((tr, cols), lambda i: (i, 0)),
        compiler_params=_params(1),
    )(parts)


def _adamw(w, g, m, v, name):
    rows, cols = w.shape
    tr = rows
    while tr * cols * 4 > 2 * 1024 * 1024 and tr % 16 == 0:
        tr //= 2

    def body(w_ref, g_ref, m_ref, v_ref, d_ref, nm_ref, nv_ref):
        gv = g_ref[...]
        nm = ADAM_B1 * m_ref[...] + (1.0 - ADAM_B1) * gv
        nv = ADAM_B2 * v_ref[...] + (1.0 - ADAM_B2) * (gv * gv)
        m_hat = nm / (1.0 - ADAM_B1 ** ADAM_STEP)
        v_hat = nv / (1.0 - ADAM_B2 ** ADAM_STEP)
        d_ref[...] = -ADAM_LR * (m_hat / (jnp.sqrt(v_hat) + ADAM_EPS) + ADAM_WD * w_ref[...])
        nm_ref[...] = nm
        nv_ref[...] = nv

    spec = pl.BlockSpec((tr, cols), lambda i: (i, 0))
    shape = jax.ShapeDtypeStruct((rows, cols), F32)
    return pl.pallas_call(
        body, name=name, grid=(rows // tr,), out_shape=(shape, shape, shape),
        in_specs=[spec] * 4, out_specs=(spec, spec, spec), compiler_params=_params(1),
    )(w, g, m, v)


def _ada_rows(c_ref, cc_ref):
    rows = jnp.concatenate([c_ref[...], jnp.broadcast_to(cc_ref[...], c_ref.shape)], axis=0)
    return rows


def _ada_fwd(c_all, c_ctx, w_sh, b_sh):
    n_layers, _, ws = w_sh.shape

    def body(c_ref, cc_ref, w_ref, b_ref, o_ref):
        rows = _ada_rows(c_ref, cc_ref)
        s = rows * _sigmoid(rows)
        for i in range(n_layers):
            o_ref[i] = jnp.dot(s, w_ref[i], precision=HIGHEST, preferred_element_type=F32) + b_ref[i]

    return pl.pallas_call(
        body, name="ada_fwd", out_shape=jax.ShapeDtypeStruct((n_layers, 16, ws), F32),
        compiler_params=pltpu.CompilerParams(vmem_limit_bytes=VMEM_LIMIT),
    )(c_all, c_ctx, w_sh, b_sh)


def _ada_bwd(c_all, c_ctx, w_sh, dm_full, dm_sh):
    n_layers, d_model, ws = w_sh.shape
    n_dev = dm_full.shape[0]
    cols = dm_full.shape[-1]

    def body(c_ref, cc_ref, w_ref, dmf_ref, dms_ref, gw_ref, gb_ref, ds_ref):
        rows = _ada_rows(c_ref, cc_ref)
        s = rows * _sigmoid(rows)
        ds = jnp.zeros((8, d_model), F32)
        for i in range(n_layers):
            ctx_s = dms_ref[0, i, 1:2, :]
            ctx_f = dmf_ref[0, i, 1:2, :]
            ex_f = dmf_ref[0, i, 0:1, :]
            for k in range(1, n_dev):
                ctx_s = ctx_s + dms_ref[k, i, 1:2, :]
                ctx_f = ctx_f + dmf_ref[k, i, 1:2, :]
                ex_f = ex_f + dmf_ref[k, i, 0:1, :]
            gb_ref[i] = ex_f + ctx_f
            r = jnp.concatenate([dms_ref[k, i, 0:1, :] for k in range(n_dev)] + [ctx_s, jnp.zeros((7, ws), F32)], axis=0)
            gw_ref[i] = lax.dot_general(s, r, (((0,), (0,)), ((), ())), precision=HIGHEST, preferred_element_type=F32)
            ds = ds + lax.dot_general(jnp.broadcast_to(ctx_s, (8, ws)), w_ref[i], (((1,), (1,)), ((), ())),
                                      precision=HIGHEST, preferred_element_type=F32)
        ds_ref[...] = ds

    return pl.pallas_call(
        body, name="ada_bwd",
        out_shape=(jax.ShapeDtypeStruct((n_layers, d_model, ws), F32), jax.ShapeDtypeStruct((n_layers, 1, cols), F32),
                   jax.ShapeDtypeStruct((8, d_model), F32)),
        compiler_params=pltpu.CompilerParams(vmem_limit_bytes=VMEM_LIMIT),
    )(c_all, c_ctx, w_sh, dm_full, dm_sh)


def _cctx_grad(ds_parts, c_ctx):
    def body(p_ref, c_ref, o_ref):
        tot = p_ref[0, 0:1, :]
        for k in range(1, ds_parts.shape[0]):
            tot = tot + p_ref[k, 0:1, :]
        cv = c_ref[...]
        sg = _sigmoid(cv)
        o_ref[...] = tot * (sg * (1.0 + cv * (1.0 - sg)))

    return pl.pallas_call(body, name="cctx_grad", out_shape=jax.ShapeDtypeStruct(c_ctx.shape, F32))(ds_parts, c_ctx)


def _zoh_math(lam_re, lam_im, log_step, b_re, b_im):
    dt = jnp.exp(log_step)
    mag = jnp.exp(lam_re * dt)
    ar = mag * jnp.cos(lam_im * dt)
    ai = mag * jnp.sin(lam_im * dt)
    qr, qi = ar - 1.0, ai
    den = lam_re * lam_re + lam_im * lam_im
    fr = (qr * lam_re + qi * lam_im) / den
    fi = (qi * lam_re - qr * lam_im) / den
    bbr = fr[None] * b_re - fi[None] * b_im
    bbi = fr[None] * b_im + fi[None] * b_re
    return ar, ai, bbr, bbi


def _zoh_fwd(lam_re, lam_im, log_step, b_re, b_im):
    rg, n = lam_re.shape

    def body(lr_ref, li_ref, ls_ref, br_ref, bi_ref, pr_ref, pi_ref, bbr_ref, bbi_ref):
        ar, ai, bbr, bbi = _zoh_math(lr_ref[...], li_ref[...], ls_ref[...], br_ref[...], bi_ref[...])
        bbr_ref[...] = bbr
        bbi_ref[...] = bbi
        pr, pi_ = ar, ai
        for k in range(8):
            pr_ref[k] = pr
            pi_ref[k] = pi_
            pr, pi_ = pr * ar - pi_ * ai, pr * ai + pi_ * ar

    pw = jax.ShapeDtypeStruct((8, rg, n), F32)
    bb = jax.ShapeDtypeStruct(b_re.shape, F32)
    return pl.pallas_call(body, name="zoh_fwd", out_shape=(pw, pw, bb, bb))(lam_re, lam_im, log_step, b_re, b_im)


def _zoh_bwd(lam_re, lam_im, log_step, b_re, b_im, dar, dai, dbbr, dbbi):
    def body(lr_ref, li_ref, ls_ref, br_ref, bi_ref, dar_ref, dai_ref, dbr_ref, dbi_ref, *outs):
        _, vjp = jax.vjp(_zoh_math, lr_ref[...], li_ref[...], ls_ref[...], br_ref[...], bi_ref[...])
        grads = vjp((dar_ref[...], dai_ref[...], dbr_ref[...], dbi_ref[...]))
        for o_ref, gval in zip(outs, grads):
            o_ref[...] = gval

    shapes = tuple(jax.ShapeDtypeStruct(a.shape, F32) for a in (lam_re, lam_im, log_step, b_re, b_im))
    return pl.pallas_call(body, name="zoh_bwd", out_shape=shapes)(lam_re, lam_im, log_step, b_re, b_im, dar, dai, dbbr, dbbi)


def _inproj(xin, sc, sh, w, *, lnaff=None, tb, gb_rows=None, name):
    n_tok, d_model = xin.shape
    n_chunks, _, cw = w.shape
    tb = min(tb, n_tok)
    nq = cw // LANES
    has_ln = lnaff is not None

    def body(*refs):
        if has_ln:
            x_ref, g_ref, b_ref, sc_ref, sh_ref, w_ref, p_ref, h_ref = refs
        else:
            x_ref, sc_ref, sh_ref, w_ref, p_ref, h_ref = refs

        @pl.when(pl.program_id(1) == 0)
        def _():
            xv = x_ref[...]
            if has_ln:
                xv = xv * g_ref[...] + b_ref[...]
            h_ref[...] = (xv * (1.0 + sc_ref[...]) + sh_ref[...]).astype(BF16)

        acc = _dot(h_ref[...], w_ref[0])
        if gb_rows is None:
            p_ref[0] = acc.astype(BF16)
        else:
            for q in range(nq):
                p_ref[q] = acc[:, q * LANES:(q + 1) * LANES].astype(BF16)

    vec = pl.BlockSpec((1, d_model), lambda i, j: (0, 0))
    in_specs = [pl.BlockSpec((tb, d_model), lambda i, j: (i, 0))] + ([vec, vec] if has_ln else []) + [
        vec, vec, pl.BlockSpec((1, d_model, cw), lambda i, j: (j, 0, 0))]
    if gb_rows is None:
        p_shape = (n_chunks, n_tok, cw)
        p_spec = pl.BlockSpec((1, tb, cw), lambda i, j: (j, i, 0))
    else:
        total, off = gb_rows
        assert off % tb == 0
        ob = off // tb
        p_shape = (n_chunks * nq, total, LANES)
        p_spec = pl.BlockSpec((nq, tb, LANES), lambda i, j: (j, i + ob, 0))
    args = (xin,) + (tuple(lnaff) if has_ln else ()) + (sc, sh, w)
    return pl.pallas_call(
        body, name=name, grid=(n_tok // tb, n_chunks),
        out_shape=(jax.ShapeDtypeStruct(p_shape, BF16), jax.ShapeDtypeStruct((n_tok, d_model), BF16)),
        in_specs=in_specs, out_specs=(p_spec, pl.BlockSpec((tb, d_model), lambda i, j: (i, 0))),
        compiler_params=_params(2),
    )(*args)


def _shifted(u, rows, width, tb):
    col = rows % width
    um = jnp.where(col == 0, 0.0, pltpu.roll(u, 1, 0))
    up = jnp.where(col == width - 1, 0.0, pltpu.roll(u, tb - 1, 0))
    return um, up


def _slab_width(e):
    return min(512, e // 2)


def _convgate(p, x, gt, conv_w, w_out, ln_g, ln_b, *, grid_mode, tb, name):
    _, n_tok, e = p.shape
    d_model = x.shape[1]
    eh = e // 2
    if not grid_mode:
        tb = n_tok
    tb = min(tb, n_tok)
    nb = n_tok // tb
    hb = tb // GRID_W
    cs = _slab_width(e)

    def body(*refs):
        if grid_mode:
            (bg_ref, cg_ref, v_ref, z_ref, cgp_ref, vp_ref, cgn_ref, vn_ref, x_ref, gt_ref, cw_ref, wo_ref, lg_ref,
             lb_ref, xh_ref, rs_ref, g_ref, yc_ref, f_ref) = refs
        else:
            (bg_ref, cg_ref, v_ref, z_ref, x_ref, gt_ref, cw_ref, wo_ref, lg_ref, lb_ref, xh_ref, rs_ref, g_ref,
             yc_ref, f_ref) = refs
        i = pl.program_id(0)
        rows = lax.broadcasted_iota(jnp.int32, (tb, 1), 0)
        for c0 in range(0, e, cs):
            sl = slice(c0, c0 + cs)
            u = cg_ref[0, :, sl].astype(F32) * v_ref[0, :, sl].astype(F32)
            w = cw_ref[:, sl]
            if grid_mode and c0 >= eh:
                hs = slice(c0 - eh, c0 - eh + cs)
                uprev = cgp_ref[0, :, hs].astype(F32) * vp_ref[0, :, hs].astype(F32)
                unext = cgn_ref[0, :, hs].astype(F32) * vn_ref[0, :, hs].astype(F32)
                uprev = jnp.where(i > 0, uprev, 0.0)
                unext = jnp.where(i < nb - 1, unext, 0.0)
                if tb > GRID_W:
                    um = jnp.concatenate([uprev, u[:tb - GRID_W]], axis=0)
                    up = jnp.concatenate([u[GRID_W:], unext], axis=0)
                else:
                    um, up = uprev, unext
            else:
                um, up = _shifted(u, rows, GRID_W if grid_mode else tb, tb)
            yc = um * w[0:1] + u * w[1:2] + up * w[2:3]
            zf = z_ref[0, :, sl].astype(F32)
            gval = bg_ref[0, :, sl].astype(F32) * yc * (zf * _sigmoid(zf))
            yc_ref[:, sl] = yc.astype(BF16)
            g_ref[:, sl] = gval.astype(BF16)
        f = _dot(g_ref[...], wo_ref[...])
        f_ref[...] = f.astype(BF16)
        r = DN_ALPHA * x_ref[...] + gt_ref[...] * f
        rc = r - _rowmean(r)
        rstd = lax.rsqrt(_rowmean(rc * rc) + LN_EPS)
        xh_ref[...] = rc * rstd
        rs_ref[...] = rstd

    def chunk(k):
        return pl.BlockSpec((1, tb, e), lambda i: (k, i, 0))

    n_hrows = n_tok // GRID_W

    def halo_prev(k):
        return pl.BlockSpec((1, GRID_W, eh), lambda i: (k, jnp.maximum(i * hb - 1, 0), 1))

    def halo_next(k):
        return pl.BlockSpec((1, GRID_W, eh), lambda i: (k, jnp.minimum((i + 1) * hb, n_hrows - 1), 1))

    vec = pl.BlockSpec((1, d_model), lambda i: (0, 0))
    tok = pl.BlockSpec((tb, d_model), lambda i: (i, 0))
    wide = pl.BlockSpec((tb, e), lambda i: (i, 0))
    in_specs = [chunk(0), chunk(1), chunk(2), chunk(3)]
    args = [p, p, p, p]
    if grid_mode:
        in_specs += [halo_prev(1), halo_prev(2), halo_next(1), halo_next(2)]
        args += [p, p, p, p]
    in_specs += [tok, vec, pl.BlockSpec((3, e), lambda i: (0, 0)), pl.BlockSpec((e, d_model), lambda i: (0, 0)), vec, vec]
    args += [x, gt, conv_w, w_out, ln_g, ln_b]
    return pl.pallas_call(
        body, name=name, grid=(nb,),
        out_shape=(jax.ShapeDtypeStruct((n_tok, d_model), F32), jax.ShapeDtypeStruct((n_tok, 1), F32),
                   jax.ShapeDtypeStruct((n_tok, e), BF16), jax.ShapeDtypeStruct((n_tok, e), BF16),
                   jax.ShapeDtypeStruct((n_tok, d_model), BF16)),
        in_specs=in_specs, out_specs=(tok, pl.BlockSpec((tb, 1), lambda i: (i, 0)), wide, wide, tok),
        compiler_params=_params(1),
    )(*args)


def _scan_block(buf_ref, tab_ref, cr, ci, *, reverse, tb, sb):
    n_slabs = tb // SUBLANES

    def slab(s, carry):
        cr, ci = carry
        idx = (n_slabs - 1 - s) if reverse else s
        r0 = pl.multiple_of(idx * SUBLANES, SUBLANES)
        xr = buf_ref[pl.ds(r0, SUBLANES), 0:sb]
        xi = buf_ref[pl.ds(r0, SUBLANES), sb:2 * sb]
        for k, step in enumerate((1, 2, 4)):
            ar = tab_ref[2 * k]
            ai = tab_ref[2 * k + 1]
            shift = (SUBLANES - step) if reverse else step
            rr = pltpu.roll(xr, shift, 0)
            ri = pltpu.roll(xi, shift, 0)
            xr, xi = xr + ar * rr - ai * ri, xi + ar * ri + ai * rr
        pr = tab_ref[6]
        pi_ = tab_ref[7]
        xr, xi = xr + pr * cr - pi_ * ci, xi + pr * ci + pi_ * cr
        buf_ref[pl.ds(r0, SUBLANES), 0:sb] = xr
        buf_ref[pl.ds(r0, SUBLANES), sb:2 * sb] = xi
        last = 0 if reverse else SUBLANES - 1
        return (jnp.broadcast_to(xr[last:last + 1, :], (SUBLANES, sb)),
                jnp.broadcast_to(xi[last:last + 1, :], (SUBLANES, sb)))

    return lax.fori_loop(0, n_slabs, slab, (cr, ci))


def _s5_fwd(ucat, bmat, cmat, tab, *, reverse, n_lat, tb, name):
    ngb, n_cat, _ = ucat.shape
    sb = bmat.shape[2] // 2
    nb_lat = n_lat // tb
    steps = nb_lat + 1
    nb_cat = n_cat // tb

    def blk(i):
        return (nb_cat - 1 - i) if reverse else i

    def body(u_ref, b_ref, c_ref, tab_ref, y_ref, hc_ref, h_scr, carry_scr):
        i = pl.program_id(1)

        @pl.when(i == 0)
        def _():
            carry_scr[...] = jnp.zeros_like(carry_scr)

        hc_ref[0, 0] = carry_scr[...]
        h_scr[...] = _dot(u_ref[0], b_ref[0])
        cr, ci = _scan_block(h_scr, tab_ref.at[0], carry_scr[:, 0:sb], carry_scr[:, sb:2 * sb],
                             reverse=reverse, tb=tb, sb=sb)
        carry_scr[:, 0:sb] = cr
        carry_scr[:, sb:2 * sb] = ci
        y_ref[0] = _dot(h_scr[...].astype(BF16), c_ref[0]).astype(BF16)

    return pl.pallas_call(
        body, name=name, grid=(ngb, steps),
        out_shape=(jax.ShapeDtypeStruct((ngb, n_lat, LANES), BF16),
                   jax.ShapeDtypeStruct((ngb, steps, SUBLANES, 2 * sb), F32)),
        in_specs=[pl.BlockSpec((1, tb, LANES), lambda g, i: (g, blk(i), 0)),
                  pl.BlockSpec((1, LANES, 2 * sb), lambda g, i: (g, 0, 0)),
                  pl.BlockSpec((1, 2 * sb, LANES), lambda g, i: (g, 0, 0)),
                  pl.BlockSpec((1, 8, SUBLANES, sb), lambda g, i: (g, 0, 0, 0))],
        out_specs=(pl.BlockSpec((1, tb, LANES), lambda g, i: (g, jnp.clip(blk(i) - 1, 0, nb_lat - 1), 0)),
                   pl.BlockSpec((1, 1, SUBLANES, 2 * sb), lambda g, i: (g, i, 0, 0))),
        scratch_shapes=[pltpu.VMEM((tb, 2 * sb), F32), pltpu.VMEM((SUBLANES, 2 * sb), F32)],
        compiler_params=_params(2),
    )(ucat, bmat, cmat, tab)


def _s5_bwd(ucat, dy, hc, bmat, bmat_t, cmat_t, tab, tab_adj, *, reverse, n_lat, tb, name):
    ngb, n_cat, _ = ucat.shape
    sb = bmat.shape[2] // 2
    nb_lat = n_lat // tb
    steps = nb_lat + 1
    nb_cat = n_cat // tb
    n_slabs = tb // SUBLANES

    def fwd_step(i):
        return steps - 1 - i

    def blk(i):
        s = fwd_step(i)
        return (nb_cat - 1 - s) if reverse else s

    def body(u_ref, dy_ref, hc_ref, b_ref, bt_ref, ct_ref, tab_ref, taba_ref, du_ref, db_ref, dc_ref, da_ref,
             h_scr, lam_scr, lcarry_scr, da_scr):
        i = pl.program_id(1)
        first = i == 0

        @pl.when(first)
        def _():
            lcarry_scr[...] = jnp.zeros_like(lcarry_scr)
            da_scr[...] = jnp.zeros_like(da_scr)

        uv = u_ref[0]
        h_scr[...] = _dot(uv, b_ref[0])
        h0r = hc_ref[0, 0, :, 0:sb]
        h0i = hc_ref[0, 0, :, sb:2 * sb]
        _scan_block(h_scr, tab_ref.at[0], h0r, h0i, reverse=reverse, tb=tb, sb=sb)

        b = blk(i)
        is_lat = jnp.logical_and(b >= 1, b <= nb_lat)
        dyv = jnp.where(is_lat, dy_ref[0], jnp.zeros_like(dy_ref[0]))
        lam_scr[...] = _dot(dyv, ct_ref[0])
        lr, li = _scan_block(lam_scr, taba_ref.at[0], lcarry_scr[:, 0:sb], lcarry_scr[:, sb:2 * sb],
                             reverse=not reverse, tb=tb, sb=sb)
        lcarry_scr[:, 0:sb] = lr
        lcarry_scr[:, sb:2 * sb] = li

        lam_bf = lam_scr[...].astype(BF16)
        du_ref[0] = _dot(lam_bf, bt_ref[0]).astype(BF16)
        _acc(db_ref.at[0], first, _dot_tn(uv, lam_bf))
        _acc(dc_ref.at[0], first, _dot_tn(dyv, h_scr[...].astype(BF16)))

        sub = lax.broadcasted_iota(jnp.int32, (SUBLANES, sb), 0)

        def slab(s, carry):
            pr, pi_, ar_acc, ai_acc = carry
            idx = (n_slabs - 1 - s) if reverse else s
            r0 = pl.multiple_of(idx * SUBLANES, SUBLANES)
            hr = h_scr[pl.ds(r0, SUBLANES), 0:sb]
            hi = h_scr[pl.ds(r0, SUBLANES), sb:2 * sb]
            qr = lam_scr[pl.ds(r0, SUBLANES), 0:sb]
            qi = lam_scr[pl.ds(r0, SUBLANES), sb:2 * sb]
            if reverse:
                hpr = jnp.where(sub == SUBLANES - 1, pr, pltpu.roll(hr, SUBLANES - 1, 0))
                hpi = jnp.where(sub == SUBLANES - 1, pi_, pltpu.roll(hi, SUBLANES - 1, 0))
                last = 0
            else:
                hpr = jnp.where(sub == 0, pr, pltpu.roll(hr, 1, 0))
                hpi = jnp.where(sub == 0, pi_, pltpu.roll(hi, 1, 0))
                last = SUBLANES - 1
            ar_acc = ar_acc + qr * hpr + qi * hpi
            ai_acc = ai_acc + qi * hpr - qr * hpi
            return (jnp.broadcast_to(hr[last:last + 1, :], (SUBLANES, sb)),
                    jnp.broadcast_to(hi[last:last + 1, :], (SUBLANES, sb)), ar_acc, ai_acc)

        _, _, ar_acc, ai_acc = lax.fori_loop(0, n_slabs, slab, (h0r, h0i, da_scr[:, 0:sb], da_scr[:, sb:2 * sb]))
        da_scr[:, 0:sb] = ar_acc
        da_scr[:, sb:2 * sb] = ai_acc

        @pl.when(i == steps - 1)
        def _():
            da_ref[0] = _colsum(da_scr[...])

    mat = pl.BlockSpec((1, LANES, 2 * sb), lambda g, i: (g, 0, 0))
    tabs = pl.BlockSpec((1, 8, SUBLANES, sb), lambda g, i: (g, 0, 0, 0))
    return pl.pallas_call(
        body, name=name, grid=(ngb, steps),
        out_shape=(jax.ShapeDtypeStruct((ngb, n_cat, LANES), BF16),
                   jax.ShapeDtypeStruct((ngb, LANES, 2 * sb), F32),
                   jax.ShapeDtypeStruct((ngb, LANES, 2 * sb), F32),
                   jax.ShapeDtypeStruct((ngb, 1, 2 * sb), F32)),
        in_specs=[pl.BlockSpec((1, tb, LANES), lambda g, i: (g, blk(i), 0)),
                  pl.BlockSpec((1, tb, LANES), lambda g, i: (g, jnp.clip(blk(i) - 1, 0, nb_lat - 1), 0)),
                  pl.BlockSpec((1, 1, SUBLANES, 2 * sb), lambda g, i: (g, fwd_step(i), 0, 0)),
                  mat, pl.BlockSpec((1, 2 * sb, LANES), lambda g, i: (g, 0, 0)), mat, tabs, tabs],
        out_specs=(pl.BlockSpec((1, tb, LANES), lambda g, i: (g, blk(i), 0)), mat, mat,
                   pl.BlockSpec((1, 1, 2 * sb), lambda g, i: (g, 0, 0))),
        scratch_shapes=[pltpu.VMEM((tb, 2 * sb), F32), pltpu.VMEM((tb, 2 * sb), F32),
                        pltpu.VMEM((SUBLANES, 2 * sb), F32), pltpu.VMEM((SUBLANES, 2 * sb), F32)],
        compiler_params=_params(2),
    )(ucat, dy, hc, bmat, bmat_t, cmat_t, tab, tab_adj)


def _glu_loss(ucat, yf, yb, z, xhat0, ln0, gt, d_vec, w_glu, b_glu, w_out, ln1, target, *, row_off, tb, name):
    ngb = ucat.shape[0]
    n_tok, d_model = xhat0.shape
    e = ngb * LANES
    tb = min(tb, n_tok)
    ob = row_off // tb
    nz = z.shape[0]

    def body(u_ref, yf_ref, yb_ref, z_ref, xh0_ref, g0_ref, b0_ref, gt_ref, d_ref, wg_ref, bg_ref, wo_ref, g1_ref,
             b1_ref, t_ref, loss_ref, dxr_ref, do_ref, gz_ref, gg_ref, dq_ref, dz_ref, dyl_ref, dg1_ref, db1_ref,
             dgt_ref, dbg_ref, dd_ref, loss_scr, yl_scr, th_scr, s_scr, dg_scr):
        i = pl.program_id(0)
        first = i == 0
        zw = e // nz
        cs = min(512, zw)

        def z_slab(c0):
            return z_ref[c0 // zw, :, c0 % zw:c0 % zw + cs].astype(F32)

        for q in range(ngb):
            sl = slice(q * LANES, (q + 1) * LANES)
            yl = d_ref[:, sl] * u_ref[q].astype(F32) + yf_ref[q].astype(F32) + yb_ref[q].astype(F32)
            th = jnp.tanh(GELU_K * (yl + GELU_C * yl * yl * yl))
            yl_scr[:, sl] = yl
            th_scr[:, sl] = th
            gg_ref[:, sl] = (0.5 * yl * (1.0 + th)).astype(BF16)
        s_scr[...] = _sigmoid(_dot(gg_ref[...], wg_ref[...]) + bg_ref[...])
        for c0 in range(0, e, cs):
            sl = slice(c0, c0 + cs)
            zf = z_slab(c0)
            g2 = 0.5 * yl_scr[:, sl] * (1.0 + th_scr[:, sl]) * s_scr[:, sl]
            gz_ref[:, sl] = (g2 * (zf * _sigmoid(zf))).astype(BF16)
        o = _dot(gz_ref[...], wo_ref[...])
        x1 = xh0_ref[...] * g0_ref[...] + b0_ref[...]
        r = DN_ALPHA * x1 + gt_ref[...] * o
        rc = r - _rowmean(r)
        rstd = lax.rsqrt(_rowmean(rc * rc) + LN_EPS)
        xh = rc * rstd
        err = xh * g1_ref[...] + b1_ref[...] - t_ref[...]
        _acc(loss_scr, first, _colsum(err * err))
        dy = err * (1.0 / d_model)
        _acc(dg1_ref, first, _colsum(dy * xh))
        _acc(db1_ref, first, _colsum(dy))
        dxh = dy * g1_ref[...]
        dr = rstd * (dxh - _rowmean(dxh) - xh * _rowmean(dxh * xh))
        dxr_ref[...] = DN_ALPHA * dr
        _acc(dgt_ref, first, _colsum(dr * o))
        do_bf = (dr * gt_ref[...]).astype(BF16)
        do_ref[...] = do_bf
        dg_scr[...] = _dot_nt(do_bf, wo_ref[...])
        for c0 in range(0, e, cs):
            sl = slice(c0, c0 + cs)
            zf = z_slab(c0)
            sz = _sigmoid(zf)
            g = 0.5 * yl_scr[:, sl] * (1.0 + th_scr[:, sl])
            s = s_scr[:, sl]
            dgz = dg_scr[:, sl]
            dg2 = dgz * (zf * sz)
            dz_ref[:, sl] = (dgz * (g * s) * (sz * (1.0 + zf * (1.0 - sz)))).astype(BF16)
            dq = dg2 * g * s * (1.0 - s)
            _acc(dbg_ref.at[:, sl], first, _colsum(dq))
            dq_ref[:, sl] = dq.astype(BF16)
            dg_scr[:, sl] = dg2 * s
        dg_scr[...] += _dot_nt(dq_ref[...], wg_ref[...])
        for q in range(ngb):
            sl = slice(q * LANES, (q + 1) * LANES)
            yl = yl_scr[:, sl]
            th = th_scr[:, sl]
            dgelu = 0.5 * (1.0 + th) + 0.5 * yl * (1.0 - th * th) * (GELU_K * (1.0 + 3.0 * GELU_C * yl * yl))
            dyl = dg_scr[:, sl] * dgelu
            _acc(dd_ref.at[:, sl], first, _colsum(dyl * u_ref[q].astype(F32)))
            dyl_ref[q] = dyl.astype(BF16)

        @pl.when(i == pl.num_programs(0) - 1)
        def _():
            loss_ref[...] = (0.5 / d_model) * jnp.sum(loss_scr[...], axis=1, keepdims=True)

    vec = pl.BlockSpec((1, d_model), lambda i: (0, 0))
    evec = pl.BlockSpec((1, e), lambda i: (0, 0))
    tok = pl.BlockSpec((tb, d_model), lambda i: (i, 0))
    wide = pl.BlockSpec((tb, e), lambda i: (i, 0))
    gblk = pl.BlockSpec((ngb, tb, LANES), lambda i: (0, i, 0))
    once = dict(pipeline_mode=pl.Buffered(1))
    tok_f = jax.ShapeDtypeStruct((n_tok, d_model), F32)
    tok_b = jax.ShapeDtypeStruct((n_tok, d_model), BF16)
    wide_b = jax.ShapeDtypeStruct((n_tok, e), BF16)
    vec_f = jax.ShapeDtypeStruct((1, d_model), F32)
    evec_f = jax.ShapeDtypeStruct((1, e), F32)
    return pl.pallas_call(
        body, name=name, grid=(n_tok // tb,),
        out_shape=(jax.ShapeDtypeStruct((1, 1), F32), tok_f, tok_b, wide_b, wide_b, wide_b, wide_b,
                   jax.ShapeDtypeStruct((ngb, n_tok, LANES), BF16), vec_f, vec_f, vec_f, evec_f, evec_f),
        in_specs=[pl.BlockSpec((ngb, tb, LANES), lambda i: (0, i + ob, 0)), gblk, gblk,
                  pl.BlockSpec((nz, tb, e // nz), lambda i: (0, i, 0)), tok, vec, vec, vec, evec,
                  pl.BlockSpec((e, e), lambda i: (0, 0), **once), evec,
                  pl.BlockSpec((e, d_model), lambda i: (0, 0), **once), vec, vec, tok],
        out_specs=(pl.BlockSpec((1, 1), lambda i: (0, 0)), tok, tok, wide, wide, wide, wide, gblk, vec, vec, vec, evec,
                   evec),
        scratch_shapes=[pltpu.VMEM((1, d_model), F32)] + [pltpu.VMEM((tb, e), F32)] * 4,
        compiler_params=_params(1),
    )(ucat, yf, yb, z, xhat0, ln0[0], ln0[1], gt, d_vec, w_glu, b_glu, w_out, ln1[0], ln1[1], target)


def _ssm_inbwd(duf, dub, w, xhat, rstd, ln, sc, gt_prev, f_prev, *, lat, row_f, row_b, tb, name):
    ngb = duf.shape[0]
    e = ngb * LANES
    n_tok, d_model = xhat.shape
    tb = min(tb, n_tok)
    obf, obb = row_f // tb, row_b // tb
    has_lat = lat is not None
    n_w = w.shape[0] if has_lat else w.shape[0] // 2

    def body(*refs):
        if has_lat:
            (duf_ref, dub_ref, dyl_ref, dz_ref, d_ref, dxr_ref, w_ref, xh_ref, rs_ref, g_ref, b_ref, sc_ref, gt_ref,
             f_ref, dp_ref, dr_ref, df_ref, dsc_ref, dsh_ref, dg_ref, db_ref, dgt_ref) = refs
        else:
            (duf_ref, dub_ref, w_ref, xh_ref, rs_ref, g_ref, b_ref, sc_ref, gt_ref, f_ref, dp_ref, dr_ref, df_ref,
             dsc_ref, dsh_ref, dg_ref, db_ref, dgt_ref) = refs
        first = pl.program_id(0) == 0
        du = (jnp.concatenate([duf_ref[q] for q in range(ngb)], axis=1).astype(F32)
              + jnp.concatenate([dub_ref[q] for q in range(ngb)], axis=1).astype(F32))
        if has_lat:
            du = du + d_ref[...] * jnp.concatenate([dyl_ref[q] for q in range(ngb)], axis=1).astype(F32)
            dp_ref[:, e:2 * e] = dz_ref[...]
        else:
            dp_ref[:, e:2 * e] = jnp.zeros((tb, e), BF16)
        dp_ref[:, 0:e] = du.astype(BF16)
        dh = jnp.zeros((tb, d_model), F32)
        for j in range(n_w):
            dh = dh + _dot_nt(dp_ref[:, j * d_model:(j + 1) * d_model], w_ref[j])
        xh = xh_ref[...]
        x1 = xh * g_ref[...] + b_ref[...]
        dx1 = dh * (1.0 + sc_ref[...])
        if has_lat:
            dx1 = dx1 + dxr_ref[...]
        _acc(dsc_ref, first, _colsum(dh * x1))
        _acc(dsh_ref, first, _colsum(dh))
        _acc(dg_ref, first, _colsum(dx1 * xh))
        _acc(db_ref, first, _colsum(dx1))
        dxh = dx1 * g_ref[...]
        dr = rs_ref[...] * (dxh - _rowmean(dxh) - xh * _rowmean(dxh * xh))
        dr_ref[...] = dr
        df_ref[...] = (dr * gt_ref[...]).astype(BF16)
        _acc(dgt_ref, first, _colsum(dr * f_ref[...].astype(F32)))

    vec = pl.BlockSpec((1, d_model), lambda i: (0, 0))
    tok = pl.BlockSpec((tb, d_model), lambda i: (i, 0))
    gblk = pl.BlockSpec((ngb, tb, LANES), lambda i: (0, i, 0))
    in_specs = [pl.BlockSpec((ngb, tb, LANES), lambda i: (0, i + obf, 0)),
                pl.BlockSpec((ngb, tb, LANES), lambda i: (0, i + obb, 0))]
    args = [duf, dub]
    if has_lat:
        in_specs += [gblk, pl.BlockSpec((tb, e), lambda i: (i, 0)), pl.BlockSpec((1, e), lambda i: (0, 0)), tok]
        args += list(lat)
    in_specs += [pl.BlockSpec(w.shape, lambda i: (0, 0, 0)), tok, pl.BlockSpec((tb, 1), lambda i: (i, 0)), vec, vec, vec,
                 vec, tok]
    args += [w, xhat, rstd, ln[0], ln[1], sc, gt_prev, f_prev]
    vec_f = jax.ShapeDtypeStruct((1, d_model), F32)
    return pl.pallas_call(
        body, name=name, grid=(n_tok // tb,),
        out_shape=(jax.ShapeDtypeStruct((n_tok, 2 * e), BF16), jax.ShapeDtypeStruct((n_tok, d_model), F32),
                   jax.ShapeDtypeStruct((n_tok, d_model), BF16), vec_f, vec_f, vec_f, vec_f, vec_f),
        in_specs=in_specs,
        out_specs=(pl.BlockSpec((tb, 2 * e), lambda i: (i, 0)), tok, tok, vec, vec, vec, vec, vec),
        compiler_params=_params(1),
    )(*args)


def _conv_bwd_a(df, w_out, p, yc, *, tb, name):
    _, n_tok, e = p.shape
    d_model = df.shape[1]
    tb = min(tb, n_tok)
    cs = _slab_width(e)

    def body(df_ref, wo_ref, bg_ref, z_ref, yc_ref, dbg_ref, dz_ref, dyc_ref):
        dfv = df_ref[...]
        for c0 in range(0, e, cs):
            sl = slice(c0, c0 + cs)
            dgv = _dot_nt(dfv, wo_ref[sl, :])
            zf = z_ref[0, :, sl].astype(F32)
            sz = _sigmoid(zf)
            silu_z = zf * sz
            bg = bg_ref[0, :, sl].astype(F32)
            yc = yc_ref[:, sl].astype(F32)
            dbg_ref[:, sl] = (dgv * yc * silu_z).astype(BF16)
            dyc_ref[:, sl] = (dgv * bg * silu_z).astype(BF16)
            dz_ref[:, sl] = (dgv * bg * yc * (sz * (1.0 + zf * (1.0 - sz)))).astype(BF16)

    wide = pl.BlockSpec((tb, e), lambda i: (i, 0))
    shape = jax.ShapeDtypeStruct((n_tok, e), BF16)
    return pl.pallas_call(
        body, name=name, grid=(n_tok // tb,), out_shape=(shape, shape, shape),
        in_specs=[pl.BlockSpec((tb, d_model), lambda i: (i, 0)), pl.BlockSpec((e, d_model), lambda i: (0, 0)),
                  pl.BlockSpec((1, tb, e), lambda i: (0, i, 0)), pl.BlockSpec((1, tb, e), lambda i: (3, i, 0)), wide],
        out_specs=(wide, wide, wide), compiler_params=_params(1),
    )(df, w_out, p, p, yc)


def _conv_bwd_b(dyc, p, dbg, dz, conv_w, *, grid_mode, tb, name):
    _, n_tok, e = p.shape
    eh = e // 2
    if not grid_mode:
        tb = n_tok
    tb = min(tb, n_tok)
    nb = n_tok // tb
    hb = tb // GRID_W
    cs = _slab_width(e)

    def body(*refs):
        if grid_mode:
            dyc_ref, dycp_ref, dycn_ref, cg_ref, v_ref, dbg_ref, dz_ref, cw_ref, dp_ref, dcw_ref = refs
        else:
            dyc_ref, cg_ref, v_ref, dbg_ref, dz_ref, cw_ref, dp_ref, dcw_ref = refs
        i = pl.program_id(0)
        first = i == 0
        rows = lax.broadcasted_iota(jnp.int32, (tb, 1), 0)
        dp_ref[0] = dbg_ref[...]
        dp_ref[3] = dz_ref[...]
        for c0 in range(0, e, cs):
            sl = slice(c0, c0 + cs)
            dyc = dyc_ref[:, sl].astype(F32)
            w = cw_ref[:, sl]
            if grid_mode and c0 >= eh:
                hs = slice(c0 - eh, c0 - eh + cs)
                dprev = jnp.where(i > 0, dycp_ref[:, hs].astype(F32), 0.0)
                dnext = jnp.where(i < nb - 1, dycn_ref[:, hs].astype(F32), 0.0)
                if tb > GRID_W:
                    dm = jnp.concatenate([dprev, dyc[:tb - GRID_W]], axis=0)
                    dpl = jnp.concatenate([dyc[GRID_W:], dnext], axis=0)
                else:
                    dm, dpl = dprev, dnext
            else:
                dm, dpl = _shifted(dyc, rows, GRID_W if grid_mode else tb, tb)
            cg = cg_ref[0, :, sl].astype(F32)
            v = v_ref[0, :, sl].astype(F32)
            u = cg * v
            du = w[0:1] * dpl + w[1:2] * dyc + w[2:3] * dm
            dp_ref[1, :, sl] = (du * v).astype(BF16)
            dp_ref[2, :, sl] = (du * cg).astype(BF16)
            _acc(dcw_ref.at[:, sl], first, jnp.concatenate([_colsum(u * dpl), _colsum(u * dyc), _colsum(u * dm)], axis=0))

    n_hrows = n_tok // GRID_W
    wide = pl.BlockSpec((tb, e), lambda i: (i, 0))
    in_specs = [wide]
    args = [dyc]
    if grid_mode:
        in_specs += [pl.BlockSpec((GRID_W, eh), lambda i: (jnp.maximum(i * hb - 1, 0), 1)),
                     pl.BlockSpec((GRID_W, eh), lambda i: (jnp.minimum((i + 1) * hb, n_hrows - 1), 1))]
        args += [dyc, dyc]
    in_specs += [pl.BlockSpec((1, tb, e), lambda i: (1, i, 0)), pl.BlockSpec((1, tb, e), lambda i: (2, i, 0)), wide, wide,
                 pl.BlockSpec((3, e), lambda i: (0, 0))]
    args += [p, p, dbg, dz, conv_w]
    return pl.pallas_call(
        body, name=name, grid=(nb,),
        out_shape=(jax.ShapeDtypeStruct((4, n_tok, e), BF16), jax.ShapeDtypeStruct((3, e), F32)),
        in_specs=in_specs,
        out_specs=(pl.BlockSpec((4, tb, e), lambda i: (0, i, 0)), pl.BlockSpec((3, e), lambda i: (0, 0))),
        compiler_params=_params(1),
    )(*args)


def _conv_inbwd(dp, w, dr, x, sc, *, tb, name):
    n_chunks, n_tok, e = dp.shape
    d_model = x.shape[1]
    tb = min(tb, n_tok)

    def body(dp_ref, w_ref, dr_ref, x_ref, sc_ref, gx_ref, dsc_ref, dsh_ref, dh_scr):
        k = pl.program_id(1)
        first = pl.program_id(0) == 0
        _acc(dh_scr, k == 0, _dot_nt(dp_ref[0], w_ref[0]))

        @pl.when(k == n_chunks - 1)
        def _():
            dh = dh_scr[...]
            gx_ref[...] = DN_ALPHA * dr_ref[...] + dh * (1.0 + sc_ref[...])
            _acc(dsc_ref, first, _colsum(dh * x_ref[...]))
            _acc(dsh_ref, first, _colsum(dh))

    vec = pl.BlockSpec((1, d_model), lambda i, k: (0, 0))
    tok = pl.BlockSpec((tb, d_model), lambda i, k: (i, 0))
    vec_f = jax.ShapeDtypeStruct((1, d_model), F32)
    return pl.pallas_call(
        body, name=name, grid=(n_tok // tb, n_chunks),
        out_shape=(jax.ShapeDtypeStruct((n_tok, d_model), F32), vec_f, vec_f),
        in_specs=[pl.BlockSpec((1, tb, e), lambda i, k: (k, i, 0)), pl.BlockSpec((1, d_model, e), lambda i, k: (k, 0, 0)),
                  tok, tok, vec],
        out_specs=(tok, vec, vec),
        scratch_shapes=[pltpu.VMEM((tb, d_model), F32)],
        compiler_params=_params(2),
    )(dp, w, dr, x, sc)


def _wgrad(a, b, *, n_chunks, tm, tl, init=None, name):
    n_tok, m = a.shape
    tl = min(tl, n_tok)
    chunked = b.ndim == 3
    cw = b.shape[2] if chunked else b.shape[1] // n_chunks
    has_init = init is not None

    def body(*refs):
        if has_init:
            a_ref, b_ref, init_ref, o_ref = refs
        else:
            a_ref, b_ref, o_ref = refs
        bv = b_ref[0] if chunked else b_ref[...]
        part = _dot_tn(a_ref[...], bv)
        l = pl.program_id(2)

        @pl.when(l == 0)
        def _():
            o_ref[0] = part + init_ref[0] if has_init else part

        @pl.when(l > 0)
        def _():
            o_ref[0] += part

    o_spec = pl.BlockSpec((1, tm, cw), lambda jm, jc, l: (jc, jm, 0))
    b_spec = (pl.BlockSpec((1, tl, cw), lambda jm, jc, l: (jc, l, 0)) if chunked
              else pl.BlockSpec((tl, cw), lambda jm, jc, l: (l, jc)))
    in_specs = [pl.BlockSpec((tl, tm), lambda jm, jc, l: (l, jm)), b_spec] + ([o_spec] if has_init else [])
    args = (a, b) + ((init,) if has_init else ())
    return pl.pallas_call(
        body, name=name, grid=(m // tm, n_chunks, n_tok // tl),
        out_shape=jax.ShapeDtypeStruct((n_chunks, m, cw), F32),
        in_specs=in_specs, out_specs=o_spec, compiler_params=_params(3),
    )(*args)


def _block_diag(t, ngb):
    g, p, n = t.shape
    gpb = g // ngb
    eye = jnp.eye(gpb, dtype=t.dtype)
    return jnp.einsum("bgpn,gh->bgphn", t.reshape(ngb, gpb, p, n), eye).reshape(ngb, gpb * p, gpb * n)


def _block_diag_t(mat, g, p, n):
    ngb = mat.shape[0]
    gpb = g // ngb
    eye = jnp.eye(gpb, dtype=mat.dtype)
    return jnp.einsum("bgphn,gh->bgpn", mat.reshape(ngb, gpb, p, gpb, n), eye).reshape(g, p, n)


def _scan_tables(pw_r, pw_i, ngb, reverse):
    _, g, n = pw_r.shape
    sb = g * n // ngb
    rows = jnp.arange(SUBLANES)
    kinds = []
    for step in (1, 2, 4):
        mask = ((rows < SUBLANES - step) if reverse else (rows >= step)).astype(F32)
        for part in (pw_r[step - 1], pw_i[step - 1]):
            kinds.append(part.reshape(ngb, 1, sb) * mask[None, :, None])
    for part in (pw_r, pw_i):
        pw = part[::-1] if reverse else part
        kinds.append(jnp.transpose(pw.reshape(SUBLANES, ngb, sb), (1, 0, 2)))
    return jnp.stack(kinds, axis=1)


def _flat(parts):
    return jnp.concatenate([p.reshape(-1) for p in parts])


def _unflat(vec, shapes):
    out, off = [], 0
    for s in shapes:
        size = math.prod(s)
        out.append(vec[off:off + size].reshape(s))
        off += size
    return out


def kernel(x, c, ctx, c_ctx, ada_w, ada_b, ln_g, ln_b, conv_w_in, conv_w, conv_w_out, ssm_w_in, ssm_lam_re, ssm_lam_im, ssm_log_step, ssm_b_re, ssm_b_im, ssm_c_re, ssm_c_im, ssm_d, ssm_w_glu, ssm_b_glu, ssm_w_out, loss_target, m_c_ctx, m_ada_w, m_ada_b, m_ln_g, m_ln_b, m_conv_w_in, m_conv_w, m_conv_w_out, m_ssm_w_in, m_ssm_lam_re, m_ssm_lam_im, m_ssm_log_step, m_ssm_b_re, m_ssm_b_im, m_ssm_c_re, m_ssm_c_im, m_ssm_d, m_ssm_w_glu, m_ssm_b_glu, m_ssm_w_out, v_c_ctx, v_ada_w, v_ada_b, v_ln_g, v_ln_b, v_conv_w_in, v_conv_w, v_conv_w_out, v_ssm_w_in, v_ssm_lam_re, v_ssm_lam_im, v_ssm_log_step, v_ssm_b_re, v_ssm_b_im, v_ssm_c_re, v_ssm_c_im, v_ssm_d, v_ssm_w_glu, v_ssm_b_glu, v_ssm_w_out):
    weights = dict(c_ctx=c_ctx, ada_w=ada_w, ada_b=ada_b, ln_g=ln_g, ln_b=ln_b, conv_w_in=conv_w_in, conv_w=conv_w,
                   conv_w_out=conv_w_out, ssm_w_in=ssm_w_in, ssm_lam_re=ssm_lam_re, ssm_lam_im=ssm_lam_im,
                   ssm_log_step=ssm_log_step, ssm_b_re=ssm_b_re, ssm_b_im=ssm_b_im, ssm_c_re=ssm_c_re,
                   ssm_c_im=ssm_c_im, ssm_d=ssm_d, ssm_w_glu=ssm_w_glu, ssm_b_glu=ssm_b_glu, ssm_w_out=ssm_w_out)
    mom_m = dict(c_ctx=m_c_ctx, ada_w=m_ada_w, ada_b=m_ada_b, ln_g=m_ln_g, ln_b=m_ln_b, conv_w_in=m_conv_w_in,
                 conv_w=m_conv_w, conv_w_out=m_conv_w_out, ssm_w_in=m_ssm_w_in, ssm_lam_re=m_ssm_lam_re,
                 ssm_lam_im=m_ssm_lam_im, ssm_log_step=m_ssm_log_step, ssm_b_re=m_ssm_b_re, ssm_b_im=m_ssm_b_im,
                 ssm_c_re=m_ssm_c_re, ssm_c_im=m_ssm_c_im, ssm_d=m_ssm_d, ssm_w_glu=m_ssm_w_glu,
                 ssm_b_glu=m_ssm_b_glu, ssm_w_out=m_ssm_w_out)
    mom_v = dict(c_ctx=v_c_ctx, ada_w=v_ada_w, ada_b=v_ada_b, ln_g=v_ln_g, ln_b=v_ln_b, conv_w_in=v_conv_w_in,
                 conv_w=v_conv_w, conv_w_out=v_conv_w_out, ssm_w_in=v_ssm_w_in, ssm_lam_re=v_ssm_lam_re,
                 ssm_lam_im=v_ssm_lam_im, ssm_log_step=v_ssm_log_step, ssm_b_re=v_ssm_b_re, ssm_b_im=v_ssm_b_im,
                 ssm_c_re=v_ssm_c_re, ssm_c_im=v_ssm_c_im, ssm_d=v_ssm_d, ssm_w_glu=v_ssm_w_glu,
                 ssm_b_glu=v_ssm_b_glu, ssm_w_out=v_ssm_w_out)
    names = list(weights)

    n_lat, d_model = x.shape[1], x.shape[2]
    n_ctx = ctx.shape[1]
    e = 2 * d_model
    n_grp, n_state, grp = ssm_lam_re.shape[2], ssm_lam_re.shape[3], ssm_b_re.shape[4]
    ngb = e // LANES
    ws = ada_w.shape[2]
    tb_tok = min(512, n_lat)
    tb_s5 = min(1024, n_lat // 4)
    tb_glu = min(256, n_lat)
    n_cat = n_lat + 2 * tb_s5
    chip = 2 * lax.axis_index("x") + lax.axis_index("y")
    me = 2 * chip + lax.axis_index("c")
    chips, everyone, pair = ("x", "y"), MESH_AXES, ("c",)

    x2, ctx2, tgt2 = x[0], ctx[0], loss_target[0]

    wc_in = _exchange(conv_w_in[0].astype(BF16), chips, False, "ag_conv_w_in")
    wc_out = _exchange(conv_w_out[0].astype(BF16), chips, False, "ag_conv_w_out").reshape(e, d_model)
    ws_in = _exchange(ssm_w_in[0].astype(BF16), chips, False, "ag_ssm_w_in")
    w_glu = _exchange(ssm_w_glu[0].astype(BF16), chips, False, "ag_ssm_w_glu").reshape(e, e)
    ws_out = _exchange(ssm_w_out[0].astype(BF16), chips, False, "ag_ssm_w_out").reshape(e, d_model)
    small_full = _exchange(_flat([conv_w[0], ssm_d[0], ssm_b_glu[0]]).reshape(1, -1), chips, False, "ag_small")
    es = conv_w.shape[2]
    conv_w_full = jnp.transpose(small_full[:, 0, :3 * es].reshape(4, 3, es), (1, 0, 2)).reshape(3, e)
    d_full = small_full[:, 0, 3 * es:4 * es].reshape(1, e)
    b_glu_full = small_full[:, 0, 4 * es:5 * es].reshape(1, e)

    c_all = _exchange(c, everyone, False, "ag_c").reshape(8, d_model)
    cc2 = c_ctx.reshape(1, d_model)
    b_sh = lax.dynamic_slice_in_dim(ada_b, chip * ws, ws, axis=1).reshape(DEPTH, 1, ws)
    m_sh = _ada_fwd(c_all, cc2, ada_w, b_sh)
    m_all = _exchange(m_sh, chips, False, "ag_mod")
    m_full = jnp.transpose(m_all, (1, 2, 0, 3)).reshape(DEPTH, 16, 3 * d_model)
    m_lat = lax.dynamic_slice_in_dim(m_full, me, 1, axis=1)
    m_ctx = m_full[:, 8:9]

    def mods(m, i):
        return m[i, :, 0:d_model], m[i, :, d_model:2 * d_model], m[i, :, 2 * d_model:3 * d_model]

    sh0, sc0, gt0 = mods(m_lat, 0)
    sh1, sc1, gt1 = mods(m_lat, 1)
    shc0, scc0, gtc0 = mods(m_ctx, 0)
    shc1, scc1, _ = mods(m_ctx, 1)
    ln0 = (ln_g[0:1], ln_b[0:1])
    ln1 = (ln_g[1:2], ln_b[1:2])

    rg = 2 * n_grp
    lam_re2 = ssm_lam_re[0].reshape(rg, n_state)
    lam_im2 = ssm_lam_im[0].reshape(rg, n_state)
    log_step2 = ssm_log_step[0].reshape(rg, 1)
    b_re_t = jnp.transpose(ssm_b_re[0], (3, 0, 1, 2)).reshape(grp, rg, n_state)
    b_im_t = jnp.transpose(ssm_b_im[0], (3, 0, 1, 2)).reshape(grp, rg, n_state)
    pw_r, pw_i, bbr, bbi = _zoh_fwd(lam_re2, lam_im2, log_step2, b_re_t, b_im_t)
    pw_r = pw_r.reshape(8, 2, n_grp, n_state)
    pw_i = pw_i.reshape(8, 2, n_grp, n_state)
    bbr_g = jnp.transpose(bbr.reshape(grp, 2, n_grp, n_state), (1, 2, 0, 3))
    bbi_g = jnp.transpose(bbi.reshape(grp, 2, n_grp, n_state), (1, 2, 0, 3))
    s5 = []
    for r in range(2):
        bm = jnp.concatenate([_block_diag(bbr_g[r], ngb), _block_diag(bbi_g[r], ngb)], axis=2).astype(BF16)
        cm_t = jnp.concatenate([_block_diag(ssm_c_re[0, r], ngb), -_block_diag(ssm_c_im[0, r], ngb)], axis=2).astype(BF16)
        s5.append(dict(
            bmat=bm, bmat_t=jnp.transpose(bm, (0, 2, 1)), cmat_t=cm_t, cmat=jnp.transpose(cm_t, (0, 2, 1)),
            tab=_scan_tables(pw_r[:, r], pw_i[:, r], ngb, reverse=(r == 1)),
            tab_adj=_scan_tables(pw_r[:, r], -pw_i[:, r], ngb, reverse=(r == 0))))

    p0, h0 = _inproj(x2, sc0, sh0, wc_in, tb=tb_tok, name="l0_inproj")
    pc0, hc0 = _inproj(ctx2, scc0, shc0, wc_in, tb=tb_tok, name="l0_inproj_ctx")
    xhat0, rstd0, g0, yc0, f0 = _convgate(p0, x2, gt0, conv_w_full, wc_out, *ln0, grid_mode=True, tb=tb_tok, name="l0_conv")
    chat0, crstd0, gc0, ycc0, fc0 = _convgate(pc0, ctx2, gtc0, conv_w_full, wc_out, *ln0, grid_mode=False, tb=tb_tok,
                                              name="l0_conv_ctx")

    ucat, h1 = _inproj(xhat0, sc1, sh1, ws_in[0:2], lnaff=ln0, tb=math.gcd(tb_tok, tb_s5), gb_rows=(n_cat, tb_s5),
                       name="l1_inproj_u")
    z1, _ = _inproj(xhat0, sc1, sh1, ws_in[2:4], lnaff=ln0, tb=tb_tok, name="l1_inproj_z")
    uc, hc1 = _inproj(chat0, scc1, shc1, ws_in[0:2], lnaff=ln0, tb=tb_tok, gb_rows=(n_ctx, 0), name="l1_inproj_ctx")
    pad = tb_s5 - n_ctx
    ucat = ucat.at[:, 0:pad].set(jnp.zeros((ngb, pad, LANES), BF16))
    ucat = ucat.at[:, pad:tb_s5].set(uc)
    ucat = ucat.at[:, tb_s5 + n_lat:tb_s5 + n_lat + n_ctx].set(uc)
    ucat = ucat.at[:, tb_s5 + n_lat + n_ctx:].set(jnp.zeros((ngb, pad, LANES), BF16))
    y_dir, hc_dir = [], []
    for r in range(2):
        yr, hcr = _s5_fwd(ucat, s5[r]["bmat"], s5[r]["cmat"], s5[r]["tab"], reverse=(r == 1), n_lat=n_lat, tb=tb_s5,
                          name=f"l1_s5_fwd{r}")
        y_dir.append(yr)
        hc_dir.append(hcr)

    (loss, dxres, do1, gz1, gg1, dq1, dz1, dyl, dg1, db1, dgt1, dbglu, dd) = _glu_loss(
        ucat, y_dir[0], y_dir[1], z1, xhat0, ln0, gt1, d_full, w_glu, b_glu_full, ws_out, ln1, tgt2,
        row_off=tb_s5, tb=tb_glu, name="l1_glu_loss")

    du_dir, dbm, dcm, da = [], [], [], []
    for r in range(2):
        dur, dbr, dcr, dar = _s5_bwd(ucat, dyl, hc_dir[r], s5[r]["bmat"], s5[r]["bmat_t"], s5[r]["cmat_t"], s5[r]["tab"],
                                     s5[r]["tab_adj"], reverse=(r == 1), n_lat=n_lat, tb=tb_s5, name=f"l1_s5_bwd{r}")
        du_dir.append(dur)
        dbm.append(dbr)
        dcm.append(dcr)
        da.append(dar)
    dp1, dr0, df0, dsc1, dsh1, dg0, db0, dgt0 = _ssm_inbwd(
        du_dir[0], du_dir[1], ws_in, xhat0, rstd0, ln0, sc1, gt0, f0, lat=(dyl, dz1, d_full, dxres),
        row_f=tb_s5, row_b=tb_s5, tb=tb_glu, name="l1_inbwd")
    dpc1, drc0, dfc0, dscc1, dshc1, dgc0, dbc0, dgtc0 = _ssm_inbwd(
        du_dir[0], du_dir[1], ws_in, chat0, crstd0, ln0, scc1, gtc0, fc0, lat=None,
        row_f=tb_s5 - n_ctx, row_b=tb_s5 + n_lat, tb=n_ctx, name="l1_inbwd_ctx")

    def conv_backward(df, p, yc, dr, xin, sc, grid_mode, tag):
        dbg, dz, dyc = _conv_bwd_a(df, wc_out, p, yc, tb=tb_tok, name="l0_bwd_a" + tag)
        dp, dcw = _conv_bwd_b(dyc, p, dbg, dz, conv_w_full, grid_mode=grid_mode, tb=tb_glu, name="l0_bwd_b" + tag)
        gx, dsc, dsh = _conv_inbwd(dp, wc_in, dr, xin, sc, tb=tb_tok, name="l0_inbwd" + tag)
        return dp, dcw, gx, dsc, dsh

    dp0, dcw0, grad_x, dsc0, dsh0 = conv_backward(df0, p0, yc0, dr0, x2, sc0, True, "")
    dpc0, dcwc0, _, dscc0, dshc0 = conv_backward(dfc0, pc0, ycc0, drc0, ctx2, scc0, False, "_ctx")

    tl = tb_tok
    gw_conv_in = _wgrad(h0, dp0, n_chunks=4, tm=d_model, tl=tl, name="wg_conv_in",
                        init=_wgrad(hc0, dpc0, n_chunks=4, tm=d_model, tl=tl, name="wg_conv_in_ctx"))
    gw_conv_out = _wgrad(g0, df0, n_chunks=1, tm=e, tl=tl, name="wg_conv_out",
                         init=_wgrad(gc0, dfc0, n_chunks=1, tm=e, tl=tl, name="wg_conv_out_ctx"))
    gw_ssm_in = _wgrad(h1, dp1, n_chunks=4, tm=d_model, tl=tl, name="wg_ssm_in",
                       init=_wgrad(hc1, dpc1, n_chunks=4, tm=d_model, tl=tl, name="wg_ssm_in_ctx"))
    gw_glu = _wgrad(gg1, dq1, n_chunks=1, tm=e // 2, tl=tl, name="wg_glu")
    gw_ssm_out = _wgrad(gz1, do1, n_chunks=1, tm=e, tl=tl, name="wg_ssm_out")

    grads, deltas, new_m, new_v = {}, {}, {}, {}

    def reduce_big(name, full):
        w = weights[name]
        rows = math.prod(w.shape[:-1])
        cols = w.shape[-1]
        parts = _exchange(full.reshape(8, rows // 2, cols), everyone, True, "rs_" + name)
        half = _sum_parts(parts, "sum_" + name)
        both = _exchange(half, pair, False, "pair_" + name).reshape(rows, cols)
        dlt, nm, nv = _adamw(w.reshape(rows, cols), both, mom_m[name].reshape(rows, cols),
                             mom_v[name].reshape(rows, cols), "adamw_" + name)
        grads[name], deltas[name] = both.reshape(w.shape), dlt.reshape(w.shape)
        new_m[name], new_v[name] = nm.reshape(w.shape), nv.reshape(w.shape)

    reduce_big("conv_w_in", gw_conv_in)
    reduce_big("conv_w_out", gw_conv_out)
    reduce_big("ssm_w_in", gw_ssm_in)
    reduce_big("ssm_w_glu", gw_glu)
    reduce_big("ssm_w_out", gw_ssm_out)

    gpn = (n_grp, grp, n_state)
    small_parts = [
        jnp.concatenate([dg0 + dgc0, dg1], axis=0), jnp.concatenate([db0 + dbc0, db1], axis=0),
        dcw0 + dcwc0, dd, dbglu,
        jnp.stack([da[0], da[1]]),
        jnp.stack([_block_diag_t(dbm[r][:, :, :ngb * 0 + dbm[r].shape[2] // 2], *gpn) for r in range(2)]),
        jnp.stack([_block_diag_t(dbm[r][:, :, dbm[r].shape[2] // 2:], *gpn) for r in range(2)]),
        jnp.stack([_block_diag_t(dcm[r][:, :, :dcm[r].shape[2] // 2], *gpn) for r in range(2)]),
        jnp.stack([-_block_diag_t(dcm[r][:, :, dcm[r].shape[2] // 2:], *gpn) for r in range(2)]),
    ]
    small_shapes = [p.shape for p in small_parts]
    flat = _flat(small_parts)
    quantum = 8 * SUBLANES * LANES
    n_flat = -(-flat.shape[0] // quantum) * quantum
    flat = jnp.pad(flat, (0, n_flat - flat.shape[0])).reshape(8, n_flat // (8 * LANES), LANES)
    red = _sum_parts(_exchange(flat, everyone, True, "rs_small"), "sum_small")
    red = _exchange(red, everyone, False, "ag_small_grads").reshape(-1)
    g_ln_g, g_ln_b, g_conv_w, g_d, g_bglu, g_a, g_bbr, g_bbi, g_cre, g_cim = _unflat(red, small_shapes)

    sbk = n_grp * n_state // ngb
    g_a = g_a.reshape(2, ngb, 2, sbk)
    dar = g_a[:, :, 0].reshape(rg, n_state)
    dai = g_a[:, :, 1].reshape(rg, n_state)
    dbbr_t = jnp.transpose(g_bbr, (2, 0, 1, 3)).reshape(grp, rg, n_state)
    dbbi_t = jnp.transpose(g_bbi, (2, 0, 1, 3)).reshape(grp, rg, n_state)
    z_lre, z_lim, z_ls, z_bre, z_bim = _zoh_bwd(lam_re2, lam_im2, log_step2, b_re_t, b_im_t, dar, dai, dbbr_t, dbbi_t)

    zero = jnp.zeros((1, d_model), F32)
    dm_rows = jnp.stack([
        jnp.stack([jnp.concatenate([dsh0, dsc0, dgt0], axis=1), jnp.concatenate([dshc0, dscc0, dgtc0], axis=1)]),
        jnp.stack([jnp.concatenate([dsh1, dsc1, dgt1], axis=1), jnp.concatenate([dshc1, dscc1, zero], axis=1)]),
    ]).reshape(DEPTH, 2, 3 * d_model)
    dm_all = _exchange(dm_rows, everyone, False, "ag_dmod")
    dm_sh = lax.dynamic_slice_in_dim(dm_all, chip * ws, ws, axis=3)
    g_ada_w, g_ada_b, ds_part = _ada_bwd(c_all, cc2, ada_w, dm_all, dm_sh)
    g_cctx = _cctx_grad(_exchange(ds_part, chips, False, "ag_dsctx"), cc2)

    grads["ada_w"] = g_ada_w
    dlt, nm, nv = _adamw(ada_w.reshape(-1, ws), g_ada_w.reshape(-1, ws), m_ada_w.reshape(-1, ws),
                         v_ada_w.reshape(-1, ws), "adamw_ada_w")
    deltas["ada_w"], new_m["ada_w"], new_v["ada_w"] = dlt.reshape(ada_w.shape), nm.reshape(ada_w.shape), nv.reshape(ada_w.shape)

    def chip_cols(full, rows):
        return lax.dynamic_slice_in_dim(full.reshape(rows, e), chip * es, es, axis=1)

    small_grads = dict(
        c_ctx=g_cctx.reshape(c_ctx.shape), ada_b=g_ada_b.reshape(ada_b.shape), ln_g=g_ln_g, ln_b=g_ln_b,
        conv_w=chip_cols(g_conv_w, 3).reshape(conv_w.shape),
        ssm_lam_re=z_lre.reshape(ssm_lam_re.shape), ssm_lam_im=z_lim.reshape(ssm_lam_im.shape),
        ssm_log_step=z_ls.reshape(ssm_log_step.shape),
        ssm_b_re=jnp.transpose(z_bre.reshape(grp, 2, n_grp, n_state), (1, 2, 3, 0)).reshape(ssm_b_re.shape),
        ssm_b_im=jnp.transpose(z_bim.reshape(grp, 2, n_grp, n_state), (1, 2, 3, 0)).reshape(ssm_b_im.shape),
        ssm_c_re=g_cre.reshape(ssm_c_re.shape), ssm_c_im=g_cim.reshape(ssm_c_im.shape),
        ssm_d=chip_cols(g_d, 1).reshape(ssm_d.shape), ssm_b_glu=chip_cols(g_bglu, 1).reshape(ssm_b_glu.shape))
    small_names = list(small_grads)
    shapes = [weights[n].shape for n in small_names]
    quantum = SUBLANES * LANES

    def pack(parts, fill):
        vec = _flat(parts)
        n_pad = -(-vec.shape[0] // quantum) * quantum
        return jnp.pad(vec, (0, n_pad - vec.shape[0]), constant_values=fill).reshape(-1, LANES)

    dlt, nm, nv = _adamw(pack([weights[n] for n in small_names], 0.0), pack([small_grads[n] for n in small_names], 0.0),
                         pack([mom_m[n] for n in small_names], 0.0), pack([mom_v[n] for n in small_names], 1.0),
                         "adamw_small")
    for n, dv, mv, vv in zip(small_names, _unflat(dlt.reshape(-1), shapes), _unflat(nm.reshape(-1), shapes),
                             _unflat(nv.reshape(-1), shapes)):
        grads[n], deltas[n], new_m[n], new_v[n] = small_grads[n], dv, mv, vv

    loss_total = lax.psum(loss[0, 0], MESH_AXES)
    return (loss_total, grad_x.reshape(x.shape), *[grads[n] for n in names], *[deltas[n] for n in names],
            *[new_m[n] for n in names], *[new_v[n] for n in names])
```

```python
import functools
import math

import jax
import jax.numpy as jnp
from jax import lax
from jax.experimental import pallas as pl
from jax.experimental.pallas import tpu as pltpu

F32 = jnp.float32
BF16 = jnp.bfloat16
LANES = 128
SUBLANES = 8
VMEM_LIMIT = 56 * 1024 * 1024
MESH_AXES = ("x", "y", "c")
HIGHEST = lax.Precision.HIGHEST

GRID_W = 64
LN_EPS = 1e-5
DEPTH = 2
DN_ALPHA = (2 * DEPTH) ** 0.25
ADAM_LR, ADAM_B1, ADAM_B2, ADAM_EPS, ADAM_WD, ADAM_STEP = 0.001, 0.9, 0.999, 1e-08, 0.01, 10
GELU_K = math.sqrt(2.0 / math.pi)
GELU_C = 0.044715


def _params(n_grid_axes):
    return pltpu.CompilerParams(dimension_semantics=("arbitrary",) * n_grid_axes, vmem_limit_bytes=VMEM_LIMIT)


def _dot(a, b):
    return jnp.dot(a, b, preferred_element_type=F32)


def _dot_nt(a, b):
    return lax.dot_general(a, b, (((1,), (1,)), ((), ())), preferred_element_type=F32)


def _dot_tn(a, b):
    return lax.dot_general(a, b, (((0,), (0,)), ((), ())), preferred_element_type=F32)


def _sigmoid(x):
    return 1.0 / (1.0 + jnp.exp(-x))


def _colsum(x):
    return jnp.sum(x, axis=0, keepdims=True)


def _rowmean(x):
    return jnp.mean(x, axis=-1, keepdims=True)


def _acc(ref, first, value):
    @pl.when(first)
    def _():
        ref[...] = value

    @pl.when(jnp.logical_not(first))
    def _():
        ref[...] += value


def _exchange(src, axes, all_to_all, name):
    n_peers = 2 ** len(axes)
    block = tuple(src.shape[1:] if all_to_all else src.shape)

    def body(src_ref, out_ref, send_sems, recv_sems, own_sem):
        pos = {a: lax.axis_index(a) for a in MESH_AXES}

        def index(p):
            return sum(p[a] * (2 ** (len(axes) - 1 - i)) for i, a in enumerate(axes))

        me = index(pos)
        own = pltpu.make_async_copy(src_ref.at[me] if all_to_all else src_ref, out_ref.at[me], own_sem)
        own.start()
        copies = []
        for k in range(1, n_peers):
            peer = dict(pos)
            for i, a in enumerate(axes):
                if (k >> (len(axes) - 1 - i)) & 1:
                    peer[a] = 1 - pos[a]
            cp = pltpu.make_async_remote_copy(
                src_ref=src_ref.at[index(peer)] if all_to_all else src_ref,
                dst_ref=out_ref.at[me],
                send_sem=send_sems.at[k - 1],
                recv_sem=recv_sems.at[k - 1],
                device_id=tuple(peer[a] for a in MESH_AXES),
                device_id_type=pl.DeviceIdType.MESH,
            )
            cp.start()
            copies.append(cp)
        for cp in copies:
            cp.wait()
        own.wait()

    return pl.pallas_call(
        body,
        name=name,
        out_shape=jax.ShapeDtypeStruct((n_peers,) + block, src.dtype),
        in_specs=[pl.BlockSpec(memory_space=pltpu.HBM)],
        out_specs=pl.BlockSpec(memory_space=pltpu.HBM),
        scratch_shapes=[
            pltpu.SemaphoreType.DMA((n_peers - 1,)),
            pltpu.SemaphoreType.DMA((n_peers - 1,)),
            pltpu.SemaphoreType.DMA,
        ],
    )(src)


def _sum_parts(parts, name):
    n_parts, rows, cols = parts.shape
    tr = rows
    while n_parts * tr * cols * 4 > 8 * 1024 * 1024 and tr % 16 == 0:
        tr //= 2

    def body(p_ref, o_ref):
        total = p_ref[0]
        for k in range(1, n_parts):
            total = total + p_ref[k]
        o_ref[...] = total

    return pl.pallas_call(
        body,
        name=name,
        grid=(rows // tr,),
        out_shape=jax.ShapeDtypeStruct((rows, cols), F32),
        in_specs=[pl.BlockSpec((n_parts, tr, cols), lambda i: (0, i, 0))],
        out_specs=pl.BlockSpec((tr, cols), lambda i: (i, 0)),
        compiler_params=_params(1),
    )(parts)


def _adamw(w, g, m, v, name):
    rows, cols = w.shape
    tr = rows
    while tr * cols * 4 > 2 * 1024 * 1024 and tr % 16 == 0:
        tr //= 2

    def body(w_ref, g_ref, m_ref, v_ref, d_ref, nm_ref, nv_ref):
        gv = g_ref[...]
        nm = ADAM_B1 * m_ref[...] + (1.0 - ADAM_B1) * gv
        nv = ADAM_B2 * v_ref[...] + (1.0 - ADAM_B2) * (gv * gv)
        m_hat = nm / (1.0 - ADAM_B1 ** ADAM_STEP)
        v_hat = nv / (1.0 - ADAM_B2 ** ADAM_STEP)
        d_ref[...] = -ADAM_LR * (m_hat / (jnp.sqrt(v_hat) + ADAM_EPS) + ADAM_WD * w_ref[...])
        nm_ref[...] = nm
        nv_ref[...] = nv

    spec = pl.BlockSpec((tr, cols), lambda i: (i, 0))
    shape = jax.ShapeDtypeStruct((rows, cols), F32)
    return pl.pallas_call(
        body, name=name, grid=(rows // tr,), out_shape=(shape, shape, shape),
        in_specs=[spec] * 4, out_specs=(spec, spec, spec), compiler_params=_params(1),
    )(w, g, m, v)


def _ada_rows(c_ref, cc_ref):
    rows = jnp.concatenate([c_ref[...], jnp.broadcast_to(cc_ref[...], c_ref.shape)], axis=0)
    return rows


def _ada_fwd(c_all, c_ctx, w_sh, b_sh):
    n_layers, _, ws = w_sh.shape

    def body(c_ref, cc_ref, w_ref, b_ref, o_ref):
        rows = _ada_rows(c_ref, cc_ref)
        s = rows * _sigmoid(rows)
        for i in range(n_layers):
            o_ref[i] = jnp.dot(s, w_ref[i], precision=HIGHEST, preferred_element_type=F32) + b_ref[i]

    return pl.pallas_call(
        body, name="ada_fwd", out_shape=jax.ShapeDtypeStruct((n_layers, 16, ws), F32),
        compiler_params=pltpu.CompilerParams(vmem_limit_bytes=VMEM_LIMIT),
    )(c_all, c_ctx, w_sh, b_sh)


def _ada_bwd(c_all, c_ctx, w_sh, dm_full, dm_sh):
    n_layers, d_model, ws = w_sh.shape
    n_dev = dm_full.shape[0]
    cols = dm_full.shape[-1]

    def body(c_ref, cc_ref, w_ref, dmf_ref, dms_ref, gw_ref, gb_ref, ds_ref):
        rows = _ada_rows(c_ref, cc_ref)
        s = rows * _sigmoid(rows)
        ds = jnp.zeros((8, d_model), F32)
        for i in range(n_layers):
            ctx_s = dms_ref[0, i, 1:2, :]
            ctx_f = dmf_ref[0, i, 1:2, :]
            ex_f = dmf_ref[0, i, 0:1, :]
            for k in range(1, n_dev):
                ctx_s = ctx_s + dms_ref[k, i, 1:2, :]
                ctx_f = ctx_f + dmf_ref[k, i, 1:2, :]
                ex_f = ex_f + dmf_ref[k, i, 0:1, :]
            gb_ref[i] = ex_f + ctx_f
            r = jnp.concatenate([dms_ref[k, i, 0:1, :] for k in range(n_dev)] + [ctx_s, jnp.zeros((7, ws), F32)], axis=0)
            gw_ref[i] = lax.dot_general(s, r, (((0,), (0,)), ((), ())), precision=HIGHEST, preferred_element_type=F32)
            ds = ds + lax.dot_general(jnp.broadcast_to(ctx_s, (8, ws)), w_ref[i], (((1,), (1,)), ((), ())),
                                      precision=HIGHEST, preferred_element_type=F32)
        ds_ref[...] = ds

    return pl.pallas_call(
        body, name="ada_bwd",
        out_shape=(jax.ShapeDtypeStruct((n_layers, d_model, ws), F32), jax.ShapeDtypeStruct((n_layers, 1, cols), F32),
                   jax.ShapeDtypeStruct((8, d_model), F32)),
        compiler_params=pltpu.CompilerParams(vmem_limit_bytes=VMEM_LIMIT),
    )(c_all, c_ctx, w_sh, dm_full, dm_sh)


def _cctx_grad(ds_parts, c_ctx):
    def body(p_ref, c_ref, o_ref):
        tot = p_ref[0, 0:1, :]
        for k in range(1, ds_parts.shape[0]):
            tot = tot + p_ref[k, 0:1, :]
        cv = c_ref[...]
        sg = _sigmoid(cv)
        o_ref[...] = tot * (sg * (1.0 + cv * (1.0 - sg)))

    return pl.pallas_call(body, name="cctx_grad", out_shape=jax.ShapeDtypeStruct(c_ctx.shape, F32))(ds_parts, c_ctx)


def _zoh_math(lam_re, lam_im, log_step, b_re, b_im):
    dt = jnp.exp(log_step)
    mag = jnp.exp(lam_re * dt)
    ar = mag * jnp.cos(lam_im * dt)
    ai = mag * jnp.sin(lam_im * dt)
    qr, qi = ar - 1.0, ai
    den = lam_re * lam_re + lam_im * lam_im
    fr = (qr * lam_re + qi * lam_im) / den
    fi = (qi * lam_re - qr * lam_im) / den
    bbr = fr[None] * b_re - fi[None] * b_im
    bbi = fr[None] * b_im + fi[None] * b_re
    return ar, ai, bbr, bbi


def _zoh_fwd(lam_re, lam_im, log_step, b_re, b_im):
    rg, n = lam_re.shape

    def body(lr_ref, li_ref, ls_ref, br_ref, bi_ref, pr_ref, pi_ref, qr_ref, qi_ref, bbr_ref, bbi_ref):
        ar, ai, bbr, bbi = _zoh_math(lr_ref[...], li_ref[...], ls_ref[...], br_ref[...], bi_ref[...])
        bbr_ref[...] = bbr
        bbi_ref[...] = bbi

        def powers(base_r, base_i, r_ref, i_ref):
            pr, pi_ = base_r, base_i
            for k in range(8):
                r_ref[k] = pr
                i_ref[k] = pi_
                pr, pi_ = pr * base_r - pi_ * base_i, pr * base_i + pi_ * base_r

        powers(ar, ai, pr_ref, pi_ref)
        powers(pr_ref[7], pi_ref[7], qr_ref, qi_ref)

    pw = jax.ShapeDtypeStruct((8, rg, n), F32)
    bb = jax.ShapeDtypeStruct(b_re.shape, F32)
    return pl.pallas_call(body, name="zoh_fwd", out_shape=(pw, pw, pw, pw, bb, bb))(lam_re, lam_im, log_step, b_re, b_im)


def _zoh_bwd(lam_re, lam_im, log_step, b_re, b_im, dar, dai, dbbr, dbbi):
    def body(lr_ref, li_ref, ls_ref, br_ref, bi_ref, dar_ref, dai_ref, dbr_ref, dbi_ref, *outs):
        _, vjp = jax.vjp(_zoh_math, lr_ref[...], li_ref[...], ls_ref[...], br_ref[...], bi_ref[...])
        grads = vjp((dar_ref[...], dai_ref[...], dbr_ref[...], dbi_ref[...]))
        for o_ref, gval in zip(outs, grads):
            o_ref[...] = gval

    shapes = tuple(jax.ShapeDtypeStruct(a.shape, F32) for a in (lam_re, lam_im, log_step, b_re, b_im))
    return pl.pallas_call(body, name="zoh_bwd", out_shape=shapes)(lam_re, lam_im, log_step, b_re, b_im, dar, dai, dbbr, dbbi)


def _inproj(xin, sc, sh, w, *, lnaff=None, tb, gb_rows=None, name):
    n_tok, d_model = xin.shape
    n_chunks, _, cw = w.shape
    tb = min(tb, n_tok)
    nq = cw // LANES
    has_ln = lnaff is not None

    def body(*refs):
        if has_ln:
            x_ref, g_ref, b_ref, sc_ref, sh_ref, w_ref, p_ref, h_ref = refs
        else:
            x_ref, sc_ref, sh_ref, w_ref, p_ref, h_ref = refs

        @pl.when(pl.program_id(1) == 0)
        def _():
            xv = x_ref[...]
            if has_ln:
                xv = xv * g_ref[...] + b_ref[...]
            h_ref[...] = (xv * (1.0 + sc_ref[...]) + sh_ref[...]).astype(BF16)

        acc = _dot(h_ref[...], w_ref[0])
        if gb_rows is None:
            p_ref[0] = acc.astype(BF16)
        else:
            for q in range(nq):
                p_ref[q] = acc[:, q * LANES:(q + 1) * LANES].astype(BF16)

    vec = pl.BlockSpec((1, d_model), lambda i, j: (0, 0))
    in_specs = [pl.BlockSpec((tb, d_model), lambda i, j: (i, 0))] + ([vec, vec] if has_ln else []) + [
        vec, vec, pl.BlockSpec((1, d_model, cw), lambda i, j: (j, 0, 0))]
    if gb_rows is None:
        p_shape = (n_chunks, n_tok, cw)
        p_spec = pl.BlockSpec((1, tb, cw), lambda i, j: (j, i, 0))
    else:
        total, off = gb_rows
        assert off % tb == 0
        ob = off // tb
        p_shape = (n_chunks * nq, total, LANES)
        p_spec = pl.BlockSpec((nq, tb, LANES), lambda i, j: (j, i + ob, 0))
    args = (xin,) + (tuple(lnaff) if has_ln else ()) + (sc, sh, w)
    return pl.pallas_call(
        body, name=name, grid=(n_tok // tb, n_chunks),
        out_shape=(jax.ShapeDtypeStruct(p_shape, BF16), jax.ShapeDtypeStruct((n_tok, d_model), BF16)),
        in_specs=in_specs, out_specs=(p_spec, pl.BlockSpec((tb, d_model), lambda i, j: (i, 0))),
        compiler_params=_params(2),
    )(*args)


def _shifted(u, rows, width, tb):
    col = rows % width
    um = jnp.where(col == 0, 0.0, pltpu.roll(u, 1, 0))
    up = jnp.where(col == width - 1, 0.0, pltpu.roll(u, tb - 1, 0))
    return um, up


def _slab_width(e):
    return min(512, e // 2)


def _convgate(p, x, gt, conv_w, w_out, ln_g, ln_b, *, grid_mode, tb, name):
    _, n_tok, e = p.shape
    d_model = x.shape[1]
    eh = e // 2
    if not grid_mode:
        tb = n_tok
    tb = min(tb, n_tok)
    nb = n_tok // tb
    hb = tb // GRID_W
    cs = _slab_width(e)

    def body(*refs):
        if grid_mode:
            (bg_ref, cg_ref, v_ref, z_ref, cgp_ref, vp_ref, cgn_ref, vn_ref, x_ref, gt_ref, cw_ref, wo_ref, lg_ref,
             lb_ref, xh_ref, rs_ref, g_ref, yc_ref, f_ref) = refs
        else:
            (bg_ref, cg_ref, v_ref, z_ref, x_ref, gt_ref, cw_ref, wo_ref, lg_ref, lb_ref, xh_ref, rs_ref, g_ref,
             yc_ref, f_ref) = refs
        i = pl.program_id(0)
        rows = lax.broadcasted_iota(jnp.int32, (tb, 1), 0)
        for c0 in range(0, e, cs):
            sl = slice(c0, c0 + cs)
            u = cg_ref[0, :, sl].astype(F32) * v_ref[0, :, sl].astype(F32)
            w = cw_ref[:, sl]
            if grid_mode and c0 >= eh:
                hs = slice(c0 - eh, c0 - eh + cs)
                uprev = cgp_ref[0, :, hs].astype(F32) * vp_ref[0, :, hs].astype(F32)
                unext = cgn_ref[0, :, hs].astype(F32) * vn_ref[0, :, hs].astype(F32)
                uprev = jnp.where(i > 0, uprev, 0.0)
                unext = jnp.where(i < nb - 1, unext, 0.0)
                if tb > GRID_W:
                    um = jnp.concatenate([uprev, u[:tb - GRID_W]], axis=0)
                    up = jnp.concatenate([u[GRID_W:], unext], axis=0)
                else:
                    um, up = uprev, unext
            else:
                um, up = _shifted(u, rows, GRID_W if grid_mode else tb, tb)
            yc = um * w[0:1] + u * w[1:2] + up * w[2:3]
            zf = z_ref[0, :, sl].astype(F32)
            gval = bg_ref[0, :, sl].astype(F32) * yc * (zf * _sigmoid(zf))
            yc_ref[:, sl] = yc.astype(BF16)
            g_ref[:, sl] = gval.astype(BF16)
        f = _dot(g_ref[...], wo_ref[...])
        f_ref[...] = f.astype(BF16)
        r = DN_ALPHA * x_ref[...] + gt_ref[...] * f
        rc = r - _rowmean(r)
        rstd = lax.rsqrt(_rowmean(rc * rc) + LN_EPS)
        xh_ref[...] = rc * rstd
        rs_ref[...] = rstd

    def chunk(k):
        return pl.BlockSpec((1, tb, e), lambda i: (k, i, 0))

    n_hrows = n_tok // GRID_W

    def halo_prev(k):
        return pl.BlockSpec((1, GRID_W, eh), lambda i: (k, jnp.maximum(i * hb - 1, 0), 1))

    def halo_next(k):
        return pl.BlockSpec((1, GRID_W, eh), lambda i: (k, jnp.minimum((i + 1) * hb, n_hrows - 1), 1))

    vec = pl.BlockSpec((1, d_model), lambda i: (0, 0))
    tok = pl.BlockSpec((tb, d_model), lambda i: (i, 0))
    wide = pl.BlockSpec((tb, e), lambda i: (i, 0))
    in_specs = [chunk(0), chunk(1), chunk(2), chunk(3)]
    args = [p, p, p, p]
    if grid_mode:
        in_specs += [halo_prev(1), halo_prev(2), halo_next(1), halo_next(2)]
        args += [p, p, p, p]
    in_specs += [tok, vec, pl.BlockSpec((3, e), lambda i: (0, 0)), pl.BlockSpec((e, d_model), lambda i: (0, 0)), vec, vec]
    args += [x, gt, conv_w, w_out, ln_g, ln_b]
    return pl.pallas_call(
        body, name=name, grid=(nb,),
        out_shape=(jax.ShapeDtypeStruct((n_tok, d_model), F32), jax.ShapeDtypeStruct((n_tok, 1), F32),
                   jax.ShapeDtypeStruct((n_tok, e), BF16), jax.ShapeDtypeStruct((n_tok, e), BF16),
                   jax.ShapeDtypeStruct((n_tok, d_model), BF16)),
        in_specs=in_specs, out_specs=(tok, pl.BlockSpec((tb, 1), lambda i: (i, 0)), wide, wide, tok),
        compiler_params=_params(1),
    )(*args)


def _scan_block(buf_ref, tab_ref, cr, ci, *, reverse, tb, sb):
    n_slabs = tb // SUBLANES

    def slab(s, carry):
        cr, ci = carry
        idx = (n_slabs - 1 - s) if reverse else s
        r0 = pl.multiple_of(idx * SUBLANES, SUBLANES)
        xr = buf_ref[pl.ds(r0, SUBLANES), 0:sb]
        xi = buf_ref[pl.ds(r0, SUBLANES), sb:2 * sb]
        for k, step in enumerate((1, 2, 4)):
            ar = tab_ref[2 * k]
            ai = tab_ref[2 * k + 1]
            shift = (SUBLANES - step) if reverse else step
            rr = pltpu.roll(xr, shift, 0)
            ri = pltpu.roll(xi, shift, 0)
            xr, xi = xr + ar * rr - ai * ri, xi + ar * ri + ai * rr
        pr = tab_ref[6]
        pi_ = tab_ref[7]
        xr, xi = xr + pr * cr - pi_ * ci, xi + pr * ci + pi_ * cr
        buf_ref[pl.ds(r0, SUBLANES), 0:sb] = xr
        buf_ref[pl.ds(r0, SUBLANES), sb:2 * sb] = xi
        last = 0 if reverse else SUBLANES - 1
        return (jnp.broadcast_to(xr[last:last + 1, :], (SUBLANES, sb)),
                jnp.broadcast_to(xi[last:last + 1, :], (SUBLANES, sb)))

    return lax.fori_loop(0, n_slabs, slab, (cr, ci))


CHUNK = SUBLANES


def _group_mask():
    r = lax.broadcasted_iota(jnp.int32, (LANES, LANES), 0)
    c = lax.broadcasted_iota(jnp.int32, (LANES, LANES), 1)
    return r // 16 == c // 16


def _s5_ops(bre, bim, cre, cim, wr, wi, *, reverse, name):
    ngb, _, sb = bre.shape
    n_rows = CHUNK * LANES

    def body(bre_ref, bim_ref, cre_ref, cim_ref, wr_ref, wi_ref, t_ref, bp_ref, cp_ref):
        b_re, b_im, c_re, c_im = bre_ref[0], bim_ref[0], cre_ref[0], cim_ref[0]
        mask = _group_mask()
        er, ei = [], []
        for tau in range(CHUNK + 1):
            w_r, w_i = wr_ref[0, tau:tau + 1, :], wi_ref[0, tau:tau + 1, :]
            er.append(c_re * w_r - c_im * w_i)
            ei.append(c_re * w_i + c_im * w_r)
        kt = []
        for tau in range(CHUNK):
            k = (lax.dot_general(b_re, er[tau], (((1,), (1,)), ((), ())), precision=HIGHEST, preferred_element_type=F32)
                 - lax.dot_general(b_im, ei[tau], (((1,), (1,)), ((), ())), precision=HIGHEST, preferred_element_type=F32))
            kt.append(jnp.where(mask, k, 0.0).astype(BF16))
        zero = jnp.zeros((LANES, LANES), BF16)
        for i in range(CHUNK):
            rows = slice(i * LANES, (i + 1) * LANES)
            for j in range(CHUNK):
                lag = (i - j) if reverse else (j - i)
                t_ref[0, rows, j * LANES:(j + 1) * LANES] = kt[lag] if lag >= 0 else zero
            tau = i if reverse else CHUNK - 1 - i
            w_r, w_i = wr_ref[0, tau:tau + 1, :], wi_ref[0, tau:tau + 1, :]
            bp_ref[0, rows, 0:sb] = (b_re * w_r - b_im * w_i).astype(BF16)
            bp_ref[0, rows, sb:2 * sb] = (b_re * w_i + b_im * w_r).astype(BF16)
            tau = CHUNK - i if reverse else i + 1
            cp_ref[0, rows, 0:sb] = er[tau].astype(BF16)
            cp_ref[0, rows, sb:2 * sb] = (-ei[tau]).astype(BF16)

    mat = pl.BlockSpec((1, LANES, sb), lambda g: (g, 0, 0))
    pw = pl.BlockSpec((1, CHUNK + 1, sb), lambda g: (g, 0, 0))
    return pl.pallas_call(
        body, name=name, grid=(ngb,),
        out_shape=(jax.ShapeDtypeStruct((ngb, n_rows, n_rows), BF16), jax.ShapeDtypeStruct((ngb, n_rows, 2 * sb), BF16),
                   jax.ShapeDtypeStruct((ngb, n_rows, 2 * sb), BF16)),
        in_specs=[mat, mat, mat, mat, pw, pw],
        out_specs=(pl.BlockSpec((1, n_rows, n_rows), lambda g: (g, 0, 0)),
                   pl.BlockSpec((1, n_rows, 2 * sb), lambda g: (g, 0, 0)),
                   pl.BlockSpec((1, n_rows, 2 * sb), lambda g: (g, 0, 0))),
        compiler_params=_params(1),
    )(bre, bim, cre, cim, wr, wi)


def _s5_ops_bwd(bre, bim, cre, cim, wr, wi, ar, ai, dt, dbp, dcp, da8, *, reverse, name):
    ngb, _, sb = bre.shape
    n_rows = CHUNK * LANES

    def dot_hi(a, b, dims):
        return lax.dot_general(a, b, (dims, ((), ())), precision=HIGHEST, preferred_element_type=F32)

    def body(bre_ref, bim_ref, cre_ref, cim_ref, wr_ref, wi_ref, ar_ref, ai_ref, dt_ref, dbp_ref, dcp_ref, da8_ref,
             dbre_ref, dbim_ref, dcre_ref, dcim_ref, da_ref):
        b_re, b_im, c_re, c_im = bre_ref[0], bim_ref[0], cre_ref[0], cim_ref[0]
        mask = _group_mask()
        w_r = [wr_ref[0, tau:tau + 1, :] for tau in range(CHUNK + 1)]
        w_i = [wi_ref[0, tau:tau + 1, :] for tau in range(CHUNK + 1)]
        der = [jnp.zeros((LANES, sb), F32) for _ in range(CHUNK + 1)]
        dei = [jnp.zeros((LANES, sb), F32) for _ in range(CHUNK + 1)]
        dwr = [jnp.zeros((1, sb), F32) for _ in range(CHUNK + 1)]
        dwi = [jnp.zeros((1, sb), F32) for _ in range(CHUNK + 1)]
        dwr[CHUNK] = da8_ref[0, :, 0:sb]
        dwi[CHUNK] = da8_ref[0, :, sb:2 * sb]
        d_bre = jnp.zeros((LANES, sb), F32)
        d_bim = jnp.zeros((LANES, sb), F32)
        dkt = [jnp.zeros((LANES, LANES), F32) for _ in range(CHUNK)]
        for i in range(CHUNK):
            rows = slice(i * LANES, (i + 1) * LANES)
            for j in range(CHUNK):
                lag = (i - j) if reverse else (j - i)
                if lag >= 0:
                    dkt[lag] = dkt[lag] + dt_ref[0, rows, j * LANES:(j + 1) * LANES]
            tau = i if reverse else CHUNK - 1 - i
            g_r, g_i = dbp_ref[0, rows, 0:sb], dbp_ref[0, rows, sb:2 * sb]
            d_bre = d_bre + g_r * w_r[tau] + g_i * w_i[tau]
            d_bim = d_bim - g_r * w_i[tau] + g_i * w_r[tau]
            dwr[tau] = dwr[tau] + _colsum(g_r * b_re + g_i * b_im)
            dwi[tau] = dwi[tau] + _colsum(g_i * b_re - g_r * b_im)
            tau = CHUNK - i if reverse else i + 1
            der[tau] = der[tau] + dcp_ref[0, rows, 0:sb]
            dei[tau] = dei[tau] - dcp_ref[0, rows, sb:2 * sb]
        d_cre = jnp.zeros((LANES, sb), F32)
        d_cim = jnp.zeros((LANES, sb), F32)
        for tau in range(CHUNK + 1):
            if tau < CHUNK:
                e_r = c_re * w_r[tau] - c_im * w_i[tau]
                e_i = c_re * w_i[tau] + c_im * w_r[tau]
                dk = jnp.where(mask, dkt[tau], 0.0)
                d_bre = d_bre + dot_hi(dk, e_r, ((1,), (0,)))
                d_bim = d_bim - dot_hi(dk, e_i, ((1,), (0,)))
                der[tau] = der[tau] + dot_hi(dk, b_re, ((0,), (0,)))
                dei[tau] = dei[tau] - dot_hi(dk, b_im, ((0,), (0,)))
            d_cre = d_cre + der[tau] * w_r[tau] + dei[tau] * w_i[tau]
            d_cim = d_cim - der[tau] * w_i[tau] + dei[tau] * w_r[tau]
            dwr[tau] = dwr[tau] + _colsum(der[tau] * c_re + dei[tau] * c_im)
            dwi[tau] = dwi[tau] + _colsum(dei[tau] * c_re - der[tau] * c_im)
        a_r, a_i = ar_ref[0], ai_ref[0]
        d_ar = jnp.zeros((1, sb), F32)
        d_ai = jnp.zeros((1, sb), F32)
        for tau in range(CHUNK, 0, -1):
            d_ar = d_ar + dwr[tau] * w_r[tau - 1] + dwi[tau] * w_i[tau - 1]
            d_ai = d_ai - dwr[tau] * w_i[tau - 1] + dwi[tau] * w_r[tau - 1]
            dwr[tau - 1], dwi[tau - 1] = (dwr[tau - 1] + dwr[tau] * a_r + dwi[tau] * a_i,
                                          dwi[tau - 1] - dwr[tau] * a_i + dwi[tau] * a_r)
        dbre_ref[0] = d_bre
        dbim_ref[0] = d_bim
        dcre_ref[0] = d_cre
        dcim_ref[0] = d_cim
        da_ref[0, :, 0:sb] = d_ar
        da_ref[0, :, sb:2 * sb] = d_ai

    mat = pl.BlockSpec((1, LANES, sb), lambda g: (g, 0, 0))
    pw = pl.BlockSpec((1, CHUNK + 1, sb), lambda g: (g, 0, 0))
    one = pl.BlockSpec((1, 1, sb), lambda g: (g, 0, 0))
    two = pl.BlockSpec((1, 1, 2 * sb), lambda g: (g, 0, 0))
    big = pl.BlockSpec((1, n_rows, n_rows), lambda g: (g, 0, 0))
    big2 = pl.BlockSpec((1, n_rows, 2 * sb), lambda g: (g, 0, 0))
    mshape = jax.ShapeDtypeStruct((ngb, LANES, sb), F32)
    return pl.pallas_call(
        body, name=name, grid=(ngb,),
        out_shape=(mshape, mshape, mshape, mshape, jax.ShapeDtypeStruct((ngb, 1, 2 * sb), F32)),
        in_specs=[mat, mat, mat, mat, pw, pw, one, one, big, big2, big2, two],
        out_specs=(mat, mat, mat, mat, two),
        compiler_params=_params(1),
    )(bre, bim, cre, cim, wr, wi, ar, ai, dt, dbp, dcp, da8)


def _shift_rows(xv, edge, rows, n_rows, down):
    if down:
        return jnp.where(rows == 0, edge, pltpu.roll(xv, 1, 0))
    return jnp.where(rows == n_rows - 1, edge, pltpu.roll(xv, n_rows - 1, 0))


def _s5_fwd(u8, t_op, bp, cp, tab, *, reverse, n_lat, tb, name):
    ngb, n_rows_cat, width = u8.shape
    sb = bp.shape[2] // 2
    rb = tb // CHUNK
    nb_lat = n_lat // tb
    steps = nb_lat + 1
    nb_cat = n_rows_cat // rb

    def blk(i):
        return (nb_cat - 1 - i) if reverse else i

    def body(u_ref, t_ref, b_ref, c_ref, tab_ref, y_ref, hc_ref, h_scr, carry_scr):
        i = pl.program_id(1)

        @pl.when(i == 0)
        def _():
            carry_scr[...] = jnp.zeros_like(carry_scr)

        hc_ref[0, 0] = carry_scr[...]
        enter = carry_scr[0:1, :]
        uv = u_ref[0]
        h_scr[...] = _dot(uv, b_ref[0])
        cr, ci = _scan_block(h_scr, tab_ref.at[0], carry_scr[:, 0:sb], carry_scr[:, sb:2 * sb],
                             reverse=reverse, tb=rb, sb=sb)
        carry_scr[:, 0:sb] = cr
        carry_scr[:, sb:2 * sb] = ci
        rows = lax.broadcasted_iota(jnp.int32, (rb, 1), 0)
        hprev = _shift_rows(h_scr[...], enter, rows, rb, down=not reverse)
        y_ref[0] = (_dot(uv, t_ref[0]) + _dot_nt(hprev.astype(BF16), c_ref[0])).astype(BF16)

    op = pl.BlockSpec((1, width, width), lambda g, i: (g, 0, 0))
    op2 = pl.BlockSpec((1, width, 2 * sb), lambda g, i: (g, 0, 0))
    return pl.pallas_call(
        body, name=name, grid=(ngb, steps),
        out_shape=(jax.ShapeDtypeStruct((ngb, n_lat // CHUNK, width), BF16),
                   jax.ShapeDtypeStruct((ngb, steps, SUBLANES, 2 * sb), F32)),
        in_specs=[pl.BlockSpec((1, rb, width), lambda g, i: (g, blk(i), 0)), op, op2, op2,
                  pl.BlockSpec((1, 8, SUBLANES, sb), lambda g, i: (g, 0, 0, 0))],
        out_specs=(pl.BlockSpec((1, rb, width), lambda g, i: (g, jnp.clip(blk(i) - 1, 0, nb_lat - 1), 0)),
                   pl.BlockSpec((1, 1, SUBLANES, 2 * sb), lambda g, i: (g, i, 0, 0))),
        scratch_shapes=[pltpu.VMEM((rb, 2 * sb), F32), pltpu.VMEM((SUBLANES, 2 * sb), F32)],
        compiler_params=_params(2),
    )(u8, t_op, bp, cp, tab)


def _s5_bwd(u8, dy8, hc, t_op, bp, cp, tab, tab_adj, *, reverse, n_lat, tb, name):
    ngb, n_rows_cat, width = u8.shape
    sb = bp.shape[2] // 2
    rb = tb // CHUNK
    nb_lat = n_lat // tb
    steps = nb_lat + 1
    nb_cat = n_rows_cat // rb

    def fwd_step(i):
        return steps - 1 - i

    def blk(i):
        s = fwd_step(i)
        return (nb_cat - 1 - s) if reverse else s

    def body(u_ref, dy_ref, hc_ref, t_ref, b_ref, c_ref, tab_ref, taba_ref, du_ref, dt_ref, db_ref, dc_ref, da_ref,
             h_scr, lam_scr, lcarry_scr, gedge_scr, da_scr):
        i = pl.program_id(1)
        first = i == 0

        @pl.when(first)
        def _():
            lcarry_scr[...] = jnp.zeros_like(lcarry_scr)
            gedge_scr[...] = jnp.zeros_like(gedge_scr)
            da_scr[...] = jnp.zeros_like(da_scr)

        rows = lax.broadcasted_iota(jnp.int32, (rb, 1), 0)
        uv = u_ref[0]
        h_scr[...] = _dot(uv, b_ref[0])
        _scan_block(h_scr, tab_ref.at[0], hc_ref[0, 0, :, 0:sb], hc_ref[0, 0, :, sb:2 * sb], reverse=reverse, tb=rb, sb=sb)
        hprev = _shift_rows(h_scr[...], hc_ref[0, 0, 0:1, :], rows, rb, down=not reverse)

        b = blk(i)
        is_lat = jnp.logical_and(b >= 1, b <= nb_lat)
        dyv = jnp.where(is_lat, dy_ref[0], jnp.zeros_like(dy_ref[0]))
        gy = _dot(dyv, c_ref[0])
        edge = gy[rb - 1:rb, :] if reverse else gy[0:1, :]
        lam_scr[...] = _shift_rows(gy, gedge_scr[...], rows, rb, down=reverse)
        gedge_scr[...] = edge
        lr, li = _scan_block(lam_scr, taba_ref.at[0], lcarry_scr[:, 0:sb], lcarry_scr[:, sb:2 * sb],
                             reverse=not reverse, tb=rb, sb=sb)
        lcarry_scr[:, 0:sb] = lr
        lcarry_scr[:, sb:2 * sb] = li

        lam = lam_scr[...]
        lam_bf = lam.astype(BF16)
        du_ref[0] = (_dot_nt(dyv, t_ref[0]) + _dot_nt(lam_bf, b_ref[0])).astype(BF16)
        _acc(dt_ref.at[0], first, _dot_tn(uv, dyv))
        _acc(db_ref.at[0], first, _dot_tn(uv, lam_bf))
        _acc(dc_ref.at[0], first, _dot_tn(dyv, hprev.astype(BF16)))
        lam_r, lam_i = lam[:, 0:sb], lam[:, sb:2 * sb]
        hp_r, hp_i = hprev[:, 0:sb], hprev[:, sb:2 * sb]
        da_scr[:, 0:sb] += _colsum(lam_r * hp_r + lam_i * hp_i)
        da_scr[:, sb:2 * sb] += _colsum(lam_i * hp_r - lam_r * hp_i)

        @pl.when(i == steps - 1)
        def _():
            da_ref[0] = da_scr[...]

    op = pl.BlockSpec((1, width, width), lambda g, i: (g, 0, 0))
    op2 = pl.BlockSpec((1, width, 2 * sb), lambda g, i: (g, 0, 0))
    tabs = pl.BlockSpec((1, 8, SUBLANES, sb), lambda g, i: (g, 0, 0, 0))
    return pl.pallas_call(
        body, name=name, grid=(ngb, steps),
        out_shape=(jax.ShapeDtypeStruct((ngb, n_rows_cat, width), BF16),
                   jax.ShapeDtypeStruct((ngb, width, width), F32),
                   jax.ShapeDtypeStruct((ngb, width, 2 * sb), F32),
                   jax.ShapeDtypeStruct((ngb, width, 2 * sb), F32),
                   jax.ShapeDtypeStruct((ngb, 1, 2 * sb), F32)),
        in_specs=[pl.BlockSpec((1, rb, width), lambda g, i: (g, blk(i), 0)),
                  pl.BlockSpec((1, rb, width), lambda g, i: (g, jnp.clip(blk(i) - 1, 0, nb_lat - 1), 0)),
                  pl.BlockSpec((1, 1, SUBLANES, 2 * sb), lambda g, i: (g, fwd_step(i), 0, 0)),
                  op, op2, op2, tabs, tabs],
        out_specs=(pl.BlockSpec((1, rb, width), lambda g, i: (g, blk(i), 0)), op, op2, op2,
                   pl.BlockSpec((1, 1, 2 * sb), lambda g, i: (g, 0, 0))),
        scratch_shapes=[pltpu.VMEM((rb, 2 * sb), F32), pltpu.VMEM((rb, 2 * sb), F32),
                        pltpu.VMEM((SUBLANES, 2 * sb), F32), pltpu.VMEM((1, 2 * sb), F32), pltpu.VMEM((1, 2 * sb), F32)],
        compiler_params=_params(2),
    )(u8, dy8, hc, t_op, bp, cp, tab, tab_adj)


def _glu_loss(ucat, yf, yb, z, xhat0, ln0, gt, d_vec, w_glu, b_glu, w_out, ln1, target, *, row_off, tb, name):
    ngb = ucat.shape[0]
    n_tok, d_model = xhat0.shape
    e = ngb * LANES
    tb = min(tb, n_tok)
    ob = row_off // tb
    nz = z.shape[0]

    def body(u_ref, yf_ref, yb_ref, z_ref, xh0_ref, g0_ref, b0_ref, gt_ref, d_ref, wg_ref, bg_ref, wo_ref, g1_ref,
             b1_ref, t_ref, loss_ref, dxr_ref, do_ref, gz_ref, gg_ref, dq_ref, dz_ref, dyl_ref, dg1_ref, db1_ref,
             dgt_ref, dbg_ref, dd_ref, loss_scr, yl_scr, th_scr, s_scr, dg_scr):
        i = pl.program_id(0)
        first = i == 0
        zw = e // nz
        cs = min(512, zw)

        def z_slab(c0):
            return z_ref[c0 // zw, :, c0 % zw:c0 % zw + cs].astype(F32)

        for q in range(ngb):
            sl = slice(q * LANES, (q + 1) * LANES)
            yl = d_ref[:, sl] * u_ref[q].astype(F32) + yf_ref[q].astype(F32) + yb_ref[q].astype(F32)
            th = jnp.tanh(GELU_K * (yl + GELU_C * yl * yl * yl))
            yl_scr[:, sl] = yl
            th_scr[:, sl] = th
            gg_ref[:, sl] = (0.5 * yl * (1.0 + th)).astype(BF16)
        s_scr[...] = _sigmoid(_dot(gg_ref[...], wg_ref[...]) + bg_ref[...])
        for c0 in range(0, e, cs):
            sl = slice(c0, c0 + cs)
            zf = z_slab(c0)
            g2 = 0.5 * yl_scr[:, sl] * (1.0 + th_scr[:, sl]) * s_scr[:, sl]
            gz_ref[:, sl] = (g2 * (zf * _sigmoid(zf))).astype(BF16)
        o = _dot(gz_ref[...], wo_ref[...])
        x1 = xh0_ref[...] * g0_ref[...] + b0_ref[...]
        r = DN_ALPHA * x1 + gt_ref[...] * o
        rc = r - _rowmean(r)
        rstd = lax.rsqrt(_rowmean(rc * rc) + LN_EPS)
        xh = rc * rstd
        err = xh * g1_ref[...] + b1_ref[...] - t_ref[...]
        _acc(loss_scr, first, _colsum(err * err))
        dy = err * (1.0 / d_model)
        _acc(dg1_ref, first, _colsum(dy * xh))
        _acc(db1_ref, first, _colsum(dy))
        dxh = dy * g1_ref[...]
        dr = rstd * (dxh - _rowmean(dxh) - xh * _rowmean(dxh * xh))
        dxr_ref[...] = DN_ALPHA * dr
        _acc(dgt_ref, first, _colsum(dr * o))
        do_bf = (dr * gt_ref[...]).astype(BF16)
        do_ref[...] = do_bf
        dg_scr[...] = _dot_nt(do_bf, wo_ref[...])
        for c0 in range(0, e, cs):
            sl = slice(c0, c0 + cs)
            zf = z_slab(c0)
            sz = _sigmoid(zf)
            g = 0.5 * yl_scr[:, sl] * (1.0 + th_scr[:, sl])
            s = s_scr[:, sl]
            dgz = dg_scr[:, sl]
            dg2 = dgz * (zf * sz)
            dz_ref[:, sl] = (dgz * (g * s) * (sz * (1.0 + zf * (1.0 - sz)))).astype(BF16)
            dq = dg2 * g * s * (1.0 - s)
            _acc(dbg_ref.at[:, sl], first, _colsum(dq))
            dq_ref[:, sl] = dq.astype(BF16)
            dg_scr[:, sl] = dg2 * s
        dg_scr[...] += _dot_nt(dq_ref[...], wg_ref[...])
        for q in range(ngb):
            sl = slice(q * LANES, (q + 1) * LANES)
            yl = yl_scr[:, sl]
            th = th_scr[:, sl]
            dgelu = 0.5 * (1.0 + th) + 0.5 * yl * (1.0 - th * th) * (GELU_K * (1.0 + 3.0 * GELU_C * yl * yl))
            dyl = dg_scr[:, sl] * dgelu
            _acc(dd_ref.at[:, sl], first, _colsum(dyl * u_ref[q].astype(F32)))
            dyl_ref[q] = dyl.astype(BF16)

        @pl.when(i == pl.num_programs(0) - 1)
        def _():
            loss_ref[...] = (0.5 / d_model) * jnp.sum(loss_scr[...], axis=1, keepdims=True)

    vec = pl.BlockSpec((1, d_model), lambda i: (0, 0))
    evec = pl.BlockSpec((1, e), lambda i: (0, 0))
    tok = pl.BlockSpec((tb, d_model), lambda i: (i, 0))
    wide = pl.BlockSpec((tb, e), lambda i: (i, 0))
    gblk = pl.BlockSpec((ngb, tb, LANES), lambda i: (0, i, 0))
    once = dict(pipeline_mode=pl.Buffered(1))
    tok_f = jax.ShapeDtypeStruct((n_tok, d_model), F32)
    tok_b = jax.ShapeDtypeStruct((n_tok, d_model), BF16)
    wide_b = jax.ShapeDtypeStruct((n_tok, e), BF16)
    vec_f = jax.ShapeDtypeStruct((1, d_model), F32)
    evec_f = jax.ShapeDtypeStruct((1, e), F32)
    return pl.pallas_call(
        body, name=name, grid=(n_tok // tb,),
        out_shape=(jax.ShapeDtypeStruct((1, 1), F32), tok_f, tok_b, wide_b, wide_b, wide_b, wide_b,
                   jax.ShapeDtypeStruct((ngb, n_tok, LANES), BF16), vec_f, vec_f, vec_f, evec_f, evec_f),
        in_specs=[pl.BlockSpec((ngb, tb, LANES), lambda i: (0, i + ob, 0)), gblk, gblk,
                  pl.BlockSpec((nz, tb, e // nz), lambda i: (0, i, 0)), tok, vec, vec, vec, evec,
                  pl.BlockSpec((e, e), lambda i: (0, 0), **once), evec,
                  pl.BlockSpec((e, d_model), lambda i: (0, 0), **once), vec, vec, tok],
        out_specs=(pl.BlockSpec((1, 1), lambda i: (0, 0)), tok, tok, wide, wide, wide, wide, gblk, vec, vec, vec, evec,
                   evec),
        scratch_shapes=[pltpu.VMEM((1, d_model), F32)] + [pltpu.VMEM((tb, e), F32)] * 4,
        compiler_params=_params(1),
    )(ucat, yf, yb, z, xhat0, ln0[0], ln0[1], gt, d_vec, w_glu, b_glu, w_out, ln1[0], ln1[1], target)


def _ssm_inbwd(duf, dub, w, xhat, rstd, ln, sc, gt_prev, f_prev, *, lat, row_f, row_b, tb, name):
    ngb = duf.shape[0]
    e = ngb * LANES
    n_tok, d_model = xhat.shape
    tb = min(tb, n_tok)
    obf, obb = row_f // tb, row_b // tb
    has_lat = lat is not None
    n_w = w.shape[0] if has_lat else w.shape[0] // 2

    def body(*refs):
        if has_lat:
            (duf_ref, dub_ref, dyl_ref, dz_ref, d_ref, dxr_ref, w_ref, xh_ref, rs_ref, g_ref, b_ref, sc_ref, gt_ref,
             f_ref, dp_ref, dr_ref, df_ref, dsc_ref, dsh_ref, dg_ref, db_ref, dgt_ref) = refs
        else:
            (duf_ref, dub_ref, w_ref, xh_ref, rs_ref, g_ref, b_ref, sc_ref, gt_ref, f_ref, dp_ref, dr_ref, df_ref,
             dsc_ref, dsh_ref, dg_ref, db_ref, dgt_ref) = refs
        first = pl.program_id(0) == 0
        du = (jnp.concatenate([duf_ref[q] for q in range(ngb)], axis=1).astype(F32)
              + jnp.concatenate([dub_ref[q] for q in range(ngb)], axis=1).astype(F32))
        if has_lat:
            du = du + d_ref[...] * jnp.concatenate([dyl_ref[q] for q in range(ngb)], axis=1).astype(F32)
            dp_ref[:, e:2 * e] = dz_ref[...]
        else:
            dp_ref[:, e:2 * e] = jnp.zeros((tb, e), BF16)
        dp_ref[:, 0:e] = du.astype(BF16)
        dh = jnp.zeros((tb, d_model), F32)
        for j in range(n_w):
            dh = dh + _dot_nt(dp_ref[:, j * d_model:(j + 1) * d_model], w_ref[j])
        xh = xh_ref[...]
        x1 = xh * g_ref[...] + b_ref[...]
        dx1 = dh * (1.0 + sc_ref[...])
        if has_lat:
            dx1 = dx1 + dxr_ref[...]
        _acc(dsc_ref, first, _colsum(dh * x1))
        _acc(dsh_ref, first, _colsum(dh))
        _acc(dg_ref, first, _colsum(dx1 * xh))
        _acc(db_ref, first, _colsum(dx1))
        dxh = dx1 * g_ref[...]
        dr = rs_ref[...] * (dxh - _rowmean(dxh) - xh * _rowmean(dxh * xh))
        dr_ref[...] = dr
        df_ref[...] = (dr * gt_ref[...]).astype(BF16)
        _acc(dgt_ref, first, _colsum(dr * f_ref[...].astype(F32)))

    vec = pl.BlockSpec((1, d_model), lambda i: (0, 0))
    tok = pl.BlockSpec((tb, d_model), lambda i: (i, 0))
    gblk = pl.BlockSpec((ngb, tb, LANES), lambda i: (0, i, 0))
    in_specs = [pl.BlockSpec((ngb, tb, LANES), lambda i: (0, i + obf, 0)),
                pl.BlockSpec((ngb, tb, LANES), lambda i: (0, i + obb, 0))]
    args = [duf, dub]
    if has_lat:
        in_specs += [gblk, pl.BlockSpec((tb, e), lambda i: (i, 0)), pl.BlockSpec((1, e), lambda i: (0, 0)), tok]
        args += list(lat)
    in_specs += [pl.BlockSpec(w.shape, lambda i: (0, 0, 0)), tok, pl.BlockSpec((tb, 1), lambda i: (i, 0)), vec, vec, vec,
                 vec, tok]
    args += [w, xhat, rstd, ln[0], ln[1], sc, gt_prev, f_prev]
    vec_f = jax.ShapeDtypeStruct((1, d_model), F32)
    return pl.pallas_call(
        body, name=name, grid=(n_tok // tb,),
        out_shape=(jax.ShapeDtypeStruct((n_tok, 2 * e), BF16), jax.ShapeDtypeStruct((n_tok, d_model), F32),
                   jax.ShapeDtypeStruct((n_tok, d_model), BF16), vec_f, vec_f, vec_f, vec_f, vec_f),
        in_specs=in_specs,
        out_specs=(pl.BlockSpec((tb, 2 * e), lambda i: (i, 0)), tok, tok, vec, vec, vec, vec, vec),
        compiler_params=_params(1),
    )(*args)


def _conv_bwd_a(df, w_out, p, yc, *, tb, name):
    _, n_tok, e = p.shape
    d_model = df.shape[1]
    tb = min(tb, n_tok)
    cs = _slab_width(e)

    def body(df_ref, wo_ref, bg_ref, z_ref, yc_ref, dbg_ref, dz_ref, dyc_ref):
        dfv = df_ref[...]
        for c0 in range(0, e, cs):
            sl = slice(c0, c0 + cs)
            dgv = _dot_nt(dfv, wo_ref[sl, :])
            zf = z_ref[0, :, sl].astype(F32)
            sz = _sigmoid(zf)
            silu_z = zf * sz
            bg = bg_ref[0, :, sl].astype(F32)
            yc = yc_ref[:, sl].astype(F32)
            dbg_ref[:, sl] = (dgv * yc * silu_z).astype(BF16)
            dyc_ref[:, sl] = (dgv * bg * silu_z).astype(BF16)
            dz_ref[:, sl] = (dgv * bg * yc * (sz * (1.0 + zf * (1.0 - sz)))).astype(BF16)

    wide = pl.BlockSpec((tb, e), lambda i: (i, 0))
    shape = jax.ShapeDtypeStruct((n_tok, e), BF16)
    return pl.pallas_call(
        body, name=name, grid=(n_tok // tb,), out_shape=(shape, shape, shape),
        in_specs=[pl.BlockSpec((tb, d_model), lambda i: (i, 0)), pl.BlockSpec((e, d_model), lambda i: (0, 0)),
                  pl.BlockSpec((1, tb, e), lambda i: (0, i, 0)), pl.BlockSpec((1, tb, e), lambda i: (3, i, 0)), wide],
        out_specs=(wide, wide, wide), compiler_params=_params(1),
    )(df, w_out, p, p, yc)


def _conv_bwd_b(dyc, p, dbg, dz, conv_w, *, grid_mode, tb, name):
    _, n_tok, e = p.shape
    eh = e // 2
    if not grid_mode:
        tb = n_tok
    tb = min(tb, n_tok)
    nb = n_tok // tb
    hb = tb // GRID_W
    cs = _slab_width(e)

    def body(*refs):
        if grid_mode:
            dyc_ref, dycp_ref, dycn_ref, cg_ref, v_ref, dbg_ref, dz_ref, cw_ref, dp_ref, dcw_ref = refs
        else:
            dyc_ref, cg_ref, v_ref, dbg_ref, dz_ref, cw_ref, dp_ref, dcw_ref = refs
        i = pl.program_id(0)
        first = i == 0
        rows = lax.broadcasted_iota(jnp.int32, (tb, 1), 0)
        dp_ref[0] = dbg_ref[...]
        dp_ref[3] = dz_ref[...]
        for c0 in range(0, e, cs):
            sl = slice(c0, c0 + cs)
            dyc = dyc_ref[:, sl].astype(F32)
            w = cw_ref[:, sl]
            if grid_mode and c0 >= eh:
                hs = slice(c0 - eh, c0 - eh + cs)
                dprev = jnp.where(i > 0, dycp_ref[:, hs].astype(F32), 0.0)
                dnext = jnp.where(i < nb - 1, dycn_ref[:, hs].astype(F32), 0.0)
                if tb > GRID_W:
                    dm = jnp.concatenate([dprev, dyc[:tb - GRID_W]], axis=0)
                    dpl = jnp.concatenate([dyc[GRID_W:], dnext], axis=0)
                else:
                    dm, dpl = dprev, dnext
            else:
                dm, dpl = _shifted(dyc, rows, GRID_W if grid_mode else tb, tb)
            cg = cg_ref[0, :, sl].astype(F32)
            v = v_ref[0, :, sl].astype(F32)
            u = cg * v
            du = w[0:1] * dpl + w[1:2] * dyc + w[2:3] * dm
            dp_ref[1, :, sl] = (du * v).astype(BF16)
            dp_ref[2, :, sl] = (du * cg).astype(BF16)
            _acc(dcw_ref.at[:, sl], first, jnp.concatenate([_colsum(u * dpl), _colsum(u * dyc), _colsum(u * dm)], axis=0))

    n_hrows = n_tok // GRID_W
    wide = pl.BlockSpec((tb, e), lambda i: (i, 0))
    in_specs = [wide]
    args = [dyc]
    if grid_mode:
        in_specs += [pl.BlockSpec((GRID_W, eh), lambda i: (jnp.maximum(i * hb - 1, 0), 1)),
                     pl.BlockSpec((GRID_W, eh), lambda i: (jnp.minimum((i + 1) * hb, n_hrows - 1), 1))]
        args += [dyc, dyc]
    in_specs += [pl.BlockSpec((1, tb, e), lambda i: (1, i, 0)), pl.BlockSpec((1, tb, e), lambda i: (2, i, 0)), wide, wide,
                 pl.BlockSpec((3, e), lambda i: (0, 0))]
    args += [p, p, dbg, dz, conv_w]
    return pl.pallas_call(
        body, name=name, grid=(nb,),
        out_shape=(jax.ShapeDtypeStruct((4, n_tok, e), BF16), jax.ShapeDtypeStruct((3, e), F32)),
        in_specs=in_specs,
        out_specs=(pl.BlockSpec((4, tb, e), lambda i: (0, i, 0)), pl.BlockSpec((3, e), lambda i: (0, 0))),
        compiler_params=_params(1),
    )(*args)


def _conv_inbwd(dp, w, dr, x, sc, *, tb, name):
    n_chunks, n_tok, e = dp.shape
    d_model = x.shape[1]
    tb = min(tb, n_tok)

    def body(dp_ref, w_ref, dr_ref, x_ref, sc_ref, gx_ref, dsc_ref, dsh_ref, dh_scr):
        k = pl.program_id(1)
        first = pl.program_id(0) == 0
        _acc(dh_scr, k == 0, _dot_nt(dp_ref[0], w_ref[0]))

        @pl.when(k == n_chunks - 1)
        def _():
            dh = dh_scr[...]
            gx_ref[...] = DN_ALPHA * dr_ref[...] + dh * (1.0 + sc_ref[...])
            _acc(dsc_ref, first, _colsum(dh * x_ref[...]))
            _acc(dsh_ref, first, _colsum(dh))

    vec = pl.BlockSpec((1, d_model), lambda i, k: (0, 0))
    tok = pl.BlockSpec((tb, d_model), lambda i, k: (i, 0))
    vec_f = jax.ShapeDtypeStruct((1, d_model), F32)
    return pl.pallas_call(
        body, name=name, grid=(n_tok // tb, n_chunks),
        out_shape=(jax.ShapeDtypeStruct((n_tok, d_model), F32), vec_f, vec_f),
        in_specs=[pl.BlockSpec((1, tb, e), lambda i, k: (k, i, 0)), pl.BlockSpec((1, d_model, e), lambda i, k: (k, 0, 0)),
                  tok, tok, vec],
        out_specs=(tok, vec, vec),
        scratch_shapes=[pltpu.VMEM((tb, d_model), F32)],
        compiler_params=_params(2),
    )(dp, w, dr, x, sc)


def _wgrad(a, b, *, n_chunks, tm, tl, init=None, name):
    n_tok, m = a.shape
    tl = min(tl, n_tok)
    chunked = b.ndim == 3
    cw = b.shape[2] if chunked else b.shape[1] // n_chunks
    has_init = init is not None

    def body(*refs):
        if has_init:
            a_ref, b_ref, init_ref, o_ref = refs
        else:
            a_ref, b_ref, o_ref = refs
        bv = b_ref[0] if chunked else b_ref[...]
        part = _dot_tn(a_ref[...], bv)
        l = pl.program_id(2)

        @pl.when(l == 0)
        def _():
            o_ref[0] = part + init_ref[0] if has_init else part

        @pl.when(l > 0)
        def _():
            o_ref[0] += part

    o_spec = pl.BlockSpec((1, tm, cw), lambda jm, jc, l: (jc, jm, 0))
    b_spec = (pl.BlockSpec((1, tl, cw), lambda jm, jc, l: (jc, l, 0)) if chunked
              else pl.BlockSpec((tl, cw), lambda jm, jc, l: (l, jc)))
    in_specs = [pl.BlockSpec((tl, tm), lambda jm, jc, l: (l, jm)), b_spec] + ([o_spec] if has_init else [])
    args = (a, b) + ((init,) if has_init else ())
    return pl.pallas_call(
        body, name=name, grid=(m // tm, n_chunks, n_tok // tl),
        out_shape=jax.ShapeDtypeStruct((n_chunks, m, cw), F32),
        in_specs=in_specs, out_specs=o_spec, compiler_params=_params(3),
    )(*args)


def _block_diag(t, ngb):
    g, p, n = t.shape
    gpb = g // ngb
    eye = jnp.eye(gpb, dtype=t.dtype)
    return jnp.einsum("bgpn,gh->bgphn", t.reshape(ngb, gpb, p, n), eye).reshape(ngb, gpb * p, gpb * n)


def _block_diag_t(mat, g, p, n):
    ngb = mat.shape[0]
    gpb = g // ngb
    eye = jnp.eye(gpb, dtype=mat.dtype)
    return jnp.einsum("bgphn,gh->bgpn", mat.reshape(ngb, gpb, p, gpb, n), eye).reshape(g, p, n)


def _scan_tables(pw_r, pw_i, ngb, reverse):
    _, g, n = pw_r.shape
    sb = g * n // ngb
    rows = jnp.arange(SUBLANES)
    kinds = []
    for step in (1, 2, 4):
        mask = ((rows < SUBLANES - step) if reverse else (rows >= step)).astype(F32)
        for part in (pw_r[step - 1], pw_i[step - 1]):
            kinds.append(part.reshape(ngb, 1, sb) * mask[None, :, None])
    for part in (pw_r, pw_i):
        pw = part[::-1] if reverse else part
        kinds.append(jnp.transpose(pw.reshape(SUBLANES, ngb, sb), (1, 0, 2)))
    return jnp.stack(kinds, axis=1)


def _flat(parts):
    return jnp.concatenate([p.reshape(-1) for p in parts])


def _unflat(vec, shapes):
    out, off = [], 0
    for s in shapes:
        size = math.prod(s)
        out.append(vec[off:off + size].reshape(s))
        off += size
    return out


def kernel(x, c, ctx, c_ctx, ada_w, ada_b, ln_g, ln_b, conv_w_in, conv_w, conv_w_out, ssm_w_in, ssm_lam_re, ssm_lam_im, ssm_log_step, ssm_b_re, ssm_b_im, ssm_c_re, ssm_c_im, ssm_d, ssm_w_glu, ssm_b_glu, ssm_w_out, loss_target, m_c_ctx, m_ada_w, m_ada_b, m_ln_g, m_ln_b, m_conv_w_in, m_conv_w, m_conv_w_out, m_ssm_w_in, m_ssm_lam_re, m_ssm_lam_im, m_ssm_log_step, m_ssm_b_re, m_ssm_b_im, m_ssm_c_re, m_ssm_c_im, m_ssm_d, m_ssm_w_glu, m_ssm_b_glu, m_ssm_w_out, v_c_ctx, v_ada_w, v_ada_b, v_ln_g, v_ln_b, v_conv_w_in, v_conv_w, v_conv_w_out, v_ssm_w_in, v_ssm_lam_re, v_ssm_lam_im, v_ssm_log_step, v_ssm_b_re, v_ssm_b_im, v_ssm_c_re, v_ssm_c_im, v_ssm_d, v_ssm_w_glu, v_ssm_b_glu, v_ssm_w_out):
    weights = dict(c_ctx=c_ctx, ada_w=ada_w, ada_b=ada_b, ln_g=ln_g, ln_b=ln_b, conv_w_in=conv_w_in, conv_w=conv_w,
                   conv_w_out=conv_w_out, ssm_w_in=ssm_w_in, ssm_lam_re=ssm_lam_re, ssm_lam_im=ssm_lam_im,
                   ssm_log_step=ssm_log_step, ssm_b_re=ssm_b_re, ssm_b_im=ssm_b_im, ssm_c_re=ssm_c_re,
                   ssm_c_im=ssm_c_im, ssm_d=ssm_d, ssm_w_glu=ssm_w_glu, ssm_b_glu=ssm_b_glu, ssm_w_out=ssm_w_out)
    mom_m = dict(c_ctx=m_c_ctx, ada_w=m_ada_w, ada_b=m_ada_b, ln_g=m_ln_g, ln_b=m_ln_b, conv_w_in=m_conv_w_in,
                 conv_w=m_conv_w, conv_w_out=m_conv_w_out, ssm_w_in=m_ssm_w_in, ssm_lam_re=m_ssm_lam_re,
                 ssm_lam_im=m_ssm_lam_im, ssm_log_step=m_ssm_log_step, ssm_b_re=m_ssm_b_re, ssm_b_im=m_ssm_b_im,
                 ssm_c_re=m_ssm_c_re, ssm_c_im=m_ssm_c_im, ssm_d=m_ssm_d, ssm_w_glu=m_ssm_w_glu,
                 ssm_b_glu=m_ssm_b_glu, ssm_w_out=m_ssm_w_out)
    mom_v = dict(c_ctx=v_c_ctx, ada_w=v_ada_w, ada_b=v_ada_b, ln_g=v_ln_g, ln_b=v_ln_b, conv_w_in=v_conv_w_in,
                 conv_w=v_conv_w, conv_w_out=v_conv_w_out, ssm_w_in=v_ssm_w_in, ssm_lam_re=v_ssm_lam_re,
                 ssm_lam_im=v_ssm_lam_im, ssm_log_step=v_ssm_log_step, ssm_b_re=v_ssm_b_re, ssm_b_im=v_ssm_b_im,
                 ssm_c_re=v_ssm_c_re, ssm_c_im=v_ssm_c_im, ssm_d=v_ssm_d, ssm_w_glu=v_ssm_w_glu,
                 ssm_b_glu=v_ssm_b_glu, ssm_w_out=v_ssm_w_out)
    names = list(weights)

    n_lat, d_model = x.shape[1], x.shape[2]
    n_ctx = ctx.shape[1]
    e = 2 * d_model
    n_grp, n_state, grp = ssm_lam_re.shape[2], ssm_lam_re.shape[3], ssm_b_re.shape[4]
    ngb = e // LANES
    ws = ada_w.shape[2]
    tb_tok = min(512, n_lat)
    tb_s5 = min(2048, n_lat // 4)
    tb_glu = min(256, n_lat)
    n_cat = n_lat + 2 * tb_s5
    chip = 2 * lax.axis_index("x") + lax.axis_index("y")
    me = 2 * chip + lax.axis_index("c")
    chips, everyone, pair = ("x", "y"), MESH_AXES, ("c",)

    x2, ctx2, tgt2 = x[0], ctx[0], loss_target[0]

    wc_in = _exchange(conv_w_in[0].astype(BF16), chips, False, "ag_conv_w_in")
    wc_out = _exchange(conv_w_out[0].astype(BF16), chips, False, "ag_conv_w_out").reshape(e, d_model)
    ws_in = _exchange(ssm_w_in[0].astype(BF16), chips, False, "ag_ssm_w_in")
    w_glu = _exchange(ssm_w_glu[0].astype(BF16), chips, False, "ag_ssm_w_glu").reshape(e, e)
    ws_out = _exchange(ssm_w_out[0].astype(BF16), chips, False, "ag_ssm_w_out").reshape(e, d_model)
    small_full = _exchange(_flat([conv_w[0], ssm_d[0], ssm_b_glu[0]]).reshape(1, -1), chips, False, "ag_small")
    es = conv_w.shape[2]
    conv_w_full = jnp.transpose(small_full[:, 0, :3 * es].reshape(4, 3, es), (1, 0, 2)).reshape(3, e)
    d_full = small_full[:, 0, 3 * es:4 * es].reshape(1, e)
    b_glu_full = small_full[:, 0, 4 * es:5 * es].reshape(1, e)

    c_all = _exchange(c, everyone, False, "ag_c").reshape(8, d_model)
    cc2 = c_ctx.reshape(1, d_model)
    b_sh = lax.dynamic_slice_in_dim(ada_b, chip * ws, ws, axis=1).reshape(DEPTH, 1, ws)
    m_sh = _ada_fwd(c_all, cc2, ada_w, b_sh)
    m_all = _exchange(m_sh, chips, False, "ag_mod")
    m_full = jnp.transpose(m_all, (1, 2, 0, 3)).reshape(DEPTH, 16, 3 * d_model)
    m_lat = lax.dynamic_slice_in_dim(m_full, me, 1, axis=1)
    m_ctx = m_full[:, 8:9]

    def mods(m, i):
        return m[i, :, 0:d_model], m[i, :, d_model:2 * d_model], m[i, :, 2 * d_model:3 * d_model]

    sh0, sc0, gt0 = mods(m_lat, 0)
    sh1, sc1, gt1 = mods(m_lat, 1)
    shc0, scc0, gtc0 = mods(m_ctx, 0)
    shc1, scc1, _ = mods(m_ctx, 1)
    ln0 = (ln_g[0:1], ln_b[0:1])
    ln1 = (ln_g[1:2], ln_b[1:2])

    rg = 2 * n_grp
    lam_re2 = ssm_lam_re[0].reshape(rg, n_state)
    lam_im2 = ssm_lam_im[0].reshape(rg, n_state)
    log_step2 = ssm_log_step[0].reshape(rg, 1)
    b_re_t = jnp.transpose(ssm_b_re[0], (3, 0, 1, 2)).reshape(grp, rg, n_state)
    b_im_t = jnp.transpose(ssm_b_im[0], (3, 0, 1, 2)).reshape(grp, rg, n_state)
    pw_r, pw_i, pq_r, pq_i, bbr, bbi = _zoh_fwd(lam_re2, lam_im2, log_step2, b_re_t, b_im_t)
    sbk = n_grp * n_state // ngb
    pw_r, pw_i, pq_r, pq_i = (t.reshape(8, 2, n_grp, n_state) for t in (pw_r, pw_i, pq_r, pq_i))
    bbr_g = jnp.transpose(bbr.reshape(grp, 2, n_grp, n_state), (1, 2, 0, 3))
    bbi_g = jnp.transpose(bbi.reshape(grp, 2, n_grp, n_state), (1, 2, 0, 3))

    def power_rows(pw, r, first):
        full = jnp.concatenate([jnp.full((1, n_grp, n_state), first, F32), pw[:, r]], axis=0)
        return jnp.transpose(full.reshape(CHUNK + 1, ngb, sbk), (1, 0, 2))

    s5 = []
    for r in range(2):
        prm = dict(bre=_block_diag(bbr_g[r], ngb), bim=_block_diag(bbi_g[r], ngb),
                   cre=_block_diag(ssm_c_re[0, r], ngb), cim=_block_diag(ssm_c_im[0, r], ngb),
                   wr=power_rows(pw_r, r, 1.0), wi=power_rows(pw_i, r, 0.0))
        t_op, bp_op, cp_op = _s5_ops(prm["bre"], prm["bim"], prm["cre"], prm["cim"], prm["wr"], prm["wi"],
                                     reverse=(r == 1), name=f"l1_s5_ops{r}")
        s5.append(dict(
            prm, t=t_op, bp=bp_op, cp=cp_op,
            tab=_scan_tables(pq_r[:, r], pq_i[:, r], ngb, reverse=(r == 1)),
            tab_adj=_scan_tables(pq_r[:, r], -pq_i[:, r], ngb, reverse=(r == 0))))

    p0, h0 = _inproj(x2, sc0, sh0, wc_in, tb=tb_tok, name="l0_inproj")
    pc0, hc0 = _inproj(ctx2, scc0, shc0, wc_in, tb=tb_tok, name="l0_inproj_ctx")
    xhat0, rstd0, g0, yc0, f0 = _convgate(p0, x2, gt0, conv_w_full, wc_out, *ln0, grid_mode=True, tb=tb_tok, name="l0_conv")
    chat0, crstd0, gc0, ycc0, fc0 = _convgate(pc0, ctx2, gtc0, conv_w_full, wc_out, *ln0, grid_mode=False, tb=tb_tok,
                                              name="l0_conv_ctx")

    ucat, h1 = _inproj(xhat0, sc1, sh1, ws_in[0:2], lnaff=ln0, tb=math.gcd(tb_tok, tb_s5), gb_rows=(n_cat, tb_s5),
                       name="l1_inproj_u")
    z1, _ = _inproj(xhat0, sc1, sh1, ws_in[2:4], lnaff=ln0, tb=tb_tok, name="l1_inproj_z")
    uc, hc1 = _inproj(chat0, scc1, shc1, ws_in[0:2], lnaff=ln0, tb=tb_tok, gb_rows=(n_ctx, 0), name="l1_inproj_ctx")
    pad = tb_s5 - n_ctx
    ucat = ucat.at[:, 0:pad].set(jnp.zeros((ngb, pad, LANES), BF16))
    ucat = ucat.at[:, pad:tb_s5].set(uc)
    ucat = ucat.at[:, tb_s5 + n_lat:tb_s5 + n_lat + n_ctx].set(uc)
    ucat = ucat.at[:, tb_s5 + n_lat + n_ctx:].set(jnp.zeros((ngb, pad, LANES), BF16))
    u8 = ucat.reshape(ngb, n_cat // CHUNK, CHUNK * LANES)
    y_dir, hc_dir = [], []
    for r in range(2):
        yr, hcr = _s5_fwd(u8, s5[r]["t"], s5[r]["bp"], s5[r]["cp"], s5[r]["tab"], reverse=(r == 1), n_lat=n_lat,
                          tb=tb_s5, name=f"l1_s5_fwd{r}")
        y_dir.append(yr.reshape(ngb, n_lat, LANES))
        hc_dir.append(hcr)

    (loss, dxres, do1, gz1, gg1, dq1, dz1, dyl, dg1, db1, dgt1, dbglu, dd) = _glu_loss(
        ucat, y_dir[0], y_dir[1], z1, xhat0, ln0, gt1, d_full, w_glu, b_glu_full, ws_out, ln1, tgt2,
        row_off=tb_s5, tb=tb_glu, name="l1_glu_loss")

    dy8 = dyl.reshape(ngb, n_lat // CHUNK, CHUNK * LANES)
    du_dir, s5_grads = [], []
    for r in range(2):
        dur, dt_op, dbp_op, dcp_op, da8 = _s5_bwd(u8, dy8, hc_dir[r], s5[r]["t"], s5[r]["bp"], s5[r]["cp"], s5[r]["tab"],
                                                  s5[r]["tab_adj"], reverse=(r == 1), n_lat=n_lat, tb=tb_s5,
                                                  name=f"l1_s5_bwd{r}")
        du_dir.append(dur.reshape(ngb, n_cat, LANES))
        prm = s5[r]
        s5_grads.append(_s5_ops_bwd(prm["bre"], prm["bim"], prm["cre"], prm["cim"], prm["wr"], prm["wi"],
                                    prm["wr"][:, 1:2], prm["wi"][:, 1:2], dt_op, dbp_op, dcp_op, da8,
                                    reverse=(r == 1), name=f"l1_s5_ops_bwd{r}"))
    dp1, dr0, df0, dsc1, dsh1, dg0, db0, dgt0 = _ssm_inbwd(
        du_dir[0], du_dir[1], ws_in, xhat0, rstd0, ln0, sc1, gt0, f0, lat=(dyl, dz1, d_full, dxres),
        row_f=tb_s5, row_b=tb_s5, tb=tb_glu, name="l1_inbwd")
    dpc1, drc0, dfc0, dscc1, dshc1, dgc0, dbc0, dgtc0 = _ssm_inbwd(
        du_dir[0], du_dir[1], ws_in, chat0, crstd0, ln0, scc1, gtc0, fc0, lat=None,
        row_f=tb_s5 - n_ctx, row_b=tb_s5 + n_lat, tb=n_ctx, name="l1_inbwd_ctx")

    def conv_backward(df, p, yc, dr, xin, sc, grid_mode, tag):
        dbg, dz, dyc = _conv_bwd_a(df, wc_out, p, yc, tb=tb_tok, name="l0_bwd_a" + tag)
        dp, dcw = _conv_bwd_b(dyc, p, dbg, dz, conv_w_full, grid_mode=grid_mode, tb=tb_glu, name="l0_bwd_b" + tag)
        gx, dsc, dsh = _conv_inbwd(dp, wc_in, dr, xin, sc, tb=tb_tok, name="l0_inbwd" + tag)
        return dp, dcw, gx, dsc, dsh

    dp0, dcw0, grad_x, dsc0, dsh0 = conv_backward(df0, p0, yc0, dr0, x2, sc0, True, "")
    dpc0, dcwc0, _, dscc0, dshc0 = conv_backward(dfc0, pc0, ycc0, drc0, ctx2, scc0, False, "_ctx")

    tl = tb_tok
    gw_conv_in = _wgrad(h0, dp0, n_chunks=4, tm=d_model, tl=tl, name="wg_conv_in",
                        init=_wgrad(hc0, dpc0, n_chunks=4, tm=d_model, tl=tl, name="wg_conv_in_ctx"))
    gw_conv_out = _wgrad(g0, df0, n_chunks=1, tm=e, tl=tl, name="wg_conv_out",
                         init=_wgrad(gc0, dfc0, n_chunks=1, tm=e, tl=tl, name="wg_conv_out_ctx"))
    gw_ssm_in = _wgrad(h1, dp1, n_chunks=4, tm=d_model, tl=tl, name="wg_ssm_in",
                       init=_wgrad(hc1, dpc1, n_chunks=4, tm=d_model, tl=tl, name="wg_ssm_in_ctx"))
    gw_glu = _wgrad(gg1, dq1, n_chunks=1, tm=e // 2, tl=tl, name="wg_glu")
    gw_ssm_out = _wgrad(gz1, do1, n_chunks=1, tm=e, tl=tl, name="wg_ssm_out")

    grads, deltas, new_m, new_v = {}, {}, {}, {}

    def reduce_big(name, full):
        w = weights[name]
        rows = math.prod(w.shape[:-1])
        cols = w.shape[-1]
        parts = _exchange(full.reshape(8, rows // 2, cols), everyone, True, "rs_" + name)
        half = _sum_parts(parts, "sum_" + name)
        both = _exchange(half, pair, False, "pair_" + name).reshape(rows, cols)
        dlt, nm, nv = _adamw(w.reshape(rows, cols), both, mom_m[name].reshape(rows, cols),
                             mom_v[name].reshape(rows, cols), "adamw_" + name)
        grads[name], deltas[name] = both.reshape(w.shape), dlt.reshape(w.shape)
        new_m[name], new_v[name] = nm.reshape(w.shape), nv.reshape(w.shape)

    reduce_big("conv_w_in", gw_conv_in)
    reduce_big("conv_w_out", gw_conv_out)
    reduce_big("ssm_w_in", gw_ssm_in)
    reduce_big("ssm_w_glu", gw_glu)
    reduce_big("ssm_w_out", gw_ssm_out)

    gpn = (n_grp, grp, n_state)
    small_parts = [
        jnp.concatenate([dg0 + dgc0, dg1], axis=0), jnp.concatenate([db0 + dbc0, db1], axis=0),
        dcw0 + dcwc0, dd, dbglu,
        jnp.stack([s5_grads[r][4] for r in range(2)]),
    ] + [jnp.stack([_block_diag_t(s5_grads[r][k], *gpn) for r in range(2)]) for k in range(4)]
    small_shapes = [p.shape for p in small_parts]
    flat = _flat(small_parts)
    quantum = 8 * SUBLANES * LANES
    n_flat = -(-flat.shape[0] // quantum) * quantum
    flat = jnp.pad(flat, (0, n_flat - flat.shape[0])).reshape(8, n_flat // (8 * LANES), LANES)
    red = _sum_parts(_exchange(flat, everyone, True, "rs_small"), "sum_small")
    red = _exchange(red, everyone, False, "ag_small_grads").reshape(-1)
    g_ln_g, g_ln_b, g_conv_w, g_d, g_bglu, g_a, g_bbr, g_bbi, g_cre, g_cim = _unflat(red, small_shapes)

    g_a = g_a.reshape(2, ngb, 2, sbk)
    dar = g_a[:, :, 0].reshape(rg, n_state)
    dai = g_a[:, :, 1].reshape(rg, n_state)
    dbbr_t = jnp.transpose(g_bbr, (2, 0, 1, 3)).reshape(grp, rg, n_state)
    dbbi_t = jnp.transpose(g_bbi, (2, 0, 1, 3)).reshape(grp, rg, n_state)
    z_lre, z_lim, z_ls, z_bre, z_bim = _zoh_bwd(lam_re2, lam_im2, log_step2, b_re_t, b_im_t, dar, dai, dbbr_t, dbbi_t)

    zero = jnp.zeros((1, d_model), F32)
    dm_rows = jnp.stack([
        jnp.stack([jnp.concatenate([dsh0, dsc0, dgt0], axis=1), jnp.concatenate([dshc0, dscc0, dgtc0], axis=1)]),
        jnp.stack([jnp.concatenate([dsh1, dsc1, dgt1], axis=1), jnp.concatenate([dshc1, dscc1, zero], axis=1)]),
    ]).reshape(DEPTH, 2, 3 * d_model)
    dm_all = _exchange(dm_rows, everyone, False, "ag_dmod")
    dm_sh = lax.dynamic_slice_in_dim(dm_all, chip * ws, ws, axis=3)
    g_ada_w, g_ada_b, ds_part = _ada_bwd(c_all, cc2, ada_w, dm_all, dm_sh)
    g_cctx = _cctx_grad(_exchange(ds_part, chips, False, "ag_dsctx"), cc2)

    grads["ada_w"] = g_ada_w
    dlt, nm, nv = _adamw(ada_w.reshape(-1, ws), g_ada_w.reshape(-1, ws), m_ada_w.reshape(-1, ws),
                         v_ada_w.reshape(-1, ws), "adamw_ada_w")
    deltas["ada_w"], new_m["ada_w"], new_v["ada_w"] = dlt.reshape(ada_w.shape), nm.reshape(ada_w.shape), nv.reshape(ada_w.shape)

    def chip_cols(full, rows):
        return lax.dynamic_slice_in_dim(full.reshape(rows, e), chip * es, es, axis=1)

    small_grads = dict(
        c_ctx=g_cctx.reshape(c_ctx.shape), ada_b=g_ada_b.reshape(ada_b.shape), ln_g=g_ln_g, ln_b=g_ln_b,
        conv_w=chip_cols(g_conv_w, 3).reshape(conv_w.shape),
        ssm_lam_re=z_lre.reshape(ssm_lam_re.shape), ssm_lam_im=z_lim.reshape(ssm_lam_im.shape),
        ssm_log_step=z_ls.reshape(ssm_log_step.shape),
        ssm_b_re=jnp.transpose(z_bre.reshape(grp, 2, n_grp, n_state), (1, 2, 3, 0)).reshape(ssm_b_re.shape),
        ssm_b_im=jnp.transpose(z_bim.reshape(grp, 2, n_grp, n_state), (1, 2, 3, 0)).reshape(ssm_b_im.shape),
        ssm_c_re=g_cre.reshape(ssm_c_re.shape), ssm_c_im=g_cim.reshape(ssm_c_im.shape),
        ssm_d=chip_cols(g_d, 1).reshape(ssm_d.shape), ssm_b_glu=chip_cols(g_bglu, 1).reshape(ssm_b_glu.shape))
    small_names = list(small_grads)
    shapes = [weights[n].shape for n in small_names]
    quantum = SUBLANES * LANES

    def pack(parts, fill):
        vec = _flat(parts)
        n_pad = -(-vec.shape[0] // quantum) * quantum
        return jnp.pad(vec, (0, n_pad - vec.shape[0]), constant_values=fill).reshape(-1, LANES)

    dlt, nm, nv = _adamw(pack([weights[n] for n in small_names], 0.0), pack([small_grads[n] for n in small_names], 0.0),
                         pack([mom_m[n] for n in small_names], 0.0), pack([mom_v[n] for n in small_names], 1.0),
                         "adamw_small")
    for n, dv, mv, vv in zip(small_names, _unflat(dlt.reshape(-1), shapes), _unflat(nm.reshape(-1), shapes),
                             _unflat(nv.reshape(-1), shapes)):
        grads[n], deltas[n], new_m[n], new_v[n] = small_grads[n], dv, mv, vv

    loss_total = lax.psum(loss[0, 0], MESH_AXES)
    return (loss_total, grad_x.reshape(x.shape), *[grads[n] for n in names], *[deltas[n] for n in names],
            *[new_m[n] for n in names], *[new_v[n] for n in names])
```

```python
import functools
import math

import jax
import jax.numpy as jnp
from jax import lax
from jax.experimental import pallas as pl
from jax.experimental.pallas import tpu as pltpu

F32 = jnp.float32
BF16 = jnp.bfloat16
LANES = 128
SUBLANES = 8
VMEM_LIMIT = 56 * 1024 * 1024
MESH_AXES = ("x", "y", "c")
HIGHEST = lax.Precision.HIGHEST

GRID_W = 64
LN_EPS = 1e-5
DEPTH = 2
DN_ALPHA = (2 * DEPTH) ** 0.25
ADAM_LR, ADAM_B1, ADAM_B2, ADAM_EPS, ADAM_WD, ADAM_STEP = 0.001, 0.9, 0.999, 1e-08, 0.01, 10
GELU_K = math.sqrt(2.0 / math.pi)
GELU_C = 0.044715


def _params(n_grid_axes):
    return pltpu.CompilerParams(dimension_semantics=("arbitrary",) * n_grid_axes, vmem_limit_bytes=VMEM_LIMIT)


def _dot(a, b):
    return jnp.dot(a, b, preferred_element_type=F32)


def _dot_nt(a, b):
    return lax.dot_general(a, b, (((1,), (1,)), ((), ())), preferred_element_type=F32)


def _dot_tn(a, b):
    return lax.dot_general(a, b, (((0,), (0,)), ((), ())), preferred_element_type=F32)


def _sigmoid(x):
    return 1.0 / (1.0 + jnp.exp(-x))


def _colsum(x):
    return jnp.sum(x, axis=0, keepdims=True)


def _rowmean(x):
    return jnp.mean(x, axis=-1, keepdims=True)


def _acc(ref, first, value):
    @pl.when(first)
    def _():
        ref[...] = value

    @pl.when(jnp.logical_not(first))
    def _():
        ref[...] += value


def _exchange(src, axes, all_to_all, name, n_split=1):
    n_peers = 2 ** len(axes)
    block = tuple(src.shape[1:] if all_to_all else src.shape)
    piece = block[0] // n_split
    assert piece * n_split == block[0]

    def body(src_ref, out_ref, send_sems, recv_sems, own_sem):
        pos = {a: lax.axis_index(a) for a in MESH_AXES}

        def index(p):
            return sum(p[a] * (2 ** (len(axes) - 1 - i)) for i, a in enumerate(axes))

        me = index(pos)
        own = pltpu.make_async_copy(src_ref.at[me] if all_to_all else src_ref, out_ref.at[me], own_sem)
        own.start()
        copies = []
        for k in range(1, n_peers):
            peer = dict(pos)
            for i, a in enumerate(axes):
                if (k >> (len(axes) - 1 - i)) & 1:
                    peer[a] = 1 - pos[a]
            for s in range(n_split):
                part = pl.ds(s * piece, piece)
                sem = (k - 1) * n_split + s
                cp = pltpu.make_async_remote_copy(
                    src_ref=(src_ref.at[index(peer)] if all_to_all else src_ref).at[part],
                    dst_ref=out_ref.at[me].at[part],
                    send_sem=send_sems.at[sem],
                    recv_sem=recv_sems.at[sem],
                    device_id=tuple(peer[a] for a in MESH_AXES),
                    device_id_type=pl.DeviceIdType.MESH,
                )
                cp.start()
                copies.append(cp)
        for cp in copies:
            cp.wait()
        own.wait()

    return pl.pallas_call(
        body,
        name=name,
        out_shape=jax.ShapeDtypeStruct((n_peers,) + block, src.dtype),
        in_specs=[pl.BlockSpec(memory_space=pltpu.HBM)],
        out_specs=pl.BlockSpec(memory_space=pltpu.HBM),
        scratch_shapes=[
            pltpu.SemaphoreType.DMA(((n_peers - 1) * n_split,)),
            pltpu.SemaphoreType.DMA(((n_peers - 1) * n_split,)),
            pltpu.SemaphoreType.DMA,
        ],
    )(src)


def _sum_parts(parts, name):
    n_parts, rows, cols = parts.shape
    tr = rows
    while n_parts * tr * cols * 4 > 8 * 1024 * 1024 and tr % 16 == 0:
        tr //= 2

    def body(p_ref, o_ref):
        total = p_ref[0]
        for k in range(1, n_parts):
            total = total + p_ref[k]
        o_ref[...] = total

    return pl.pallas_call(
        body,
        name=name,
        grid=(rows // tr,),
        out_shape=jax.ShapeDtypeStruct((rows, cols), F32),
        in_specs=[pl.BlockSpec((n_parts, tr, cols), lambda i: (0, i, 0))],
        out_specs=pl.BlockSpec((tr, cols), lambda i: (i, 0)),
        compiler_params=_params(1),
    )(parts)


def _adamw(w, g, m, v, name):
    rows, cols = w.shape
    tr = rows
    while tr * cols * 4 > 2 * 1024 * 1024 and tr % 16 == 0:
        tr //= 2

    def body(w_ref, g_ref, m_ref, v_ref, d_ref, nm_ref, nv_ref):
        gv = g_ref[...]
        nm = ADAM_B1 * m_ref[...] + (1.0 - ADAM_B1) * gv
        nv = ADAM_B2 * v_ref[...] + (1.0 - ADAM_B2) * (gv * gv)
        m_hat = nm / (1.0 - ADAM_B1 ** ADAM_STEP)
        v_hat = nv / (1.0 - ADAM_B2 ** ADAM_STEP)
        d_ref[...] = -ADAM_LR * (m_hat / (jnp.sqrt(v_hat) + ADAM_EPS) + ADAM_WD * w_ref[...])
        nm_ref[...] = nm
        nv_ref[...] = nv

    spec = pl.BlockSpec((tr, cols), lambda i: (i, 0))
    shape = jax.ShapeDtypeStruct((rows, cols), F32)
    return pl.pallas_call(
        body, name=name, grid=(rows // tr,), out_shape=(shape, shape, shape),
        in_specs=[spec] * 4, out_specs=(spec, spec, spec), compiler_params=_params(1),
    )(w, g, m, v)


def _ada_rows(c_ref, cc_ref):
    rows = jnp.concatenate([c_ref[...], jnp.broadcast_to(cc_ref[...], c_ref.shape)], axis=0)
    return rows


def _ada_fwd(c_all, c_ctx, w_sh, b_sh):
    n_layers, _, ws = w_sh.shape

    def body(c_ref, cc_ref, w_ref, b_ref, o_ref):
        rows = _ada_rows(c_ref, cc_ref)
        s = rows * _sigmoid(rows)
        for i in range(n_layers):
            o_ref[i] = jnp.dot(s, w_ref[i], precision=HIGHEST, preferred_element_type=F32) + b_ref[i]

    return pl.pallas_call(
        body, name="ada_fwd", out_shape=jax.ShapeDtypeStruct((n_layers, 16, ws), F32),
        compiler_params=pltpu.CompilerParams(vmem_limit_bytes=VMEM_LIMIT),
    )(c_all, c_ctx, w_sh, b_sh)


def _ada_bwd(c_all, c_ctx, w_sh, dm_full, dm_sh):
    n_layers, d_model, ws = w_sh.shape
    n_dev = dm_full.shape[0]
    cols = dm_full.shape[-1]

    def body(c_ref, cc_ref, w_ref, dmf_ref, dms_ref, gw_ref, gb_ref, ds_ref):
        rows = _ada_rows(c_ref, cc_ref)
        s = rows * _sigmoid(rows)
        ds = jnp.zeros((8, d_model), F32)
        for i in range(n_layers):
            ctx_s = dms_ref[0, i, 1:2, :]
            ctx_f = dmf_ref[0, i, 1:2, :]
            ex_f = dmf_ref[0, i, 0:1, :]
            for k in range(1, n_dev):
                ctx_s = ctx_s + dms_ref[k, i, 1:2, :]
                ctx_f = ctx_f + dmf_ref[k, i, 1:2, :]
                ex_f = ex_f + dmf_ref[k, i, 0:1, :]
            gb_ref[i] = ex_f + ctx_f
            r = jnp.concatenate([dms_ref[k, i, 0:1, :] for k in range(n_dev)] + [ctx_s, jnp.zeros((7, ws), F32)], axis=0)
            gw_ref[i] = lax.dot_general(s, r, (((0,), (0,)), ((), ())), precision=HIGHEST, preferred_element_type=F32)
            ds = ds + lax.dot_general(jnp.broadcast_to(ctx_s, (8, ws)), w_ref[i], (((1,), (1,)), ((), ())),
                                      precision=HIGHEST, preferred_element_type=F32)
        ds_ref[...] = ds

    return pl.pallas_call(
        body, name="ada_bwd",
        out_shape=(jax.ShapeDtypeStruct((n_layers, d_model, ws), F32), jax.ShapeDtypeStruct((n_layers, 1, cols), F32),
                   jax.ShapeDtypeStruct((8, d_model), F32)),
        compiler_params=pltpu.CompilerParams(vmem_limit_bytes=VMEM_LIMIT),
    )(c_all, c_ctx, w_sh, dm_full, dm_sh)


def _cctx_grad(ds_parts, c_ctx):
    def body(p_ref, c_ref, o_ref):
        tot = p_ref[0, 0:1, :]
        for k in range(1, ds_parts.shape[0]):
            tot = tot + p_ref[k, 0:1, :]
        cv = c_ref[...]
        sg = _sigmoid(cv)
        o_ref[...] = tot * (sg * (1.0 + cv * (1.0 - sg)))

    return pl.pallas_call(body, name="cctx_grad", out_shape=jax.ShapeDtypeStruct(c_ctx.shape, F32))(ds_parts, c_ctx)


def _zoh_math(lam_re, lam_im, log_step, b_re, b_im):
    dt = jnp.exp(log_step)
    mag = jnp.exp(lam_re * dt)
    ar = mag * jnp.cos(lam_im * dt)
    ai = mag * jnp.sin(lam_im * dt)
    qr, qi = ar - 1.0, ai
    den = lam_re * lam_re + lam_im * lam_im
    fr = (qr * lam_re + qi * lam_im) / den
    fi = (qi * lam_re - qr * lam_im) / den
    bbr = fr[None] * b_re - fi[None] * b_im
    bbi = fr[None] * b_im + fi[None] * b_re
    return ar, ai, bbr, bbi


def _zoh_fwd(lam_re, lam_im, log_step, b_re, b_im):
    rg, n = lam_re.shape

    def body(lr_ref, li_ref, ls_ref, br_ref, bi_ref, pr_ref, pi_ref, qr_ref, qi_ref, bbr_ref, bbi_ref):
        ar, ai, bbr, bbi = _zoh_math(lr_ref[...], li_ref[...], ls_ref[...], br_ref[...], bi_ref[...])
        bbr_ref[...] = bbr
        bbi_ref[...] = bbi

        def powers(base_r, base_i, r_ref, i_ref):
            pr, pi_ = base_r, base_i
            for k in range(8):
                r_ref[k] = pr
                i_ref[k] = pi_
                pr, pi_ = pr * base_r - pi_ * base_i, pr * base_i + pi_ * base_r

        powers(ar, ai, pr_ref, pi_ref)
        powers(pr_ref[7], pi_ref[7], qr_ref, qi_ref)

    pw = jax.ShapeDtypeStruct((8, rg, n), F32)
    bb = jax.ShapeDtypeStruct(b_re.shape, F32)
    return pl.pallas_call(body, name="zoh_fwd", out_shape=(pw, pw, pw, pw, bb, bb))(lam_re, lam_im, log_step, b_re, b_im)


def _zoh_bwd(lam_re, lam_im, log_step, b_re, b_im, dar, dai, dbbr, dbbi):
    def body(lr_ref, li_ref, ls_ref, br_ref, bi_ref, dar_ref, dai_ref, dbr_ref, dbi_ref, *outs):
        _, vjp = jax.vjp(_zoh_math, lr_ref[...], li_ref[...], ls_ref[...], br_ref[...], bi_ref[...])
        grads = vjp((dar_ref[...], dai_ref[...], dbr_ref[...], dbi_ref[...]))
        for o_ref, gval in zip(outs, grads):
            o_ref[...] = gval

    shapes = tuple(jax.ShapeDtypeStruct(a.shape, F32) for a in (lam_re, lam_im, log_step, b_re, b_im))
    return pl.pallas_call(body, name="zoh_bwd", out_shape=shapes)(lam_re, lam_im, log_step, b_re, b_im, dar, dai, dbbr, dbbi)


def _inproj(xin, sc, sh, w, *, lnaff=None, tb, gb_rows=None, name):
    n_tok, d_model = xin.shape
    n_chunks, _, cw = w.shape
    tb = min(tb, n_tok)
    nq = cw // LANES
    has_ln = lnaff is not None
    n_out = 1 if gb_rows is None else len(gb_rows)

    def body(*refs):
        if has_ln:
            x_ref, g_ref, b_ref, sc_ref, sh_ref, w_ref = refs[:6]
        else:
            x_ref, sc_ref, sh_ref, w_ref = refs[:4]
        p_refs, h_ref = refs[-1 - n_out:-1], refs[-1]

        @pl.when(pl.program_id(1) == 0)
        def _():
            xv = x_ref[...]
            if has_ln:
                xv = xv * g_ref[...] + b_ref[...]
            h_ref[...] = (xv * (1.0 + sc_ref[...]) + sh_ref[...]).astype(BF16)

        acc = _dot(h_ref[...], w_ref[0]).astype(BF16)
        if gb_rows is None:
            p_refs[0][0] = acc
        else:
            for p_ref in p_refs:
                for q in range(nq):
                    p_ref[q] = acc[:, q * LANES:(q + 1) * LANES]

    vec = pl.BlockSpec((1, d_model), lambda i, j: (0, 0))
    in_specs = [pl.BlockSpec((tb, d_model), lambda i, j: (i, 0))] + ([vec, vec] if has_ln else []) + [
        vec, vec, pl.BlockSpec((1, d_model, cw), lambda i, j: (j, 0, 0))]
    if gb_rows is None:
        p_shapes = [jax.ShapeDtypeStruct((n_chunks, n_tok, cw), BF16)]
        p_specs = [pl.BlockSpec((1, tb, cw), lambda i, j: (j, i, 0))]
    else:
        p_shapes, p_specs = [], []
        for total, off in gb_rows:
            assert off % tb == 0
            p_shapes.append(jax.ShapeDtypeStruct((n_chunks * nq, total, LANES), BF16))
            p_specs.append(pl.BlockSpec((nq, tb, LANES), functools.partial(lambda i, j, ob: (j, i + ob, 0), ob=off // tb)))
    args = (xin,) + (tuple(lnaff) if has_ln else ()) + (sc, sh, w)
    return pl.pallas_call(
        body, name=name, grid=(n_tok // tb, n_chunks),
        out_shape=(*p_shapes, jax.ShapeDtypeStruct((n_tok, d_model), BF16)),
        in_specs=in_specs, out_specs=(*p_specs, pl.BlockSpec((tb, d_model), lambda i, j: (i, 0))),
        compiler_params=_params(2),
    )(*args)


def _shifted(u, rows, width, tb):
    col = rows % width
    um = jnp.where(col == 0, 0.0, pltpu.roll(u, 1, 0))
    up = jnp.where(col == width - 1, 0.0, pltpu.roll(u, tb - 1, 0))
    return um, up


def _slab_width(e):
    return min(512, e // 2)


def _convgate(p, x, gt, conv_w, w_out, ln_g, ln_b, *, grid_mode, tb, name):
    _, n_tok, e = p.shape
    d_model = x.shape[1]
    eh = e // 2
    if not grid_mode:
        tb = n_tok
    tb = min(tb, n_tok)
    nb = n_tok // tb
    hb = tb // GRID_W
    cs = _slab_width(e)

    def body(*refs):
        if grid_mode:
            (bg_ref, cg_ref, v_ref, z_ref, cgp_ref, vp_ref, cgn_ref, vn_ref, x_ref, gt_ref, cw_ref, wo_ref, lg_ref,
             lb_ref, xh_ref, rs_ref, g_ref, yc_ref, f_ref) = refs
        else:
            (bg_ref, cg_ref, v_ref, z_ref, x_ref, gt_ref, cw_ref, wo_ref, lg_ref, lb_ref, xh_ref, rs_ref, g_ref,
             yc_ref, f_ref) = refs
        i = pl.program_id(0)
        rows = lax.broadcasted_iota(jnp.int32, (tb, 1), 0)
        for c0 in range(0, e, cs):
            sl = slice(c0, c0 + cs)
            u = cg_ref[0, :, sl].astype(F32) * v_ref[0, :, sl].astype(F32)
            w = cw_ref[:, sl]
            if grid_mode and c0 >= eh:
                hs = slice(c0 - eh, c0 - eh + cs)
                uprev = cgp_ref[0, :, hs].astype(F32) * vp_ref[0, :, hs].astype(F32)
                unext = cgn_ref[0, :, hs].astype(F32) * vn_ref[0, :, hs].astype(F32)
                uprev = jnp.where(i > 0, uprev, 0.0)
                unext = jnp.where(i < nb - 1, unext, 0.0)
                if tb > GRID_W:
                    um = jnp.concatenate([uprev, u[:tb - GRID_W]], axis=0)
                    up = jnp.concatenate([u[GRID_W:], unext], axis=0)
                else:
                    um, up = uprev, unext
            else:
                um, up = _shifted(u, rows, GRID_W if grid_mode else tb, tb)
            yc = um * w[0:1] + u * w[1:2] + up * w[2:3]
            zf = z_ref[0, :, sl].astype(F32)
            gval = bg_ref[0, :, sl].astype(F32) * yc * (zf * _sigmoid(zf))
            yc_ref[:, sl] = yc.astype(BF16)
            g_ref[:, sl] = gval.astype(BF16)
        f = _dot(g_ref[...], wo_ref[...])
        f_ref[...] = f.astype(BF16)
        r = DN_ALPHA * x_ref[...] + gt_ref[...] * f
        rc = r - _rowmean(r)
        rstd = lax.rsqrt(_rowmean(rc * rc) + LN_EPS)
        xh_ref[...] = rc * rstd
        rs_ref[...] = rstd

    def chunk(k):
        return pl.BlockSpec((1, tb, e), lambda i: (k, i, 0))

    n_hrows = n_tok // GRID_W

    def halo_prev(k):
        return pl.BlockSpec((1, GRID_W, eh), lambda i: (k, jnp.maximum(i * hb - 1, 0), 1))

    def halo_next(k):
        return pl.BlockSpec((1, GRID_W, eh), lambda i: (k, jnp.minimum((i + 1) * hb, n_hrows - 1), 1))

    vec = pl.BlockSpec((1, d_model), lambda i: (0, 0))
    tok = pl.BlockSpec((tb, d_model), lambda i: (i, 0))
    wide = pl.BlockSpec((tb, e), lambda i: (i, 0))
    in_specs = [chunk(0), chunk(1), chunk(2), chunk(3)]
    args = [p, p, p, p]
    if grid_mode:
        in_specs += [halo_prev(1), halo_prev(2), halo_next(1), halo_next(2)]
        args += [p, p, p, p]
    in_specs += [tok, vec, pl.BlockSpec((3, e), lambda i: (0, 0)), pl.BlockSpec((e, d_model), lambda i: (0, 0)), vec, vec]
    args += [x, gt, conv_w, w_out, ln_g, ln_b]
    return pl.pallas_call(
        body, name=name, grid=(nb,),
        out_shape=(jax.ShapeDtypeStruct((n_tok, d_model), F32), jax.ShapeDtypeStruct((n_tok, 1), F32),
                   jax.ShapeDtypeStruct((n_tok, e), BF16), jax.ShapeDtypeStruct((n_tok, e), BF16),
                   jax.ShapeDtypeStruct((n_tok, d_model), BF16)),
        in_specs=in_specs, out_specs=(tok, pl.BlockSpec((tb, 1), lambda i: (i, 0)), wide, wide, tok),
        compiler_params=_params(1),
    )(*args)


def _scan_block(buf_ref, tab_ref, cr, ci, *, reverse, tb, sb):
    n_slabs = tb // SUBLANES

    def slab(s, carry):
        cr, ci = carry
        idx = (n_slabs - 1 - s) if reverse else s
        r0 = pl.multiple_of(idx * SUBLANES, SUBLANES)
        xr = buf_ref[pl.ds(r0, SUBLANES), 0:sb]
        xi = buf_ref[pl.ds(r0, SUBLANES), sb:2 * sb]
        for k, step in enumerate((1, 2, 4)):
            ar = tab_ref[2 * k]
            ai = tab_ref[2 * k + 1]
            shift = (SUBLANES - step) if reverse else step
            rr = pltpu.roll(xr, shift, 0)
            ri = pltpu.roll(xi, shift, 0)
            xr, xi = xr + ar * rr - ai * ri, xi + ar * ri + ai * rr
        pr = tab_ref[6]
        pi_ = tab_ref[7]
        xr, xi = xr + pr * cr - pi_ * ci, xi + pr * ci + pi_ * cr
        buf_ref[pl.ds(r0, SUBLANES), 0:sb] = xr
        buf_ref[pl.ds(r0, SUBLANES), sb:2 * sb] = xi
        last = 0 if reverse else SUBLANES - 1
        return (jnp.broadcast_to(xr[last:last + 1, :], (SUBLANES, sb)),
                jnp.broadcast_to(xi[last:last + 1, :], (SUBLANES, sb)))

    return lax.fori_loop(0, n_slabs, slab, (cr, ci))


CHUNK = SUBLANES


def _group_mask():
    r = lax.broadcasted_iota(jnp.int32, (LANES, LANES), 0)
    c = lax.broadcasted_iota(jnp.int32, (LANES, LANES), 1)
    return r // 16 == c // 16


def _s5_ops(bre, bim, cre, cim, wr, wi, *, reverse, name):
    ngb, _, sb = bre.shape
    n_rows = CHUNK * LANES

    def body(bre_ref, bim_ref, cre_ref, cim_ref, wr_ref, wi_ref, t_ref, bp_ref, cp_ref):
        b_re, b_im, c_re, c_im = bre_ref[0], bim_ref[0], cre_ref[0], cim_ref[0]
        mask = _group_mask()
        er, ei = [], []
        for tau in range(CHUNK + 1):
            w_r, w_i = wr_ref[0, tau:tau + 1, :], wi_ref[0, tau:tau + 1, :]
            er.append(c_re * w_r - c_im * w_i)
            ei.append(c_re * w_i + c_im * w_r)
        kt = []
        for tau in range(CHUNK):
            k = (lax.dot_general(b_re, er[tau], (((1,), (1,)), ((), ())), precision=HIGHEST, preferred_element_type=F32)
                 - lax.dot_general(b_im, ei[tau], (((1,), (1,)), ((), ())), precision=HIGHEST, preferred_element_type=F32))
            kt.append(jnp.where(mask, k, 0.0).astype(BF16))
        zero = jnp.zeros((LANES, LANES), BF16)
        for i in range(CHUNK):
            rows = slice(i * LANES, (i + 1) * LANES)
            for j in range(CHUNK):
                lag = (i - j) if reverse else (j - i)
                t_ref[0, rows, j * LANES:(j + 1) * LANES] = kt[lag] if lag >= 0 else zero
            tau = i if reverse else CHUNK - 1 - i
            w_r, w_i = wr_ref[0, tau:tau + 1, :], wi_ref[0, tau:tau + 1, :]
            bp_ref[0, rows, 0:sb] = (b_re * w_r - b_im * w_i).astype(BF16)
            bp_ref[0, rows, sb:2 * sb] = (b_re * w_i + b_im * w_r).astype(BF16)
            tau = CHUNK - i if reverse else i + 1
            cp_ref[0, rows, 0:sb] = er[tau].astype(BF16)
            cp_ref[0, rows, sb:2 * sb] = (-ei[tau]).astype(BF16)

    mat = pl.BlockSpec((1, LANES, sb), lambda g: (g, 0, 0))
    pw = pl.BlockSpec((1, CHUNK + 1, sb), lambda g: (g, 0, 0))
    return pl.pallas_call(
        body, name=name, grid=(ngb,),
        out_shape=(jax.ShapeDtypeStruct((ngb, n_rows, n_rows), BF16), jax.ShapeDtypeStruct((ngb, n_rows, 2 * sb), BF16),
                   jax.ShapeDtypeStruct((ngb, n_rows, 2 * sb), BF16)),
        in_specs=[mat, mat, mat, mat, pw, pw],
        out_specs=(pl.BlockSpec((1, n_rows, n_rows), lambda g: (g, 0, 0)),
                   pl.BlockSpec((1, n_rows, 2 * sb), lambda g: (g, 0, 0)),
                   pl.BlockSpec((1, n_rows, 2 * sb), lambda g: (g, 0, 0))),
        compiler_params=_params(1),
    )(bre, bim, cre, cim, wr, wi)


def _s5_ops_bwd(bre, bim, cre, cim, wr, wi, ar, ai, dt, dbp, dcp, da8, *, reverse, name):
    ngb, _, sb = bre.shape
    n_rows = CHUNK * LANES

    def dot_hi(a, b, dims):
        return lax.dot_general(a, b, (dims, ((), ())), precision=HIGHEST, preferred_element_type=F32)

    def body(bre_ref, bim_ref, cre_ref, cim_ref, wr_ref, wi_ref, ar_ref, ai_ref, dt_ref, dbp_ref, dcp_ref, da8_ref,
             dbre_ref, dbim_ref, dcre_ref, dcim_ref, da_ref):
        b_re, b_im, c_re, c_im = bre_ref[0], bim_ref[0], cre_ref[0], cim_ref[0]
        mask = _group_mask()
        w_r = [wr_ref[0, tau:tau + 1, :] for tau in range(CHUNK + 1)]
        w_i = [wi_ref[0, tau:tau + 1, :] for tau in range(CHUNK + 1)]
        der = [jnp.zeros((LANES, sb), F32) for _ in range(CHUNK + 1)]
        dei = [jnp.zeros((LANES, sb), F32) for _ in range(CHUNK + 1)]
        dwr = [jnp.zeros((1, sb), F32) for _ in range(CHUNK + 1)]
        dwi = [jnp.zeros((1, sb), F32) for _ in range(CHUNK + 1)]
        dwr[CHUNK] = da8_ref[0, :, 0:sb]
        dwi[CHUNK] = da8_ref[0, :, sb:2 * sb]
        d_bre = jnp.zeros((LANES, sb), F32)
        d_bim = jnp.zeros((LANES, sb), F32)
        dkt = [jnp.zeros((LANES, LANES), F32) for _ in range(CHUNK)]
        for i in range(CHUNK):
            rows = slice(i * LANES, (i + 1) * LANES)
            for j in range(CHUNK):
                lag = (i - j) if reverse else (j - i)
                if lag >= 0:
                    dkt[lag] = dkt[lag] + dt_ref[0, rows, j * LANES:(j + 1) * LANES]
            tau = i if reverse else CHUNK - 1 - i
            g_r, g_i = dbp_ref[0, rows, 0:sb], dbp_ref[0, rows, sb:2 * sb]
            d_bre = d_bre + g_r * w_r[tau] + g_i * w_i[tau]
            d_bim = d_bim - g_r * w_i[tau] + g_i * w_r[tau]
            dwr[tau] = dwr[tau] + _colsum(g_r * b_re + g_i * b_im)
            dwi[tau] = dwi[tau] + _colsum(g_i * b_re - g_r * b_im)
            tau = CHUNK - i if reverse else i + 1
            der[tau] = der[tau] + dcp_ref[0, rows, 0:sb]
            dei[tau] = dei[tau] - dcp_ref[0, rows, sb:2 * sb]
        d_cre = jnp.zeros((LANES, sb), F32)
        d_cim = jnp.zeros((LANES, sb), F32)
        for tau in range(CHUNK + 1):
            if tau < CHUNK:
                e_r = c_re * w_r[tau] - c_im * w_i[tau]
                e_i = c_re * w_i[tau] + c_im * w_r[tau]
                dk = jnp.where(mask, dkt[tau], 0.0)
                d_bre = d_bre + dot_hi(dk, e_r, ((1,), (0,)))
                d_bim = d_bim - dot_hi(dk, e_i, ((1,), (0,)))
                der[tau] = der[tau] + dot_hi(dk, b_re, ((0,), (0,)))
                dei[tau] = dei[tau] - dot_hi(dk, b_im, ((0,), (0,)))
            d_cre = d_cre + der[tau] * w_r[tau] + dei[tau] * w_i[tau]
            d_cim = d_cim - der[tau] * w_i[tau] + dei[tau] * w_r[tau]
            dwr[tau] = dwr[tau] + _colsum(der[tau] * c_re + dei[tau] * c_im)
            dwi[tau] = dwi[tau] + _colsum(dei[tau] * c_re - der[tau] * c_im)
        a_r, a_i = ar_ref[0], ai_ref[0]
        d_ar = jnp.zeros((1, sb), F32)
        d_ai = jnp.zeros((1, sb), F32)
        for tau in range(CHUNK, 0, -1):
            d_ar = d_ar + dwr[tau] * w_r[tau - 1] + dwi[tau] * w_i[tau - 1]
            d_ai = d_ai - dwr[tau] * w_i[tau - 1] + dwi[tau] * w_r[tau - 1]
            dwr[tau - 1], dwi[tau - 1] = (dwr[tau - 1] + dwr[tau] * a_r + dwi[tau] * a_i,
                                          dwi[tau - 1] - dwr[tau] * a_i + dwi[tau] * a_r)
        dbre_ref[0] = d_bre
        dbim_ref[0] = d_bim
        dcre_ref[0] = d_cre
        dcim_ref[0] = d_cim
        da_ref[0, :, 0:sb] = d_ar
        da_ref[0, :, sb:2 * sb] = d_ai

    mat = pl.BlockSpec((1, LANES, sb), lambda g: (g, 0, 0))
    pw = pl.BlockSpec((1, CHUNK + 1, sb), lambda g: (g, 0, 0))
    one = pl.BlockSpec((1, 1, sb), lambda g: (g, 0, 0))
    two = pl.BlockSpec((1, 1, 2 * sb), lambda g: (g, 0, 0))
    big = pl.BlockSpec((1, n_rows, n_rows), lambda g: (g, 0, 0))
    big2 = pl.BlockSpec((1, n_rows, 2 * sb), lambda g: (g, 0, 0))
    mshape = jax.ShapeDtypeStruct((ngb, LANES, sb), F32)
    return pl.pallas_call(
        body, name=name, grid=(ngb,),
        out_shape=(mshape, mshape, mshape, mshape, jax.ShapeDtypeStruct((ngb, 1, 2 * sb), F32)),
        in_specs=[mat, mat, mat, mat, pw, pw, one, one, big, big2, big2, two],
        out_specs=(mat, mat, mat, mat, two),
        compiler_params=_params(1),
    )(bre, bim, cre, cim, wr, wi, ar, ai, dt, dbp, dcp, da8)


def _shift_rows(xv, edge, rows, n_rows, down):
    if down:
        return jnp.where(rows == 0, edge, pltpu.roll(xv, 1, 0))
    return jnp.where(rows == n_rows - 1, edge, pltpu.roll(xv, n_rows - 1, 0))


def _rows_of_tokens(tok_ref, conv_scr, rb):
    conv_scr[...] = tok_ref[0].astype(F32)
    return jnp.concatenate([conv_scr[pl.ds(j, rb, stride=CHUNK), :] for j in range(CHUNK)], axis=1).astype(BF16)


def _tokens_of_rows(val, tok_ref, conv_scr, rb):
    for j in range(CHUNK):
        conv_scr[pl.ds(j, rb, stride=CHUNK), :] = val[:, j * LANES:(j + 1) * LANES]
    tok_ref[0] = conv_scr[...].astype(BF16)


def _s5_row_block(n_seq, target=416):
    n_rows = n_seq // CHUNK
    best = 16
    for rb in range(16, min(target, n_rows) + 1, 16):
        if n_rows % rb == 0:
            best = rb
    assert n_rows % best == 0
    return best


def _s5_fwd(useq, t_op, bp, cp, tab, *, reverse, name):
    ngb, n_seq, _ = useq.shape
    sb = bp.shape[2] // 2
    width = CHUNK * LANES
    rb = _s5_row_block(n_seq)
    tbk = rb * CHUNK
    steps = n_seq // tbk

    def blk(i):
        return (steps - 1 - i) if reverse else i

    def body(u_ref, t_ref, b_ref, c_ref, tab_ref, y_ref, hc_ref, h_scr, conv_scr, carry_scr):
        i = pl.program_id(1)

        @pl.when(i == 0)
        def _():
            carry_scr[...] = jnp.zeros_like(carry_scr)

        hc_ref[0, 0] = carry_scr[...]
        enter = carry_scr[0:1, :]
        uv = _rows_of_tokens(u_ref, conv_scr, rb)
        h_scr[...] = _dot(uv, b_ref[0])
        cr, ci = _scan_block(h_scr, tab_ref.at[0], carry_scr[:, 0:sb], carry_scr[:, sb:2 * sb],
                             reverse=reverse, tb=rb, sb=sb)
        carry_scr[:, 0:sb] = cr
        carry_scr[:, sb:2 * sb] = ci
        rows = lax.broadcasted_iota(jnp.int32, (rb, 1), 0)
        hprev = _shift_rows(h_scr[...], enter, rows, rb, down=not reverse)
        _tokens_of_rows(_dot(uv, t_ref[0]) + _dot_nt(hprev.astype(BF16), c_ref[0]), y_ref, conv_scr, rb)

    op = pl.BlockSpec((1, width, width), lambda g, i: (g, 0, 0))
    op2 = pl.BlockSpec((1, width, 2 * sb), lambda g, i: (g, 0, 0))
    tok = pl.BlockSpec((1, tbk, LANES), lambda g, i: (g, blk(i), 0))
    return pl.pallas_call(
        body, name=name, grid=(ngb, steps),
        out_shape=(jax.ShapeDtypeStruct((ngb, n_seq, LANES), BF16),
                   jax.ShapeDtypeStruct((ngb, steps, SUBLANES, 2 * sb), F32)),
        in_specs=[tok, op, op2, op2, pl.BlockSpec((1, 8, SUBLANES, sb), lambda g, i: (g, 0, 0, 0))],
        out_specs=(tok, pl.BlockSpec((1, 1, SUBLANES, 2 * sb), lambda g, i: (g, i, 0, 0))),
        scratch_shapes=[pltpu.VMEM((rb, 2 * sb), F32), pltpu.VMEM((tbk, LANES), F32),
                        pltpu.VMEM((SUBLANES, 2 * sb), F32)],
        compiler_params=_params(2),
    )(useq, t_op, bp, cp, tab)


def _s5_bwd(useq, dy, hc, t_op, bp, cp, tab, tab_adj, *, reverse, name):
    ngb, n_seq, _ = useq.shape
    sb = bp.shape[2] // 2
    width = CHUNK * LANES
    rb = _s5_row_block(n_seq)
    tbk = rb * CHUNK
    steps = n_seq // tbk

    def fwd_step(i):
        return steps - 1 - i

    def blk(i):
        s = fwd_step(i)
        return (steps - 1 - s) if reverse else s

    def body(u_ref, dy_ref, hc_ref, t_ref, b_ref, c_ref, tab_ref, taba_ref, du_ref, dt_ref, db_ref, dc_ref, da_ref,
             h_scr, lam_scr, conv_scr, lcarry_scr, gedge_scr, da_scr):
        i = pl.program_id(1)
        first = i == 0

        @pl.when(first)
        def _():
            lcarry_scr[...] = jnp.zeros_like(lcarry_scr)
            gedge_scr[...] = jnp.zeros_like(gedge_scr)
            da_scr[...] = jnp.zeros_like(da_scr)

        rows = lax.broadcasted_iota(jnp.int32, (rb, 1), 0)
        uv = _rows_of_tokens(u_ref, conv_scr, rb)
        h_scr[...] = _dot(uv, b_ref[0])
        _scan_block(h_scr, tab_ref.at[0], hc_ref[0, 0, :, 0:sb], hc_ref[0, 0, :, sb:2 * sb], reverse=reverse, tb=rb, sb=sb)
        hprev = _shift_rows(h_scr[...], hc_ref[0, 0, 0:1, :], rows, rb, down=not reverse)

        dyv = _rows_of_tokens(dy_ref, conv_scr, rb)
        gy = _dot(dyv, c_ref[0])
        edge = gy[rb - 1:rb, :] if reverse else gy[0:1, :]
        lam_scr[...] = _shift_rows(gy, gedge_scr[...], rows, rb, down=reverse)
        gedge_scr[...] = edge
        lr, li = _scan_block(lam_scr, taba_ref.at[0], lcarry_scr[:, 0:sb], lcarry_scr[:, sb:2 * sb],
                             reverse=not reverse, tb=rb, sb=sb)
        lcarry_scr[:, 0:sb] = lr
        lcarry_scr[:, sb:2 * sb] = li

        lam = lam_scr[...]
        lam_bf = lam.astype(BF16)
        _tokens_of_rows(_dot_nt(dyv, t_ref[0]) + _dot_nt(lam_bf, b_ref[0]), du_ref, conv_scr, rb)
        _acc(dt_ref.at[0], first, _dot_tn(uv, dyv))
        _acc(db_ref.at[0], first, _dot_tn(uv, lam_bf))
        _acc(dc_ref.at[0], first, _dot_tn(dyv, hprev.astype(BF16)))
        lam_r, lam_i = lam[:, 0:sb], lam[:, sb:2 * sb]
        hp_r, hp_i = hprev[:, 0:sb], hprev[:, sb:2 * sb]
        da_scr[:, 0:sb] += _colsum(lam_r * hp_r + lam_i * hp_i)
        da_scr[:, sb:2 * sb] += _colsum(lam_i * hp_r - lam_r * hp_i)

        @pl.when(i == steps - 1)
        def _():
            da_ref[0] = da_scr[...]

    op = pl.BlockSpec((1, width, width), lambda g, i: (g, 0, 0))
    op2 = pl.BlockSpec((1, width, 2 * sb), lambda g, i: (g, 0, 0))
    tabs = pl.BlockSpec((1, 8, SUBLANES, sb), lambda g, i: (g, 0, 0, 0))
    tok = pl.BlockSpec((1, tbk, LANES), lambda g, i: (g, blk(i), 0))
    return pl.pallas_call(
        body, name=name, grid=(ngb, steps),
        out_shape=(jax.ShapeDtypeStruct((ngb, n_seq, LANES), BF16),
                   jax.ShapeDtypeStruct((ngb, width, width), F32),
                   jax.ShapeDtypeStruct((ngb, width, 2 * sb), F32),
                   jax.ShapeDtypeStruct((ngb, width, 2 * sb), F32),
                   jax.ShapeDtypeStruct((ngb, 1, 2 * sb), F32)),
        in_specs=[tok, tok, pl.BlockSpec((1, 1, SUBLANES, 2 * sb), lambda g, i: (g, fwd_step(i), 0, 0)),
                  op, op2, op2, tabs, tabs],
        out_specs=(tok, op, op2, op2, pl.BlockSpec((1, 1, 2 * sb), lambda g, i: (g, 0, 0))),
        scratch_shapes=[pltpu.VMEM((rb, 2 * sb), F32), pltpu.VMEM((rb, 2 * sb), F32), pltpu.VMEM((tbk, LANES), F32),
                        pltpu.VMEM((SUBLANES, 2 * sb), F32), pltpu.VMEM((1, 2 * sb), F32), pltpu.VMEM((1, 2 * sb), F32)],
        compiler_params=_params(2),
    )(useq, dy, hc, t_op, bp, cp, tab, tab_adj)


def _glu_loss(useq, yf, yb, z, xhat0, ln0, gt, d_vec, w_glu, b_glu, w_out, ln1, target, *, offs, dy_rows, tb, name):
    ngb = useq.shape[0]
    n_tok, d_model = xhat0.shape
    e = ngb * LANES
    tb = min(tb, n_tok)
    assert all(off % tb == 0 for off in offs) and all(off % tb == 0 for _, off in dy_rows)
    nz = z.shape[0]

    def body(u_ref, yf_ref, yb_ref, z_ref, xh0_ref, g0_ref, b0_ref, gt_ref, d_ref, wg_ref, bg_ref, wo_ref, g1_ref,
             b1_ref, t_ref, loss_ref, dxr_ref, do_ref, gz_ref, gg_ref, dq_ref, dz_ref, dyf_ref, dyb_ref, dg1_ref, db1_ref,
             dgt_ref, dbg_ref, dd_ref, loss_scr, yl_scr, th_scr, s_scr, dg_scr):
        i = pl.program_id(0)
        first = i == 0
        zw = e // nz
        cs = min(512, zw)

        def z_slab(c0):
            return z_ref[c0 // zw, :, c0 % zw:c0 % zw + cs].astype(F32)

        for q in range(ngb):
            sl = slice(q * LANES, (q + 1) * LANES)
            yl = d_ref[:, sl] * u_ref[q].astype(F32) + yf_ref[q].astype(F32) + yb_ref[q].astype(F32)
            th = jnp.tanh(GELU_K * (yl + GELU_C * yl * yl * yl))
            yl_scr[:, sl] = yl
            th_scr[:, sl] = th
            gg_ref[:, sl] = (0.5 * yl * (1.0 + th)).astype(BF16)
        s_scr[...] = _sigmoid(_dot(gg_ref[...], wg_ref[...]) + bg_ref[...])
        for c0 in range(0, e, cs):
            sl = slice(c0, c0 + cs)
            zf = z_slab(c0)
            g2 = 0.5 * yl_scr[:, sl] * (1.0 + th_scr[:, sl]) * s_scr[:, sl]
            gz_ref[:, sl] = (g2 * (zf * _sigmoid(zf))).astype(BF16)
        o = _dot(gz_ref[...], wo_ref[...])
        x1 = xh0_ref[...] * g0_ref[...] + b0_ref[...]
        r = DN_ALPHA * x1 + gt_ref[...] * o
        rc = r - _rowmean(r)
        rstd = lax.rsqrt(_rowmean(rc * rc) + LN_EPS)
        xh = rc * rstd
        err = xh * g1_ref[...] + b1_ref[...] - t_ref[...]
        _acc(loss_scr, first, _colsum(err * err))
        dy = err * (1.0 / d_model)
        _acc(dg1_ref, first, _colsum(dy * xh))
        _acc(db1_ref, first, _colsum(dy))
        dxh = dy * g1_ref[...]
        dr = rstd * (dxh - _rowmean(dxh) - xh * _rowmean(dxh * xh))
        dxr_ref[...] = DN_ALPHA * dr
        _acc(dgt_ref, first, _colsum(dr * o))
        do_bf = (dr * gt_ref[...]).astype(BF16)
        do_ref[...] = do_bf
        dg_scr[...] = _dot_nt(do_bf, wo_ref[...])
        for c0 in range(0, e, cs):
            sl = slice(c0, c0 + cs)
            zf = z_slab(c0)
            sz = _sigmoid(zf)
            g = 0.5 * yl_scr[:, sl] * (1.0 + th_scr[:, sl])
            s = s_scr[:, sl]
            dgz = dg_scr[:, sl]
            dg2 = dgz * (zf * sz)
            dz_ref[:, sl] = (dgz * (g * s) * (sz * (1.0 + zf * (1.0 - sz)))).astype(BF16)
            dq = dg2 * g * s * (1.0 - s)
            _acc(dbg_ref.at[:, sl], first, _colsum(dq))
            dq_ref[:, sl] = dq.astype(BF16)
            dg_scr[:, sl] = dg2 * s
        dg_scr[...] += _dot_nt(dq_ref[...], wg_ref[...])
        for q in range(ngb):
            sl = slice(q * LANES, (q + 1) * LANES)
            yl = yl_scr[:, sl]
            th = th_scr[:, sl]
            dgelu = 0.5 * (1.0 + th) + 0.5 * yl * (1.0 - th * th) * (GELU_K * (1.0 + 3.0 * GELU_C * yl * yl))
            dyl = dg_scr[:, sl] * dgelu
            _acc(dd_ref.at[:, sl], first, _colsum(dyl * u_ref[q].astype(F32)))
            dyf_ref[q] = dyl.astype(BF16)
            dyb_ref[q] = dyl.astype(BF16)

        @pl.when(i == pl.num_programs(0) - 1)
        def _():
            loss_ref[...] = (0.5 / d_model) * jnp.sum(loss_scr[...], axis=1, keepdims=True)

    vec = pl.BlockSpec((1, d_model), lambda i: (0, 0))
    evec = pl.BlockSpec((1, e), lambda i: (0, 0))
    tok = pl.BlockSpec((tb, d_model), lambda i: (i, 0))
    wide = pl.BlockSpec((tb, e), lambda i: (i, 0))
    def gblk(off):
        return pl.BlockSpec((ngb, tb, LANES), functools.partial(lambda i, ob: (0, i + ob, 0), ob=off // tb))

    once = dict(pipeline_mode=pl.Buffered(1))
    tok_f = jax.ShapeDtypeStruct((n_tok, d_model), F32)
    tok_b = jax.ShapeDtypeStruct((n_tok, d_model), BF16)
    wide_b = jax.ShapeDtypeStruct((n_tok, e), BF16)
    vec_f = jax.ShapeDtypeStruct((1, d_model), F32)
    evec_f = jax.ShapeDtypeStruct((1, e), F32)
    return pl.pallas_call(
        body, name=name, grid=(n_tok // tb,),
        out_shape=(jax.ShapeDtypeStruct((1, 1), F32), tok_f, tok_b, wide_b, wide_b, wide_b, wide_b,
                   *[jax.ShapeDtypeStruct((ngb, total, LANES), BF16) for total, _ in dy_rows],
                   vec_f, vec_f, vec_f, evec_f, evec_f),
        in_specs=[gblk(offs[0]), gblk(offs[1]), gblk(offs[2]),
                  pl.BlockSpec((nz, tb, e // nz), lambda i: (0, i, 0)), tok, vec, vec, vec, evec,
                  pl.BlockSpec((e, e), lambda i: (0, 0), **once), evec,
                  pl.BlockSpec((e, d_model), lambda i: (0, 0), **once), vec, vec, tok],
        out_specs=(pl.BlockSpec((1, 1), lambda i: (0, 0)), tok, tok, wide, wide, wide, wide,
                   *[gblk(off) for _, off in dy_rows], vec, vec, vec, evec, evec),
        scratch_shapes=[pltpu.VMEM((1, d_model), F32)] + [pltpu.VMEM((tb, e), F32)] * 4,
        compiler_params=_params(1),
    )(useq, yf, yb, z, xhat0, ln0[0], ln0[1], gt, d_vec, w_glu, b_glu, w_out, ln1[0], ln1[1], target)


def _ssm_inbwd(duf, dub, w, xhat, rstd, ln, sc, gt_prev, f_prev, *, lat, row_f, row_b, tb, name):
    ngb = duf.shape[0]
    e = ngb * LANES
    n_tok, d_model = xhat.shape
    tb = min(tb, n_tok)
    obf, obb = row_f // tb, row_b // tb
    has_lat = lat is not None
    n_w = w.shape[0] if has_lat else w.shape[0] // 2

    def body(*refs):
        if has_lat:
            (duf_ref, dub_ref, dyl_ref, dz_ref, d_ref, dxr_ref, w_ref, xh_ref, rs_ref, g_ref, b_ref, sc_ref, gt_ref,
             f_ref, dp_ref, dr_ref, df_ref, dsc_ref, dsh_ref, dg_ref, db_ref, dgt_ref) = refs
        else:
            (duf_ref, dub_ref, w_ref, xh_ref, rs_ref, g_ref, b_ref, sc_ref, gt_ref, f_ref, dp_ref, dr_ref, df_ref,
             dsc_ref, dsh_ref, dg_ref, db_ref, dgt_ref) = refs
        first = pl.program_id(0) == 0
        du = (jnp.concatenate([duf_ref[q] for q in range(ngb)], axis=1).astype(F32)
              + jnp.concatenate([dub_ref[q] for q in range(ngb)], axis=1).astype(F32))
        if has_lat:
            du = du + d_ref[...] * jnp.concatenate([dyl_ref[q] for q in range(ngb)], axis=1).astype(F32)
            dp_ref[:, e:2 * e] = dz_ref[...]
        else:
            dp_ref[:, e:2 * e] = jnp.zeros((tb, e), BF16)
        dp_ref[:, 0:e] = du.astype(BF16)
        dh = jnp.zeros((tb, d_model), F32)
        for j in range(n_w):
            dh = dh + _dot_nt(dp_ref[:, j * d_model:(j + 1) * d_model], w_ref[j])
        xh = xh_ref[...]
        x1 = xh * g_ref[...] + b_ref[...]
        dx1 = dh * (1.0 + sc_ref[...])
        if has_lat:
            dx1 = dx1 + dxr_ref[...]
        _acc(dsc_ref, first, _colsum(dh * x1))
        _acc(dsh_ref, first, _colsum(dh))
        _acc(dg_ref, first, _colsum(dx1 * xh))
        _acc(db_ref, first, _colsum(dx1))
        dxh = dx1 * g_ref[...]
        dr = rs_ref[...] * (dxh - _rowmean(dxh) - xh * _rowmean(dxh * xh))
        dr_ref[...] = dr
        df_ref[...] = (dr * gt_ref[...]).astype(BF16)
        _acc(dgt_ref, first, _colsum(dr * f_ref[...].astype(F32)))

    vec = pl.BlockSpec((1, d_model), lambda i: (0, 0))
    tok = pl.BlockSpec((tb, d_model), lambda i: (i, 0))
    gblk = pl.BlockSpec((ngb, tb, LANES), lambda i: (0, i, 0))
    in_specs = [pl.BlockSpec((ngb, tb, LANES), lambda i: (0, i + obf, 0)),
                pl.BlockSpec((ngb, tb, LANES), lambda i: (0, i + obb, 0))]
    args = [duf, dub]
    if has_lat:
        in_specs += [gblk, pl.BlockSpec((tb, e), lambda i: (i, 0)), pl.BlockSpec((1, e), lambda i: (0, 0)), tok]
        args += list(lat)
    in_specs += [pl.BlockSpec(w.shape, lambda i: (0, 0, 0)), tok, pl.BlockSpec((tb, 1), lambda i: (i, 0)), vec, vec, vec,
                 vec, tok]
    args += [w, xhat, rstd, ln[0], ln[1], sc, gt_prev, f_prev]
    vec_f = jax.ShapeDtypeStruct((1, d_model), F32)
    return pl.pallas_call(
        body, name=name, grid=(n_tok // tb,),
        out_shape=(jax.ShapeDtypeStruct((n_tok, 2 * e), BF16), jax.ShapeDtypeStruct((n_tok, d_model), F32),
                   jax.ShapeDtypeStruct((n_tok, d_model), BF16), vec_f, vec_f, vec_f, vec_f, vec_f),
        in_specs=in_specs,
        out_specs=(pl.BlockSpec((tb, 2 * e), lambda i: (i, 0)), tok, tok, vec, vec, vec, vec, vec),
        compiler_params=_params(1),
    )(*args)


def _conv_bwd_a(df, w_out, p, yc, *, tb, name):
    _, n_tok, e = p.shape
    d_model = df.shape[1]
    tb = min(tb, n_tok)
    cs = _slab_width(e)

    def body(df_ref, wo_ref, bg_ref, z_ref, yc_ref, dbg_ref, dz_ref, dyc_ref):
        dfv = df_ref[...]
        for c0 in range(0, e, cs):
            sl = slice(c0, c0 + cs)
            dgv = _dot_nt(dfv, wo_ref[sl, :])
            zf = z_ref[0, :, sl].astype(F32)
            sz = _sigmoid(zf)
            silu_z = zf * sz
            bg = bg_ref[0, :, sl].astype(F32)
            yc = yc_ref[:, sl].astype(F32)
            dbg_ref[:, sl] = (dgv * yc * silu_z).astype(BF16)
            dyc_ref[:, sl] = (dgv * bg * silu_z).astype(BF16)
            dz_ref[:, sl] = (dgv * bg * yc * (sz * (1.0 + zf * (1.0 - sz)))).astype(BF16)

    wide = pl.BlockSpec((tb, e), lambda i: (i, 0))
    shape = jax.ShapeDtypeStruct((n_tok, e), BF16)
    return pl.pallas_call(
        body, name=name, grid=(n_tok // tb,), out_shape=(shape, shape, shape),
        in_specs=[pl.BlockSpec((tb, d_model), lambda i: (i, 0)), pl.BlockSpec((e, d_model), lambda i: (0, 0)),
                  pl.BlockSpec((1, tb, e), lambda i: (0, i, 0)), pl.BlockSpec((1, tb, e), lambda i: (3, i, 0)), wide],
        out_specs=(wide, wide, wide), compiler_params=_params(1),
    )(df, w_out, p, p, yc)


def _conv_bwd_b(dyc, p, dbg, dz, conv_w, *, grid_mode, tb, name):
    _, n_tok, e = p.shape
    eh = e // 2
    if not grid_mode:
        tb = n_tok
    tb = min(tb, n_tok)
    nb = n_tok // tb
    hb = tb // GRID_W
    cs = _slab_width(e)

    def body(*refs):
        if grid_mode:
            dyc_ref, dycp_ref, dycn_ref, cg_ref, v_ref, dbg_ref, dz_ref, cw_ref, dp_ref, dcw_ref = refs
        else:
            dyc_ref, cg_ref, v_ref, dbg_ref, dz_ref, cw_ref, dp_ref, dcw_ref = refs
        i = pl.program_id(0)
        first = i == 0
        rows = lax.broadcasted_iota(jnp.int32, (tb, 1), 0)
        dp_ref[0] = dbg_ref[...]
        dp_ref[3] = dz_ref[...]
        for c0 in range(0, e, cs):
            sl = slice(c0, c0 + cs)
            dyc = dyc_ref[:, sl].astype(F32)
            w = cw_ref[:, sl]
            if grid_mode and c0 >= eh:
                hs = slice(c0 - eh, c0 - eh + cs)
                dprev = jnp.where(i > 0, dycp_ref[:, hs].astype(F32), 0.0)
                dnext = jnp.where(i < nb - 1, dycn_ref[:, hs].astype(F32), 0.0)
                if tb > GRID_W:
                    dm = jnp.concatenate([dprev, dyc[:tb - GRID_W]], axis=0)
                    dpl = jnp.concatenate([dyc[GRID_W:], dnext], axis=0)
                else:
                    dm, dpl = dprev, dnext
            else:
                dm, dpl = _shifted(dyc, rows, GRID_W if grid_mode else tb, tb)
            cg = cg_ref[0, :, sl].astype(F32)
            v = v_ref[0, :, sl].astype(F32)
            u = cg * v
            du = w[0:1] * dpl + w[1:2] * dyc + w[2:3] * dm
            dp_ref[1, :, sl] = (du * v).astype(BF16)
            dp_ref[2, :, sl] = (du * cg).astype(BF16)
            _acc(dcw_ref.at[:, sl], first, jnp.concatenate([_colsum(u * dpl), _colsum(u * dyc), _colsum(u * dm)], axis=0))

    n_hrows = n_tok // GRID_W
    wide = pl.BlockSpec((tb, e), lambda i: (i, 0))
    in_specs = [wide]
    args = [dyc]
    if grid_mode:
        in_specs += [pl.BlockSpec((GRID_W, eh), lambda i: (jnp.maximum(i * hb - 1, 0), 1)),
                     pl.BlockSpec((GRID_W, eh), lambda i: (jnp.minimum((i + 1) * hb, n_hrows - 1), 1))]
        args += [dyc, dyc]
    in_specs += [pl.BlockSpec((1, tb, e), lambda i: (1, i, 0)), pl.BlockSpec((1, tb, e), lambda i: (2, i, 0)), wide, wide,
                 pl.BlockSpec((3, e), lambda i: (0, 0))]
    args += [p, p, dbg, dz, conv_w]
    return pl.pallas_call(
        body, name=name, grid=(nb,),
        out_shape=(jax.ShapeDtypeStruct((4, n_tok, e), BF16), jax.ShapeDtypeStruct((3, e), F32)),
        in_specs=in_specs,
        out_specs=(pl.BlockSpec((4, tb, e), lambda i: (0, i, 0)), pl.BlockSpec((3, e), lambda i: (0, 0))),
        compiler_params=_params(1),
    )(*args)


def _conv_inbwd(dp, w, dr, x, sc, *, tb, name):
    n_chunks, n_tok, e = dp.shape
    d_model = x.shape[1]
    tb = min(tb, n_tok)

    def body(dp_ref, w_ref, dr_ref, x_ref, sc_ref, gx_ref, dsc_ref, dsh_ref, dh_scr):
        k = pl.program_id(1)
        first = pl.program_id(0) == 0
        _acc(dh_scr, k == 0, _dot_nt(dp_ref[0], w_ref[0]))

        @pl.when(k == n_chunks - 1)
        def _():
            dh = dh_scr[...]
            gx_ref[...] = DN_ALPHA * dr_ref[...] + dh * (1.0 + sc_ref[...])
            _acc(dsc_ref, first, _colsum(dh * x_ref[...]))
            _acc(dsh_ref, first, _colsum(dh))

    vec = pl.BlockSpec((1, d_model), lambda i, k: (0, 0))
    tok = pl.BlockSpec((tb, d_model), lambda i, k: (i, 0))
    vec_f = jax.ShapeDtypeStruct((1, d_model), F32)
    return pl.pallas_call(
        body, name=name, grid=(n_tok // tb, n_chunks),
        out_shape=(jax.ShapeDtypeStruct((n_tok, d_model), F32), vec_f, vec_f),
        in_specs=[pl.BlockSpec((1, tb, e), lambda i, k: (k, i, 0)), pl.BlockSpec((1, d_model, e), lambda i, k: (k, 0, 0)),
                  tok, tok, vec],
        out_specs=(tok, vec, vec),
        scratch_shapes=[pltpu.VMEM((tb, d_model), F32)],
        compiler_params=_params(2),
    )(dp, w, dr, x, sc)


def _wgrad(a, b, *, n_chunks, tm, tl, init=None, name):
    n_tok, m = a.shape
    tl = min(tl, n_tok)
    chunked = b.ndim == 3
    cw = b.shape[2] if chunked else b.shape[1] // n_chunks
    has_init = init is not None

    def body(*refs):
        if has_init:
            a_ref, b_ref, init_ref, o_ref = refs
        else:
            a_ref, b_ref, o_ref = refs
        bv = b_ref[0] if chunked else b_ref[...]
        part = _dot_tn(a_ref[...], bv)
        l = pl.program_id(2)

        @pl.when(l == 0)
        def _():
            o_ref[0] = part + init_ref[0] if has_init else part

        @pl.when(l > 0)
        def _():
            o_ref[0] += part

    o_spec = pl.BlockSpec((1, tm, cw), lambda jm, jc, l: (jc, jm, 0))
    b_spec = (pl.BlockSpec((1, tl, cw), lambda jm, jc, l: (jc, l, 0)) if chunked
              else pl.BlockSpec((tl, cw), lambda jm, jc, l: (l, jc)))
    in_specs = [pl.BlockSpec((tl, tm), lambda jm, jc, l: (l, jm)), b_spec] + ([o_spec] if has_init else [])
    args = (a, b) + ((init,) if has_init else ())
    return pl.pallas_call(
        body, name=name, grid=(m // tm, n_chunks, n_tok // tl),
        out_shape=jax.ShapeDtypeStruct((n_chunks, m, cw), F32),
        in_specs=in_specs, out_specs=o_spec, compiler_params=_params(3),
    )(*args)


def _block_diag(t, ngb):
    g, p, n = t.shape
    gpb = g // ngb
    eye = jnp.eye(gpb, dtype=t.dtype)
    return jnp.einsum("bgpn,gh->bgphn", t.reshape(ngb, gpb, p, n), eye).reshape(ngb, gpb * p, gpb * n)


def _block_diag_t(mat, g, p, n):
    ngb = mat.shape[0]
    gpb = g // ngb
    eye = jnp.eye(gpb, dtype=mat.dtype)
    return jnp.einsum("bgphn,gh->bgpn", mat.reshape(ngb, gpb, p, gpb, n), eye).reshape(g, p, n)


def _scan_tables(pw_r, pw_i, ngb, reverse):
    _, g, n = pw_r.shape
    sb = g * n // ngb
    rows = jnp.arange(SUBLANES)
    kinds = []
    for step in (1, 2, 4):
        mask = ((rows < SUBLANES - step) if reverse else (rows >= step)).astype(F32)
        for part in (pw_r[step - 1], pw_i[step - 1]):
            kinds.append(part.reshape(ngb, 1, sb) * mask[None, :, None])
    for part in (pw_r, pw_i):
        pw = part[::-1] if reverse else part
        kinds.append(jnp.transpose(pw.reshape(SUBLANES, ngb, sb), (1, 0, 2)))
    return jnp.stack(kinds, axis=1)


def _flat(parts):
    return jnp.concatenate([p.reshape(-1) for p in parts])


def _unflat(vec, shapes):
    out, off = [], 0
    for s in shapes:
        size = math.prod(s)
        out.append(vec[off:off + size].reshape(s))
        off += size
    return out


def kernel(x, c, ctx, c_ctx, ada_w, ada_b, ln_g, ln_b, conv_w_in, conv_w, conv_w_out, ssm_w_in, ssm_lam_re, ssm_lam_im, ssm_log_step, ssm_b_re, ssm_b_im, ssm_c_re, ssm_c_im, ssm_d, ssm_w_glu, ssm_b_glu, ssm_w_out, loss_target, m_c_ctx, m_ada_w, m_ada_b, m_ln_g, m_ln_b, m_conv_w_in, m_conv_w, m_conv_w_out, m_ssm_w_in, m_ssm_lam_re, m_ssm_lam_im, m_ssm_log_step, m_ssm_b_re, m_ssm_b_im, m_ssm_c_re, m_ssm_c_im, m_ssm_d, m_ssm_w_glu, m_ssm_b_glu, m_ssm_w_out, v_c_ctx, v_ada_w, v_ada_b, v_ln_g, v_ln_b, v_conv_w_in, v_conv_w, v_conv_w_out, v_ssm_w_in, v_ssm_lam_re, v_ssm_lam_im, v_ssm_log_step, v_ssm_b_re, v_ssm_b_im, v_ssm_c_re, v_ssm_c_im, v_ssm_d, v_ssm_w_glu, v_ssm_b_glu, v_ssm_w_out):
    weights = dict(c_ctx=c_ctx, ada_w=ada_w, ada_b=ada_b, ln_g=ln_g, ln_b=ln_b, conv_w_in=conv_w_in, conv_w=conv_w,
                   conv_w_out=conv_w_out, ssm_w_in=ssm_w_in, ssm_lam_re=ssm_lam_re, ssm_lam_im=ssm_lam_im,
                   ssm_log_step=ssm_log_step, ssm_b_re=ssm_b_re, ssm_b_im=ssm_b_im, ssm_c_re=ssm_c_re,
                   ssm_c_im=ssm_c_im, ssm_d=ssm_d, ssm_w_glu=ssm_w_glu, ssm_b_glu=ssm_b_glu, ssm_w_out=ssm_w_out)
    mom_m = dict(c_ctx=m_c_ctx, ada_w=m_ada_w, ada_b=m_ada_b, ln_g=m_ln_g, ln_b=m_ln_b, conv_w_in=m_conv_w_in,
                 conv_w=m_conv_w, conv_w_out=m_conv_w_out, ssm_w_in=m_ssm_w_in, ssm_lam_re=m_ssm_lam_re,
                 ssm_lam_im=m_ssm_lam_im, ssm_log_step=m_ssm_log_step, ssm_b_re=m_ssm_b_re, ssm_b_im=m_ssm_b_im,
                 ssm_c_re=m_ssm_c_re, ssm_c_im=m_ssm_c_im, ssm_d=m_ssm_d, ssm_w_glu=m_ssm_w_glu,
                 ssm_b_glu=m_ssm_b_glu, ssm_w_out=m_ssm_w_out)
    mom_v = dict(c_ctx=v_c_ctx, ada_w=v_ada_w, ada_b=v_ada_b, ln_g=v_ln_g, ln_b=v_ln_b, conv_w_in=v_conv_w_in,
                 conv_w=v_conv_w, conv_w_out=v_conv_w_out, ssm_w_in=v_ssm_w_in, ssm_lam_re=v_ssm_lam_re,
                 ssm_lam_im=v_ssm_lam_im, ssm_log_step=v_ssm_log_step, ssm_b_re=v_ssm_b_re, ssm_b_im=v_ssm_b_im,
                 ssm_c_re=v_ssm_c_re, ssm_c_im=v_ssm_c_im, ssm_d=v_ssm_d, ssm_w_glu=v_ssm_w_glu,
                 ssm_b_glu=v_ssm_b_glu, ssm_w_out=v_ssm_w_out)
    names = list(weights)

    n_lat, d_model = x.shape[1], x.shape[2]
    n_ctx = ctx.shape[1]
    e = 2 * d_model
    n_grp, n_state, grp = ssm_lam_re.shape[2], ssm_lam_re.shape[3], ssm_b_re.shape[4]
    ngb = e // LANES
    ws = ada_w.shape[2]
    tb_tok = min(512, n_lat)
    n_seq = n_ctx + n_lat
    tb_glu = math.gcd(256, n_ctx)
    chip = 2 * lax.axis_index("x") + lax.axis_index("y")
    me = 2 * chip + lax.axis_index("c")
    chips, everyone, pair = ("x", "y"), MESH_AXES, ("c",)

    x2, ctx2, tgt2 = x[0], ctx[0], loss_target[0]

    wc_in = _exchange(conv_w_in[0].astype(BF16), chips, False, "ag_conv_w_in")
    wc_out = _exchange(conv_w_out[0].astype(BF16), chips, False, "ag_conv_w_out").reshape(e, d_model)
    ws_in = _exchange(ssm_w_in[0].astype(BF16), chips, False, "ag_ssm_w_in")
    w_glu = _exchange(ssm_w_glu[0].astype(BF16), chips, False, "ag_ssm_w_glu").reshape(e, e)
    ws_out = _exchange(ssm_w_out[0].astype(BF16), chips, False, "ag_ssm_w_out").reshape(e, d_model)
    small_full = _exchange(_flat([conv_w[0], ssm_d[0], ssm_b_glu[0]]).reshape(1, -1), chips, False, "ag_small")
    es = conv_w.shape[2]
    conv_w_full = jnp.transpose(small_full[:, 0, :3 * es].reshape(4, 3, es), (1, 0, 2)).reshape(3, e)
    d_full = small_full[:, 0, 3 * es:4 * es].reshape(1, e)
    b_glu_full = small_full[:, 0, 4 * es:5 * es].reshape(1, e)

    c_all = _exchange(c, everyone, False, "ag_c").reshape(8, d_model)
    cc2 = c_ctx.reshape(1, d_model)
    b_sh = lax.dynamic_slice_in_dim(ada_b, chip * ws, ws, axis=1).reshape(DEPTH, 1, ws)
    m_sh = _ada_fwd(c_all, cc2, ada_w, b_sh)
    m_all = _exchange(m_sh, chips, False, "ag_mod")
    m_full = jnp.transpose(m_all, (1, 2, 0, 3)).reshape(DEPTH, 16, 3 * d_model)
    m_lat = lax.dynamic_slice_in_dim(m_full, me, 1, axis=1)
    m_ctx = m_full[:, 8:9]

    def mods(m, i):
        return m[i, :, 0:d_model], m[i, :, d_model:2 * d_model], m[i, :, 2 * d_model:3 * d_model]

    sh0, sc0, gt0 = mods(m_lat, 0)
    sh1, sc1, gt1 = mods(m_lat, 1)
    shc0, scc0, gtc0 = mods(m_ctx, 0)
    shc1, scc1, _ = mods(m_ctx, 1)
    ln0 = (ln_g[0:1], ln_b[0:1])
    ln1 = (ln_g[1:2], ln_b[1:2])

    rg = 2 * n_grp
    lam_re2 = ssm_lam_re[0].reshape(rg, n_state)
    lam_im2 = ssm_lam_im[0].reshape(rg, n_state)
    log_step2 = ssm_log_step[0].reshape(rg, 1)
    b_re_t = jnp.transpose(ssm_b_re[0], (3, 0, 1, 2)).reshape(grp, rg, n_state)
    b_im_t = jnp.transpose(ssm_b_im[0], (3, 0, 1, 2)).reshape(grp, rg, n_state)
    pw_r, pw_i, pq_r, pq_i, bbr, bbi = _zoh_fwd(lam_re2, lam_im2, log_step2, b_re_t, b_im_t)
    sbk = n_grp * n_state // ngb
    pw_r, pw_i, pq_r, pq_i = (t.reshape(8, 2, n_grp, n_state) for t in (pw_r, pw_i, pq_r, pq_i))
    bbr_g = jnp.transpose(bbr.reshape(grp, 2, n_grp, n_state), (1, 2, 0, 3))
    bbi_g = jnp.transpose(bbi.reshape(grp, 2, n_grp, n_state), (1, 2, 0, 3))

    def power_rows(pw, r, first):
        full = jnp.concatenate([jnp.full((1, n_grp, n_state), first, F32), pw[:, r]], axis=0)
        return jnp.transpose(full.reshape(CHUNK + 1, ngb, sbk), (1, 0, 2))

    s5 = []
    for r in range(2):
        prm = dict(bre=_block_diag(bbr_g[r], ngb), bim=_block_diag(bbi_g[r], ngb),
                   cre=_block_diag(ssm_c_re[0, r], ngb), cim=_block_diag(ssm_c_im[0, r], ngb),
                   wr=power_rows(pw_r, r, 1.0), wi=power_rows(pw_i, r, 0.0))
        t_op, bp_op, cp_op = _s5_ops(prm["bre"], prm["bim"], prm["cre"], prm["cim"], prm["wr"], prm["wi"],
                                     reverse=(r == 1), name=f"l1_s5_ops{r}")
        s5.append(dict(
            prm, t=t_op, bp=bp_op, cp=cp_op,
            tab=_scan_tables(pq_r[:, r], pq_i[:, r], ngb, reverse=(r == 1)),
            tab_adj=_scan_tables(pq_r[:, r], -pq_i[:, r], ngb, reverse=(r == 0))))

    p0, h0 = _inproj(x2, sc0, sh0, wc_in, tb=tb_tok, name="l0_inproj")
    pc0, hc0 = _inproj(ctx2, scc0, shc0, wc_in, tb=tb_tok, name="l0_inproj_ctx")
    xhat0, rstd0, g0, yc0, f0 = _convgate(p0, x2, gt0, conv_w_full, wc_out, *ln0, grid_mode=True, tb=tb_tok, name="l0_conv")
    chat0, crstd0, gc0, ycc0, fc0 = _convgate(pc0, ctx2, gtc0, conv_w_full, wc_out, *ln0, grid_mode=False, tb=tb_tok,
                                              name="l0_conv_ctx")

    seq_rows = [(n_seq, n_ctx), (n_seq, 0)]
    useq_f, useq_b, h1 = _inproj(xhat0, sc1, sh1, ws_in[0:2], lnaff=ln0, tb=math.gcd(tb_tok, n_ctx), gb_rows=seq_rows,
                                 name="l1_inproj_u")
    z1, _ = _inproj(xhat0, sc1, sh1, ws_in[2:4], lnaff=ln0, tb=tb_tok, name="l1_inproj_z")
    uc, hc1 = _inproj(chat0, scc1, shc1, ws_in[0:2], lnaff=ln0, tb=tb_tok, gb_rows=[(n_ctx, 0)], name="l1_inproj_ctx")
    useq = [useq_f.at[:, 0:n_ctx].set(uc), useq_b.at[:, n_lat:].set(uc)]
    y_dir, hc_dir = [], []
    for r in range(2):
        yr, hcr = _s5_fwd(useq[r], s5[r]["t"], s5[r]["bp"], s5[r]["cp"], s5[r]["tab"], reverse=(r == 1),
                          name=f"l1_s5_fwd{r}")
        y_dir.append(yr)
        hc_dir.append(hcr)

    (loss, dxres, do1, gz1, gg1, dq1, dz1, dy_f, dy_b, dg1, db1, dgt1, dbglu, dd) = _glu_loss(
        useq[0], y_dir[0], y_dir[1], z1, xhat0, ln0, gt1, d_full, w_glu, b_glu_full, ws_out, ln1, tgt2,
        offs=(n_ctx, n_ctx, 0), dy_rows=seq_rows, tb=tb_glu, name="l1_glu_loss")
    no_dy = jnp.zeros((ngb, n_ctx, LANES), BF16)
    dy_dir = [dy_f.at[:, 0:n_ctx].set(no_dy), dy_b.at[:, n_lat:].set(no_dy)]

    du_dir, s5_grads = [], []
    for r in range(2):
        dur, dt_op, dbp_op, dcp_op, da8 = _s5_bwd(useq[r], dy_dir[r], hc_dir[r], s5[r]["t"], s5[r]["bp"], s5[r]["cp"],
                                                  s5[r]["tab"], s5[r]["tab_adj"], reverse=(r == 1),
                                                  name=f"l1_s5_bwd{r}")
        du_dir.append(dur)
        prm = s5[r]
        s5_grads.append(_s5_ops_bwd(prm["bre"], prm["bim"], prm["cre"], prm["cim"], prm["wr"], prm["wi"],
                                    prm["wr"][:, 1:2], prm["wi"][:, 1:2], dt_op, dbp_op, dcp_op, da8,
                                    reverse=(r == 1), name=f"l1_s5_ops_bwd{r}"))
    dp1, dr0, df0, dsc1, dsh1, dg0, db0, dgt0 = _ssm_inbwd(
        du_dir[0], du_dir[1], ws_in, xhat0, rstd0, ln0, sc1, gt0, f0, lat=(dy_dir[1], dz1, d_full, dxres),
        row_f=n_ctx, row_b=0, tb=tb_glu, name="l1_inbwd")
    dpc1, drc0, dfc0, dscc1, dshc1, dgc0, dbc0, dgtc0 = _ssm_inbwd(
        du_dir[0], du_dir[1], ws_in, chat0, crstd0, ln0, scc1, gtc0, fc0, lat=None,
        row_f=0, row_b=n_lat, tb=n_ctx, name="l1_inbwd_ctx")

    def conv_backward(df, p, yc, dr, xin, sc, grid_mode, tag):
        dbg, dz, dyc = _conv_bwd_a(df, wc_out, p, yc, tb=tb_tok, name="l0_bwd_a" + tag)
        dp, dcw = _conv_bwd_b(dyc, p, dbg, dz, conv_w_full, grid_mode=grid_mode, tb=tb_glu, name="l0_bwd_b" + tag)
        gx, dsc, dsh = _conv_inbwd(dp, wc_in, dr, xin, sc, tb=tb_tok, name="l0_inbwd" + tag)
        return dp, dcw, gx, dsc, dsh

    dp0, dcw0, grad_x, dsc0, dsh0 = conv_backward(df0, p0, yc0, dr0, x2, sc0, True, "")
    dpc0, dcwc0, _, dscc0, dshc0 = conv_backward(dfc0, pc0, ycc0, drc0, ctx2, scc0, False, "_ctx")

    tl = tb_tok
    gw_conv_in = _wgrad(h0, dp0, n_chunks=4, tm=d_model, tl=tl, name="wg_conv_in",
                        init=_wgrad(hc0, dpc0, n_chunks=4, tm=d_model, tl=tl, name="wg_conv_in_ctx"))
    gw_conv_out = _wgrad(g0, df0, n_chunks=1, tm=e, tl=tl, name="wg_conv_out",
                         init=_wgrad(gc0, dfc0, n_chunks=1, tm=e, tl=tl, name="wg_conv_out_ctx"))
    gw_ssm_in = _wgrad(h1, dp1, n_chunks=4, tm=d_model, tl=tl, name="wg_ssm_in",
                       init=_wgrad(hc1, dpc1, n_chunks=4, tm=d_model, tl=tl, name="wg_ssm_in_ctx"))
    gw_glu = _wgrad(gg1, dq1, n_chunks=1, tm=e // 2, tl=tl, name="wg_glu")
    gw_ssm_out = _wgrad(gz1, do1, n_chunks=1, tm=e, tl=tl, name="wg_ssm_out")

    grads, deltas, new_m, new_v = {}, {}, {}, {}

    def reduce_big(name, full):
        w = weights[name]
        rows = math.prod(w.shape[:-1])
        cols = w.shape[-1]
        parts = _exchange(full.reshape(8, rows // 2, cols), everyone, True, "rs_" + name)
        half = _sum_parts(parts, "sum_" + name)
        both = _exchange(half, pair, False, "pair_" + name, n_split=8).reshape(rows, cols)
        dlt, nm, nv = _adamw(w.reshape(rows, cols), both, mom_m[name].reshape(rows, cols),
                             mom_v[name].reshape(rows, cols), "adamw_" + name)
        grads[name], deltas[name] = both.reshape(w.shape), dlt.reshape(w.shape)
        new_m[name], new_v[name] = nm.reshape(w.shape), nv.reshape(w.shape)

    reduce_big("conv_w_in", gw_conv_in)
    reduce_big("conv_w_out", gw_conv_out)
    reduce_big("ssm_w_in", gw_ssm_in)
    reduce_big("ssm_w_glu", gw_glu)
    reduce_big("ssm_w_out", gw_ssm_out)

    gpn = (n_grp, grp, n_state)
    small_parts = [
        jnp.concatenate([dg0 + dgc0, dg1], axis=0), jnp.concatenate([db0 + dbc0, db1], axis=0),
        dcw0 + dcwc0, dd, dbglu,
        jnp.stack([s5_grads[r][4] for r in range(2)]),
    ] + [jnp.stack([_block_diag_t(s5_grads[r][k], *gpn) for r in range(2)]) for k in range(4)]
    small_shapes = [p.shape for p in small_parts]
    flat = _flat(small_parts)
    quantum = 8 * SUBLANES * LANES
    n_flat = -(-flat.shape[0] // quantum) * quantum
    flat = jnp.pad(flat, (0, n_flat - flat.shape[0])).reshape(8, n_flat // (8 * LANES), LANES)
    red = _sum_parts(_exchange(flat, everyone, True, "rs_small"), "sum_small")
    red = _exchange(red, everyone, False, "ag_small_grads").reshape(-1)
    g_ln_g, g_ln_b, g_conv_w, g_d, g_bglu, g_a, g_bbr, g_bbi, g_cre, g_cim = _unflat(red, small_shapes)

    g_a = g_a.reshape(2, ngb, 2, sbk)
    dar = g_a[:, :, 0].reshape(rg, n_state)
    dai = g_a[:, :, 1].reshape(rg, n_state)
    dbbr_t = jnp.transpose(g_bbr, (2, 0, 1, 3)).reshape(grp, rg, n_state)
    dbbi_t = jnp.transpose(g_bbi, (2, 0, 1, 3)).reshape(grp, rg, n_state)
    z_lre, z_lim, z_ls, z_bre, z_bim = _zoh_bwd(lam_re2, lam_im2, log_step2, b_re_t, b_im_t, dar, dai, dbbr_t, dbbi_t)

    zero = jnp.zeros((1, d_model), F32)
    dm_rows = jnp.stack([
        jnp.stack([jnp.concatenate([dsh0, dsc0, dgt0], axis=1), jnp.concatenate([dshc0, dscc0, dgtc0], axis=1)]),
        jnp.stack([jnp.concatenate([dsh1, dsc1, dgt1], axis=1), jnp.concatenate([dshc1, dscc1, zero], axis=1)]),
    ]).reshape(DEPTH, 2, 3 * d_model)
    dm_all = _exchange(dm_rows, everyone, False, "ag_dmod")
    dm_sh = lax.dynamic_slice_in_dim(dm_all, chip * ws, ws, axis=3)
    g_ada_w, g_ada_b, ds_part = _ada_bwd(c_all, cc2, ada_w, dm_all, dm_sh)
    g_cctx = _cctx_grad(_exchange(ds_part, chips, False, "ag_dsctx"), cc2)

    grads["ada_w"] = g_ada_w
    dlt, nm, nv = _adamw(ada_w.reshape(-1, ws), g_ada_w.reshape(-1, ws), m_ada_w.reshape(-1, ws),
                         v_ada_w.reshape(-1, ws), "adamw_ada_w")
    deltas["ada_w"], new_m["ada_w"], new_v["ada_w"] = dlt.reshape(ada_w.shape), nm.reshape(ada_w.shape), nv.reshape(ada_w.shape)

    def chip_cols(full, rows):
        return lax.dynamic_slice_in_dim(full.reshape(rows, e), chip * es, es, axis=1)

    small_grads = dict(
        c_ctx=g_cctx.reshape(c_ctx.shape), ada_b=g_ada_b.reshape(ada_b.shape), ln_g=g_ln_g, ln_b=g_ln_b,
        conv_w=chip_cols(g_conv_w, 3).reshape(conv_w.shape),
        ssm_lam_re=z_lre.reshape(ssm_lam_re.shape), ssm_lam_im=z_lim.reshape(ssm_lam_im.shape),
        ssm_log_step=z_ls.reshape(ssm_log_step.shape),
        ssm_b_re=jnp.transpose(z_bre.reshape(grp, 2, n_grp, n_state), (1, 2, 3, 0)).reshape(ssm_b_re.shape),
        ssm_b_im=jnp.transpose(z_bim.reshape(grp, 2, n_grp, n_state), (1, 2, 3, 0)).reshape(ssm_b_im.shape),
        ssm_c_re=g_cre.reshape(ssm_c_re.shape), ssm_c_im=g_cim.reshape(ssm_c_im.shape),
        ssm_d=chip_cols(g_d, 1).reshape(ssm_d.shape), ssm_b_glu=chip_cols(g_bglu, 1).reshape(ssm_b_glu.shape))
    small_names = list(small_grads)
    shapes = [weights[n].shape for n in small_names]
    quantum = SUBLANES * LANES

    def pack(parts, fill):
        vec = _flat(parts)
        n_pad = -(-vec.shape[0] // quantum) * quantum
        return jnp.pad(vec, (0, n_pad - vec.shape[0]), constant_values=fill).reshape(-1, LANES)

    dlt, nm, nv = _adamw(pack([weights[n] for n in small_names], 0.0), pack([small_grads[n] for n in small_names], 0.0),
                         pack([mom_m[n] for n in small_names], 0.0), pack([mom_v[n] for n in small_names], 1.0),
                         "adamw_small")
    for n, dv, mv, vv in zip(small_names, _unflat(dlt.reshape(-1), shapes), _unflat(nm.reshape(-1), shapes),
                             _unflat(nv.reshape(-1), shapes)):
        grads[n], deltas[n], new_m[n], new_v[n] = small_grads[n], dv, mv, vv

    loss_total = lax.psum(loss[0, 0], MESH_AXES)
    return (loss_total, grad_x.reshape(x.shape), *[grads[n] for n in names], *[deltas[n] for n in names],
            *[new_m[n] for n in names], *[new_v[n] for n in names])
```

```python
import functools
import math

import jax
import jax.numpy as jnp
from jax import lax
from jax.experimental import pallas as pl
from jax.experimental.pallas import tpu as pltpu

F32 = jnp.float32
BF16 = jnp.bfloat16
LANES = 128
SUBLANES = 8
VMEM_LIMIT = 56 * 1024 * 1024
MESH_AXES = ("x", "y", "c")
HIGHEST = lax.Precision.HIGHEST

GRID_W = 64
LN_EPS = 1e-5
DEPTH = 2
DN_ALPHA = (2 * DEPTH) ** 0.25
ADAM_LR, ADAM_B1, ADAM_B2, ADAM_EPS, ADAM_WD, ADAM_STEP = 0.001, 0.9, 0.999, 1e-08, 0.01, 10
GELU_K = math.sqrt(2.0 / math.pi)
GELU_C = 0.044715


def _params(n_grid_axes):
    return pltpu.CompilerParams(dimension_semantics=("arbitrary",) * n_grid_axes, vmem_limit_bytes=VMEM_LIMIT)


def _dot(a, b):
    return jnp.dot(a, b, preferred_element_type=F32)


def _dot_nt(a, b):
    return lax.dot_general(a, b, (((1,), (1,)), ((), ())), preferred_element_type=F32)


def _dot_tn(a, b):
    return lax.dot_general(a, b, (((0,), (0,)), ((), ())), preferred_element_type=F32)


def _sigmoid(x):
    return 1.0 / (1.0 + jnp.exp(-x))


def _colsum(x):
    return jnp.sum(x, axis=0, keepdims=True)


def _rowmean(x):
    return jnp.mean(x, axis=-1, keepdims=True)


def _acc(ref, first, value):
    @pl.when(first)
    def _():
        ref[...] = value

    @pl.when(jnp.logical_not(first))
    def _():
        ref[...] += value


def _exchange_copies(src_ref, out_ref, send_sems, recv_sems, own_sem, axes, all_to_all, sem0=0):
    n_peers = 2 ** len(axes)
    pos = {a: lax.axis_index(a) for a in MESH_AXES}

    def index(p):
        return sum(p[a] * (2 ** (len(axes) - 1 - i)) for i, a in enumerate(axes))

    me = index(pos)
    own = pltpu.make_async_copy(src_ref.at[me] if all_to_all else src_ref, out_ref.at[me], own_sem)
    copies = []
    for k in range(1, n_peers):
        peer = dict(pos)
        for i, a in enumerate(axes):
            if (k >> (len(axes) - 1 - i)) & 1:
                peer[a] = 1 - pos[a]
        copies.append(pltpu.make_async_remote_copy(
            src_ref=src_ref.at[index(peer)] if all_to_all else src_ref,
            dst_ref=out_ref.at[me],
            send_sem=send_sems.at[sem0 + k - 1],
            recv_sem=recv_sems.at[sem0 + k - 1],
            device_id=tuple(peer[a] for a in MESH_AXES),
            device_id_type=pl.DeviceIdType.MESH,
        ))
    return copies, own


def _exchange_shape(src, axes, all_to_all):
    block = tuple(src.shape[1:] if all_to_all else src.shape)
    return jax.ShapeDtypeStruct((2 ** len(axes),) + block, src.dtype)


def _exchange(src, axes, all_to_all, name):
    n_peers = 2 ** len(axes)

    def body(src_ref, out_ref, send_sems, recv_sems, own_sem):
        copies, own = _exchange_copies(src_ref, out_ref, send_sems, recv_sems, own_sem, axes, all_to_all)
        own.start()
        for cp in copies:
            cp.start()
        for cp in copies:
            cp.wait()
        own.wait()

    return pl.pallas_call(
        body,
        name=name,
        out_shape=_exchange_shape(src, axes, all_to_all),
        in_specs=[pl.BlockSpec(memory_space=pltpu.HBM)],
        out_specs=pl.BlockSpec(memory_space=pltpu.HBM),
        scratch_shapes=[
            pltpu.SemaphoreType.DMA((n_peers - 1,)),
            pltpu.SemaphoreType.DMA((n_peers - 1,)),
            pltpu.SemaphoreType.DMA,
        ],
    )(src)


class _Hosted:
    def __init__(self, items):
        self.items = items
        self.args = [src for src, _, _ in items]
        self.in_specs = [pl.BlockSpec(memory_space=pltpu.HBM)] * len(items)
        self.out_specs = [pl.BlockSpec(memory_space=pltpu.HBM)] * len(items)
        self.out_shapes = [_exchange_shape(*item) for item in items]
        n_remote = sum(2 ** len(axes) - 1 for _, axes, _ in items)
        self.scratch = [pltpu.SemaphoreType.DMA((n_remote,)), pltpu.SemaphoreType.DMA((n_remote,)),
                        pltpu.SemaphoreType.DMA((len(items),))]

    def _copies(self, src_refs, out_refs, send_sems, recv_sems, own_sems):
        out, sem0 = [], 0
        for n, (_, axes, all_to_all) in enumerate(self.items):
            copies, own = _exchange_copies(src_refs[n], out_refs[n], send_sems, recv_sems, own_sems.at[n], axes,
                                           all_to_all, sem0)
            out += [own] + copies
            sem0 += len(copies)
        return out

    def start(self, *refs):
        for cp in self._copies(*refs):
            cp.start()

    def wait(self, *refs):
        for cp in self._copies(*refs):
            cp.wait()


def _call(body, *, name, grid, in_specs, out_specs, out_shape, scratch_shapes, args, hosted=None):
    params = _params(len(grid))
    if hosted is None:
        outs = pl.pallas_call(body, name=name, grid=grid, in_specs=in_specs, out_specs=tuple(out_specs),
                              out_shape=tuple(out_shape), scratch_shapes=list(scratch_shapes), compiler_params=params)(*args)
        return list(outs), []
    n_in, n_out, n_scr, n_h = len(in_specs), len(out_shape), len(scratch_shapes), len(hosted.items)

    def wrapped(*refs):
        ins, h_in = refs[:n_in], refs[n_in:n_in + n_h]
        outs, h_out = refs[n_in + n_h:n_in + n_h + n_out], refs[n_in + n_h + n_out:n_in + 2 * n_h + n_out]
        scr = refs[n_in + 2 * n_h + n_out:]
        first = functools.reduce(jnp.logical_and, [pl.program_id(k) == 0 for k in range(len(grid))])
        last = functools.reduce(jnp.logical_and, [pl.program_id(k) == grid[k] - 1 for k in range(len(grid))])

        @pl.when(first)
        def _():
            hosted.start(h_in, h_out, *scr[n_scr:])

        body(*ins, *outs, *scr[:n_scr])

        @pl.when(last)
        def _():
            hosted.wait(h_in, h_out, *scr[n_scr:])

    outs = pl.pallas_call(
        wrapped, name=name, grid=grid, in_specs=[*in_specs, *hosted.in_specs],
        out_specs=(*out_specs, *hosted.out_specs), out_shape=(*out_shape, *hosted.out_shapes),
        scratch_shapes=[*scratch_shapes, *hosted.scratch], compiler_params=params)(*args, *hosted.args)
    return list(outs[:n_out]), list(outs[n_out:])


def _sum_parts(parts, name):
    n_parts, rows, cols = parts.shape
    tr = rows
    while n_parts * tr * cols * 4 > 8 * 1024 * 1024 and tr % 16 == 0:
        tr //= 2

    def body(p_ref, o_ref):
        total = p_ref[0]
        for k in range(1, n_parts):
            total = total + p_ref[k]
        o_ref[...] = total

    return pl.pallas_call(
        body,
        name=name,
        grid=(rows // tr,),
        out_shape=jax.ShapeDtypeStruct((rows, cols), F32),
        in_specs=[pl.BlockSpec((n_parts, tr, cols), lambda i: (0, i, 0))],
        out_specs=pl.BlockSpec((tr, cols), lambda i: (i, 0)),
        compiler_params=_params(1),
    )(parts)


def _adamw(w, g, m, v, name):
    rows, cols = w.shape
    tr = rows
    while tr * cols * 4 > 2 * 1024 * 1024 and tr % 16 == 0:
        tr //= 2

    def body(w_ref, g_ref, m_ref, v_ref, d_ref, nm_ref, nv_ref):
        gv = g_ref[...]
        nm = ADAM_B1 * m_ref[...] + (1.0 - ADAM_B1) * gv
        nv = ADAM_B2 * v_ref[...] + (1.0 - ADAM_B2) * (gv * gv)
        m_hat = nm / (1.0 - ADAM_B1 ** ADAM_STEP)
        v_hat = nv / (1.0 - ADAM_B2 ** ADAM_STEP)
        d_ref[...] = -ADAM_LR * (m_hat / (jnp.sqrt(v_hat) + ADAM_EPS) + ADAM_WD * w_ref[...])
        nm_ref[...] = nm
        nv_ref[...] = nv

    spec = pl.BlockSpec((tr, cols), lambda i: (i, 0))
    shape = jax.ShapeDtypeStruct((rows, cols), F32)
    return pl.pallas_call(
        body, name=name, grid=(rows // tr,), out_shape=(shape, shape, shape),
        in_specs=[spec] * 4, out_specs=(spec, spec, spec), compiler_params=_params(1),
    )(w, g, m, v)


def _ada_rows(c_ref, cc_ref):
    rows = jnp.concatenate([c_ref[...], jnp.broadcast_to(cc_ref[...], c_ref.shape)], axis=0)
    return rows


def _ada_fwd(c_all, c_ctx, w_sh, b_sh):
    n_layers, _, ws = w_sh.shape

    def body(c_ref, cc_ref, w_ref, b_ref, o_ref):
        rows = _ada_rows(c_ref, cc_ref)
        s = rows * _sigmoid(rows)
        for i in range(n_layers):
            o_ref[i] = jnp.dot(s, w_ref[i], precision=HIGHEST, preferred_element_type=F32) + b_ref[i]

    return pl.pallas_call(
        body, name="ada_fwd", out_shape=jax.ShapeDtypeStruct((n_layers, 16, ws), F32),
        compiler_params=pltpu.CompilerParams(vmem_limit_bytes=VMEM_LIMIT),
    )(c_all, c_ctx, w_sh, b_sh)


def _ada_bwd(c_all, c_ctx, w_sh, dm_full, dm_sh):
    n_layers, d_model, ws = w_sh.shape
    n_dev = dm_full.shape[0]
    cols = dm_full.shape[-1]

    def body(c_ref, cc_ref, w_ref, dmf_ref, dms_ref, gw_ref, gb_ref, ds_ref):
        rows = _ada_rows(c_ref, cc_ref)
        s = rows * _sigmoid(rows)
        ds = jnp.zeros((8, d_model), F32)
        for i in range(n_layers):
            ctx_s = dms_ref[0, i, 1:2, :]
            ctx_f = dmf_ref[0, i, 1:2, :]
            ex_f = dmf_ref[0, i, 0:1, :]
            for k in range(1, n_dev):
                ctx_s = ctx_s + dms_ref[k, i, 1:2, :]
                ctx_f = ctx_f + dmf_ref[k, i, 1:2, :]
                ex_f = ex_f + dmf_ref[k, i, 0:1, :]
            gb_ref[i] = ex_f + ctx_f
            r = jnp.concatenate([dms_ref[k, i, 0:1, :] for k in range(n_dev)] + [ctx_s, jnp.zeros((7, ws), F32)], axis=0)
            gw_ref[i] = lax.dot_general(s, r, (((0,), (0,)), ((), ())), precision=HIGHEST, preferred_element_type=F32)
            ds = ds + lax.dot_general(jnp.broadcast_to(ctx_s, (8, ws)), w_ref[i], (((1,), (1,)), ((), ())),
                                      precision=HIGHEST, preferred_element_type=F32)
        ds_ref[...] = ds

    return pl.pallas_call(
        body, name="ada_bwd",
        out_shape=(jax.ShapeDtypeStruct((n_layers, d_model, ws), F32), jax.ShapeDtypeStruct((n_layers, 1, cols), F32),
                   jax.ShapeDtypeStruct((8, d_model), F32)),
        compiler_params=pltpu.CompilerParams(vmem_limit_bytes=VMEM_LIMIT),
    )(c_all, c_ctx, w_sh, dm_full, dm_sh)


def _cctx_grad(ds_parts, c_ctx):
    def body(p_ref, c_ref, o_ref):
        tot = p_ref[0, 0:1, :]
        for k in range(1, ds_parts.shape[0]):
            tot = tot + p_ref[k, 0:1, :]
        cv = c_ref[...]
        sg = _sigmoid(cv)
        o_ref[...] = tot * (sg * (1.0 + cv * (1.0 - sg)))

    return pl.pallas_call(body, name="cctx_grad", out_shape=jax.ShapeDtypeStruct(c_ctx.shape, F32))(ds_parts, c_ctx)


def _zoh_math(lam_re, lam_im, log_step, b_re, b_im):
    dt = jnp.exp(log_step)
    mag = jnp.exp(lam_re * dt)
    ar = mag * jnp.cos(lam_im * dt)
    ai = mag * jnp.sin(lam_im * dt)
    qr, qi = ar - 1.0, ai
    den = lam_re * lam_re + lam_im * lam_im
    fr = (qr * lam_re + qi * lam_im) / den
    fi = (qi * lam_re - qr * lam_im) / den
    bbr = fr[None] * b_re - fi[None] * b_im
    bbi = fr[None] * b_im + fi[None] * b_re
    return ar, ai, bbr, bbi


def _zoh_fwd(lam_re, lam_im, log_step, b_re, b_im):
    rg, n = lam_re.shape

    def body(lr_ref, li_ref, ls_ref, br_ref, bi_ref, pr_ref, pi_ref, qr_ref, qi_ref, bbr_ref, bbi_ref):
        ar, ai, bbr, bbi = _zoh_math(lr_ref[...], li_ref[...], ls_ref[...], br_ref[...], bi_ref[...])
        bbr_ref[...] = bbr
        bbi_ref[...] = bbi

        def powers(base_r, base_i, r_ref, i_ref):
            pr, pi_ = base_r, base_i
            for k in range(8):
                r_ref[k] = pr
                i_ref[k] = pi_
                pr, pi_ = pr * base_r - pi_ * base_i, pr * base_i + pi_ * base_r

        powers(ar, ai, pr_ref, pi_ref)
        powers(pr_ref[7], pi_ref[7], qr_ref, qi_ref)

    pw = jax.ShapeDtypeStruct((8, rg, n), F32)
    bb = jax.ShapeDtypeStruct(b_re.shape, F32)
    return pl.pallas_call(body, name="zoh_fwd", out_shape=(pw, pw, pw, pw, bb, bb))(lam_re, lam_im, log_step, b_re, b_im)


def _zoh_bwd(lam_re, lam_im, log_step, b_re, b_im, dar, dai, dbbr, dbbi):
    def body(lr_ref, li_ref, ls_ref, br_ref, bi_ref, dar_ref, dai_ref, dbr_ref, dbi_ref, *outs):
        _, vjp = jax.vjp(_zoh_math, lr_ref[...], li_ref[...], ls_ref[...], br_ref[...], bi_ref[...])
        grads = vjp((dar_ref[...], dai_ref[...], dbr_ref[...], dbi_ref[...]))
        for o_ref, gval in zip(outs, grads):
            o_ref[...] = gval

    shapes = tuple(jax.ShapeDtypeStruct(a.shape, F32) for a in (lam_re, lam_im, log_step, b_re, b_im))
    return pl.pallas_call(body, name="zoh_bwd", out_shape=shapes)(lam_re, lam_im, log_step, b_re, b_im, dar, dai, dbbr, dbbi)


def _inproj(xin, sc, sh, w, *, lnaff=None, tb, gb_rows=None, name, hosted=None):
    n_tok, d_model = xin.shape
    n_chunks, _, cw = w.shape
    tb = min(tb, n_tok)
    nq = cw // LANES
    has_ln = lnaff is not None
    n_out = 1 if gb_rows is None else len(gb_rows)

    def body(*refs):
        if has_ln:
            x_ref, g_ref, b_ref, sc_ref, sh_ref, w_ref = refs[:6]
        else:
            x_ref, sc_ref, sh_ref, w_ref = refs[:4]
        p_refs, h_ref = refs[-1 - n_out:-1], refs[-1]

        @pl.when(pl.program_id(1) == 0)
        def _():
            xv = x_ref[...]
            if has_ln:
                xv = xv * g_ref[...] + b_ref[...]
            h_ref[...] = (xv * (1.0 + sc_ref[...]) + sh_ref[...]).astype(BF16)

        acc = _dot(h_ref[...], w_ref[0]).astype(BF16)
        if gb_rows is None:
            p_refs[0][0] = acc
        else:
            for p_ref in p_refs:
                for q in range(nq):
                    p_ref[q] = acc[:, q * LANES:(q + 1) * LANES]

    vec = pl.BlockSpec((1, d_model), lambda i, j: (0, 0))
    in_specs = [pl.BlockSpec((tb, d_model), lambda i, j: (i, 0))] + ([vec, vec] if has_ln else []) + [
        vec, vec, pl.BlockSpec((1, d_model, cw), lambda i, j: (j, 0, 0))]
    if gb_rows is None:
        p_shapes = [jax.ShapeDtypeStruct((n_chunks, n_tok, cw), BF16)]
        p_specs = [pl.BlockSpec((1, tb, cw), lambda i, j: (j, i, 0))]
    else:
        p_shapes, p_specs = [], []
        for total, off in gb_rows:
            assert off % tb == 0
            p_shapes.append(jax.ShapeDtypeStruct((n_chunks * nq, total, LANES), BF16))
            p_specs.append(pl.BlockSpec((nq, tb, LANES), functools.partial(lambda i, j, ob: (j, i + ob, 0), ob=off // tb)))
    args = (xin,) + (tuple(lnaff) if has_ln else ()) + (sc, sh, w)
    outs, extra = _call(
        body, name=name, grid=(n_tok // tb, n_chunks), in_specs=in_specs,
        out_specs=[*p_specs, pl.BlockSpec((tb, d_model), lambda i, j: (i, 0))],
        out_shape=[*p_shapes, jax.ShapeDtypeStruct((n_tok, d_model), BF16)], scratch_shapes=[], args=args, hosted=hosted)
    return (*outs, extra) if hosted is not None else tuple(outs)


def _shifted(u, rows, width, tb):
    col = rows % width
    um = jnp.where(col == 0, 0.0, pltpu.roll(u, 1, 0))
    up = jnp.where(col == width - 1, 0.0, pltpu.roll(u, tb - 1, 0))
    return um, up


def _slab_width(e):
    return min(512, e // 2)


def _convgate(p, x, gt, conv_w, w_out, ln_g, ln_b, *, grid_mode, tb, name):
    _, n_tok, e = p.shape
    d_model = x.shape[1]
    eh = e // 2
    if not grid_mode:
        tb = n_tok
    tb = min(tb, n_tok)
    nb = n_tok // tb
    hb = tb // GRID_W
    cs = _slab_width(e)

    def body(*refs):
        if grid_mode:
            (bg_ref, cg_ref, v_ref, z_ref, cgp_ref, vp_ref, cgn_ref, vn_ref, x_ref, gt_ref, cw_ref, wo_ref, lg_ref,
             lb_ref, xh_ref, rs_ref, g_ref, yc_ref, f_ref) = refs
        else:
            (bg_ref, cg_ref, v_ref, z_ref, x_ref, gt_ref, cw_ref, wo_ref, lg_ref, lb_ref, xh_ref, rs_ref, g_ref,
             yc_ref, f_ref) = refs
        i = pl.program_id(0)
        rows = lax.broadcasted_iota(jnp.int32, (tb, 1), 0)
        for c0 in range(0, e, cs):
            sl = slice(c0, c0 + cs)
            u = cg_ref[0, :, sl].astype(F32) * v_ref[0, :, sl].astype(F32)
            w = cw_ref[:, sl]
            if grid_mode and c0 >= eh:
                hs = slice(c0 - eh, c0 - eh + cs)
                uprev = cgp_ref[0, :, hs].astype(F32) * vp_ref[0, :, hs].astype(F32)
                unext = cgn_ref[0, :, hs].astype(F32) * vn_ref[0, :, hs].astype(F32)
                uprev = jnp.where(i > 0, uprev, 0.0)
                unext = jnp.where(i < nb - 1, unext, 0.0)
                if tb > GRID_W:
                    um = jnp.concatenate([uprev, u[:tb - GRID_W]], axis=0)
                    up = jnp.concatenate([u[GRID_W:], unext], axis=0)
                else:
                    um, up = uprev, unext
            else:
                um, up = _shifted(u, rows, GRID_W if grid_mode else tb, tb)
            yc = um * w[0:1] + u * w[1:2] + up * w[2:3]
            zf = z_ref[0, :, sl].astype(F32)
            gval = bg_ref[0, :, sl].astype(F32) * yc * (zf * _sigmoid(zf))
            yc_ref[:, sl] = yc.astype(BF16)
            g_ref[:, sl] = gval.astype(BF16)
        f = _dot(g_ref[...], wo_ref[...])
        f_ref[...] = f.astype(BF16)
        r = DN_ALPHA * x_ref[...] + gt_ref[...] * f
        rc = r - _rowmean(r)
        rstd = lax.rsqrt(_rowmean(rc * rc) + LN_EPS)
        xh_ref[...] = rc * rstd
        rs_ref[...] = rstd

    def chunk(k):
        return pl.BlockSpec((1, tb, e), lambda i: (k, i, 0))

    n_hrows = n_tok // GRID_W

    def halo_prev(k):
        return pl.BlockSpec((1, GRID_W, eh), lambda i: (k, jnp.maximum(i * hb - 1, 0), 1))

    def halo_next(k):
        return pl.BlockSpec((1, GRID_W, eh), lambda i: (k, jnp.minimum((i + 1) * hb, n_hrows - 1), 1))

    vec = pl.BlockSpec((1, d_model), lambda i: (0, 0))
    tok = pl.BlockSpec((tb, d_model), lambda i: (i, 0))
    wide = pl.BlockSpec((tb, e), lambda i: (i, 0))
    in_specs = [chunk(0), chunk(1), chunk(2), chunk(3)]
    args = [p, p, p, p]
    if grid_mode:
        in_specs += [halo_prev(1), halo_prev(2), halo_next(1), halo_next(2)]
        args += [p, p, p, p]
    in_specs += [tok, vec, pl.BlockSpec((3, e), lambda i: (0, 0)), pl.BlockSpec((e, d_model), lambda i: (0, 0)), vec, vec]
    args += [x, gt, conv_w, w_out, ln_g, ln_b]
    return pl.pallas_call(
        body, name=name, grid=(nb,),
        out_shape=(jax.ShapeDtypeStruct((n_tok, d_model), F32), jax.ShapeDtypeStruct((n_tok, 1), F32),
                   jax.ShapeDtypeStruct((n_tok, e), BF16), jax.ShapeDtypeStruct((n_tok, e), BF16),
                   jax.ShapeDtypeStruct((n_tok, d_model), BF16)),
        in_specs=in_specs, out_specs=(tok, pl.BlockSpec((tb, 1), lambda i: (i, 0)), wide, wide, tok),
        compiler_params=_params(1),
    )(*args)


def _scan_block(buf_ref, tab_ref, cr, ci, *, reverse, tb, sb):
    n_slabs = tb // SUBLANES

    def slab(s, carry):
        cr, ci = carry
        idx = (n_slabs - 1 - s) if reverse else s
        r0 = pl.multiple_of(idx * SUBLANES, SUBLANES)
        xr = buf_ref[pl.ds(r0, SUBLANES), 0:sb]
        xi = buf_ref[pl.ds(r0, SUBLANES), sb:2 * sb]
        for k, step in enumerate((1, 2, 4)):
            ar = tab_ref[2 * k]
            ai = tab_ref[2 * k + 1]
            shift = (SUBLANES - step) if reverse else step
            rr = pltpu.roll(xr, shift, 0)
            ri = pltpu.roll(xi, shift, 0)
            xr, xi = xr + ar * rr - ai * ri, xi + ar * ri + ai * rr
        pr = tab_ref[6]
        pi_ = tab_ref[7]
        xr, xi = xr + pr * cr - pi_ * ci, xi + pr * ci + pi_ * cr
        buf_ref[pl.ds(r0, SUBLANES), 0:sb] = xr
        buf_ref[pl.ds(r0, SUBLANES), sb:2 * sb] = xi
        last = 0 if reverse else SUBLANES - 1
        return (jnp.broadcast_to(xr[last:last + 1, :], (SUBLANES, sb)),
                jnp.broadcast_to(xi[last:last + 1, :], (SUBLANES, sb)))

    return lax.fori_loop(0, n_slabs, slab, (cr, ci))


CHUNK = SUBLANES


def _group_mask():
    r = lax.broadcasted_iota(jnp.int32, (LANES, LANES), 0)
    c = lax.broadcasted_iota(jnp.int32, (LANES, LANES), 1)
    return r // 16 == c // 16


def _s5_ops(bre, bim, cre, cim, wr, wi, *, reverse, name):
    ngb, _, sb = bre.shape
    n_rows = CHUNK * LANES

    def body(bre_ref, bim_ref, cre_ref, cim_ref, wr_ref, wi_ref, t_ref, bp_ref, cp_ref):
        b_re, b_im, c_re, c_im = bre_ref[0], bim_ref[0], cre_ref[0], cim_ref[0]
        mask = _group_mask()
        er, ei = [], []
        for tau in range(CHUNK + 1):
            w_r, w_i = wr_ref[0, tau:tau + 1, :], wi_ref[0, tau:tau + 1, :]
            er.append(c_re * w_r - c_im * w_i)
            ei.append(c_re * w_i + c_im * w_r)
        kt = []
        for tau in range(CHUNK):
            k = (lax.dot_general(b_re, er[tau], (((1,), (1,)), ((), ())), precision=HIGHEST, preferred_element_type=F32)
                 - lax.dot_general(b_im, ei[tau], (((1,), (1,)), ((), ())), precision=HIGHEST, preferred_element_type=F32))
            kt.append(jnp.where(mask, k, 0.0).astype(BF16))
        zero = jnp.zeros((LANES, LANES), BF16)
        for i in range(CHUNK):
            rows = slice(i * LANES, (i + 1) * LANES)
            for j in range(CHUNK):
                lag = (i - j) if reverse else (j - i)
                t_ref[0, rows, j * LANES:(j + 1) * LANES] = kt[lag] if lag >= 0 else zero
            tau = i if reverse else CHUNK - 1 - i
            w_r, w_i = wr_ref[0, tau:tau + 1, :], wi_ref[0, tau:tau + 1, :]
            bp_ref[0, rows, 0:sb] = (b_re * w_r - b_im * w_i).astype(BF16)
            bp_ref[0, rows, sb:2 * sb] = (b_re * w_i + b_im * w_r).astype(BF16)
            tau = CHUNK - i if reverse else i + 1
            cp_ref[0, rows, 0:sb] = er[tau].astype(BF16)
            cp_ref[0, rows, sb:2 * sb] = (-ei[tau]).astype(BF16)

    mat = pl.BlockSpec((1, LANES, sb), lambda g: (g, 0, 0))
    pw = pl.BlockSpec((1, CHUNK + 1, sb), lambda g: (g, 0, 0))
    return pl.pallas_call(
        body, name=name, grid=(ngb,),
        out_shape=(jax.ShapeDtypeStruct((ngb, n_rows, n_rows), BF16), jax.ShapeDtypeStruct((ngb, n_rows, 2 * sb), BF16),
                   jax.ShapeDtypeStruct((ngb, n_rows, 2 * sb), BF16)),
        in_specs=[mat, mat, mat, mat, pw, pw],
        out_specs=(pl.BlockSpec((1, n_rows, n_rows), lambda g: (g, 0, 0)),
                   pl.BlockSpec((1, n_rows, 2 * sb), lambda g: (g, 0, 0)),
                   pl.BlockSpec((1, n_rows, 2 * sb), lambda g: (g, 0, 0))),
        compiler_params=_params(1),
    )(bre, bim, cre, cim, wr, wi)


def _s5_ops_bwd(bre, bim, cre, cim, wr, wi, ar, ai, dt, dbp, dcp, da8, *, reverse, name):
    ngb, _, sb = bre.shape
    n_rows = CHUNK * LANES

    def dot_hi(a, b, dims):
        return lax.dot_general(a, b, (dims, ((), ())), precision=HIGHEST, preferred_element_type=F32)

    def body(bre_ref, bim_ref, cre_ref, cim_ref, wr_ref, wi_ref, ar_ref, ai_ref, dt_ref, dbp_ref, dcp_ref, da8_ref,
             dbre_ref, dbim_ref, dcre_ref, dcim_ref, da_ref):
        b_re, b_im, c_re, c_im = bre_ref[0], bim_ref[0], cre_ref[0], cim_ref[0]
        mask = _group_mask()
        w_r = [wr_ref[0, tau:tau + 1, :] for tau in range(CHUNK + 1)]
        w_i = [wi_ref[0, tau:tau + 1, :] for tau in range(CHUNK + 1)]
        der = [jnp.zeros((LANES, sb), F32) for _ in range(CHUNK + 1)]
        dei = [jnp.zeros((LANES, sb), F32) for _ in range(CHUNK + 1)]
        dwr = [jnp.zeros((1, sb), F32) for _ in range(CHUNK + 1)]
        dwi = [jnp.zeros((1, sb), F32) for _ in range(CHUNK + 1)]
        dwr[CHUNK] = da8_ref[0, :, 0:sb]
        dwi[CHUNK] = da8_ref[0, :, sb:2 * sb]
        d_bre = jnp.zeros((LANES, sb), F32)
        d_bim = jnp.zeros((LANES, sb), F32)
        dkt = [jnp.zeros((LANES, LANES), F32) for _ in range(CHUNK)]
        for i in range(CHUNK):
            rows = slice(i * LANES, (i + 1) * LANES)
            for j in range(CHUNK):
                lag = (i - j) if reverse else (j - i)
                if lag >= 0:
                    dkt[lag] = dkt[lag] + dt_ref[0, rows, j * LANES:(j + 1) * LANES]
            tau = i if reverse else CHUNK - 1 - i
            g_r, g_i = dbp_ref[0, rows, 0:sb], dbp_ref[0, rows, sb:2 * sb]
            d_bre = d_bre + g_r * w_r[tau] + g_i * w_i[tau]
            d_bim = d_bim - g_r * w_i[tau] + g_i * w_r[tau]
            dwr[tau] = dwr[tau] + _colsum(g_r * b_re + g_i * b_im)
            dwi[tau] = dwi[tau] + _colsum(g_i * b_re - g_r * b_im)
            tau = CHUNK - i if reverse else i + 1
            der[tau] = der[tau] + dcp_ref[0, rows, 0:sb]
            dei[tau] = dei[tau] - dcp_ref[0, rows, sb:2 * sb]
        d_cre = jnp.zeros((LANES, sb), F32)
        d_cim = jnp.zeros((LANES, sb), F32)
        for tau in range(CHUNK + 1):
            if tau < CHUNK:
                e_r = c_re * w_r[tau] - c_im * w_i[tau]
                e_i = c_re * w_i[tau] + c_im * w_r[tau]
                dk = jnp.where(mask, dkt[tau], 0.0)
                d_bre = d_bre + dot_hi(dk, e_r, ((1,), (0,)))
                d_bim = d_bim - dot_hi(dk, e_i, ((1,), (0,)))
                der[tau] = der[tau] + dot_hi(dk, b_re, ((0,), (0,)))
                dei[tau] = dei[tau] - dot_hi(dk, b_im, ((0,), (0,)))
            d_cre = d_cre + der[tau] * w_r[tau] + dei[tau] * w_i[tau]
            d_cim = d_cim - der[tau] * w_i[tau] + dei[tau] * w_r[tau]
            dwr[tau] = dwr[tau] + _colsum(der[tau] * c_re + dei[tau] * c_im)
            dwi[tau] = dwi[tau] + _colsum(dei[tau] * c_re - der[tau] * c_im)
        a_r, a_i = ar_ref[0], ai_ref[0]
        d_ar = jnp.zeros((1, sb), F32)
        d_ai = jnp.zeros((1, sb), F32)
        for tau in range(CHUNK, 0, -1):
            d_ar = d_ar + dwr[tau] * w_r[tau - 1] + dwi[tau] * w_i[tau - 1]
            d_ai = d_ai - dwr[tau] * w_i[tau - 1] + dwi[tau] * w_r[tau - 1]
            dwr[tau - 1], dwi[tau - 1] = (dwr[tau - 1] + dwr[tau] * a_r + dwi[tau] * a_i,
                                          dwi[tau - 1] - dwr[tau] * a_i + dwi[tau] * a_r)
        dbre_ref[0] = d_bre
        dbim_ref[0] = d_bim
        dcre_ref[0] = d_cre
        dcim_ref[0] = d_cim
        da_ref[0, :, 0:sb] = d_ar
        da_ref[0, :, sb:2 * sb] = d_ai

    mat = pl.BlockSpec((1, LANES, sb), lambda g: (g, 0, 0))
    pw = pl.BlockSpec((1, CHUNK + 1, sb), lambda g: (g, 0, 0))
    one = pl.BlockSpec((1, 1, sb), lambda g: (g, 0, 0))
    two = pl.BlockSpec((1, 1, 2 * sb), lambda g: (g, 0, 0))
    big = pl.BlockSpec((1, n_rows, n_rows), lambda g: (g, 0, 0))
    big2 = pl.BlockSpec((1, n_rows, 2 * sb), lambda g: (g, 0, 0))
    mshape = jax.ShapeDtypeStruct((ngb, LANES, sb), F32)
    return pl.pallas_call(
        body, name=name, grid=(ngb,),
        out_shape=(mshape, mshape, mshape, mshape, jax.ShapeDtypeStruct((ngb, 1, 2 * sb), F32)),
        in_specs=[mat, mat, mat, mat, pw, pw, one, one, big, big2, big2, two],
        out_specs=(mat, mat, mat, mat, two),
        compiler_params=_params(1),
    )(bre, bim, cre, cim, wr, wi, ar, ai, dt, dbp, dcp, da8)


def _shift_rows(xv, edge, rows, n_rows, down):
    if down:
        return jnp.where(rows == 0, edge, pltpu.roll(xv, 1, 0))
    return jnp.where(rows == n_rows - 1, edge, pltpu.roll(xv, n_rows - 1, 0))


def _rows_of_tokens(tok_ref, conv_scr, rb):
    conv_scr[...] = tok_ref[0].astype(F32)
    return jnp.concatenate([conv_scr[pl.ds(j, rb, stride=CHUNK), :] for j in range(CHUNK)], axis=1).astype(BF16)


def _tokens_of_rows(val, tok_ref, conv_scr, rb):
    for j in range(CHUNK):
        conv_scr[pl.ds(j, rb, stride=CHUNK), :] = val[:, j * LANES:(j + 1) * LANES]
    tok_ref[0] = conv_scr[...].astype(BF16)


def _s5_row_block(n_seq, target=416):
    n_rows = n_seq // CHUNK
    best = 16
    for rb in range(16, min(target, n_rows) + 1, 16):
        if n_rows % rb == 0:
            best = rb
    assert n_rows % best == 0
    return best


def _s5_fwd(useq, t_op, bp, cp, tab, *, reverse, name):
    ngb, n_seq, _ = useq.shape
    sb = bp.shape[2] // 2
    width = CHUNK * LANES
    rb = _s5_row_block(n_seq)
    tbk = rb * CHUNK
    steps = n_seq // tbk

    def blk(i):
        return (steps - 1 - i) if reverse else i

    def body(u_ref, t_ref, b_ref, c_ref, tab_ref, y_ref, hc_ref, h_scr, conv_scr, carry_scr):
        i = pl.program_id(1)

        @pl.when(i == 0)
        def _():
            carry_scr[...] = jnp.zeros_like(carry_scr)

        hc_ref[0, 0] = carry_scr[...]
        enter = carry_scr[0:1, :]
        uv = _rows_of_tokens(u_ref, conv_scr, rb)
        h_scr[...] = _dot(uv, b_ref[0])
        cr, ci = _scan_block(h_scr, tab_ref.at[0], carry_scr[:, 0:sb], carry_scr[:, sb:2 * sb],
                             reverse=reverse, tb=rb, sb=sb)
        carry_scr[:, 0:sb] = cr
        carry_scr[:, sb:2 * sb] = ci
        rows = lax.broadcasted_iota(jnp.int32, (rb, 1), 0)
        hprev = _shift_rows(h_scr[...], enter, rows, rb, down=not reverse)
        _tokens_of_rows(_dot(uv, t_ref[0]) + _dot_nt(hprev.astype(BF16), c_ref[0]), y_ref, conv_scr, rb)

    op = pl.BlockSpec((1, width, width), lambda g, i: (g, 0, 0))
    op2 = pl.BlockSpec((1, width, 2 * sb), lambda g, i: (g, 0, 0))
    tok = pl.BlockSpec((1, tbk, LANES), lambda g, i: (g, blk(i), 0))
    return pl.pallas_call(
        body, name=name, grid=(ngb, steps),
        out_shape=(jax.ShapeDtypeStruct((ngb, n_seq, LANES), BF16),
                   jax.ShapeDtypeStruct((ngb, steps, SUBLANES, 2 * sb), F32)),
        in_specs=[tok, op, op2, op2, pl.BlockSpec((1, 8, SUBLANES, sb), lambda g, i: (g, 0, 0, 0))],
        out_specs=(tok, pl.BlockSpec((1, 1, SUBLANES, 2 * sb), lambda g, i: (g, i, 0, 0))),
        scratch_shapes=[pltpu.VMEM((rb, 2 * sb), F32), pltpu.VMEM((tbk, LANES), F32),
                        pltpu.VMEM((SUBLANES, 2 * sb), F32)],
        compiler_params=_params(2),
    )(useq, t_op, bp, cp, tab)


def _s5_bwd(useq, dy, hc, t_op, bp, cp, tab, tab_adj, *, reverse, name, hosted=None):
    ngb, n_seq, _ = useq.shape
    sb = bp.shape[2] // 2
    width = CHUNK * LANES
    rb = _s5_row_block(n_seq)
    tbk = rb * CHUNK
    steps = n_seq // tbk

    def fwd_step(i):
        return steps - 1 - i

    def blk(i):
        s = fwd_step(i)
        return (steps - 1 - s) if reverse else s

    def body(u_ref, dy_ref, hc_ref, t_ref, b_ref, c_ref, tab_ref, taba_ref, du_ref, dt_ref, db_ref, dc_ref, da_ref,
             h_scr, lam_scr, conv_scr, lcarry_scr, gedge_scr, da_scr):
        i = pl.program_id(1)
        first = i == 0

        @pl.when(first)
        def _():
            lcarry_scr[...] = jnp.zeros_like(lcarry_scr)
            gedge_scr[...] = jnp.zeros_like(gedge_scr)
            da_scr[...] = jnp.zeros_like(da_scr)

        rows = lax.broadcasted_iota(jnp.int32, (rb, 1), 0)
        uv = _rows_of_tokens(u_ref, conv_scr, rb)
        h_scr[...] = _dot(uv, b_ref[0])
        _scan_block(h_scr, tab_ref.at[0], hc_ref[0, 0, :, 0:sb], hc_ref[0, 0, :, sb:2 * sb], reverse=reverse, tb=rb, sb=sb)
        hprev = _shift_rows(h_scr[...], hc_ref[0, 0, 0:1, :], rows, rb, down=not reverse)

        dyv = _rows_of_tokens(dy_ref, conv_scr, rb)
        gy = _dot(dyv, c_ref[0])
        edge = gy[rb - 1:rb, :] if reverse else gy[0:1, :]
        lam_scr[...] = _shift_rows(gy, gedge_scr[...], rows, rb, down=reverse)
        gedge_scr[...] = edge
        lr, li = _scan_block(lam_scr, taba_ref.at[0], lcarry_scr[:, 0:sb], lcarry_scr[:, sb:2 * sb],
                             reverse=not reverse, tb=rb, sb=sb)
        lcarry_scr[:, 0:sb] = lr
        lcarry_scr[:, sb:2 * sb] = li

        lam = lam_scr[...]
        lam_bf = lam.astype(BF16)
        _tokens_of_rows(_dot_nt(dyv, t_ref[0]) + _dot_nt(lam_bf, b_ref[0]), du_ref, conv_scr, rb)
        _acc(dt_ref.at[0], first, _dot_tn(uv, dyv))
        _acc(db_ref.at[0], first, _dot_tn(uv, lam_bf))
        _acc(dc_ref.at[0], first, _dot_tn(dyv, hprev.astype(BF16)))
        lam_r, lam_i = lam[:, 0:sb], lam[:, sb:2 * sb]
        hp_r, hp_i = hprev[:, 0:sb], hprev[:, sb:2 * sb]
        da_scr[:, 0:sb] += _colsum(lam_r * hp_r + lam_i * hp_i)
        da_scr[:, sb:2 * sb] += _colsum(lam_i * hp_r - lam_r * hp_i)

        @pl.when(i == steps - 1)
        def _():
            da_ref[0] = da_scr[...]

    op = pl.BlockSpec((1, width, width), lambda g, i: (g, 0, 0))
    op2 = pl.BlockSpec((1, width, 2 * sb), lambda g, i: (g, 0, 0))
    tabs = pl.BlockSpec((1, 8, SUBLANES, sb), lambda g, i: (g, 0, 0, 0))
    tok = pl.BlockSpec((1, tbk, LANES), lambda g, i: (g, blk(i), 0))
    outs, extra = _call(
        body, name=name, grid=(ngb, steps),
        out_shape=[jax.ShapeDtypeStruct((ngb, n_seq, LANES), BF16),
                   jax.ShapeDtypeStruct((ngb, width, width), F32),
                   jax.ShapeDtypeStruct((ngb, width, 2 * sb), F32),
                   jax.ShapeDtypeStruct((ngb, width, 2 * sb), F32),
                   jax.ShapeDtypeStruct((ngb, 1, 2 * sb), F32)],
        in_specs=[tok, tok, pl.BlockSpec((1, 1, SUBLANES, 2 * sb), lambda g, i: (g, fwd_step(i), 0, 0)),
                  op, op2, op2, tabs, tabs],
        out_specs=[tok, op, op2, op2, pl.BlockSpec((1, 1, 2 * sb), lambda g, i: (g, 0, 0))],
        scratch_shapes=[pltpu.VMEM((rb, 2 * sb), F32), pltpu.VMEM((rb, 2 * sb), F32), pltpu.VMEM((tbk, LANES), F32),
                        pltpu.VMEM((SUBLANES, 2 * sb), F32), pltpu.VMEM((1, 2 * sb), F32), pltpu.VMEM((1, 2 * sb), F32)],
        args=(useq, dy, hc, t_op, bp, cp, tab, tab_adj), hosted=hosted)
    return (*outs, extra)


def _glu_loss(useq, yf, yb, z, xhat0, ln0, gt, d_vec, w_glu, b_glu, w_out, ln1, target, *, offs, dy_rows, tb, name):
    ngb = useq.shape[0]
    n_tok, d_model = xhat0.shape
    e = ngb * LANES
    tb = min(tb, n_tok)
    assert all(off % tb == 0 for off in offs) and all(off % tb == 0 for _, off in dy_rows)
    nz = z.shape[0]

    def body(u_ref, yf_ref, yb_ref, z_ref, xh0_ref, g0_ref, b0_ref, gt_ref, d_ref, wg_ref, bg_ref, wo_ref, g1_ref,
             b1_ref, t_ref, loss_ref, dxr_ref, do_ref, gz_ref, gg_ref, dq_ref, dz_ref, dyf_ref, dyb_ref, dg1_ref, db1_ref,
             dgt_ref, dbg_ref, dd_ref, loss_scr, yl_scr, th_scr, s_scr, dg_scr):
        i = pl.program_id(0)
        first = i == 0
        zw = e // nz
        cs = min(512, zw)

        def z_slab(c0):
            return z_ref[c0 // zw, :, c0 % zw:c0 % zw + cs].astype(F32)

        for q in range(ngb):
            sl = slice(q * LANES, (q + 1) * LANES)
            yl = d_ref[:, sl] * u_ref[q].astype(F32) + yf_ref[q].astype(F32) + yb_ref[q].astype(F32)
            th = jnp.tanh(GELU_K * (yl + GELU_C * yl * yl * yl))
            yl_scr[:, sl] = yl
            th_scr[:, sl] = th
            gg_ref[:, sl] = (0.5 * yl * (1.0 + th)).astype(BF16)
        s_scr[...] = _sigmoid(_dot(gg_ref[...], wg_ref[...]) + bg_ref[...])
        for c0 in range(0, e, cs):
            sl = slice(c0, c0 + cs)
            zf = z_slab(c0)
            g2 = 0.5 * yl_scr[:, sl] * (1.0 + th_scr[:, sl]) * s_scr[:, sl]
            gz_ref[:, sl] = (g2 * (zf * _sigmoid(zf))).astype(BF16)
        o = _dot(gz_ref[...], wo_ref[...])
        x1 = xh0_ref[...] * g0_ref[...] + b0_ref[...]
        r = DN_ALPHA * x1 + gt_ref[...] * o
        rc = r - _rowmean(r)
        rstd = lax.rsqrt(_rowmean(rc * rc) + LN_EPS)
        xh = rc * rstd
        err = xh * g1_ref[...] + b1_ref[...] - t_ref[...]
        _acc(loss_scr, first, _colsum(err * err))
        dy = err * (1.0 / d_model)
        _acc(dg1_ref, first, _colsum(dy * xh))
        _acc(db1_ref, first, _colsum(dy))
        dxh = dy * g1_ref[...]
        dr = rstd * (dxh - _rowmean(dxh) - xh * _rowmean(dxh * xh))
        dxr_ref[...] = DN_ALPHA * dr
        _acc(dgt_ref, first, _colsum(dr * o))
        do_bf = (dr * gt_ref[...]).astype(BF16)
        do_ref[...] = do_bf
        dg_scr[...] = _dot_nt(do_bf, wo_ref[...])
        for c0 in range(0, e, cs):
            sl = slice(c0, c0 + cs)
            zf = z_slab(c0)
            sz = _sigmoid(zf)
            g = 0.5 * yl_scr[:, sl] * (1.0 + th_scr[:, sl])
            s = s_scr[:, sl]
            dgz = dg_scr[:, sl]
            dg2 = dgz * (zf * sz)
            dz_ref[:, sl] = (dgz * (g * s) * (sz * (1.0 + zf * (1.0 - sz)))).astype(BF16)
            dq = dg2 * g * s * (1.0 - s)
            _acc(dbg_ref.at[:, sl], first, _colsum(dq))
            dq_ref[:, sl] = dq.astype(BF16)
            dg_scr[:, sl] = dg2 * s
        dg_scr[...] += _dot_nt(dq_ref[...], wg_ref[...])
        for q in range(ngb):
            sl = slice(q * LANES, (q + 1) * LANES)
            yl = yl_scr[:, sl]
            th = th_scr[:, sl]
            dgelu = 0.5 * (1.0 + th) + 0.5 * yl * (1.0 - th * th) * (GELU_K * (1.0 + 3.0 * GELU_C * yl * yl))
            dyl = dg_scr[:, sl] * dgelu
            _acc(dd_ref.at[:, sl], first, _colsum(dyl * u_ref[q].astype(F32)))
            dyf_ref[q] = dyl.astype(BF16)
            dyb_ref[q] = dyl.astype(BF16)

        @pl.when(i == pl.num_programs(0) - 1)
        def _():
            loss_ref[...] = (0.5 / d_model) * jnp.sum(loss_scr[...], axis=1, keepdims=True)

    vec = pl.BlockSpec((1, d_model), lambda i: (0, 0))
    evec = pl.BlockSpec((1, e), lambda i: (0, 0))
    tok = pl.BlockSpec((tb, d_model), lambda i: (i, 0))
    wide = pl.BlockSpec((tb, e), lambda i: (i, 0))
    def gblk(off):
        return pl.BlockSpec((ngb, tb, LANES), functools.partial(lambda i, ob: (0, i + ob, 0), ob=off // tb))

    once = dict(pipeline_mode=pl.Buffered(1))
    tok_f = jax.ShapeDtypeStruct((n_tok, d_model), F32)
    tok_b = jax.ShapeDtypeStruct((n_tok, d_model), BF16)
    wide_b = jax.ShapeDtypeStruct((n_tok, e), BF16)
    vec_f = jax.ShapeDtypeStruct((1, d_model), F32)
    evec_f = jax.ShapeDtypeStruct((1, e), F32)
    return pl.pallas_call(
        body, name=name, grid=(n_tok // tb,),
        out_shape=(jax.ShapeDtypeStruct((1, 1), F32), tok_f, tok_b, wide_b, wide_b, wide_b, wide_b,
                   *[jax.ShapeDtypeStruct((ngb, total, LANES), BF16) for total, _ in dy_rows],
                   vec_f, vec_f, vec_f, evec_f, evec_f),
        in_specs=[gblk(offs[0]), gblk(offs[1]), gblk(offs[2]),
                  pl.BlockSpec((nz, tb, e // nz), lambda i: (0, i, 0)), tok, vec, vec, vec, evec,
                  pl.BlockSpec((e, e), lambda i: (0, 0), **once), evec,
                  pl.BlockSpec((e, d_model), lambda i: (0, 0), **once), vec, vec, tok],
        out_specs=(pl.BlockSpec((1, 1), lambda i: (0, 0)), tok, tok, wide, wide, wide, wide,
                   *[gblk(off) for _, off in dy_rows], vec, vec, vec, evec, evec),
        scratch_shapes=[pltpu.VMEM((1, d_model), F32)] + [pltpu.VMEM((tb, e), F32)] * 4,
        compiler_params=_params(1),
    )(useq, yf, yb, z, xhat0, ln0[0], ln0[1], gt, d_vec, w_glu, b_glu, w_out, ln1[0], ln1[1], target)


def _ssm_inbwd(duf, dub, w, xhat, rstd, ln, sc, gt_prev, f_prev, *, lat, row_f, row_b, tb, name):
    ngb = duf.shape[0]
    e = ngb * LANES
    n_tok, d_model = xhat.shape
    tb = min(tb, n_tok)
    obf, obb = row_f // tb, row_b // tb
    has_lat = lat is not None
    n_w = w.shape[0] if has_lat else w.shape[0] // 2

    def body(*refs):
        if has_lat:
            (duf_ref, dub_ref, dyl_ref, dz_ref, d_ref, dxr_ref, w_ref, xh_ref, rs_ref, g_ref, b_ref, sc_ref, gt_ref,
             f_ref, dp_ref, dr_ref, df_ref, dsc_ref, dsh_ref, dg_ref, db_ref, dgt_ref) = refs
        else:
            (duf_ref, dub_ref, w_ref, xh_ref, rs_ref, g_ref, b_ref, sc_ref, gt_ref, f_ref, dp_ref, dr_ref, df_ref,
             dsc_ref, dsh_ref, dg_ref, db_ref, dgt_ref) = refs
        first = pl.program_id(0) == 0
        du = (jnp.concatenate([duf_ref[q] for q in range(ngb)], axis=1).astype(F32)
              + jnp.concatenate([dub_ref[q] for q in range(ngb)], axis=1).astype(F32))
        if has_lat:
            du = du + d_ref[...] * jnp.concatenate([dyl_ref[q] for q in range(ngb)], axis=1).astype(F32)
            dp_ref[:, e:2 * e] = dz_ref[...]
        else:
            dp_ref[:, e:2 * e] = jnp.zeros((tb, e), BF16)
        dp_ref[:, 0:e] = du.astype(BF16)
        dh = jnp.zeros((tb, d_model), F32)
        for j in range(n_w):
            dh = dh + _dot_nt(dp_ref[:, j * d_model:(j + 1) * d_model], w_ref[j])
        xh = xh_ref[...]
        x1 = xh * g_ref[...] + b_ref[...]
        dx1 = dh * (1.0 + sc_ref[...])
        if has_lat:
            dx1 = dx1 + dxr_ref[...]
        _acc(dsc_ref, first, _colsum(dh * x1))
        _acc(dsh_ref, first, _colsum(dh))
        _acc(dg_ref, first, _colsum(dx1 * xh))
        _acc(db_ref, first, _colsum(dx1))
        dxh = dx1 * g_ref[...]
        dr = rs_ref[...] * (dxh - _rowmean(dxh) - xh * _rowmean(dxh * xh))
        dr_ref[...] = dr
        df_ref[...] = (dr * gt_ref[...]).astype(BF16)
        _acc(dgt_ref, first, _colsum(dr * f_ref[...].astype(F32)))

    vec = pl.BlockSpec((1, d_model), lambda i: (0, 0))
    tok = pl.BlockSpec((tb, d_model), lambda i: (i, 0))
    gblk = pl.BlockSpec((ngb, tb, LANES), lambda i: (0, i, 0))
    in_specs = [pl.BlockSpec((ngb, tb, LANES), lambda i: (0, i + obf, 0)),
                pl.BlockSpec((ngb, tb, LANES), lambda i: (0, i + obb, 0))]
    args = [duf, dub]
    if has_lat:
        in_specs += [gblk, pl.BlockSpec((tb, e), lambda i: (i, 0)), pl.BlockSpec((1, e), lambda i: (0, 0)), tok]
        args += list(lat)
    in_specs += [pl.BlockSpec(w.shape, lambda i: (0, 0, 0)), tok, pl.BlockSpec((tb, 1), lambda i: (i, 0)), vec, vec, vec,
                 vec, tok]
    args += [w, xhat, rstd, ln[0], ln[1], sc, gt_prev, f_prev]
    vec_f = jax.ShapeDtypeStruct((1, d_model), F32)
    return pl.pallas_call(
        body, name=name, grid=(n_tok // tb,),
        out_shape=(jax.ShapeDtypeStruct((n_tok, 2 * e), BF16), jax.ShapeDtypeStruct((n_tok, d_model), F32),
                   jax.ShapeDtypeStruct((n_tok, d_model), BF16), vec_f, vec_f, vec_f, vec_f, vec_f),
        in_specs=in_specs,
        out_specs=(pl.BlockSpec((tb, 2 * e), lambda i: (i, 0)), tok, tok, vec, vec, vec, vec, vec),
        compiler_params=_params(1),
    )(*args)


def _conv_bwd_a(df, w_out, p, yc, *, tb, name):
    _, n_tok, e = p.shape
    d_model = df.shape[1]
    tb = min(tb, n_tok)
    cs = _slab_width(e)

    def body(df_ref, wo_ref, bg_ref, z_ref, yc_ref, dbg_ref, dz_ref, dyc_ref):
        dfv = df_ref[...]
        for c0 in range(0, e, cs):
            sl = slice(c0, c0 + cs)
            dgv = _dot_nt(dfv, wo_ref[sl, :])
            zf = z_ref[0, :, sl].astype(F32)
            sz = _sigmoid(zf)
            silu_z = zf * sz
            bg = bg_ref[0, :, sl].astype(F32)
            yc = yc_ref[:, sl].astype(F32)
            dbg_ref[:, sl] = (dgv * yc * silu_z).astype(BF16)
            dyc_ref[:, sl] = (dgv * bg * silu_z).astype(BF16)
            dz_ref[:, sl] = (dgv * bg * yc * (sz * (1.0 + zf * (1.0 - sz)))).astype(BF16)

    wide = pl.BlockSpec((tb, e), lambda i: (i, 0))
    shape = jax.ShapeDtypeStruct((n_tok, e), BF16)
    return pl.pallas_call(
        body, name=name, grid=(n_tok // tb,), out_shape=(shape, shape, shape),
        in_specs=[pl.BlockSpec((tb, d_model), lambda i: (i, 0)), pl.BlockSpec((e, d_model), lambda i: (0, 0)),
                  pl.BlockSpec((1, tb, e), lambda i: (0, i, 0)), pl.BlockSpec((1, tb, e), lambda i: (3, i, 0)), wide],
        out_specs=(wide, wide, wide), compiler_params=_params(1),
    )(df, w_out, p, p, yc)


def _conv_bwd_b(dyc, p, dbg, dz, conv_w, *, grid_mode, tb, name, hosted=None):
    _, n_tok, e = p.shape
    eh = e // 2
    if not grid_mode:
        tb = n_tok
    tb = min(tb, n_tok)
    nb = n_tok // tb
    hb = tb // GRID_W
    cs = _slab_width(e)

    def body(*refs):
        if grid_mode:
            dyc_ref, dycp_ref, dycn_ref, cg_ref, v_ref, dbg_ref, dz_ref, cw_ref, dp_ref, dcw_ref = refs
        else:
            dyc_ref, cg_ref, v_ref, dbg_ref, dz_ref, cw_ref, dp_ref, dcw_ref = refs
        i = pl.program_id(0)
        first = i == 0
        rows = lax.broadcasted_iota(jnp.int32, (tb, 1), 0)
        dp_ref[0] = dbg_ref[...]
        dp_ref[3] = dz_ref[...]
        for c0 in range(0, e, cs):
            sl = slice(c0, c0 + cs)
            dyc = dyc_ref[:, sl].astype(F32)
            w = cw_ref[:, sl]
            if grid_mode and c0 >= eh:
                hs = slice(c0 - eh, c0 - eh + cs)
                dprev = jnp.where(i > 0, dycp_ref[:, hs].astype(F32), 0.0)
                dnext = jnp.where(i < nb - 1, dycn_ref[:, hs].astype(F32), 0.0)
                if tb > GRID_W:
                    dm = jnp.concatenate([dprev, dyc[:tb - GRID_W]], axis=0)
                    dpl = jnp.concatenate([dyc[GRID_W:], dnext], axis=0)
                else:
                    dm, dpl = dprev, dnext
            else:
                dm, dpl = _shifted(dyc, rows, GRID_W if grid_mode else tb, tb)
            cg = cg_ref[0, :, sl].astype(F32)
            v = v_ref[0, :, sl].astype(F32)
            u = cg * v
            du = w[0:1] * dpl + w[1:2] * dyc + w[2:3] * dm
            dp_ref[1, :, sl] = (du * v).astype(BF16)
            dp_ref[2, :, sl] = (du * cg).astype(BF16)
            _acc(dcw_ref.at[:, sl], first, jnp.concatenate([_colsum(u * dpl), _colsum(u * dyc), _colsum(u * dm)], axis=0))

    n_hrows = n_tok // GRID_W
    wide = pl.BlockSpec((tb, e), lambda i: (i, 0))
    in_specs = [wide]
    args = [dyc]
    if grid_mode:
        in_specs += [pl.BlockSpec((GRID_W, eh), lambda i: (jnp.maximum(i * hb - 1, 0), 1)),
                     pl.BlockSpec((GRID_W, eh), lambda i: (jnp.minimum((i + 1) * hb, n_hrows - 1), 1))]
        args += [dyc, dyc]
    in_specs += [pl.BlockSpec((1, tb, e), lambda i: (1, i, 0)), pl.BlockSpec((1, tb, e), lambda i: (2, i, 0)), wide, wide,
                 pl.BlockSpec((3, e), lambda i: (0, 0))]
    args += [p, p, dbg, dz, conv_w]
    outs, extra = _call(
        body, name=name, grid=(nb,),
        out_shape=[jax.ShapeDtypeStruct((4, n_tok, e), BF16), jax.ShapeDtypeStruct((3, e), F32)],
        in_specs=in_specs,
        out_specs=[pl.BlockSpec((4, tb, e), lambda i: (0, i, 0)), pl.BlockSpec((3, e), lambda i: (0, 0))],
        scratch_shapes=[], args=args, hosted=hosted)
    return (*outs, extra)


def _conv_inbwd(dp, w, dr, x, sc, *, tb, name, hosted=None):
    n_chunks, n_tok, e = dp.shape
    d_model = x.shape[1]
    tb = min(tb, n_tok)

    def body(dp_ref, w_ref, dr_ref, x_ref, sc_ref, gx_ref, dsc_ref, dsh_ref, dh_scr):
        k = pl.program_id(1)
        first = pl.program_id(0) == 0
        _acc(dh_scr, k == 0, _dot_nt(dp_ref[0], w_ref[0]))

        @pl.when(k == n_chunks - 1)
        def _():
            dh = dh_scr[...]
            gx_ref[...] = DN_ALPHA * dr_ref[...] + dh * (1.0 + sc_ref[...])
            _acc(dsc_ref, first, _colsum(dh * x_ref[...]))
            _acc(dsh_ref, first, _colsum(dh))

    vec = pl.BlockSpec((1, d_model), lambda i, k: (0, 0))
    tok = pl.BlockSpec((tb, d_model), lambda i, k: (i, 0))
    vec_f = jax.ShapeDtypeStruct((1, d_model), F32)
    outs, extra = _call(
        body, name=name, grid=(n_tok // tb, n_chunks),
        out_shape=[jax.ShapeDtypeStruct((n_tok, d_model), F32), vec_f, vec_f],
        in_specs=[pl.BlockSpec((1, tb, e), lambda i, k: (k, i, 0)), pl.BlockSpec((1, d_model, e), lambda i, k: (k, 0, 0)),
                  tok, tok, vec],
        out_specs=[tok, vec, vec],
        scratch_shapes=[pltpu.VMEM((tb, d_model), F32)], args=(dp, w, dr, x, sc), hosted=hosted)
    return (*outs, extra)


def _wgrad(a, b, *, n_chunks, tm, tl, init=None, name):
    n_tok, m = a.shape
    tl = min(tl, n_tok)
    chunked = b.ndim == 3
    cw = b.shape[2] if chunked else b.shape[1] // n_chunks
    has_init = init is not None

    def body(*refs):
        if has_init:
            a_ref, b_ref, init_ref, o_ref = refs
        else:
            a_ref, b_ref, o_ref = refs
        bv = b_ref[0] if chunked else b_ref[...]
        part = _dot_tn(a_ref[...], bv)
        l = pl.program_id(2)

        @pl.when(l == 0)
        def _():
            o_ref[0] = part + init_ref[0] if has_init else part

        @pl.when(l > 0)
        def _():
            o_ref[0] += part

    o_spec = pl.BlockSpec((1, tm, cw), lambda jm, jc, l: (jc, jm, 0))
    b_spec = (pl.BlockSpec((1, tl, cw), lambda jm, jc, l: (jc, l, 0)) if chunked
              else pl.BlockSpec((tl, cw), lambda jm, jc, l: (l, jc)))
    in_specs = [pl.BlockSpec((tl, tm), lambda jm, jc, l: (l, jm)), b_spec] + ([o_spec] if has_init else [])
    args = (a, b) + ((init,) if has_init else ())
    return pl.pallas_call(
        body, name=name, grid=(m // tm, n_chunks, n_tok // tl),
        out_shape=jax.ShapeDtypeStruct((n_chunks, m, cw), F32),
        in_specs=in_specs, out_specs=o_spec, compiler_params=_params(3),
    )(*args)


def _block_diag(t, ngb):
    g, p, n = t.shape
    gpb = g // ngb
    eye = jnp.eye(gpb, dtype=t.dtype)
    return jnp.einsum("bgpn,gh->bgphn", t.reshape(ngb, gpb, p, n), eye).reshape(ngb, gpb * p, gpb * n)


def _block_diag_t(mat, g, p, n):
    ngb = mat.shape[0]
    gpb = g // ngb
    eye = jnp.eye(gpb, dtype=mat.dtype)
    return jnp.einsum("bgphn,gh->bgpn", mat.reshape(ngb, gpb, p, gpb, n), eye).reshape(g, p, n)


def _scan_tables(pw_r, pw_i, ngb, reverse):
    _, g, n = pw_r.shape
    sb = g * n // ngb
    rows = jnp.arange(SUBLANES)
    kinds = []
    for step in (1, 2, 4):
        mask = ((rows < SUBLANES - step) if reverse else (rows >= step)).astype(F32)
        for part in (pw_r[step - 1], pw_i[step - 1]):
            kinds.append(part.reshape(ngb, 1, sb) * mask[None, :, None])
    for part in (pw_r, pw_i):
        pw = part[::-1] if reverse else part
        kinds.append(jnp.transpose(pw.reshape(SUBLANES, ngb, sb), (1, 0, 2)))
    return jnp.stack(kinds, axis=1)


def _flat(parts):
    return jnp.concatenate([p.reshape(-1) for p in parts])


def _unflat(vec, shapes):
    out, off = [], 0
    for s in shapes:
        size = math.prod(s)
        out.append(vec[off:off + size].reshape(s))
        off += size
    return out


def kernel(x, c, ctx, c_ctx, ada_w, ada_b, ln_g, ln_b, conv_w_in, conv_w, conv_w_out, ssm_w_in, ssm_lam_re, ssm_lam_im, ssm_log_step, ssm_b_re, ssm_b_im, ssm_c_re, ssm_c_im, ssm_d, ssm_w_glu, ssm_b_glu, ssm_w_out, loss_target, m_c_ctx, m_ada_w, m_ada_b, m_ln_g, m_ln_b, m_conv_w_in, m_conv_w, m_conv_w_out, m_ssm_w_in, m_ssm_lam_re, m_ssm_lam_im, m_ssm_log_step, m_ssm_b_re, m_ssm_b_im, m_ssm_c_re, m_ssm_c_im, m_ssm_d, m_ssm_w_glu, m_ssm_b_glu, m_ssm_w_out, v_c_ctx, v_ada_w, v_ada_b, v_ln_g, v_ln_b, v_conv_w_in, v_conv_w, v_conv_w_out, v_ssm_w_in, v_ssm_lam_re, v_ssm_lam_im, v_ssm_log_step, v_ssm_b_re, v_ssm_b_im, v_ssm_c_re, v_ssm_c_im, v_ssm_d, v_ssm_w_glu, v_ssm_b_glu, v_ssm_w_out):
    weights = dict(c_ctx=c_ctx, ada_w=ada_w, ada_b=ada_b, ln_g=ln_g, ln_b=ln_b, conv_w_in=conv_w_in, conv_w=conv_w,
                   conv_w_out=conv_w_out, ssm_w_in=ssm_w_in, ssm_lam_re=ssm_lam_re, ssm_lam_im=ssm_lam_im,
                   ssm_log_step=ssm_log_step, ssm_b_re=ssm_b_re, ssm_b_im=ssm_b_im, ssm_c_re=ssm_c_re,
                   ssm_c_im=ssm_c_im, ssm_d=ssm_d, ssm_w_glu=ssm_w_glu, ssm_b_glu=ssm_b_glu, ssm_w_out=ssm_w_out)
    mom_m = dict(c_ctx=m_c_ctx, ada_w=m_ada_w, ada_b=m_ada_b, ln_g=m_ln_g, ln_b=m_ln_b, conv_w_in=m_conv_w_in,
                 conv_w=m_conv_w, conv_w_out=m_conv_w_out, ssm_w_in=m_ssm_w_in, ssm_lam_re=m_ssm_lam_re,
                 ssm_lam_im=m_ssm_lam_im, ssm_log_step=m_ssm_log_step, ssm_b_re=m_ssm_b_re, ssm_b_im=m_ssm_b_im,
                 ssm_c_re=m_ssm_c_re, ssm_c_im=m_ssm_c_im, ssm_d=m_ssm_d, ssm_w_glu=m_ssm_w_glu,
                 ssm_b_glu=m_ssm_b_glu, ssm_w_out=m_ssm_w_out)
    mom_v = dict(c_ctx=v_c_ctx, ada_w=v_ada_w, ada_b=v_ada_b, ln_g=v_ln_g, ln_b=v_ln_b, conv_w_in=v_conv_w_in,
                 conv_w=v_conv_w, conv_w_out=v_conv_w_out, ssm_w_in=v_ssm_w_in, ssm_lam_re=v_ssm_lam_re,
                 ssm_lam_im=v_ssm_lam_im, ssm_log_step=v_ssm_log_step, ssm_b_re=v_ssm_b_re, ssm_b_im=v_ssm_b_im,
                 ssm_c_re=v_ssm_c_re, ssm_c_im=v_ssm_c_im, ssm_d=v_ssm_d, ssm_w_glu=v_ssm_w_glu,
                 ssm_b_glu=v_ssm_b_glu, ssm_w_out=v_ssm_w_out)
    names = list(weights)

    n_lat, d_model = x.shape[1], x.shape[2]
    n_ctx = ctx.shape[1]
    e = 2 * d_model
    n_grp, n_state, grp = ssm_lam_re.shape[2], ssm_lam_re.shape[3], ssm_b_re.shape[4]
    ngb = e // LANES
    ws = ada_w.shape[2]
    tb_tok = min(512, n_lat)
    n_seq = n_ctx + n_lat
    tb_glu = math.gcd(256, n_ctx)
    chip = 2 * lax.axis_index("x") + lax.axis_index("y")
    me = 2 * chip + lax.axis_index("c")
    chips, everyone, pair = ("x", "y"), MESH_AXES, ("c",)

    x2, ctx2, tgt2 = x[0], ctx[0], loss_target[0]

    wc_in = _exchange(conv_w_in[0].astype(BF16), chips, False, "ag_conv_w_in")
    later_weights = _Hosted([(w[0].astype(BF16), chips, False) for w in (conv_w_out, ssm_w_in, ssm_w_glu, ssm_w_out)])
    small_full = _exchange(_flat([conv_w[0], ssm_d[0], ssm_b_glu[0]]).reshape(1, -1), chips, False, "ag_small")
    es = conv_w.shape[2]
    conv_w_full = jnp.transpose(small_full[:, 0, :3 * es].reshape(4, 3, es), (1, 0, 2)).reshape(3, e)
    d_full = small_full[:, 0, 3 * es:4 * es].reshape(1, e)
    b_glu_full = small_full[:, 0, 4 * es:5 * es].reshape(1, e)

    c_all = _exchange(c, everyone, False, "ag_c").reshape(8, d_model)
    cc2 = c_ctx.reshape(1, d_model)
    b_sh = lax.dynamic_slice_in_dim(ada_b, chip * ws, ws, axis=1).reshape(DEPTH, 1, ws)
    m_sh = _ada_fwd(c_all, cc2, ada_w, b_sh)
    m_all = _exchange(m_sh, chips, False, "ag_mod")
    m_full = jnp.transpose(m_all, (1, 2, 0, 3)).reshape(DEPTH, 16, 3 * d_model)
    m_lat = lax.dynamic_slice_in_dim(m_full, me, 1, axis=1)
    m_ctx = m_full[:, 8:9]

    def mods(m, i):
        return m[i, :, 0:d_model], m[i, :, d_model:2 * d_model], m[i, :, 2 * d_model:3 * d_model]

    sh0, sc0, gt0 = mods(m_lat, 0)
    sh1, sc1, gt1 = mods(m_lat, 1)
    shc0, scc0, gtc0 = mods(m_ctx, 0)
    shc1, scc1, _ = mods(m_ctx, 1)
    ln0 = (ln_g[0:1], ln_b[0:1])
    ln1 = (ln_g[1:2], ln_b[1:2])

    rg = 2 * n_grp
    lam_re2 = ssm_lam_re[0].reshape(rg, n_state)
    lam_im2 = ssm_lam_im[0].reshape(rg, n_state)
    log_step2 = ssm_log_step[0].reshape(rg, 1)
    b_re_t = jnp.transpose(ssm_b_re[0], (3, 0, 1, 2)).reshape(grp, rg, n_state)
    b_im_t = jnp.transpose(ssm_b_im[0], (3, 0, 1, 2)).reshape(grp, rg, n_state)
    pw_r, pw_i, pq_r, pq_i, bbr, bbi = _zoh_fwd(lam_re2, lam_im2, log_step2, b_re_t, b_im_t)
    sbk = n_grp * n_state // ngb
    pw_r, pw_i, pq_r, pq_i = (t.reshape(8, 2, n_grp, n_state) for t in (pw_r, pw_i, pq_r, pq_i))
    bbr_g = jnp.transpose(bbr.reshape(grp, 2, n_grp, n_state), (1, 2, 0, 3))
    bbi_g = jnp.transpose(bbi.reshape(grp, 2, n_grp, n_state), (1, 2, 0, 3))

    def power_rows(pw, r, first):
        full = jnp.concatenate([jnp.full((1, n_grp, n_state), first, F32), pw[:, r]], axis=0)
        return jnp.transpose(full.reshape(CHUNK + 1, ngb, sbk), (1, 0, 2))

    s5 = []
    for r in range(2):
        prm = dict(bre=_block_diag(bbr_g[r], ngb), bim=_block_diag(bbi_g[r], ngb),
                   cre=_block_diag(ssm_c_re[0, r], ngb), cim=_block_diag(ssm_c_im[0, r], ngb),
                   wr=power_rows(pw_r, r, 1.0), wi=power_rows(pw_i, r, 0.0))
        t_op, bp_op, cp_op = _s5_ops(prm["bre"], prm["bim"], prm["cre"], prm["cim"], prm["wr"], prm["wi"],
                                     reverse=(r == 1), name=f"l1_s5_ops{r}")
        s5.append(dict(
            prm, t=t_op, bp=bp_op, cp=cp_op,
            tab=_scan_tables(pq_r[:, r], pq_i[:, r], ngb, reverse=(r == 1)),
            tab_adj=_scan_tables(pq_r[:, r], -pq_i[:, r], ngb, reverse=(r == 0))))

    p0, h0, gathered = _inproj(x2, sc0, sh0, wc_in, tb=tb_tok, name="l0_inproj", hosted=later_weights)
    wc_out, ws_in, w_glu, ws_out = gathered
    wc_out, w_glu, ws_out = wc_out.reshape(e, d_model), w_glu.reshape(e, e), ws_out.reshape(e, d_model)
    pc0, hc0 = _inproj(ctx2, scc0, shc0, wc_in, tb=tb_tok, name="l0_inproj_ctx")
    xhat0, rstd0, g0, yc0, f0 = _convgate(p0, x2, gt0, conv_w_full, wc_out, *ln0, grid_mode=True, tb=tb_tok, name="l0_conv")
    chat0, crstd0, gc0, ycc0, fc0 = _convgate(pc0, ctx2, gtc0, conv_w_full, wc_out, *ln0, grid_mode=False, tb=tb_tok,
                                              name="l0_conv_ctx")

    seq_rows = [(n_seq, n_ctx), (n_seq, 0)]
    useq_f, useq_b, h1 = _inproj(xhat0, sc1, sh1, ws_in[0:2], lnaff=ln0, tb=math.gcd(tb_tok, n_ctx), gb_rows=seq_rows,
                                 name="l1_inproj_u")
    z1, _ = _inproj(xhat0, sc1, sh1, ws_in[2:4], lnaff=ln0, tb=tb_tok, name="l1_inproj_z")
    uc, hc1 = _inproj(chat0, scc1, shc1, ws_in[0:2], lnaff=ln0, tb=tb_tok, gb_rows=[(n_ctx, 0)], name="l1_inproj_ctx")
    useq = [useq_f.at[:, 0:n_ctx].set(uc), useq_b.at[:, n_lat:].set(uc)]
    y_dir, hc_dir = [], []
    for r in range(2):
        yr, hcr = _s5_fwd(useq[r], s5[r]["t"], s5[r]["bp"], s5[r]["cp"], s5[r]["tab"], reverse=(r == 1),
                          name=f"l1_s5_fwd{r}")
        y_dir.append(yr)
        hc_dir.append(hcr)

    (loss, dxres, do1, gz1, gg1, dq1, dz1, dy_f, dy_b, dg1, db1, dgt1, dbglu, dd) = _glu_loss(
        useq[0], y_dir[0], y_dir[1], z1, xhat0, ln0, gt1, d_full, w_glu, b_glu_full, ws_out, ln1, tgt2,
        offs=(n_ctx, n_ctx, 0), dy_rows=seq_rows, tb=tb_glu, name="l1_glu_loss")
    no_dy = jnp.zeros((ngb, n_ctx, LANES), BF16)
    dy_dir = [dy_f.at[:, 0:n_ctx].set(no_dy), dy_b.at[:, n_lat:].set(no_dy)]

    tl = tb_tok

    def owner_slices(name, full):
        w = weights[name]
        return full.reshape(8, math.prod(w.shape[:-1]) // 2, w.shape[-1])

    def scatter(named):
        return _Hosted([(owner_slices(name, full), everyone, True) for name, full in named])

    rs_parts = {}

    gw_glu = _wgrad(gg1, dq1, n_chunks=1, tm=e // 2, tl=tl, name="wg_glu")
    gw_ssm_out = _wgrad(gz1, do1, n_chunks=1, tm=e, tl=tl, name="wg_ssm_out")
    du_dir, s5_grads = [], []
    for r in range(2):
        hosted = scatter([("ssm_w_glu", gw_glu), ("ssm_w_out", gw_ssm_out)]) if r == 0 else None
        dur, dt_op, dbp_op, dcp_op, da8, extra = _s5_bwd(useq[r], dy_dir[r], hc_dir[r], s5[r]["t"], s5[r]["bp"],
                                                         s5[r]["cp"], s5[r]["tab"], s5[r]["tab_adj"], reverse=(r == 1),
                                                         name=f"l1_s5_bwd{r}", hosted=hosted)
        if r == 0:
            rs_parts["ssm_w_glu"], rs_parts["ssm_w_out"] = extra
        du_dir.append(dur)
        prm = s5[r]
        s5_grads.append(_s5_ops_bwd(prm["bre"], prm["bim"], prm["cre"], prm["cim"], prm["wr"], prm["wi"],
                                    prm["wr"][:, 1:2], prm["wi"][:, 1:2], dt_op, dbp_op, dcp_op, da8,
                                    reverse=(r == 1), name=f"l1_s5_ops_bwd{r}"))
    dp1, dr0, df0, dsc1, dsh1, dg0, db0, dgt0 = _ssm_inbwd(
        du_dir[0], du_dir[1], ws_in, xhat0, rstd0, ln0, sc1, gt0, f0, lat=(dy_dir[1], dz1, d_full, dxres),
        row_f=n_ctx, row_b=0, tb=tb_glu, name="l1_inbwd")
    dpc1, drc0, dfc0, dscc1, dshc1, dgc0, dbc0, dgtc0 = _ssm_inbwd(
        du_dir[0], du_dir[1], ws_in, chat0, crstd0, ln0, scc1, gtc0, fc0, lat=None,
        row_f=0, row_b=n_lat, tb=n_ctx, name="l1_inbwd_ctx")

    def conv_backward(df, p, yc, dr, xin, sc, grid_mode, tag, hosted_b=None, hosted_in=None):
        dbg, dz, dyc = _conv_bwd_a(df, wc_out, p, yc, tb=tb_tok, name="l0_bwd_a" + tag)
        dp, dcw, extra_b = _conv_bwd_b(dyc, p, dbg, dz, conv_w_full, grid_mode=grid_mode, tb=tb_glu,
                                       name="l0_bwd_b" + tag, hosted=hosted_b)
        gx, dsc, dsh, extra_in = _conv_inbwd(dp, wc_in, dr, xin, sc, tb=tb_tok, name="l0_inbwd" + tag,
                                             hosted=None if hosted_in is None else hosted_in(dp))
        return dp, dcw, gx, dsc, dsh, extra_b, extra_in

    dpc0, dcwc0, _, dscc0, dshc0, _, _ = conv_backward(dfc0, pc0, ycc0, drc0, ctx2, scc0, False, "_ctx")
    gw_conv_out = _wgrad(g0, df0, n_chunks=1, tm=e, tl=tl, name="wg_conv_out",
                         init=_wgrad(gc0, dfc0, n_chunks=1, tm=e, tl=tl, name="wg_conv_out_ctx"))
    gw_ssm_in = _wgrad(h1, dp1, n_chunks=4, tm=d_model, tl=tl, name="wg_ssm_in",
                       init=_wgrad(hc1, dpc1, n_chunks=4, tm=d_model, tl=tl, name="wg_ssm_in_ctx"))
    gw_conv_in_ctx = _wgrad(hc0, dpc0, n_chunks=4, tm=d_model, tl=tl, name="wg_conv_in_ctx")

    def conv_in_scatter(dp):
        gw_conv_in = _wgrad(h0, dp, n_chunks=4, tm=d_model, tl=tl, name="wg_conv_in", init=gw_conv_in_ctx)
        return scatter([("conv_w_in", gw_conv_in)])

    dp0, dcw0, grad_x, dsc0, dsh0, extra_b, extra_in = conv_backward(
        df0, p0, yc0, dr0, x2, sc0, True, "", hosted_b=scatter([("ssm_w_in", gw_ssm_in), ("conv_w_out", gw_conv_out)]),
        hosted_in=conv_in_scatter)
    rs_parts["ssm_w_in"], rs_parts["conv_w_out"] = extra_b
    rs_parts["conv_w_in"], = extra_in

    grads, deltas, new_m, new_v = {}, {}, {}, {}
    for name in ("ssm_w_glu", "ssm_w_out", "ssm_w_in", "conv_w_out", "conv_w_in"):
        w = weights[name]
        rows, cols = math.prod(w.shape[:-1]), w.shape[-1]
        half = _sum_parts(rs_parts[name], "sum_" + name)
        both = _exchange(half, pair, False, "pair_" + name).reshape(rows, cols)
        dlt, nm, nv = _adamw(w.reshape(rows, cols), both, mom_m[name].reshape(rows, cols),
                             mom_v[name].reshape(rows, cols), "adamw_" + name)
        grads[name], deltas[name] = both.reshape(w.shape), dlt.reshape(w.shape)
        new_m[name], new_v[name] = nm.reshape(w.shape), nv.reshape(w.shape)

    gpn = (n_grp, grp, n_state)
    small_parts = [
        jnp.concatenate([dg0 + dgc0, dg1], axis=0), jnp.concatenate([db0 + dbc0, db1], axis=0),
        dcw0 + dcwc0, dd, dbglu,
        jnp.stack([s5_grads[r][4] for r in range(2)]),
    ] + [jnp.stack([_block_diag_t(s5_grads[r][k], *gpn) for r in range(2)]) for k in range(4)]
    small_shapes = [p.shape for p in small_parts]
    flat = _flat(small_parts)
    quantum = 8 * SUBLANES * LANES
    n_flat = -(-flat.shape[0] // quantum) * quantum
    flat = jnp.pad(flat, (0, n_flat - flat.shape[0])).reshape(8, n_flat // (8 * LANES), LANES)
    red = _sum_parts(_exchange(flat, everyone, True, "rs_small"), "sum_small")
    red = _exchange(red, everyone, False, "ag_small_grads").reshape(-1)
    g_ln_g, g_ln_b, g_conv_w, g_d, g_bglu, g_a, g_bbr, g_bbi, g_cre, g_cim = _unflat(red, small_shapes)

    g_a = g_a.reshape(2, ngb, 2, sbk)
    dar = g_a[:, :, 0].reshape(rg, n_state)
    dai = g_a[:, :, 1].reshape(rg, n_state)
    dbbr_t = jnp.transpose(g_bbr, (2, 0, 1, 3)).reshape(grp, rg, n_state)
    dbbi_t = jnp.transpose(g_bbi, (2, 0, 1, 3)).reshape(grp, rg, n_state)
    z_lre, z_lim, z_ls, z_bre, z_bim = _zoh_bwd(lam_re2, lam_im2, log_step2, b_re_t, b_im_t, dar, dai, dbbr_t, dbbi_t)

    zero = jnp.zeros((1, d_model), F32)
    dm_rows = jnp.stack([
        jnp.stack([jnp.concatenate([dsh0, dsc0, dgt0], axis=1), jnp.concatenate([dshc0, dscc0, dgtc0], axis=1)]),
        jnp.stack([jnp.concatenate([dsh1, dsc1, dgt1], axis=1), jnp.concatenate([dshc1, dscc1, zero], axis=1)]),
    ]).reshape(DEPTH, 2, 3 * d_model)
    dm_all = _exchange(dm_rows, everyone, False, "ag_dmod")
    dm_sh = lax.dynamic_slice_in_dim(dm_all, chip * ws, ws, axis=3)
    g_ada_w, g_ada_b, ds_part = _ada_bwd(c_all, cc2, ada_w, dm_all, dm_sh)
    g_cctx = _cctx_grad(_exchange(ds_part, chips, False, "ag_dsctx"), cc2)

    grads["ada_w"] = g_ada_w
    dlt, nm, nv = _adamw(ada_w.reshape(-1, ws), g_ada_w.reshape(-1, ws), m_ada_w.reshape(-1, ws),
                         v_ada_w.reshape(-1, ws), "adamw_ada_w")
    deltas["ada_w"], new_m["ada_w"], new_v["ada_w"] = dlt.reshape(ada_w.shape), nm.reshape(ada_w.shape), nv.reshape(ada_w.shape)

    def chip_cols(full, rows):
        return lax.dynamic_slice_in_dim(full.reshape(rows, e), chip * es, es, axis=1)

    small_grads = dict(
        c_ctx=g_cctx.reshape(c_ctx.shape), ada_b=g_ada_b.reshape(ada_b.shape), ln_g=g_ln_g, ln_b=g_ln_b,
        conv_w=chip_cols(g_conv_w, 3).reshape(conv_w.shape),
        ssm_lam_re=z_lre.reshape(ssm_lam_re.shape), ssm_lam_im=z_lim.reshape(ssm_lam_im.shape),
        ssm_log_step=z_ls.reshape(ssm_log_step.shape),
        ssm_b_re=jnp.transpose(z_bre.reshape(grp, 2, n_grp, n_state), (1, 2, 3, 0)).reshape(ssm_b_re.shape),
        ssm_b_im=jnp.transpose(z_bim.reshape(grp, 2, n_grp, n_state), (1, 2, 3, 0)).reshape(ssm_b_im.shape),
        ssm_c_re=g_cre.reshape(ssm_c_re.shape), ssm_c_im=g_cim.reshape(ssm_c_im.shape),
        ssm_d=chip_cols(g_d, 1).reshape(ssm_d.shape), ssm_b_glu=chip_cols(g_bglu, 1).reshape(ssm_b_glu.shape))
    small_names = list(small_grads)
    shapes = [weights[n].shape for n in small_names]
    quantum = SUBLANES * LANES

    def pack(parts, fill):
        vec = _flat(parts)
        n_pad = -(-vec.shape[0] // quantum) * quantum
        return jnp.pad(vec, (0, n_pad - vec.shape[0]), constant_values=fill).reshape(-1, LANES)

    dlt, nm, nv = _adamw(pack([weights[n] for n in small_names], 0.0), pack([small_grads[n] for n in small_names], 0.0),
                         pack([mom_m[n] for n in small_names], 0.0), pack([mom_v[n] for n in small_names], 1.0),
                         "adamw_small")
    for n, dv, mv, vv in zip(small_names, _unflat(dlt.reshape(-1), shapes), _unflat(nm.reshape(-1), shapes),
                             _unflat(nv.reshape(-1), shapes)):
        grads[n], deltas[n], new_m[n], new_v[n] = small_grads[n], dv, mv, vv

    loss_total = lax.psum(loss[0, 0], MESH_AXES)
    return (loss_total, grad_x.reshape(x.shape), *[grads[n] for n in names], *[deltas[n] for n in names],
            *[new_m[n] for n in names], *[new_v[n] for n in names])
```

```python
import functools
import math

import jax
import jax.numpy as jnp
from jax import lax
from jax.experimental import pallas as pl
from jax.experimental.pallas import tpu as pltpu

F32 = jnp.float32
BF16 = jnp.bfloat16
LANES = 128
SUBLANES = 8
VMEM_LIMIT = 56 * 1024 * 1024
MESH_AXES = ("x", "y", "c")
HIGHEST = lax.Precision.HIGHEST

GRID_W = 64
LN_EPS = 1e-5
DEPTH = 2
DN_ALPHA = (2 * DEPTH) ** 0.25
ADAM_LR, ADAM_B1, ADAM_B2, ADAM_EPS, ADAM_WD, ADAM_STEP = 0.001, 0.9, 0.999, 1e-08, 0.01, 10
GELU_K = math.sqrt(2.0 / math.pi)
GELU_C = 0.044715


def _params(n_grid_axes):
    return pltpu.CompilerParams(dimension_semantics=("arbitrary",) * n_grid_axes, vmem_limit_bytes=VMEM_LIMIT)


def _dot(a, b):
    return jnp.dot(a, b, preferred_element_type=F32)


def _dot_nt(a, b):
    return lax.dot_general(a, b, (((1,), (1,)), ((), ())), preferred_element_type=F32)


def _dot_tn(a, b):
    return lax.dot_general(a, b, (((0,), (0,)), ((), ())), preferred_element_type=F32)


def _sigmoid(x):
    return 0.5 * jnp.tanh(0.5 * x) + 0.5


def _colsum(x):
    return jnp.sum(x, axis=0, keepdims=True)


def _rowmean(x):
    return jnp.mean(x, axis=-1, keepdims=True)


def _acc(ref, first, value):
    @pl.when(first)
    def _():
        ref[...] = value

    @pl.when(jnp.logical_not(first))
    def _():
        ref[...] += value


def _exchange_copies(src_ref, out_ref, send_sems, recv_sems, own_sem, axes, all_to_all, sem0=0):
    n_peers = 2 ** len(axes)
    pos = {a: lax.axis_index(a) for a in MESH_AXES}

    def index(p):
        return sum(p[a] * (2 ** (len(axes) - 1 - i)) for i, a in enumerate(axes))

    me = index(pos)
    own = pltpu.make_async_copy(src_ref.at[me] if all_to_all else src_ref, out_ref.at[me], own_sem)
    copies = []
    for k in range(1, n_peers):
        peer = dict(pos)
        for i, a in enumerate(axes):
            if (k >> (len(axes) - 1 - i)) & 1:
                peer[a] = 1 - pos[a]
        copies.append(pltpu.make_async_remote_copy(
            src_ref=src_ref.at[index(peer)] if all_to_all else src_ref,
            dst_ref=out_ref.at[me],
            send_sem=send_sems.at[sem0 + k - 1],
            recv_sem=recv_sems.at[sem0 + k - 1],
            device_id=tuple(peer[a] for a in MESH_AXES),
            device_id_type=pl.DeviceIdType.MESH,
        ))
    return copies, own


def _exchange_shape(src, axes, all_to_all):
    block = tuple(src.shape[1:] if all_to_all else src.shape)
    return jax.ShapeDtypeStruct((2 ** len(axes),) + block, src.dtype)


def _exchange(src, axes, all_to_all, name):
    n_peers = 2 ** len(axes)

    def body(src_ref, out_ref, send_sems, recv_sems, own_sem):
        copies, own = _exchange_copies(src_ref, out_ref, send_sems, recv_sems, own_sem, axes, all_to_all)
        own.start()
        for cp in copies:
            cp.start()
        for cp in copies:
            cp.wait()
        own.wait()

    return pl.pallas_call(
        body,
        name=name,
        out_shape=_exchange_shape(src, axes, all_to_all),
        in_specs=[pl.BlockSpec(memory_space=pltpu.HBM)],
        out_specs=pl.BlockSpec(memory_space=pltpu.HBM),
        scratch_shapes=[
            pltpu.SemaphoreType.DMA((n_peers - 1,)),
            pltpu.SemaphoreType.DMA((n_peers - 1,)),
            pltpu.SemaphoreType.DMA,
        ],
    )(src)


class _Hosted:
    def __init__(self, items):
        self.items = items
        self.args = [src for src, _, _ in items]
        self.in_specs = [pl.BlockSpec(memory_space=pltpu.HBM)] * len(items)
        self.out_specs = [pl.BlockSpec(memory_space=pltpu.HBM)] * len(items)
        self.out_shapes = [_exchange_shape(*item) for item in items]
        n_remote = sum(2 ** len(axes) - 1 for _, axes, _ in items)
        self.scratch = [pltpu.SemaphoreType.DMA((n_remote,)), pltpu.SemaphoreType.DMA((n_remote,)),
                        pltpu.SemaphoreType.DMA((len(items),))]

    def _copies(self, src_refs, out_refs, send_sems, recv_sems, own_sems):
        out, sem0 = [], 0
        for n, (_, axes, all_to_all) in enumerate(self.items):
            copies, own = _exchange_copies(src_refs[n], out_refs[n], send_sems, recv_sems, own_sems.at[n], axes,
                                           all_to_all, sem0)
            out += [own] + copies
            sem0 += len(copies)
        return out

    def start(self, *refs):
        for cp in self._copies(*refs):
            cp.start()

    def wait(self, *refs):
        for cp in self._copies(*refs):
            cp.wait()


def _call(body, *, name, grid, in_specs, out_specs, out_shape, scratch_shapes, args, hosted=None):
    params = _params(len(grid))
    if hosted is None:
        outs = pl.pallas_call(body, name=name, grid=grid, in_specs=in_specs, out_specs=tuple(out_specs),
                              out_shape=tuple(out_shape), scratch_shapes=list(scratch_shapes), compiler_params=params)(*args)
        return list(outs), []
    n_in, n_out, n_scr, n_h = len(in_specs), len(out_shape), len(scratch_shapes), len(hosted.items)

    def wrapped(*refs):
        ins, h_in = refs[:n_in], refs[n_in:n_in + n_h]
        outs, h_out = refs[n_in + n_h:n_in + n_h + n_out], refs[n_in + n_h + n_out:n_in + 2 * n_h + n_out]
        scr = refs[n_in + 2 * n_h + n_out:]
        first = functools.reduce(jnp.logical_and, [pl.program_id(k) == 0 for k in range(len(grid))])
        last = functools.reduce(jnp.logical_and, [pl.program_id(k) == grid[k] - 1 for k in range(len(grid))])

        @pl.when(first)
        def _():
            hosted.start(h_in, h_out, *scr[n_scr:])

        body(*ins, *outs, *scr[:n_scr])

        @pl.when(last)
        def _():
            hosted.wait(h_in, h_out, *scr[n_scr:])

    outs = pl.pallas_call(
        wrapped, name=name, grid=grid, in_specs=[*in_specs, *hosted.in_specs],
        out_specs=(*out_specs, *hosted.out_specs), out_shape=(*out_shape, *hosted.out_shapes),
        scratch_shapes=[*scratch_shapes, *hosted.scratch], compiler_params=params)(*args, *hosted.args)
    return list(outs[:n_out]), list(outs[n_out:])


def _sum_parts(parts, name):
    n_parts, rows, cols = parts.shape
    tr = rows
    while n_parts * tr * cols * 4 > 8 * 1024 * 1024 and tr % 16 == 0:
        tr //= 2

    def body(p_ref, o_ref):
        total = p_ref[0]
        for k in range(1, n_parts):
            total = total + p_ref[k]
        o_ref[...] = total

    return pl.pallas_call(
        body,
        name=name,
        grid=(rows // tr,),
        out_shape=jax.ShapeDtypeStruct((rows, cols), F32),
        in_specs=[pl.BlockSpec((n_parts, tr, cols), lambda i: (0, i, 0))],
        out_specs=pl.BlockSpec((tr, cols), lambda i: (i, 0)),
        compiler_params=_params(1),
    )(parts)


def _adamw(w, g, m, v, name):
    rows, cols = w.shape
    tr = rows
    while tr * cols * 4 > 2 * 1024 * 1024 and tr % 16 == 0:
        tr //= 2

    def body(w_ref, g_ref, m_ref, v_ref, d_ref, nm_ref, nv_ref):
        gv = g_ref[...]
        nm = ADAM_B1 * m_ref[...] + (1.0 - ADAM_B1) * gv
        nv = ADAM_B2 * v_ref[...] + (1.0 - ADAM_B2) * (gv * gv)
        m_hat = nm / (1.0 - ADAM_B1 ** ADAM_STEP)
        v_hat = nv / (1.0 - ADAM_B2 ** ADAM_STEP)
        d_ref[...] = -ADAM_LR * (m_hat / (jnp.sqrt(v_hat) + ADAM_EPS) + ADAM_WD * w_ref[...])
        nm_ref[...] = nm
        nv_ref[...] = nv

    spec = pl.BlockSpec((tr, cols), lambda i: (i, 0))
    shape = jax.ShapeDtypeStruct((rows, cols), F32)
    return pl.pallas_call(
        body, name=name, grid=(rows // tr,), out_shape=(shape, shape, shape),
        in_specs=[spec] * 4, out_specs=(spec, spec, spec), compiler_params=_params(1),
    )(w, g, m, v)


def _ada_rows(c_ref, cc_ref):
    rows = jnp.concatenate([c_ref[...], jnp.broadcast_to(cc_ref[...], c_ref.shape)], axis=0)
    return rows


def _ada_fwd(c_all, c_ctx, w_sh, b_sh):
    n_layers, _, ws = w_sh.shape

    def body(c_ref, cc_ref, w_ref, b_ref, o_ref):
        rows = _ada_rows(c_ref, cc_ref)
        s = rows * _sigmoid(rows)
        for i in range(n_layers):
            o_ref[i] = jnp.dot(s, w_ref[i], precision=HIGHEST, preferred_element_type=F32) + b_ref[i]

    return pl.pallas_call(
        body, name="ada_fwd", out_shape=jax.ShapeDtypeStruct((n_layers, 16, ws), F32),
        compiler_params=pltpu.CompilerParams(vmem_limit_bytes=VMEM_LIMIT),
    )(c_all, c_ctx, w_sh, b_sh)


def _ada_bwd(c_all, c_ctx, w_sh, dm_full, dm_sh):
    n_layers, d_model, ws = w_sh.shape
    n_dev = dm_full.shape[0]
    cols = dm_full.shape[-1]

    def body(c_ref, cc_ref, w_ref, dmf_ref, dms_ref, gw_ref, gb_ref, ds_ref):
        rows = _ada_rows(c_ref, cc_ref)
        s = rows * _sigmoid(rows)
        ds = jnp.zeros((8, d_model), F32)
        for i in range(n_layers):
            ctx_s = dms_ref[0, i, 1:2, :]
            ctx_f = dmf_ref[0, i, 1:2, :]
            ex_f = dmf_ref[0, i, 0:1, :]
            for k in range(1, n_dev):
                ctx_s = ctx_s + dms_ref[k, i, 1:2, :]
                ctx_f = ctx_f + dmf_ref[k, i, 1:2, :]
                ex_f = ex_f + dmf_ref[k, i, 0:1, :]
            gb_ref[i] = ex_f + ctx_f
            r = jnp.concatenate([dms_ref[k, i, 0:1, :] for k in range(n_dev)] + [ctx_s, jnp.zeros((7, ws), F32)], axis=0)
            gw_ref[i] = lax.dot_general(s, r, (((0,), (0,)), ((), ())), precision=HIGHEST, preferred_element_type=F32)
            ds = ds + lax.dot_general(jnp.broadcast_to(ctx_s, (8, ws)), w_ref[i], (((1,), (1,)), ((), ())),
                                      precision=HIGHEST, preferred_element_type=F32)
        ds_ref[...] = ds

    return pl.pallas_call(
        body, name="ada_bwd",
        out_shape=(jax.ShapeDtypeStruct((n_layers, d_model, ws), F32), jax.ShapeDtypeStruct((n_layers, 1, cols), F32),
                   jax.ShapeDtypeStruct((8, d_model), F32)),
        compiler_params=pltpu.CompilerParams(vmem_limit_bytes=VMEM_LIMIT),
    )(c_all, c_ctx, w_sh, dm_full, dm_sh)


def _cctx_grad(ds_parts, c_ctx):
    def body(p_ref, c_ref, o_ref):
        tot = p_ref[0, 0:1, :]
        for k in range(1, ds_parts.shape[0]):
            tot = tot + p_ref[k, 0:1, :]
        cv = c_ref[...]
        sg = _sigmoid(cv)
        o_ref[...] = tot * (sg * (1.0 + cv * (1.0 - sg)))

    return pl.pallas_call(body, name="cctx_grad", out_shape=jax.ShapeDtypeStruct(c_ctx.shape, F32))(ds_parts, c_ctx)


def _zoh_math(lam_re, lam_im, log_step, b_re, b_im):
    dt = jnp.exp(log_step)
    mag = jnp.exp(lam_re * dt)
    ar = mag * jnp.cos(lam_im * dt)
    ai = mag * jnp.sin(lam_im * dt)
    qr, qi = ar - 1.0, ai
    den = lam_re * lam_re + lam_im * lam_im
    fr = (qr * lam_re + qi * lam_im) / den
    fi = (qi * lam_re - qr * lam_im) / den
    bbr = fr[None] * b_re - fi[None] * b_im
    bbi = fr[None] * b_im + fi[None] * b_re
    return ar, ai, bbr, bbi


def _zoh_fwd(lam_re, lam_im, log_step, b_re, b_im):
    rg, n = lam_re.shape

    def body(lr_ref, li_ref, ls_ref, br_ref, bi_ref, pr_ref, pi_ref, qr_ref, qi_ref, bbr_ref, bbi_ref):
        ar, ai, bbr, bbi = _zoh_math(lr_ref[...], li_ref[...], ls_ref[...], br_ref[...], bi_ref[...])
        bbr_ref[...] = bbr
        bbi_ref[...] = bbi

        def powers(base_r, base_i, r_ref, i_ref):
            pr, pi_ = base_r, base_i
            for k in range(8):
                r_ref[k] = pr
                i_ref[k] = pi_
                pr, pi_ = pr * base_r - pi_ * base_i, pr * base_i + pi_ * base_r

        powers(ar, ai, pr_ref, pi_ref)
        powers(pr_ref[7], pi_ref[7], qr_ref, qi_ref)

    pw = jax.ShapeDtypeStruct((8, rg, n), F32)
    bb = jax.ShapeDtypeStruct(b_re.shape, F32)
    return pl.pallas_call(body, name="zoh_fwd", out_shape=(pw, pw, pw, pw, bb, bb))(lam_re, lam_im, log_step, b_re, b_im)


def _zoh_bwd(lam_re, lam_im, log_step, b_re, b_im, dar, dai, dbbr, dbbi):
    def body(lr_ref, li_ref, ls_ref, br_ref, bi_ref, dar_ref, dai_ref, dbr_ref, dbi_ref, *outs):
        _, vjp = jax.vjp(_zoh_math, lr_ref[...], li_ref[...], ls_ref[...], br_ref[...], bi_ref[...])
        grads = vjp((dar_ref[...], dai_ref[...], dbr_ref[...], dbi_ref[...]))
        for o_ref, gval in zip(outs, grads):
            o_ref[...] = gval

    shapes = tuple(jax.ShapeDtypeStruct(a.shape, F32) for a in (lam_re, lam_im, log_step, b_re, b_im))
    return pl.pallas_call(body, name="zoh_bwd", out_shape=shapes)(lam_re, lam_im, log_step, b_re, b_im, dar, dai, dbbr, dbbi)


def _inproj(xin, sc, sh, w, *, lnaff=None, tb, gb_rows=None, name, hosted=None):
    n_tok, d_model = xin.shape
    n_chunks, _, cw = w.shape
    tb = min(tb, n_tok)
    nq = cw // LANES
    has_ln = lnaff is not None
    n_out = 1 if gb_rows is None else len(gb_rows)

    def body(*refs):
        if has_ln:
            x_ref, g_ref, b_ref, sc_ref, sh_ref, w_ref = refs[:6]
        else:
            x_ref, sc_ref, sh_ref, w_ref = refs[:4]
        p_refs, h_ref = refs[-1 - n_out:-1], refs[-1]

        @pl.when(pl.program_id(1) == 0)
        def _():
            xv = x_ref[...]
            if has_ln:
                xv = xv * g_ref[...] + b_ref[...]
            h_ref[...] = (xv * (1.0 + sc_ref[...]) + sh_ref[...]).astype(BF16)

        acc = _dot(h_ref[...], w_ref[0]).astype(BF16)
        if gb_rows is None:
            p_refs[0][0] = acc
        else:
            for p_ref in p_refs:
                for q in range(nq):
                    p_ref[q] = acc[:, q * LANES:(q + 1) * LANES]

    vec = pl.BlockSpec((1, d_model), lambda i, j: (0, 0))
    in_specs = [pl.BlockSpec((tb, d_model), lambda i, j: (i, 0))] + ([vec, vec] if has_ln else []) + [
        vec, vec, pl.BlockSpec((1, d_model, cw), lambda i, j: (j, 0, 0))]
    if gb_rows is None:
        p_shapes = [jax.ShapeDtypeStruct((n_chunks, n_tok, cw), BF16)]
        p_specs = [pl.BlockSpec((1, tb, cw), lambda i, j: (j, i, 0))]
    else:
        p_shapes, p_specs = [], []
        for total, off in gb_rows:
            assert off % tb == 0
            p_shapes.append(jax.ShapeDtypeStruct((n_chunks * nq, total, LANES), BF16))
            p_specs.append(pl.BlockSpec((nq, tb, LANES), functools.partial(lambda i, j, ob: (j, i + ob, 0), ob=off // tb)))
    args = (xin,) + (tuple(lnaff) if has_ln else ()) + (sc, sh, w)
    outs, extra = _call(
        body, name=name, grid=(n_tok // tb, n_chunks), in_specs=in_specs,
        out_specs=[*p_specs, pl.BlockSpec((tb, d_model), lambda i, j: (i, 0))],
        out_shape=[*p_shapes, jax.ShapeDtypeStruct((n_tok, d_model), BF16)], scratch_shapes=[], args=args, hosted=hosted)
    return (*outs, extra) if hosted is not None else tuple(outs)


def _shifted(u, rows, width, tb):
    col = rows % width
    um = jnp.where(col == 0, 0.0, pltpu.roll(u, 1, 0))
    up = jnp.where(col == width - 1, 0.0, pltpu.roll(u, tb - 1, 0))
    return um, up


def _slab_width(e):
    return min(512, e // 2)


def _convgate(p, x, gt, conv_w, w_out, ln_g, ln_b, *, grid_mode, tb, name):
    _, n_tok, e = p.shape
    d_model = x.shape[1]
    eh = e // 2
    if not grid_mode:
        tb = n_tok
    tb = min(tb, n_tok)
    nb = n_tok // tb
    hb = tb // GRID_W
    cs = _slab_width(e)

    def body(*refs):
        if grid_mode:
            (bg_ref, cg_ref, v_ref, z_ref, cgp_ref, vp_ref, cgn_ref, vn_ref, x_ref, gt_ref, cw_ref, wo_ref, lg_ref,
             lb_ref, xh_ref, rs_ref, g_ref, yc_ref, f_ref) = refs
        else:
            (bg_ref, cg_ref, v_ref, z_ref, x_ref, gt_ref, cw_ref, wo_ref, lg_ref, lb_ref, xh_ref, rs_ref, g_ref,
             yc_ref, f_ref) = refs
        i = pl.program_id(0)
        rows = lax.broadcasted_iota(jnp.int32, (tb, 1), 0)
        for c0 in range(0, e, cs):
            sl = slice(c0, c0 + cs)
            u = cg_ref[0, :, sl].astype(F32) * v_ref[0, :, sl].astype(F32)
            w = cw_ref[:, sl]
            if grid_mode and c0 >= eh:
                hs = slice(c0 - eh, c0 - eh + cs)
                uprev = cgp_ref[0, :, hs].astype(F32) * vp_ref[0, :, hs].astype(F32)
                unext = cgn_ref[0, :, hs].astype(F32) * vn_ref[0, :, hs].astype(F32)
                uprev = jnp.where(i > 0, uprev, 0.0)
                unext = jnp.where(i < nb - 1, unext, 0.0)
                if tb > GRID_W:
                    um = jnp.concatenate([uprev, u[:tb - GRID_W]], axis=0)
                    up = jnp.concatenate([u[GRID_W:], unext], axis=0)
                else:
                    um, up = uprev, unext
            else:
                um, up = _shifted(u, rows, GRID_W if grid_mode else tb, tb)
            yc = um * w[0:1] + u * w[1:2] + up * w[2:3]
            zf = z_ref[0, :, sl].astype(F32)
            gval = bg_ref[0, :, sl].astype(F32) * yc * (zf * _sigmoid(zf))
            yc_ref[:, sl] = yc.astype(BF16)
            g_ref[:, sl] = gval.astype(BF16)
        f = _dot(g_ref[...], wo_ref[...])
        f_ref[...] = f.astype(BF16)
        r = DN_ALPHA * x_ref[...] + gt_ref[...] * f
        rc = r - _rowmean(r)
        rstd = lax.rsqrt(_rowmean(rc * rc) + LN_EPS)
        xh_ref[...] = rc * rstd
        rs_ref[...] = rstd

    def chunk(k):
        return pl.BlockSpec((1, tb, e), lambda i: (k, i, 0))

    n_hrows = n_tok // GRID_W

    def halo_prev(k):
        return pl.BlockSpec((1, GRID_W, eh), lambda i: (k, jnp.maximum(i * hb - 1, 0), 1))

    def halo_next(k):
        return pl.BlockSpec((1, GRID_W, eh), lambda i: (k, jnp.minimum((i + 1) * hb, n_hrows - 1), 1))

    vec = pl.BlockSpec((1, d_model), lambda i: (0, 0))
    tok = pl.BlockSpec((tb, d_model), lambda i: (i, 0))
    wide = pl.BlockSpec((tb, e), lambda i: (i, 0))
    in_specs = [chunk(0), chunk(1), chunk(2), chunk(3)]
    args = [p, p, p, p]
    if grid_mode:
        in_specs += [halo_prev(1), halo_prev(2), halo_next(1), halo_next(2)]
        args += [p, p, p, p]
    in_specs += [tok, vec, pl.BlockSpec((3, e), lambda i: (0, 0)), pl.BlockSpec((e, d_model), lambda i: (0, 0)), vec, vec]
    args += [x, gt, conv_w, w_out, ln_g, ln_b]
    return pl.pallas_call(
        body, name=name, grid=(nb,),
        out_shape=(jax.ShapeDtypeStruct((n_tok, d_model), F32), jax.ShapeDtypeStruct((n_tok, 1), F32),
                   jax.ShapeDtypeStruct((n_tok, e), BF16), jax.ShapeDtypeStruct((n_tok, e), BF16),
                   jax.ShapeDtypeStruct((n_tok, d_model), BF16)),
        in_specs=in_specs, out_specs=(tok, pl.BlockSpec((tb, 1), lambda i: (i, 0)), wide, wide, tok),
        compiler_params=_params(1),
    )(*args)


def _scan_block(buf_ref, tab_ref, cr, ci, *, reverse, tb, sb):
    n_slabs = tb // SUBLANES

    def slab(s, carry):
        cr, ci = carry
        idx = (n_slabs - 1 - s) if reverse else s
        r0 = pl.multiple_of(idx * SUBLANES, SUBLANES)
        xr = buf_ref[pl.ds(r0, SUBLANES), 0:sb]
        xi = buf_ref[pl.ds(r0, SUBLANES), sb:2 * sb]
        for k, step in enumerate((1, 2, 4)):
            ar = tab_ref[2 * k]
            ai = tab_ref[2 * k + 1]
            shift = (SUBLANES - step) if reverse else step
            rr = pltpu.roll(xr, shift, 0)
            ri = pltpu.roll(xi, shift, 0)
            xr, xi = xr + ar * rr - ai * ri, xi + ar * ri + ai * rr
        pr = tab_ref[6]
        pi_ = tab_ref[7]
        xr, xi = xr + pr * cr - pi_ * ci, xi + pr * ci + pi_ * cr
        buf_ref[pl.ds(r0, SUBLANES), 0:sb] = xr
        buf_ref[pl.ds(r0, SUBLANES), sb:2 * sb] = xi
        last = 0 if reverse else SUBLANES - 1
        return (jnp.broadcast_to(xr[last:last + 1, :], (SUBLANES, sb)),
                jnp.broadcast_to(xi[last:last + 1, :], (SUBLANES, sb)))

    return lax.fori_loop(0, n_slabs, slab, (cr, ci))


CHUNK = SUBLANES


def _group_mask():
    r = lax.broadcasted_iota(jnp.int32, (LANES, LANES), 0)
    c = lax.broadcasted_iota(jnp.int32, (LANES, LANES), 1)
    return r // 16 == c // 16


def _s5_ops(bre, bim, cre, cim, wr, wi, *, reverse, name):
    ngb, _, sb = bre.shape
    n_rows = CHUNK * LANES

    def body(bre_ref, bim_ref, cre_ref, cim_ref, wr_ref, wi_ref, t_ref, bp_ref, cp_ref):
        b_re, b_im, c_re, c_im = bre_ref[0], bim_ref[0], cre_ref[0], cim_ref[0]
        mask = _group_mask()
        er, ei = [], []
        for tau in range(CHUNK + 1):
            w_r, w_i = wr_ref[0, tau:tau + 1, :], wi_ref[0, tau:tau + 1, :]
            er.append(c_re * w_r - c_im * w_i)
            ei.append(c_re * w_i + c_im * w_r)
        kt = []
        for tau in range(CHUNK):
            k = (lax.dot_general(b_re, er[tau], (((1,), (1,)), ((), ())), precision=HIGHEST, preferred_element_type=F32)
                 - lax.dot_general(b_im, ei[tau], (((1,), (1,)), ((), ())), precision=HIGHEST, preferred_element_type=F32))
            kt.append(jnp.where(mask, k, 0.0).astype(BF16))
        zero = jnp.zeros((LANES, LANES), BF16)
        for i in range(CHUNK):
            rows = slice(i * LANES, (i + 1) * LANES)
            for j in range(CHUNK):
                lag = (i - j) if reverse else (j - i)
                t_ref[0, rows, j * LANES:(j + 1) * LANES] = kt[lag] if lag >= 0 else zero
            tau = i if reverse else CHUNK - 1 - i
            w_r, w_i = wr_ref[0, tau:tau + 1, :], wi_ref[0, tau:tau + 1, :]
            bp_ref[0, rows, 0:sb] = (b_re * w_r - b_im * w_i).astype(BF16)
            bp_ref[0, rows, sb:2 * sb] = (b_re * w_i + b_im * w_r).astype(BF16)
            tau = CHUNK - i if reverse else i + 1
            cp_ref[0, rows, 0:sb] = er[tau].astype(BF16)
            cp_ref[0, rows, sb:2 * sb] = (-ei[tau]).astype(BF16)

    mat = pl.BlockSpec((1, LANES, sb), lambda g: (g, 0, 0))
    pw = pl.BlockSpec((1, CHUNK + 1, sb), lambda g: (g, 0, 0))
    return pl.pallas_call(
        body, name=name, grid=(ngb,),
        out_shape=(jax.ShapeDtypeStruct((ngb, n_rows, n_rows), BF16), jax.ShapeDtypeStruct((ngb, n_rows, 2 * sb), BF16),
                   jax.ShapeDtypeStruct((ngb, n_rows, 2 * sb), BF16)),
        in_specs=[mat, mat, mat, mat, pw, pw],
        out_specs=(pl.BlockSpec((1, n_rows, n_rows), lambda g: (g, 0, 0)),
                   pl.BlockSpec((1, n_rows, 2 * sb), lambda g: (g, 0, 0)),
                   pl.BlockSpec((1, n_rows, 2 * sb), lambda g: (g, 0, 0))),
        compiler_params=_params(1),
    )(bre, bim, cre, cim, wr, wi)


def _s5_ops_bwd(bre, bim, cre, cim, wr, wi, ar, ai, dt, dbp, dcp, da8, *, reverse, name):
    ngb, _, sb = bre.shape
    n_rows = CHUNK * LANES

    def dot_hi(a, b, dims):
        return lax.dot_general(a, b, (dims, ((), ())), precision=HIGHEST, preferred_element_type=F32)

    def body(bre_ref, bim_ref, cre_ref, cim_ref, wr_ref, wi_ref, ar_ref, ai_ref, dt_ref, dbp_ref, dcp_ref, da8_ref,
             dbre_ref, dbim_ref, dcre_ref, dcim_ref, da_ref):
        b_re, b_im, c_re, c_im = bre_ref[0], bim_ref[0], cre_ref[0], cim_ref[0]
        mask = _group_mask()
        w_r = [wr_ref[0, tau:tau + 1, :] for tau in range(CHUNK + 1)]
        w_i = [wi_ref[0, tau:tau + 1, :] for tau in range(CHUNK + 1)]
        der = [jnp.zeros((LANES, sb), F32) for _ in range(CHUNK + 1)]
        dei = [jnp.zeros((LANES, sb), F32) for _ in range(CHUNK + 1)]
        dwr = [jnp.zeros((1, sb), F32) for _ in range(CHUNK + 1)]
        dwi = [jnp.zeros((1, sb), F32) for _ in range(CHUNK + 1)]
        dwr[CHUNK] = da8_ref[0, :, 0:sb]
        dwi[CHUNK] = da8_ref[0, :, sb:2 * sb]
        d_bre = jnp.zeros((LANES, sb), F32)
        d_bim = jnp.zeros((LANES, sb), F32)
        dkt = [jnp.zeros((LANES, LANES), F32) for _ in range(CHUNK)]
        for i in range(CHUNK):
            rows = slice(i * LANES, (i + 1) * LANES)
            for j in range(CHUNK):
                lag = (i - j) if reverse else (j - i)
                if lag >= 0:
                    dkt[lag] = dkt[lag] + dt_ref[0, rows, j * LANES:(j + 1) * LANES]
            tau = i if reverse else CHUNK - 1 - i
            g_r, g_i = dbp_ref[0, rows, 0:sb], dbp_ref[0, rows, sb:2 * sb]
            d_bre = d_bre + g_r * w_r[tau] + g_i * w_i[tau]
            d_bim = d_bim - g_r * w_i[tau] + g_i * w_r[tau]
            dwr[tau] = dwr[tau] + _colsum(g_r * b_re + g_i * b_im)
            dwi[tau] = dwi[tau] + _colsum(g_i * b_re - g_r * b_im)
            tau = CHUNK - i if reverse else i + 1
            der[tau] = der[tau] + dcp_ref[0, rows, 0:sb]
            dei[tau] = dei[tau] - dcp_ref[0, rows, sb:2 * sb]
        d_cre = jnp.zeros((LANES, sb), F32)
        d_cim = jnp.zeros((LANES, sb), F32)
        for tau in range(CHUNK + 1):
            if tau < CHUNK:
                e_r = c_re * w_r[tau] - c_im * w_i[tau]
                e_i = c_re * w_i[tau] + c_im * w_r[tau]
                dk = jnp.where(mask, dkt[tau], 0.0)
                d_bre = d_bre + dot_hi(dk, e_r, ((1,), (0,)))
                d_bim = d_bim - dot_hi(dk, e_i, ((1,), (0,)))
                der[tau] = der[tau] + dot_hi(dk, b_re, ((0,), (0,)))
                dei[tau] = dei[tau] - dot_hi(dk, b_im, ((0,), (0,)))
            d_cre = d_cre + der[tau] * w_r[tau] + dei[tau] * w_i[tau]
            d_cim = d_cim - der[tau] * w_i[tau] + dei[tau] * w_r[tau]
            dwr[tau] = dwr[tau] + _colsum(der[tau] * c_re + dei[tau] * c_im)
            dwi[tau] = dwi[tau] + _colsum(dei[tau] * c_re - der[tau] * c_im)
        a_r, a_i = ar_ref[0], ai_ref[0]
        d_ar = jnp.zeros((1, sb), F32)
        d_ai = jnp.zeros((1, sb), F32)
        for tau in range(CHUNK, 0, -1):
            d_ar = d_ar + dwr[tau] * w_r[tau - 1] + dwi[tau] * w_i[tau - 1]
            d_ai = d_ai - dwr[tau] * w_i[tau - 1] + dwi[tau] * w_r[tau - 1]
            dwr[tau - 1], dwi[tau - 1] = (dwr[tau - 1] + dwr[tau] * a_r + dwi[tau] * a_i,
                                          dwi[tau - 1] - dwr[tau] * a_i + dwi[tau] * a_r)
        dbre_ref[0] = d_bre
        dbim_ref[0] = d_bim
        dcre_ref[0] = d_cre
        dcim_ref[0] = d_cim
        da_ref[0, :, 0:sb] = d_ar
        da_ref[0, :, sb:2 * sb] = d_ai

    mat = pl.BlockSpec((1, LANES, sb), lambda g: (g, 0, 0))
    pw = pl.BlockSpec((1, CHUNK + 1, sb), lambda g: (g, 0, 0))
    one = pl.BlockSpec((1, 1, sb), lambda g: (g, 0, 0))
    two = pl.BlockSpec((1, 1, 2 * sb), lambda g: (g, 0, 0))
    big = pl.BlockSpec((1, n_rows, n_rows), lambda g: (g, 0, 0))
    big2 = pl.BlockSpec((1, n_rows, 2 * sb), lambda g: (g, 0, 0))
    mshape = jax.ShapeDtypeStruct((ngb, LANES, sb), F32)
    return pl.pallas_call(
        body, name=name, grid=(ngb,),
        out_shape=(mshape, mshape, mshape, mshape, jax.ShapeDtypeStruct((ngb, 1, 2 * sb), F32)),
        in_specs=[mat, mat, mat, mat, pw, pw, one, one, big, big2, big2, two],
        out_specs=(mat, mat, mat, mat, two),
        compiler_params=_params(1),
    )(bre, bim, cre, cim, wr, wi, ar, ai, dt, dbp, dcp, da8)


def _shift_rows(xv, edge, rows, n_rows, down):
    if down:
        return jnp.where(rows == 0, edge, pltpu.roll(xv, 1, 0))
    return jnp.where(rows == n_rows - 1, edge, pltpu.roll(xv, n_rows - 1, 0))


def _rows_of_tokens(tok_ref, conv_scr, rb):
    conv_scr[...] = tok_ref[0].astype(F32)
    return jnp.concatenate([conv_scr[pl.ds(j, rb, stride=CHUNK), :] for j in range(CHUNK)], axis=1).astype(BF16)


def _tokens_of_rows(val, tok_ref, conv_scr, rb):
    for j in range(CHUNK):
        conv_scr[pl.ds(j, rb, stride=CHUNK), :] = val[:, j * LANES:(j + 1) * LANES]
    tok_ref[0] = conv_scr[...].astype(BF16)


def _s5_row_block(n_seq, target=416):
    n_rows = n_seq // CHUNK
    best = 16
    for rb in range(16, min(target, n_rows) + 1, 16):
        if n_rows % rb == 0:
            best = rb
    assert n_rows % best == 0
    return best


def _s5_fwd(useq, t_op, bp, cp, tab, *, reverse, name):
    ngb, n_seq, _ = useq.shape
    sb = bp.shape[2] // 2
    width = CHUNK * LANES
    rb = _s5_row_block(n_seq)
    tbk = rb * CHUNK
    steps = n_seq // tbk

    def blk(i):
        return (steps - 1 - i) if reverse else i

    def body(u_ref, t_ref, b_ref, c_ref, tab_ref, y_ref, hp_ref, h_scr, conv_scr, carry_scr):
        i = pl.program_id(1)

        @pl.when(i == 0)
        def _():
            carry_scr[...] = jnp.zeros_like(carry_scr)

        enter = carry_scr[0:1, :]
        uv = _rows_of_tokens(u_ref, conv_scr, rb)
        h_scr[...] = _dot(uv, b_ref[0])
        cr, ci = _scan_block(h_scr, tab_ref.at[0], carry_scr[:, 0:sb], carry_scr[:, sb:2 * sb],
                             reverse=reverse, tb=rb, sb=sb)
        carry_scr[:, 0:sb] = cr
        carry_scr[:, sb:2 * sb] = ci
        rows = lax.broadcasted_iota(jnp.int32, (rb, 1), 0)
        hprev = _shift_rows(h_scr[...], enter, rows, rb, down=not reverse)
        hp_ref[0] = hprev
        _tokens_of_rows(_dot(uv, t_ref[0]) + _dot_nt(hprev.astype(BF16), c_ref[0]), y_ref, conv_scr, rb)

    op = pl.BlockSpec((1, width, width), lambda g, i: (g, 0, 0))
    op2 = pl.BlockSpec((1, width, 2 * sb), lambda g, i: (g, 0, 0))
    tok = pl.BlockSpec((1, tbk, LANES), lambda g, i: (g, blk(i), 0))
    return pl.pallas_call(
        body, name=name, grid=(ngb, steps),
        out_shape=(jax.ShapeDtypeStruct((ngb, n_seq, LANES), BF16),
                   jax.ShapeDtypeStruct((ngb, n_seq // CHUNK, 2 * sb), F32)),
        in_specs=[tok, op, op2, op2, pl.BlockSpec((1, 8, SUBLANES, sb), lambda g, i: (g, 0, 0, 0))],
        out_specs=(tok, pl.BlockSpec((1, rb, 2 * sb), lambda g, i: (g, blk(i), 0))),
        scratch_shapes=[pltpu.VMEM((rb, 2 * sb), F32), pltpu.VMEM((tbk, LANES), F32),
                        pltpu.VMEM((SUBLANES, 2 * sb), F32)],
        compiler_params=_params(2),
    )(useq, t_op, bp, cp, tab)


def _s5_bwd(useq, dy, hprev, t_op, bp, cp, tab_adj, *, reverse, name, hosted=None):
    ngb, n_seq, _ = useq.shape
    sb = bp.shape[2] // 2
    width = CHUNK * LANES
    rb = _s5_row_block(n_seq)
    tbk = rb * CHUNK
    steps = n_seq // tbk

    def fwd_step(i):
        return steps - 1 - i

    def blk(i):
        s = fwd_step(i)
        return (steps - 1 - s) if reverse else s

    def body(u_ref, dy_ref, hp_ref, t_ref, b_ref, c_ref, taba_ref, du_ref, dt_ref, db_ref, dc_ref, da_ref,
             lam_scr, conv_scr, lcarry_scr, gedge_scr, da_scr):
        i = pl.program_id(1)
        first = i == 0

        @pl.when(first)
        def _():
            lcarry_scr[...] = jnp.zeros_like(lcarry_scr)
            gedge_scr[...] = jnp.zeros_like(gedge_scr)
            da_scr[...] = jnp.zeros_like(da_scr)

        rows = lax.broadcasted_iota(jnp.int32, (rb, 1), 0)
        uv = _rows_of_tokens(u_ref, conv_scr, rb)
        dyv = _rows_of_tokens(dy_ref, conv_scr, rb)
        gy = _dot(dyv, c_ref[0])
        edge = gy[rb - 1:rb, :] if reverse else gy[0:1, :]
        lam_scr[...] = _shift_rows(gy, gedge_scr[...], rows, rb, down=reverse)
        gedge_scr[...] = edge
        lr, li = _scan_block(lam_scr, taba_ref.at[0], lcarry_scr[:, 0:sb], lcarry_scr[:, sb:2 * sb],
                             reverse=not reverse, tb=rb, sb=sb)
        lcarry_scr[:, 0:sb] = lr
        lcarry_scr[:, sb:2 * sb] = li

        lam = lam_scr[...]
        lam_bf = lam.astype(BF16)
        _tokens_of_rows(_dot_nt(dyv, t_ref[0]) + _dot_nt(lam_bf, b_ref[0]), du_ref, conv_scr, rb)
        _acc(dt_ref.at[0], first, _dot_tn(uv, dyv))
        _acc(db_ref.at[0], first, _dot_tn(uv, lam_bf))
        _acc(dc_ref.at[0], first, _dot_tn(dyv, hp_ref[0].astype(BF16)))
        lam_r, lam_i = lam[:, 0:sb], lam[:, sb:2 * sb]
        hp_r, hp_i = hp_ref[0, :, 0:sb], hp_ref[0, :, sb:2 * sb]
        da_scr[:, 0:sb] += _colsum(lam_r * hp_r + lam_i * hp_i)
        da_scr[:, sb:2 * sb] += _colsum(lam_i * hp_r - lam_r * hp_i)

        @pl.when(i == steps - 1)
        def _():
            da_ref[0] = da_scr[...]

    op = pl.BlockSpec((1, width, width), lambda g, i: (g, 0, 0))
    op2 = pl.BlockSpec((1, width, 2 * sb), lambda g, i: (g, 0, 0))
    tabs = pl.BlockSpec((1, 8, SUBLANES, sb), lambda g, i: (g, 0, 0, 0))
    tok = pl.BlockSpec((1, tbk, LANES), lambda g, i: (g, blk(i), 0))
    outs, extra = _call(
        body, name=name, grid=(ngb, steps),
        out_shape=[jax.ShapeDtypeStruct((ngb, n_seq, LANES), BF16),
                   jax.ShapeDtypeStruct((ngb, width, width), F32),
                   jax.ShapeDtypeStruct((ngb, width, 2 * sb), F32),
                   jax.ShapeDtypeStruct((ngb, width, 2 * sb), F32),
                   jax.ShapeDtypeStruct((ngb, 1, 2 * sb), F32)],
        in_specs=[tok, tok, pl.BlockSpec((1, rb, 2 * sb), lambda g, i: (g, blk(i), 0)), op, op2, op2, tabs],
        out_specs=[tok, op, op2, op2, pl.BlockSpec((1, 1, 2 * sb), lambda g, i: (g, 0, 0))],
        scratch_shapes=[pltpu.VMEM((rb, 2 * sb), F32), pltpu.VMEM((tbk, LANES), F32),
                        pltpu.VMEM((SUBLANES, 2 * sb), F32), pltpu.VMEM((1, 2 * sb), F32), pltpu.VMEM((1, 2 * sb), F32)],
        args=(useq, dy, hprev, t_op, bp, cp, tab_adj), hosted=hosted)
    return (*outs, extra)


def _glu_loss(useq, yf, yb, z, xhat0, ln0, gt, d_vec, w_glu, b_glu, w_out, ln1, target, *, offs, dy_rows, tb, name):
    ngb = useq.shape[0]
    n_tok, d_model = xhat0.shape
    e = ngb * LANES
    tb = min(tb, n_tok)
    assert all(off % tb == 0 for off in offs) and all(off % tb == 0 for _, off in dy_rows)
    nz = z.shape[0]

    def body(u_ref, yf_ref, yb_ref, z_ref, xh0_ref, g0_ref, b0_ref, gt_ref, d_ref, wg_ref, bg_ref, wo_ref, g1_ref,
             b1_ref, t_ref, loss_ref, dxr_ref, do_ref, gz_ref, gg_ref, dq_ref, dz_ref, dyf_ref, dyb_ref, dg1_ref, db1_ref,
             dgt_ref, dbg_ref, dd_ref, loss_scr, yl_scr, th_scr, s_scr, dg_scr):
        i = pl.program_id(0)
        first = i == 0
        zw = e // nz
        cs = min(512, zw)

        def z_slab(c0):
            return z_ref[c0 // zw, :, c0 % zw:c0 % zw + cs].astype(F32)

        for q in range(ngb):
            sl = slice(q * LANES, (q + 1) * LANES)
            yl = d_ref[:, sl] * u_ref[q].astype(F32) + yf_ref[q].astype(F32) + yb_ref[q].astype(F32)
            th = jnp.tanh(GELU_K * (yl + GELU_C * yl * yl * yl))
            yl_scr[:, sl] = yl
            th_scr[:, sl] = th
            gg_ref[:, sl] = (0.5 * yl * (1.0 + th)).astype(BF16)
        s_scr[...] = _sigmoid(_dot(gg_ref[...], wg_ref[...]) + bg_ref[...])
        for c0 in range(0, e, cs):
            sl = slice(c0, c0 + cs)
            zf = z_slab(c0)
            g2 = 0.5 * yl_scr[:, sl] * (1.0 + th_scr[:, sl]) * s_scr[:, sl]
            gz_ref[:, sl] = (g2 * (zf * _sigmoid(zf))).astype(BF16)
        o = _dot(gz_ref[...], wo_ref[...])
        x1 = xh0_ref[...] * g0_ref[...] + b0_ref[...]
        r = DN_ALPHA * x1 + gt_ref[...] * o
        rc = r - _rowmean(r)
        rstd = lax.rsqrt(_rowmean(rc * rc) + LN_EPS)
        xh = rc * rstd
        err = xh * g1_ref[...] + b1_ref[...] - t_ref[...]
        _acc(loss_scr, first, _colsum(err * err))
        dy = err * (1.0 / d_model)
        _acc(dg1_ref, first, _colsum(dy * xh))
        _acc(db1_ref, first, _colsum(dy))
        dxh = dy * g1_ref[...]
        dr = rstd * (dxh - _rowmean(dxh) - xh * _rowmean(dxh * xh))
        dxr_ref[...] = DN_ALPHA * dr
        _acc(dgt_ref, first, _colsum(dr * o))
        do_bf = (dr * gt_ref[...]).astype(BF16)
        do_ref[...] = do_bf
        dg_scr[...] = _dot_nt(do_bf, wo_ref[...])
        for c0 in range(0, e, cs):
            sl = slice(c0, c0 + cs)
            zf = z_slab(c0)
            sz = _sigmoid(zf)
            g = 0.5 * yl_scr[:, sl] * (1.0 + th_scr[:, sl])
            s = s_scr[:, sl]
            dgz = dg_scr[:, sl]
            dg2 = dgz * (zf * sz)
            dz_ref[:, sl] = (dgz * (g * s) * (sz * (1.0 + zf * (1.0 - sz)))).astype(BF16)
            dq = dg2 * g * s * (1.0 - s)
            _acc(dbg_ref.at[:, sl], first, _colsum(dq))
            dq_ref[:, sl] = dq.astype(BF16)
            dg_scr[:, sl] = dg2 * s
        dg_scr[...] += _dot_nt(dq_ref[...], wg_ref[...])
        for q in range(ngb):
            sl = slice(q * LANES, (q + 1) * LANES)
            yl = yl_scr[:, sl]
            th = th_scr[:, sl]
            dgelu = 0.5 * (1.0 + th) + 0.5 * yl * (1.0 - th * th) * (GELU_K * (1.0 + 3.0 * GELU_C * yl * yl))
            dyl = dg_scr[:, sl] * dgelu
            _acc(dd_ref.at[:, sl], first, _colsum(dyl * u_ref[q].astype(F32)))
            dyf_ref[q] = dyl.astype(BF16)
            dyb_ref[q] = dyl.astype(BF16)

        @pl.when(i == pl.num_programs(0) - 1)
        def _():
            loss_ref[...] = (0.5 / d_model) * jnp.sum(loss_scr[...], axis=1, keepdims=True)

    vec = pl.BlockSpec((1, d_model), lambda i: (0, 0))
    evec = pl.BlockSpec((1, e), lambda i: (0, 0))
    tok = pl.BlockSpec((tb, d_model), lambda i: (i, 0))
    wide = pl.BlockSpec((tb, e), lambda i: (i, 0))
    def gblk(off):
        return pl.BlockSpec((ngb, tb, LANES), functools.partial(lambda i, ob: (0, i + ob, 0), ob=off // tb))

    once = dict(pipeline_mode=pl.Buffered(1))
    tok_f = jax.ShapeDtypeStruct((n_tok, d_model), F32)
    tok_b = jax.ShapeDtypeStruct((n_tok, d_model), BF16)
    wide_b = jax.ShapeDtypeStruct((n_tok, e), BF16)
    vec_f = jax.ShapeDtypeStruct((1, d_model), F32)
    evec_f = jax.ShapeDtypeStruct((1, e), F32)
    return pl.pallas_call(
        body, name=name, grid=(n_tok // tb,),
        out_shape=(jax.ShapeDtypeStruct((1, 1), F32), tok_f, tok_b, wide_b, wide_b, wide_b, wide_b,
                   *[jax.ShapeDtypeStruct((ngb, total, LANES), BF16) for total, _ in dy_rows],
                   vec_f, vec_f, vec_f, evec_f, evec_f),
        in_specs=[gblk(offs[0]), gblk(offs[1]), gblk(offs[2]),
                  pl.BlockSpec((nz, tb, e // nz), lambda i: (0, i, 0)), tok, vec, vec, vec, evec,
                  pl.BlockSpec((e, e), lambda i: (0, 0), **once), evec,
                  pl.BlockSpec((e, d_model), lambda i: (0, 0), **once), vec, vec, tok],
        out_specs=(pl.BlockSpec((1, 1), lambda i: (0, 0)), tok, tok, wide, wide, wide, wide,
                   *[gblk(off) for _, off in dy_rows], vec, vec, vec, evec, evec),
        scratch_shapes=[pltpu.VMEM((1, d_model), F32)] + [pltpu.VMEM((tb, e), F32)] * 4,
        compiler_params=_params(1),
    )(useq, yf, yb, z, xhat0, ln0[0], ln0[1], gt, d_vec, w_glu, b_glu, w_out, ln1[0], ln1[1], target)


def _ssm_inbwd(duf, dub, w, xhat, rstd, ln, sc, gt_prev, f_prev, *, lat, row_f, row_b, tb, name):
    ngb = duf.shape[0]
    e = ngb * LANES
    n_tok, d_model = xhat.shape
    tb = min(tb, n_tok)
    obf, obb = row_f // tb, row_b // tb
    has_lat = lat is not None
    n_w = w.shape[0] if has_lat else w.shape[0] // 2

    def body(*refs):
        if has_lat:
            (duf_ref, dub_ref, dyl_ref, dz_ref, d_ref, dxr_ref, w_ref, xh_ref, rs_ref, g_ref, b_ref, sc_ref, gt_ref,
             f_ref, dp_ref, dr_ref, df_ref, dsc_ref, dsh_ref, dg_ref, db_ref, dgt_ref) = refs
        else:
            (duf_ref, dub_ref, w_ref, xh_ref, rs_ref, g_ref, b_ref, sc_ref, gt_ref, f_ref, dp_ref, dr_ref, df_ref,
             dsc_ref, dsh_ref, dg_ref, db_ref, dgt_ref) = refs
        first = pl.program_id(0) == 0
        du = (jnp.concatenate([duf_ref[q] for q in range(ngb)], axis=1).astype(F32)
              + jnp.concatenate([dub_ref[q] for q in range(ngb)], axis=1).astype(F32))
        if has_lat:
            du = du + d_ref[...] * jnp.concatenate([dyl_ref[q] for q in range(ngb)], axis=1).astype(F32)
            dp_ref[:, e:2 * e] = dz_ref[...]
        else:
            dp_ref[:, e:2 * e] = jnp.zeros((tb, e), BF16)
        dp_ref[:, 0:e] = du.astype(BF16)
        dh = jnp.zeros((tb, d_model), F32)
        for j in range(n_w):
            dh = dh + _dot(dp_ref[:, j * d_model:(j + 1) * d_model], w_ref[j])
        xh = xh_ref[...]
        x1 = xh * g_ref[...] + b_ref[...]
        dx1 = dh * (1.0 + sc_ref[...])
        if has_lat:
            dx1 = dx1 + dxr_ref[...]
        _acc(dsc_ref, first, _colsum(dh * x1))
        _acc(dsh_ref, first, _colsum(dh))
        _acc(dg_ref, first, _colsum(dx1 * xh))
        _acc(db_ref, first, _colsum(dx1))
        dxh = dx1 * g_ref[...]
        dr = rs_ref[...] * (dxh - _rowmean(dxh) - xh * _rowmean(dxh * xh))
        dr_ref[...] = dr
        df_ref[...] = (dr * gt_ref[...]).astype(BF16)
        _acc(dgt_ref, first, _colsum(dr * f_ref[...].astype(F32)))

    vec = pl.BlockSpec((1, d_model), lambda i: (0, 0))
    tok = pl.BlockSpec((tb, d_model), lambda i: (i, 0))
    gblk = pl.BlockSpec((ngb, tb, LANES), lambda i: (0, i, 0))
    in_specs = [pl.BlockSpec((ngb, tb, LANES), lambda i: (0, i + obf, 0)),
                pl.BlockSpec((ngb, tb, LANES), lambda i: (0, i + obb, 0))]
    args = [duf, dub]
    if has_lat:
        in_specs += [gblk, pl.BlockSpec((tb, e), lambda i: (i, 0)), pl.BlockSpec((1, e), lambda i: (0, 0)), tok]
        args += list(lat)
    in_specs += [pl.BlockSpec(w.shape, lambda i: (0, 0, 0)), tok, pl.BlockSpec((tb, 1), lambda i: (i, 0)), vec, vec, vec,
                 vec, tok]
    args += [w, xhat, rstd, ln[0], ln[1], sc, gt_prev, f_prev]
    vec_f = jax.ShapeDtypeStruct((1, d_model), F32)
    return pl.pallas_call(
        body, name=name, grid=(n_tok // tb,),
        out_shape=(jax.ShapeDtypeStruct((n_tok, 2 * e), BF16), jax.ShapeDtypeStruct((n_tok, d_model), F32),
                   jax.ShapeDtypeStruct((n_tok, d_model), BF16), vec_f, vec_f, vec_f, vec_f, vec_f),
        in_specs=in_specs,
        out_specs=(pl.BlockSpec((tb, 2 * e), lambda i: (i, 0)), tok, tok, vec, vec, vec, vec, vec),
        compiler_params=_params(1),
    )(*args)


def _conv_bwd_a(df, w_out_t, p, yc, *, tb, name):
    _, n_tok, e = p.shape
    d_model = df.shape[1]
    tb = min(tb, n_tok)
    cs = _slab_width(e)

    def body(df_ref, wo_ref, bg_ref, z_ref, yc_ref, dbg_ref, dz_ref, dyc_ref):
        dfv = df_ref[...]
        for c0 in range(0, e, cs):
            sl = slice(c0, c0 + cs)
            dgv = _dot(dfv, wo_ref[:, sl])
            zf = z_ref[0, :, sl].astype(F32)
            sz = _sigmoid(zf)
            silu_z = zf * sz
            bg = bg_ref[0, :, sl].astype(F32)
            yc = yc_ref[:, sl].astype(F32)
            dbg_ref[:, sl] = (dgv * yc * silu_z).astype(BF16)
            dyc_ref[:, sl] = (dgv * bg * silu_z).astype(BF16)
            dz_ref[:, sl] = (dgv * bg * yc * (sz * (1.0 + zf * (1.0 - sz)))).astype(BF16)

    wide = pl.BlockSpec((tb, e), lambda i: (i, 0))
    shape = jax.ShapeDtypeStruct((n_tok, e), BF16)
    return pl.pallas_call(
        body, name=name, grid=(n_tok // tb,), out_shape=(shape, shape, shape),
        in_specs=[pl.BlockSpec((tb, d_model), lambda i: (i, 0)), pl.BlockSpec((d_model, e), lambda i: (0, 0)),
                  pl.BlockSpec((1, tb, e), lambda i: (0, i, 0)), pl.BlockSpec((1, tb, e), lambda i: (3, i, 0)), wide],
        out_specs=(wide, wide, wide), compiler_params=_params(1),
    )(df, w_out_t, p, p, yc)


def _conv_bwd_b(dyc, p, dbg, dz, conv_w, *, grid_mode, tb, name, hosted=None):
    _, n_tok, e = p.shape
    eh = e // 2
    if not grid_mode:
        tb = n_tok
    tb = min(tb, n_tok)
    nb = n_tok // tb
    hb = tb // GRID_W
    cs = _slab_width(e)

    def body(*refs):
        if grid_mode:
            dyc_ref, dycp_ref, dycn_ref, cg_ref, v_ref, dbg_ref, dz_ref, cw_ref, dp_ref, dcw_ref = refs
        else:
            dyc_ref, cg_ref, v_ref, dbg_ref, dz_ref, cw_ref, dp_ref, dcw_ref = refs
        i = pl.program_id(0)
        first = i == 0
        rows = lax.broadcasted_iota(jnp.int32, (tb, 1), 0)
        dp_ref[0] = dbg_ref[...]
        dp_ref[3] = dz_ref[...]
        for c0 in range(0, e, cs):
            sl = slice(c0, c0 + cs)
            dyc = dyc_ref[:, sl].astype(F32)
            w = cw_ref[:, sl]
            if grid_mode and c0 >= eh:
                hs = slice(c0 - eh, c0 - eh + cs)
                dprev = jnp.where(i > 0, dycp_ref[:, hs].astype(F32), 0.0)
                dnext = jnp.where(i < nb - 1, dycn_ref[:, hs].astype(F32), 0.0)
                if tb > GRID_W:
                    dm = jnp.concatenate([dprev, dyc[:tb - GRID_W]], axis=0)
                    dpl = jnp.concatenate([dyc[GRID_W:], dnext], axis=0)
                else:
                    dm, dpl = dprev, dnext
            else:
                dm, dpl = _shifted(dyc, rows, GRID_W if grid_mode else tb, tb)
            cg = cg_ref[0, :, sl].astype(F32)
            v = v_ref[0, :, sl].astype(F32)
            u = cg * v
            du = w[0:1] * dpl + w[1:2] * dyc + w[2:3] * dm
            dp_ref[1, :, sl] = (du * v).astype(BF16)
            dp_ref[2, :, sl] = (du * cg).astype(BF16)
            _acc(dcw_ref.at[:, sl], first, jnp.concatenate([_colsum(u * dpl), _colsum(u * dyc), _colsum(u * dm)], axis=0))

    n_hrows = n_tok // GRID_W
    wide = pl.BlockSpec((tb, e), lambda i: (i, 0))
    in_specs = [wide]
    args = [dyc]
    if grid_mode:
        in_specs += [pl.BlockSpec((GRID_W, eh), lambda i: (jnp.maximum(i * hb - 1, 0), 1)),
                     pl.BlockSpec((GRID_W, eh), lambda i: (jnp.minimum((i + 1) * hb, n_hrows - 1), 1))]
        args += [dyc, dyc]
    in_specs += [pl.BlockSpec((1, tb, e), lambda i: (1, i, 0)), pl.BlockSpec((1, tb, e), lambda i: (2, i, 0)), wide, wide,
                 pl.BlockSpec((3, e), lambda i: (0, 0))]
    args += [p, p, dbg, dz, conv_w]
    outs, extra = _call(
        body, name=name, grid=(nb,),
        out_shape=[jax.ShapeDtypeStruct((4, n_tok, e), BF16), jax.ShapeDtypeStruct((3, e), F32)],
        in_specs=in_specs,
        out_specs=[pl.BlockSpec((4, tb, e), lambda i: (0, i, 0)), pl.BlockSpec((3, e), lambda i: (0, 0))],
        scratch_shapes=[], args=args, hosted=hosted)
    return (*outs, extra)


def _conv_inbwd(dp, w, dr, x, sc, *, tb, name, hosted=None):
    n_chunks, n_tok, e = dp.shape
    d_model = x.shape[1]
    tb = min(tb, n_tok)

    def body(dp_ref, w_ref, dr_ref, x_ref, sc_ref, gx_ref, dsc_ref, dsh_ref, dh_scr):
        k = pl.program_id(1)
        first = pl.program_id(0) == 0
        _acc(dh_scr, k == 0, _dot(dp_ref[0], w_ref[0]))

        @pl.when(k == n_chunks - 1)
        def _():
            dh = dh_scr[...]
            gx_ref[...] = DN_ALPHA * dr_ref[...] + dh * (1.0 + sc_ref[...])
            _acc(dsc_ref, first, _colsum(dh * x_ref[...]))
            _acc(dsh_ref, first, _colsum(dh))

    vec = pl.BlockSpec((1, d_model), lambda i, k: (0, 0))
    tok = pl.BlockSpec((tb, d_model), lambda i, k: (i, 0))
    vec_f = jax.ShapeDtypeStruct((1, d_model), F32)
    outs, extra = _call(
        body, name=name, grid=(n_tok // tb, n_chunks),
        out_shape=[jax.ShapeDtypeStruct((n_tok, d_model), F32), vec_f, vec_f],
        in_specs=[pl.BlockSpec((1, tb, e), lambda i, k: (k, i, 0)), pl.BlockSpec((1, e, d_model), lambda i, k: (k, 0, 0)),
                  tok, tok, vec],
        out_specs=[tok, vec, vec],
        scratch_shapes=[pltpu.VMEM((tb, d_model), F32)], args=(dp, w, dr, x, sc), hosted=hosted)
    return (*outs, extra)


def _wgrad(a, b, *, n_chunks, tm, tl, init=None, name):
    n_tok, m = a.shape
    tl = min(tl, n_tok)
    chunked = b.ndim == 3
    cw = b.shape[2] if chunked else b.shape[1] // n_chunks
    has_init = init is not None

    def body(*refs):
        if has_init:
            a_ref, b_ref, init_ref, o_ref = refs
        else:
            a_ref, b_ref, o_ref = refs
        bv = b_ref[0] if chunked else b_ref[...]
        part = _dot_tn(a_ref[...], bv)
        l = pl.program_id(2)

        @pl.when(l == 0)
        def _():
            o_ref[0] = part + init_ref[0] if has_init else part

        @pl.when(l > 0)
        def _():
            o_ref[0] += part

    o_spec = pl.BlockSpec((1, tm, cw), lambda jm, jc, l: (jc, jm, 0))
    b_spec = (pl.BlockSpec((1, tl, cw), lambda jm, jc, l: (jc, l, 0)) if chunked
              else pl.BlockSpec((tl, cw), lambda jm, jc, l: (l, jc)))
    init_spec = pl.BlockSpec((1, tm, cw), lambda jm, jc, l: (jc, jm, 0), pipeline_mode=pl.Buffered(1))
    in_specs = [pl.BlockSpec((tl, tm), lambda jm, jc, l: (l, jm)), b_spec] + ([init_spec] if has_init else [])
    args = (a, b) + ((init,) if has_init else ())
    return pl.pallas_call(
        body, name=name, grid=(m // tm, n_chunks, n_tok // tl),
        out_shape=jax.ShapeDtypeStruct((n_chunks, m, cw), F32),
        in_specs=in_specs, out_specs=o_spec, compiler_params=_params(3),
    )(*args)


def _block_diag(t, ngb):
    g, p, n = t.shape
    gpb = g // ngb
    eye = jnp.eye(gpb, dtype=t.dtype)
    return jnp.einsum("bgpn,gh->bgphn", t.reshape(ngb, gpb, p, n), eye).reshape(ngb, gpb * p, gpb * n)


def _block_diag_t(mat, g, p, n):
    ngb = mat.shape[0]
    gpb = g // ngb
    eye = jnp.eye(gpb, dtype=mat.dtype)
    return jnp.einsum("bgphn,gh->bgpn", mat.reshape(ngb, gpb, p, gpb, n), eye).reshape(g, p, n)


def _scan_tables(pw_r, pw_i, ngb, reverse):
    _, g, n = pw_r.shape
    sb = g * n // ngb
    rows = jnp.arange(SUBLANES)
    kinds = []
    for step in (1, 2, 4):
        mask = ((rows < SUBLANES - step) if reverse else (rows >= step)).astype(F32)
        for part in (pw_r[step - 1], pw_i[step - 1]):
            kinds.append(part.reshape(ngb, 1, sb) * mask[None, :, None])
    for part in (pw_r, pw_i):
        pw = part[::-1] if reverse else part
        kinds.append(jnp.transpose(pw.reshape(SUBLANES, ngb, sb), (1, 0, 2)))
    return jnp.stack(kinds, axis=1)


def _flat(parts):
    return jnp.concatenate([p.reshape(-1) for p in parts])


def _unflat(vec, shapes):
    out, off = [], 0
    for s in shapes:
        size = math.prod(s)
        out.append(vec[off:off + size].reshape(s))
        off += size
    return out


def kernel(x, c, ctx, c_ctx, ada_w, ada_b, ln_g, ln_b, conv_w_in, conv_w, conv_w_out, ssm_w_in, ssm_lam_re, ssm_lam_im, ssm_log_step, ssm_b_re, ssm_b_im, ssm_c_re, ssm_c_im, ssm_d, ssm_w_glu, ssm_b_glu, ssm_w_out, loss_target, m_c_ctx, m_ada_w, m_ada_b, m_ln_g, m_ln_b, m_conv_w_in, m_conv_w, m_conv_w_out, m_ssm_w_in, m_ssm_lam_re, m_ssm_lam_im, m_ssm_log_step, m_ssm_b_re, m_ssm_b_im, m_ssm_c_re, m_ssm_c_im, m_ssm_d, m_ssm_w_glu, m_ssm_b_glu, m_ssm_w_out, v_c_ctx, v_ada_w, v_ada_b, v_ln_g, v_ln_b, v_conv_w_in, v_conv_w, v_conv_w_out, v_ssm_w_in, v_ssm_lam_re, v_ssm_lam_im, v_ssm_log_step, v_ssm_b_re, v_ssm_b_im, v_ssm_c_re, v_ssm_c_im, v_ssm_d, v_ssm_w_glu, v_ssm_b_glu, v_ssm_w_out):
    weights = dict(c_ctx=c_ctx, ada_w=ada_w, ada_b=ada_b, ln_g=ln_g, ln_b=ln_b, conv_w_in=conv_w_in, conv_w=conv_w,
                   conv_w_out=conv_w_out, ssm_w_in=ssm_w_in, ssm_lam_re=ssm_lam_re, ssm_lam_im=ssm_lam_im,
                   ssm_log_step=ssm_log_step, ssm_b_re=ssm_b_re, ssm_b_im=ssm_b_im, ssm_c_re=ssm_c_re,
                   ssm_c_im=ssm_c_im, ssm_d=ssm_d, ssm_w_glu=ssm_w_glu, ssm_b_glu=ssm_b_glu, ssm_w_out=ssm_w_out)
    mom_m = dict(c_ctx=m_c_ctx, ada_w=m_ada_w, ada_b=m_ada_b, ln_g=m_ln_g, ln_b=m_ln_b, conv_w_in=m_conv_w_in,
                 conv_w=m_conv_w, conv_w_out=m_conv_w_out, ssm_w_in=m_ssm_w_in, ssm_lam_re=m_ssm_lam_re,
                 ssm_lam_im=m_ssm_lam_im, ssm_log_step=m_ssm_log_step, ssm_b_re=m_ssm_b_re, ssm_b_im=m_ssm_b_im,
                 ssm_c_re=m_ssm_c_re, ssm_c_im=m_ssm_c_im, ssm_d=m_ssm_d, ssm_w_glu=m_ssm_w_glu,
                 ssm_b_glu=m_ssm_b_glu, ssm_w_out=m_ssm_w_out)
    mom_v = dict(c_ctx=v_c_ctx, ada_w=v_ada_w, ada_b=v_ada_b, ln_g=v_ln_g, ln_b=v_ln_b, conv_w_in=v_conv_w_in,
                 conv_w=v_conv_w, conv_w_out=v_conv_w_out, ssm_w_in=v_ssm_w_in, ssm_lam_re=v_ssm_lam_re,
                 ssm_lam_im=v_ssm_lam_im, ssm_log_step=v_ssm_log_step, ssm_b_re=v_ssm_b_re, ssm_b_im=v_ssm_b_im,
                 ssm_c_re=v_ssm_c_re, ssm_c_im=v_ssm_c_im, ssm_d=v_ssm_d, ssm_w_glu=v_ssm_w_glu,
                 ssm_b_glu=v_ssm_b_glu, ssm_w_out=v_ssm_w_out)
    names = list(weights)

    n_lat, d_model = x.shape[1], x.shape[2]
    n_ctx = ctx.shape[1]
    e = 2 * d_model
    n_grp, n_state, grp = ssm_lam_re.shape[2], ssm_lam_re.shape[3], ssm_b_re.shape[4]
    ngb = e // LANES
    ws = ada_w.shape[2]
    tb_tok = min(512, n_lat)
    n_seq = n_ctx + n_lat
    tb_glu = math.gcd(256, n_ctx)
    chip = 2 * lax.axis_index("x") + lax.axis_index("y")
    me = 2 * chip + lax.axis_index("c")
    chips, everyone, pair = ("x", "y"), MESH_AXES, ("c",)

    x2, ctx2, tgt2 = x[0], ctx[0], loss_target[0]

    wc_in = _exchange(conv_w_in[0].astype(BF16), chips, False, "ag_conv_w_in")
    later_weights = _Hosted([(w[0].astype(BF16), chips, False) for w in (conv_w_out, ssm_w_in, ssm_w_glu, ssm_w_out)])
    small_full = _exchange(_flat([conv_w[0], ssm_d[0], ssm_b_glu[0]]).reshape(1, -1), chips, False, "ag_small")
    es = conv_w.shape[2]
    conv_w_full = jnp.transpose(small_full[:, 0, :3 * es].reshape(4, 3, es), (1, 0, 2)).reshape(3, e)
    d_full = small_full[:, 0, 3 * es:4 * es].reshape(1, e)
    b_glu_full = small_full[:, 0, 4 * es:5 * es].reshape(1, e)

    c_all = _exchange(c, everyone, False, "ag_c").reshape(8, d_model)
    cc2 = c_ctx.reshape(1, d_model)
    b_sh = lax.dynamic_slice_in_dim(ada_b, chip * ws, ws, axis=1).reshape(DEPTH, 1, ws)
    m_sh = _ada_fwd(c_all, cc2, ada_w, b_sh)
    m_all = _exchange(m_sh, chips, False, "ag_mod")
    m_full = jnp.transpose(m_all, (1, 2, 0, 3)).reshape(DEPTH, 16, 3 * d_model)
    m_lat = lax.dynamic_slice_in_dim(m_full, me, 1, axis=1)
    m_ctx = m_full[:, 8:9]

    def mods(m, i):
        return m[i, :, 0:d_model], m[i, :, d_model:2 * d_model], m[i, :, 2 * d_model:3 * d_model]

    sh0, sc0, gt0 = mods(m_lat, 0)
    sh1, sc1, gt1 = mods(m_lat, 1)
    shc0, scc0, gtc0 = mods(m_ctx, 0)
    shc1, scc1, _ = mods(m_ctx, 1)
    ln0 = (ln_g[0:1], ln_b[0:1])
    ln1 = (ln_g[1:2], ln_b[1:2])

    rg = 2 * n_grp
    lam_re2 = ssm_lam_re[0].reshape(rg, n_state)
    lam_im2 = ssm_lam_im[0].reshape(rg, n_state)
    log_step2 = ssm_log_step[0].reshape(rg, 1)
    b_re_t = jnp.transpose(ssm_b_re[0], (3, 0, 1, 2)).reshape(grp, rg, n_state)
    b_im_t = jnp.transpose(ssm_b_im[0], (3, 0, 1, 2)).reshape(grp, rg, n_state)
    pw_r, pw_i, pq_r, pq_i, bbr, bbi = _zoh_fwd(lam_re2, lam_im2, log_step2, b_re_t, b_im_t)
    sbk = n_grp * n_state // ngb
    pw_r, pw_i, pq_r, pq_i = (t.reshape(8, 2, n_grp, n_state) for t in (pw_r, pw_i, pq_r, pq_i))
    bbr_g = jnp.transpose(bbr.reshape(grp, 2, n_grp, n_state), (1, 2, 0, 3))
    bbi_g = jnp.transpose(bbi.reshape(grp, 2, n_grp, n_state), (1, 2, 0, 3))

    def power_rows(pw, r, first):
        full = jnp.concatenate([jnp.full((1, n_grp, n_state), first, F32), pw[:, r]], axis=0)
        return jnp.transpose(full.reshape(CHUNK + 1, ngb, sbk), (1, 0, 2))

    s5 = []
    for r in range(2):
        prm = dict(bre=_block_diag(bbr_g[r], ngb), bim=_block_diag(bbi_g[r], ngb),
                   cre=_block_diag(ssm_c_re[0, r], ngb), cim=_block_diag(ssm_c_im[0, r], ngb),
                   wr=power_rows(pw_r, r, 1.0), wi=power_rows(pw_i, r, 0.0))
        t_op, bp_op, cp_op = _s5_ops(prm["bre"], prm["bim"], prm["cre"], prm["cim"], prm["wr"], prm["wi"],
                                     reverse=(r == 1), name=f"l1_s5_ops{r}")
        s5.append(dict(
            prm, t=t_op, bp=bp_op, cp=cp_op,
            tab=_scan_tables(pq_r[:, r], pq_i[:, r], ngb, reverse=(r == 1)),
            tab_adj=_scan_tables(pq_r[:, r], -pq_i[:, r], ngb, reverse=(r == 0))))

    p0, h0, gathered = _inproj(x2, sc0, sh0, wc_in, tb=min(1024, n_lat), name="l0_inproj", hosted=later_weights)
    wc_out, ws_in, w_glu, ws_out = gathered
    wc_out, w_glu, ws_out = wc_out.reshape(e, d_model), w_glu.reshape(e, e), ws_out.reshape(e, d_model)
    wc_in_t, ws_in_t, wc_out_t = jnp.transpose(wc_in, (0, 2, 1)), jnp.transpose(ws_in, (0, 2, 1)), wc_out.T
    pc0, hc0 = _inproj(ctx2, scc0, shc0, wc_in, tb=tb_tok, name="l0_inproj_ctx")
    xhat0, rstd0, g0, yc0, f0 = _convgate(p0, x2, gt0, conv_w_full, wc_out, *ln0, grid_mode=True, tb=tb_tok, name="l0_conv")
    chat0, crstd0, gc0, ycc0, fc0 = _convgate(pc0, ctx2, gtc0, conv_w_full, wc_out, *ln0, grid_mode=False, tb=tb_tok,
                                              name="l0_conv_ctx")

    seq_rows = [(n_seq, n_ctx), (n_seq, 0)]
    useq_f, useq_b, h1 = _inproj(xhat0, sc1, sh1, ws_in[0:2], lnaff=ln0, tb=math.gcd(tb_tok, n_ctx), gb_rows=seq_rows,
                                 name="l1_inproj_u")
    z1, _ = _inproj(xhat0, sc1, sh1, ws_in[2:4], lnaff=ln0, tb=tb_tok, name="l1_inproj_z")
    uc, hc1 = _inproj(chat0, scc1, shc1, ws_in[0:2], lnaff=ln0, tb=tb_tok, gb_rows=[(n_ctx, 0)], name="l1_inproj_ctx")
    useq = [useq_f.at[:, 0:n_ctx].set(uc), useq_b.at[:, n_lat:].set(uc)]
    y_dir, hp_dir = [], []
    for r in range(2):
        yr, hcr = _s5_fwd(useq[r], s5[r]["t"], s5[r]["bp"], s5[r]["cp"], s5[r]["tab"], reverse=(r == 1),
                          name=f"l1_s5_fwd{r}")
        y_dir.append(yr)
        hp_dir.append(hcr)

    (loss, dxres, do1, gz1, gg1, dq1, dz1, dy_f, dy_b, dg1, db1, dgt1, dbglu, dd) = _glu_loss(
        useq[0], y_dir[0], y_dir[1], z1, xhat0, ln0, gt1, d_full, w_glu, b_glu_full, ws_out, ln1, tgt2,
        offs=(n_ctx, n_ctx, 0), dy_rows=seq_rows, tb=tb_glu, name="l1_glu_loss")
    no_dy = jnp.zeros((ngb, n_ctx, LANES), BF16)
    dy_dir = [dy_f.at[:, 0:n_ctx].set(no_dy), dy_b.at[:, n_lat:].set(no_dy)]

    tl = min(1024, n_lat)

    def owner_slices(name, full):
        w = weights[name]
        return full.reshape(8, math.prod(w.shape[:-1]) // 2, w.shape[-1])

    def scatter(named):
        return _Hosted([(owner_slices(name, full), everyone, True) for name, full in named])

    def siblings(names):
        return _Hosted([(_sum_parts(rs_parts[name], "sum_" + name), pair, False) for name in names])

    rs_parts, both_halves = {}, {}

    gw_glu = _wgrad(gg1, dq1, n_chunks=1, tm=e // 2, tl=tl, name="wg_glu")
    gw_ssm_out = _wgrad(gz1, do1, n_chunks=1, tm=e, tl=tl, name="wg_ssm_out")
    du_dir, s5_grads = [], []
    for r in range(2):
        if r == 0:
            hosted = scatter([("ssm_w_glu", gw_glu), ("ssm_w_out", gw_ssm_out)])
        else:
            hosted = siblings(["ssm_w_glu", "ssm_w_out"])
        dur, dt_op, dbp_op, dcp_op, da8, extra = _s5_bwd(useq[r], dy_dir[r], hp_dir[r], s5[r]["t"], s5[r]["bp"],
                                                         s5[r]["cp"], s5[r]["tab_adj"], reverse=(r == 1),
                                                         name=f"l1_s5_bwd{r}", hosted=hosted)
        if r == 0:
            rs_parts["ssm_w_glu"], rs_parts["ssm_w_out"] = extra
        else:
            both_halves["ssm_w_glu"], both_halves["ssm_w_out"] = extra
        du_dir.append(dur)
        prm = s5[r]
        s5_grads.append(_s5_ops_bwd(prm["bre"], prm["bim"], prm["cre"], prm["cim"], prm["wr"], prm["wi"],
                                    prm["wr"][:, 1:2], prm["wi"][:, 1:2], dt_op, dbp_op, dcp_op, da8,
                                    reverse=(r == 1), name=f"l1_s5_ops_bwd{r}"))
    dp1, dr0, df0, dsc1, dsh1, dg0, db0, dgt0 = _ssm_inbwd(
        du_dir[0], du_dir[1], ws_in_t, xhat0, rstd0, ln0, sc1, gt0, f0, lat=(dy_dir[1], dz1, d_full, dxres),
        row_f=n_ctx, row_b=0, tb=tb_glu, name="l1_inbwd")
    dpc1, drc0, dfc0, dscc1, dshc1, dgc0, dbc0, dgtc0 = _ssm_inbwd(
        du_dir[0], du_dir[1], ws_in_t, chat0, crstd0, ln0, scc1, gtc0, fc0, lat=None,
        row_f=0, row_b=n_lat, tb=n_ctx, name="l1_inbwd_ctx")

    def conv_backward(df, p, yc, dr, xin, sc, grid_mode, tag, hosted_b=None, hosted_in=None):
        dbg, dz, dyc = _conv_bwd_a(df, wc_out_t, p, yc, tb=tb_tok, name="l0_bwd_a" + tag)
        dp, dcw, extra_b = _conv_bwd_b(dyc, p, dbg, dz, conv_w_full, grid_mode=grid_mode, tb=tb_glu,
                                       name="l0_bwd_b" + tag, hosted=hosted_b)
        gx, dsc, dsh, extra_in = _conv_inbwd(dp, wc_in_t, dr, xin, sc, tb=tb_tok, name="l0_inbwd" + tag,
                                             hosted=None if hosted_in is None else hosted_in(dp, extra_b))
        return dp, dcw, gx, dsc, dsh, extra_b, extra_in

    dpc0, dcwc0, _, dscc0, dshc0, _, _ = conv_backward(dfc0, pc0, ycc0, drc0, ctx2, scc0, False, "_ctx")
    gw_conv_out = _wgrad(g0, df0, n_chunks=1, tm=e, tl=tl, name="wg_conv_out",
                         init=_wgrad(gc0, dfc0, n_chunks=1, tm=e, tl=tl, name="wg_conv_out_ctx"))
    gw_ssm_in = _wgrad(h1, dp1, n_chunks=4, tm=d_model, tl=tl, name="wg_ssm_in",
                       init=_wgrad(hc1, dpc1, n_chunks=4, tm=d_model, tl=tl, name="wg_ssm_in_ctx"))
    gw_conv_in_ctx = _wgrad(hc0, dpc0, n_chunks=4, tm=d_model, tl=tl, name="wg_conv_in_ctx")

    def behind_inbwd(dp, arrived):
        rs_parts["ssm_w_in"], rs_parts["conv_w_out"] = arrived
        gw_conv_in = _wgrad(h0, dp, n_chunks=4, tm=d_model, tl=tl, name="wg_conv_in", init=gw_conv_in_ctx)
        both = siblings(["ssm_w_in", "conv_w_out"])
        return _Hosted(scatter([("conv_w_in", gw_conv_in)]).items + both.items)

    dp0, dcw0, grad_x, dsc0, dsh0, _, extra_in = conv_backward(
        df0, p0, yc0, dr0, x2, sc0, True, "", hosted_b=scatter([("ssm_w_in", gw_ssm_in), ("conv_w_out", gw_conv_out)]),
        hosted_in=behind_inbwd)
    rs_parts["conv_w_in"], both_halves["ssm_w_in"], both_halves["conv_w_out"] = extra_in
    both_halves["conv_w_in"] = _exchange(_sum_parts(rs_parts["conv_w_in"], "sum_conv_w_in"), pair, False, "pair_conv_w_in")

    grads, deltas, new_m, new_v = {}, {}, {}, {}
    for name in ("ssm_w_glu", "ssm_w_out", "ssm_w_in", "conv_w_out", "conv_w_in"):
        w = weights[name]
        rows, cols = math.prod(w.shape[:-1]), w.shape[-1]
        both = both_halves[name].reshape(rows, cols)
        dlt, nm, nv = _adamw(w.reshape(rows, cols), both, mom_m[name].reshape(rows, cols),
                             mom_v[name].reshape(rows, cols), "adamw_" + name)
        grads[name], deltas[name] = both.reshape(w.shape), dlt.reshape(w.shape)
        new_m[name], new_v[name] = nm.reshape(w.shape), nv.reshape(w.shape)

    gpn = (n_grp, grp, n_state)
    small_parts = [
        jnp.concatenate([dg0 + dgc0, dg1], axis=0), jnp.concatenate([db0 + dbc0, db1], axis=0),
        dcw0 + dcwc0, dd, dbglu,
        jnp.stack([s5_grads[r][4] for r in range(2)]),
    ] + [jnp.stack([_block_diag_t(s5_grads[r][k], *gpn) for r in range(2)]) for k in range(4)]
    small_shapes = [p.shape for p in small_parts]
    flat = _flat(small_parts)
    quantum = 8 * SUBLANES * LANES
    n_flat = -(-flat.shape[0] // quantum) * quantum
    flat = jnp.pad(flat, (0, n_flat - flat.shape[0])).reshape(8, n_flat // (8 * LANES), LANES)
    red = _sum_parts(_exchange(flat, everyone, True, "rs_small"), "sum_small")
    red = _exchange(red, everyone, False, "ag_small_grads").reshape(-1)
    g_ln_g, g_ln_b, g_conv_w, g_d, g_bglu, g_a, g_bbr, g_bbi, g_cre, g_cim = _unflat(red, small_shapes)

    g_a = g_a.reshape(2, ngb, 2, sbk)
    dar = g_a[:, :, 0].reshape(rg, n_state)
    dai = g_a[:, :, 1].reshape(rg, n_state)
    dbbr_t = jnp.transpose(g_bbr, (2, 0, 1, 3)).reshape(grp, rg, n_state)
    dbbi_t = jnp.transpose(g_bbi, (2, 0, 1, 3)).reshape(grp, rg, n_state)
    z_lre, z_lim, z_ls, z_bre, z_bim = _zoh_bwd(lam_re2, lam_im2, log_step2, b_re_t, b_im_t, dar, dai, dbbr_t, dbbi_t)

    zero = jnp.zeros((1, d_model), F32)
    dm_rows = jnp.stack([
        jnp.stack([jnp.concatenate([dsh0, dsc0, dgt0], axis=1), jnp.concatenate([dshc0, dscc0, dgtc0], axis=1)]),
        jnp.stack([jnp.concatenate([dsh1, dsc1, dgt1], axis=1), jnp.concatenate([dshc1, dscc1, zero], axis=1)]),
    ]).reshape(DEPTH, 2, 3 * d_model)
    dm_all = _exchange(dm_rows, everyone, False, "ag_dmod")
    dm_sh = lax.dynamic_slice_in_dim(dm_all, chip * ws, ws, axis=3)
    g_ada_w, g_ada_b, ds_part = _ada_bwd(c_all, cc2, ada_w, dm_all, dm_sh)
    g_cctx = _cctx_grad(_exchange(ds_part, chips, False, "ag_dsctx"), cc2)

    grads["ada_w"] = g_ada_w
    dlt, nm, nv = _adamw(ada_w.reshape(-1, ws), g_ada_w.reshape(-1, ws), m_ada_w.reshape(-1, ws),
                         v_ada_w.reshape(-1, ws), "adamw_ada_w")
    deltas["ada_w"], new_m["ada_w"], new_v["ada_w"] = dlt.reshape(ada_w.shape), nm.reshape(ada_w.shape), nv.reshape(ada_w.shape)

    def chip_cols(full, rows):
        return lax.dynamic_slice_in_dim(full.reshape(rows, e), chip * es, es, axis=1)

    small_grads = dict(
        c_ctx=g_cctx.reshape(c_ctx.shape), ada_b=g_ada_b.reshape(ada_b.shape), ln_g=g_ln_g, ln_b=g_ln_b,
        conv_w=chip_cols(g_conv_w, 3).reshape(conv_w.shape),
        ssm_lam_re=z_lre.reshape(ssm_lam_re.shape), ssm_lam_im=z_lim.reshape(ssm_lam_im.shape),
        ssm_log_step=z_ls.reshape(ssm_log_step.shape),
        ssm_b_re=jnp.transpose(z_bre.reshape(grp, 2, n_grp, n_state), (1, 2, 3, 0)).reshape(ssm_b_re.shape),
        ssm_b_im=jnp.transpose(z_bim.reshape(grp, 2, n_grp, n_state), (1, 2, 3, 0)).reshape(ssm_b_im.shape),
        ssm_c_re=g_cre.reshape(ssm_c_re.shape), ssm_c_im=g_cim.reshape(ssm_c_im.shape),
        ssm_d=chip_cols(g_d, 1).reshape(ssm_d.shape), ssm_b_glu=chip_cols(g_bglu, 1).reshape(ssm_b_glu.shape))
    small_names = list(small_grads)
    shapes = [weights[n].shape for n in small_names]
    quantum = SUBLANES * LANES

    def pack(parts, fill):
        vec = _flat(parts)
        n_pad = -(-vec.shape[0] // quantum) * quantum
        return jnp.pad(vec, (0, n_pad - vec.shape[0]), constant_values=fill).reshape(-1, LANES)

    dlt, nm, nv = _adamw(pack([weights[n] for n in small_names], 0.0), pack([small_grads[n] for n in small_names], 0.0),
                         pack([mom_m[n] for n in small_names], 0.0), pack([mom_v[n] for n in small_names], 1.0),
                         "adamw_small")
    for n, dv, mv, vv in zip(small_names, _unflat(dlt.reshape(-1), shapes), _unflat(nm.reshape(-1), shapes),
                             _unflat(nv.reshape(-1), shapes)):
        grads[n], deltas[n], new_m[n], new_v[n] = small_grads[n], dv, mv, vv

    loss_total = lax.psum(loss[0, 0], MESH_AXES)
    return (loss_total, grad_x.reshape(x.shape), *[grads[n] for n in names], *[deltas[n] for n in names],
            *[new_m[n] for n in names], *[new_v[n] for n in names])
```

```python
import functools
import math

import jax
import jax.numpy as jnp
from jax import lax
from jax.experimental import pallas as pl
from jax.experimental.pallas import tpu as pltpu

F32 = jnp.float32
BF16 = jnp.bfloat16
LANES = 128
SUBLANES = 8
VMEM_LIMIT = 56 * 1024 * 1024
MESH_AXES = ("x", "y", "c")
HIGHEST = lax.Precision.HIGHEST

GRID_W = 64
LN_EPS = 1e-5
DEPTH = 2
DN_ALPHA = (2 * DEPTH) ** 0.25
ADAM_LR, ADAM_B1, ADAM_B2, ADAM_EPS, ADAM_WD, ADAM_STEP = 0.001, 0.9, 0.999, 1e-08, 0.01, 10
GELU_K = math.sqrt(2.0 / math.pi)
GELU_C = 0.044715


def _params(n_grid_axes):
    return pltpu.CompilerParams(dimension_semantics=("arbitrary",) * n_grid_axes, vmem_limit_bytes=VMEM_LIMIT)


def _dot(a, b):
    return jnp.dot(a, b, preferred_element_type=F32)


def _dot_nt(a, b):
    return lax.dot_general(a, b, (((1,), (1,)), ((), ())), preferred_element_type=F32)


def _dot_tn(a, b):
    return lax.dot_general(a, b, (((0,), (0,)), ((), ())), preferred_element_type=F32)


def _sigmoid(x):
    return 0.5 * jnp.tanh(0.5 * x) + 0.5


def _colsum(x):
    return jnp.sum(x, axis=0, keepdims=True)


def _rowmean(x):
    return jnp.mean(x, axis=-1, keepdims=True)


def _acc(ref, first, value):
    @pl.when(first)
    def _():
        ref[...] = value

    @pl.when(jnp.logical_not(first))
    def _():
        ref[...] += value


def _exchange_copies(src_ref, out_ref, send_sems, recv_sems, own_sem, axes, all_to_all, sem0=0):
    n_peers = 2 ** len(axes)
    pos = {a: lax.axis_index(a) for a in MESH_AXES}

    def index(p):
        return sum(p[a] * (2 ** (len(axes) - 1 - i)) for i, a in enumerate(axes))

    me = index(pos)
    own = pltpu.make_async_copy(src_ref.at[me] if all_to_all else src_ref, out_ref.at[me], own_sem)
    copies = []
    for k in range(1, n_peers):
        peer = dict(pos)
        for i, a in enumerate(axes):
            if (k >> (len(axes) - 1 - i)) & 1:
                peer[a] = 1 - pos[a]
        copies.append(pltpu.make_async_remote_copy(
            src_ref=src_ref.at[index(peer)] if all_to_all else src_ref,
            dst_ref=out_ref.at[me],
            send_sem=send_sems.at[sem0 + k - 1],
            recv_sem=recv_sems.at[sem0 + k - 1],
            device_id=tuple(peer[a] for a in MESH_AXES),
            device_id_type=pl.DeviceIdType.MESH,
        ))
    return copies, own


def _exchange_shape(src, axes, all_to_all):
    block = tuple(src.shape[1:] if all_to_all else src.shape)
    return jax.ShapeDtypeStruct((2 ** len(axes),) + block, src.dtype)


def _exchange(src, axes, all_to_all, name):
    n_peers = 2 ** len(axes)

    def body(src_ref, out_ref, send_sems, recv_sems, own_sem):
        copies, own = _exchange_copies(src_ref, out_ref, send_sems, recv_sems, own_sem, axes, all_to_all)
        own.start()
        for cp in copies:
            cp.start()
        for cp in copies:
            cp.wait()
        own.wait()

    return pl.pallas_call(
        body,
        name=name,
        out_shape=_exchange_shape(src, axes, all_to_all),
        in_specs=[pl.BlockSpec(memory_space=pltpu.HBM)],
        out_specs=pl.BlockSpec(memory_space=pltpu.HBM),
        scratch_shapes=[
            pltpu.SemaphoreType.DMA((n_peers - 1,)),
            pltpu.SemaphoreType.DMA((n_peers - 1,)),
            pltpu.SemaphoreType.DMA,
        ],
    )(src)


class _Hosted:
    def __init__(self, items):
        self.items = items
        self.args = [src for src, _, _ in items]
        self.in_specs = [pl.BlockSpec(memory_space=pltpu.HBM)] * len(items)
        self.out_specs = [pl.BlockSpec(memory_space=pltpu.HBM)] * len(items)
        self.out_shapes = [_exchange_shape(*item) for item in items]
        n_remote = sum(2 ** len(axes) - 1 for _, axes, _ in items)
        self.scratch = [pltpu.SemaphoreType.DMA((n_remote,)), pltpu.SemaphoreType.DMA((n_remote,)),
                        pltpu.SemaphoreType.DMA((len(items),))]

    def _copies(self, src_refs, out_refs, send_sems, recv_sems, own_sems):
        out, sem0 = [], 0
        for n, (_, axes, all_to_all) in enumerate(self.items):
            copies, own = _exchange_copies(src_refs[n], out_refs[n], send_sems, recv_sems, own_sems.at[n], axes,
                                           all_to_all, sem0)
            out += [own] + copies
            sem0 += len(copies)
        return out

    def start(self, *refs):
        for cp in self._copies(*refs):
            cp.start()

    def wait(self, *refs):
        for cp in self._copies(*refs):
            cp.wait()


def _call(body, *, name, grid, in_specs, out_specs, out_shape, scratch_shapes, args, hosted=None):
    params = _params(len(grid))
    if hosted is None:
        outs = pl.pallas_call(body, name=name, grid=grid, in_specs=in_specs, out_specs=tuple(out_specs),
                              out_shape=tuple(out_shape), scratch_shapes=list(scratch_shapes), compiler_params=params)(*args)
        return list(outs), []
    n_in, n_out, n_scr, n_h = len(in_specs), len(out_shape), len(scratch_shapes), len(hosted.items)

    def wrapped(*refs):
        ins, h_in = refs[:n_in], refs[n_in:n_in + n_h]
        outs, h_out = refs[n_in + n_h:n_in + n_h + n_out], refs[n_in + n_h + n_out:n_in + 2 * n_h + n_out]
        scr = refs[n_in + 2 * n_h + n_out:]
        first = functools.reduce(jnp.logical_and, [pl.program_id(k) == 0 for k in range(len(grid))])
        last = functools.reduce(jnp.logical_and, [pl.program_id(k) == grid[k] - 1 for k in range(len(grid))])

        @pl.when(first)
        def _():
            hosted.start(h_in, h_out, *scr[n_scr:])

        body(*ins, *outs, *scr[:n_scr])

        @pl.when(last)
        def _():
            hosted.wait(h_in, h_out, *scr[n_scr:])

    outs = pl.pallas_call(
        wrapped, name=name, grid=grid, in_specs=[*in_specs, *hosted.in_specs],
        out_specs=(*out_specs, *hosted.out_specs), out_shape=(*out_shape, *hosted.out_shapes),
        scratch_shapes=[*scratch_shapes, *hosted.scratch], compiler_params=params)(*args, *hosted.args)
    return list(outs[:n_out]), list(outs[n_out:])


def _sum_parts(parts, name):
    n_parts, rows, cols = parts.shape
    tr = rows
    while n_parts * tr * cols * 4 > 8 * 1024 * 1024 and tr % 16 == 0:
        tr //= 2

    def body(p_ref, o_ref):
        total = p_ref[0]
        for k in range(1, n_parts):
            total = total + p_ref[k]
        o_ref[...] = total

    return pl.pallas_call(
        body,
        name=name,
        grid=(rows // tr,),
        out_shape=jax.ShapeDtypeStruct((rows, cols), F32),
        in_specs=[pl.BlockSpec((n_parts, tr, cols), lambda i: (0, i, 0))],
        out_specs=pl.BlockSpec((tr, cols), lambda i: (i, 0)),
        compiler_params=_params(1),
    )(parts)


def _adamw(w, g, m, v, name):
    rows, cols = w.shape
    tr = rows
    while tr * cols * 4 > 2 * 1024 * 1024 and tr % 16 == 0:
        tr //= 2

    def body(w_ref, g_ref, m_ref, v_ref, d_ref, nm_ref, nv_ref):
        gv = g_ref[...]
        nm = ADAM_B1 * m_ref[...] + (1.0 - ADAM_B1) * gv
        nv = ADAM_B2 * v_ref[...] + (1.0 - ADAM_B2) * (gv * gv)
        m_hat = nm / (1.0 - ADAM_B1 ** ADAM_STEP)
        v_hat = nv / (1.0 - ADAM_B2 ** ADAM_STEP)
        d_ref[...] = -ADAM_LR * (m_hat / (jnp.sqrt(v_hat) + ADAM_EPS) + ADAM_WD * w_ref[...])
        nm_ref[...] = nm
        nv_ref[...] = nv

    spec = pl.BlockSpec((tr, cols), lambda i: (i, 0))
    shape = jax.ShapeDtypeStruct((rows, cols), F32)
    return pl.pallas_call(
        body, name=name, grid=(rows // tr,), out_shape=(shape, shape, shape),
        in_specs=[spec] * 4, out_specs=(spec, spec, spec), compiler_params=_params(1),
    )(w, g, m, v)


def _ada_rows(c_ref, cc_ref):
    rows = jnp.concatenate([c_ref[...], jnp.broadcast_to(cc_ref[...], c_ref.shape)], axis=0)
    return rows


def _ada_fwd(c_all, c_ctx, w_sh, b_sh):
    n_layers, _, ws = w_sh.shape

    def body(c_ref, cc_ref, w_ref, b_ref, o_ref):
        rows = _ada_rows(c_ref, cc_ref)
        s = rows * _sigmoid(rows)
        for i in range(n_layers):
            o_ref[i] = jnp.dot(s, w_ref[i], precision=HIGHEST, preferred_element_type=F32) + b_ref[i]

    return pl.pallas_call(
        body, name="ada_fwd", out_shape=jax.ShapeDtypeStruct((n_layers, 16, ws), F32),
        compiler_params=pltpu.CompilerParams(vmem_limit_bytes=VMEM_LIMIT),
    )(c_all, c_ctx, w_sh, b_sh)


def _ada_bwd(c_all, c_ctx, w_sh, dm_full, dm_sh):
    n_layers, d_model, ws = w_sh.shape
    n_dev = dm_full.shape[0]
    cols = dm_full.shape[-1]

    def body(c_ref, cc_ref, w_ref, dmf_ref, dms_ref, gw_ref, gb_ref, ds_ref):
        rows = _ada_rows(c_ref, cc_ref)
        s = rows * _sigmoid(rows)
        ds = jnp.zeros((8, d_model), F32)
        for i in range(n_layers):
            ctx_s = dms_ref[0, i, 1:2, :]
            ctx_f = dmf_ref[0, i, 1:2, :]
            ex_f = dmf_ref[0, i, 0:1, :]
            for k in range(1, n_dev):
                ctx_s = ctx_s + dms_ref[k, i, 1:2, :]
                ctx_f = ctx_f + dmf_ref[k, i, 1:2, :]
                ex_f = ex_f + dmf_ref[k, i, 0:1, :]
            gb_ref[i] = ex_f + ctx_f
            r = jnp.concatenate([dms_ref[k, i, 0:1, :] for k in range(n_dev)] + [ctx_s, jnp.zeros((7, ws), F32)], axis=0)
            gw_ref[i] = lax.dot_general(s, r, (((0,), (0,)), ((), ())), precision=HIGHEST, preferred_element_type=F32)
            ds = ds + lax.dot_general(jnp.broadcast_to(ctx_s, (8, ws)), w_ref[i], (((1,), (1,)), ((), ())),
                                      precision=HIGHEST, preferred_element_type=F32)
        ds_ref[...] = ds

    return pl.pallas_call(
        body, name="ada_bwd",
        out_shape=(jax.ShapeDtypeStruct((n_layers, d_model, ws), F32), jax.ShapeDtypeStruct((n_layers, 1, cols), F32),
                   jax.ShapeDtypeStruct((8, d_model), F32)),
        compiler_params=pltpu.CompilerParams(vmem_limit_bytes=VMEM_LIMIT),
    )(c_all, c_ctx, w_sh, dm_full, dm_sh)


def _cctx_grad(ds_parts, c_ctx):
    def body(p_ref, c_ref, o_ref):
        tot = p_ref[0, 0:1, :]
        for k in range(1, ds_parts.shape[0]):
            tot = tot + p_ref[k, 0:1, :]
        cv = c_ref[...]
        sg = _sigmoid(cv)
        o_ref[...] = tot * (sg * (1.0 + cv * (1.0 - sg)))

    return pl.pallas_call(body, name="cctx_grad", out_shape=jax.ShapeDtypeStruct(c_ctx.shape, F32))(ds_parts, c_ctx)


def _zoh_math(lam_re, lam_im, log_step, b_re, b_im):
    dt = jnp.exp(log_step)
    mag = jnp.exp(lam_re * dt)
    ar = mag * jnp.cos(lam_im * dt)
    ai = mag * jnp.sin(lam_im * dt)
    qr, qi = ar - 1.0, ai
    den = lam_re * lam_re + lam_im * lam_im
    fr = (qr * lam_re + qi * lam_im) / den
    fi = (qi * lam_re - qr * lam_im) / den
    bbr = fr[None] * b_re - fi[None] * b_im
    bbi = fr[None] * b_im + fi[None] * b_re
    return ar, ai, bbr, bbi


def _zoh_fwd(lam_re, lam_im, log_step, b_re, b_im):
    rg, n = lam_re.shape

    def body(lr_ref, li_ref, ls_ref, br_ref, bi_ref, pr_ref, pi_ref, qr_ref, qi_ref, bbr_ref, bbi_ref):
        ar, ai, bbr, bbi = _zoh_math(lr_ref[...], li_ref[...], ls_ref[...], br_ref[...], bi_ref[...])
        bbr_ref[...] = bbr
        bbi_ref[...] = bbi

        def powers(base_r, base_i, r_ref, i_ref):
            pr, pi_ = base_r, base_i
            for k in range(8):
                r_ref[k] = pr
                i_ref[k] = pi_
                pr, pi_ = pr * base_r - pi_ * base_i, pr * base_i + pi_ * base_r

        powers(ar, ai, pr_ref, pi_ref)
        powers(pr_ref[7], pi_ref[7], qr_ref, qi_ref)

    pw = jax.ShapeDtypeStruct((8, rg, n), F32)
    bb = jax.ShapeDtypeStruct(b_re.shape, F32)
    return pl.pallas_call(body, name="zoh_fwd", out_shape=(pw, pw, pw, pw, bb, bb))(lam_re, lam_im, log_step, b_re, b_im)


def _zoh_bwd(lam_re, lam_im, log_step, b_re, b_im, dar, dai, dbbr, dbbi):
    def body(lr_ref, li_ref, ls_ref, br_ref, bi_ref, dar_ref, dai_ref, dbr_ref, dbi_ref, *outs):
        _, vjp = jax.vjp(_zoh_math, lr_ref[...], li_ref[...], ls_ref[...], br_ref[...], bi_ref[...])
        grads = vjp((dar_ref[...], dai_ref[...], dbr_ref[...], dbi_ref[...]))
        for o_ref, gval in zip(outs, grads):
            o_ref[...] = gval

    shapes = tuple(jax.ShapeDtypeStruct(a.shape, F32) for a in (lam_re, lam_im, log_step, b_re, b_im))
    return pl.pallas_call(body, name="zoh_bwd", out_shape=shapes)(lam_re, lam_im, log_step, b_re, b_im, dar, dai, dbbr, dbbi)


def _inproj(xin, sc, sh, w, *, lnaff=None, tb, gb_rows=None, name, hosted=None):
    n_tok, d_model = xin.shape
    n_chunks, _, cw = w.shape
    tb = min(tb, n_tok)
    nq = cw // LANES
    has_ln = lnaff is not None
    n_out = 1 if gb_rows is None else len(gb_rows)

    def body(*refs):
        if has_ln:
            x_ref, g_ref, b_ref, sc_ref, sh_ref, w_ref = refs[:6]
        else:
            x_ref, sc_ref, sh_ref, w_ref = refs[:4]
        p_refs, h_ref = refs[-1 - n_out:-1], refs[-1]

        @pl.when(pl.program_id(1) == 0)
        def _():
            xv = x_ref[...]
            if has_ln:
                xv = xv * g_ref[...] + b_ref[...]
            h_ref[...] = (xv * (1.0 + sc_ref[...]) + sh_ref[...]).astype(BF16)

        acc = _dot(h_ref[...], w_ref[0]).astype(BF16)
        if gb_rows is None:
            p_refs[0][0] = acc
        else:
            for p_ref in p_refs:
                for q in range(nq):
                    p_ref[q] = acc[:, q * LANES:(q + 1) * LANES]

    vec = pl.BlockSpec((1, d_model), lambda i, j: (0, 0))
    in_specs = [pl.BlockSpec((tb, d_model), lambda i, j: (i, 0))] + ([vec, vec] if has_ln else []) + [
        vec, vec, pl.BlockSpec((1, d_model, cw), lambda i, j: (j, 0, 0))]
    if gb_rows is None:
        p_shapes = [jax.ShapeDtypeStruct((n_chunks, n_tok, cw), BF16)]
        p_specs = [pl.BlockSpec((1, tb, cw), lambda i, j: (j, i, 0))]
    else:
        p_shapes, p_specs = [], []
        for total, off in gb_rows:
            assert off % tb == 0
            p_shapes.append(jax.ShapeDtypeStruct((n_chunks * nq, total, LANES), BF16))
            p_specs.append(pl.BlockSpec((nq, tb, LANES), functools.partial(lambda i, j, ob: (j, i + ob, 0), ob=off // tb)))
    args = (xin,) + (tuple(lnaff) if has_ln else ()) + (sc, sh, w)
    outs, extra = _call(
        body, name=name, grid=(n_tok // tb, n_chunks), in_specs=in_specs,
        out_specs=[*p_specs, pl.BlockSpec((tb, d_model), lambda i, j: (i, 0))],
        out_shape=[*p_shapes, jax.ShapeDtypeStruct((n_tok, d_model), BF16)], scratch_shapes=[], args=args, hosted=hosted)
    return (*outs, extra) if hosted is not None else tuple(outs)


def _shifted(u, rows, width, tb):
    col = rows % width
    um = jnp.where(col == 0, 0.0, pltpu.roll(u, 1, 0))
    up = jnp.where(col == width - 1, 0.0, pltpu.roll(u, tb - 1, 0))
    return um, up


def _slab_width(e):
    return min(512, e // 2)


def _convgate(p, x, gt, conv_w, w_out, ln_g, ln_b, *, grid_mode, tb, name):
    _, n_tok, e = p.shape
    d_model = x.shape[1]
    eh = e // 2
    if not grid_mode:
        tb = n_tok
    tb = min(tb, n_tok)
    nb = n_tok // tb
    hb = tb // GRID_W
    cs = _slab_width(e)

    def body(*refs):
        if grid_mode:
            (bg_ref, cg_ref, v_ref, z_ref, cgp_ref, vp_ref, cgn_ref, vn_ref, x_ref, gt_ref, cw_ref, wo_ref, lg_ref,
             lb_ref, xh_ref, rs_ref, g_ref, yc_ref, f_ref) = refs
        else:
            (bg_ref, cg_ref, v_ref, z_ref, x_ref, gt_ref, cw_ref, wo_ref, lg_ref, lb_ref, xh_ref, rs_ref, g_ref,
             yc_ref, f_ref) = refs
        i = pl.program_id(0)
        rows = lax.broadcasted_iota(jnp.int32, (tb, 1), 0)
        for c0 in range(0, e, cs):
            sl = slice(c0, c0 + cs)
            u = cg_ref[0, :, sl].astype(F32) * v_ref[0, :, sl].astype(F32)
            w = cw_ref[:, sl]
            if grid_mode and c0 >= eh:
                hs = slice(c0 - eh, c0 - eh + cs)
                uprev = cgp_ref[0, :, hs].astype(F32) * vp_ref[0, :, hs].astype(F32)
                unext = cgn_ref[0, :, hs].astype(F32) * vn_ref[0, :, hs].astype(F32)
                uprev = jnp.where(i > 0, uprev, 0.0)
                unext = jnp.where(i < nb - 1, unext, 0.0)
                if tb > GRID_W:
                    um = jnp.concatenate([uprev, u[:tb - GRID_W]], axis=0)
                    up = jnp.concatenate([u[GRID_W:], unext], axis=0)
                else:
                    um, up = uprev, unext
            else:
                um, up = _shifted(u, rows, GRID_W if grid_mode else tb, tb)
            yc = um * w[0:1] + u * w[1:2] + up * w[2:3]
            zf = z_ref[0, :, sl].astype(F32)
            gval = bg_ref[0, :, sl].astype(F32) * yc * (zf * _sigmoid(zf))
            yc_ref[:, sl] = yc.astype(BF16)
            g_ref[:, sl] = gval.astype(BF16)
        f = _dot(g_ref[...], wo_ref[...])
        f_ref[...] = f.astype(BF16)
        r = DN_ALPHA * x_ref[...] + gt_ref[...] * f
        rc = r - _rowmean(r)
        rstd = lax.rsqrt(_rowmean(rc * rc) + LN_EPS)
        xh_ref[...] = rc * rstd
        rs_ref[...] = rstd

    def chunk(k):
        return pl.BlockSpec((1, tb, e), lambda i: (k, i, 0))

    n_hrows = n_tok // GRID_W

    def halo_prev(k):
        return pl.BlockSpec((1, GRID_W, eh), lambda i: (k, jnp.maximum(i * hb - 1, 0), 1))

    def halo_next(k):
        return pl.BlockSpec((1, GRID_W, eh), lambda i: (k, jnp.minimum((i + 1) * hb, n_hrows - 1), 1))

    vec = pl.BlockSpec((1, d_model), lambda i: (0, 0))
    tok = pl.BlockSpec((tb, d_model), lambda i: (i, 0))
    wide = pl.BlockSpec((tb, e), lambda i: (i, 0))
    in_specs = [chunk(0), chunk(1), chunk(2), chunk(3)]
    args = [p, p, p, p]
    if grid_mode:
        in_specs += [halo_prev(1), halo_prev(2), halo_next(1), halo_next(2)]
        args += [p, p, p, p]
    in_specs += [tok, vec, pl.BlockSpec((3, e), lambda i: (0, 0)), pl.BlockSpec((e, d_model), lambda i: (0, 0)), vec, vec]
    args += [x, gt, conv_w, w_out, ln_g, ln_b]
    return pl.pallas_call(
        body, name=name, grid=(nb,),
        out_shape=(jax.ShapeDtypeStruct((n_tok, d_model), F32), jax.ShapeDtypeStruct((n_tok, 1), F32),
                   jax.ShapeDtypeStruct((n_tok, e), BF16), jax.ShapeDtypeStruct((n_tok, e), BF16),
                   jax.ShapeDtypeStruct((n_tok, d_model), BF16)),
        in_specs=in_specs, out_specs=(tok, pl.BlockSpec((tb, 1), lambda i: (i, 0)), wide, wide, tok),
        compiler_params=_params(1),
    )(*args)


def _scan_block(buf_ref, tab_ref, cr, ci, *, reverse, tb, sb):
    n_slabs = tb // SUBLANES
    unrolled = n_slabs <= 64

    def slab(s, carry):
        cr, ci = carry
        idx = (n_slabs - 1 - s) if reverse else s
        r0 = idx * SUBLANES if unrolled else pl.multiple_of(idx * SUBLANES, SUBLANES)
        xr = buf_ref[pl.ds(r0, SUBLANES), 0:sb]
        xi = buf_ref[pl.ds(r0, SUBLANES), sb:2 * sb]
        for k, step in enumerate((1, 2, 4)):
            ar = tab_ref[2 * k]
            ai = tab_ref[2 * k + 1]
            shift = (SUBLANES - step) if reverse else step
            rr = pltpu.roll(xr, shift, 0)
            ri = pltpu.roll(xi, shift, 0)
            xr, xi = xr + ar * rr - ai * ri, xi + ar * ri + ai * rr
        pr = tab_ref[6]
        pi_ = tab_ref[7]
        xr, xi = xr + pr * cr - pi_ * ci, xi + pr * ci + pi_ * cr
        buf_ref[pl.ds(r0, SUBLANES), 0:sb] = xr
        buf_ref[pl.ds(r0, SUBLANES), sb:2 * sb] = xi
        last = 0 if reverse else SUBLANES - 1
        return (jnp.broadcast_to(xr[last:last + 1, :], (SUBLANES, sb)),
                jnp.broadcast_to(xi[last:last + 1, :], (SUBLANES, sb)))

    if unrolled:
        carry = (cr, ci)
        for s in range(n_slabs):
            carry = slab(s, carry)
        return carry
    return lax.fori_loop(0, n_slabs, slab, (cr, ci))


CHUNK = SUBLANES


def _group_mask():
    r = lax.broadcasted_iota(jnp.int32, (LANES, LANES), 0)
    c = lax.broadcasted_iota(jnp.int32, (LANES, LANES), 1)
    return r // 16 == c // 16


def _s5_ops(bre, bim, cre, cim, wr, wi, *, reverse, name, hosted=None):
    ngb, _, sb = bre.shape
    n_rows = CHUNK * LANES

    def body(bre_ref, bim_ref, cre_ref, cim_ref, wr_ref, wi_ref, t_ref, bp_ref, cp_ref):
        b_re, b_im, c_re, c_im = bre_ref[0], bim_ref[0], cre_ref[0], cim_ref[0]
        mask = _group_mask()
        er, ei = [], []
        for tau in range(CHUNK + 1):
            w_r, w_i = wr_ref[0, tau:tau + 1, :], wi_ref[0, tau:tau + 1, :]
            er.append(c_re * w_r - c_im * w_i)
            ei.append(c_re * w_i + c_im * w_r)
        kt = []
        for tau in range(CHUNK):
            k = (lax.dot_general(b_re, er[tau], (((1,), (1,)), ((), ())), precision=HIGHEST, preferred_element_type=F32)
                 - lax.dot_general(b_im, ei[tau], (((1,), (1,)), ((), ())), precision=HIGHEST, preferred_element_type=F32))
            kt.append(jnp.where(mask, k, 0.0).astype(BF16))
        zero = jnp.zeros((LANES, LANES), BF16)
        for i in range(CHUNK):
            rows = slice(i * LANES, (i + 1) * LANES)
            for j in range(CHUNK):
                lag = (i - j) if reverse else (j - i)
                t_ref[0, rows, j * LANES:(j + 1) * LANES] = kt[lag] if lag >= 0 else zero
            tau = i if reverse else CHUNK - 1 - i
            w_r, w_i = wr_ref[0, tau:tau + 1, :], wi_ref[0, tau:tau + 1, :]
            bp_ref[0, rows, 0:sb] = (b_re * w_r - b_im * w_i).astype(BF16)
            bp_ref[0, rows, sb:2 * sb] = (b_re * w_i + b_im * w_r).astype(BF16)
            tau = CHUNK - i if reverse else i + 1
            cp_ref[0, rows, 0:sb] = er[tau].astype(BF16)
            cp_ref[0, rows, sb:2 * sb] = (-ei[tau]).astype(BF16)

    mat = pl.BlockSpec((1, LANES, sb), lambda g: (g, 0, 0))
    pw = pl.BlockSpec((1, CHUNK + 1, sb), lambda g: (g, 0, 0))
    outs, extra = _call(
        body, name=name, grid=(ngb,),
        out_shape=[jax.ShapeDtypeStruct((ngb, n_rows, n_rows), BF16), jax.ShapeDtypeStruct((ngb, n_rows, 2 * sb), BF16),
                   jax.ShapeDtypeStruct((ngb, n_rows, 2 * sb), BF16)],
        in_specs=[mat, mat, mat, mat, pw, pw],
        out_specs=[pl.BlockSpec((1, n_rows, n_rows), lambda g: (g, 0, 0)),
                   pl.BlockSpec((1, n_rows, 2 * sb), lambda g: (g, 0, 0)),
                   pl.BlockSpec((1, n_rows, 2 * sb), lambda g: (g, 0, 0))],
        scratch_shapes=[], args=(bre, bim, cre, cim, wr, wi), hosted=hosted)
    return (*outs, extra)


def _s5_ops_bwd(bre, bim, cre, cim, wr, wi, ar, ai, dt, dbp, dcp, da8, *, reverse, name, hosted=None):
    ngb, _, sb = bre.shape
    n_rows = CHUNK * LANES

    def dot_hi(a, b, dims):
        return lax.dot_general(a, b, (dims, ((), ())), precision=HIGHEST, preferred_element_type=F32)

    def body(bre_ref, bim_ref, cre_ref, cim_ref, wr_ref, wi_ref, ar_ref, ai_ref, dt_ref, dbp_ref, dcp_ref, da8_ref,
             dbre_ref, dbim_ref, dcre_ref, dcim_ref, da_ref):
        b_re, b_im, c_re, c_im = bre_ref[0], bim_ref[0], cre_ref[0], cim_ref[0]
        mask = _group_mask()
        w_r = [wr_ref[0, tau:tau + 1, :] for tau in range(CHUNK + 1)]
        w_i = [wi_ref[0, tau:tau + 1, :] for tau in range(CHUNK + 1)]
        der = [jnp.zeros((LANES, sb), F32) for _ in range(CHUNK + 1)]
        dei = [jnp.zeros((LANES, sb), F32) for _ in range(CHUNK + 1)]
        dwr = [jnp.zeros((1, sb), F32) for _ in range(CHUNK + 1)]
        dwi = [jnp.zeros((1, sb), F32) for _ in range(CHUNK + 1)]
        dwr[CHUNK] = da8_ref[0, :, 0:sb]
        dwi[CHUNK] = da8_ref[0, :, sb:2 * sb]
        d_bre = jnp.zeros((LANES, sb), F32)
        d_bim = jnp.zeros((LANES, sb), F32)
        dkt = [jnp.zeros((LANES, LANES), F32) for _ in range(CHUNK)]
        for i in range(CHUNK):
            rows = slice(i * LANES, (i + 1) * LANES)
            for j in range(CHUNK):
                lag = (i - j) if reverse else (j - i)
                if lag >= 0:
                    dkt[lag] = dkt[lag] + dt_ref[0, rows, j * LANES:(j + 1) * LANES]
            tau = i if reverse else CHUNK - 1 - i
            g_r, g_i = dbp_ref[0, rows, 0:sb], dbp_ref[0, rows, sb:2 * sb]
            d_bre = d_bre + g_r * w_r[tau] + g_i * w_i[tau]
            d_bim = d_bim - g_r * w_i[tau] + g_i * w_r[tau]
            dwr[tau] = dwr[tau] + _colsum(g_r * b_re + g_i * b_im)
            dwi[tau] = dwi[tau] + _colsum(g_i * b_re - g_r * b_im)
            tau = CHUNK - i if reverse else i + 1
            der[tau] = der[tau] + dcp_ref[0, rows, 0:sb]
            dei[tau] = dei[tau] - dcp_ref[0, rows, sb:2 * sb]
        d_cre = jnp.zeros((LANES, sb), F32)
        d_cim = jnp.zeros((LANES, sb), F32)
        for tau in range(CHUNK + 1):
            if tau < CHUNK:
                e_r = c_re * w_r[tau] - c_im * w_i[tau]
                e_i = c_re * w_i[tau] + c_im * w_r[tau]
                dk = jnp.where(mask, dkt[tau], 0.0)
                d_bre = d_bre + dot_hi(dk, e_r, ((1,), (0,)))
                d_bim = d_bim - dot_hi(dk, e_i, ((1,), (0,)))
                der[tau] = der[tau] + dot_hi(dk, b_re, ((0,), (0,)))
                dei[tau] = dei[tau] - dot_hi(dk, b_im, ((0,), (0,)))
            d_cre = d_cre + der[tau] * w_r[tau] + dei[tau] * w_i[tau]
            d_cim = d_cim - der[tau] * w_i[tau] + dei[tau] * w_r[tau]
            dwr[tau] = dwr[tau] + _colsum(der[tau] * c_re + dei[tau] * c_im)
            dwi[tau] = dwi[tau] + _colsum(dei[tau] * c_re - der[tau] * c_im)
        a_r, a_i = ar_ref[0], ai_ref[0]
        d_ar = jnp.zeros((1, sb), F32)
        d_ai = jnp.zeros((1, sb), F32)
        for tau in range(CHUNK, 0, -1):
            d_ar = d_ar + dwr[tau] * w_r[tau - 1] + dwi[tau] * w_i[tau - 1]
            d_ai = d_ai - dwr[tau] * w_i[tau - 1] + dwi[tau] * w_r[tau - 1]
            dwr[tau - 1], dwi[tau - 1] = (dwr[tau - 1] + dwr[tau] * a_r + dwi[tau] * a_i,
                                          dwi[tau - 1] - dwr[tau] * a_i + dwi[tau] * a_r)
        dbre_ref[0] = d_bre
        dbim_ref[0] = d_bim
        dcre_ref[0] = d_cre
        dcim_ref[0] = d_cim
        da_ref[0, :, 0:sb] = d_ar
        da_ref[0, :, sb:2 * sb] = d_ai

    mat = pl.BlockSpec((1, LANES, sb), lambda g: (g, 0, 0))
    pw = pl.BlockSpec((1, CHUNK + 1, sb), lambda g: (g, 0, 0))
    one = pl.BlockSpec((1, 1, sb), lambda g: (g, 0, 0))
    two = pl.BlockSpec((1, 1, 2 * sb), lambda g: (g, 0, 0))
    big = pl.BlockSpec((1, n_rows, n_rows), lambda g: (g, 0, 0))
    big2 = pl.BlockSpec((1, n_rows, 2 * sb), lambda g: (g, 0, 0))
    mshape = jax.ShapeDtypeStruct((ngb, LANES, sb), F32)
    outs, extra = _call(
        body, name=name, grid=(ngb,),
        out_shape=[mshape, mshape, mshape, mshape, jax.ShapeDtypeStruct((ngb, 1, 2 * sb), F32)],
        in_specs=[mat, mat, mat, mat, pw, pw, one, one, big, big2, big2, two],
        out_specs=[mat, mat, mat, mat, two],
        scratch_shapes=[], args=(bre, bim, cre, cim, wr, wi, ar, ai, dt, dbp, dcp, da8), hosted=hosted)
    return (*outs, extra)


def _shift_rows(xv, edge, rows, n_rows, down):
    if down:
        return jnp.where(rows == 0, edge, pltpu.roll(xv, 1, 0))
    return jnp.where(rows == n_rows - 1, edge, pltpu.roll(xv, n_rows - 1, 0))


def _rows_of_tokens(tok_ref, conv_scr, rb):
    conv_scr[...] = tok_ref[0].astype(F32)
    return jnp.concatenate([conv_scr[pl.ds(j, rb, stride=CHUNK), :] for j in range(CHUNK)], axis=1).astype(BF16)


def _tokens_of_rows(val, tok_ref, conv_scr, rb):
    for j in range(CHUNK):
        conv_scr[pl.ds(j, rb, stride=CHUNK), :] = val[:, j * LANES:(j + 1) * LANES]
    tok_ref[0] = conv_scr[...].astype(BF16)


def _s5_row_block(n_seq, target=416):
    n_rows = n_seq // CHUNK
    best = 16
    for rb in range(16, min(target, n_rows) + 1, 16):
        if n_rows % rb == 0:
            best = rb
    assert n_rows % best == 0
    return best


def _s5_fwd(useq, t_op, bp, cp, tab, *, reverse, name):
    ngb, n_seq, _ = useq.shape
    sb = bp.shape[2] // 2
    width = CHUNK * LANES
    rb = _s5_row_block(n_seq)
    tbk = rb * CHUNK
    steps = n_seq // tbk

    def blk(i):
        return (steps - 1 - i) if reverse else i

    def body(u_ref, t_ref, b_ref, c_ref, tab_ref, y_ref, hp_ref, h_scr, conv_scr, carry_scr):
        i = pl.program_id(1)

        @pl.when(i == 0)
        def _():
            carry_scr[...] = jnp.zeros_like(carry_scr)

        enter = carry_scr[0:1, :]
        uv = _rows_of_tokens(u_ref, conv_scr, rb)
        h_scr[...] = _dot(uv, b_ref[0])
        cr, ci = _scan_block(h_scr, tab_ref.at[0], carry_scr[:, 0:sb], carry_scr[:, sb:2 * sb],
                             reverse=reverse, tb=rb, sb=sb)
        carry_scr[:, 0:sb] = cr
        carry_scr[:, sb:2 * sb] = ci
        rows = lax.broadcasted_iota(jnp.int32, (rb, 1), 0)
        hprev = _shift_rows(h_scr[...], enter, rows, rb, down=not reverse)
        hp_ref[0] = hprev
        _tokens_of_rows(_dot(uv, t_ref[0]) + _dot_nt(hprev.astype(BF16), c_ref[0]), y_ref, conv_scr, rb)

    op = pl.BlockSpec((1, width, width), lambda g, i: (g, 0, 0))
    op2 = pl.BlockSpec((1, width, 2 * sb), lambda g, i: (g, 0, 0))
    tok = pl.BlockSpec((1, tbk, LANES), lambda g, i: (g, blk(i), 0))
    return pl.pallas_call(
        body, name=name, grid=(ngb, steps),
        out_shape=(jax.ShapeDtypeStruct((ngb, n_seq, LANES), BF16),
                   jax.ShapeDtypeStruct((ngb, n_seq // CHUNK, 2 * sb), F32)),
        in_specs=[tok, op, op2, op2, pl.BlockSpec((1, 8, SUBLANES, sb), lambda g, i: (g, 0, 0, 0))],
        out_specs=(tok, pl.BlockSpec((1, rb, 2 * sb), lambda g, i: (g, blk(i), 0))),
        scratch_shapes=[pltpu.VMEM((rb, 2 * sb), F32), pltpu.VMEM((tbk, LANES), F32),
                        pltpu.VMEM((SUBLANES, 2 * sb), F32)],
        compiler_params=_params(2),
    )(useq, t_op, bp, cp, tab)


def _s5_bwd(useq, dy, hprev, t_op, bp, cp, tab_adj, *, reverse, name, hosted=None):
    ngb, n_seq, _ = useq.shape
    sb = bp.shape[2] // 2
    width = CHUNK * LANES
    rb = _s5_row_block(n_seq)
    tbk = rb * CHUNK
    steps = n_seq // tbk

    def fwd_step(i):
        return steps - 1 - i

    def blk(i):
        s = fwd_step(i)
        return (steps - 1 - s) if reverse else s

    def body(u_ref, dy_ref, hp_ref, t_ref, b_ref, c_ref, taba_ref, du_ref, dt_ref, db_ref, dc_ref, da_ref,
             lam_scr, conv_scr, lcarry_scr, gedge_scr, da_scr):
        i = pl.program_id(1)
        first = i == 0

        @pl.when(first)
        def _():
            lcarry_scr[...] = jnp.zeros_like(lcarry_scr)
            gedge_scr[...] = jnp.zeros_like(gedge_scr)
            da_scr[...] = jnp.zeros_like(da_scr)

        rows = lax.broadcasted_iota(jnp.int32, (rb, 1), 0)
        uv = _rows_of_tokens(u_ref, conv_scr, rb)
        dyv = _rows_of_tokens(dy_ref, conv_scr, rb)
        gy = _dot(dyv, c_ref[0])
        edge = gy[rb - 1:rb, :] if reverse else gy[0:1, :]
        lam_scr[...] = _shift_rows(gy, gedge_scr[...], rows, rb, down=reverse)
        gedge_scr[...] = edge
        lr, li = _scan_block(lam_scr, taba_ref.at[0], lcarry_scr[:, 0:sb], lcarry_scr[:, sb:2 * sb],
                             reverse=not reverse, tb=rb, sb=sb)
        lcarry_scr[:, 0:sb] = lr
        lcarry_scr[:, sb:2 * sb] = li

        lam = lam_scr[...]
        lam_bf = lam.astype(BF16)
        _tokens_of_rows(_dot_nt(dyv, t_ref[0]) + _dot_nt(lam_bf, b_ref[0]), du_ref, conv_scr, rb)
        _acc(dt_ref.at[0], first, _dot_tn(uv, dyv))
        _acc(db_ref.at[0], first, _dot_tn(uv, lam_bf))
        _acc(dc_ref.at[0], first, _dot_tn(dyv, hp_ref[0].astype(BF16)))
        lam_r, lam_i = lam[:, 0:sb], lam[:, sb:2 * sb]
        hp_r, hp_i = hp_ref[0, :, 0:sb], hp_ref[0, :, sb:2 * sb]
        da_scr[:, 0:sb] += _colsum(lam_r * hp_r + lam_i * hp_i)
        da_scr[:, sb:2 * sb] += _colsum(lam_i * hp_r - lam_r * hp_i)

        @pl.when(i == steps - 1)
        def _():
            da_ref[0] = da_scr[...]

    op = pl.BlockSpec((1, width, width), lambda g, i: (g, 0, 0))
    op2 = pl.BlockSpec((1, width, 2 * sb), lambda g, i: (g, 0, 0))
    tabs = pl.BlockSpec((1, 8, SUBLANES, sb), lambda g, i: (g, 0, 0, 0))
    tok = pl.BlockSpec((1, tbk, LANES), lambda g, i: (g, blk(i), 0))
    outs, extra = _call(
        body, name=name, grid=(ngb, steps),
        out_shape=[jax.ShapeDtypeStruct((ngb, n_seq, LANES), BF16),
                   jax.ShapeDtypeStruct((ngb, width, width), F32),
                   jax.ShapeDtypeStruct((ngb, width, 2 * sb), F32),
                   jax.ShapeDtypeStruct((ngb, width, 2 * sb), F32),
                   jax.ShapeDtypeStruct((ngb, 1, 2 * sb), F32)],
        in_specs=[tok, tok, pl.BlockSpec((1, rb, 2 * sb), lambda g, i: (g, blk(i), 0)), op, op2, op2, tabs],
        out_specs=[tok, op, op2, op2, pl.BlockSpec((1, 1, 2 * sb), lambda g, i: (g, 0, 0))],
        scratch_shapes=[pltpu.VMEM((rb, 2 * sb), F32), pltpu.VMEM((tbk, LANES), F32),
                        pltpu.VMEM((SUBLANES, 2 * sb), F32), pltpu.VMEM((1, 2 * sb), F32), pltpu.VMEM((1, 2 * sb), F32)],
        args=(useq, dy, hprev, t_op, bp, cp, tab_adj), hosted=hosted)
    return (*outs, extra)


def _glu_loss(useq, yf, yb, z, xhat0, ln0, gt, d_vec, w_glu, b_glu, w_out, ln1, target, *, offs, dy_rows, tb, name):
    ngb = useq.shape[0]
    n_tok, d_model = xhat0.shape
    e = ngb * LANES
    tb = min(tb, n_tok)
    assert all(off % tb == 0 for off in offs) and all(off % tb == 0 for _, off in dy_rows)
    nz = z.shape[0]

    def body(u_ref, yf_ref, yb_ref, z_ref, xh0_ref, g0_ref, b0_ref, gt_ref, d_ref, wg_ref, bg_ref, wo_ref, g1_ref,
             b1_ref, t_ref, loss_ref, dxr_ref, do_ref, gz_ref, gg_ref, dq_ref, dz_ref, dyf_ref, dyb_ref, dg1_ref, db1_ref,
             dgt_ref, dbg_ref, dd_ref, loss_scr, yl_scr, th_scr, s_scr, dg_scr):
        i = pl.program_id(0)
        first = i == 0
        zw = e // nz
        cs = min(512, zw)

        def z_slab(c0):
            return z_ref[c0 // zw, :, c0 % zw:c0 % zw + cs].astype(F32)

        for q in range(ngb):
            sl = slice(q * LANES, (q + 1) * LANES)
            yl = d_ref[:, sl] * u_ref[q].astype(F32) + yf_ref[q].astype(F32) + yb_ref[q].astype(F32)
            th = jnp.tanh(GELU_K * (yl + GELU_C * yl * yl * yl))
            yl_scr[:, sl] = yl
            th_scr[:, sl] = th
            gg_ref[:, sl] = (0.5 * yl * (1.0 + th)).astype(BF16)
        s_scr[...] = _sigmoid(_dot(gg_ref[...], wg_ref[...]) + bg_ref[...])
        for c0 in range(0, e, cs):
            sl = slice(c0, c0 + cs)
            zf = z_slab(c0)
            g2 = 0.5 * yl_scr[:, sl] * (1.0 + th_scr[:, sl]) * s_scr[:, sl]
            gz_ref[:, sl] = (g2 * (zf * _sigmoid(zf))).astype(BF16)
        o = _dot(gz_ref[...], wo_ref[...])
        x1 = xh0_ref[...] * g0_ref[...] + b0_ref[...]
        r = DN_ALPHA * x1 + gt_ref[...] * o
        rc = r - _rowmean(r)
        rstd = lax.rsqrt(_rowmean(rc * rc) + LN_EPS)
        xh = rc * rstd
        err = xh * g1_ref[...] + b1_ref[...] - t_ref[...]
        _acc(loss_scr, first, _colsum(err * err))
        dy = err * (1.0 / d_model)
        _acc(dg1_ref, first, _colsum(dy * xh))
        _acc(db1_ref, first, _colsum(dy))
        dxh = dy * g1_ref[...]
        dr = rstd * (dxh - _rowmean(dxh) - xh * _rowmean(dxh * xh))
        dxr_ref[...] = DN_ALPHA * dr
        _acc(dgt_ref, first, _colsum(dr * o))
        do_bf = (dr * gt_ref[...]).astype(BF16)
        do_ref[...] = do_bf
        dg_scr[...] = _dot_nt(do_bf, wo_ref[...])
        for c0 in range(0, e, cs):
            sl = slice(c0, c0 + cs)
            zf = z_slab(c0)
            sz = _sigmoid(zf)
            g = 0.5 * yl_scr[:, sl] * (1.0 + th_scr[:, sl])
            s = s_scr[:, sl]
            dgz = dg_scr[:, sl]
            dg2 = dgz * (zf * sz)
            dz_ref[:, sl] = (dgz * (g * s) * (sz * (1.0 + zf * (1.0 - sz)))).astype(BF16)
            dq = dg2 * g * s * (1.0 - s)
            _acc(dbg_ref.at[:, sl], first, _colsum(dq))
            dq_ref[:, sl] = dq.astype(BF16)
            dg_scr[:, sl] = dg2 * s
        dg_scr[...] += _dot_nt(dq_ref[...], wg_ref[...])
        for q in range(ngb):
            sl = slice(q * LANES, (q + 1) * LANES)
            yl = yl_scr[:, sl]
            th = th_scr[:, sl]
            dgelu = 0.5 * (1.0 + th) + 0.5 * yl * (1.0 - th * th) * (GELU_K * (1.0 + 3.0 * GELU_C * yl * yl))
            dyl = dg_scr[:, sl] * dgelu
            _acc(dd_ref.at[:, sl], first, _colsum(dyl * u_ref[q].astype(F32)))
            dyf_ref[q] = dyl.astype(BF16)
            dyb_ref[q] = dyl.astype(BF16)

        @pl.when(i == pl.num_programs(0) - 1)
        def _():
            loss_ref[...] = (0.5 / d_model) * jnp.sum(loss_scr[...], axis=1, keepdims=True)

    vec = pl.BlockSpec((1, d_model), lambda i: (0, 0))
    evec = pl.BlockSpec((1, e), lambda i: (0, 0))
    tok = pl.BlockSpec((tb, d_model), lambda i: (i, 0))
    wide = pl.BlockSpec((tb, e), lambda i: (i, 0))
    def gblk(off):
        return pl.BlockSpec((ngb, tb, LANES), functools.partial(lambda i, ob: (0, i + ob, 0), ob=off // tb))

    once = dict(pipeline_mode=pl.Buffered(1))
    tok_f = jax.ShapeDtypeStruct((n_tok, d_model), F32)
    tok_b = jax.ShapeDtypeStruct((n_tok, d_model), BF16)
    wide_b = jax.ShapeDtypeStruct((n_tok, e), BF16)
    vec_f = jax.ShapeDtypeStruct((1, d_model), F32)
    evec_f = jax.ShapeDtypeStruct((1, e), F32)
    return pl.pallas_call(
        body, name=name, grid=(n_tok // tb,),
        out_shape=(jax.ShapeDtypeStruct((1, 1), F32), tok_f, tok_b, wide_b, wide_b, wide_b, wide_b,
                   *[jax.ShapeDtypeStruct((ngb, total, LANES), BF16) for total, _ in dy_rows],
                   vec_f, vec_f, vec_f, evec_f, evec_f),
        in_specs=[gblk(offs[0]), gblk(offs[1]), gblk(offs[2]),
                  pl.BlockSpec((nz, tb, e // nz), lambda i: (0, i, 0)), tok, vec, vec, vec, evec,
                  pl.BlockSpec((e, e), lambda i: (0, 0), **once), evec,
                  pl.BlockSpec((e, d_model), lambda i: (0, 0), **once), vec, vec, tok],
        out_specs=(pl.BlockSpec((1, 1), lambda i: (0, 0)), tok, tok, wide, wide, wide, wide,
                   *[gblk(off) for _, off in dy_rows], vec, vec, vec, evec, evec),
        scratch_shapes=[pltpu.VMEM((1, d_model), F32)] + [pltpu.VMEM((tb, e), F32)] * 4,
        compiler_params=_params(1),
    )(useq, yf, yb, z, xhat0, ln0[0], ln0[1], gt, d_vec, w_glu, b_glu, w_out, ln1[0], ln1[1], target)


def _ssm_inbwd(duf, dub, w, xhat, rstd, ln, sc, gt_prev, f_prev, *, lat, row_f, row_b, tb, name):
    ngb = duf.shape[0]
    e = ngb * LANES
    n_tok, d_model = xhat.shape
    tb = min(tb, n_tok)
    obf, obb = row_f // tb, row_b // tb
    has_lat = lat is not None
    n_w = w.shape[0] if has_lat else w.shape[0] // 2

    def body(*refs):
        if has_lat:
            (duf_ref, dub_ref, dyl_ref, dz_ref, d_ref, dxr_ref, w_ref, xh_ref, rs_ref, g_ref, b_ref, sc_ref, gt_ref,
             f_ref, dp_ref, dr_ref, df_ref, dsc_ref, dsh_ref, dg_ref, db_ref, dgt_ref) = refs
        else:
            (duf_ref, dub_ref, w_ref, xh_ref, rs_ref, g_ref, b_ref, sc_ref, gt_ref, f_ref, dp_ref, dr_ref, df_ref,
             dsc_ref, dsh_ref, dg_ref, db_ref, dgt_ref) = refs
        first = pl.program_id(0) == 0
        du = (jnp.concatenate([duf_ref[q] for q in range(ngb)], axis=1).astype(F32)
              + jnp.concatenate([dub_ref[q] for q in range(ngb)], axis=1).astype(F32))
        if has_lat:
            du = du + d_ref[...] * jnp.concatenate([dyl_ref[q] for q in range(ngb)], axis=1).astype(F32)
            dp_ref[:, e:2 * e] = dz_ref[...]
        else:
            dp_ref[:, e:2 * e] = jnp.zeros((tb, e), BF16)
        dp_ref[:, 0:e] = du.astype(BF16)
        dh = jnp.zeros((tb, d_model), F32)
        for j in range(n_w):
            dh = dh + _dot(dp_ref[:, j * d_model:(j + 1) * d_model], w_ref[j])
        xh = xh_ref[...]
        x1 = xh * g_ref[...] + b_ref[...]
        dx1 = dh * (1.0 + sc_ref[...])
        if has_lat:
            dx1 = dx1 + dxr_ref[...]
        _acc(dsc_ref, first, _colsum(dh * x1))
        _acc(dsh_ref, first, _colsum(dh))
        _acc(dg_ref, first, _colsum(dx1 * xh))
        _acc(db_ref, first, _colsum(dx1))
        dxh = dx1 * g_ref[...]
        dr = rs_ref[...] * (dxh - _rowmean(dxh) - xh * _rowmean(dxh * xh))
        dr_ref[...] = dr
        df_ref[...] = (dr * gt_ref[...]).astype(BF16)
        _acc(dgt_ref, first, _colsum(dr * f_ref[...].astype(F32)))

    vec = pl.BlockSpec((1, d_model), lambda i: (0, 0))
    tok = pl.BlockSpec((tb, d_model), lambda i: (i, 0))
    gblk = pl.BlockSpec((ngb, tb, LANES), lambda i: (0, i, 0))
    in_specs = [pl.BlockSpec((ngb, tb, LANES), lambda i: (0, i + obf, 0)),
                pl.BlockSpec((ngb, tb, LANES), lambda i: (0, i + obb, 0))]
    args = [duf, dub]
    if has_lat:
        in_specs += [gblk, pl.BlockSpec((tb, e), lambda i: (i, 0)), pl.BlockSpec((1, e), lambda i: (0, 0)), tok]
        args += list(lat)
    in_specs += [pl.BlockSpec(w.shape, lambda i: (0, 0, 0)), tok, pl.BlockSpec((tb, 1), lambda i: (i, 0)), vec, vec, vec,
                 vec, tok]
    args += [w, xhat, rstd, ln[0], ln[1], sc, gt_prev, f_prev]
    vec_f = jax.ShapeDtypeStruct((1, d_model), F32)
    return pl.pallas_call(
        body, name=name, grid=(n_tok // tb,),
        out_shape=(jax.ShapeDtypeStruct((n_tok, 2 * e), BF16), jax.ShapeDtypeStruct((n_tok, d_model), F32),
                   jax.ShapeDtypeStruct((n_tok, d_model), BF16), vec_f, vec_f, vec_f, vec_f, vec_f),
        in_specs=in_specs,
        out_specs=(pl.BlockSpec((tb, 2 * e), lambda i: (i, 0)), tok, tok, vec, vec, vec, vec, vec),
        compiler_params=_params(1),
    )(*args)


def _conv_bwd_a(df, w_out_t, p, yc, *, tb, name):
    _, n_tok, e = p.shape
    d_model = df.shape[1]
    tb = min(tb, n_tok)
    cs = _slab_width(e)

    def body(df_ref, wo_ref, bg_ref, z_ref, yc_ref, dbg_ref, dz_ref, dyc_ref):
        dfv = df_ref[...]
        for c0 in range(0, e, cs):
            sl = slice(c0, c0 + cs)
            dgv = _dot(dfv, wo_ref[:, sl])
            zf = z_ref[0, :, sl].astype(F32)
            sz = _sigmoid(zf)
            silu_z = zf * sz
            bg = bg_ref[0, :, sl].astype(F32)
            yc = yc_ref[:, sl].astype(F32)
            dbg_ref[:, sl] = (dgv * yc * silu_z).astype(BF16)
            dyc_ref[:, sl] = (dgv * bg * silu_z).astype(BF16)
            dz_ref[:, sl] = (dgv * bg * yc * (sz * (1.0 + zf * (1.0 - sz)))).astype(BF16)

    wide = pl.BlockSpec((tb, e), lambda i: (i, 0))
    shape = jax.ShapeDtypeStruct((n_tok, e), BF16)
    return pl.pallas_call(
        body, name=name, grid=(n_tok // tb,), out_shape=(shape, shape, shape),
        in_specs=[pl.BlockSpec((tb, d_model), lambda i: (i, 0)), pl.BlockSpec((d_model, e), lambda i: (0, 0)),
                  pl.BlockSpec((1, tb, e), lambda i: (0, i, 0)), pl.BlockSpec((1, tb, e), lambda i: (3, i, 0)), wide],
        out_specs=(wide, wide, wide), compiler_params=_params(1),
    )(df, w_out_t, p, p, yc)


def _conv_bwd_b(dyc, p, dbg, dz, conv_w, *, grid_mode, tb, name, hosted=None):
    _, n_tok, e = p.shape
    eh = e // 2
    if not grid_mode:
        tb = n_tok
    tb = min(tb, n_tok)
    nb = n_tok // tb
    hb = tb // GRID_W
    cs = _slab_width(e)

    def body(*refs):
        if grid_mode:
            dyc_ref, dycp_ref, dycn_ref, cg_ref, v_ref, dbg_ref, dz_ref, cw_ref, dp_ref, dcw_ref = refs
        else:
            dyc_ref, cg_ref, v_ref, dbg_ref, dz_ref, cw_ref, dp_ref, dcw_ref = refs
        i = pl.program_id(0)
        first = i == 0
        rows = lax.broadcasted_iota(jnp.int32, (tb, 1), 0)
        dp_ref[0] = dbg_ref[...]
        dp_ref[3] = dz_ref[...]
        for c0 in range(0, e, cs):
            sl = slice(c0, c0 + cs)
            dyc = dyc_ref[:, sl].astype(F32)
            w = cw_ref[:, sl]
            if grid_mode and c0 >= eh:
                hs = slice(c0 - eh, c0 - eh + cs)
                dprev = jnp.where(i > 0, dycp_ref[:, hs].astype(F32), 0.0)
                dnext = jnp.where(i < nb - 1, dycn_ref[:, hs].astype(F32), 0.0)
                if tb > GRID_W:
                    dm = jnp.concatenate([dprev, dyc[:tb - GRID_W]], axis=0)
                    dpl = jnp.concatenate([dyc[GRID_W:], dnext], axis=0)
                else:
                    dm, dpl = dprev, dnext
            else:
                dm, dpl = _shifted(dyc, rows, GRID_W if grid_mode else tb, tb)
            cg = cg_ref[0, :, sl].astype(F32)
            v = v_ref[0, :, sl].astype(F32)
            u = cg * v
            du = w[0:1] * dpl + w[1:2] * dyc + w[2:3] * dm
            dp_ref[1, :, sl] = (du * v).astype(BF16)
            dp_ref[2, :, sl] = (du * cg).astype(BF16)
            _acc(dcw_ref.at[:, sl], first, jnp.concatenate([_colsum(u * dpl), _colsum(u * dyc), _colsum(u * dm)], axis=0))

    n_hrows = n_tok // GRID_W
    wide = pl.BlockSpec((tb, e), lambda i: (i, 0))
    in_specs = [wide]
    args = [dyc]
    if grid_mode:
        in_specs += [pl.BlockSpec((GRID_W, eh), lambda i: (jnp.maximum(i * hb - 1, 0), 1)),
                     pl.BlockSpec((GRID_W, eh), lambda i: (jnp.minimum((i + 1) * hb, n_hrows - 1), 1))]
        args += [dyc, dyc]
    in_specs += [pl.BlockSpec((1, tb, e), lambda i: (1, i, 0)), pl.BlockSpec((1, tb, e), lambda i: (2, i, 0)), wide, wide,
                 pl.BlockSpec((3, e), lambda i: (0, 0))]
    args += [p, p, dbg, dz, conv_w]
    outs, extra = _call(
        body, name=name, grid=(nb,),
        out_shape=[jax.ShapeDtypeStruct((4, n_tok, e), BF16), jax.ShapeDtypeStruct((3, e), F32)],
        in_specs=in_specs,
        out_specs=[pl.BlockSpec((4, tb, e), lambda i: (0, i, 0)), pl.BlockSpec((3, e), lambda i: (0, 0))],
        scratch_shapes=[], args=args, hosted=hosted)
    return (*outs, extra)


def _conv_inbwd(dp, w, dr, x, sc, *, tb, name, hosted=None):
    n_chunks, n_tok, e = dp.shape
    d_model = x.shape[1]
    tb = min(tb, n_tok)

    def body(dp_ref, w_ref, dr_ref, x_ref, sc_ref, gx_ref, dsc_ref, dsh_ref):
        first = pl.program_id(0) == 0
        dh = _dot(dp_ref[0], w_ref[0])
        for k in range(1, n_chunks):
            dh = dh + _dot(dp_ref[k], w_ref[k])
        gx_ref[...] = DN_ALPHA * dr_ref[...] + dh * (1.0 + sc_ref[...])
        _acc(dsc_ref, first, _colsum(dh * x_ref[...]))
        _acc(dsh_ref, first, _colsum(dh))

    vec = pl.BlockSpec((1, d_model), lambda i: (0, 0))
    tok = pl.BlockSpec((tb, d_model), lambda i: (i, 0))
    vec_f = jax.ShapeDtypeStruct((1, d_model), F32)
    outs, extra = _call(
        body, name=name, grid=(n_tok // tb,),
        out_shape=[jax.ShapeDtypeStruct((n_tok, d_model), F32), vec_f, vec_f],
        in_specs=[pl.BlockSpec((n_chunks, tb, e), lambda i: (0, i, 0)),
                  pl.BlockSpec((n_chunks, e, d_model), lambda i: (0, 0, 0), pipeline_mode=pl.Buffered(1)),
                  tok, tok, vec],
        out_specs=[tok, vec, vec],
        scratch_shapes=[], args=(dp, w, dr, x, sc), hosted=hosted)
    return (*outs, extra)


def _wgrad(a, b, *, n_chunks, tm, tl, init=None, name):
    n_tok, m = a.shape
    tl = min(tl, n_tok)
    chunked = b.ndim == 3
    cw = b.shape[2] if chunked else b.shape[1] // n_chunks
    has_init = init is not None

    def body(*refs):
        if has_init:
            a_ref, b_ref, init_ref, o_ref = refs
        else:
            a_ref, b_ref, o_ref = refs
        bv = b_ref[0] if chunked else b_ref[...]
        part = _dot_tn(a_ref[...], bv)
        l = pl.program_id(2)

        @pl.when(l == 0)
        def _():
            o_ref[0] = part + init_ref[0] if has_init else part

        @pl.when(l > 0)
        def _():
            o_ref[0] += part

    o_spec = pl.BlockSpec((1, tm, cw), lambda jm, jc, l: (jc, jm, 0))
    b_spec = (pl.BlockSpec((1, tl, cw), lambda jm, jc, l: (jc, l, 0)) if chunked
              else pl.BlockSpec((tl, cw), lambda jm, jc, l: (l, jc)))
    init_spec = pl.BlockSpec((1, tm, cw), lambda jm, jc, l: (jc, jm, 0), pipeline_mode=pl.Buffered(1))
    in_specs = [pl.BlockSpec((tl, tm), lambda jm, jc, l: (l, jm)), b_spec] + ([init_spec] if has_init else [])
    args = (a, b) + ((init,) if has_init else ())
    return pl.pallas_call(
        body, name=name, grid=(m // tm, n_chunks, n_tok // tl),
        out_shape=jax.ShapeDtypeStruct((n_chunks, m, cw), F32),
        in_specs=in_specs, out_specs=o_spec, compiler_params=_params(3),
    )(*args)


def _block_diag(t, ngb):
    g, p, n = t.shape
    gpb = g // ngb
    eye = jnp.eye(gpb, dtype=t.dtype)
    return jnp.einsum("bgpn,gh->bgphn", t.reshape(ngb, gpb, p, n), eye).reshape(ngb, gpb * p, gpb * n)


def _block_diag_t(mat, g, p, n):
    ngb = mat.shape[0]
    gpb = g // ngb
    eye = jnp.eye(gpb, dtype=mat.dtype)
    return jnp.einsum("bgphn,gh->bgpn", mat.reshape(ngb, gpb, p, gpb, n), eye).reshape(g, p, n)


def _scan_tables(pw_r, pw_i, ngb, reverse):
    _, g, n = pw_r.shape
    sb = g * n // ngb
    rows = jnp.arange(SUBLANES)
    kinds = []
    for step in (1, 2, 4):
        mask = ((rows < SUBLANES - step) if reverse else (rows >= step)).astype(F32)
        for part in (pw_r[step - 1], pw_i[step - 1]):
            kinds.append(part.reshape(ngb, 1, sb) * mask[None, :, None])
    for part in (pw_r, pw_i):
        pw = part[::-1] if reverse else part
        kinds.append(jnp.transpose(pw.reshape(SUBLANES, ngb, sb), (1, 0, 2)))
    return jnp.stack(kinds, axis=1)


def _flat(parts):
    return jnp.concatenate([p.reshape(-1) for p in parts])


def _unflat(vec, shapes):
    out, off = [], 0
    for s in shapes:
        size = math.prod(s)
        out.append(vec[off:off + size].reshape(s))
        off += size
    return out


def kernel(x, c, ctx, c_ctx, ada_w, ada_b, ln_g, ln_b, conv_w_in, conv_w, conv_w_out, ssm_w_in, ssm_lam_re, ssm_lam_im, ssm_log_step, ssm_b_re, ssm_b_im, ssm_c_re, ssm_c_im, ssm_d, ssm_w_glu, ssm_b_glu, ssm_w_out, loss_target, m_c_ctx, m_ada_w, m_ada_b, m_ln_g, m_ln_b, m_conv_w_in, m_conv_w, m_conv_w_out, m_ssm_w_in, m_ssm_lam_re, m_ssm_lam_im, m_ssm_log_step, m_ssm_b_re, m_ssm_b_im, m_ssm_c_re, m_ssm_c_im, m_ssm_d, m_ssm_w_glu, m_ssm_b_glu, m_ssm_w_out, v_c_ctx, v_ada_w, v_ada_b, v_ln_g, v_ln_b, v_conv_w_in, v_conv_w, v_conv_w_out, v_ssm_w_in, v_ssm_lam_re, v_ssm_lam_im, v_ssm_log_step, v_ssm_b_re, v_ssm_b_im, v_ssm_c_re, v_ssm_c_im, v_ssm_d, v_ssm_w_glu, v_ssm_b_glu, v_ssm_w_out):
    weights = dict(c_ctx=c_ctx, ada_w=ada_w, ada_b=ada_b, ln_g=ln_g, ln_b=ln_b, conv_w_in=conv_w_in, conv_w=conv_w,
                   conv_w_out=conv_w_out, ssm_w_in=ssm_w_in, ssm_lam_re=ssm_lam_re, ssm_lam_im=ssm_lam_im,
                   ssm_log_step=ssm_log_step, ssm_b_re=ssm_b_re, ssm_b_im=ssm_b_im, ssm_c_re=ssm_c_re,
                   ssm_c_im=ssm_c_im, ssm_d=ssm_d, ssm_w_glu=ssm_w_glu, ssm_b_glu=ssm_b_glu, ssm_w_out=ssm_w_out)
    mom_m = dict(c_ctx=m_c_ctx, ada_w=m_ada_w, ada_b=m_ada_b, ln_g=m_ln_g, ln_b=m_ln_b, conv_w_in=m_conv_w_in,
                 conv_w=m_conv_w, conv_w_out=m_conv_w_out, ssm_w_in=m_ssm_w_in, ssm_lam_re=m_ssm_lam_re,
                 ssm_lam_im=m_ssm_lam_im, ssm_log_step=m_ssm_log_step, ssm_b_re=m_ssm_b_re, ssm_b_im=m_ssm_b_im,
                 ssm_c_re=m_ssm_c_re, ssm_c_im=m_ssm_c_im, ssm_d=m_ssm_d, ssm_w_glu=m_ssm_w_glu,
                 ssm_b_glu=m_ssm_b_glu, ssm_w_out=m_ssm_w_out)
    mom_v = dict(c_ctx=v_c_ctx, ada_w=v_ada_w, ada_b=v_ada_b, ln_g=v_ln_g, ln_b=v_ln_b, conv_w_in=v_conv_w_in,
                 conv_w=v_conv_w, conv_w_out=v_conv_w_out, ssm_w_in=v_ssm_w_in, ssm_lam_re=v_ssm_lam_re,
                 ssm_lam_im=v_ssm_lam_im, ssm_log_step=v_ssm_log_step, ssm_b_re=v_ssm_b_re, ssm_b_im=v_ssm_b_im,
                 ssm_c_re=v_ssm_c_re, ssm_c_im=v_ssm_c_im, ssm_d=v_ssm_d, ssm_w_glu=v_ssm_w_glu,
                 ssm_b_glu=v_ssm_b_glu, ssm_w_out=v_ssm_w_out)
    names = list(weights)

    n_lat, d_model = x.shape[1], x.shape[2]
    n_ctx = ctx.shape[1]
    e = 2 * d_model
    n_grp, n_state, grp = ssm_lam_re.shape[2], ssm_lam_re.shape[3], ssm_b_re.shape[4]
    ngb = e // LANES
    ws = ada_w.shape[2]
    tb_tok = min(512, n_lat)
    n_seq = n_ctx + n_lat
    tb_glu = math.gcd(256, n_ctx)
    chip = 2 * lax.axis_index("x") + lax.axis_index("y")
    me = 2 * chip + lax.axis_index("c")
    chips, everyone, pair = ("x", "y"), MESH_AXES, ("c",)

    x2, ctx2, tgt2 = x[0], ctx[0], loss_target[0]

    wc_in_own = conv_w_in[0].astype(BF16)
    later_weights = _Hosted([(w[0].astype(BF16), chips, False) for w in (conv_w_out, ssm_w_in, ssm_w_glu, ssm_w_out)])
    small_full = _exchange(_flat([conv_w[0], ssm_d[0], ssm_b_glu[0]]).reshape(1, -1), chips, False, "ag_small")
    es = conv_w.shape[2]
    conv_w_full = jnp.transpose(small_full[:, 0, :3 * es].reshape(4, 3, es), (1, 0, 2)).reshape(3, e)
    d_full = small_full[:, 0, 3 * es:4 * es].reshape(1, e)
    b_glu_full = small_full[:, 0, 4 * es:5 * es].reshape(1, e)

    c_all = _exchange(c, everyone, False, "ag_c").reshape(8, d_model)
    cc2 = c_ctx.reshape(1, d_model)
    b_sh = lax.dynamic_slice_in_dim(ada_b, chip * ws, ws, axis=1).reshape(DEPTH, 1, ws)
    m_sh = _ada_fwd(c_all, cc2, ada_w, b_sh)
    m_all = _exchange(m_sh, chips, False, "ag_mod")
    m_full = jnp.transpose(m_all, (1, 2, 0, 3)).reshape(DEPTH, 16, 3 * d_model)
    m_lat = lax.dynamic_slice_in_dim(m_full, me, 1, axis=1)
    m_ctx = m_full[:, 8:9]

    def mods(m, i):
        return m[i, :, 0:d_model], m[i, :, d_model:2 * d_model], m[i, :, 2 * d_model:3 * d_model]

    sh0, sc0, gt0 = mods(m_lat, 0)
    sh1, sc1, gt1 = mods(m_lat, 1)
    shc0, scc0, gtc0 = mods(m_ctx, 0)
    shc1, scc1, _ = mods(m_ctx, 1)
    ln0 = (ln_g[0:1], ln_b[0:1])
    ln1 = (ln_g[1:2], ln_b[1:2])

    rg = 2 * n_grp
    lam_re2 = ssm_lam_re[0].reshape(rg, n_state)
    lam_im2 = ssm_lam_im[0].reshape(rg, n_state)
    log_step2 = ssm_log_step[0].reshape(rg, 1)
    b_re_t = jnp.transpose(ssm_b_re[0], (3, 0, 1, 2)).reshape(grp, rg, n_state)
    b_im_t = jnp.transpose(ssm_b_im[0], (3, 0, 1, 2)).reshape(grp, rg, n_state)
    pw_r, pw_i, pq_r, pq_i, bbr, bbi = _zoh_fwd(lam_re2, lam_im2, log_step2, b_re_t, b_im_t)
    sbk = n_grp * n_state // ngb
    pw_r, pw_i, pq_r, pq_i = (t.reshape(8, 2, n_grp, n_state) for t in (pw_r, pw_i, pq_r, pq_i))
    bbr_g = jnp.transpose(bbr.reshape(grp, 2, n_grp, n_state), (1, 2, 0, 3))
    bbi_g = jnp.transpose(bbi.reshape(grp, 2, n_grp, n_state), (1, 2, 0, 3))

    def power_rows(pw, r, first):
        full = jnp.concatenate([jnp.full((1, n_grp, n_state), first, F32), pw[:, r]], axis=0)
        return jnp.transpose(full.reshape(CHUNK + 1, ngb, sbk), (1, 0, 2))

    s5 = []
    for r in range(2):
        prm = dict(bre=_block_diag(bbr_g[r], ngb), bim=_block_diag(bbi_g[r], ngb),
                   cre=_block_diag(ssm_c_re[0, r], ngb), cim=_block_diag(ssm_c_im[0, r], ngb),
                   wr=power_rows(pw_r, r, 1.0), wi=power_rows(pw_i, r, 0.0))
        half_rows = wc_in_own[r * (d_model // 2):(r + 1) * (d_model // 2)]
        t_op, bp_op, cp_op, (wc_in_half,) = _s5_ops(
            prm["bre"], prm["bim"], prm["cre"], prm["cim"], prm["wr"], prm["wi"], reverse=(r == 1),
            name=f"l1_s5_ops{r}", hosted=_Hosted([(half_rows, chips, False)]))
        s5.append(dict(
            prm, t=t_op, bp=bp_op, cp=cp_op, wc_in_half=wc_in_half,
            tab=_scan_tables(pq_r[:, r], pq_i[:, r], ngb, reverse=(r == 1)),
            tab_adj=_scan_tables(pq_r[:, r], -pq_i[:, r], ngb, reverse=(r == 0))))
    wc_in = jnp.concatenate([s5[0]["wc_in_half"], s5[1]["wc_in_half"]], axis=1)

    p0, h0, gathered = _inproj(x2, sc0, sh0, wc_in, tb=min(1024, n_lat), name="l0_inproj", hosted=later_weights)
    wc_out, ws_in, w_glu, ws_out = gathered
    wc_out, w_glu, ws_out = wc_out.reshape(e, d_model), w_glu.reshape(e, e), ws_out.reshape(e, d_model)
    wc_in_t, ws_in_t, wc_out_t = jnp.transpose(wc_in, (0, 2, 1)), jnp.transpose(ws_in, (0, 2, 1)), wc_out.T
    pc0, hc0 = _inproj(ctx2, scc0, shc0, wc_in, tb=tb_tok, name="l0_inproj_ctx")
    xhat0, rstd0, g0, yc0, f0 = _convgate(p0, x2, gt0, conv_w_full, wc_out, *ln0, grid_mode=True, tb=tb_tok, name="l0_conv")
    chat0, crstd0, gc0, ycc0, fc0 = _convgate(pc0, ctx2, gtc0, conv_w_full, wc_out, *ln0, grid_mode=False, tb=tb_tok,
                                              name="l0_conv_ctx")

    seq_rows = [(n_seq, n_ctx), (n_seq, 0)]
    useq_f, useq_b, h1 = _inproj(xhat0, sc1, sh1, ws_in[0:2], lnaff=ln0, tb=math.gcd(tb_tok, n_ctx), gb_rows=seq_rows,
                                 name="l1_inproj_u")
    z1, _ = _inproj(xhat0, sc1, sh1, ws_in[2:4], lnaff=ln0, tb=tb_tok, name="l1_inproj_z")
    uc, hc1 = _inproj(chat0, scc1, shc1, ws_in[0:2], lnaff=ln0, tb=tb_tok, gb_rows=[(n_ctx, 0)], name="l1_inproj_ctx")
    useq = [useq_f.at[:, 0:n_ctx].set(uc), useq_b.at[:, n_lat:].set(uc)]
    y_dir, hp_dir = [], []
    for r in range(2):
        yr, hcr = _s5_fwd(useq[r], s5[r]["t"], s5[r]["bp"], s5[r]["cp"], s5[r]["tab"], reverse=(r == 1),
                          name=f"l1_s5_fwd{r}")
        y_dir.append(yr)
        hp_dir.append(hcr)

    (loss, dxres, do1, gz1, gg1, dq1, dz1, dy_f, dy_b, dg1, db1, dgt1, dbglu, dd) = _glu_loss(
        useq[0], y_dir[0], y_dir[1], z1, xhat0, ln0, gt1, d_full, w_glu, b_glu_full, ws_out, ln1, tgt2,
        offs=(n_ctx, n_ctx, 0), dy_rows=seq_rows, tb=tb_glu, name="l1_glu_loss")
    no_dy = jnp.zeros((ngb, n_ctx, LANES), BF16)
    dy_dir = [dy_f.at[:, 0:n_ctx].set(no_dy), dy_b.at[:, n_lat:].set(no_dy)]

    tl = min(1024, n_lat)

    def owner_slices(name, full):
        w = weights[name]
        return full.reshape(8, math.prod(w.shape[:-1]) // 2, w.shape[-1])

    def scatter(named):
        return _Hosted([(owner_slices(name, full), everyone, True) for name, full in named])

    def siblings(names):
        return _Hosted([(_sum_parts(rs_parts[name], "sum_" + name), pair, False) for name in names])

    rs_parts, both_halves = {}, {}

    gw_glu = _wgrad(gg1, dq1, n_chunks=1, tm=e // 2, tl=tl, name="wg_glu")
    gw_ssm_out = _wgrad(gz1, do1, n_chunks=1, tm=e, tl=tl, name="wg_ssm_out")
    du_dir, s5_grads = [], []
    for r in range(2):
        if r == 0:
            hosted = scatter([("ssm_w_glu", gw_glu), ("ssm_w_out", gw_ssm_out)])
        else:
            hosted = siblings(["ssm_w_glu", "ssm_w_out"])
        dur, dt_op, dbp_op, dcp_op, da8, extra = _s5_bwd(useq[r], dy_dir[r], hp_dir[r], s5[r]["t"], s5[r]["bp"],
                                                         s5[r]["cp"], s5[r]["tab_adj"], reverse=(r == 1),
                                                         name=f"l1_s5_bwd{r}", hosted=hosted)
        if r == 0:
            rs_parts["ssm_w_glu"], rs_parts["ssm_w_out"] = extra
        else:
            both_halves["ssm_w_glu"], both_halves["ssm_w_out"] = extra
        du_dir.append(dur)
        prm = s5[r]
        s5_grads.append(functools.partial(
            _s5_ops_bwd, prm["bre"], prm["bim"], prm["cre"], prm["cim"], prm["wr"], prm["wi"], prm["wr"][:, 1:2],
            prm["wi"][:, 1:2], dt_op, dbp_op, dcp_op, da8, reverse=(r == 1), name=f"l1_s5_ops_bwd{r}"))
    dp1, dr0, df0, dsc1, dsh1, dg0, db0, dgt0 = _ssm_inbwd(
        du_dir[0], du_dir[1], ws_in_t, xhat0, rstd0, ln0, sc1, gt0, f0, lat=(dy_dir[1], dz1, d_full, dxres),
        row_f=n_ctx, row_b=0, tb=tb_glu, name="l1_inbwd")
    dpc1, drc0, dfc0, dscc1, dshc1, dgc0, dbc0, dgtc0 = _ssm_inbwd(
        du_dir[0], du_dir[1], ws_in_t, chat0, crstd0, ln0, scc1, gtc0, fc0, lat=None,
        row_f=0, row_b=n_lat, tb=n_ctx, name="l1_inbwd_ctx")

    def conv_backward(df, p, yc, dr, xin, sc, grid_mode, tag, hosted_b=None, hosted_in=None):
        dbg, dz, dyc = _conv_bwd_a(df, wc_out_t, p, yc, tb=tb_tok, name="l0_bwd_a" + tag)
        dp, dcw, extra_b = _conv_bwd_b(dyc, p, dbg, dz, conv_w_full, grid_mode=grid_mode, tb=tb_glu,
                                       name="l0_bwd_b" + tag, hosted=hosted_b)
        gx, dsc, dsh, extra_in = _conv_inbwd(dp, wc_in_t, dr, xin, sc, tb=tb_tok, name="l0_inbwd" + tag,
                                             hosted=None if hosted_in is None else hosted_in(dp, extra_b))
        return dp, dcw, gx, dsc, dsh, extra_b, extra_in

    dpc0, dcwc0, _, dscc0, dshc0, _, _ = conv_backward(dfc0, pc0, ycc0, drc0, ctx2, scc0, False, "_ctx")
    gw_conv_out = _wgrad(g0, df0, n_chunks=1, tm=e, tl=tl, name="wg_conv_out",
                         init=_wgrad(gc0, dfc0, n_chunks=1, tm=e, tl=tl, name="wg_conv_out_ctx"))
    gw_ssm_in = _wgrad(h1, dp1, n_chunks=4, tm=d_model, tl=tl, name="wg_ssm_in",
                       init=_wgrad(hc1, dpc1, n_chunks=4, tm=d_model, tl=tl, name="wg_ssm_in_ctx"))
    gw_conv_in_ctx = _wgrad(hc0, dpc0, n_chunks=4, tm=d_model, tl=tl, name="wg_conv_in_ctx")

    def behind_inbwd(dp, arrived):
        rs_parts["ssm_w_in"], rs_parts["conv_w_out"] = arrived
        gw_conv_in = _wgrad(h0, dp, n_chunks=4, tm=d_model, tl=tl, name="wg_conv_in", init=gw_conv_in_ctx)
        both = siblings(["ssm_w_in", "conv_w_out"])
        return _Hosted(scatter([("conv_w_in", gw_conv_in)]).items + both.items)

    dp0, dcw0, grad_x, dsc0, dsh0, _, extra_in = conv_backward(
        df0, p0, yc0, dr0, x2, sc0, True, "", hosted_b=scatter([("ssm_w_in", gw_ssm_in), ("conv_w_out", gw_conv_out)]),
        hosted_in=behind_inbwd)
    rs_parts["conv_w_in"], both_halves["ssm_w_in"], both_halves["conv_w_out"] = extra_in
    *grads_r0, _ = s5_grads[0]()
    *grads_r1, (both_halves["conv_w_in"],) = s5_grads[1](hosted=siblings(["conv_w_in"]))
    s5_grads = [grads_r0, grads_r1]

    grads, deltas, new_m, new_v = {}, {}, {}, {}
    for name in ("ssm_w_glu", "ssm_w_out", "ssm_w_in", "conv_w_out", "conv_w_in"):
        w = weights[name]
        rows, cols = math.prod(w.shape[:-1]), w.shape[-1]
        both = both_halves[name].reshape(rows, cols)
        dlt, nm, nv = _adamw(w.reshape(rows, cols), both, mom_m[name].reshape(rows, cols),
                             mom_v[name].reshape(rows, cols), "adamw_" + name)
        grads[name], deltas[name] = both.reshape(w.shape), dlt.reshape(w.shape)
        new_m[name], new_v[name] = nm.reshape(w.shape), nv.reshape(w.shape)

    gpn = (n_grp, grp, n_state)
    small_parts = [
        jnp.concatenate([dg0 + dgc0, dg1], axis=0), jnp.concatenate([db0 + dbc0, db1], axis=0),
        dcw0 + dcwc0, dd, dbglu,
        jnp.stack([s5_grads[r][4] for r in range(2)]),
    ] + [jnp.stack([_block_diag_t(s5_grads[r][k], *gpn) for r in range(2)]) for k in range(4)]
    small_shapes = [p.shape for p in small_parts]
    flat = _flat(small_parts)
    quantum = 8 * SUBLANES * LANES
    n_flat = -(-flat.shape[0] // quantum) * quantum
    flat = jnp.pad(flat, (0, n_flat - flat.shape[0])).reshape(8, n_flat // (8 * LANES), LANES)
    red = _sum_parts(_exchange(flat, everyone, True, "rs_small"), "sum_small")
    red = _exchange(red, everyone, False, "ag_small_grads").reshape(-1)
    g_ln_g, g_ln_b, g_conv_w, g_d, g_bglu, g_a, g_bbr, g_bbi, g_cre, g_cim = _unflat(red, small_shapes)

    g_a = g_a.reshape(2, ngb, 2, sbk)
    dar = g_a[:, :, 0].reshape(rg, n_state)
    dai = g_a[:, :, 1].reshape(rg, n_state)
    dbbr_t = jnp.transpose(g_bbr, (2, 0, 1, 3)).reshape(grp, rg, n_state)
    dbbi_t = jnp.transpose(g_bbi, (2, 0, 1, 3)).reshape(grp, rg, n_state)
    z_lre, z_lim, z_ls, z_bre, z_bim = _zoh_bwd(lam_re2, lam_im2, log_step2, b_re_t, b_im_t, dar, dai, dbbr_t, dbbi_t)

    zero = jnp.zeros((1, d_model), F32)
    dm_rows = jnp.stack([
        jnp.stack([jnp.concatenate([dsh0, dsc0, dgt0], axis=1), jnp.concatenate([dshc0, dscc0, dgtc0], axis=1)]),
        jnp.stack([jnp.concatenate([dsh1, dsc1, dgt1], axis=1), jnp.concatenate([dshc1, dscc1, zero], axis=1)]),
    ]).reshape(DEPTH, 2, 3 * d_model)
    dm_all = _exchange(dm_rows, everyone, False, "ag_dmod")
    dm_sh = lax.dynamic_slice_in_dim(dm_all, chip * ws, ws, axis=3)
    g_ada_w, g_ada_b, ds_part = _ada_bwd(c_all, cc2, ada_w, dm_all, dm_sh)
    g_cctx = _cctx_grad(_exchange(ds_part, chips, False, "ag_dsctx"), cc2)

    grads["ada_w"] = g_ada_w
    dlt, nm, nv = _adamw(ada_w.reshape(-1, ws), g_ada_w.reshape(-1, ws), m_ada_w.reshape(-1, ws),
                         v_ada_w.reshape(-1, ws), "adamw_ada_w")
    deltas["ada_w"], new_m["ada_w"], new_v["ada_w"] = dlt.reshape(ada_w.shape), nm.reshape(ada_w.shape), nv.reshape(ada_w.shape)

    def chip_cols(full, rows):
        return lax.dynamic_slice_in_dim(full.reshape(rows, e), chip * es, es, axis=1)

    small_grads = dict(
        c_ctx=g_cctx.reshape(c_ctx.shape), ada_b=g_ada_b.reshape(ada_b.shape), ln_g=g_ln_g, ln_b=g_ln_b,
        conv_w=chip_cols(g_conv_w, 3).reshape(conv_w.shape),
        ssm_lam_re=z_lre.reshape(ssm_lam_re.shape), ssm_lam_im=z_lim.reshape(ssm_lam_im.shape),
        ssm_log_step=z_ls.reshape(ssm_log_step.shape),
        ssm_b_re=jnp.transpose(z_bre.reshape(grp, 2, n_grp, n_state), (1, 2, 3, 0)).reshape(ssm_b_re.shape),
        ssm_b_im=jnp.transpose(z_bim.reshape(grp, 2, n_grp, n_state), (1, 2, 3, 0)).reshape(ssm_b_im.shape),
        ssm_c_re=g_cre.reshape(ssm_c_re.shape), ssm_c_im=g_cim.reshape(ssm_c_im.shape),
        ssm_d=chip_cols(g_d, 1).reshape(ssm_d.shape), ssm_b_glu=chip_cols(g_bglu, 1).reshape(ssm_b_glu.shape))
    small_names = list(small_grads)
    shapes = [weights[n].shape for n in small_names]
    quantum = SUBLANES * LANES

    def pack(parts, fill):
        vec = _flat(parts)
        n_pad = -(-vec.shape[0] // quantum) * quantum
        return jnp.pad(vec, (0, n_pad - vec.shape[0]), constant_values=fill).reshape(-1, LANES)

    dlt, nm, nv = _adamw(pack([weights[n] for n in small_names], 0.0), pack([small_grads[n] for n in small_names], 0.0),
                         pack([mom_m[n] for n in small_names], 0.0), pack([mom_v[n] for n in small_names], 1.0),
                         "adamw_small")
    for n, dv, mv, vv in zip(small_names, _unflat(dlt.reshape(-1), shapes), _unflat(nm.reshape(-1), shapes),
                             _unflat(nv.reshape(-1), shapes)):
        grads[n], deltas[n], new_m[n], new_v[n] = small_grads[n], dv, mv, vv

    loss_total = lax.psum(loss[0, 0], MESH_AXES)
    return (loss_total, grad_x.reshape(x.shape), *[grads[n] for n in names], *[deltas[n] for n in names],
            *[new_m[n] for n in names], *[new_v[n] for n in names])
```

```python
import functools
import math

import jax
import jax.numpy as jnp
from jax import lax
from jax.experimental import pallas as pl
from jax.experimental.pallas import tpu as pltpu

F32 = jnp.float32
BF16 = jnp.bfloat16
LANES = 128
SUBLANES = 8
VMEM_LIMIT = 56 * 1024 * 1024
MESH_AXES = ("x", "y", "c")
HIGHEST = lax.Precision.HIGHEST

GRID_W = 64
LN_EPS = 1e-5
DEPTH = 2
DN_ALPHA = (2 * DEPTH) ** 0.25
ADAM_LR, ADAM_B1, ADAM_B2, ADAM_EPS, ADAM_WD, ADAM_STEP = 0.001, 0.9, 0.999, 1e-08, 0.01, 10
GELU_K = math.sqrt(2.0 / math.pi)
GELU_C = 0.044715


def _params(n_grid_axes):
    return pltpu.CompilerParams(dimension_semantics=("arbitrary",) * n_grid_axes, vmem_limit_bytes=VMEM_LIMIT)


def _dot(a, b):
    return jnp.dot(a, b, preferred_element_type=F32)


def _dot_nt(a, b):
    return lax.dot_general(a, b, (((1,), (1,)), ((), ())), preferred_element_type=F32)


def _dot_tn(a, b):
    return lax.dot_general(a, b, (((0,), (0,)), ((), ())), preferred_element_type=F32)


def _sigmoid(x):
    return 0.5 * jnp.tanh(0.5 * x) + 0.5


def _colsum(x):
    return jnp.sum(x, axis=0, keepdims=True)


def _rowmean(x):
    return jnp.mean(x, axis=-1, keepdims=True)


def _zero_first(first, *refs):
    @pl.when(first)
    def _():
        for ref in refs:
            ref[...] = jnp.zeros_like(ref)


def _acc(ref, value):
    ref[...] += value


def _exchange_copies(src_ref, out_ref, send_sems, recv_sems, own_sem, axes, all_to_all, sem0=0):
    n_peers = 2 ** len(axes)
    pos = {a: lax.axis_index(a) for a in MESH_AXES}

    def index(p):
        return sum(p[a] * (2 ** (len(axes) - 1 - i)) for i, a in enumerate(axes))

    me = index(pos)
    own = pltpu.make_async_copy(src_ref.at[me] if all_to_all else src_ref, out_ref.at[me], own_sem)
    copies = []
    for k in range(1, n_peers):
        peer = dict(pos)
        for i, a in enumerate(axes):
            if (k >> (len(axes) - 1 - i)) & 1:
                peer[a] = 1 - pos[a]
        copies.append(pltpu.make_async_remote_copy(
            src_ref=src_ref.at[index(peer)] if all_to_all else src_ref,
            dst_ref=out_ref.at[me],
            send_sem=send_sems.at[sem0 + k - 1],
            recv_sem=recv_sems.at[sem0 + k - 1],
            device_id=tuple(peer[a] for a in MESH_AXES),
            device_id_type=pl.DeviceIdType.MESH,
        ))
    return copies, own


def _exchange_shape(src, axes, all_to_all):
    block = tuple(src.shape[1:] if all_to_all else src.shape)
    return jax.ShapeDtypeStruct((2 ** len(axes),) + block, src.dtype)


def _exchange(src, axes, all_to_all, name):
    n_peers = 2 ** len(axes)

    def body(src_ref, out_ref, send_sems, recv_sems, own_sem):
        copies, own = _exchange_copies(src_ref, out_ref, send_sems, recv_sems, own_sem, axes, all_to_all)
        own.start()
        for cp in copies:
            cp.start()
        for cp in copies:
            cp.wait()
        own.wait()

    return pl.pallas_call(
        body,
        name=name,
        out_shape=_exchange_shape(src, axes, all_to_all),
        in_specs=[pl.BlockSpec(memory_space=pltpu.HBM)],
        out_specs=pl.BlockSpec(memory_space=pltpu.HBM),
        scratch_shapes=[
            pltpu.SemaphoreType.DMA((n_peers - 1,)),
            pltpu.SemaphoreType.DMA((n_peers - 1,)),
            pltpu.SemaphoreType.DMA,
        ],
    )(src)


class _Hosted:
    def __init__(self, items):
        self.items = items
        self.args = [src for src, _, _ in items]
        self.in_specs = [pl.BlockSpec(memory_space=pltpu.HBM)] * len(items)
        self.out_specs = [pl.BlockSpec(memory_space=pltpu.HBM)] * len(items)
        self.out_shapes = [_exchange_shape(*item) for item in items]
        n_remote = sum(2 ** len(axes) - 1 for _, axes, _ in items)
        self.scratch = [pltpu.SemaphoreType.DMA((n_remote,)), pltpu.SemaphoreType.DMA((n_remote,)),
                        pltpu.SemaphoreType.DMA((len(items),))]

    def _copies(self, src_refs, out_refs, send_sems, recv_sems, own_sems):
        out, sem0 = [], 0
        for n, (_, axes, all_to_all) in enumerate(self.items):
            copies, own = _exchange_copies(src_refs[n], out_refs[n], send_sems, recv_sems, own_sems.at[n], axes,
                                           all_to_all, sem0)
            out += [own] + copies
            sem0 += len(copies)
        return out

    def start(self, *refs):
        for cp in self._copies(*refs):
            cp.start()

    def wait(self, *refs):
        for cp in self._copies(*refs):
            cp.wait()


def _call(body, *, name, grid, in_specs, out_specs, out_shape, scratch_shapes, args, hosted=None):
    params = _params(len(grid))
    if hosted is None:
        outs = pl.pallas_call(body, name=name, grid=grid, in_specs=in_specs, out_specs=tuple(out_specs),
                              out_shape=tuple(out_shape), scratch_shapes=list(scratch_shapes), compiler_params=params)(*args)
        return list(outs), []
    n_in, n_out, n_scr, n_h = len(in_specs), len(out_shape), len(scratch_shapes), len(hosted.items)

    def wrapped(*refs):
        ins, h_in = refs[:n_in], refs[n_in:n_in + n_h]
        outs, h_out = refs[n_in + n_h:n_in + n_h + n_out], refs[n_in + n_h + n_out:n_in + 2 * n_h + n_out]
        scr = refs[n_in + 2 * n_h + n_out:]
        first = functools.reduce(jnp.logical_and, [pl.program_id(k) == 0 for k in range(len(grid))])
        last = functools.reduce(jnp.logical_and, [pl.program_id(k) == grid[k] - 1 for k in range(len(grid))])

        @pl.when(first)
        def _():
            hosted.start(h_in, h_out, *scr[n_scr:])

        body(*ins, *outs, *scr[:n_scr])

        @pl.when(last)
        def _():
            hosted.wait(h_in, h_out, *scr[n_scr:])

    outs = pl.pallas_call(
        wrapped, name=name, grid=grid, in_specs=[*in_specs, *hosted.in_specs],
        out_specs=(*out_specs, *hosted.out_specs), out_shape=(*out_shape, *hosted.out_shapes),
        scratch_shapes=[*scratch_shapes, *hosted.scratch], compiler_params=params)(*args, *hosted.args)
    return list(outs[:n_out]), list(outs[n_out:])


def _sum_parts(parts, name):
    n_parts, rows, cols = parts.shape
    tr = rows
    while n_parts * tr * cols * 4 > 8 * 1024 * 1024 and tr % 16 == 0:
        tr //= 2

    def body(p_ref, o_ref):
        total = p_ref[0]
        for k in range(1, n_parts):
            total = total + p_ref[k]
        o_ref[...] = total

    return pl.pallas_call(
        body,
        name=name,
        grid=(rows // tr,),
        out_shape=jax.ShapeDtypeStruct((rows, cols), F32),
        in_specs=[pl.BlockSpec((n_parts, tr, cols), lambda i: (0, i, 0))],
        out_specs=pl.BlockSpec((tr, cols), lambda i: (i, 0)),
        compiler_params=_params(1),
    )(parts)


def _adamw(w, g, m, v, name):
    rows, cols = w.shape
    tr = rows
    while tr * cols * 4 > 2 * 1024 * 1024 and tr % 16 == 0:
        tr //= 2

    def body(w_ref, g_ref, m_ref, v_ref, d_ref, nm_ref, nv_ref):
        gv = g_ref[...]
        nm = ADAM_B1 * m_ref[...] + (1.0 - ADAM_B1) * gv
        nv = ADAM_B2 * v_ref[...] + (1.0 - ADAM_B2) * (gv * gv)
        m_hat = nm / (1.0 - ADAM_B1 ** ADAM_STEP)
        v_hat = nv / (1.0 - ADAM_B2 ** ADAM_STEP)
        d_ref[...] = -ADAM_LR * (m_hat / (jnp.sqrt(v_hat) + ADAM_EPS) + ADAM_WD * w_ref[...])
        nm_ref[...] = nm
        nv_ref[...] = nv

    spec = pl.BlockSpec((tr, cols), lambda i: (i, 0))
    shape = jax.ShapeDtypeStruct((rows, cols), F32)
    return pl.pallas_call(
        body, name=name, grid=(rows // tr,), out_shape=(shape, shape, shape),
        in_specs=[spec] * 4, out_specs=(spec, spec, spec), compiler_params=_params(1),
    )(w, g, m, v)


def _ada_rows(c_ref, cc_ref):
    rows = jnp.concatenate([c_ref[...], jnp.broadcast_to(cc_ref[...], c_ref.shape)], axis=0)
    return rows


def _ada_fwd(c_all, c_ctx, w_sh, b_sh):
    n_layers, _, ws = w_sh.shape

    def body(c_ref, cc_ref, w_ref, b_ref, o_ref):
        rows = _ada_rows(c_ref, cc_ref)
        s = rows * _sigmoid(rows)
        for i in range(n_layers):
            o_ref[i] = jnp.dot(s, w_ref[i], precision=HIGHEST, preferred_element_type=F32) + b_ref[i]

    return pl.pallas_call(
        body, name="ada_fwd", out_shape=jax.ShapeDtypeStruct((n_layers, 16, ws), F32),
        compiler_params=pltpu.CompilerParams(vmem_limit_bytes=VMEM_LIMIT),
    )(c_all, c_ctx, w_sh, b_sh)


def _ada_bwd(c_all, c_ctx, w_sh, dm_full, dm_sh):
    n_layers, d_model, ws = w_sh.shape
    n_dev = dm_full.shape[0]
    cols = dm_full.shape[-1]

    def body(c_ref, cc_ref, w_ref, dmf_ref, dms_ref, gw_ref, gb_ref, ds_ref):
        rows = _ada_rows(c_ref, cc_ref)
        s = rows * _sigmoid(rows)
        ds = jnp.zeros((8, d_model), F32)
        for i in range(n_layers):
            ctx_s = dms_ref[0, i, 1:2, :]
            ctx_f = dmf_ref[0, i, 1:2, :]
            ex_f = dmf_ref[0, i, 0:1, :]
            for k in range(1, n_dev):
                ctx_s = ctx_s + dms_ref[k, i, 1:2, :]
                ctx_f = ctx_f + dmf_ref[k, i, 1:2, :]
                ex_f = ex_f + dmf_ref[k, i, 0:1, :]
            gb_ref[i] = ex_f + ctx_f
            r = jnp.concatenate([dms_ref[k, i, 0:1, :] for k in range(n_dev)] + [ctx_s, jnp.zeros((7, ws), F32)], axis=0)
            gw_ref[i] = lax.dot_general(s, r, (((0,), (0,)), ((), ())), precision=HIGHEST, preferred_element_type=F32)
            ds = ds + lax.dot_general(jnp.broadcast_to(ctx_s, (8, ws)), w_ref[i], (((1,), (1,)), ((), ())),
                                      precision=HIGHEST, preferred_element_type=F32)
        ds_ref[...] = ds

    return pl.pallas_call(
        body, name="ada_bwd",
        out_shape=(jax.ShapeDtypeStruct((n_layers, d_model, ws), F32), jax.ShapeDtypeStruct((n_layers, 1, cols), F32),
                   jax.ShapeDtypeStruct((8, d_model), F32)),
        compiler_params=pltpu.CompilerParams(vmem_limit_bytes=VMEM_LIMIT),
    )(c_all, c_ctx, w_sh, dm_full, dm_sh)


def _cctx_grad(ds_parts, c_ctx):
    def body(p_ref, c_ref, o_ref):
        tot = p_ref[0, 0:1, :]
        for k in range(1, ds_parts.shape[0]):
            tot = tot + p_ref[k, 0:1, :]
        cv = c_ref[...]
        sg = _sigmoid(cv)
        o_ref[...] = tot * (sg * (1.0 + cv * (1.0 - sg)))

    return pl.pallas_call(body, name="cctx_grad", out_shape=jax.ShapeDtypeStruct(c_ctx.shape, F32))(ds_parts, c_ctx)


def _zoh_math(lam_re, lam_im, log_step, b_re, b_im):
    n_state = lam_re.shape[0] // 2
    dt = jnp.exp(jnp.concatenate([jnp.broadcast_to(log_step[r:r + 1], (n_state, log_step.shape[1])) for r in range(2)],
                                 axis=0))
    mag = jnp.exp(lam_re * dt)
    ar = mag * jnp.cos(lam_im * dt)
    ai = mag * jnp.sin(lam_im * dt)
    qr, qi = ar - 1.0, ai
    den = lam_re * lam_re + lam_im * lam_im
    fr = (qr * lam_re + qi * lam_im) / den
    fi = (qi * lam_re - qr * lam_im) / den
    bbr = fr[None] * b_re - fi[None] * b_im
    bbi = fr[None] * b_im + fi[None] * b_re
    return ar, ai, bbr, bbi


def _zoh_fwd(lam_re, lam_im, log_step, b_re, b_im):
    rg, n = lam_re.shape

    def body(lr_ref, li_ref, ls_ref, br_ref, bi_ref, pr_ref, pi_ref, qr_ref, qi_ref, bbr_ref, bbi_ref):
        ar, ai, bbr, bbi = _zoh_math(lr_ref[...], li_ref[...], ls_ref[...], br_ref[...], bi_ref[...])
        bbr_ref[...] = bbr
        bbi_ref[...] = bbi

        def powers(base_r, base_i, r_ref, i_ref):
            pr, pi_ = base_r, base_i
            for k in range(8):
                r_ref[k] = pr
                i_ref[k] = pi_
                pr, pi_ = pr * base_r - pi_ * base_i, pr * base_i + pi_ * base_r

        powers(ar, ai, pr_ref, pi_ref)
        powers(pr_ref[7], pi_ref[7], qr_ref, qi_ref)

    pw = jax.ShapeDtypeStruct((8, rg, n), F32)
    bb = jax.ShapeDtypeStruct(b_re.shape, F32)
    return pl.pallas_call(body, name="zoh_fwd", out_shape=(pw, pw, pw, pw, bb, bb))(lam_re, lam_im, log_step, b_re, b_im)


def _zoh_bwd(lam_re, lam_im, log_step, b_re, b_im, dar, dai, dbbr, dbbi):
    def body(lr_ref, li_ref, ls_ref, br_ref, bi_ref, dar_ref, dai_ref, dbr_ref, dbi_ref, *outs):
        _, vjp = jax.vjp(_zoh_math, lr_ref[...], li_ref[...], ls_ref[...], br_ref[...], bi_ref[...])
        grads = vjp((dar_ref[...], dai_ref[...], dbr_ref[...], dbi_ref[...]))
        for o_ref, gval in zip(outs, grads):
            o_ref[...] = gval

    shapes = tuple(jax.ShapeDtypeStruct(a.shape, F32) for a in (lam_re, lam_im, log_step, b_re, b_im))
    return pl.pallas_call(body, name="zoh_bwd", out_shape=shapes)(lam_re, lam_im, log_step, b_re, b_im, dar, dai, dbbr, dbbi)


def _inproj(xin, sc, sh, w, *, lnaff=None, tb, gb_rows=None, name, hosted=None):
    n_tok, d_model = xin.shape
    n_chunks, _, cw = w.shape
    tb = min(tb, n_tok)
    nq = cw // LANES
    has_ln = lnaff is not None
    n_out = 1 if gb_rows is None else len(gb_rows)

    def body(*refs):
        if has_ln:
            x_ref, g_ref, b_ref, sc_ref, sh_ref, w_ref = refs[:6]
        else:
            x_ref, sc_ref, sh_ref, w_ref = refs[:4]
        p_refs, h_ref = refs[-1 - n_out:-1], refs[-1]

        @pl.when(pl.program_id(1) == 0)
        def _():
            xv = x_ref[...]
            if has_ln:
                xv = xv * g_ref[...] + b_ref[...]
            h_ref[...] = (xv * (1.0 + sc_ref[...]) + sh_ref[...]).astype(BF16)

        acc = _dot(h_ref[...], w_ref[0]).astype(BF16)
        if gb_rows is None:
            p_refs[0][0] = acc
        else:
            for p_ref in p_refs:
                for q in range(nq):
                    p_ref[q] = acc[:, q * LANES:(q + 1) * LANES]

    vec = pl.BlockSpec((1, d_model), lambda i, j: (0, 0))
    in_specs = [pl.BlockSpec((tb, d_model), lambda i, j: (i, 0))] + ([vec, vec] if has_ln else []) + [
        vec, vec, pl.BlockSpec((1, d_model, cw), lambda i, j: (j, 0, 0))]
    if gb_rows is None:
        p_shapes = [jax.ShapeDtypeStruct((n_chunks, n_tok, cw), BF16)]
        p_specs = [pl.BlockSpec((1, tb, cw), lambda i, j: (j, i, 0))]
    else:
        p_shapes, p_specs = [], []
        for total, off in gb_rows:
            assert off % tb == 0
            p_shapes.append(jax.ShapeDtypeStruct((n_chunks * nq, total, LANES), BF16))
            p_specs.append(pl.BlockSpec((nq, tb, LANES), functools.partial(lambda i, j, ob: (j, i + ob, 0), ob=off // tb)))
    args = (xin,) + (tuple(lnaff) if has_ln else ()) + (sc, sh, w)
    outs, extra = _call(
        body, name=name, grid=(n_tok // tb, n_chunks), in_specs=in_specs,
        out_specs=[*p_specs, pl.BlockSpec((tb, d_model), lambda i, j: (i, 0))],
        out_shape=[*p_shapes, jax.ShapeDtypeStruct((n_tok, d_model), BF16)], scratch_shapes=[], args=args, hosted=hosted)
    return (*outs, extra) if hosted is not None else tuple(outs)


def _shifted(u, rows, width, tb):
    col = rows % width
    um = jnp.where(col == 0, 0.0, pltpu.roll(u, 1, 0))
    up = jnp.where(col == width - 1, 0.0, pltpu.roll(u, tb - 1, 0))
    return um, up


def _slab_width(e):
    return min(512, e // 2)


def _convgate(p, x, gt, conv_w, w_out, ln_g, ln_b, *, grid_mode, tb, name):
    _, n_tok, e = p.shape
    d_model = x.shape[1]
    eh = e // 2
    if not grid_mode:
        tb = n_tok
    tb = min(tb, n_tok)
    nb = n_tok // tb
    hb = tb // GRID_W
    cs = _slab_width(e)

    def body(*refs):
        if grid_mode:
            (bg_ref, cg_ref, v_ref, z_ref, cgp_ref, vp_ref, cgn_ref, vn_ref, x_ref, gt_ref, cw_ref, wo_ref, lg_ref,
             lb_ref, xh_ref, rs_ref, g_ref, yc_ref, f_ref) = refs
        else:
            (bg_ref, cg_ref, v_ref, z_ref, x_ref, gt_ref, cw_ref, wo_ref, lg_ref, lb_ref, xh_ref, rs_ref, g_ref,
             yc_ref, f_ref) = refs
        i = pl.program_id(0)
        rows = lax.broadcasted_iota(jnp.int32, (tb, 1), 0)
        for c0 in range(0, e, cs):
            sl = slice(c0, c0 + cs)
            u = cg_ref[0, :, sl].astype(F32) * v_ref[0, :, sl].astype(F32)
            w = cw_ref[:, sl]
            if grid_mode and c0 >= eh:
                hs = slice(c0 - eh, c0 - eh + cs)
                uprev = cgp_ref[0, :, hs].astype(F32) * vp_ref[0, :, hs].astype(F32)
                unext = cgn_ref[0, :, hs].astype(F32) * vn_ref[0, :, hs].astype(F32)
                uprev = jnp.where(i > 0, uprev, 0.0)
                unext = jnp.where(i < nb - 1, unext, 0.0)
                if tb > GRID_W:
                    um = jnp.concatenate([uprev, u[:tb - GRID_W]], axis=0)
                    up = jnp.concatenate([u[GRID_W:], unext], axis=0)
                else:
                    um, up = uprev, unext
            else:
                um, up = _shifted(u, rows, GRID_W if grid_mode else tb, tb)
            yc = um * w[0:1] + u * w[1:2] + up * w[2:3]
            zf = z_ref[0, :, sl].astype(F32)
            gval = bg_ref[0, :, sl].astype(F32) * yc * (zf * _sigmoid(zf))
            yc_ref[:, sl] = yc.astype(BF16)
            g_ref[:, sl] = gval.astype(BF16)
        f = _dot(g_ref[...], wo_ref[...])
        f_ref[...] = f.astype(BF16)
        r = DN_ALPHA * x_ref[...] + gt_ref[...] * f
        rc = r - _rowmean(r)
        rstd = lax.rsqrt(_rowmean(rc * rc) + LN_EPS)
        xh_ref[...] = rc * rstd
        rs_ref[...] = rstd

    def chunk(k):
        return pl.BlockSpec((1, tb, e), lambda i: (k, i, 0))

    n_hrows = n_tok // GRID_W

    def halo_prev(k):
        return pl.BlockSpec((1, GRID_W, eh), lambda i: (k, jnp.maximum(i * hb - 1, 0), 1))

    def halo_next(k):
        return pl.BlockSpec((1, GRID_W, eh), lambda i: (k, jnp.minimum((i + 1) * hb, n_hrows - 1), 1))

    vec = pl.BlockSpec((1, d_model), lambda i: (0, 0))
    tok = pl.BlockSpec((tb, d_model), lambda i: (i, 0))
    wide = pl.BlockSpec((tb, e), lambda i: (i, 0))
    in_specs = [chunk(0), chunk(1), chunk(2), chunk(3)]
    args = [p, p, p, p]
    if grid_mode:
        in_specs += [halo_prev(1), halo_prev(2), halo_next(1), halo_next(2)]
        args += [p, p, p, p]
    in_specs += [tok, vec, pl.BlockSpec((3, e), lambda i: (0, 0)), pl.BlockSpec((e, d_model), lambda i: (0, 0)), vec, vec]
    args += [x, gt, conv_w, w_out, ln_g, ln_b]
    return pl.pallas_call(
        body, name=name, grid=(nb,),
        out_shape=(jax.ShapeDtypeStruct((n_tok, d_model), F32), jax.ShapeDtypeStruct((n_tok, 1), F32),
                   jax.ShapeDtypeStruct((n_tok, e), BF16), jax.ShapeDtypeStruct((n_tok, e), BF16),
                   jax.ShapeDtypeStruct((n_tok, d_model), BF16)),
        in_specs=in_specs, out_specs=(tok, pl.BlockSpec((tb, 1), lambda i: (i, 0)), wide, wide, tok),
        compiler_params=_params(1),
    )(*args)


def _scan_block(buf_ref, tab_ref, cr, ci, *, reverse, tb, sb):
    n_slabs = tb // SUBLANES
    unrolled = n_slabs <= 64

    def slab(s, carry):
        cr, ci = carry
        idx = (n_slabs - 1 - s) if reverse else s
        r0 = idx * SUBLANES if unrolled else pl.multiple_of(idx * SUBLANES, SUBLANES)
        xr = buf_ref[pl.ds(r0, SUBLANES), 0:sb]
        xi = buf_ref[pl.ds(r0, SUBLANES), sb:2 * sb]
        for k, step in enumerate((1, 2, 4)):
            ar = tab_ref[2 * k]
            ai = tab_ref[2 * k + 1]
            shift = (SUBLANES - step) if reverse else step
            rr = pltpu.roll(xr, shift, 0)
            ri = pltpu.roll(xi, shift, 0)
            xr, xi = xr + ar * rr - ai * ri, xi + ar * ri + ai * rr
        pr = tab_ref[6]
        pi_ = tab_ref[7]
        xr, xi = xr + pr * cr - pi_ * ci, xi + pr * ci + pi_ * cr
        buf_ref[pl.ds(r0, SUBLANES), 0:sb] = xr
        buf_ref[pl.ds(r0, SUBLANES), sb:2 * sb] = xi
        last = 0 if reverse else SUBLANES - 1
        return (jnp.broadcast_to(xr[last:last + 1, :], (SUBLANES, sb)),
                jnp.broadcast_to(xi[last:last + 1, :], (SUBLANES, sb)))

    if unrolled:
        carry = (cr, ci)
        for s in range(n_slabs):
            carry = slab(s, carry)
        return carry
    return lax.fori_loop(0, n_slabs, slab, (cr, ci))


CHUNK = SUBLANES


def _group_mask():
    r = lax.broadcasted_iota(jnp.int32, (LANES, LANES), 0)
    c = lax.broadcasted_iota(jnp.int32, (LANES, LANES), 1)
    return r // 16 == c // 16


def _s5_ops(bre, bim, cre, cim, wr, wi, *, reverse, name, hosted=None):
    ngb, _, sb = bre.shape
    n_rows = CHUNK * LANES

    def body(bre_ref, bim_ref, cre_ref, cim_ref, wr_ref, wi_ref, t_ref, bp_ref, cp_ref):
        b_re, b_im, c_re, c_im = bre_ref[0], bim_ref[0], cre_ref[0], cim_ref[0]
        mask = _group_mask()
        er, ei = [], []
        for tau in range(CHUNK + 1):
            w_r, w_i = wr_ref[0, tau:tau + 1, :], wi_ref[0, tau:tau + 1, :]
            er.append(c_re * w_r - c_im * w_i)
            ei.append(c_re * w_i + c_im * w_r)
        kt = []
        for tau in range(CHUNK):
            k = (lax.dot_general(b_re, er[tau], (((1,), (1,)), ((), ())), precision=HIGHEST, preferred_element_type=F32)
                 - lax.dot_general(b_im, ei[tau], (((1,), (1,)), ((), ())), precision=HIGHEST, preferred_element_type=F32))
            kt.append(jnp.where(mask, k, 0.0).astype(BF16))
        zero = jnp.zeros((LANES, LANES), BF16)
        for i in range(CHUNK):
            rows = slice(i * LANES, (i + 1) * LANES)
            for j in range(CHUNK):
                lag = (i - j) if reverse else (j - i)
                t_ref[0, rows, j * LANES:(j + 1) * LANES] = kt[lag] if lag >= 0 else zero
            tau = i if reverse else CHUNK - 1 - i
            w_r, w_i = wr_ref[0, tau:tau + 1, :], wi_ref[0, tau:tau + 1, :]
            bp_ref[0, rows, 0:sb] = (b_re * w_r - b_im * w_i).astype(BF16)
            bp_ref[0, rows, sb:2 * sb] = (b_re * w_i + b_im * w_r).astype(BF16)
            tau = CHUNK - i if reverse else i + 1
            cp_ref[0, rows, 0:sb] = er[tau].astype(BF16)
            cp_ref[0, rows, sb:2 * sb] = (-ei[tau]).astype(BF16)

    mat = pl.BlockSpec((1, LANES, sb), lambda g: (g, 0, 0))
    pw = pl.BlockSpec((1, CHUNK + 1, sb), lambda g: (g, 0, 0))
    outs, extra = _call(
        body, name=name, grid=(ngb,),
        out_shape=[jax.ShapeDtypeStruct((ngb, n_rows, n_rows), BF16), jax.ShapeDtypeStruct((ngb, n_rows, 2 * sb), BF16),
                   jax.ShapeDtypeStruct((ngb, n_rows, 2 * sb), BF16)],
        in_specs=[mat, mat, mat, mat, pw, pw],
        out_specs=[pl.BlockSpec((1, n_rows, n_rows), lambda g: (g, 0, 0)),
                   pl.BlockSpec((1, n_rows, 2 * sb), lambda g: (g, 0, 0)),
                   pl.BlockSpec((1, n_rows, 2 * sb), lambda g: (g, 0, 0))],
        scratch_shapes=[], args=(bre, bim, cre, cim, wr, wi), hosted=hosted)
    return (*outs, extra)


def _s5_ops_bwd(bre, bim, cre, cim, wr, wi, ar, ai, dt, dbp, dcp, da8, *, reverse, name, hosted=None):
    ngb, _, sb = bre.shape
    n_rows = CHUNK * LANES

    def dot_hi(a, b, dims):
        return lax.dot_general(a, b, (dims, ((), ())), precision=HIGHEST, preferred_element_type=F32)

    def body(bre_ref, bim_ref, cre_ref, cim_ref, wr_ref, wi_ref, ar_ref, ai_ref, dt_ref, dbp_ref, dcp_ref, da8_ref,
             dbre_ref, dbim_ref, dcre_ref, dcim_ref, da_ref):
        b_re, b_im, c_re, c_im = bre_ref[0], bim_ref[0], cre_ref[0], cim_ref[0]
        mask = _group_mask()
        w_r = [wr_ref[0, tau:tau + 1, :] for tau in range(CHUNK + 1)]
        w_i = [wi_ref[0, tau:tau + 1, :] for tau in range(CHUNK + 1)]
        der = [jnp.zeros((LANES, sb), F32) for _ in range(CHUNK + 1)]
        dei = [jnp.zeros((LANES, sb), F32) for _ in range(CHUNK + 1)]
        dwr = [jnp.zeros((1, sb), F32) for _ in range(CHUNK + 1)]
        dwi = [jnp.zeros((1, sb), F32) for _ in range(CHUNK + 1)]
        dwr[CHUNK] = da8_ref[0, :, 0:sb]
        dwi[CHUNK] = da8_ref[0, :, sb:2 * sb]
        d_bre = jnp.zeros((LANES, sb), F32)
        d_bim = jnp.zeros((LANES, sb), F32)
        dkt = [jnp.zeros((LANES, LANES), F32) for _ in range(CHUNK)]
        for i in range(CHUNK):
            rows = slice(i * LANES, (i + 1) * LANES)
            for j in range(CHUNK):
                lag = (i - j) if reverse else (j - i)
                if lag >= 0:
                    dkt[lag] = dkt[lag] + dt_ref[0, rows, j * LANES:(j + 1) * LANES]
            tau = i if reverse else CHUNK - 1 - i
            g_r, g_i = dbp_ref[0, rows, 0:sb], dbp_ref[0, rows, sb:2 * sb]
            d_bre = d_bre + g_r * w_r[tau] + g_i * w_i[tau]
            d_bim = d_bim - g_r * w_i[tau] + g_i * w_r[tau]
            dwr[tau] = dwr[tau] + _colsum(g_r * b_re + g_i * b_im)
            dwi[tau] = dwi[tau] + _colsum(g_i * b_re - g_r * b_im)
            tau = CHUNK - i if reverse else i + 1
            der[tau] = der[tau] + dcp_ref[0, rows, 0:sb]
            dei[tau] = dei[tau] - dcp_ref[0, rows, sb:2 * sb]
        d_cre = jnp.zeros((LANES, sb), F32)
        d_cim = jnp.zeros((LANES, sb), F32)
        for tau in range(CHUNK + 1):
            if tau < CHUNK:
                e_r = c_re * w_r[tau] - c_im * w_i[tau]
                e_i = c_re * w_i[tau] + c_im * w_r[tau]
                dk = jnp.where(mask, dkt[tau], 0.0)
                d_bre = d_bre + dot_hi(dk, e_r, ((1,), (0,)))
                d_bim = d_bim - dot_hi(dk, e_i, ((1,), (0,)))
                der[tau] = der[tau] + dot_hi(dk, b_re, ((0,), (0,)))
                dei[tau] = dei[tau] - dot_hi(dk, b_im, ((0,), (0,)))
            d_cre = d_cre + der[tau] * w_r[tau] + dei[tau] * w_i[tau]
            d_cim = d_cim - der[tau] * w_i[tau] + dei[tau] * w_r[tau]
            dwr[tau] = dwr[tau] + _colsum(der[tau] * c_re + dei[tau] * c_im)
            dwi[tau] = dwi[tau] + _colsum(dei[tau] * c_re - der[tau] * c_im)
        a_r, a_i = ar_ref[0], ai_ref[0]
        d_ar = jnp.zeros((1, sb), F32)
        d_ai = jnp.zeros((1, sb), F32)
        for tau in range(CHUNK, 0, -1):
            d_ar = d_ar + dwr[tau] * w_r[tau - 1] + dwi[tau] * w_i[tau - 1]
            d_ai = d_ai - dwr[tau] * w_i[tau - 1] + dwi[tau] * w_r[tau - 1]
            dwr[tau - 1], dwi[tau - 1] = (dwr[tau - 1] + dwr[tau] * a_r + dwi[tau] * a_i,
                                          dwi[tau - 1] - dwr[tau] * a_i + dwi[tau] * a_r)
        dbre_ref[0] = d_bre
        dbim_ref[0] = d_bim
        dcre_ref[0] = d_cre
        dcim_ref[0] = d_cim
        da_ref[0, :, 0:sb] = d_ar
        da_ref[0, :, sb:2 * sb] = d_ai

    mat = pl.BlockSpec((1, LANES, sb), lambda g: (g, 0, 0))
    pw = pl.BlockSpec((1, CHUNK + 1, sb), lambda g: (g, 0, 0))
    one = pl.BlockSpec((1, 1, sb), lambda g: (g, 0, 0))
    two = pl.BlockSpec((1, 1, 2 * sb), lambda g: (g, 0, 0))
    big = pl.BlockSpec((1, n_rows, n_rows), lambda g: (g, 0, 0))
    big2 = pl.BlockSpec((1, n_rows, 2 * sb), lambda g: (g, 0, 0))
    mshape = jax.ShapeDtypeStruct((ngb, LANES, sb), F32)
    outs, extra = _call(
        body, name=name, grid=(ngb,),
        out_shape=[mshape, mshape, mshape, mshape, jax.ShapeDtypeStruct((ngb, 1, 2 * sb), F32)],
        in_specs=[mat, mat, mat, mat, pw, pw, one, one, big, big2, big2, two],
        out_specs=[mat, mat, mat, mat, two],
        scratch_shapes=[], args=(bre, bim, cre, cim, wr, wi, ar, ai, dt, dbp, dcp, da8), hosted=hosted)
    return (*outs, extra)


def _shift_rows(xv, edge, rows, n_rows, down):
    if down:
        return jnp.where(rows == 0, edge, pltpu.roll(xv, 1, 0))
    return jnp.where(rows == n_rows - 1, edge, pltpu.roll(xv, n_rows - 1, 0))


def _rows_of_tokens(tok_ref, conv_scr, rb):
    conv_scr[...] = tok_ref[0].astype(F32)
    return jnp.concatenate([conv_scr[pl.ds(j, rb, stride=CHUNK), :] for j in range(CHUNK)], axis=1).astype(BF16)


def _tokens_of_rows(val, tok_ref, conv_scr, rb):
    for j in range(CHUNK):
        conv_scr[pl.ds(j, rb, stride=CHUNK), :] = val[:, j * LANES:(j + 1) * LANES]
    tok_ref[0] = conv_scr[...].astype(BF16)


def _s5_row_block(n_seq, target=416):
    n_rows = n_seq // CHUNK
    best = 16
    for rb in range(16, min(target, n_rows) + 1, 16):
        if n_rows % rb == 0:
            best = rb
    assert n_rows % best == 0
    return best


def _s5_fwd(useq, t_op, bp, cp, tab, *, reverse, name):
    ngb, n_seq, _ = useq.shape
    sb = bp.shape[2] // 2
    width = CHUNK * LANES
    rb = _s5_row_block(n_seq)
    tbk = rb * CHUNK
    steps = n_seq // tbk

    def blk(i):
        return (steps - 1 - i) if reverse else i

    def body(u_ref, t_ref, b_ref, c_ref, tab_ref, y_ref, hp_ref, h_scr, conv_scr, carry_scr):
        i = pl.program_id(1)

        @pl.when(i == 0)
        def _():
            carry_scr[...] = jnp.zeros_like(carry_scr)

        enter = carry_scr[0:1, :]
        uv = _rows_of_tokens(u_ref, conv_scr, rb)
        h_scr[...] = _dot(uv, b_ref[0])
        cr, ci = _scan_block(h_scr, tab_ref.at[0], carry_scr[:, 0:sb], carry_scr[:, sb:2 * sb],
                             reverse=reverse, tb=rb, sb=sb)
        carry_scr[:, 0:sb] = cr
        carry_scr[:, sb:2 * sb] = ci
        rows = lax.broadcasted_iota(jnp.int32, (rb, 1), 0)
        hprev = _shift_rows(h_scr[...], enter, rows, rb, down=not reverse)
        hp_ref[0] = hprev
        _tokens_of_rows(_dot(uv, t_ref[0]) + _dot_nt(hprev.astype(BF16), c_ref[0]), y_ref, conv_scr, rb)

    op = pl.BlockSpec((1, width, width), lambda g, i: (g, 0, 0))
    op2 = pl.BlockSpec((1, width, 2 * sb), lambda g, i: (g, 0, 0))
    tok = pl.BlockSpec((1, tbk, LANES), lambda g, i: (g, blk(i), 0))
    return pl.pallas_call(
        body, name=name, grid=(ngb, steps),
        out_shape=(jax.ShapeDtypeStruct((ngb, n_seq, LANES), BF16),
                   jax.ShapeDtypeStruct((ngb, n_seq // CHUNK, 2 * sb), F32)),
        in_specs=[tok, op, op2, op2, pl.BlockSpec((1, 8, SUBLANES, sb), lambda g, i: (g, 0, 0, 0))],
        out_specs=(tok, pl.BlockSpec((1, rb, 2 * sb), lambda g, i: (g, blk(i), 0))),
        scratch_shapes=[pltpu.VMEM((rb, 2 * sb), F32), pltpu.VMEM((tbk, LANES), F32),
                        pltpu.VMEM((SUBLANES, 2 * sb), F32)],
        compiler_params=_params(2),
    )(useq, t_op, bp, cp, tab)


def _s5_bwd(useq, dy, hprev, t_op, bp, cp, tab_adj, *, reverse, name, hosted=None):
    ngb, n_seq, _ = useq.shape
    sb = bp.shape[2] // 2
    width = CHUNK * LANES
    rb = _s5_row_block(n_seq)
    tbk = rb * CHUNK
    steps = n_seq // tbk

    def fwd_step(i):
        return steps - 1 - i

    def blk(i):
        s = fwd_step(i)
        return (steps - 1 - s) if reverse else s

    def body(u_ref, dy_ref, hp_ref, t_ref, b_ref, c_ref, taba_ref, du_ref, dt_ref, db_ref, dc_ref, da_ref,
             lam_scr, conv_scr, lcarry_scr, gedge_scr, da_scr):
        i = pl.program_id(1)
        first = i == 0

        _zero_first(first, lcarry_scr, gedge_scr, da_scr, dt_ref, db_ref, dc_ref)
        rows = lax.broadcasted_iota(jnp.int32, (rb, 1), 0)
        uv = _rows_of_tokens(u_ref, conv_scr, rb)
        dyv = _rows_of_tokens(dy_ref, conv_scr, rb)
        gy = _dot(dyv, c_ref[0])
        edge = gy[rb - 1:rb, :] if reverse else gy[0:1, :]
        lam_scr[...] = _shift_rows(gy, gedge_scr[...], rows, rb, down=reverse)
        gedge_scr[...] = edge
        lr, li = _scan_block(lam_scr, taba_ref.at[0], lcarry_scr[:, 0:sb], lcarry_scr[:, sb:2 * sb],
                             reverse=not reverse, tb=rb, sb=sb)
        lcarry_scr[:, 0:sb] = lr
        lcarry_scr[:, sb:2 * sb] = li

        lam = lam_scr[...]
        lam_bf = lam.astype(BF16)
        _tokens_of_rows(_dot_nt(dyv, t_ref[0]) + _dot_nt(lam_bf, b_ref[0]), du_ref, conv_scr, rb)
        _acc(dt_ref.at[0], _dot_tn(uv, dyv))
        _acc(db_ref.at[0], _dot_tn(uv, lam_bf))
        _acc(dc_ref.at[0], _dot_tn(dyv, hp_ref[0].astype(BF16)))
        lam_r, lam_i = lam[:, 0:sb], lam[:, sb:2 * sb]
        hp_r, hp_i = hp_ref[0, :, 0:sb], hp_ref[0, :, sb:2 * sb]
        da_scr[:, 0:sb] += _colsum(lam_r * hp_r + lam_i * hp_i)
        da_scr[:, sb:2 * sb] += _colsum(lam_i * hp_r - lam_r * hp_i)

        @pl.when(i == steps - 1)
        def _():
            da_ref[0] = da_scr[...]

    op = pl.BlockSpec((1, width, width), lambda g, i: (g, 0, 0))
    op2 = pl.BlockSpec((1, width, 2 * sb), lambda g, i: (g, 0, 0))
    tabs = pl.BlockSpec((1, 8, SUBLANES, sb), lambda g, i: (g, 0, 0, 0))
    tok = pl.BlockSpec((1, tbk, LANES), lambda g, i: (g, blk(i), 0))
    outs, extra = _call(
        body, name=name, grid=(ngb, steps),
        out_shape=[jax.ShapeDtypeStruct((ngb, n_seq, LANES), BF16),
                   jax.ShapeDtypeStruct((ngb, width, width), F32),
                   jax.ShapeDtypeStruct((ngb, width, 2 * sb), F32),
                   jax.ShapeDtypeStruct((ngb, width, 2 * sb), F32),
                   jax.ShapeDtypeStruct((ngb, 1, 2 * sb), F32)],
        in_specs=[tok, tok, pl.BlockSpec((1, rb, 2 * sb), lambda g, i: (g, blk(i), 0)), op, op2, op2, tabs],
        out_specs=[tok, op, op2, op2, pl.BlockSpec((1, 1, 2 * sb), lambda g, i: (g, 0, 0))],
        scratch_shapes=[pltpu.VMEM((rb, 2 * sb), F32), pltpu.VMEM((tbk, LANES), F32),
                        pltpu.VMEM((SUBLANES, 2 * sb), F32), pltpu.VMEM((1, 2 * sb), F32), pltpu.VMEM((1, 2 * sb), F32)],
        args=(useq, dy, hprev, t_op, bp, cp, tab_adj), hosted=hosted)
    return (*outs, extra)


def _glu_loss(useq, yf, yb, z, xhat0, ln0, gt, d_vec, w_glu, b_glu, w_out, ln1, target, *, offs, dy_rows, tb, name):
    ngb = useq.shape[0]
    n_tok, d_model = xhat0.shape
    e = ngb * LANES
    tb = min(tb, n_tok)
    assert all(off % tb == 0 for off in offs) and all(off % tb == 0 for _, off in dy_rows)
    nz = z.shape[0]

    def body(u_ref, yf_ref, yb_ref, z_ref, xh0_ref, g0_ref, b0_ref, gt_ref, d_ref, wg_ref, bg_ref, wo_ref, g1_ref,
             b1_ref, t_ref, loss_ref, dxr_ref, do_ref, gz_ref, gg_ref, dq_ref, dz_ref, dyf_ref, dyb_ref, dg1_ref, db1_ref,
             dgt_ref, dbg_ref, dd_ref, loss_scr, yl_scr, th_scr, s_scr, dg_scr):
        i = pl.program_id(0)
        _zero_first(i == 0, loss_scr, dg1_ref, db1_ref, dgt_ref, dbg_ref, dd_ref)
        zw = e // nz
        cs = min(512, zw)

        def z_slab(c0):
            return z_ref[c0 // zw, :, c0 % zw:c0 % zw + cs].astype(F32)

        for q in range(ngb):
            sl = slice(q * LANES, (q + 1) * LANES)
            yl = d_ref[:, sl] * u_ref[q].astype(F32) + yf_ref[q].astype(F32) + yb_ref[q].astype(F32)
            th = jnp.tanh(GELU_K * (yl + GELU_C * yl * yl * yl))
            yl_scr[:, sl] = yl
            th_scr[:, sl] = th
            gg_ref[:, sl] = (0.5 * yl * (1.0 + th)).astype(BF16)
        s_scr[...] = _sigmoid(_dot(gg_ref[...], wg_ref[...]) + bg_ref[...])
        for c0 in range(0, e, cs):
            sl = slice(c0, c0 + cs)
            zf = z_slab(c0)
            g2 = 0.5 * yl_scr[:, sl] * (1.0 + th_scr[:, sl]) * s_scr[:, sl]
            gz_ref[:, sl] = (g2 * (zf * _sigmoid(zf))).astype(BF16)
        o = _dot(gz_ref[...], wo_ref[...])
        x1 = xh0_ref[...] * g0_ref[...] + b0_ref[...]
        r = DN_ALPHA * x1 + gt_ref[...] * o
        rc = r - _rowmean(r)
        rstd = lax.rsqrt(_rowmean(rc * rc) + LN_EPS)
        xh = rc * rstd
        err = xh * g1_ref[...] + b1_ref[...] - t_ref[...]
        _acc(loss_scr, _colsum(err * err))
        dy = err * (1.0 / d_model)
        _acc(dg1_ref, _colsum(dy * xh))
        _acc(db1_ref, _colsum(dy))
        dxh = dy * g1_ref[...]
        dr = rstd * (dxh - _rowmean(dxh) - xh * _rowmean(dxh * xh))
        dxr_ref[...] = DN_ALPHA * dr
        _acc(dgt_ref, _colsum(dr * o))
        do_bf = (dr * gt_ref[...]).astype(BF16)
        do_ref[...] = do_bf
        dg_scr[...] = _dot_nt(do_bf, wo_ref[...])
        for c0 in range(0, e, cs):
            sl = slice(c0, c0 + cs)
            zf = z_slab(c0)
            sz = _sigmoid(zf)
            g = 0.5 * yl_scr[:, sl] * (1.0 + th_scr[:, sl])
            s = s_scr[:, sl]
            dgz = dg_scr[:, sl]
            dg2 = dgz * (zf * sz)
            dz_ref[:, sl] = (dgz * (g * s) * (sz * (1.0 + zf * (1.0 - sz)))).astype(BF16)
            dq = dg2 * g * s * (1.0 - s)
            _acc(dbg_ref.at[:, sl], _colsum(dq))
            dq_ref[:, sl] = dq.astype(BF16)
            dg_scr[:, sl] = dg2 * s
        dg_scr[...] += _dot_nt(dq_ref[...], wg_ref[...])
        for q in range(ngb):
            sl = slice(q * LANES, (q + 1) * LANES)
            yl = yl_scr[:, sl]
            th = th_scr[:, sl]
            dgelu = 0.5 * (1.0 + th) + 0.5 * yl * (1.0 - th * th) * (GELU_K * (1.0 + 3.0 * GELU_C * yl * yl))
            dyl = dg_scr[:, sl] * dgelu
            _acc(dd_ref.at[:, sl], _colsum(dyl * u_ref[q].astype(F32)))
            dyf_ref[q] = dyl.astype(BF16)
            dyb_ref[q] = dyl.astype(BF16)

        @pl.when(i == pl.num_programs(0) - 1)
        def _():
            loss_ref[...] = (0.5 / d_model) * jnp.sum(loss_scr[...], axis=1, keepdims=True)

    vec = pl.BlockSpec((1, d_model), lambda i: (0, 0))
    evec = pl.BlockSpec((1, e), lambda i: (0, 0))
    tok = pl.BlockSpec((tb, d_model), lambda i: (i, 0))
    wide = pl.BlockSpec((tb, e), lambda i: (i, 0))
    def gblk(off):
        return pl.BlockSpec((ngb, tb, LANES), functools.partial(lambda i, ob: (0, i + ob, 0), ob=off // tb))

    once = dict(pipeline_mode=pl.Buffered(1))
    tok_f = jax.ShapeDtypeStruct((n_tok, d_model), F32)
    tok_b = jax.ShapeDtypeStruct((n_tok, d_model), BF16)
    wide_b = jax.ShapeDtypeStruct((n_tok, e), BF16)
    vec_f = jax.ShapeDtypeStruct((1, d_model), F32)
    evec_f = jax.ShapeDtypeStruct((1, e), F32)
    return pl.pallas_call(
        body, name=name, grid=(n_tok // tb,),
        out_shape=(jax.ShapeDtypeStruct((1, 1), F32), tok_f, tok_b, wide_b, wide_b, wide_b, wide_b,
                   *[jax.ShapeDtypeStruct((ngb, total, LANES), BF16) for total, _ in dy_rows],
                   vec_f, vec_f, vec_f, evec_f, evec_f),
        in_specs=[gblk(offs[0]), gblk(offs[1]), gblk(offs[2]),
                  pl.BlockSpec((nz, tb, e // nz), lambda i: (0, i, 0)), tok, vec, vec, vec, evec,
                  pl.BlockSpec((e, e), lambda i: (0, 0), **once), evec,
                  pl.BlockSpec((e, d_model), lambda i: (0, 0), **once), vec, vec, tok],
        out_specs=(pl.BlockSpec((1, 1), lambda i: (0, 0)), tok, tok, wide, wide, wide, wide,
                   *[gblk(off) for _, off in dy_rows], vec, vec, vec, evec, evec),
        scratch_shapes=[pltpu.VMEM((1, d_model), F32)] + [pltpu.VMEM((tb, e), F32)] * 4,
        compiler_params=_params(1),
    )(useq, yf, yb, z, xhat0, ln0[0], ln0[1], gt, d_vec, w_glu, b_glu, w_out, ln1[0], ln1[1], target)


def _ssm_inbwd(duf, dub, w, xhat, rstd, ln, sc, gt_prev, f_prev, *, lat, row_f, row_b, tb, name):
    ngb = duf.shape[0]
    e = ngb * LANES
    n_tok, d_model = xhat.shape
    tb = min(tb, n_tok)
    obf, obb = row_f // tb, row_b // tb
    has_lat = lat is not None
    n_w = w.shape[0] if has_lat else w.shape[0] // 2

    def body(*refs):
        if has_lat:
            (duf_ref, dub_ref, dyl_ref, dz_ref, d_ref, dxr_ref, w_ref, xh_ref, rs_ref, g_ref, b_ref, sc_ref, gt_ref,
             f_ref, dp_ref, dr_ref, df_ref, dsc_ref, dsh_ref, dg_ref, db_ref, dgt_ref) = refs
        else:
            (duf_ref, dub_ref, w_ref, xh_ref, rs_ref, g_ref, b_ref, sc_ref, gt_ref, f_ref, dp_ref, dr_ref, df_ref,
             dsc_ref, dsh_ref, dg_ref, db_ref, dgt_ref) = refs
        _zero_first(pl.program_id(0) == 0, dsc_ref, dsh_ref, dg_ref, db_ref, dgt_ref)
        du = (jnp.concatenate([duf_ref[q] for q in range(ngb)], axis=1).astype(F32)
              + jnp.concatenate([dub_ref[q] for q in range(ngb)], axis=1).astype(F32))
        if has_lat:
            du = du + d_ref[...] * jnp.concatenate([dyl_ref[q] for q in range(ngb)], axis=1).astype(F32)
            dp_ref[:, e:2 * e] = dz_ref[...]
        else:
            dp_ref[:, e:2 * e] = jnp.zeros((tb, e), BF16)
        dp_ref[:, 0:e] = du.astype(BF16)
        dh = jnp.zeros((tb, d_model), F32)
        for j in range(n_w):
            dh = dh + _dot(dp_ref[:, j * d_model:(j + 1) * d_model], w_ref[j])
        xh = xh_ref[...]
        x1 = xh * g_ref[...] + b_ref[...]
        dx1 = dh * (1.0 + sc_ref[...])
        if has_lat:
            dx1 = dx1 + dxr_ref[...]
        _acc(dsc_ref, _colsum(dh * x1))
        _acc(dsh_ref, _colsum(dh))
        _acc(dg_ref, _colsum(dx1 * xh))
        _acc(db_ref, _colsum(dx1))
        dxh = dx1 * g_ref[...]
        dr = rs_ref[...] * (dxh - _rowmean(dxh) - xh * _rowmean(dxh * xh))
        dr_ref[...] = dr
        df_ref[...] = (dr * gt_ref[...]).astype(BF16)
        _acc(dgt_ref, _colsum(dr * f_ref[...].astype(F32)))

    vec = pl.BlockSpec((1, d_model), lambda i: (0, 0))
    tok = pl.BlockSpec((tb, d_model), lambda i: (i, 0))
    gblk = pl.BlockSpec((ngb, tb, LANES), lambda i: (0, i, 0))
    in_specs = [pl.BlockSpec((ngb, tb, LANES), lambda i: (0, i + obf, 0)),
                pl.BlockSpec((ngb, tb, LANES), lambda i: (0, i + obb, 0))]
    args = [duf, dub]
    if has_lat:
        in_specs += [gblk, pl.BlockSpec((tb, e), lambda i: (i, 0)), pl.BlockSpec((1, e), lambda i: (0, 0)), tok]
        args += list(lat)
    in_specs += [pl.BlockSpec(w.shape, lambda i: (0, 0, 0)), tok, pl.BlockSpec((tb, 1), lambda i: (i, 0)), vec, vec, vec,
                 vec, tok]
    args += [w, xhat, rstd, ln[0], ln[1], sc, gt_prev, f_prev]
    vec_f = jax.ShapeDtypeStruct((1, d_model), F32)
    return pl.pallas_call(
        body, name=name, grid=(n_tok // tb,),
        out_shape=(jax.ShapeDtypeStruct((n_tok, 2 * e), BF16), jax.ShapeDtypeStruct((n_tok, d_model), F32),
                   jax.ShapeDtypeStruct((n_tok, d_model), BF16), vec_f, vec_f, vec_f, vec_f, vec_f),
        in_specs=in_specs,
        out_specs=(pl.BlockSpec((tb, 2 * e), lambda i: (i, 0)), tok, tok, vec, vec, vec, vec, vec),
        compiler_params=_params(1),
    )(*args)


def _conv_bwd_a(df, w_out_t, p, yc, *, tb, name):
    _, n_tok, e = p.shape
    d_model = df.shape[1]
    tb = min(tb, n_tok)
    cs = _slab_width(e)

    def body(df_ref, wo_ref, bg_ref, z_ref, yc_ref, dbg_ref, dz_ref, dyc_ref):
        dfv = df_ref[...]
        for c0 in range(0, e, cs):
            sl = slice(c0, c0 + cs)
            dgv = _dot(dfv, wo_ref[:, sl])
            zf = z_ref[0, :, sl].astype(F32)
            sz = _sigmoid(zf)
            silu_z = zf * sz
            bg = bg_ref[0, :, sl].astype(F32)
            yc = yc_ref[:, sl].astype(F32)
            dbg_ref[:, sl] = (dgv * yc * silu_z).astype(BF16)
            dyc_ref[:, sl] = (dgv * bg * silu_z).astype(BF16)
            dz_ref[:, sl] = (dgv * bg * yc * (sz * (1.0 + zf * (1.0 - sz)))).astype(BF16)

    wide = pl.BlockSpec((tb, e), lambda i: (i, 0))
    shape = jax.ShapeDtypeStruct((n_tok, e), BF16)
    return pl.pallas_call(
        body, name=name, grid=(n_tok // tb,), out_shape=(shape, shape, shape),
        in_specs=[pl.BlockSpec((tb, d_model), lambda i: (i, 0)), pl.BlockSpec((d_model, e), lambda i: (0, 0)),
                  pl.BlockSpec((1, tb, e), lambda i: (0, i, 0)), pl.BlockSpec((1, tb, e), lambda i: (3, i, 0)), wide],
        out_specs=(wide, wide, wide), compiler_params=_params(1),
    )(df, w_out_t, p, p, yc)


def _conv_bwd_b(dyc, p, dbg, dz, conv_w, *, grid_mode, tb, name, hosted=None):
    _, n_tok, e = p.shape
    eh = e // 2
    if not grid_mode:
        tb = n_tok
    tb = min(tb, n_tok)
    nb = n_tok // tb
    hb = tb // GRID_W
    cs = _slab_width(e)

    def body(*refs):
        if grid_mode:
            dyc_ref, dycp_ref, dycn_ref, cg_ref, v_ref, dbg_ref, dz_ref, cw_ref, dp_ref, dcw_ref = refs
        else:
            dyc_ref, cg_ref, v_ref, dbg_ref, dz_ref, cw_ref, dp_ref, dcw_ref = refs
        i = pl.program_id(0)
        _zero_first(i == 0, dcw_ref)
        rows = lax.broadcasted_iota(jnp.int32, (tb, 1), 0)
        dp_ref[0] = dbg_ref[...]
        dp_ref[3] = dz_ref[...]
        for c0 in range(0, e, cs):
            sl = slice(c0, c0 + cs)
            dyc = dyc_ref[:, sl].astype(F32)
            w = cw_ref[:, sl]
            if grid_mode and c0 >= eh:
                hs = slice(c0 - eh, c0 - eh + cs)
                dprev = jnp.where(i > 0, dycp_ref[:, hs].astype(F32), 0.0)
                dnext = jnp.where(i < nb - 1, dycn_ref[:, hs].astype(F32), 0.0)
                if tb > GRID_W:
                    dm = jnp.concatenate([dprev, dyc[:tb - GRID_W]], axis=0)
                    dpl = jnp.concatenate([dyc[GRID_W:], dnext], axis=0)
                else:
                    dm, dpl = dprev, dnext
            else:
                dm, dpl = _shifted(dyc, rows, GRID_W if grid_mode else tb, tb)
            cg = cg_ref[0, :, sl].astype(F32)
            v = v_ref[0, :, sl].astype(F32)
            u = cg * v
            du = w[0:1] * dpl + w[1:2] * dyc + w[2:3] * dm
            dp_ref[1, :, sl] = (du * v).astype(BF16)
            dp_ref[2, :, sl] = (du * cg).astype(BF16)
            _acc(dcw_ref.at[:, sl], jnp.concatenate([_colsum(u * dpl), _colsum(u * dyc), _colsum(u * dm)], axis=0))

    n_hrows = n_tok // GRID_W
    wide = pl.BlockSpec((tb, e), lambda i: (i, 0))
    in_specs = [wide]
    args = [dyc]
    if grid_mode:
        in_specs += [pl.BlockSpec((GRID_W, eh), lambda i: (jnp.maximum(i * hb - 1, 0), 1)),
                     pl.BlockSpec((GRID_W, eh), lambda i: (jnp.minimum((i + 1) * hb, n_hrows - 1), 1))]
        args += [dyc, dyc]
    in_specs += [pl.BlockSpec((1, tb, e), lambda i: (1, i, 0)), pl.BlockSpec((1, tb, e), lambda i: (2, i, 0)), wide, wide,
                 pl.BlockSpec((3, e), lambda i: (0, 0))]
    args += [p, p, dbg, dz, conv_w]
    outs, extra = _call(
        body, name=name, grid=(nb,),
        out_shape=[jax.ShapeDtypeStruct((4, n_tok, e), BF16), jax.ShapeDtypeStruct((3, e), F32)],
        in_specs=in_specs,
        out_specs=[pl.BlockSpec((4, tb, e), lambda i: (0, i, 0)), pl.BlockSpec((3, e), lambda i: (0, 0))],
        scratch_shapes=[], args=args, hosted=hosted)
    return (*outs, extra)


def _conv_inbwd(dp, w, dr, x, sc, *, tb, name, hosted=None):
    n_chunks, n_tok, e = dp.shape
    d_model = x.shape[1]
    tb = min(tb, n_tok)

    def body(dp_ref, w_ref, dr_ref, x_ref, sc_ref, gx_ref, dsc_ref, dsh_ref):
        _zero_first(pl.program_id(0) == 0, dsc_ref, dsh_ref)
        dh = _dot(dp_ref[0], w_ref[0])
        for k in range(1, n_chunks):
            dh = dh + _dot(dp_ref[k], w_ref[k])
        gx_ref[...] = DN_ALPHA * dr_ref[...] + dh * (1.0 + sc_ref[...])
        _acc(dsc_ref, _colsum(dh * x_ref[...]))
        _acc(dsh_ref, _colsum(dh))

    vec = pl.BlockSpec((1, d_model), lambda i: (0, 0))
    tok = pl.BlockSpec((tb, d_model), lambda i: (i, 0))
    vec_f = jax.ShapeDtypeStruct((1, d_model), F32)
    outs, extra = _call(
        body, name=name, grid=(n_tok // tb,),
        out_shape=[jax.ShapeDtypeStruct((n_tok, d_model), F32), vec_f, vec_f],
        in_specs=[pl.BlockSpec((n_chunks, tb, e), lambda i: (0, i, 0)),
                  pl.BlockSpec((n_chunks, e, d_model), lambda i: (0, 0, 0), pipeline_mode=pl.Buffered(1)),
                  tok, tok, vec],
        out_specs=[tok, vec, vec],
        scratch_shapes=[], args=(dp, w, dr, x, sc), hosted=hosted)
    return (*outs, extra)


def _wgrad(a, b, *, n_chunks, tm, tl, init=None, name):
    n_tok, m = a.shape
    tl = min(tl, n_tok)
    chunked = b.ndim == 3
    cw = b.shape[2] if chunked else b.shape[1] // n_chunks
    has_init = init is not None

    def body(*refs):
        if has_init:
            a_ref, b_ref, init_ref, o_ref = refs
        else:
            a_ref, b_ref, o_ref = refs
        @pl.when(pl.program_id(2) == 0)
        def _():
            o_ref[0] = init_ref[0] if has_init else jnp.zeros_like(o_ref[0])

        o_ref[0] += _dot_tn(a_ref[...], b_ref[0] if chunked else b_ref[...])

    o_spec = pl.BlockSpec((1, tm, cw), lambda jm, jc, l: (jc, jm, 0))
    b_spec = (pl.BlockSpec((1, tl, cw), lambda jm, jc, l: (jc, l, 0)) if chunked
              else pl.BlockSpec((tl, cw), lambda jm, jc, l: (l, jc)))
    init_spec = pl.BlockSpec((1, tm, cw), lambda jm, jc, l: (jc, jm, 0), pipeline_mode=pl.Buffered(1))
    in_specs = [pl.BlockSpec((tl, tm), lambda jm, jc, l: (l, jm)), b_spec] + ([init_spec] if has_init else [])
    args = (a, b) + ((init,) if has_init else ())
    return pl.pallas_call(
        body, name=name, grid=(m // tm, n_chunks, n_tok // tl),
        out_shape=jax.ShapeDtypeStruct((n_chunks, m, cw), F32),
        in_specs=in_specs, out_specs=o_spec, compiler_params=_params(3),
    )(*args)


def _block_diag(t, ngb):
    g, p, n = t.shape
    gpb = g // ngb
    eye = jnp.eye(gpb, dtype=t.dtype)
    return jnp.einsum("bgpn,gh->bgphn", t.reshape(ngb, gpb, p, n), eye).reshape(ngb, gpb * p, gpb * n)


def _block_diag_t(mat, g, p, n):
    ngb = mat.shape[0]
    gpb = g // ngb
    eye = jnp.eye(gpb, dtype=mat.dtype)
    return jnp.einsum("bgphn,gh->bgpn", mat.reshape(ngb, gpb, p, gpb, n), eye).reshape(g, p, n)


def _scan_tables(pw_r, pw_i, ngb, reverse):
    _, g, n = pw_r.shape
    sb = g * n // ngb
    rows = jnp.arange(SUBLANES)
    kinds = []
    for step in (1, 2, 4):
        mask = ((rows < SUBLANES - step) if reverse else (rows >= step)).astype(F32)
        for part in (pw_r[step - 1], pw_i[step - 1]):
            kinds.append(part.reshape(ngb, 1, sb) * mask[None, :, None])
    for part in (pw_r, pw_i):
        pw = part[::-1] if reverse else part
        kinds.append(jnp.transpose(pw.reshape(SUBLANES, ngb, sb), (1, 0, 2)))
    return jnp.stack(kinds, axis=1)


def _flat(parts):
    return jnp.concatenate([p.reshape(-1) for p in parts])


def _unflat(vec, shapes):
    out, off = [], 0
    for s in shapes:
        size = math.prod(s)
        out.append(vec[off:off + size].reshape(s))
        off += size
    return out


def kernel(x, c, ctx, c_ctx, ada_w, ada_b, ln_g, ln_b, conv_w_in, conv_w, conv_w_out, ssm_w_in, ssm_lam_re, ssm_lam_im, ssm_log_step, ssm_b_re, ssm_b_im, ssm_c_re, ssm_c_im, ssm_d, ssm_w_glu, ssm_b_glu, ssm_w_out, loss_target, m_c_ctx, m_ada_w, m_ada_b, m_ln_g, m_ln_b, m_conv_w_in, m_conv_w, m_conv_w_out, m_ssm_w_in, m_ssm_lam_re, m_ssm_lam_im, m_ssm_log_step, m_ssm_b_re, m_ssm_b_im, m_ssm_c_re, m_ssm_c_im, m_ssm_d, m_ssm_w_glu, m_ssm_b_glu, m_ssm_w_out, v_c_ctx, v_ada_w, v_ada_b, v_ln_g, v_ln_b, v_conv_w_in, v_conv_w, v_conv_w_out, v_ssm_w_in, v_ssm_lam_re, v_ssm_lam_im, v_ssm_log_step, v_ssm_b_re, v_ssm_b_im, v_ssm_c_re, v_ssm_c_im, v_ssm_d, v_ssm_w_glu, v_ssm_b_glu, v_ssm_w_out):
    weights = dict(c_ctx=c_ctx, ada_w=ada_w, ada_b=ada_b, ln_g=ln_g, ln_b=ln_b, conv_w_in=conv_w_in, conv_w=conv_w,
                   conv_w_out=conv_w_out, ssm_w_in=ssm_w_in, ssm_lam_re=ssm_lam_re, ssm_lam_im=ssm_lam_im,
                   ssm_log_step=ssm_log_step, ssm_b_re=ssm_b_re, ssm_b_im=ssm_b_im, ssm_c_re=ssm_c_re,
                   ssm_c_im=ssm_c_im, ssm_d=ssm_d, ssm_w_glu=ssm_w_glu, ssm_b_glu=ssm_b_glu, ssm_w_out=ssm_w_out)
    mom_m = dict(c_ctx=m_c_ctx, ada_w=m_ada_w, ada_b=m_ada_b, ln_g=m_ln_g, ln_b=m_ln_b, conv_w_in=m_conv_w_in,
                 conv_w=m_conv_w, conv_w_out=m_conv_w_out, ssm_w_in=m_ssm_w_in, ssm_lam_re=m_ssm_lam_re,
                 ssm_lam_im=m_ssm_lam_im, ssm_log_step=m_ssm_log_step, ssm_b_re=m_ssm_b_re, ssm_b_im=m_ssm_b_im,
                 ssm_c_re=m_ssm_c_re, ssm_c_im=m_ssm_c_im, ssm_d=m_ssm_d, ssm_w_glu=m_ssm_w_glu,
                 ssm_b_glu=m_ssm_b_glu, ssm_w_out=m_ssm_w_out)
    mom_v = dict(c_ctx=v_c_ctx, ada_w=v_ada_w, ada_b=v_ada_b, ln_g=v_ln_g, ln_b=v_ln_b, conv_w_in=v_conv_w_in,
                 conv_w=v_conv_w, conv_w_out=v_conv_w_out, ssm_w_in=v_ssm_w_in, ssm_lam_re=v_ssm_lam_re,
                 ssm_lam_im=v_ssm_lam_im, ssm_log_step=v_ssm_log_step, ssm_b_re=v_ssm_b_re, ssm_b_im=v_ssm_b_im,
                 ssm_c_re=v_ssm_c_re, ssm_c_im=v_ssm_c_im, ssm_d=v_ssm_d, ssm_w_glu=v_ssm_w_glu,
                 ssm_b_glu=v_ssm_b_glu, ssm_w_out=v_ssm_w_out)
    names = list(weights)

    n_lat, d_model = x.shape[1], x.shape[2]
    n_ctx = ctx.shape[1]
    e = 2 * d_model
    n_grp, n_state, grp = ssm_lam_re.shape[2], ssm_lam_re.shape[3], ssm_b_re.shape[4]
    ngb = e // LANES
    ws = ada_w.shape[2]
    tb_tok = min(512, n_lat)
    n_seq = n_ctx + n_lat
    tb_glu = math.gcd(256, n_ctx)
    chip = 2 * lax.axis_index("x") + lax.axis_index("y")
    me = 2 * chip + lax.axis_index("c")
    chips, everyone, pair = ("x", "y"), MESH_AXES, ("c",)

    x2, ctx2, tgt2 = x[0], ctx[0], loss_target[0]

    wc_in_own = conv_w_in[0].astype(BF16)
    later_weights = _Hosted([(w[0].astype(BF16), chips, False) for w in (conv_w_out, ssm_w_in, ssm_w_glu, ssm_w_out)])
    small_full = _exchange(_flat([conv_w[0], ssm_d[0], ssm_b_glu[0]]).reshape(1, -1), chips, False, "ag_small")
    es = conv_w.shape[2]
    conv_w_full = jnp.transpose(small_full[:, 0, :3 * es].reshape(4, 3, es), (1, 0, 2)).reshape(3, e)
    d_full = small_full[:, 0, 3 * es:4 * es].reshape(1, e)
    b_glu_full = small_full[:, 0, 4 * es:5 * es].reshape(1, e)

    c_all = _exchange(c, everyone, False, "ag_c").reshape(8, d_model)
    cc2 = c_ctx.reshape(1, d_model)
    b_sh = lax.dynamic_slice_in_dim(ada_b, chip * ws, ws, axis=1).reshape(DEPTH, 1, ws)
    m_sh = _ada_fwd(c_all, cc2, ada_w, b_sh)
    m_all = _exchange(m_sh, chips, False, "ag_mod")
    m_full = jnp.transpose(m_all, (1, 2, 0, 3)).reshape(DEPTH, 16, 3 * d_model)
    m_lat = lax.dynamic_slice_in_dim(m_full, me, 1, axis=1)
    m_ctx = m_full[:, 8:9]

    def mods(m, i):
        return m[i, :, 0:d_model], m[i, :, d_model:2 * d_model], m[i, :, 2 * d_model:3 * d_model]

    sh0, sc0, gt0 = mods(m_lat, 0)
    sh1, sc1, gt1 = mods(m_lat, 1)
    shc0, scc0, gtc0 = mods(m_ctx, 0)
    shc1, scc1, _ = mods(m_ctx, 1)
    ln0 = (ln_g[0:1], ln_b[0:1])
    ln1 = (ln_g[1:2], ln_b[1:2])

    def lam_view(t):
        return jnp.transpose(t[0], (0, 2, 1)).reshape(2 * n_state, n_grp)

    def lam_back(t):
        return jnp.transpose(t.reshape(2, n_state, n_grp), (0, 2, 1)).reshape(ssm_lam_re.shape)

    def b_view(t):
        return jnp.transpose(t[0], (0, 2, 3, 1)).reshape(2 * n_state * grp, n_grp)

    def b_back(t):
        return jnp.transpose(t.reshape(2, n_state, grp, n_grp), (0, 3, 1, 2)).reshape(ssm_b_re.shape)

    def c_view(t):
        return jnp.transpose(t[0], (0, 2, 3, 1)).reshape(2 * grp * n_state, n_grp)

    def c_back(t):
        return jnp.transpose(t.reshape(2, grp, n_state, n_grp), (0, 3, 1, 2)).reshape(ssm_c_re.shape)

    def channel_major(t):
        return jnp.transpose(t.reshape(2 * n_state, grp, n_grp), (1, 0, 2))

    def by_group(t):
        return jnp.transpose(t.reshape(t.shape[0], 2, n_state, n_grp), (0, 1, 3, 2))

    lam_re2, lam_im2, log_step2 = lam_view(ssm_lam_re), lam_view(ssm_lam_im), ssm_log_step[0]
    b_re_t, b_im_t = channel_major(b_view(ssm_b_re)), channel_major(b_view(ssm_b_im))
    pw_r, pw_i, pq_r, pq_i, bbr, bbi = _zoh_fwd(lam_re2, lam_im2, log_step2, b_re_t, b_im_t)
    sbk = n_grp * n_state // ngb
    pw_r, pw_i, pq_r, pq_i = (by_group(t) for t in (pw_r, pw_i, pq_r, pq_i))
    bbr_g = jnp.transpose(by_group(bbr), (1, 2, 0, 3))
    bbi_g = jnp.transpose(by_group(bbi), (1, 2, 0, 3))

    def power_rows(pw, r, first):
        full = jnp.concatenate([jnp.full((1, n_grp, n_state), first, F32), pw[:, r]], axis=0)
        return jnp.transpose(full.reshape(CHUNK + 1, ngb, sbk), (1, 0, 2))

    s5 = []
    for r in range(2):
        prm = dict(bre=_block_diag(bbr_g[r], ngb), bim=_block_diag(bbi_g[r], ngb),
                   cre=_block_diag(ssm_c_re[0, r], ngb), cim=_block_diag(ssm_c_im[0, r], ngb),
                   wr=power_rows(pw_r, r, 1.0), wi=power_rows(pw_i, r, 0.0))
        half_rows = wc_in_own[r * (d_model // 2):(r + 1) * (d_model // 2)]
        t_op, bp_op, cp_op, (wc_in_half,) = _s5_ops(
            prm["bre"], prm["bim"], prm["cre"], prm["cim"], prm["wr"], prm["wi"], reverse=(r == 1),
            name=f"l1_s5_ops{r}", hosted=_Hosted([(half_rows, chips, False)]))
        s5.append(dict(
            prm, t=t_op, bp=bp_op, cp=cp_op, wc_in_half=wc_in_half,
            tab=_scan_tables(pq_r[:, r], pq_i[:, r], ngb, reverse=(r == 1)),
            tab_adj=_scan_tables(pq_r[:, r], -pq_i[:, r], ngb, reverse=(r == 0))))
    wc_in = jnp.concatenate([s5[0]["wc_in_half"], s5[1]["wc_in_half"]], axis=1)

    p0, h0, gathered = _inproj(x2, sc0, sh0, wc_in, tb=min(1024, n_lat), name="l0_inproj", hosted=later_weights)
    wc_out, ws_in, w_glu, ws_out = gathered
    wc_out, w_glu, ws_out = wc_out.reshape(e, d_model), w_glu.reshape(e, e), ws_out.reshape(e, d_model)
    wc_in_t, ws_in_t, wc_out_t = jnp.transpose(wc_in, (0, 2, 1)), jnp.transpose(ws_in, (0, 2, 1)), wc_out.T
    pc0, hc0 = _inproj(ctx2, scc0, shc0, wc_in, tb=tb_tok, name="l0_inproj_ctx")
    xhat0, rstd0, g0, yc0, f0 = _convgate(p0, x2, gt0, conv_w_full, wc_out, *ln0, grid_mode=True, tb=tb_tok, name="l0_conv")
    chat0, crstd0, gc0, ycc0, fc0 = _convgate(pc0, ctx2, gtc0, conv_w_full, wc_out, *ln0, grid_mode=False, tb=tb_tok,
                                              name="l0_conv_ctx")

    seq_rows = [(n_seq, n_ctx), (n_seq, 0)]
    useq_f, useq_b, h1 = _inproj(xhat0, sc1, sh1, ws_in[0:2], lnaff=ln0, tb=math.gcd(tb_tok, n_ctx), gb_rows=seq_rows,
                                 name="l1_inproj_u")
    z1, _ = _inproj(xhat0, sc1, sh1, ws_in[2:4], lnaff=ln0, tb=tb_tok, name="l1_inproj_z")
    uc, hc1 = _inproj(chat0, scc1, shc1, ws_in[0:2], lnaff=ln0, tb=tb_tok, gb_rows=[(n_ctx, 0)], name="l1_inproj_ctx")
    useq = [useq_f.at[:, 0:n_ctx].set(uc), useq_b.at[:, n_lat:].set(uc)]
    y_dir, hp_dir = [], []
    for r in range(2):
        yr, hcr = _s5_fwd(useq[r], s5[r]["t"], s5[r]["bp"], s5[r]["cp"], s5[r]["tab"], reverse=(r == 1),
                          name=f"l1_s5_fwd{r}")
        y_dir.append(yr)
        hp_dir.append(hcr)

    (loss, dxres, do1, gz1, gg1, dq1, dz1, dy_f, dy_b, dg1, db1, dgt1, dbglu, dd) = _glu_loss(
        useq[0], y_dir[0], y_dir[1], z1, xhat0, ln0, gt1, d_full, w_glu, b_glu_full, ws_out, ln1, tgt2,
        offs=(n_ctx, n_ctx, 0), dy_rows=seq_rows, tb=tb_glu, name="l1_glu_loss")
    no_dy = jnp.zeros((ngb, n_ctx, LANES), BF16)
    dy_dir = [dy_f.at[:, 0:n_ctx].set(no_dy), dy_b.at[:, n_lat:].set(no_dy)]

    tl = min(1024, n_lat)

    def owner_slices(name, full):
        w = weights[name]
        return full.reshape(8, math.prod(w.shape[:-1]) // 2, w.shape[-1])

    def scatter(named):
        return _Hosted([(owner_slices(name, full), everyone, True) for name, full in named])

    def siblings(names):
        return _Hosted([(_sum_parts(rs_parts[name], "sum_" + name), pair, False) for name in names])

    rs_parts, both_halves = {}, {}

    gw_glu = _wgrad(gg1, dq1, n_chunks=1, tm=e // 2, tl=tl, name="wg_glu")
    gw_ssm_out = _wgrad(gz1, do1, n_chunks=1, tm=e, tl=tl, name="wg_ssm_out")
    du_dir, s5_grads = [], []
    for r in range(2):
        if r == 0:
            hosted = scatter([("ssm_w_glu", gw_glu), ("ssm_w_out", gw_ssm_out)])
        else:
            hosted = siblings(["ssm_w_glu", "ssm_w_out"])
        dur, dt_op, dbp_op, dcp_op, da8, extra = _s5_bwd(useq[r], dy_dir[r], hp_dir[r], s5[r]["t"], s5[r]["bp"],
                                                         s5[r]["cp"], s5[r]["tab_adj"], reverse=(r == 1),
                                                         name=f"l1_s5_bwd{r}", hosted=hosted)
        if r == 0:
            rs_parts["ssm_w_glu"], rs_parts["ssm_w_out"] = extra
        else:
            both_halves["ssm_w_glu"], both_halves["ssm_w_out"] = extra
        du_dir.append(dur)
        prm = s5[r]
        s5_grads.append(functools.partial(
            _s5_ops_bwd, prm["bre"], prm["bim"], prm["cre"], prm["cim"], prm["wr"], prm["wi"], prm["wr"][:, 1:2],
            prm["wi"][:, 1:2], dt_op, dbp_op, dcp_op, da8, reverse=(r == 1), name=f"l1_s5_ops_bwd{r}"))
    dp1, dr0, df0, dsc1, dsh1, dg0, db0, dgt0 = _ssm_inbwd(
        du_dir[0], du_dir[1], ws_in_t, xhat0, rstd0, ln0, sc1, gt0, f0, lat=(dy_dir[1], dz1, d_full, dxres),
        row_f=n_ctx, row_b=0, tb=tb_glu, name="l1_inbwd")
    dpc1, drc0, dfc0, dscc1, dshc1, dgc0, dbc0, dgtc0 = _ssm_inbwd(
        du_dir[0], du_dir[1], ws_in_t, chat0, crstd0, ln0, scc1, gtc0, fc0, lat=None,
        row_f=0, row_b=n_lat, tb=n_ctx, name="l1_inbwd_ctx")

    def conv_backward(df, p, yc, dr, xin, sc, grid_mode, tag, hosted_b=None, hosted_in=None):
        dbg, dz, dyc = _conv_bwd_a(df, wc_out_t, p, yc, tb=tb_tok, name="l0_bwd_a" + tag)
        dp, dcw, extra_b = _conv_bwd_b(dyc, p, dbg, dz, conv_w_full, grid_mode=grid_mode, tb=tb_glu,
                                       name="l0_bwd_b" + tag, hosted=hosted_b)
        gx, dsc, dsh, extra_in = _conv_inbwd(dp, wc_in_t, dr, xin, sc, tb=tb_tok, name="l0_inbwd" + tag,
                                             hosted=None if hosted_in is None else hosted_in(dp, extra_b))
        return dp, dcw, gx, dsc, dsh, extra_b, extra_in

    dpc0, dcwc0, _, dscc0, dshc0, _, _ = conv_backward(dfc0, pc0, ycc0, drc0, ctx2, scc0, False, "_ctx")
    gw_conv_out = _wgrad(g0, df0, n_chunks=1, tm=e, tl=tl, name="wg_conv_out",
                         init=_wgrad(gc0, dfc0, n_chunks=1, tm=e, tl=tl, name="wg_conv_out_ctx"))
    gw_ssm_in = _wgrad(h1, dp1, n_chunks=4, tm=d_model, tl=tl, name="wg_ssm_in",
                       init=_wgrad(hc1, dpc1, n_chunks=4, tm=d_model, tl=tl, name="wg_ssm_in_ctx"))
    gw_conv_in_ctx = _wgrad(hc0, dpc0, n_chunks=4, tm=d_model, tl=tl, name="wg_conv_in_ctx")

    def behind_inbwd(dp, arrived):
        rs_parts["ssm_w_in"], rs_parts["conv_w_out"] = arrived
        gw_conv_in = _wgrad(h0, dp, n_chunks=4, tm=d_model, tl=tl, name="wg_conv_in", init=gw_conv_in_ctx)
        both = siblings(["ssm_w_in", "conv_w_out"])
        return _Hosted(scatter([("conv_w_in", gw_conv_in)]).items + both.items)

    dp0, dcw0, grad_x, dsc0, dsh0, _, extra_in = conv_backward(
        df0, p0, yc0, dr0, x2, sc0, True, "", hosted_b=scatter([("ssm_w_in", gw_ssm_in), ("conv_w_out", gw_conv_out)]),
        hosted_in=behind_inbwd)
    rs_parts["conv_w_in"], both_halves["ssm_w_in"], both_halves["conv_w_out"] = extra_in
    *grads_r0, _ = s5_grads[0]()
    *grads_r1, (both_halves["conv_w_in"],) = s5_grads[1](hosted=siblings(["conv_w_in"]))
    s5_grads = [grads_r0, grads_r1]

    grads, deltas, new_m, new_v = {}, {}, {}, {}
    for name in ("ssm_w_glu", "ssm_w_out", "ssm_w_in", "conv_w_out", "conv_w_in"):
        w = weights[name]
        rows, cols = math.prod(w.shape[:-1]), w.shape[-1]
        both = both_halves[name].reshape(rows, cols)
        dlt, nm, nv = _adamw(w.reshape(rows, cols), both, mom_m[name].reshape(rows, cols),
                             mom_v[name].reshape(rows, cols), "adamw_" + name)
        grads[name], deltas[name] = both.reshape(w.shape), dlt.reshape(w.shape)
        new_m[name], new_v[name] = nm.reshape(w.shape), nv.reshape(w.shape)

    gpn = (n_grp, grp, n_state)
    small_parts = [
        jnp.concatenate([dg0 + dgc0, dg1], axis=0), jnp.concatenate([db0 + dbc0, db1], axis=0),
        dcw0 + dcwc0, dd, dbglu,
        jnp.stack([s5_grads[r][4] for r in range(2)]),
    ] + [jnp.stack([_block_diag_t(s5_grads[r][k], *gpn) for r in range(2)]) for k in range(4)]
    small_shapes = [p.shape for p in small_parts]
    flat = _flat(small_parts)
    quantum = 8 * SUBLANES * LANES
    n_flat = -(-flat.shape[0] // quantum) * quantum
    flat = jnp.pad(flat, (0, n_flat - flat.shape[0])).reshape(8, n_flat // (8 * LANES), LANES)
    red = _sum_parts(_exchange(flat, everyone, True, "rs_small"), "sum_small")
    red = _exchange(red, everyone, False, "ag_small_grads").reshape(-1)
    g_ln_g, g_ln_b, g_conv_w, g_d, g_bglu, g_a, g_bbr, g_bbi, g_cre, g_cim = _unflat(red, small_shapes)

    def groups_minor(t, lead):
        return jnp.moveaxis(t, 1, -1).reshape(lead, n_grp)

    g_a = g_a.reshape(2, ngb, 2, sbk)
    dar = groups_minor(g_a[:, :, 0].reshape(2, n_grp, n_state), 2 * n_state)
    dai = groups_minor(g_a[:, :, 1].reshape(2, n_grp, n_state), 2 * n_state)
    dbbr_t = jnp.transpose(g_bbr, (2, 0, 3, 1)).reshape(grp, 2 * n_state, n_grp)
    dbbi_t = jnp.transpose(g_bbi, (2, 0, 3, 1)).reshape(grp, 2 * n_state, n_grp)
    z_lre, z_lim, z_ls, z_bre, z_bim = _zoh_bwd(lam_re2, lam_im2, log_step2, b_re_t, b_im_t, dar, dai, dbbr_t, dbbi_t)

    zero = jnp.zeros((1, d_model), F32)
    dm_rows = jnp.stack([
        jnp.stack([jnp.concatenate([dsh0, dsc0, dgt0], axis=1), jnp.concatenate([dshc0, dscc0, dgtc0], axis=1)]),
        jnp.stack([jnp.concatenate([dsh1, dsc1, dgt1], axis=1), jnp.concatenate([dshc1, dscc1, zero], axis=1)]),
    ]).reshape(DEPTH, 2, 3 * d_model)
    dm_all = _exchange(dm_rows, everyone, False, "ag_dmod")
    dm_sh = lax.dynamic_slice_in_dim(dm_all, chip * ws, ws, axis=3)
    g_ada_w, g_ada_b, ds_part = _ada_bwd(c_all, cc2, ada_w, dm_all, dm_sh)
    g_cctx = _cctx_grad(_exchange(ds_part, chips, False, "ag_dsctx"), cc2)

    grads["ada_w"] = g_ada_w
    dlt, nm, nv = _adamw(ada_w.reshape(-1, ws), g_ada_w.reshape(-1, ws), m_ada_w.reshape(-1, ws),
                         v_ada_w.reshape(-1, ws), "adamw_ada_w")
    deltas["ada_w"], new_m["ada_w"], new_v["ada_w"] = dlt.reshape(ada_w.shape), nm.reshape(ada_w.shape), nv.reshape(ada_w.shape)

    def chip_cols(full, rows):
        return lax.dynamic_slice_in_dim(full.reshape(rows, e), chip * es, es, axis=1)

    def same(t):
        return t

    def channel_minor_back(t):
        return jnp.transpose(t, (1, 0, 2)).reshape(2 * n_state * grp, n_grp)

    small = dict(
        c_ctx=(g_cctx, lambda t: t.reshape(1, d_model), lambda t: t.reshape(c_ctx.shape)),
        ada_b=(g_ada_b.reshape(ada_b.shape), same, same),
        ln_g=(g_ln_g, same, same), ln_b=(g_ln_b, same, same),
        conv_w=(chip_cols(g_conv_w, 3), lambda t: t[0], lambda t: t.reshape(conv_w.shape)),
        ssm_lam_re=(z_lre, lam_view, lam_back), ssm_lam_im=(z_lim, lam_view, lam_back),
        ssm_log_step=(z_ls, lambda t: t[0], lambda t: t.reshape(ssm_log_step.shape)),
        ssm_b_re=(channel_minor_back(z_bre), b_view, b_back), ssm_b_im=(channel_minor_back(z_bim), b_view, b_back),
        ssm_c_re=(groups_minor(g_cre, 2 * grp * n_state), c_view, c_back),
        ssm_c_im=(groups_minor(g_cim, 2 * grp * n_state), c_view, c_back),
        ssm_d=(chip_cols(g_d, 1), same, same), ssm_b_glu=(chip_cols(g_bglu, 1), same, same))
    for n, (g_view, view, back) in small.items():
        dlt, nm, nv = _adamw(view(weights[n]), g_view, view(mom_m[n]), view(mom_v[n]), "adamw_" + n)
        grads[n], deltas[n], new_m[n], new_v[n] = back(g_view), back(dlt), back(nm), back(nv)

    loss_total = lax.psum(loss[0, 0], MESH_AXES)
    return (loss_total, grad_x.reshape(x.shape), *[grads[n] for n in names], *[deltas[n] for n in names],
            *[new_m[n] for n in names], *[new_v[n] for n in names])
```

```python
import functools
import math

import jax
import jax.numpy as jnp
from jax import lax
from jax.experimental import pallas as pl
from jax.experimental.pallas import tpu as pltpu

F32 = jnp.float32
BF16 = jnp.bfloat16
LANES = 128
SUBLANES = 8
VMEM_LIMIT = 56 * 1024 * 1024
MESH_AXES = ("x", "y", "c")
HIGHEST = lax.Precision.HIGHEST

GRID_W = 64
LN_EPS = 1e-5
DEPTH = 2
DN_ALPHA = (2 * DEPTH) ** 0.25
ADAM_LR, ADAM_B1, ADAM_B2, ADAM_EPS, ADAM_WD, ADAM_STEP = 0.001, 0.9, 0.999, 1e-08, 0.01, 10
GELU_K = math.sqrt(2.0 / math.pi)
GELU_C = 0.044715


def _params(n_grid_axes):
    return pltpu.CompilerParams(dimension_semantics=("arbitrary",) * n_grid_axes, vmem_limit_bytes=VMEM_LIMIT)


def _dot(a, b):
    return jnp.dot(a, b, preferred_element_type=F32)


def _dot_nt(a, b):
    return lax.dot_general(a, b, (((1,), (1,)), ((), ())), preferred_element_type=F32)


def _dot_tn(a, b):
    return lax.dot_general(a, b, (((0,), (0,)), ((), ())), preferred_element_type=F32)


def _sigmoid(x):
    return 0.5 * jnp.tanh(0.5 * x) + 0.5


def _colsum(x):
    return jnp.sum(x, axis=0, keepdims=True)


def _rowmean(x):
    return jnp.mean(x, axis=-1, keepdims=True)


def _zero_first(first, *refs):
    @pl.when(first)
    def _():
        for ref in refs:
            ref[...] = jnp.zeros_like(ref)


def _acc(ref, value):
    ref[...] += value


def _exchange_copies(src_ref, out_ref, send_sems, recv_sems, own_sem, axes, all_to_all, sem0=0):
    n_peers = 2 ** len(axes)
    pos = {a: lax.axis_index(a) for a in MESH_AXES}

    def index(p):
        return sum(p[a] * (2 ** (len(axes) - 1 - i)) for i, a in enumerate(axes))

    me = index(pos)
    own = pltpu.make_async_copy(src_ref.at[me] if all_to_all else src_ref, out_ref.at[me], own_sem)
    copies = []
    for k in range(1, n_peers):
        peer = dict(pos)
        for i, a in enumerate(axes):
            if (k >> (len(axes) - 1 - i)) & 1:
                peer[a] = 1 - pos[a]
        copies.append(pltpu.make_async_remote_copy(
            src_ref=src_ref.at[index(peer)] if all_to_all else src_ref,
            dst_ref=out_ref.at[me],
            send_sem=send_sems.at[sem0 + k - 1],
            recv_sem=recv_sems.at[sem0 + k - 1],
            device_id=tuple(peer[a] for a in MESH_AXES),
            device_id_type=pl.DeviceIdType.MESH,
        ))
    return copies, own


def _exchange_shape(src, axes, all_to_all):
    block = tuple(src.shape[1:] if all_to_all else src.shape)
    return jax.ShapeDtypeStruct((2 ** len(axes),) + block, src.dtype)


def _exchange(src, axes, all_to_all, name):
    n_peers = 2 ** len(axes)

    def body(src_ref, out_ref, send_sems, recv_sems, own_sem):
        copies, own = _exchange_copies(src_ref, out_ref, send_sems, recv_sems, own_sem, axes, all_to_all)
        own.start()
        for cp in copies:
            cp.start()
        for cp in copies:
            cp.wait()
        own.wait()

    return pl.pallas_call(
        body,
        name=name,
        out_shape=_exchange_shape(src, axes, all_to_all),
        in_specs=[pl.BlockSpec(memory_space=pltpu.HBM)],
        out_specs=pl.BlockSpec(memory_space=pltpu.HBM),
        scratch_shapes=[
            pltpu.SemaphoreType.DMA((n_peers - 1,)),
            pltpu.SemaphoreType.DMA((n_peers - 1,)),
            pltpu.SemaphoreType.DMA,
        ],
    )(src)


class _Hosted:
    def __init__(self, items):
        self.items = items
        self.args = [src for src, _, _ in items]
        self.in_specs = [pl.BlockSpec(memory_space=pltpu.HBM)] * len(items)
        self.out_specs = [pl.BlockSpec(memory_space=pltpu.HBM)] * len(items)
        self.out_shapes = [_exchange_shape(*item) for item in items]
        n_remote = sum(2 ** len(axes) - 1 for _, axes, _ in items)
        self.scratch = [pltpu.SemaphoreType.DMA((n_remote,)), pltpu.SemaphoreType.DMA((n_remote,)),
                        pltpu.SemaphoreType.DMA((len(items),))]

    def _copies(self, src_refs, out_refs, send_sems, recv_sems, own_sems):
        out, sem0 = [], 0
        for n, (_, axes, all_to_all) in enumerate(self.items):
            copies, own = _exchange_copies(src_refs[n], out_refs[n], send_sems, recv_sems, own_sems.at[n], axes,
                                           all_to_all, sem0)
            out += [own] + copies
            sem0 += len(copies)
        return out

    def start(self, *refs):
        for cp in self._copies(*refs):
            cp.start()

    def wait(self, *refs):
        for cp in self._copies(*refs):
            cp.wait()


def _call(body, *, name, grid, in_specs, out_specs, out_shape, scratch_shapes, args, hosted=None):
    params = _params(len(grid))
    if hosted is None:
        outs = pl.pallas_call(body, name=name, grid=grid, in_specs=in_specs, out_specs=tuple(out_specs),
                              out_shape=tuple(out_shape), scratch_shapes=list(scratch_shapes), compiler_params=params)(*args)
        return list(outs), []
    n_in, n_out, n_scr, n_h = len(in_specs), len(out_shape), len(scratch_shapes), len(hosted.items)

    def wrapped(*refs):
        ins, h_in = refs[:n_in], refs[n_in:n_in + n_h]
        outs, h_out = refs[n_in + n_h:n_in + n_h + n_out], refs[n_in + n_h + n_out:n_in + 2 * n_h + n_out]
        scr = refs[n_in + 2 * n_h + n_out:]
        first = functools.reduce(jnp.logical_and, [pl.program_id(k) == 0 for k in range(len(grid))])
        last = functools.reduce(jnp.logical_and, [pl.program_id(k) == grid[k] - 1 for k in range(len(grid))])

        @pl.when(first)
        def _():
            hosted.start(h_in, h_out, *scr[n_scr:])

        body(*ins, *outs, *scr[:n_scr])

        @pl.when(last)
        def _():
            hosted.wait(h_in, h_out, *scr[n_scr:])

    outs = pl.pallas_call(
        wrapped, name=name, grid=grid, in_specs=[*in_specs, *hosted.in_specs],
        out_specs=(*out_specs, *hosted.out_specs), out_shape=(*out_shape, *hosted.out_shapes),
        scratch_shapes=[*scratch_shapes, *hosted.scratch], compiler_params=params)(*args, *hosted.args)
    return list(outs[:n_out]), list(outs[n_out:])


def _sum_parts(parts, name):
    n_parts, rows, cols = parts.shape
    tr = rows
    while n_parts * tr * cols * 4 > 8 * 1024 * 1024 and tr % 16 == 0:
        tr //= 2

    def body(p_ref, o_ref):
        total = p_ref[0]
        for k in range(1, n_parts):
            total = total + p_ref[k]
        o_ref[...] = total

    return pl.pallas_call(
        body,
        name=name,
        grid=(rows // tr,),
        out_shape=jax.ShapeDtypeStruct((rows, cols), F32),
        in_specs=[pl.BlockSpec((n_parts, tr, cols), lambda i: (0, i, 0))],
        out_specs=pl.BlockSpec((tr, cols), lambda i: (i, 0)),
        compiler_params=_params(1),
    )(parts)


def _adamw(w, g, m, v, name):
    rows, cols = w.shape
    tr = rows
    while tr * cols * 4 > 2 * 1024 * 1024 and tr % 16 == 0:
        tr //= 2

    def body(w_ref, g_ref, m_ref, v_ref, d_ref, nm_ref, nv_ref):
        gv = g_ref[...]
        nm = ADAM_B1 * m_ref[...] + (1.0 - ADAM_B1) * gv
        nv = ADAM_B2 * v_ref[...] + (1.0 - ADAM_B2) * (gv * gv)
        m_hat = nm / (1.0 - ADAM_B1 ** ADAM_STEP)
        v_hat = nv / (1.0 - ADAM_B2 ** ADAM_STEP)
        d_ref[...] = -ADAM_LR * (m_hat / (jnp.sqrt(v_hat) + ADAM_EPS) + ADAM_WD * w_ref[...])
        nm_ref[...] = nm
        nv_ref[...] = nv

    spec = pl.BlockSpec((tr, cols), lambda i: (i, 0))
    shape = jax.ShapeDtypeStruct((rows, cols), F32)
    return pl.pallas_call(
        body, name=name, grid=(rows // tr,), out_shape=(shape, shape, shape),
        in_specs=[spec] * 4, out_specs=(spec, spec, spec), compiler_params=_params(1),
    )(w, g, m, v)


def _ada_rows(c_ref, cc_ref):
    rows = jnp.concatenate([c_ref[...], jnp.broadcast_to(cc_ref[...], c_ref.shape)], axis=0)
    return rows


def _ada_fwd(c_all, c_ctx, w_sh, b_sh):
    n_layers, _, ws = w_sh.shape

    def body(c_ref, cc_ref, w_ref, b_ref, o_ref):
        rows = _ada_rows(c_ref, cc_ref)
        s = rows * _sigmoid(rows)
        for i in range(n_layers):
            o_ref[i] = jnp.dot(s, w_ref[i], precision=HIGHEST, preferred_element_type=F32) + b_ref[i]

    return pl.pallas_call(
        body, name="ada_fwd", out_shape=jax.ShapeDtypeStruct((n_layers, 16, ws), F32),
        compiler_params=pltpu.CompilerParams(vmem_limit_bytes=VMEM_LIMIT),
    )(c_all, c_ctx, w_sh, b_sh)


def _ada_bwd(c_all, c_ctx, w_sh, dm_full, dm_sh):
    n_layers, d_model, ws = w_sh.shape
    n_dev = dm_full.shape[0]
    cols = dm_full.shape[-1]

    def body(c_ref, cc_ref, w_ref, dmf_ref, dms_ref, gw_ref, gb_ref, ds_ref):
        rows = _ada_rows(c_ref, cc_ref)
        s = rows * _sigmoid(rows)
        ds = jnp.zeros((8, d_model), F32)
        for i in range(n_layers):
            ctx_s = dms_ref[0, i, 1:2, :]
            ctx_f = dmf_ref[0, i, 1:2, :]
            ex_f = dmf_ref[0, i, 0:1, :]
            for k in range(1, n_dev):
                ctx_s = ctx_s + dms_ref[k, i, 1:2, :]
                ctx_f = ctx_f + dmf_ref[k, i, 1:2, :]
                ex_f = ex_f + dmf_ref[k, i, 0:1, :]
            gb_ref[i] = ex_f + ctx_f
            r = jnp.concatenate([dms_ref[k, i, 0:1, :] for k in range(n_dev)] + [ctx_s, jnp.zeros((7, ws), F32)], axis=0)
            gw_ref[i] = lax.dot_general(s, r, (((0,), (0,)), ((), ())), precision=HIGHEST, preferred_element_type=F32)
            ds = ds + lax.dot_general(jnp.broadcast_to(ctx_s, (8, ws)), w_ref[i], (((1,), (1,)), ((), ())),
                                      precision=HIGHEST, preferred_element_type=F32)
        ds_ref[...] = ds

    return pl.pallas_call(
        body, name="ada_bwd",
        out_shape=(jax.ShapeDtypeStruct((n_layers, d_model, ws), F32), jax.ShapeDtypeStruct((n_layers, 1, cols), F32),
                   jax.ShapeDtypeStruct((8, d_model), F32)),
        compiler_params=pltpu.CompilerParams(vmem_limit_bytes=VMEM_LIMIT),
    )(c_all, c_ctx, w_sh, dm_full, dm_sh)


def _cctx_grad(ds_parts, c_ctx):
    def body(p_ref, c_ref, o_ref):
        tot = p_ref[0, 0:1, :]
        for k in range(1, ds_parts.shape[0]):
            tot = tot + p_ref[k, 0:1, :]
        cv = c_ref[...]
        sg = _sigmoid(cv)
        o_ref[...] = tot * (sg * (1.0 + cv * (1.0 - sg)))

    return pl.pallas_call(body, name="cctx_grad", out_shape=jax.ShapeDtypeStruct(c_ctx.shape, F32))(ds_parts, c_ctx)


def _zoh_math(lam_re, lam_im, log_step, b_re, b_im):
    n_state = lam_re.shape[0] // 2
    dt = jnp.exp(jnp.concatenate([jnp.broadcast_to(log_step[r:r + 1], (n_state, log_step.shape[1])) for r in range(2)],
                                 axis=0))
    mag = jnp.exp(lam_re * dt)
    ar = mag * jnp.cos(lam_im * dt)
    ai = mag * jnp.sin(lam_im * dt)
    qr, qi = ar - 1.0, ai
    den = lam_re * lam_re + lam_im * lam_im
    fr = (qr * lam_re + qi * lam_im) / den
    fi = (qi * lam_re - qr * lam_im) / den
    bbr = fr[None] * b_re - fi[None] * b_im
    bbi = fr[None] * b_im + fi[None] * b_re
    return ar, ai, bbr, bbi


def _zoh_fwd(lam_re, lam_im, log_step, b_re, b_im):
    rg, n = lam_re.shape

    def body(lr_ref, li_ref, ls_ref, br_ref, bi_ref, pr_ref, pi_ref, qr_ref, qi_ref, bbr_ref, bbi_ref):
        ar, ai, bbr, bbi = _zoh_math(lr_ref[...], li_ref[...], ls_ref[...], br_ref[...], bi_ref[...])
        bbr_ref[...] = bbr
        bbi_ref[...] = bbi

        def powers(base_r, base_i, r_ref, i_ref):
            pr, pi_ = base_r, base_i
            for k in range(8):
                r_ref[k] = pr
                i_ref[k] = pi_
                pr, pi_ = pr * base_r - pi_ * base_i, pr * base_i + pi_ * base_r

        powers(ar, ai, pr_ref, pi_ref)
        powers(pr_ref[7], pi_ref[7], qr_ref, qi_ref)

    pw = jax.ShapeDtypeStruct((8, rg, n), F32)
    bb = jax.ShapeDtypeStruct(b_re.shape, F32)
    return pl.pallas_call(body, name="zoh_fwd", out_shape=(pw, pw, pw, pw, bb, bb))(lam_re, lam_im, log_step, b_re, b_im)


def _zoh_bwd(lam_re, lam_im, log_step, b_re, b_im, dar, dai, dbbr, dbbi):
    def body(lr_ref, li_ref, ls_ref, br_ref, bi_ref, dar_ref, dai_ref, dbr_ref, dbi_ref, *outs):
        _, vjp = jax.vjp(_zoh_math, lr_ref[...], li_ref[...], ls_ref[...], br_ref[...], bi_ref[...])
        grads = vjp((dar_ref[...], dai_ref[...], dbr_ref[...], dbi_ref[...]))
        for o_ref, gval in zip(outs, grads):
            o_ref[...] = gval

    shapes = tuple(jax.ShapeDtypeStruct(a.shape, F32) for a in (lam_re, lam_im, log_step, b_re, b_im))
    return pl.pallas_call(body, name="zoh_bwd", out_shape=shapes)(lam_re, lam_im, log_step, b_re, b_im, dar, dai, dbbr, dbbi)


def _inproj(xin, sc, sh, w, *, lnaff=None, tb, gb_rows=None, name, hosted=None):
    n_tok, d_model = xin.shape
    n_chunks, _, cw = w.shape
    tb = min(tb, n_tok)
    nq = cw // LANES
    has_ln = lnaff is not None
    n_out = 1 if gb_rows is None else len(gb_rows)

    def body(*refs):
        if has_ln:
            x_ref, g_ref, b_ref, sc_ref, sh_ref, w_ref = refs[:6]
        else:
            x_ref, sc_ref, sh_ref, w_ref = refs[:4]
        p_refs, h_ref = refs[-1 - n_out:-1], refs[-1]

        @pl.when(pl.program_id(1) == 0)
        def _():
            xv = x_ref[...]
            if has_ln:
                xv = xv * g_ref[...] + b_ref[...]
            h_ref[...] = (xv * (1.0 + sc_ref[...]) + sh_ref[...]).astype(BF16)

        acc = _dot(h_ref[...], w_ref[0]).astype(BF16)
        if gb_rows is None:
            p_refs[0][0] = acc
        else:
            for p_ref in p_refs:
                for q in range(nq):
                    p_ref[q] = acc[:, q * LANES:(q + 1) * LANES]

    vec = pl.BlockSpec((1, d_model), lambda i, j: (0, 0))
    in_specs = [pl.BlockSpec((tb, d_model), lambda i, j: (i, 0))] + ([vec, vec] if has_ln else []) + [
        vec, vec, pl.BlockSpec((1, d_model, cw), lambda i, j: (j, 0, 0))]
    if gb_rows is None:
        p_shapes = [jax.ShapeDtypeStruct((n_chunks, n_tok, cw), BF16)]
        p_specs = [pl.BlockSpec((1, tb, cw), lambda i, j: (j, i, 0))]
    else:
        p_shapes, p_specs = [], []
        for total, off in gb_rows:
            assert off % tb == 0
            p_shapes.append(jax.ShapeDtypeStruct((n_chunks * nq, total, LANES), BF16))
            p_specs.append(pl.BlockSpec((nq, tb, LANES), functools.partial(lambda i, j, ob: (j, i + ob, 0), ob=off // tb)))
    args = (xin,) + (tuple(lnaff) if has_ln else ()) + (sc, sh, w)
    outs, extra = _call(
        body, name=name, grid=(n_tok // tb, n_chunks), in_specs=in_specs,
        out_specs=[*p_specs, pl.BlockSpec((tb, d_model), lambda i, j: (i, 0))],
        out_shape=[*p_shapes, jax.ShapeDtypeStruct((n_tok, d_model), BF16)], scratch_shapes=[], args=args, hosted=hosted)
    return (*outs, extra) if hosted is not None else tuple(outs)


def _shifted(u, rows, width, tb):
    col = rows % width
    um = jnp.where(col == 0, 0.0, pltpu.roll(u, 1, 0))
    up = jnp.where(col == width - 1, 0.0, pltpu.roll(u, tb - 1, 0))
    return um, up


def _slab_width(e):
    return min(512, e // 2)


def _convgate(p, x, gt, conv_w, w_out, ln_g, ln_b, *, grid_mode, tb, name):
    _, n_tok, e = p.shape
    d_model = x.shape[1]
    eh = e // 2
    if not grid_mode:
        tb = n_tok
    tb = min(tb, n_tok)
    nb = n_tok // tb
    hb = tb // GRID_W
    cs = _slab_width(e)

    def body(*refs):
        if grid_mode:
            (bg_ref, cg_ref, v_ref, z_ref, cgp_ref, vp_ref, cgn_ref, vn_ref, x_ref, gt_ref, cw_ref, wo_ref, lg_ref,
             lb_ref, xh_ref, rs_ref, g_ref, yc_ref, f_ref) = refs
        else:
            (bg_ref, cg_ref, v_ref, z_ref, x_ref, gt_ref, cw_ref, wo_ref, lg_ref, lb_ref, xh_ref, rs_ref, g_ref,
             yc_ref, f_ref) = refs
        i = pl.program_id(0)
        rows = lax.broadcasted_iota(jnp.int32, (tb, 1), 0)
        for c0 in range(0, e, cs):
            sl = slice(c0, c0 + cs)
            u = cg_ref[0, :, sl].astype(F32) * v_ref[0, :, sl].astype(F32)
            w = cw_ref[:, sl]
            if grid_mode and c0 >= eh:
                hs = slice(c0 - eh, c0 - eh + cs)
                uprev = cgp_ref[0, :, hs].astype(F32) * vp_ref[0, :, hs].astype(F32)
                unext = cgn_ref[0, :, hs].astype(F32) * vn_ref[0, :, hs].astype(F32)
                uprev = jnp.where(i > 0, uprev, 0.0)
                unext = jnp.where(i < nb - 1, unext, 0.0)
                if tb > GRID_W:
                    um = jnp.concatenate([uprev, u[:tb - GRID_W]], axis=0)
                    up = jnp.concatenate([u[GRID_W:], unext], axis=0)
                else:
                    um, up = uprev, unext
            else:
                um, up = _shifted(u, rows, GRID_W if grid_mode else tb, tb)
            yc = um * w[0:1] + u * w[1:2] + up * w[2:3]
            zf = z_ref[0, :, sl].astype(F32)
            gval = bg_ref[0, :, sl].astype(F32) * yc * (zf * _sigmoid(zf))
            yc_ref[:, sl] = yc.astype(BF16)
            g_ref[:, sl] = gval.astype(BF16)
        f = _dot(g_ref[...], wo_ref[...])
        f_ref[...] = f.astype(BF16)
        r = DN_ALPHA * x_ref[...] + gt_ref[...] * f
        rc = r - _rowmean(r)
        rstd = lax.rsqrt(_rowmean(rc * rc) + LN_EPS)
        xh_ref[...] = rc * rstd
        rs_ref[...] = rstd

    def chunk(k):
        return pl.BlockSpec((1, tb, e), lambda i: (k, i, 0))

    n_hrows = n_tok // GRID_W

    def halo_prev(k):
        return pl.BlockSpec((1, GRID_W, eh), lambda i: (k, jnp.maximum(i * hb - 1, 0), 1))

    def halo_next(k):
        return pl.BlockSpec((1, GRID_W, eh), lambda i: (k, jnp.minimum((i + 1) * hb, n_hrows - 1), 1))

    vec = pl.BlockSpec((1, d_model), lambda i: (0, 0))
    tok = pl.BlockSpec((tb, d_model), lambda i: (i, 0))
    wide = pl.BlockSpec((tb, e), lambda i: (i, 0))
    in_specs = [chunk(0), chunk(1), chunk(2), chunk(3)]
    args = [p, p, p, p]
    if grid_mode:
        in_specs += [halo_prev(1), halo_prev(2), halo_next(1), halo_next(2)]
        args += [p, p, p, p]
    in_specs += [tok, vec, pl.BlockSpec((3, e), lambda i: (0, 0)), pl.BlockSpec((e, d_model), lambda i: (0, 0)), vec, vec]
    args += [x, gt, conv_w, w_out, ln_g, ln_b]
    return pl.pallas_call(
        body, name=name, grid=(nb,),
        out_shape=(jax.ShapeDtypeStruct((n_tok, d_model), F32), jax.ShapeDtypeStruct((n_tok, 1), F32),
                   jax.ShapeDtypeStruct((n_tok, e), BF16), jax.ShapeDtypeStruct((n_tok, e), BF16),
                   jax.ShapeDtypeStruct((n_tok, d_model), BF16)),
        in_specs=in_specs, out_specs=(tok, pl.BlockSpec((tb, 1), lambda i: (i, 0)), wide, wide, tok),
        compiler_params=_params(1),
    )(*args)


def _scan_block(buf_ref, tab_ref, cr, ci, *, reverse, tb, sb):
    n_slabs = tb // SUBLANES
    unrolled = n_slabs <= 64

    def slab(s, carry):
        cr, ci = carry
        idx = (n_slabs - 1 - s) if reverse else s
        r0 = idx * SUBLANES if unrolled else pl.multiple_of(idx * SUBLANES, SUBLANES)
        xr = buf_ref[pl.ds(r0, SUBLANES), 0:sb]
        xi = buf_ref[pl.ds(r0, SUBLANES), sb:2 * sb]
        for k, step in enumerate((1, 2, 4)):
            ar = tab_ref[2 * k]
            ai = tab_ref[2 * k + 1]
            shift = (SUBLANES - step) if reverse else step
            rr = pltpu.roll(xr, shift, 0)
            ri = pltpu.roll(xi, shift, 0)
            xr, xi = xr + ar * rr - ai * ri, xi + ar * ri + ai * rr
        pr = tab_ref[6]
        pi_ = tab_ref[7]
        xr, xi = xr + pr * cr - pi_ * ci, xi + pr * ci + pi_ * cr
        buf_ref[pl.ds(r0, SUBLANES), 0:sb] = xr
        buf_ref[pl.ds(r0, SUBLANES), sb:2 * sb] = xi
        last = 0 if reverse else SUBLANES - 1
        return (jnp.broadcast_to(xr[last:last + 1, :], (SUBLANES, sb)),
                jnp.broadcast_to(xi[last:last + 1, :], (SUBLANES, sb)))

    if unrolled:
        carry = (cr, ci)
        for s in range(n_slabs):
            carry = slab(s, carry)
        return carry
    return lax.fori_loop(0, n_slabs, slab, (cr, ci))


CHUNK = SUBLANES


def _group_mask():
    r = lax.broadcasted_iota(jnp.int32, (LANES, LANES), 0)
    c = lax.broadcasted_iota(jnp.int32, (LANES, LANES), 1)
    return r // 16 == c // 16


def _s5_ops(bre, bim, cre, cim, wr, wi, *, reverse, name, hosted=None):
    ngb, _, sb = bre.shape
    n_rows = CHUNK * LANES

    def body(bre_ref, bim_ref, cre_ref, cim_ref, wr_ref, wi_ref, t_ref, bp_ref, cp_ref):
        b_re, b_im, c_re, c_im = bre_ref[0], bim_ref[0], cre_ref[0], cim_ref[0]
        mask = _group_mask()
        er, ei = [], []
        for tau in range(CHUNK + 1):
            w_r, w_i = wr_ref[0, tau:tau + 1, :], wi_ref[0, tau:tau + 1, :]
            er.append(c_re * w_r - c_im * w_i)
            ei.append(c_re * w_i + c_im * w_r)
        kt = []
        for tau in range(CHUNK):
            k = (lax.dot_general(b_re, er[tau], (((1,), (1,)), ((), ())), precision=HIGHEST, preferred_element_type=F32)
                 - lax.dot_general(b_im, ei[tau], (((1,), (1,)), ((), ())), precision=HIGHEST, preferred_element_type=F32))
            kt.append(jnp.where(mask, k, 0.0).astype(BF16))
        zero = jnp.zeros((LANES, LANES), BF16)
        for i in range(CHUNK):
            rows = slice(i * LANES, (i + 1) * LANES)
            for j in range(CHUNK):
                lag = (i - j) if reverse else (j - i)
                t_ref[0, rows, j * LANES:(j + 1) * LANES] = kt[lag] if lag >= 0 else zero
            tau = i if reverse else CHUNK - 1 - i
            w_r, w_i = wr_ref[0, tau:tau + 1, :], wi_ref[0, tau:tau + 1, :]
            bp_ref[0, rows, 0:sb] = (b_re * w_r - b_im * w_i).astype(BF16)
            bp_ref[0, rows, sb:2 * sb] = (b_re * w_i + b_im * w_r).astype(BF16)
            tau = CHUNK - i if reverse else i + 1
            cp_ref[0, rows, 0:sb] = er[tau].astype(BF16)
            cp_ref[0, rows, sb:2 * sb] = (-ei[tau]).astype(BF16)

    mat = pl.BlockSpec((1, LANES, sb), lambda g: (g, 0, 0))
    pw = pl.BlockSpec((1, CHUNK + 1, sb), lambda g: (g, 0, 0))
    outs, extra = _call(
        body, name=name, grid=(ngb,),
        out_shape=[jax.ShapeDtypeStruct((ngb, n_rows, n_rows), BF16), jax.ShapeDtypeStruct((ngb, n_rows, 2 * sb), BF16),
                   jax.ShapeDtypeStruct((ngb, n_rows, 2 * sb), BF16)],
        in_specs=[mat, mat, mat, mat, pw, pw],
        out_specs=[pl.BlockSpec((1, n_rows, n_rows), lambda g: (g, 0, 0)),
                   pl.BlockSpec((1, n_rows, 2 * sb), lambda g: (g, 0, 0)),
                   pl.BlockSpec((1, n_rows, 2 * sb), lambda g: (g, 0, 0))],
        scratch_shapes=[], args=(bre, bim, cre, cim, wr, wi), hosted=hosted)
    return (*outs, extra)


def _s5_ops_bwd(bre, bim, cre, cim, wr, wi, ar, ai, dt, dbp, dcp, da8, *, reverse, name, hosted=None):
    ngb, _, sb = bre.shape
    n_rows = CHUNK * LANES

    def dot_hi(a, b, dims):
        return lax.dot_general(a, b, (dims, ((), ())), precision=HIGHEST, preferred_element_type=F32)

    def body(bre_ref, bim_ref, cre_ref, cim_ref, wr_ref, wi_ref, ar_ref, ai_ref, dt_ref, dbp_ref, dcp_ref, da8_ref,
             dbre_ref, dbim_ref, dcre_ref, dcim_ref, da_ref):
        b_re, b_im, c_re, c_im = bre_ref[0], bim_ref[0], cre_ref[0], cim_ref[0]
        mask = _group_mask()
        w_r = [wr_ref[0, tau:tau + 1, :] for tau in range(CHUNK + 1)]
        w_i = [wi_ref[0, tau:tau + 1, :] for tau in range(CHUNK + 1)]
        der = [jnp.zeros((LANES, sb), F32) for _ in range(CHUNK + 1)]
        dei = [jnp.zeros((LANES, sb), F32) for _ in range(CHUNK + 1)]
        dwr = [jnp.zeros((1, sb), F32) for _ in range(CHUNK + 1)]
        dwi = [jnp.zeros((1, sb), F32) for _ in range(CHUNK + 1)]
        dwr[CHUNK] = da8_ref[0, :, 0:sb]
        dwi[CHUNK] = da8_ref[0, :, sb:2 * sb]
        d_bre = jnp.zeros((LANES, sb), F32)
        d_bim = jnp.zeros((LANES, sb), F32)
        dkt = [jnp.zeros((LANES, LANES), F32) for _ in range(CHUNK)]
        for i in range(CHUNK):
            rows = slice(i * LANES, (i + 1) * LANES)
            for j in range(CHUNK):
                lag = (i - j) if reverse else (j - i)
                if lag >= 0:
                    dkt[lag] = dkt[lag] + dt_ref[0, rows, j * LANES:(j + 1) * LANES]
            tau = i if reverse else CHUNK - 1 - i
            g_r, g_i = dbp_ref[0, rows, 0:sb], dbp_ref[0, rows, sb:2 * sb]
            d_bre = d_bre + g_r * w_r[tau] + g_i * w_i[tau]
            d_bim = d_bim - g_r * w_i[tau] + g_i * w_r[tau]
            dwr[tau] = dwr[tau] + _colsum(g_r * b_re + g_i * b_im)
            dwi[tau] = dwi[tau] + _colsum(g_i * b_re - g_r * b_im)
            tau = CHUNK - i if reverse else i + 1
            der[tau] = der[tau] + dcp_ref[0, rows, 0:sb]
            dei[tau] = dei[tau] - dcp_ref[0, rows, sb:2 * sb]
        d_cre = jnp.zeros((LANES, sb), F32)
        d_cim = jnp.zeros((LANES, sb), F32)
        for tau in range(CHUNK + 1):
            if tau < CHUNK:
                e_r = c_re * w_r[tau] - c_im * w_i[tau]
                e_i = c_re * w_i[tau] + c_im * w_r[tau]
                dk = jnp.where(mask, dkt[tau], 0.0)
                d_bre = d_bre + dot_hi(dk, e_r, ((1,), (0,)))
                d_bim = d_bim - dot_hi(dk, e_i, ((1,), (0,)))
                der[tau] = der[tau] + dot_hi(dk, b_re, ((0,), (0,)))
                dei[tau] = dei[tau] - dot_hi(dk, b_im, ((0,), (0,)))
            d_cre = d_cre + der[tau] * w_r[tau] + dei[tau] * w_i[tau]
            d_cim = d_cim - der[tau] * w_i[tau] + dei[tau] * w_r[tau]
            dwr[tau] = dwr[tau] + _colsum(der[tau] * c_re + dei[tau] * c_im)
            dwi[tau] = dwi[tau] + _colsum(dei[tau] * c_re - der[tau] * c_im)
        a_r, a_i = ar_ref[0], ai_ref[0]
        d_ar = jnp.zeros((1, sb), F32)
        d_ai = jnp.zeros((1, sb), F32)
        for tau in range(CHUNK, 0, -1):
            d_ar = d_ar + dwr[tau] * w_r[tau - 1] + dwi[tau] * w_i[tau - 1]
            d_ai = d_ai - dwr[tau] * w_i[tau - 1] + dwi[tau] * w_r[tau - 1]
            dwr[tau - 1], dwi[tau - 1] = (dwr[tau - 1] + dwr[tau] * a_r + dwi[tau] * a_i,
                                          dwi[tau - 1] - dwr[tau] * a_i + dwi[tau] * a_r)
        dbre_ref[0] = d_bre
        dbim_ref[0] = d_bim
        dcre_ref[0] = d_cre
        dcim_ref[0] = d_cim
        da_ref[0, :, 0:sb] = d_ar
        da_ref[0, :, sb:2 * sb] = d_ai

    mat = pl.BlockSpec((1, LANES, sb), lambda g: (g, 0, 0))
    pw = pl.BlockSpec((1, CHUNK + 1, sb), lambda g: (g, 0, 0))
    one = pl.BlockSpec((1, 1, sb), lambda g: (g, 0, 0))
    two = pl.BlockSpec((1, 1, 2 * sb), lambda g: (g, 0, 0))
    big = pl.BlockSpec((1, n_rows, n_rows), lambda g: (g, 0, 0))
    big2 = pl.BlockSpec((1, n_rows, 2 * sb), lambda g: (g, 0, 0))
    mshape = jax.ShapeDtypeStruct((ngb, LANES, sb), F32)
    outs, extra = _call(
        body, name=name, grid=(ngb,),
        out_shape=[mshape, mshape, mshape, mshape, jax.ShapeDtypeStruct((ngb, 1, 2 * sb), F32)],
        in_specs=[mat, mat, mat, mat, pw, pw, one, one, big, big2, big2, two],
        out_specs=[mat, mat, mat, mat, two],
        scratch_shapes=[], args=(bre, bim, cre, cim, wr, wi, ar, ai, dt, dbp, dcp, da8), hosted=hosted)
    return (*outs, extra)


MXU_TILE = 256


def _causal_span(tile, n_tiles, reverse, of_output):
    upto, onward = slice(0, (tile + 1) * MXU_TILE), slice(tile * MXU_TILE, n_tiles * MXU_TILE)
    return (onward if reverse else upto) if of_output else (upto if reverse else onward)


def _apply_causal(uv, t_ref, reverse):
    n_tiles = uv.shape[1] // MXU_TILE
    cols = []
    for tj in range(n_tiles):
        span = _causal_span(tj, n_tiles, reverse, True)
        cols.append(_dot(uv[:, span], t_ref[0, span, tj * MXU_TILE:(tj + 1) * MXU_TILE]))
    return jnp.concatenate(cols, axis=1)


def _apply_causal_t(dyv, t_ref, reverse):
    n_tiles = dyv.shape[1] // MXU_TILE
    cols = []
    for ti in range(n_tiles):
        span = _causal_span(ti, n_tiles, reverse, False)
        cols.append(_dot_nt(dyv[:, span], t_ref[0, ti * MXU_TILE:(ti + 1) * MXU_TILE, span]))
    return jnp.concatenate(cols, axis=1)


def _shift_rows(xv, edge, rows, n_rows, down):
    if down:
        return jnp.where(rows == 0, edge, pltpu.roll(xv, 1, 0))
    return jnp.where(rows == n_rows - 1, edge, pltpu.roll(xv, n_rows - 1, 0))


def _rows_of_tokens(tok_ref, conv_scr, rb):
    conv_scr[...] = tok_ref[0].astype(F32)
    return jnp.concatenate([conv_scr[pl.ds(j, rb, stride=CHUNK), :] for j in range(CHUNK)], axis=1).astype(BF16)


def _tokens_of_rows(val, tok_ref, conv_scr, rb):
    for j in range(CHUNK):
        conv_scr[pl.ds(j, rb, stride=CHUNK), :] = val[:, j * LANES:(j + 1) * LANES]
    tok_ref[0] = conv_scr[...].astype(BF16)


def _s5_row_block(n_seq, target=416):
    n_rows = n_seq // CHUNK
    best = 16
    for rb in range(16, min(target, n_rows) + 1, 16):
        if n_rows % rb == 0:
            best = rb
    assert n_rows % best == 0
    return best


def _s5_fwd(useq, t_op, bp, cp, tab, *, reverse, name):
    ngb, n_seq, _ = useq.shape
    sb = bp.shape[2] // 2
    width = CHUNK * LANES
    rb = _s5_row_block(n_seq)
    tbk = rb * CHUNK
    steps = n_seq // tbk

    def blk(i):
        return (steps - 1 - i) if reverse else i

    def body(u_ref, t_ref, b_ref, c_ref, tab_ref, y_ref, hp_ref, h_scr, conv_scr, carry_scr):
        i = pl.program_id(1)

        @pl.when(i == 0)
        def _():
            carry_scr[...] = jnp.zeros_like(carry_scr)

        enter = carry_scr[0:1, :]
        uv = _rows_of_tokens(u_ref, conv_scr, rb)
        h_scr[...] = _dot(uv, b_ref[0])
        cr, ci = _scan_block(h_scr, tab_ref.at[0], carry_scr[:, 0:sb], carry_scr[:, sb:2 * sb],
                             reverse=reverse, tb=rb, sb=sb)
        carry_scr[:, 0:sb] = cr
        carry_scr[:, sb:2 * sb] = ci
        rows = lax.broadcasted_iota(jnp.int32, (rb, 1), 0)
        hprev = _shift_rows(h_scr[...], enter, rows, rb, down=not reverse)
        hp_ref[0] = hprev
        _tokens_of_rows(_apply_causal(uv, t_ref, reverse) + _dot_nt(hprev.astype(BF16), c_ref[0]), y_ref, conv_scr, rb)

    op = pl.BlockSpec((1, width, width), lambda g, i: (g, 0, 0))
    op2 = pl.BlockSpec((1, width, 2 * sb), lambda g, i: (g, 0, 0))
    tok = pl.BlockSpec((1, tbk, LANES), lambda g, i: (g, blk(i), 0))
    return pl.pallas_call(
        body, name=name, grid=(ngb, steps),
        out_shape=(jax.ShapeDtypeStruct((ngb, n_seq, LANES), BF16),
                   jax.ShapeDtypeStruct((ngb, n_seq // CHUNK, 2 * sb), F32)),
        in_specs=[tok, op, op2, op2, pl.BlockSpec((1, 8, SUBLANES, sb), lambda g, i: (g, 0, 0, 0))],
        out_specs=(tok, pl.BlockSpec((1, rb, 2 * sb), lambda g, i: (g, blk(i), 0))),
        scratch_shapes=[pltpu.VMEM((rb, 2 * sb), F32), pltpu.VMEM((tbk, LANES), F32),
                        pltpu.VMEM((SUBLANES, 2 * sb), F32)],
        compiler_params=_params(2),
    )(useq, t_op, bp, cp, tab)


def _s5_bwd(useq, dy, hprev, t_op, bp, cp, tab_adj, *, reverse, name, hosted=None):
    ngb, n_seq, _ = useq.shape
    sb = bp.shape[2] // 2
    width = CHUNK * LANES
    rb = _s5_row_block(n_seq)
    tbk = rb * CHUNK
    steps = n_seq // tbk

    def fwd_step(i):
        return steps - 1 - i

    def blk(i):
        s = fwd_step(i)
        return (steps - 1 - s) if reverse else s

    def body(u_ref, dy_ref, hp_ref, t_ref, b_ref, c_ref, taba_ref, du_ref, dt_ref, db_ref, dc_ref, da_ref,
             lam_scr, conv_scr, lcarry_scr, gedge_scr, da_scr):
        i = pl.program_id(1)
        first = i == 0

        _zero_first(first, lcarry_scr, gedge_scr, da_scr, dt_ref, db_ref, dc_ref)
        rows = lax.broadcasted_iota(jnp.int32, (rb, 1), 0)
        uv = _rows_of_tokens(u_ref, conv_scr, rb)
        dyv = _rows_of_tokens(dy_ref, conv_scr, rb)
        gy = _dot(dyv, c_ref[0])
        edge = gy[rb - 1:rb, :] if reverse else gy[0:1, :]
        lam_scr[...] = _shift_rows(gy, gedge_scr[...], rows, rb, down=reverse)
        gedge_scr[...] = edge
        lr, li = _scan_block(lam_scr, taba_ref.at[0], lcarry_scr[:, 0:sb], lcarry_scr[:, sb:2 * sb],
                             reverse=not reverse, tb=rb, sb=sb)
        lcarry_scr[:, 0:sb] = lr
        lcarry_scr[:, sb:2 * sb] = li

        lam = lam_scr[...]
        lam_bf = lam.astype(BF16)
        _tokens_of_rows(_apply_causal_t(dyv, t_ref, reverse) + _dot_nt(lam_bf, b_ref[0]), du_ref, conv_scr, rb)
        for tj in range(width // MXU_TILE):
            span, cols = _causal_span(tj, width // MXU_TILE, reverse, True), slice(tj * MXU_TILE, (tj + 1) * MXU_TILE)
            _acc(dt_ref.at[0, span, cols], _dot_tn(uv[:, span], dyv[:, cols]))
        _acc(db_ref.at[0], _dot_tn(uv, lam_bf))
        _acc(dc_ref.at[0], _dot_tn(dyv, hp_ref[0].astype(BF16)))
        lam_r, lam_i = lam[:, 0:sb], lam[:, sb:2 * sb]
        hp_r, hp_i = hp_ref[0, :, 0:sb], hp_ref[0, :, sb:2 * sb]
        da_scr[:, 0:sb] += _colsum(lam_r * hp_r + lam_i * hp_i)
        da_scr[:, sb:2 * sb] += _colsum(lam_i * hp_r - lam_r * hp_i)

        @pl.when(i == steps - 1)
        def _():
            da_ref[0] = da_scr[...]

    op = pl.BlockSpec((1, width, width), lambda g, i: (g, 0, 0))
    op2 = pl.BlockSpec((1, width, 2 * sb), lambda g, i: (g, 0, 0))
    tabs = pl.BlockSpec((1, 8, SUBLANES, sb), lambda g, i: (g, 0, 0, 0))
    tok = pl.BlockSpec((1, tbk, LANES), lambda g, i: (g, blk(i), 0))
    outs, extra = _call(
        body, name=name, grid=(ngb, steps),
        out_shape=[jax.ShapeDtypeStruct((ngb, n_seq, LANES), BF16),
                   jax.ShapeDtypeStruct((ngb, width, width), F32),
                   jax.ShapeDtypeStruct((ngb, width, 2 * sb), F32),
                   jax.ShapeDtypeStruct((ngb, width, 2 * sb), F32),
                   jax.ShapeDtypeStruct((ngb, 1, 2 * sb), F32)],
        in_specs=[tok, tok, pl.BlockSpec((1, rb, 2 * sb), lambda g, i: (g, blk(i), 0)), op, op2, op2, tabs],
        out_specs=[tok, op, op2, op2, pl.BlockSpec((1, 1, 2 * sb), lambda g, i: (g, 0, 0))],
        scratch_shapes=[pltpu.VMEM((rb, 2 * sb), F32), pltpu.VMEM((tbk, LANES), F32),
                        pltpu.VMEM((SUBLANES, 2 * sb), F32), pltpu.VMEM((1, 2 * sb), F32), pltpu.VMEM((1, 2 * sb), F32)],
        args=(useq, dy, hprev, t_op, bp, cp, tab_adj), hosted=hosted)
    return (*outs, extra)


def _glu_loss(useq, yf, yb, z, xhat0, ln0, gt, d_vec, w_glu, b_glu, w_out, ln1, target, *, offs, dy_rows, tb, name):
    ngb = useq.shape[0]
    n_tok, d_model = xhat0.shape
    e = ngb * LANES
    tb = min(tb, n_tok)
    assert all(off % tb == 0 for off in offs) and all(off % tb == 0 for _, off in dy_rows)
    nz = z.shape[0]

    def body(u_ref, yf_ref, yb_ref, z_ref, xh0_ref, g0_ref, b0_ref, gt_ref, d_ref, wg_ref, bg_ref, wo_ref, g1_ref,
             b1_ref, t_ref, loss_ref, dxr_ref, do_ref, gz_ref, gg_ref, dq_ref, dz_ref, dyf_ref, dyb_ref, dg1_ref, db1_ref,
             dgt_ref, dbg_ref, dd_ref, loss_scr, yl_scr, th_scr, s_scr, dg_scr):
        i = pl.program_id(0)
        _zero_first(i == 0, loss_scr, dg1_ref, db1_ref, dgt_ref, dbg_ref, dd_ref)
        zw = e // nz
        cs = min(512, zw)

        def z_slab(c0):
            return z_ref[c0 // zw, :, c0 % zw:c0 % zw + cs].astype(F32)

        for q in range(ngb):
            sl = slice(q * LANES, (q + 1) * LANES)
            yl = d_ref[:, sl] * u_ref[q].astype(F32) + yf_ref[q].astype(F32) + yb_ref[q].astype(F32)
            th = jnp.tanh(GELU_K * (yl + GELU_C * yl * yl * yl))
            yl_scr[:, sl] = yl
            th_scr[:, sl] = th
            gg_ref[:, sl] = (0.5 * yl * (1.0 + th)).astype(BF16)
        s_scr[...] = _sigmoid(_dot(gg_ref[...], wg_ref[...]) + bg_ref[...])
        for c0 in range(0, e, cs):
            sl = slice(c0, c0 + cs)
            zf = z_slab(c0)
            g2 = 0.5 * yl_scr[:, sl] * (1.0 + th_scr[:, sl]) * s_scr[:, sl]
            gz_ref[:, sl] = (g2 * (zf * _sigmoid(zf))).astype(BF16)
        o = _dot(gz_ref[...], wo_ref[...])
        x1 = xh0_ref[...] * g0_ref[...] + b0_ref[...]
        r = DN_ALPHA * x1 + gt_ref[...] * o
        rc = r - _rowmean(r)
        rstd = lax.rsqrt(_rowmean(rc * rc) + LN_EPS)
        xh = rc * rstd
        err = xh * g1_ref[...] + b1_ref[...] - t_ref[...]
        _acc(loss_scr, _colsum(err * err))
        dy = err * (1.0 / d_model)
        _acc(dg1_ref, _colsum(dy * xh))
        _acc(db1_ref, _colsum(dy))
        dxh = dy * g1_ref[...]
        dr = rstd * (dxh - _rowmean(dxh) - xh * _rowmean(dxh * xh))
        dxr_ref[...] = DN_ALPHA * dr
        _acc(dgt_ref, _colsum(dr * o))
        do_bf = (dr * gt_ref[...]).astype(BF16)
        do_ref[...] = do_bf
        dg_scr[...] = _dot_nt(do_bf, wo_ref[...])
        for c0 in range(0, e, cs):
            sl = slice(c0, c0 + cs)
            zf = z_slab(c0)
            sz = _sigmoid(zf)
            g = 0.5 * yl_scr[:, sl] * (1.0 + th_scr[:, sl])
            s = s_scr[:, sl]
            dgz = dg_scr[:, sl]
            dg2 = dgz * (zf * sz)
            dz_ref[:, sl] = (dgz * (g * s) * (sz * (1.0 + zf * (1.0 - sz)))).astype(BF16)
            dq = dg2 * g * s * (1.0 - s)
            _acc(dbg_ref.at[:, sl], _colsum(dq))
            dq_ref[:, sl] = dq.astype(BF16)
            dg_scr[:, sl] = dg2 * s
        dg_scr[...] += _dot_nt(dq_ref[...], wg_ref[...])
        for q in range(ngb):
            sl = slice(q * LANES, (q + 1) * LANES)
            yl = yl_scr[:, sl]
            th = th_scr[:, sl]
            dgelu = 0.5 * (1.0 + th) + 0.5 * yl * (1.0 - th * th) * (GELU_K * (1.0 + 3.0 * GELU_C * yl * yl))
            dyl = dg_scr[:, sl] * dgelu
            _acc(dd_ref.at[:, sl], _colsum(dyl * u_ref[q].astype(F32)))
            dyf_ref[q] = dyl.astype(BF16)
            dyb_ref[q] = dyl.astype(BF16)

        @pl.when(i == pl.num_programs(0) - 1)
        def _():
            loss_ref[...] = (0.5 / d_model) * jnp.sum(loss_scr[...], axis=1, keepdims=True)

    vec = pl.BlockSpec((1, d_model), lambda i: (0, 0))
    evec = pl.BlockSpec((1, e), lambda i: (0, 0))
    tok = pl.BlockSpec((tb, d_model), lambda i: (i, 0))
    wide = pl.BlockSpec((tb, e), lambda i: (i, 0))
    def gblk(off):
        return pl.BlockSpec((ngb, tb, LANES), functools.partial(lambda i, ob: (0, i + ob, 0), ob=off // tb))

    once = dict(pipeline_mode=pl.Buffered(1))
    tok_f = jax.ShapeDtypeStruct((n_tok, d_model), F32)
    tok_b = jax.ShapeDtypeStruct((n_tok, d_model), BF16)
    wide_b = jax.ShapeDtypeStruct((n_tok, e), BF16)
    vec_f = jax.ShapeDtypeStruct((1, d_model), F32)
    evec_f = jax.ShapeDtypeStruct((1, e), F32)
    return pl.pallas_call(
        body, name=name, grid=(n_tok // tb,),
        out_shape=(jax.ShapeDtypeStruct((1, 1), F32), tok_f, tok_b, wide_b, wide_b, wide_b, wide_b,
                   *[jax.ShapeDtypeStruct((ngb, total, LANES), BF16) for total, _ in dy_rows],
                   vec_f, vec_f, vec_f, evec_f, evec_f),
        in_specs=[gblk(offs[0]), gblk(offs[1]), gblk(offs[2]),
                  pl.BlockSpec((nz, tb, e // nz), lambda i: (0, i, 0)), tok, vec, vec, vec, evec,
                  pl.BlockSpec((e, e), lambda i: (0, 0), **once), evec,
                  pl.BlockSpec((e, d_model), lambda i: (0, 0), **once), vec, vec, tok],
        out_specs=(pl.BlockSpec((1, 1), lambda i: (0, 0)), tok, tok, wide, wide, wide, wide,
                   *[gblk(off) for _, off in dy_rows], vec, vec, vec, evec, evec),
        scratch_shapes=[pltpu.VMEM((1, d_model), F32)] + [pltpu.VMEM((tb, e), F32)] * 4,
        compiler_params=_params(1),
    )(useq, yf, yb, z, xhat0, ln0[0], ln0[1], gt, d_vec, w_glu, b_glu, w_out, ln1[0], ln1[1], target)


def _ssm_inbwd(duf, dub, w, xhat, rstd, ln, sc, gt_prev, f_prev, *, lat, row_f, row_b, tb, name):
    ngb = duf.shape[0]
    e = ngb * LANES
    n_tok, d_model = xhat.shape
    tb = min(tb, n_tok)
    obf, obb = row_f // tb, row_b // tb
    has_lat = lat is not None
    n_w = w.shape[0] if has_lat else w.shape[0] // 2

    def body(*refs):
        if has_lat:
            (duf_ref, dub_ref, dyl_ref, dz_ref, d_ref, dxr_ref, w_ref, xh_ref, rs_ref, g_ref, b_ref, sc_ref, gt_ref,
             f_ref, dp_ref, dr_ref, df_ref, dsc_ref, dsh_ref, dg_ref, db_ref, dgt_ref) = refs
        else:
            (duf_ref, dub_ref, w_ref, xh_ref, rs_ref, g_ref, b_ref, sc_ref, gt_ref, f_ref, dp_ref, dr_ref, df_ref,
             dsc_ref, dsh_ref, dg_ref, db_ref, dgt_ref) = refs
        _zero_first(pl.program_id(0) == 0, dsc_ref, dsh_ref, dg_ref, db_ref, dgt_ref)
        du = (jnp.concatenate([duf_ref[q] for q in range(ngb)], axis=1).astype(F32)
              + jnp.concatenate([dub_ref[q] for q in range(ngb)], axis=1).astype(F32))
        if has_lat:
            du = du + d_ref[...] * jnp.concatenate([dyl_ref[q] for q in range(ngb)], axis=1).astype(F32)
            dp_ref[:, e:2 * e] = dz_ref[...]
        else:
            dp_ref[:, e:2 * e] = jnp.zeros((tb, e), BF16)
        dp_ref[:, 0:e] = du.astype(BF16)
        dh = jnp.zeros((tb, d_model), F32)
        for j in range(n_w):
            dh = dh + _dot(dp_ref[:, j * d_model:(j + 1) * d_model], w_ref[j])
        xh = xh_ref[...]
        x1 = xh * g_ref[...] + b_ref[...]
        dx1 = dh * (1.0 + sc_ref[...])
        if has_lat:
            dx1 = dx1 + dxr_ref[...]
        _acc(dsc_ref, _colsum(dh * x1))
        _acc(dsh_ref, _colsum(dh))
        _acc(dg_ref, _colsum(dx1 * xh))
        _acc(db_ref, _colsum(dx1))
        dxh = dx1 * g_ref[...]
        dr = rs_ref[...] * (dxh - _rowmean(dxh) - xh * _rowmean(dxh * xh))
        dr_ref[...] = dr
        df_ref[...] = (dr * gt_ref[...]).astype(BF16)
        _acc(dgt_ref, _colsum(dr * f_ref[...].astype(F32)))

    vec = pl.BlockSpec((1, d_model), lambda i: (0, 0))
    tok = pl.BlockSpec((tb, d_model), lambda i: (i, 0))
    gblk = pl.BlockSpec((ngb, tb, LANES), lambda i: (0, i, 0))
    in_specs = [pl.BlockSpec((ngb, tb, LANES), lambda i: (0, i + obf, 0)),
                pl.BlockSpec((ngb, tb, LANES), lambda i: (0, i + obb, 0))]
    args = [duf, dub]
    if has_lat:
        in_specs += [gblk, pl.BlockSpec((tb, e), lambda i: (i, 0)), pl.BlockSpec((1, e), lambda i: (0, 0)), tok]
        args += list(lat)
    in_specs += [pl.BlockSpec(w.shape, lambda i: (0, 0, 0)), tok, pl.BlockSpec((tb, 1), lambda i: (i, 0)), vec, vec, vec,
                 vec, tok]
    args += [w, xhat, rstd, ln[0], ln[1], sc, gt_prev, f_prev]
    vec_f = jax.ShapeDtypeStruct((1, d_model), F32)
    return pl.pallas_call(
        body, name=name, grid=(n_tok // tb,),
        out_shape=(jax.ShapeDtypeStruct((n_tok, 2 * e), BF16), jax.ShapeDtypeStruct((n_tok, d_model), F32),
                   jax.ShapeDtypeStruct((n_tok, d_model), BF16), vec_f, vec_f, vec_f, vec_f, vec_f),
        in_specs=in_specs,
        out_specs=(pl.BlockSpec((tb, 2 * e), lambda i: (i, 0)), tok, tok, vec, vec, vec, vec, vec),
        compiler_params=_params(1),
    )(*args)


def _conv_bwd_a(df, w_out_t, p, yc, *, tb, name):
    _, n_tok, e = p.shape
    d_model = df.shape[1]
    tb = min(tb, n_tok)
    cs = _slab_width(e)

    def body(df_ref, wo_ref, bg_ref, z_ref, yc_ref, dbg_ref, dz_ref, dyc_ref):
        dfv = df_ref[...]
        for c0 in range(0, e, cs):
            sl = slice(c0, c0 + cs)
            dgv = _dot(dfv, wo_ref[:, sl])
            zf = z_ref[0, :, sl].astype(F32)
            sz = _sigmoid(zf)
            silu_z = zf * sz
            bg = bg_ref[0, :, sl].astype(F32)
            yc = yc_ref[:, sl].astype(F32)
            dbg_ref[:, sl] = (dgv * yc * silu_z).astype(BF16)
            dyc_ref[:, sl] = (dgv * bg * silu_z).astype(BF16)
            dz_ref[:, sl] = (dgv * bg * yc * (sz * (1.0 + zf * (1.0 - sz)))).astype(BF16)

    wide = pl.BlockSpec((tb, e), lambda i: (i, 0))
    shape = jax.ShapeDtypeStruct((n_tok, e), BF16)
    return pl.pallas_call(
        body, name=name, grid=(n_tok // tb,), out_shape=(shape, shape, shape),
        in_specs=[pl.BlockSpec((tb, d_model), lambda i: (i, 0)), pl.BlockSpec((d_model, e), lambda i: (0, 0)),
                  pl.BlockSpec((1, tb, e), lambda i: (0, i, 0)), pl.BlockSpec((1, tb, e), lambda i: (3, i, 0)), wide],
        out_specs=(wide, wide, wide), compiler_params=_params(1),
    )(df, w_out_t, p, p, yc)


def _conv_bwd_b(dyc, p, dbg, dz, conv_w, *, grid_mode, tb, name, hosted=None):
    _, n_tok, e = p.shape
    eh = e // 2
    if not grid_mode:
        tb = n_tok
    tb = min(tb, n_tok)
    nb = n_tok // tb
    hb = tb // GRID_W
    cs = _slab_width(e)

    def body(*refs):
        if grid_mode:
            dyc_ref, dycp_ref, dycn_ref, cg_ref, v_ref, dbg_ref, dz_ref, cw_ref, dp_ref, dcw_ref = refs
        else:
            dyc_ref, cg_ref, v_ref, dbg_ref, dz_ref, cw_ref, dp_ref, dcw_ref = refs
        i = pl.program_id(0)
        _zero_first(i == 0, dcw_ref)
        rows = lax.broadcasted_iota(jnp.int32, (tb, 1), 0)
        dp_ref[0] = dbg_ref[...]
        dp_ref[3] = dz_ref[...]
        for c0 in range(0, e, cs):
            sl = slice(c0, c0 + cs)
            dyc = dyc_ref[:, sl].astype(F32)
            w = cw_ref[:, sl]
            if grid_mode and c0 >= eh:
                hs = slice(c0 - eh, c0 - eh + cs)
                dprev = jnp.where(i > 0, dycp_ref[:, hs].astype(F32), 0.0)
                dnext = jnp.where(i < nb - 1, dycn_ref[:, hs].astype(F32), 0.0)
                if tb > GRID_W:
                    dm = jnp.concatenate([dprev, dyc[:tb - GRID_W]], axis=0)
                    dpl = jnp.concatenate([dyc[GRID_W:], dnext], axis=0)
                else:
                    dm, dpl = dprev, dnext
            else:
                dm, dpl = _shifted(dyc, rows, GRID_W if grid_mode else tb, tb)
            cg = cg_ref[0, :, sl].astype(F32)
            v = v_ref[0, :, sl].astype(F32)
            u = cg * v
            du = w[0:1] * dpl + w[1:2] * dyc + w[2:3] * dm
            dp_ref[1, :, sl] = (du * v).astype(BF16)
            dp_ref[2, :, sl] = (du * cg).astype(BF16)
            _acc(dcw_ref.at[:, sl], jnp.concatenate([_colsum(u * dpl), _colsum(u * dyc), _colsum(u * dm)], axis=0))

    n_hrows = n_tok // GRID_W
    wide = pl.BlockSpec((tb, e), lambda i: (i, 0))
    in_specs = [wide]
    args = [dyc]
    if grid_mode:
        in_specs += [pl.BlockSpec((GRID_W, eh), lambda i: (jnp.maximum(i * hb - 1, 0), 1)),
                     pl.BlockSpec((GRID_W, eh), lambda i: (jnp.minimum((i + 1) * hb, n_hrows - 1), 1))]
        args += [dyc, dyc]
    in_specs += [pl.BlockSpec((1, tb, e), lambda i: (1, i, 0)), pl.BlockSpec((1, tb, e), lambda i: (2, i, 0)), wide, wide,
                 pl.BlockSpec((3, e), lambda i: (0, 0))]
    args += [p, p, dbg, dz, conv_w]
    outs, extra = _call(
        body, name=name, grid=(nb,),
        out_shape=[jax.ShapeDtypeStruct((4, n_tok, e), BF16), jax.ShapeDtypeStruct((3, e), F32)],
        in_specs=in_specs,
        out_specs=[pl.BlockSpec((4, tb, e), lambda i: (0, i, 0)), pl.BlockSpec((3, e), lambda i: (0, 0))],
        scratch_shapes=[], args=args, hosted=hosted)
    return (*outs, extra)


def _conv_inbwd(dp, w, dr, x, sc, *, tb, name, hosted=None):
    n_chunks, n_tok, e = dp.shape
    d_model = x.shape[1]
    tb = min(tb, n_tok)

    def body(dp_ref, w_ref, dr_ref, x_ref, sc_ref, gx_ref, dsc_ref, dsh_ref):
        _zero_first(pl.program_id(0) == 0, dsc_ref, dsh_ref)
        dh = _dot(dp_ref[0], w_ref[0])
        for k in range(1, n_chunks):
            dh = dh + _dot(dp_ref[k], w_ref[k])
        gx_ref[...] = DN_ALPHA * dr_ref[...] + dh * (1.0 + sc_ref[...])
        _acc(dsc_ref, _colsum(dh * x_ref[...]))
        _acc(dsh_ref, _colsum(dh))

    vec = pl.BlockSpec((1, d_model), lambda i: (0, 0))
    tok = pl.BlockSpec((tb, d_model), lambda i: (i, 0))
    vec_f = jax.ShapeDtypeStruct((1, d_model), F32)
    outs, extra = _call(
        body, name=name, grid=(n_tok // tb,),
        out_shape=[jax.ShapeDtypeStruct((n_tok, d_model), F32), vec_f, vec_f],
        in_specs=[pl.BlockSpec((n_chunks, tb, e), lambda i: (0, i, 0)),
                  pl.BlockSpec((n_chunks, e, d_model), lambda i: (0, 0, 0), pipeline_mode=pl.Buffered(1)),
                  tok, tok, vec],
        out_specs=[tok, vec, vec],
        scratch_shapes=[], args=(dp, w, dr, x, sc), hosted=hosted)
    return (*outs, extra)


def _wgrad(a, b, *, n_chunks, tm, tl, init=None, name):
    n_tok, m = a.shape
    tl = min(tl, n_tok)
    chunked = b.ndim == 3
    cw = b.shape[2] if chunked else b.shape[1] // n_chunks
    has_init = init is not None

    def body(*refs):
        if has_init:
            a_ref, b_ref, init_ref, o_ref = refs
        else:
            a_ref, b_ref, o_ref = refs
        @pl.when(pl.program_id(2) == 0)
        def _():
            o_ref[0] = init_ref[0] if has_init else jnp.zeros_like(o_ref[0])

        o_ref[0] += _dot_tn(a_ref[...], b_ref[0] if chunked else b_ref[...])

    o_spec = pl.BlockSpec((1, tm, cw), lambda jm, jc, l: (jc, jm, 0))
    b_spec = (pl.BlockSpec((1, tl, cw), lambda jm, jc, l: (jc, l, 0)) if chunked
              else pl.BlockSpec((tl, cw), lambda jm, jc, l: (l, jc)))
    init_spec = pl.BlockSpec((1, tm, cw), lambda jm, jc, l: (jc, jm, 0), pipeline_mode=pl.Buffered(1))
    in_specs = [pl.BlockSpec((tl, tm), lambda jm, jc, l: (l, jm)), b_spec] + ([init_spec] if has_init else [])
    args = (a, b) + ((init,) if has_init else ())
    return pl.pallas_call(
        body, name=name, grid=(m // tm, n_chunks, n_tok // tl),
        out_shape=jax.ShapeDtypeStruct((n_chunks, m, cw), F32),
        in_specs=in_specs, out_specs=o_spec, compiler_params=_params(3),
    )(*args)


def _block_diag(t, ngb):
    g, p, n = t.shape
    gpb = g // ngb
    eye = jnp.eye(gpb, dtype=t.dtype)
    return jnp.einsum("bgpn,gh->bgphn", t.reshape(ngb, gpb, p, n), eye).reshape(ngb, gpb * p, gpb * n)


def _block_diag_t(mat, g, p, n):
    ngb = mat.shape[0]
    gpb = g // ngb
    eye = jnp.eye(gpb, dtype=mat.dtype)
    return jnp.einsum("bgphn,gh->bgpn", mat.reshape(ngb, gpb, p, gpb, n), eye).reshape(g, p, n)


def _scan_tables(pw_r, pw_i, ngb, reverse):
    _, g, n = pw_r.shape
    sb = g * n // ngb
    rows = jnp.arange(SUBLANES)
    kinds = []
    for step in (1, 2, 4):
        mask = ((rows < SUBLANES - step) if reverse else (rows >= step)).astype(F32)
        for part in (pw_r[step - 1], pw_i[step - 1]):
            kinds.append(part.reshape(ngb, 1, sb) * mask[None, :, None])
    for part in (pw_r, pw_i):
        pw = part[::-1] if reverse else part
        kinds.append(jnp.transpose(pw.reshape(SUBLANES, ngb, sb), (1, 0, 2)))
    return jnp.stack(kinds, axis=1)


def _flat(parts):
    return jnp.concatenate([p.reshape(-1) for p in parts])


def _unflat(vec, shapes):
    out, off = [], 0
    for s in shapes:
        size = math.prod(s)
        out.append(vec[off:off + size].reshape(s))
        off += size
    return out


def kernel(x, c, ctx, c_ctx, ada_w, ada_b, ln_g, ln_b, conv_w_in, conv_w, conv_w_out, ssm_w_in, ssm_lam_re, ssm_lam_im, ssm_log_step, ssm_b_re, ssm_b_im, ssm_c_re, ssm_c_im, ssm_d, ssm_w_glu, ssm_b_glu, ssm_w_out, loss_target, m_c_ctx, m_ada_w, m_ada_b, m_ln_g, m_ln_b, m_conv_w_in, m_conv_w, m_conv_w_out, m_ssm_w_in, m_ssm_lam_re, m_ssm_lam_im, m_ssm_log_step, m_ssm_b_re, m_ssm_b_im, m_ssm_c_re, m_ssm_c_im, m_ssm_d, m_ssm_w_glu, m_ssm_b_glu, m_ssm_w_out, v_c_ctx, v_ada_w, v_ada_b, v_ln_g, v_ln_b, v_conv_w_in, v_conv_w, v_conv_w_out, v_ssm_w_in, v_ssm_lam_re, v_ssm_lam_im, v_ssm_log_step, v_ssm_b_re, v_ssm_b_im, v_ssm_c_re, v_ssm_c_im, v_ssm_d, v_ssm_w_glu, v_ssm_b_glu, v_ssm_w_out):
    weights = dict(c_ctx=c_ctx, ada_w=ada_w, ada_b=ada_b, ln_g=ln_g, ln_b=ln_b, conv_w_in=conv_w_in, conv_w=conv_w,
                   conv_w_out=conv_w_out, ssm_w_in=ssm_w_in, ssm_lam_re=ssm_lam_re, ssm_lam_im=ssm_lam_im,
                   ssm_log_step=ssm_log_step, ssm_b_re=ssm_b_re, ssm_b_im=ssm_b_im, ssm_c_re=ssm_c_re,
                   ssm_c_im=ssm_c_im, ssm_d=ssm_d, ssm_w_glu=ssm_w_glu, ssm_b_glu=ssm_b_glu, ssm_w_out=ssm_w_out)
    mom_m = dict(c_ctx=m_c_ctx, ada_w=m_ada_w, ada_b=m_ada_b, ln_g=m_ln_g, ln_b=m_ln_b, conv_w_in=m_conv_w_in,
                 conv_w=m_conv_w, conv_w_out=m_conv_w_out, ssm_w_in=m_ssm_w_in, ssm_lam_re=m_ssm_lam_re,
                 ssm_lam_im=m_ssm_lam_im, ssm_log_step=m_ssm_log_step, ssm_b_re=m_ssm_b_re, ssm_b_im=m_ssm_b_im,
                 ssm_c_re=m_ssm_c_re, ssm_c_im=m_ssm_c_im, ssm_d=m_ssm_d, ssm_w_glu=m_ssm_w_glu,
                 ssm_b_glu=m_ssm_b_glu, ssm_w_out=m_ssm_w_out)
    mom_v = dict(c_ctx=v_c_ctx, ada_w=v_ada_w, ada_b=v_ada_b, ln_g=v_ln_g, ln_b=v_ln_b, conv_w_in=v_conv_w_in,
                 conv_w=v_conv_w, conv_w_out=v_conv_w_out, ssm_w_in=v_ssm_w_in, ssm_lam_re=v_ssm_lam_re,
                 ssm_lam_im=v_ssm_lam_im, ssm_log_step=v_ssm_log_step, ssm_b_re=v_ssm_b_re, ssm_b_im=v_ssm_b_im,
                 ssm_c_re=v_ssm_c_re, ssm_c_im=v_ssm_c_im, ssm_d=v_ssm_d, ssm_w_glu=v_ssm_w_glu,
                 ssm_b_glu=v_ssm_b_glu, ssm_w_out=v_ssm_w_out)
    names = list(weights)

    n_lat, d_model = x.shape[1], x.shape[2]
    n_ctx = ctx.shape[1]
    e = 2 * d_model
    n_grp, n_state, grp = ssm_lam_re.shape[2], ssm_lam_re.shape[3], ssm_b_re.shape[4]
    ngb = e // LANES
    ws = ada_w.shape[2]
    tb_tok = min(512, n_lat)
    n_seq = n_ctx + n_lat
    tb_glu = math.gcd(256, n_ctx)
    chip = 2 * lax.axis_index("x") + lax.axis_index("y")
    me = 2 * chip + lax.axis_index("c")
    chips, everyone, pair = ("x", "y"), MESH_AXES, ("c",)

    x2, ctx2, tgt2 = x[0], ctx[0], loss_target[0]

    wc_in_own = conv_w_in[0].astype(BF16)
    later_weights = _Hosted([(w[0].astype(BF16), chips, False) for w in (conv_w_out, ssm_w_in, ssm_w_glu, ssm_w_out)])
    small_full = _exchange(_flat([conv_w[0], ssm_d[0], ssm_b_glu[0]]).reshape(1, -1), chips, False, "ag_small")
    es = conv_w.shape[2]
    conv_w_full = jnp.transpose(small_full[:, 0, :3 * es].reshape(4, 3, es), (1, 0, 2)).reshape(3, e)
    d_full = small_full[:, 0, 3 * es:4 * es].reshape(1, e)
    b_glu_full = small_full[:, 0, 4 * es:5 * es].reshape(1, e)

    c_all = _exchange(c, everyone, False, "ag_c").reshape(8, d_model)
    cc2 = c_ctx.reshape(1, d_model)
    b_sh = lax.dynamic_slice_in_dim(ada_b, chip * ws, ws, axis=1).reshape(DEPTH, 1, ws)
    m_sh = _ada_fwd(c_all, cc2, ada_w, b_sh)
    m_all = _exchange(m_sh, chips, False, "ag_mod")
    m_full = jnp.transpose(m_all, (1, 2, 0, 3)).reshape(DEPTH, 16, 3 * d_model)
    m_lat = lax.dynamic_slice_in_dim(m_full, me, 1, axis=1)
    m_ctx = m_full[:, 8:9]

    def mods(m, i):
        return m[i, :, 0:d_model], m[i, :, d_model:2 * d_model], m[i, :, 2 * d_model:3 * d_model]

    sh0, sc0, gt0 = mods(m_lat, 0)
    sh1, sc1, gt1 = mods(m_lat, 1)
    shc0, scc0, gtc0 = mods(m_ctx, 0)
    shc1, scc1, _ = mods(m_ctx, 1)
    ln0 = (ln_g[0:1], ln_b[0:1])
    ln1 = (ln_g[1:2], ln_b[1:2])

    def lam_view(t):
        return jnp.transpose(t[0], (0, 2, 1)).reshape(2 * n_state, n_grp)

    def lam_back(t):
        return jnp.transpose(t.reshape(2, n_state, n_grp), (0, 2, 1)).reshape(ssm_lam_re.shape)

    def b_view(t):
        return jnp.transpose(t[0], (0, 2, 3, 1)).reshape(2 * n_state * grp, n_grp)

    def b_back(t):
        return jnp.transpose(t.reshape(2, n_state, grp, n_grp), (0, 3, 1, 2)).reshape(ssm_b_re.shape)

    def c_view(t):
        return jnp.transpose(t[0], (0, 2, 3, 1)).reshape(2 * grp * n_state, n_grp)

    def c_back(t):
        return jnp.transpose(t.reshape(2, grp, n_state, n_grp), (0, 3, 1, 2)).reshape(ssm_c_re.shape)

    def channel_major(t):
        return jnp.transpose(t.reshape(2 * n_state, grp, n_grp), (1, 0, 2))

    def by_group(t):
        return jnp.transpose(t.reshape(t.shape[0], 2, n_state, n_grp), (0, 1, 3, 2))

    lam_re2, lam_im2, log_step2 = lam_view(ssm_lam_re), lam_view(ssm_lam_im), ssm_log_step[0]
    b_re_t, b_im_t = channel_major(b_view(ssm_b_re)), channel_major(b_view(ssm_b_im))
    pw_r, pw_i, pq_r, pq_i, bbr, bbi = _zoh_fwd(lam_re2, lam_im2, log_step2, b_re_t, b_im_t)
    sbk = n_grp * n_state // ngb
    pw_r, pw_i, pq_r, pq_i = (by_group(t) for t in (pw_r, pw_i, pq_r, pq_i))
    bbr_g = jnp.transpose(by_group(bbr), (1, 2, 0, 3))
    bbi_g = jnp.transpose(by_group(bbi), (1, 2, 0, 3))

    def power_rows(pw, r, first):
        full = jnp.concatenate([jnp.full((1, n_grp, n_state), first, F32), pw[:, r]], axis=0)
        return jnp.transpose(full.reshape(CHUNK + 1, ngb, sbk), (1, 0, 2))

    s5 = []
    for r in range(2):
        prm = dict(bre=_block_diag(bbr_g[r], ngb), bim=_block_diag(bbi_g[r], ngb),
                   cre=_block_diag(ssm_c_re[0, r], ngb), cim=_block_diag(ssm_c_im[0, r], ngb),
                   wr=power_rows(pw_r, r, 1.0), wi=power_rows(pw_i, r, 0.0))
        half_rows = wc_in_own[r * (d_model // 2):(r + 1) * (d_model // 2)]
        t_op, bp_op, cp_op, (wc_in_half,) = _s5_ops(
            prm["bre"], prm["bim"], prm["cre"], prm["cim"], prm["wr"], prm["wi"], reverse=(r == 1),
            name=f"l1_s5_ops{r}", hosted=_Hosted([(half_rows, chips, False)]))
        s5.append(dict(
            prm, t=t_op, bp=bp_op, cp=cp_op, wc_in_half=wc_in_half,
            tab=_scan_tables(pq_r[:, r], pq_i[:, r], ngb, reverse=(r == 1)),
            tab_adj=_scan_tables(pq_r[:, r], -pq_i[:, r], ngb, reverse=(r == 0))))
    wc_in = jnp.concatenate([s5[0]["wc_in_half"], s5[1]["wc_in_half"]], axis=1)

    p0, h0, gathered = _inproj(x2, sc0, sh0, wc_in, tb=min(1024, n_lat), name="l0_inproj", hosted=later_weights)
    wc_out, ws_in, w_glu, ws_out = gathered
    wc_out, w_glu, ws_out = wc_out.reshape(e, d_model), w_glu.reshape(e, e), ws_out.reshape(e, d_model)
    wc_in_t, ws_in_t, wc_out_t = jnp.transpose(wc_in, (0, 2, 1)), jnp.transpose(ws_in, (0, 2, 1)), wc_out.T
    pc0, hc0 = _inproj(ctx2, scc0, shc0, wc_in, tb=tb_tok, name="l0_inproj_ctx")
    xhat0, rstd0, g0, yc0, f0 = _convgate(p0, x2, gt0, conv_w_full, wc_out, *ln0, grid_mode=True, tb=tb_tok, name="l0_conv")
    chat0, crstd0, gc0, ycc0, fc0 = _convgate(pc0, ctx2, gtc0, conv_w_full, wc_out, *ln0, grid_mode=False, tb=tb_tok,
                                              name="l0_conv_ctx")

    seq_rows = [(n_seq, n_ctx), (n_seq, 0)]
    useq_f, useq_b, h1 = _inproj(xhat0, sc1, sh1, ws_in[0:2], lnaff=ln0, tb=math.gcd(tb_tok, n_ctx), gb_rows=seq_rows,
                                 name="l1_inproj_u")
    z1, _ = _inproj(xhat0, sc1, sh1, ws_in[2:4], lnaff=ln0, tb=min(1024, n_lat), name="l1_inproj_z")
    uc, hc1 = _inproj(chat0, scc1, shc1, ws_in[0:2], lnaff=ln0, tb=tb_tok, gb_rows=[(n_ctx, 0)], name="l1_inproj_ctx")
    useq = [useq_f.at[:, 0:n_ctx].set(uc), useq_b.at[:, n_lat:].set(uc)]
    y_dir, hp_dir = [], []
    for r in range(2):
        yr, hcr = _s5_fwd(useq[r], s5[r]["t"], s5[r]["bp"], s5[r]["cp"], s5[r]["tab"], reverse=(r == 1),
                          name=f"l1_s5_fwd{r}")
        y_dir.append(yr)
        hp_dir.append(hcr)

    (loss, dxres, do1, gz1, gg1, dq1, dz1, dy_f, dy_b, dg1, db1, dgt1, dbglu, dd) = _glu_loss(
        useq[0], y_dir[0], y_dir[1], z1, xhat0, ln0, gt1, d_full, w_glu, b_glu_full, ws_out, ln1, tgt2,
        offs=(n_ctx, n_ctx, 0), dy_rows=seq_rows, tb=tb_glu, name="l1_glu_loss")
    no_dy = jnp.zeros((ngb, n_ctx, LANES), BF16)
    dy_dir = [dy_f.at[:, 0:n_ctx].set(no_dy), dy_b.at[:, n_lat:].set(no_dy)]

    tl = min(1024, n_lat)

    def owner_slices(name, full):
        w = weights[name]
        return full.reshape(8, math.prod(w.shape[:-1]) // 2, w.shape[-1])

    def scatter(named):
        return _Hosted([(owner_slices(name, full), everyone, True) for name, full in named])

    def siblings(names):
        return _Hosted([(_sum_parts(rs_parts[name], "sum_" + name), pair, False) for name in names])

    rs_parts, both_halves = {}, {}

    gw_glu = _wgrad(gg1, dq1, n_chunks=1, tm=e // 2, tl=tl, name="wg_glu")
    gw_ssm_out = _wgrad(gz1, do1, n_chunks=1, tm=e, tl=tl, name="wg_ssm_out")
    du_dir, s5_grads = [], []
    for r in range(2):
        if r == 0:
            hosted = scatter([("ssm_w_glu", gw_glu), ("ssm_w_out", gw_ssm_out)])
        else:
            hosted = siblings(["ssm_w_glu", "ssm_w_out"])
        dur, dt_op, dbp_op, dcp_op, da8, extra = _s5_bwd(useq[r], dy_dir[r], hp_dir[r], s5[r]["t"], s5[r]["bp"],
                                                         s5[r]["cp"], s5[r]["tab_adj"], reverse=(r == 1),
                                                         name=f"l1_s5_bwd{r}", hosted=hosted)
        if r == 0:
            rs_parts["ssm_w_glu"], rs_parts["ssm_w_out"] = extra
        else:
            both_halves["ssm_w_glu"], both_halves["ssm_w_out"] = extra
        du_dir.append(dur)
        prm = s5[r]
        s5_grads.append(functools.partial(
            _s5_ops_bwd, prm["bre"], prm["bim"], prm["cre"], prm["cim"], prm["wr"], prm["wi"], prm["wr"][:, 1:2],
            prm["wi"][:, 1:2], dt_op, dbp_op, dcp_op, da8, reverse=(r == 1), name=f"l1_s5_ops_bwd{r}"))
    dp1, dr0, df0, dsc1, dsh1, dg0, db0, dgt0 = _ssm_inbwd(
        du_dir[0], du_dir[1], ws_in_t, xhat0, rstd0, ln0, sc1, gt0, f0, lat=(dy_dir[1], dz1, d_full, dxres),
        row_f=n_ctx, row_b=0, tb=tb_glu, name="l1_inbwd")
    dpc1, drc0, dfc0, dscc1, dshc1, dgc0, dbc0, dgtc0 = _ssm_inbwd(
        du_dir[0], du_dir[1], ws_in_t, chat0, crstd0, ln0, scc1, gtc0, fc0, lat=None,
        row_f=0, row_b=n_lat, tb=n_ctx, name="l1_inbwd_ctx")

    def conv_backward(df, p, yc, dr, xin, sc, grid_mode, tag, hosted_b=None, hosted_in=None):
        dbg, dz, dyc = _conv_bwd_a(df, wc_out_t, p, yc, tb=tb_tok, name="l0_bwd_a" + tag)
        dp, dcw, extra_b = _conv_bwd_b(dyc, p, dbg, dz, conv_w_full, grid_mode=grid_mode, tb=tb_glu,
                                       name="l0_bwd_b" + tag, hosted=hosted_b)
        gx, dsc, dsh, extra_in = _conv_inbwd(dp, wc_in_t, dr, xin, sc, tb=tb_tok, name="l0_inbwd" + tag,
                                             hosted=None if hosted_in is None else hosted_in(dp, extra_b))
        return dp, dcw, gx, dsc, dsh, extra_b, extra_in

    dpc0, dcwc0, _, dscc0, dshc0, _, _ = conv_backward(dfc0, pc0, ycc0, drc0, ctx2, scc0, False, "_ctx")
    gw_conv_out = _wgrad(g0, df0, n_chunks=1, tm=e, tl=tl, name="wg_conv_out",
                         init=_wgrad(gc0, dfc0, n_chunks=1, tm=e, tl=tl, name="wg_conv_out_ctx"))
    gw_ssm_in = _wgrad(h1, dp1, n_chunks=4, tm=d_model, tl=tl, name="wg_ssm_in",
                       init=_wgrad(hc1, dpc1, n_chunks=4, tm=d_model, tl=tl, name="wg_ssm_in_ctx"))
    gw_conv_in_ctx = _wgrad(hc0, dpc0, n_chunks=4, tm=d_model, tl=tl, name="wg_conv_in_ctx")

    def behind_inbwd(dp, arrived):
        rs_parts["ssm_w_in"], rs_parts["conv_w_out"] = arrived
        gw_conv_in = _wgrad(h0, dp, n_chunks=4, tm=d_model, tl=tl, name="wg_conv_in", init=gw_conv_in_ctx)
        both = siblings(["ssm_w_in", "conv_w_out"])
        return _Hosted(scatter([("conv_w_in", gw_conv_in)]).items + both.items)

    dp0, dcw0, grad_x, dsc0, dsh0, _, extra_in = conv_backward(
        df0, p0, yc0, dr0, x2, sc0, True, "", hosted_b=scatter([("ssm_w_in", gw_ssm_in), ("conv_w_out", gw_conv_out)]),
        hosted_in=behind_inbwd)
    rs_parts["conv_w_in"], both_halves["ssm_w_in"], both_halves["conv_w_out"] = extra_in
    *grads_r0, _ = s5_grads[0]()
    *grads_r1, (both_halves["conv_w_in"],) = s5_grads[1](hosted=siblings(["conv_w_in"]))
    s5_grads = [grads_r0, grads_r1]

    grads, deltas, new_m, new_v = {}, {}, {}, {}
    for name in ("ssm_w_glu", "ssm_w_out", "ssm_w_in", "conv_w_out", "conv_w_in"):
        w = weights[name]
        rows, cols = math.prod(w.shape[:-1]), w.shape[-1]
        both = both_halves[name].reshape(rows, cols)
        dlt, nm, nv = _adamw(w.reshape(rows, cols), both, mom_m[name].reshape(rows, cols),
                             mom_v[name].reshape(rows, cols), "adamw_" + name)
        grads[name], deltas[name] = both.reshape(w.shape), dlt.reshape(w.shape)
        new_m[name], new_v[name] = nm.reshape(w.shape), nv.reshape(w.shape)

    gpn = (n_grp, grp, n_state)
    small_parts = [
        jnp.concatenate([dg0 + dgc0, dg1], axis=0), jnp.concatenate([db0 + dbc0, db1], axis=0),
        dcw0 + dcwc0, dd, dbglu,
        jnp.stack([s5_grads[r][4] for r in range(2)]),
    ] + [jnp.stack([_block_diag_t(s5_grads[r][k], *gpn) for r in range(2)]) for k in range(4)]
    small_shapes = [p.shape for p in small_parts]
    flat = _flat(small_parts)
    quantum = 8 * SUBLANES * LANES
    n_flat = -(-flat.shape[0] // quantum) * quantum
    flat = jnp.pad(flat, (0, n_flat - flat.shape[0])).reshape(8, n_flat // (8 * LANES), LANES)
    red = _sum_parts(_exchange(flat, everyone, True, "rs_small"), "sum_small")
    red = _exchange(red, everyone, False, "ag_small_grads").reshape(-1)
    g_ln_g, g_ln_b, g_conv_w, g_d, g_bglu, g_a, g_bbr, g_bbi, g_cre, g_cim = _unflat(red, small_shapes)

    def groups_minor(t, lead):
        return jnp.moveaxis(t, 1, -1).reshape(lead, n_grp)

    g_a = g_a.reshape(2, ngb, 2, sbk)
    dar = groups_minor(g_a[:, :, 0].reshape(2, n_grp, n_state), 2 * n_state)
    dai = groups_minor(g_a[:, :, 1].reshape(2, n_grp, n_state), 2 * n_state)
    dbbr_t = jnp.transpose(g_bbr, (2, 0, 3, 1)).reshape(grp, 2 * n_state, n_grp)
    dbbi_t = jnp.transpose(g_bbi, (2, 0, 3, 1)).reshape(grp, 2 * n_state, n_grp)
    z_lre, z_lim, z_ls, z_bre, z_bim = _zoh_bwd(lam_re2, lam_im2, log_step2, b_re_t, b_im_t, dar, dai, dbbr_t, dbbi_t)

    zero = jnp.zeros((1, d_model), F32)
    dm_rows = jnp.stack([
        jnp.stack([jnp.concatenate([dsh0, dsc0, dgt0], axis=1), jnp.concatenate([dshc0, dscc0, dgtc0], axis=1)]),
        jnp.stack([jnp.concatenate([dsh1, dsc1, dgt1], axis=1), jnp.concatenate([dshc1, dscc1, zero], axis=1)]),
    ]).reshape(DEPTH, 2, 3 * d_model)
    dm_all = _exchange(dm_rows, everyone, False, "ag_dmod")
    dm_sh = lax.dynamic_slice_in_dim(dm_all, chip * ws, ws, axis=3)
    g_ada_w, g_ada_b, ds_part = _ada_bwd(c_all, cc2, ada_w, dm_all, dm_sh)
    g_cctx = _cctx_grad(_exchange(ds_part, chips, False, "ag_dsctx"), cc2)

    grads["ada_w"] = g_ada_w
    dlt, nm, nv = _adamw(ada_w.reshape(-1, ws), g_ada_w.reshape(-1, ws), m_ada_w.reshape(-1, ws),
                         v_ada_w.reshape(-1, ws), "adamw_ada_w")
    deltas["ada_w"], new_m["ada_w"], new_v["ada_w"] = dlt.reshape(ada_w.shape), nm.reshape(ada_w.shape), nv.reshape(ada_w.shape)

    def chip_cols(full, rows):
        return lax.dynamic_slice_in_dim(full.reshape(rows, e), chip * es, es, axis=1)

    def same(t):
        return t

    def channel_minor_back(t):
        return jnp.transpose(t, (1, 0, 2)).reshape(2 * n_state * grp, n_grp)

    small = dict(
        c_ctx=(g_cctx, lambda t: t.reshape(1, d_model), lambda t: t.reshape(c_ctx.shape)),
        ada_b=(g_ada_b.reshape(ada_b.shape), same, same),
        ln_g=(g_ln_g, same, same), ln_b=(g_ln_b, same, same),
        conv_w=(chip_cols(g_conv_w, 3), lambda t: t[0], lambda t: t.reshape(conv_w.shape)),
        ssm_lam_re=(z_lre, lam_view, lam_back), ssm_lam_im=(z_lim, lam_view, lam_back),
        ssm_log_step=(z_ls, lambda t: t[0], lambda t: t.reshape(ssm_log_step.shape)),
        ssm_b_re=(channel_minor_back(z_bre), b_view, b_back), ssm_b_im=(channel_minor_back(z_bim), b_view, b_back),
        ssm_c_re=(groups_minor(g_cre, 2 * grp * n_state), c_view, c_back),
        ssm_c_im=(groups_minor(g_cim, 2 * grp * n_state), c_view, c_back),
        ssm_d=(chip_cols(g_d, 1), same, same), ssm_b_glu=(chip_cols(g_bglu, 1), same, same))
    for n, (g_view, view, back) in small.items():
        dlt, nm, nv = _adamw(view(weights[n]), g_view, view(mom_m[n]), view(mom_v[n]), "adamw_" + n)
        grads[n], deltas[n], new_m[n], new_v[n] = back(g_view), back(dlt), back(nm), back(nv)

    loss_total = lax.psum(loss[0, 0], MESH_AXES)
    return (loss_total, grad_x.reshape(x.shape), *[grads[n] for n in names], *[deltas[n] for n in names],
            *[new_m[n] for n in names], *[new_v[n] for n in names])
```

```python
import functools
import math

import jax
import jax.numpy as jnp
from jax import lax
from jax.experimental import pallas as pl
from jax.experimental.pallas import tpu as pltpu

F32 = jnp.float32
BF16 = jnp.bfloat16
LANES = 128
SUBLANES = 8
VMEM_LIMIT = 56 * 1024 * 1024
MESH_AXES = ("x", "y", "c")
HIGHEST = lax.Precision.HIGHEST

GRID_W = 64
LN_EPS = 1e-5
DEPTH = 2
DN_ALPHA = (2 * DEPTH) ** 0.25
ADAM_LR, ADAM_B1, ADAM_B2, ADAM_EPS, ADAM_WD, ADAM_STEP = 0.001, 0.9, 0.999, 1e-08, 0.01, 10
GELU_K = math.sqrt(2.0 / math.pi)
GELU_C = 0.044715


def _params(n_grid_axes):
    return pltpu.CompilerParams(dimension_semantics=("arbitrary",) * n_grid_axes, vmem_limit_bytes=VMEM_LIMIT)


def _dot(a, b):
    return jnp.dot(a, b, preferred_element_type=F32)


def _dot_nt(a, b):
    return lax.dot_general(a, b, (((1,), (1,)), ((), ())), preferred_element_type=F32)


def _dot_tn(a, b):
    return lax.dot_general(a, b, (((0,), (0,)), ((), ())), preferred_element_type=F32)


def _sigmoid(x):
    return 0.5 * jnp.tanh(0.5 * x) + 0.5


def _colsum(x):
    return jnp.sum(x, axis=0, keepdims=True)


def _rowmean(x):
    return jnp.mean(x, axis=-1, keepdims=True)


def _zero_first(first, *refs):
    @pl.when(first)
    def _():
        for ref in refs:
            ref[...] = jnp.zeros_like(ref)


def _acc(ref, value):
    ref[...] += value


def _exchange_copies(src_ref, out_ref, send_sems, recv_sems, own_sem, axes, all_to_all, sem0=0):
    n_peers = 2 ** len(axes)
    pos = {a: lax.axis_index(a) for a in MESH_AXES}

    def index(p):
        return sum(p[a] * (2 ** (len(axes) - 1 - i)) for i, a in enumerate(axes))

    me = index(pos)
    own = pltpu.make_async_copy(src_ref.at[me] if all_to_all else src_ref, out_ref.at[me], own_sem)
    copies = []
    for k in range(1, n_peers):
        peer = dict(pos)
        for i, a in enumerate(axes):
            if (k >> (len(axes) - 1 - i)) & 1:
                peer[a] = 1 - pos[a]
        copies.append(pltpu.make_async_remote_copy(
            src_ref=src_ref.at[index(peer)] if all_to_all else src_ref,
            dst_ref=out_ref.at[me],
            send_sem=send_sems.at[sem0 + k - 1],
            recv_sem=recv_sems.at[sem0 + k - 1],
            device_id=tuple(peer[a] for a in MESH_AXES),
            device_id_type=pl.DeviceIdType.MESH,
        ))
    return copies, own


def _exchange_shape(src, axes, all_to_all):
    block = tuple(src.shape[1:] if all_to_all else src.shape)
    return jax.ShapeDtypeStruct((2 ** len(axes),) + block, src.dtype)


def _exchange(src, axes, all_to_all, name):
    n_peers = 2 ** len(axes)

    def body(src_ref, out_ref, send_sems, recv_sems, own_sem):
        copies, own = _exchange_copies(src_ref, out_ref, send_sems, recv_sems, own_sem, axes, all_to_all)
        own.start()
        for cp in copies:
            cp.start()
        for cp in copies:
            cp.wait()
        own.wait()

    return pl.pallas_call(
        body,
        name=name,
        out_shape=_exchange_shape(src, axes, all_to_all),
        in_specs=[pl.BlockSpec(memory_space=pltpu.HBM)],
        out_specs=pl.BlockSpec(memory_space=pltpu.HBM),
        scratch_shapes=[
            pltpu.SemaphoreType.DMA((n_peers - 1,)),
            pltpu.SemaphoreType.DMA((n_peers - 1,)),
            pltpu.SemaphoreType.DMA,
        ],
    )(src)


class _Hosted:
    def __init__(self, items):
        self.items = items
        self.args = [src for src, _, _ in items]
        self.in_specs = [pl.BlockSpec(memory_space=pltpu.HBM)] * len(items)
        self.out_specs = [pl.BlockSpec(memory_space=pltpu.HBM)] * len(items)
        self.out_shapes = [_exchange_shape(*item) for item in items]
        n_remote = sum(2 ** len(axes) - 1 for _, axes, _ in items)
        self.scratch = [pltpu.SemaphoreType.DMA((n_remote,)), pltpu.SemaphoreType.DMA((n_remote,)),
                        pltpu.SemaphoreType.DMA((len(items),))]

    def _copies(self, src_refs, out_refs, send_sems, recv_sems, own_sems):
        out, sem0 = [], 0
        for n, (_, axes, all_to_all) in enumerate(self.items):
            copies, own = _exchange_copies(src_refs[n], out_refs[n], send_sems, recv_sems, own_sems.at[n], axes,
                                           all_to_all, sem0)
            out += [own] + copies
            sem0 += len(copies)
        return out

    def start(self, *refs):
        for cp in self._copies(*refs):
            cp.start()

    def wait(self, *refs):
        for cp in self._copies(*refs):
            cp.wait()


def _call(body, *, name, grid, in_specs, out_specs, out_shape, scratch_shapes, args, hosted=None):
    params = _params(len(grid))
    if hosted is None:
        outs = pl.pallas_call(body, name=name, grid=grid, in_specs=in_specs, out_specs=tuple(out_specs),
                              out_shape=tuple(out_shape), scratch_shapes=list(scratch_shapes), compiler_params=params)(*args)
        return list(outs), []
    n_in, n_out, n_scr, n_h = len(in_specs), len(out_shape), len(scratch_shapes), len(hosted.items)

    def wrapped(*refs):
        ins, h_in = refs[:n_in], refs[n_in:n_in + n_h]
        outs, h_out = refs[n_in + n_h:n_in + n_h + n_out], refs[n_in + n_h + n_out:n_in + 2 * n_h + n_out]
        scr = refs[n_in + 2 * n_h + n_out:]
        first = functools.reduce(jnp.logical_and, [pl.program_id(k) == 0 for k in range(len(grid))])
        last = functools.reduce(jnp.logical_and, [pl.program_id(k) == grid[k] - 1 for k in range(len(grid))])

        @pl.when(first)
        def _():
            hosted.start(h_in, h_out, *scr[n_scr:])

        body(*ins, *outs, *scr[:n_scr])

        @pl.when(last)
        def _():
            hosted.wait(h_in, h_out, *scr[n_scr:])

    outs = pl.pallas_call(
        wrapped, name=name, grid=grid, in_specs=[*in_specs, *hosted.in_specs],
        out_specs=(*out_specs, *hosted.out_specs), out_shape=(*out_shape, *hosted.out_shapes),
        scratch_shapes=[*scratch_shapes, *hosted.scratch], compiler_params=params)(*args, *hosted.args)
    return list(outs[:n_out]), list(outs[n_out:])


def _sum_parts(parts, name):
    n_parts, rows, cols = parts.shape
    tr = rows
    while n_parts * tr * cols * 4 > 8 * 1024 * 1024 and tr % 16 == 0:
        tr //= 2

    def body(p_ref, o_ref):
        total = p_ref[0]
        for k in range(1, n_parts):
            total = total + p_ref[k]
        o_ref[...] = total

    return pl.pallas_call(
        body,
        name=name,
        grid=(rows // tr,),
        out_shape=jax.ShapeDtypeStruct((rows, cols), F32),
        in_specs=[pl.BlockSpec((n_parts, tr, cols), lambda i: (0, i, 0))],
        out_specs=pl.BlockSpec((tr, cols), lambda i: (i, 0)),
        compiler_params=_params(1),
    )(parts)


def _adamw(w, g, m, v, name):
    rows, cols = w.shape
    tr = rows
    while tr * cols * 4 > 2 * 1024 * 1024 and tr % 16 == 0:
        tr //= 2

    def body(w_ref, g_ref, m_ref, v_ref, d_ref, nm_ref, nv_ref):
        gv = g_ref[...]
        nm = ADAM_B1 * m_ref[...] + (1.0 - ADAM_B1) * gv
        nv = ADAM_B2 * v_ref[...] + (1.0 - ADAM_B2) * (gv * gv)
        m_hat = nm / (1.0 - ADAM_B1 ** ADAM_STEP)
        v_hat = nv / (1.0 - ADAM_B2 ** ADAM_STEP)
        d_ref[...] = -ADAM_LR * (m_hat / (jnp.sqrt(v_hat) + ADAM_EPS) + ADAM_WD * w_ref[...])
        nm_ref[...] = nm
        nv_ref[...] = nv

    spec = pl.BlockSpec((tr, cols), lambda i: (i, 0))
    shape = jax.ShapeDtypeStruct((rows, cols), F32)
    return pl.pallas_call(
        body, name=name, grid=(rows // tr,), out_shape=(shape, shape, shape),
        in_specs=[spec] * 4, out_specs=(spec, spec, spec), compiler_params=_params(1),
    )(w, g, m, v)


def _ada_rows(c_ref, cc_ref):
    rows = jnp.concatenate([c_ref[...], jnp.broadcast_to(cc_ref[...], c_ref.shape)], axis=0)
    return rows


def _ada_fwd(c_all, c_ctx, w_sh, b_sh):
    n_layers, _, ws = w_sh.shape

    def body(c_ref, cc_ref, w_ref, b_ref, o_ref):
        rows = _ada_rows(c_ref, cc_ref)
        s = rows * _sigmoid(rows)
        for i in range(n_layers):
            o_ref[i] = jnp.dot(s, w_ref[i], precision=HIGHEST, preferred_element_type=F32) + b_ref[i]

    return pl.pallas_call(
        body, name="ada_fwd", out_shape=jax.ShapeDtypeStruct((n_layers, 16, ws), F32),
        compiler_params=pltpu.CompilerParams(vmem_limit_bytes=VMEM_LIMIT),
    )(c_all, c_ctx, w_sh, b_sh)


def _ada_bwd(c_all, c_ctx, w_sh, dm_full, dm_sh):
    n_layers, d_model, ws = w_sh.shape
    n_dev = dm_full.shape[0]
    cols = dm_full.shape[-1]

    def body(c_ref, cc_ref, w_ref, dmf_ref, dms_ref, gw_ref, gb_ref, ds_ref):
        rows = _ada_rows(c_ref, cc_ref)
        s = rows * _sigmoid(rows)
        ds = jnp.zeros((8, d_model), F32)
        for i in range(n_layers):
            ctx_s = dms_ref[0, i, 1:2, :]
            ctx_f = dmf_ref[0, i, 1:2, :]
            ex_f = dmf_ref[0, i, 0:1, :]
            for k in range(1, n_dev):
                ctx_s = ctx_s + dms_ref[k, i, 1:2, :]
                ctx_f = ctx_f + dmf_ref[k, i, 1:2, :]
                ex_f = ex_f + dmf_ref[k, i, 0:1, :]
            gb_ref[i] = ex_f + ctx_f
            r = jnp.concatenate([dms_ref[k, i, 0:1, :] for k in range(n_dev)] + [ctx_s, jnp.zeros((7, ws), F32)], axis=0)
            gw_ref[i] = lax.dot_general(s, r, (((0,), (0,)), ((), ())), precision=HIGHEST, preferred_element_type=F32)
            ds = ds + lax.dot_general(jnp.broadcast_to(ctx_s, (8, ws)), w_ref[i], (((1,), (1,)), ((), ())),
                                      precision=HIGHEST, preferred_element_type=F32)
        ds_ref[...] = ds

    return pl.pallas_call(
        body, name="ada_bwd",
        out_shape=(jax.ShapeDtypeStruct((n_layers, d_model, ws), F32), jax.ShapeDtypeStruct((n_layers, 1, cols), F32),
                   jax.ShapeDtypeStruct((8, d_model), F32)),
        compiler_params=pltpu.CompilerParams(vmem_limit_bytes=VMEM_LIMIT),
    )(c_all, c_ctx, w_sh, dm_full, dm_sh)


def _cctx_grad(ds_parts, c_ctx):
    def body(p_ref, c_ref, o_ref):
        tot = p_ref[0, 0:1, :]
        for k in range(1, ds_parts.shape[0]):
            tot = tot + p_ref[k, 0:1, :]
        cv = c_ref[...]
        sg = _sigmoid(cv)
        o_ref[...] = tot * (sg * (1.0 + cv * (1.0 - sg)))

    return pl.pallas_call(body, name="cctx_grad", out_shape=jax.ShapeDtypeStruct(c_ctx.shape, F32))(ds_parts, c_ctx)


def _zoh_math(lam_re, lam_im, log_step, b_re, b_im):
    n_state = lam_re.shape[0] // 2
    dt = jnp.exp(jnp.concatenate([jnp.broadcast_to(log_step[r:r + 1], (n_state, log_step.shape[1])) for r in range(2)],
                                 axis=0))
    mag = jnp.exp(lam_re * dt)
    ar = mag * jnp.cos(lam_im * dt)
    ai = mag * jnp.sin(lam_im * dt)
    qr, qi = ar - 1.0, ai
    den = lam_re * lam_re + lam_im * lam_im
    fr = (qr * lam_re + qi * lam_im) / den
    fi = (qi * lam_re - qr * lam_im) / den
    bbr = fr[None] * b_re - fi[None] * b_im
    bbi = fr[None] * b_im + fi[None] * b_re
    return ar, ai, bbr, bbi


def _zoh_fwd(lam_re, lam_im, log_step, b_re, b_im):
    rg, n = lam_re.shape

    def body(lr_ref, li_ref, ls_ref, br_ref, bi_ref, pr_ref, pi_ref, qr_ref, qi_ref, bbr_ref, bbi_ref):
        ar, ai, bbr, bbi = _zoh_math(lr_ref[...], li_ref[...], ls_ref[...], br_ref[...], bi_ref[...])
        bbr_ref[...] = bbr
        bbi_ref[...] = bbi

        def powers(base_r, base_i, r_ref, i_ref):
            pr, pi_ = base_r, base_i
            for k in range(8):
                r_ref[k] = pr
                i_ref[k] = pi_
                pr, pi_ = pr * base_r - pi_ * base_i, pr * base_i + pi_ * base_r

        powers(ar, ai, pr_ref, pi_ref)
        powers(pr_ref[7], pi_ref[7], qr_ref, qi_ref)

    pw = jax.ShapeDtypeStruct((8, rg, n), F32)
    bb = jax.ShapeDtypeStruct(b_re.shape, F32)
    return pl.pallas_call(body, name="zoh_fwd", out_shape=(pw, pw, pw, pw, bb, bb))(lam_re, lam_im, log_step, b_re, b_im)


def _zoh_bwd(lam_re, lam_im, log_step, b_re, b_im, dar, dai, dbbr, dbbi):
    def body(lr_ref, li_ref, ls_ref, br_ref, bi_ref, dar_ref, dai_ref, dbr_ref, dbi_ref, *outs):
        _, vjp = jax.vjp(_zoh_math, lr_ref[...], li_ref[...], ls_ref[...], br_ref[...], bi_ref[...])
        grads = vjp((dar_ref[...], dai_ref[...], dbr_ref[...], dbi_ref[...]))
        for o_ref, gval in zip(outs, grads):
            o_ref[...] = gval

    shapes = tuple(jax.ShapeDtypeStruct(a.shape, F32) for a in (lam_re, lam_im, log_step, b_re, b_im))
    return pl.pallas_call(body, name="zoh_bwd", out_shape=shapes)(lam_re, lam_im, log_step, b_re, b_im, dar, dai, dbbr, dbbi)


def _inproj(xin, sc, sh, w, *, lnaff=None, tb, gb_rows=None, name, hosted=None):
    n_tok, d_model = xin.shape
    n_chunks, _, cw = w.shape
    tb = min(tb, n_tok)
    nq = cw // LANES
    has_ln = lnaff is not None
    n_out = 1 if gb_rows is None else len(gb_rows)

    def body(*refs):
        if has_ln:
            x_ref, g_ref, b_ref, sc_ref, sh_ref, w_ref = refs[:6]
        else:
            x_ref, sc_ref, sh_ref, w_ref = refs[:4]
        p_refs, h_ref = refs[-1 - n_out:-1], refs[-1]

        @pl.when(pl.program_id(1) == 0)
        def _():
            xv = x_ref[...]
            if has_ln:
                xv = xv * g_ref[...] + b_ref[...]
            h_ref[...] = (xv * (1.0 + sc_ref[...]) + sh_ref[...]).astype(BF16)

        acc = _dot(h_ref[...], w_ref[0]).astype(BF16)
        if gb_rows is None:
            p_refs[0][0] = acc
        else:
            for p_ref in p_refs:
                for q in range(nq):
                    p_ref[q] = acc[:, q * LANES:(q + 1) * LANES]

    vec = pl.BlockSpec((1, d_model), lambda i, j: (0, 0))
    in_specs = [pl.BlockSpec((tb, d_model), lambda i, j: (i, 0))] + ([vec, vec] if has_ln else []) + [
        vec, vec, pl.BlockSpec((1, d_model, cw), lambda i, j: (j, 0, 0))]
    if gb_rows is None:
        p_shapes = [jax.ShapeDtypeStruct((n_chunks, n_tok, cw), BF16)]
        p_specs = [pl.BlockSpec((1, tb, cw), lambda i, j: (j, i, 0))]
    else:
        p_shapes, p_specs = [], []
        for total, off in gb_rows:
            assert off % tb == 0
            p_shapes.append(jax.ShapeDtypeStruct((n_chunks * nq, total, LANES), BF16))
            p_specs.append(pl.BlockSpec((nq, tb, LANES), functools.partial(lambda i, j, ob: (j, i + ob, 0), ob=off // tb)))
    args = (xin,) + (tuple(lnaff) if has_ln else ()) + (sc, sh, w)
    outs, extra = _call(
        body, name=name, grid=(n_tok // tb, n_chunks), in_specs=in_specs,
        out_specs=[*p_specs, pl.BlockSpec((tb, d_model), lambda i, j: (i, 0))],
        out_shape=[*p_shapes, jax.ShapeDtypeStruct((n_tok, d_model), BF16)], scratch_shapes=[], args=args, hosted=hosted)
    return (*outs, extra) if hosted is not None else tuple(outs)


def _inproj_seq(xin, sc, sh, w, lnaff, *, tb, seq_rows, name):
    n_tok, d_model = xin.shape
    n_chunks, _, cw = w.shape
    tb = min(tb, n_tok)
    nq = cw // LANES
    n_out = len(seq_rows)
    steps = (n_tok // tb) * n_chunks

    def body(x_ref, g_ref, b_ref, sc_ref, sh_ref, w_ref, *rest):
        p_refs, h_ref, stage, sems = rest[:n_out], rest[n_out], rest[n_out + 1], rest[n_out + 2]
        i, j = pl.program_id(0), pl.program_id(1)
        step = i * n_chunks + j
        slot = step % 2

        def copies(from_slot):
            return [pltpu.make_async_copy(stage.at[from_slot],
                                          p_ref.at[pl.ds(j * nq, nq), pl.ds(off + i * tb, tb), :], sems.at[from_slot, k])
                    for k, (p_ref, (_, off)) in enumerate(zip(p_refs, seq_rows))]

        @pl.when(step >= 2)
        def _():
            for cp in copies(slot):
                cp.wait()

        @pl.when(j == 0)
        def _():
            xv = x_ref[...] * g_ref[...] + b_ref[...]
            h_ref[...] = (xv * (1.0 + sc_ref[...]) + sh_ref[...]).astype(BF16)

        acc = _dot(h_ref[...], w_ref[0]).astype(BF16)
        for q in range(nq):
            stage[slot, q] = acc[:, q * LANES:(q + 1) * LANES]
        for cp in copies(slot):
            cp.start()

        @pl.when(step == steps - 1)
        def _():
            for cp in copies(slot):
                cp.wait()
            if steps > 1:
                for cp in copies(1 - slot):
                    cp.wait()

    vec = pl.BlockSpec((1, d_model), lambda i, j: (0, 0))
    tok = pl.BlockSpec((tb, d_model), lambda i, j: (i, 0))
    return pl.pallas_call(
        body, name=name, grid=(n_tok // tb, n_chunks),
        in_specs=[tok, vec, vec, vec, vec, pl.BlockSpec((1, d_model, cw), lambda i, j: (j, 0, 0))],
        out_specs=(*[pl.BlockSpec(memory_space=pltpu.HBM)] * n_out, tok),
        out_shape=(*[jax.ShapeDtypeStruct((n_chunks * nq, total, LANES), BF16) for total, _ in seq_rows],
                   jax.ShapeDtypeStruct((n_tok, d_model), BF16)),
        scratch_shapes=[pltpu.VMEM((2, nq, tb, LANES), BF16), pltpu.SemaphoreType.DMA((2, n_out))],
        compiler_params=_params(2),
    )(xin, lnaff[0], lnaff[1], sc, sh, w)


def _shifted(u, rows, width, tb):
    col = rows % width
    um = jnp.where(col == 0, 0.0, pltpu.roll(u, 1, 0))
    up = jnp.where(col == width - 1, 0.0, pltpu.roll(u, tb - 1, 0))
    return um, up


def _slab_width(e):
    return min(512, e // 2)


def _convgate(p, x, gt, conv_w, w_out, ln_g, ln_b, *, grid_mode, tb, name):
    _, n_tok, e = p.shape
    d_model = x.shape[1]
    eh = e // 2
    if not grid_mode:
        tb = n_tok
    tb = min(tb, n_tok)
    nb = n_tok // tb
    hb = tb // GRID_W
    cs = _slab_width(e)

    def body(*refs):
        if grid_mode:
            (bg_ref, cg_ref, v_ref, z_ref, cgp_ref, vp_ref, cgn_ref, vn_ref, x_ref, gt_ref, cw_ref, wo_ref, lg_ref,
             lb_ref, xh_ref, rs_ref, g_ref, yc_ref, f_ref) = refs
        else:
            (bg_ref, cg_ref, v_ref, z_ref, x_ref, gt_ref, cw_ref, wo_ref, lg_ref, lb_ref, xh_ref, rs_ref, g_ref,
             yc_ref, f_ref) = refs
        i = pl.program_id(0)
        rows = lax.broadcasted_iota(jnp.int32, (tb, 1), 0)
        for c0 in range(0, e, cs):
            sl = slice(c0, c0 + cs)
            u = cg_ref[0, :, sl].astype(F32) * v_ref[0, :, sl].astype(F32)
            w = cw_ref[:, sl]
            if grid_mode and c0 >= eh:
                hs = slice(c0 - eh, c0 - eh + cs)
                uprev = cgp_ref[0, :, hs].astype(F32) * vp_ref[0, :, hs].astype(F32)
                unext = cgn_ref[0, :, hs].astype(F32) * vn_ref[0, :, hs].astype(F32)
                uprev = jnp.where(i > 0, uprev, 0.0)
                unext = jnp.where(i < nb - 1, unext, 0.0)
                if tb > GRID_W:
                    um = jnp.concatenate([uprev, u[:tb - GRID_W]], axis=0)
                    up = jnp.concatenate([u[GRID_W:], unext], axis=0)
                else:
                    um, up = uprev, unext
            else:
                um, up = _shifted(u, rows, GRID_W if grid_mode else tb, tb)
            yc = um * w[0:1] + u * w[1:2] + up * w[2:3]
            zf = z_ref[0, :, sl].astype(F32)
            gval = bg_ref[0, :, sl].astype(F32) * yc * (zf * _sigmoid(zf))
            yc_ref[:, sl] = yc.astype(BF16)
            g_ref[:, sl] = gval.astype(BF16)
        f = _dot(g_ref[...], wo_ref[...])
        f_ref[...] = f.astype(BF16)
        r = DN_ALPHA * x_ref[...] + gt_ref[...] * f
        rc = r - _rowmean(r)
        rstd = lax.rsqrt(_rowmean(rc * rc) + LN_EPS)
        xh_ref[...] = rc * rstd
        rs_ref[...] = rstd

    def chunk(k):
        return pl.BlockSpec((1, tb, e), lambda i: (k, i, 0))

    n_hrows = n_tok // GRID_W

    def halo_prev(k):
        return pl.BlockSpec((1, GRID_W, eh), lambda i: (k, jnp.maximum(i * hb - 1, 0), 1))

    def halo_next(k):
        return pl.BlockSpec((1, GRID_W, eh), lambda i: (k, jnp.minimum((i + 1) * hb, n_hrows - 1), 1))

    vec = pl.BlockSpec((1, d_model), lambda i: (0, 0))
    tok = pl.BlockSpec((tb, d_model), lambda i: (i, 0))
    wide = pl.BlockSpec((tb, e), lambda i: (i, 0))
    in_specs = [chunk(0), chunk(1), chunk(2), chunk(3)]
    args = [p, p, p, p]
    if grid_mode:
        in_specs += [halo_prev(1), halo_prev(2), halo_next(1), halo_next(2)]
        args += [p, p, p, p]
    in_specs += [tok, vec, pl.BlockSpec((3, e), lambda i: (0, 0)), pl.BlockSpec((e, d_model), lambda i: (0, 0)), vec, vec]
    args += [x, gt, conv_w, w_out, ln_g, ln_b]
    return pl.pallas_call(
        body, name=name, grid=(nb,),
        out_shape=(jax.ShapeDtypeStruct((n_tok, d_model), F32), jax.ShapeDtypeStruct((n_tok, 1), F32),
                   jax.ShapeDtypeStruct((n_tok, e), BF16), jax.ShapeDtypeStruct((n_tok, e), BF16),
                   jax.ShapeDtypeStruct((n_tok, d_model), BF16)),
        in_specs=in_specs, out_specs=(tok, pl.BlockSpec((tb, 1), lambda i: (i, 0)), wide, wide, tok),
        compiler_params=_params(1),
    )(*args)


def _scan_block(buf_ref, tab_ref, cr, ci, *, reverse, tb, sb):
    n_slabs = tb // SUBLANES
    unrolled = n_slabs <= 64

    def slab(s, carry):
        cr, ci = carry
        idx = (n_slabs - 1 - s) if reverse else s
        r0 = idx * SUBLANES if unrolled else pl.multiple_of(idx * SUBLANES, SUBLANES)
        xr = buf_ref[pl.ds(r0, SUBLANES), 0:sb]
        xi = buf_ref[pl.ds(r0, SUBLANES), sb:2 * sb]
        for k, step in enumerate((1, 2, 4)):
            ar = tab_ref[2 * k]
            ai = tab_ref[2 * k + 1]
            shift = (SUBLANES - step) if reverse else step
            rr = pltpu.roll(xr, shift, 0)
            ri = pltpu.roll(xi, shift, 0)
            xr, xi = xr + ar * rr - ai * ri, xi + ar * ri + ai * rr
        pr = tab_ref[6]
        pi_ = tab_ref[7]
        xr, xi = xr + pr * cr - pi_ * ci, xi + pr * ci + pi_ * cr
        buf_ref[pl.ds(r0, SUBLANES), 0:sb] = xr
        buf_ref[pl.ds(r0, SUBLANES), sb:2 * sb] = xi
        last = 0 if reverse else SUBLANES - 1
        return (jnp.broadcast_to(xr[last:last + 1, :], (SUBLANES, sb)),
                jnp.broadcast_to(xi[last:last + 1, :], (SUBLANES, sb)))

    if unrolled:
        carry = (cr, ci)
        for s in range(n_slabs):
            carry = slab(s, carry)
        return carry
    return lax.fori_loop(0, n_slabs, slab, (cr, ci))


CHUNK = SUBLANES


def _group_mask():
    r = lax.broadcasted_iota(jnp.int32, (LANES, LANES), 0)
    c = lax.broadcasted_iota(jnp.int32, (LANES, LANES), 1)
    return r // 16 == c // 16


def _s5_ops(bre, bim, cre, cim, wr, wi, *, reverse, name, hosted=None):
    ngb, _, sb = bre.shape
    n_rows = CHUNK * LANES

    def body(bre_ref, bim_ref, cre_ref, cim_ref, wr_ref, wi_ref, t_ref, bp_ref, cp_ref):
        b_re, b_im, c_re, c_im = bre_ref[0], bim_ref[0], cre_ref[0], cim_ref[0]
        mask = _group_mask()
        er, ei = [], []
        for tau in range(CHUNK + 1):
            w_r, w_i = wr_ref[0, tau:tau + 1, :], wi_ref[0, tau:tau + 1, :]
            er.append(c_re * w_r - c_im * w_i)
            ei.append(c_re * w_i + c_im * w_r)
        kt = []
        for tau in range(CHUNK):
            k = (lax.dot_general(b_re, er[tau], (((1,), (1,)), ((), ())), precision=HIGHEST, preferred_element_type=F32)
                 - lax.dot_general(b_im, ei[tau], (((1,), (1,)), ((), ())), precision=HIGHEST, preferred_element_type=F32))
            kt.append(jnp.where(mask, k, 0.0).astype(BF16))
        zero = jnp.zeros((LANES, LANES), BF16)
        for i in range(CHUNK):
            rows = slice(i * LANES, (i + 1) * LANES)
            for j in range(CHUNK):
                lag = (i - j) if reverse else (j - i)
                t_ref[0, rows, j * LANES:(j + 1) * LANES] = kt[lag] if lag >= 0 else zero
            tau = i if reverse else CHUNK - 1 - i
            w_r, w_i = wr_ref[0, tau:tau + 1, :], wi_ref[0, tau:tau + 1, :]
            bp_ref[0, rows, 0:sb] = (b_re * w_r - b_im * w_i).astype(BF16)
            bp_ref[0, rows, sb:2 * sb] = (b_re * w_i + b_im * w_r).astype(BF16)
            tau = CHUNK - i if reverse else i + 1
            cp_ref[0, rows, 0:sb] = er[tau].astype(BF16)
            cp_ref[0, rows, sb:2 * sb] = (-ei[tau]).astype(BF16)

    mat = pl.BlockSpec((1, LANES, sb), lambda g: (g, 0, 0))
    pw = pl.BlockSpec((1, CHUNK + 1, sb), lambda g: (g, 0, 0))
    outs, extra = _call(
        body, name=name, grid=(ngb,),
        out_shape=[jax.ShapeDtypeStruct((ngb, n_rows, n_rows), BF16), jax.ShapeDtypeStruct((ngb, n_rows, 2 * sb), BF16),
                   jax.ShapeDtypeStruct((ngb, n_rows, 2 * sb), BF16)],
        in_specs=[mat, mat, mat, mat, pw, pw],
        out_specs=[pl.BlockSpec((1, n_rows, n_rows), lambda g: (g, 0, 0)),
                   pl.BlockSpec((1, n_rows, 2 * sb), lambda g: (g, 0, 0)),
                   pl.BlockSpec((1, n_rows, 2 * sb), lambda g: (g, 0, 0))],
        scratch_shapes=[], args=(bre, bim, cre, cim, wr, wi), hosted=hosted)
    return (*outs, extra)


def _s5_ops_bwd(bre, bim, cre, cim, wr, wi, ar, ai, dt, dbp, dcp, da8, *, reverse, name, hosted=None):
    ngb, _, sb = bre.shape
    n_rows = CHUNK * LANES

    def dot_hi(a, b, dims):
        return lax.dot_general(a, b, (dims, ((), ())), precision=HIGHEST, preferred_element_type=F32)

    def body(bre_ref, bim_ref, cre_ref, cim_ref, wr_ref, wi_ref, ar_ref, ai_ref, dt_ref, dbp_ref, dcp_ref, da8_ref,
             dbre_ref, dbim_ref, dcre_ref, dcim_ref, da_ref):
        b_re, b_im, c_re, c_im = bre_ref[0], bim_ref[0], cre_ref[0], cim_ref[0]
        mask = _group_mask()
        w_r = [wr_ref[0, tau:tau + 1, :] for tau in range(CHUNK + 1)]
        w_i = [wi_ref[0, tau:tau + 1, :] for tau in range(CHUNK + 1)]
        der = [jnp.zeros((LANES, sb), F32) for _ in range(CHUNK + 1)]
        dei = [jnp.zeros((LANES, sb), F32) for _ in range(CHUNK + 1)]
        dwr = [jnp.zeros((1, sb), F32) for _ in range(CHUNK + 1)]
        dwi = [jnp.zeros((1, sb), F32) for _ in range(CHUNK + 1)]
        dwr[CHUNK] = da8_ref[0, :, 0:sb]
        dwi[CHUNK] = da8_ref[0, :, sb:2 * sb]
        d_bre = jnp.zeros((LANES, sb), F32)
        d_bim = jnp.zeros((LANES, sb), F32)
        dkt = [jnp.zeros((LANES, LANES), F32) for _ in range(CHUNK)]
        for i in range(CHUNK):
            rows = slice(i * LANES, (i + 1) * LANES)
            for j in range(CHUNK):
                lag = (i - j) if reverse else (j - i)
                if lag >= 0:
                    dkt[lag] = dkt[lag] + dt_ref[0, rows, j * LANES:(j + 1) * LANES]
            tau = i if reverse else CHUNK - 1 - i
            g_r, g_i = dbp_ref[0, rows, 0:sb], dbp_ref[0, rows, sb:2 * sb]
            d_bre = d_bre + g_r * w_r[tau] + g_i * w_i[tau]
            d_bim = d_bim - g_r * w_i[tau] + g_i * w_r[tau]
            dwr[tau] = dwr[tau] + _colsum(g_r * b_re + g_i * b_im)
            dwi[tau] = dwi[tau] + _colsum(g_i * b_re - g_r * b_im)
            tau = CHUNK - i if reverse else i + 1
            der[tau] = der[tau] + dcp_ref[0, rows, 0:sb]
            dei[tau] = dei[tau] - dcp_ref[0, rows, sb:2 * sb]
        d_cre = jnp.zeros((LANES, sb), F32)
        d_cim = jnp.zeros((LANES, sb), F32)
        for tau in range(CHUNK + 1):
            if tau < CHUNK:
                e_r = c_re * w_r[tau] - c_im * w_i[tau]
                e_i = c_re * w_i[tau] + c_im * w_r[tau]
                dk = jnp.where(mask, dkt[tau], 0.0)
                d_bre = d_bre + dot_hi(dk, e_r, ((1,), (0,)))
                d_bim = d_bim - dot_hi(dk, e_i, ((1,), (0,)))
                der[tau] = der[tau] + dot_hi(dk, b_re, ((0,), (0,)))
                dei[tau] = dei[tau] - dot_hi(dk, b_im, ((0,), (0,)))
            d_cre = d_cre + der[tau] * w_r[tau] + dei[tau] * w_i[tau]
            d_cim = d_cim - der[tau] * w_i[tau] + dei[tau] * w_r[tau]
            dwr[tau] = dwr[tau] + _colsum(der[tau] * c_re + dei[tau] * c_im)
            dwi[tau] = dwi[tau] + _colsum(dei[tau] * c_re - der[tau] * c_im)
        a_r, a_i = ar_ref[0], ai_ref[0]
        d_ar = jnp.zeros((1, sb), F32)
        d_ai = jnp.zeros((1, sb), F32)
        for tau in range(CHUNK, 0, -1):
            d_ar = d_ar + dwr[tau] * w_r[tau - 1] + dwi[tau] * w_i[tau - 1]
            d_ai = d_ai - dwr[tau] * w_i[tau - 1] + dwi[tau] * w_r[tau - 1]
            dwr[tau - 1], dwi[tau - 1] = (dwr[tau - 1] + dwr[tau] * a_r + dwi[tau] * a_i,
                                          dwi[tau - 1] - dwr[tau] * a_i + dwi[tau] * a_r)
        dbre_ref[0] = d_bre
        dbim_ref[0] = d_bim
        dcre_ref[0] = d_cre
        dcim_ref[0] = d_cim
        da_ref[0, :, 0:sb] = d_ar
        da_ref[0, :, sb:2 * sb] = d_ai

    mat = pl.BlockSpec((1, LANES, sb), lambda g: (g, 0, 0))
    pw = pl.BlockSpec((1, CHUNK + 1, sb), lambda g: (g, 0, 0))
    one = pl.BlockSpec((1, 1, sb), lambda g: (g, 0, 0))
    two = pl.BlockSpec((1, 1, 2 * sb), lambda g: (g, 0, 0))
    big = pl.BlockSpec((1, n_rows, n_rows), lambda g: (g, 0, 0))
    big2 = pl.BlockSpec((1, n_rows, 2 * sb), lambda g: (g, 0, 0))
    mshape = jax.ShapeDtypeStruct((ngb, LANES, sb), F32)
    outs, extra = _call(
        body, name=name, grid=(ngb,),
        out_shape=[mshape, mshape, mshape, mshape, jax.ShapeDtypeStruct((ngb, 1, 2 * sb), F32)],
        in_specs=[mat, mat, mat, mat, pw, pw, one, one, big, big2, big2, two],
        out_specs=[mat, mat, mat, mat, two],
        scratch_shapes=[], args=(bre, bim, cre, cim, wr, wi, ar, ai, dt, dbp, dcp, da8), hosted=hosted)
    return (*outs, extra)


MXU_TILE = 256


def _causal_span(tile, n_tiles, reverse, of_output):
    upto, onward = slice(0, (tile + 1) * MXU_TILE), slice(tile * MXU_TILE, n_tiles * MXU_TILE)
    return (onward if reverse else upto) if of_output else (upto if reverse else onward)


def _apply_causal(uv, t_ref, reverse):
    n_tiles = uv.shape[1] // MXU_TILE
    cols = []
    for tj in range(n_tiles):
        span = _causal_span(tj, n_tiles, reverse, True)
        cols.append(_dot(uv[:, span], t_ref[0, span, tj * MXU_TILE:(tj + 1) * MXU_TILE]))
    return jnp.concatenate(cols, axis=1)


def _apply_causal_t(dyv, t_ref, reverse):
    n_tiles = dyv.shape[1] // MXU_TILE
    cols = []
    for ti in range(n_tiles):
        span = _causal_span(ti, n_tiles, reverse, False)
        cols.append(_dot_nt(dyv[:, span], t_ref[0, ti * MXU_TILE:(ti + 1) * MXU_TILE, span]))
    return jnp.concatenate(cols, axis=1)


def _shift_rows(xv, edge, rows, n_rows, down):
    if down:
        return jnp.where(rows == 0, edge, pltpu.roll(xv, 1, 0))
    return jnp.where(rows == n_rows - 1, edge, pltpu.roll(xv, n_rows - 1, 0))


def _rows_of_tokens(tok_ref, conv_scr, rb):
    conv_scr[...] = tok_ref[0].astype(F32)
    return jnp.concatenate([conv_scr[pl.ds(j, rb, stride=CHUNK), :] for j in range(CHUNK)], axis=1).astype(BF16)


def _tokens_of_rows(val, tok_ref, conv_scr, rb):
    for j in range(CHUNK):
        conv_scr[pl.ds(j, rb, stride=CHUNK), :] = val[:, j * LANES:(j + 1) * LANES]
    tok_ref[0] = conv_scr[...].astype(BF16)


def _s5_row_block(n_seq, target=416):
    n_rows = n_seq // CHUNK
    best = 16
    for rb in range(16, min(target, n_rows) + 1, 16):
        if n_rows % rb == 0:
            best = rb
    assert n_rows % best == 0
    return best


def _s5_fwd(useq, t_op, bp, cp, tab, *, reverse, name):
    ngb, n_seq, _ = useq.shape
    sb = bp.shape[2] // 2
    width = CHUNK * LANES
    rb = _s5_row_block(n_seq)
    tbk = rb * CHUNK
    steps = n_seq // tbk

    def blk(i):
        return (steps - 1 - i) if reverse else i

    def body(u_ref, t_ref, b_ref, c_ref, tab_ref, y_ref, hp_ref, h_scr, conv_scr, carry_scr):
        i = pl.program_id(1)

        @pl.when(i == 0)
        def _():
            carry_scr[...] = jnp.zeros_like(carry_scr)

        enter = carry_scr[0:1, :]
        uv = _rows_of_tokens(u_ref, conv_scr, rb)
        h_scr[...] = _dot(uv, b_ref[0])
        cr, ci = _scan_block(h_scr, tab_ref.at[0], carry_scr[:, 0:sb], carry_scr[:, sb:2 * sb],
                             reverse=reverse, tb=rb, sb=sb)
        carry_scr[:, 0:sb] = cr
        carry_scr[:, sb:2 * sb] = ci
        rows = lax.broadcasted_iota(jnp.int32, (rb, 1), 0)
        hprev = _shift_rows(h_scr[...], enter, rows, rb, down=not reverse)
        hp_ref[0] = hprev
        _tokens_of_rows(_apply_causal(uv, t_ref, reverse) + _dot_nt(hprev.astype(BF16), c_ref[0]), y_ref, conv_scr, rb)

    op = pl.BlockSpec((1, width, width), lambda g, i: (g, 0, 0))
    op2 = pl.BlockSpec((1, width, 2 * sb), lambda g, i: (g, 0, 0))
    tok = pl.BlockSpec((1, tbk, LANES), lambda g, i: (g, blk(i), 0))
    return pl.pallas_call(
        body, name=name, grid=(ngb, steps),
        out_shape=(jax.ShapeDtypeStruct((ngb, n_seq, LANES), BF16),
                   jax.ShapeDtypeStruct((ngb, n_seq // CHUNK, 2 * sb), F32)),
        in_specs=[tok, op, op2, op2, pl.BlockSpec((1, 8, SUBLANES, sb), lambda g, i: (g, 0, 0, 0))],
        out_specs=(tok, pl.BlockSpec((1, rb, 2 * sb), lambda g, i: (g, blk(i), 0))),
        scratch_shapes=[pltpu.VMEM((rb, 2 * sb), F32), pltpu.VMEM((tbk, LANES), F32),
                        pltpu.VMEM((SUBLANES, 2 * sb), F32)],
        compiler_params=_params(2),
    )(useq, t_op, bp, cp, tab)


def _s5_bwd(useq, dy, hprev, t_op, bp, cp, tab_adj, *, reverse, name, hosted=None):
    ngb, n_seq, _ = useq.shape
    sb = bp.shape[2] // 2
    width = CHUNK * LANES
    rb = _s5_row_block(n_seq)
    tbk = rb * CHUNK
    steps = n_seq // tbk

    def blk(i):
        return i if reverse else steps - 1 - i

    def body(u_ref, dy_ref, hp_ref, t_ref, b_ref, c_ref, taba_ref, du_ref, dt_ref, db_ref, dc_ref, da_ref,
             lam_scr, conv_scr, lcarry_scr, gedge_scr, da_scr):
        i = pl.program_id(1)
        first = i == 0

        _zero_first(first, lcarry_scr, gedge_scr, da_scr, dt_ref, db_ref, dc_ref)
        rows = lax.broadcasted_iota(jnp.int32, (rb, 1), 0)
        uv = _rows_of_tokens(u_ref, conv_scr, rb)
        dyv = _rows_of_tokens(dy_ref, conv_scr, rb)
        gy = _dot(dyv, c_ref[0])
        edge = gy[rb - 1:rb, :] if reverse else gy[0:1, :]
        lam_scr[...] = _shift_rows(gy, gedge_scr[...], rows, rb, down=reverse)
        gedge_scr[...] = edge
        lr, li = _scan_block(lam_scr, taba_ref.at[0], lcarry_scr[:, 0:sb], lcarry_scr[:, sb:2 * sb],
                             reverse=not reverse, tb=rb, sb=sb)
        lcarry_scr[:, 0:sb] = lr
        lcarry_scr[:, sb:2 * sb] = li

        lam = lam_scr[...]
        lam_bf = lam.astype(BF16)
        _tokens_of_rows(_apply_causal_t(dyv, t_ref, reverse) + _dot_nt(lam_bf, b_ref[0]), du_ref, conv_scr, rb)
        for tj in range(width // MXU_TILE):
            span, cols = _causal_span(tj, width // MXU_TILE, reverse, True), slice(tj * MXU_TILE, (tj + 1) * MXU_TILE)
            _acc(dt_ref.at[0, span, cols], _dot_tn(uv[:, span], dyv[:, cols]))
        _acc(db_ref.at[0], _dot_tn(uv, lam_bf))
        _acc(dc_ref.at[0], _dot_tn(dyv, hp_ref[0].astype(BF16)))
        lam_r, lam_i = lam[:, 0:sb], lam[:, sb:2 * sb]
        hp_r, hp_i = hp_ref[0, :, 0:sb], hp_ref[0, :, sb:2 * sb]
        da_scr[:, 0:sb] += _colsum(lam_r * hp_r + lam_i * hp_i)
        da_scr[:, sb:2 * sb] += _colsum(lam_i * hp_r - lam_r * hp_i)

        @pl.when(i == steps - 1)
        def _():
            da_ref[0] = da_scr[...]

    op = pl.BlockSpec((1, width, width), lambda g, i: (g, 0, 0))
    op2 = pl.BlockSpec((1, width, 2 * sb), lambda g, i: (g, 0, 0))
    tabs = pl.BlockSpec((1, 8, SUBLANES, sb), lambda g, i: (g, 0, 0, 0))
    tok = pl.BlockSpec((1, tbk, LANES), lambda g, i: (g, blk(i), 0))
    outs, extra = _call(
        body, name=name, grid=(ngb, steps),
        out_shape=[jax.ShapeDtypeStruct((ngb, n_seq, LANES), BF16),
                   jax.ShapeDtypeStruct((ngb, width, width), F32),
                   jax.ShapeDtypeStruct((ngb, width, 2 * sb), F32),
                   jax.ShapeDtypeStruct((ngb, width, 2 * sb), F32),
                   jax.ShapeDtypeStruct((ngb, 1, 2 * sb), F32)],
        in_specs=[tok, tok, pl.BlockSpec((1, rb, 2 * sb), lambda g, i: (g, blk(i), 0)), op, op2, op2, tabs],
        out_specs=[tok, op, op2, op2, pl.BlockSpec((1, 1, 2 * sb), lambda g, i: (g, 0, 0))],
        scratch_shapes=[pltpu.VMEM((rb, 2 * sb), F32), pltpu.VMEM((tbk, LANES), F32),
                        pltpu.VMEM((SUBLANES, 2 * sb), F32), pltpu.VMEM((1, 2 * sb), F32), pltpu.VMEM((1, 2 * sb), F32)],
        args=(useq, dy, hprev, t_op, bp, cp, tab_adj), hosted=hosted)
    return (*outs, extra)


def _glu_loss(useq, yf, yb, z, xhat0, ln0, gt, d_vec, w_glu, b_glu, w_out, ln1, target, *, offs, dy_rows, tb, name):
    ngb = useq.shape[0]
    n_tok, d_model = xhat0.shape
    e = ngb * LANES
    tb = min(tb, n_tok)
    assert all(off % tb == 0 for off in offs) and all(off % tb == 0 for _, off in dy_rows)
    nz = z.shape[0]

    def body(u_ref, yf_ref, yb_ref, z_ref, xh0_ref, g0_ref, b0_ref, gt_ref, d_ref, wg_ref, bg_ref, wo_ref, g1_ref,
             b1_ref, t_ref, loss_ref, dxr_ref, do_ref, gz_ref, gg_ref, dq_ref, dz_ref, dyf_ref, dyb_ref, dg1_ref, db1_ref,
             dgt_ref, dbg_ref, dd_ref, loss_scr, yl_scr, th_scr, s_scr, dg_scr):
        i = pl.program_id(0)
        _zero_first(i == 0, loss_scr, dg1_ref, db1_ref, dgt_ref, dbg_ref, dd_ref)
        zw = e // nz
        cs = min(512, zw)

        def z_slab(c0):
            return z_ref[c0 // zw, :, c0 % zw:c0 % zw + cs].astype(F32)

        for q in range(ngb):
            sl = slice(q * LANES, (q + 1) * LANES)
            yl = d_ref[:, sl] * u_ref[q].astype(F32) + yf_ref[q].astype(F32) + yb_ref[q].astype(F32)
            th = jnp.tanh(GELU_K * (yl + GELU_C * yl * yl * yl))
            yl_scr[:, sl] = yl
            th_scr[:, sl] = th
            gg_ref[:, sl] = (0.5 * yl * (1.0 + th)).astype(BF16)
        g_all = gg_ref[...]
        for c0 in range(0, e, cs):
            sl = slice(c0, c0 + cs)
            s = _sigmoid(_dot(g_all, wg_ref[:, sl]) + bg_ref[:, sl])
            s_scr[:, sl] = s
            zf = z_slab(c0)
            g2 = 0.5 * yl_scr[:, sl] * (1.0 + th_scr[:, sl]) * s
            gz_ref[:, sl] = (g2 * (zf * _sigmoid(zf))).astype(BF16)
        o = _dot(gz_ref[...], wo_ref[...])
        x1 = xh0_ref[...] * g0_ref[...] + b0_ref[...]
        r = DN_ALPHA * x1 + gt_ref[...] * o
        rc = r - _rowmean(r)
        rstd = lax.rsqrt(_rowmean(rc * rc) + LN_EPS)
        xh = rc * rstd
        err = xh * g1_ref[...] + b1_ref[...] - t_ref[...]
        _acc(loss_scr, _colsum(err * err))
        dy = err * (1.0 / d_model)
        _acc(dg1_ref, _colsum(dy * xh))
        _acc(db1_ref, _colsum(dy))
        dxh = dy * g1_ref[...]
        dr = rstd * (dxh - _rowmean(dxh) - xh * _rowmean(dxh * xh))
        dxr_ref[...] = DN_ALPHA * dr
        _acc(dgt_ref, _colsum(dr * o))
        do_bf = (dr * gt_ref[...]).astype(BF16)
        do_ref[...] = do_bf
        for c0 in range(0, e, cs):
            sl = slice(c0, c0 + cs)
            dgz = _dot_nt(do_bf, wo_ref[sl, :])
            zf = z_slab(c0)
            sz = _sigmoid(zf)
            g = 0.5 * yl_scr[:, sl] * (1.0 + th_scr[:, sl])
            s = s_scr[:, sl]
            dg2 = dgz * (zf * sz)
            dz_ref[:, sl] = (dgz * (g * s) * (sz * (1.0 + zf * (1.0 - sz)))).astype(BF16)
            dq = dg2 * g * s * (1.0 - s)
            _acc(dbg_ref.at[:, sl], _colsum(dq))
            dq_ref[:, sl] = dq.astype(BF16)
            dg_scr[:, sl] = dg2 * s
        dq_all = dq_ref[...]
        for c0 in range(0, e, cs):
            dg_blk = dg_scr[:, c0:c0 + cs] + _dot_nt(dq_all, wg_ref[c0:c0 + cs, :])
            for q in range(c0 // LANES, (c0 + cs) // LANES):
                sl = slice(q * LANES, (q + 1) * LANES)
                yl = yl_scr[:, sl]
                th = th_scr[:, sl]
                dgelu = 0.5 * (1.0 + th) + 0.5 * yl * (1.0 - th * th) * (GELU_K * (1.0 + 3.0 * GELU_C * yl * yl))
                dyl = dg_blk[:, q * LANES - c0:(q + 1) * LANES - c0] * dgelu
                _acc(dd_ref.at[:, sl], _colsum(dyl * u_ref[q].astype(F32)))
                dyf_ref[q] = dyl.astype(BF16)
                dyb_ref[q] = dyl.astype(BF16)

        @pl.when(i == pl.num_programs(0) - 1)
        def _():
            loss_ref[...] = (0.5 / d_model) * jnp.sum(loss_scr[...], axis=1, keepdims=True)

    vec = pl.BlockSpec((1, d_model), lambda i: (0, 0))
    evec = pl.BlockSpec((1, e), lambda i: (0, 0))
    tok = pl.BlockSpec((tb, d_model), lambda i: (i, 0))
    wide = pl.BlockSpec((tb, e), lambda i: (i, 0))
    def gblk(off):
        return pl.BlockSpec((ngb, tb, LANES), functools.partial(lambda i, ob: (0, i + ob, 0), ob=off // tb))

    once = dict(pipeline_mode=pl.Buffered(1))
    tok_f = jax.ShapeDtypeStruct((n_tok, d_model), F32)
    tok_b = jax.ShapeDtypeStruct((n_tok, d_model), BF16)
    wide_b = jax.ShapeDtypeStruct((n_tok, e), BF16)
    vec_f = jax.ShapeDtypeStruct((1, d_model), F32)
    evec_f = jax.ShapeDtypeStruct((1, e), F32)
    return pl.pallas_call(
        body, name=name, grid=(n_tok // tb,),
        out_shape=(jax.ShapeDtypeStruct((1, 1), F32), tok_f, tok_b, wide_b, wide_b, wide_b, wide_b,
                   *[jax.ShapeDtypeStruct((ngb, total, LANES), BF16) for total, _ in dy_rows],
                   vec_f, vec_f, vec_f, evec_f, evec_f),
        in_specs=[gblk(offs[0]), gblk(offs[1]), gblk(offs[2]),
                  pl.BlockSpec((nz, tb, e // nz), lambda i: (0, i, 0)), tok, vec, vec, vec, evec,
                  pl.BlockSpec((e, e), lambda i: (0, 0), **once), evec,
                  pl.BlockSpec((e, d_model), lambda i: (0, 0), **once), vec, vec, tok],
        out_specs=(pl.BlockSpec((1, 1), lambda i: (0, 0)), tok, tok, wide, wide, wide, wide,
                   *[gblk(off) for _, off in dy_rows], vec, vec, vec, evec, evec),
        scratch_shapes=[pltpu.VMEM((1, d_model), F32)] + [pltpu.VMEM((tb, e), F32)] * 4,
        compiler_params=_params(1),
    )(useq, yf, yb, z, xhat0, ln0[0], ln0[1], gt, d_vec, w_glu, b_glu, w_out, ln1[0], ln1[1], target)


def _ssm_inbwd(duf, dub, w, xhat, rstd, ln, sc, gt_prev, f_prev, *, lat, row_f, row_b, tb, name):
    ngb = duf.shape[0]
    e = ngb * LANES
    n_tok, d_model = xhat.shape
    tb = min(tb, n_tok)
    obf, obb = row_f // tb, row_b // tb
    has_lat = lat is not None
    n_w = w.shape[0] if has_lat else w.shape[0] // 2

    def body(*refs):
        if has_lat:
            (duf_ref, dub_ref, dyl_ref, dz_ref, d_ref, dxr_ref, w_ref, xh_ref, rs_ref, g_ref, b_ref, sc_ref, gt_ref,
             f_ref, dp_ref, dr_ref, df_ref, dsc_ref, dsh_ref, dg_ref, db_ref, dgt_ref) = refs
        else:
            (duf_ref, dub_ref, w_ref, xh_ref, rs_ref, g_ref, b_ref, sc_ref, gt_ref, f_ref, dp_ref, dr_ref, df_ref,
             dsc_ref, dsh_ref, dg_ref, db_ref, dgt_ref) = refs
        _zero_first(pl.program_id(0) == 0, dsc_ref, dsh_ref, dg_ref, db_ref, dgt_ref)
        du = (jnp.concatenate([duf_ref[q] for q in range(ngb)], axis=1).astype(F32)
              + jnp.concatenate([dub_ref[q] for q in range(ngb)], axis=1).astype(F32))
        if has_lat:
            du = du + d_ref[...] * jnp.concatenate([dyl_ref[q] for q in range(ngb)], axis=1).astype(F32)
            dp_ref[:, e:2 * e] = dz_ref[...]
        else:
            dp_ref[:, e:2 * e] = jnp.zeros((tb, e), BF16)
        dp_ref[:, 0:e] = du.astype(BF16)
        dh = jnp.zeros((tb, d_model), F32)
        for j in range(n_w):
            dh = dh + _dot(dp_ref[:, j * d_model:(j + 1) * d_model], w_ref[j])
        xh = xh_ref[...]
        x1 = xh * g_ref[...] + b_ref[...]
        dx1 = dh * (1.0 + sc_ref[...])
        if has_lat:
            dx1 = dx1 + dxr_ref[...]
        _acc(dsc_ref, _colsum(dh * x1))
        _acc(dsh_ref, _colsum(dh))
        _acc(dg_ref, _colsum(dx1 * xh))
        _acc(db_ref, _colsum(dx1))
        dxh = dx1 * g_ref[...]
        dr = rs_ref[...] * (dxh - _rowmean(dxh) - xh * _rowmean(dxh * xh))
        dr_ref[...] = dr
        df_ref[...] = (dr * gt_ref[...]).astype(BF16)
        _acc(dgt_ref, _colsum(dr * f_ref[...].astype(F32)))

    vec = pl.BlockSpec((1, d_model), lambda i: (0, 0))
    tok = pl.BlockSpec((tb, d_model), lambda i: (i, 0))
    gblk = pl.BlockSpec((ngb, tb, LANES), lambda i: (0, i, 0))
    in_specs = [pl.BlockSpec((ngb, tb, LANES), lambda i: (0, i + obf, 0)),
                pl.BlockSpec((ngb, tb, LANES), lambda i: (0, i + obb, 0))]
    args = [duf, dub]
    if has_lat:
        in_specs += [gblk, pl.BlockSpec((tb, e), lambda i: (i, 0)), pl.BlockSpec((1, e), lambda i: (0, 0)), tok]
        args += list(lat)
    in_specs += [pl.BlockSpec(w.shape, lambda i: (0, 0, 0)), tok, pl.BlockSpec((tb, 1), lambda i: (i, 0)), vec, vec, vec,
                 vec, tok]
    args += [w, xhat, rstd, ln[0], ln[1], sc, gt_prev, f_prev]
    vec_f = jax.ShapeDtypeStruct((1, d_model), F32)
    return pl.pallas_call(
        body, name=name, grid=(n_tok // tb,),
        out_shape=(jax.ShapeDtypeStruct((n_tok, 2 * e), BF16), jax.ShapeDtypeStruct((n_tok, d_model), F32),
                   jax.ShapeDtypeStruct((n_tok, d_model), BF16), vec_f, vec_f, vec_f, vec_f, vec_f),
        in_specs=in_specs,
        out_specs=(pl.BlockSpec((tb, 2 * e), lambda i: (i, 0)), tok, tok, vec, vec, vec, vec, vec),
        compiler_params=_params(1),
    )(*args)


def _conv_bwd_a(df, w_out_t, p, yc, *, tb, name):
    _, n_tok, e = p.shape
    d_model = df.shape[1]
    tb = min(tb, n_tok)
    cs = _slab_width(e)

    def body(df_ref, wo_ref, bg_ref, z_ref, yc_ref, dbg_ref, dz_ref, dyc_ref):
        dfv = df_ref[...]
        for c0 in range(0, e, cs):
            sl = slice(c0, c0 + cs)
            dgv = _dot(dfv, wo_ref[:, sl])
            zf = z_ref[0, :, sl].astype(F32)
            sz = _sigmoid(zf)
            silu_z = zf * sz
            bg = bg_ref[0, :, sl].astype(F32)
            yc = yc_ref[:, sl].astype(F32)
            dbg_ref[:, sl] = (dgv * yc * silu_z).astype(BF16)
            dyc_ref[:, sl] = (dgv * bg * silu_z).astype(BF16)
            dz_ref[:, sl] = (dgv * bg * yc * (sz * (1.0 + zf * (1.0 - sz)))).astype(BF16)

    wide = pl.BlockSpec((tb, e), lambda i: (i, 0))
    shape = jax.ShapeDtypeStruct((n_tok, e), BF16)
    return pl.pallas_call(
        body, name=name, grid=(n_tok // tb,), out_shape=(shape, shape, shape),
        in_specs=[pl.BlockSpec((tb, d_model), lambda i: (i, 0)), pl.BlockSpec((d_model, e), lambda i: (0, 0)),
                  pl.BlockSpec((1, tb, e), lambda i: (0, i, 0)), pl.BlockSpec((1, tb, e), lambda i: (3, i, 0)), wide],
        out_specs=(wide, wide, wide), compiler_params=_params(1),
    )(df, w_out_t, p, p, yc)


def _conv_bwd_b(dyc, p, dbg, dz, conv_w, *, grid_mode, tb, name, hosted=None):
    _, n_tok, e = p.shape
    eh = e // 2
    if not grid_mode:
        tb = n_tok
    tb = min(tb, n_tok)
    nb = n_tok // tb
    hb = tb // GRID_W
    cs = _slab_width(e)

    def body(*refs):
        if grid_mode:
            dyc_ref, dycp_ref, dycn_ref, cg_ref, v_ref, dbg_ref, dz_ref, cw_ref, dp_ref, dcw_ref = refs
        else:
            dyc_ref, cg_ref, v_ref, dbg_ref, dz_ref, cw_ref, dp_ref, dcw_ref = refs
        i = pl.program_id(0)
        _zero_first(i == 0, dcw_ref)
        rows = lax.broadcasted_iota(jnp.int32, (tb, 1), 0)
        dp_ref[0] = dbg_ref[...]
        dp_ref[3] = dz_ref[...]
        for c0 in range(0, e, cs):
            sl = slice(c0, c0 + cs)
            dyc = dyc_ref[:, sl].astype(F32)
            w = cw_ref[:, sl]
            if grid_mode and c0 >= eh:
                hs = slice(c0 - eh, c0 - eh + cs)
                dprev = jnp.where(i > 0, dycp_ref[:, hs].astype(F32), 0.0)
                dnext = jnp.where(i < nb - 1, dycn_ref[:, hs].astype(F32), 0.0)
                if tb > GRID_W:
                    dm = jnp.concatenate([dprev, dyc[:tb - GRID_W]], axis=0)
                    dpl = jnp.concatenate([dyc[GRID_W:], dnext], axis=0)
                else:
                    dm, dpl = dprev, dnext
            else:
                dm, dpl = _shifted(dyc, rows, GRID_W if grid_mode else tb, tb)
            cg = cg_ref[0, :, sl].astype(F32)
            v = v_ref[0, :, sl].astype(F32)
            u = cg * v
            du = w[0:1] * dpl + w[1:2] * dyc + w[2:3] * dm
            dp_ref[1, :, sl] = (du * v).astype(BF16)
            dp_ref[2, :, sl] = (du * cg).astype(BF16)
            _acc(dcw_ref.at[:, sl], jnp.concatenate([_colsum(u * dpl), _colsum(u * dyc), _colsum(u * dm)], axis=0))

    n_hrows = n_tok // GRID_W
    wide = pl.BlockSpec((tb, e), lambda i: (i, 0))
    in_specs = [wide]
    args = [dyc]
    if grid_mode:
        in_specs += [pl.BlockSpec((GRID_W, eh), lambda i: (jnp.maximum(i * hb - 1, 0), 1)),
                     pl.BlockSpec((GRID_W, eh), lambda i: (jnp.minimum((i + 1) * hb, n_hrows - 1), 1))]
        args += [dyc, dyc]
    in_specs += [pl.BlockSpec((1, tb, e), lambda i: (1, i, 0)), pl.BlockSpec((1, tb, e), lambda i: (2, i, 0)), wide, wide,
                 pl.BlockSpec((3, e), lambda i: (0, 0))]
    args += [p, p, dbg, dz, conv_w]
    outs, extra = _call(
        body, name=name, grid=(nb,),
        out_shape=[jax.ShapeDtypeStruct((4, n_tok, e), BF16), jax.ShapeDtypeStruct((3, e), F32)],
        in_specs=in_specs,
        out_specs=[pl.BlockSpec((4, tb, e), lambda i: (0, i, 0)), pl.BlockSpec((3, e), lambda i: (0, 0))],
        scratch_shapes=[], args=args, hosted=hosted)
    return (*outs, extra)


def _conv_inbwd(dp, w, dr, x, sc, *, tb, name, hosted=None):
    n_chunks, n_tok, e = dp.shape
    d_model = x.shape[1]
    tb = min(tb, n_tok)

    def body(dp_ref, w_ref, dr_ref, x_ref, sc_ref, gx_ref, dsc_ref, dsh_ref):
        _zero_first(pl.program_id(0) == 0, dsc_ref, dsh_ref)
        dh = _dot(dp_ref[0], w_ref[0])
        for k in range(1, n_chunks):
            dh = dh + _dot(dp_ref[k], w_ref[k])
        gx_ref[...] = DN_ALPHA * dr_ref[...] + dh * (1.0 + sc_ref[...])
        _acc(dsc_ref, _colsum(dh * x_ref[...]))
        _acc(dsh_ref, _colsum(dh))

    vec = pl.BlockSpec((1, d_model), lambda i: (0, 0))
    tok = pl.BlockSpec((tb, d_model), lambda i: (i, 0))
    vec_f = jax.ShapeDtypeStruct((1, d_model), F32)
    outs, extra = _call(
        body, name=name, grid=(n_tok // tb,),
        out_shape=[jax.ShapeDtypeStruct((n_tok, d_model), F32), vec_f, vec_f],
        in_specs=[pl.BlockSpec((n_chunks, tb, e), lambda i: (0, i, 0)),
                  pl.BlockSpec((n_chunks, e, d_model), lambda i: (0, 0, 0), pipeline_mode=pl.Buffered(1)),
                  tok, tok, vec],
        out_specs=[tok, vec, vec],
        scratch_shapes=[], args=(dp, w, dr, x, sc), hosted=hosted)
    return (*outs, extra)


def _wgrad(a, b, *, n_chunks, tm, tl, init=None, name):
    n_tok, m = a.shape
    tl = min(tl, n_tok)
    chunked = b.ndim == 3
    cw = b.shape[2] if chunked else b.shape[1] // n_chunks
    has_init = init is not None

    def body(*refs):
        if has_init:
            a_ref, b_ref, init_ref, o_ref = refs
        else:
            a_ref, b_ref, o_ref = refs
        @pl.when(pl.program_id(2) == 0)
        def _():
            o_ref[0] = init_ref[0] if has_init else jnp.zeros_like(o_ref[0])

        o_ref[0] += _dot_tn(a_ref[...], b_ref[0] if chunked else b_ref[...])

    o_spec = pl.BlockSpec((1, tm, cw), lambda jm, jc, l: (jc, jm, 0))
    b_spec = (pl.BlockSpec((1, tl, cw), lambda jm, jc, l: (jc, l, 0)) if chunked
              else pl.BlockSpec((tl, cw), lambda jm, jc, l: (l, jc)))
    init_spec = pl.BlockSpec((1, tm, cw), lambda jm, jc, l: (jc, jm, 0), pipeline_mode=pl.Buffered(1))
    in_specs = [pl.BlockSpec((tl, tm), lambda jm, jc, l: (l, jm)), b_spec] + ([init_spec] if has_init else [])
    args = (a, b) + ((init,) if has_init else ())
    return pl.pallas_call(
        body, name=name, grid=(m // tm, n_chunks, n_tok // tl),
        out_shape=jax.ShapeDtypeStruct((n_chunks, m, cw), F32),
        in_specs=in_specs, out_specs=o_spec, compiler_params=_params(3),
    )(*args)


def _block_diag(t, ngb):
    g, p, n = t.shape
    gpb = g // ngb
    eye = jnp.eye(gpb, dtype=t.dtype)
    return jnp.einsum("bgpn,gh->bgphn", t.reshape(ngb, gpb, p, n), eye).reshape(ngb, gpb * p, gpb * n)


def _block_diag_t(mat, g, p, n):
    ngb = mat.shape[0]
    gpb = g // ngb
    eye = jnp.eye(gpb, dtype=mat.dtype)
    return jnp.einsum("bgphn,gh->bgpn", mat.reshape(ngb, gpb, p, gpb, n), eye).reshape(g, p, n)


def _scan_tables(pw_r, pw_i, ngb, reverse):
    _, g, n = pw_r.shape
    sb = g * n // ngb
    rows = jnp.arange(SUBLANES)
    kinds = []
    for step in (1, 2, 4):
        mask = ((rows < SUBLANES - step) if reverse else (rows >= step)).astype(F32)
        for part in (pw_r[step - 1], pw_i[step - 1]):
            kinds.append(part.reshape(ngb, 1, sb) * mask[None, :, None])
    for part in (pw_r, pw_i):
        pw = part[::-1] if reverse else part
        kinds.append(jnp.transpose(pw.reshape(SUBLANES, ngb, sb), (1, 0, 2)))
    return jnp.stack(kinds, axis=1)


def _flat(parts):
    return jnp.concatenate([p.reshape(-1) for p in parts])


def _unflat(vec, shapes):
    out, off = [], 0
    for s in shapes:
        size = math.prod(s)
        out.append(vec[off:off + size].reshape(s))
        off += size
    return out


def kernel(x, c, ctx, c_ctx, ada_w, ada_b, ln_g, ln_b, conv_w_in, conv_w, conv_w_out, ssm_w_in, ssm_lam_re, ssm_lam_im, ssm_log_step, ssm_b_re, ssm_b_im, ssm_c_re, ssm_c_im, ssm_d, ssm_w_glu, ssm_b_glu, ssm_w_out, loss_target, m_c_ctx, m_ada_w, m_ada_b, m_ln_g, m_ln_b, m_conv_w_in, m_conv_w, m_conv_w_out, m_ssm_w_in, m_ssm_lam_re, m_ssm_lam_im, m_ssm_log_step, m_ssm_b_re, m_ssm_b_im, m_ssm_c_re, m_ssm_c_im, m_ssm_d, m_ssm_w_glu, m_ssm_b_glu, m_ssm_w_out, v_c_ctx, v_ada_w, v_ada_b, v_ln_g, v_ln_b, v_conv_w_in, v_conv_w, v_conv_w_out, v_ssm_w_in, v_ssm_lam_re, v_ssm_lam_im, v_ssm_log_step, v_ssm_b_re, v_ssm_b_im, v_ssm_c_re, v_ssm_c_im, v_ssm_d, v_ssm_w_glu, v_ssm_b_glu, v_ssm_w_out):
    weights = dict(c_ctx=c_ctx, ada_w=ada_w, ada_b=ada_b, ln_g=ln_g, ln_b=ln_b, conv_w_in=conv_w_in, conv_w=conv_w,
                   conv_w_out=conv_w_out, ssm_w_in=ssm_w_in, ssm_lam_re=ssm_lam_re, ssm_lam_im=ssm_lam_im,
                   ssm_log_step=ssm_log_step, ssm_b_re=ssm_b_re, ssm_b_im=ssm_b_im, ssm_c_re=ssm_c_re,
                   ssm_c_im=ssm_c_im, ssm_d=ssm_d, ssm_w_glu=ssm_w_glu, ssm_b_glu=ssm_b_glu, ssm_w_out=ssm_w_out)
    mom_m = dict(c_ctx=m_c_ctx, ada_w=m_ada_w, ada_b=m_ada_b, ln_g=m_ln_g, ln_b=m_ln_b, conv_w_in=m_conv_w_in,
                 conv_w=m_conv_w, conv_w_out=m_conv_w_out, ssm_w_in=m_ssm_w_in, ssm_lam_re=m_ssm_lam_re,
                 ssm_lam_im=m_ssm_lam_im, ssm_log_step=m_ssm_log_step, ssm_b_re=m_ssm_b_re, ssm_b_im=m_ssm_b_im,
                 ssm_c_re=m_ssm_c_re, ssm_c_im=m_ssm_c_im, ssm_d=m_ssm_d, ssm_w_glu=m_ssm_w_glu,
                 ssm_b_glu=m_ssm_b_glu, ssm_w_out=m_ssm_w_out)
    mom_v = dict(c_ctx=v_c_ctx, ada_w=v_ada_w, ada_b=v_ada_b, ln_g=v_ln_g, ln_b=v_ln_b, conv_w_in=v_conv_w_in,
                 conv_w=v_conv_w, conv_w_out=v_conv_w_out, ssm_w_in=v_ssm_w_in, ssm_lam_re=v_ssm_lam_re,
                 ssm_lam_im=v_ssm_lam_im, ssm_log_step=v_ssm_log_step, ssm_b_re=v_ssm_b_re, ssm_b_im=v_ssm_b_im,
                 ssm_c_re=v_ssm_c_re, ssm_c_im=v_ssm_c_im, ssm_d=v_ssm_d, ssm_w_glu=v_ssm_w_glu,
                 ssm_b_glu=v_ssm_b_glu, ssm_w_out=v_ssm_w_out)
    names = list(weights)

    n_lat, d_model = x.shape[1], x.shape[2]
    n_ctx = ctx.shape[1]
    e = 2 * d_model
    n_grp, n_state, grp = ssm_lam_re.shape[2], ssm_lam_re.shape[3], ssm_b_re.shape[4]
    ngb = e // LANES
    ws = ada_w.shape[2]
    tb_tok = min(512, n_lat)
    n_seq = n_ctx + n_lat
    tb_glu = math.gcd(256, n_ctx)
    chip = 2 * lax.axis_index("x") + lax.axis_index("y")
    me = 2 * chip + lax.axis_index("c")
    chips, everyone, pair = ("x", "y"), MESH_AXES, ("c",)

    x2, ctx2, tgt2 = x[0], ctx[0], loss_target[0]

    wc_in_own = conv_w_in[0].astype(BF16)
    later_weights = _Hosted([(w[0].astype(BF16), chips, False) for w in (conv_w_out, ssm_w_in, ssm_w_glu, ssm_w_out)])
    small_full = _exchange(_flat([conv_w[0], ssm_d[0], ssm_b_glu[0]]).reshape(1, -1), chips, False, "ag_small")
    es = conv_w.shape[2]
    conv_w_full = jnp.transpose(small_full[:, 0, :3 * es].reshape(4, 3, es), (1, 0, 2)).reshape(3, e)
    d_full = small_full[:, 0, 3 * es:4 * es].reshape(1, e)
    b_glu_full = small_full[:, 0, 4 * es:5 * es].reshape(1, e)

    c_all = _exchange(c, everyone, False, "ag_c").reshape(8, d_model)
    cc2 = c_ctx.reshape(1, d_model)
    b_sh = lax.dynamic_slice_in_dim(ada_b, chip * ws, ws, axis=1).reshape(DEPTH, 1, ws)
    m_sh = _ada_fwd(c_all, cc2, ada_w, b_sh)
    m_all = _exchange(m_sh, chips, False, "ag_mod")
    m_full = jnp.transpose(m_all, (1, 2, 0, 3)).reshape(DEPTH, 16, 3 * d_model)
    m_lat = lax.dynamic_slice_in_dim(m_full, me, 1, axis=1)
    m_ctx = m_full[:, 8:9]

    def mods(m, i):
        return m[i, :, 0:d_model], m[i, :, d_model:2 * d_model], m[i, :, 2 * d_model:3 * d_model]

    sh0, sc0, gt0 = mods(m_lat, 0)
    sh1, sc1, gt1 = mods(m_lat, 1)
    shc0, scc0, gtc0 = mods(m_ctx, 0)
    shc1, scc1, _ = mods(m_ctx, 1)
    ln0 = (ln_g[0:1], ln_b[0:1])
    ln1 = (ln_g[1:2], ln_b[1:2])

    def lam_view(t):
        return jnp.transpose(t[0], (0, 2, 1)).reshape(2 * n_state, n_grp)

    def lam_back(t):
        return jnp.transpose(t.reshape(2, n_state, n_grp), (0, 2, 1)).reshape(ssm_lam_re.shape)

    def b_view(t):
        return jnp.transpose(t[0], (0, 2, 3, 1)).reshape(2 * n_state * grp, n_grp)

    def b_back(t):
        return jnp.transpose(t.reshape(2, n_state, grp, n_grp), (0, 3, 1, 2)).reshape(ssm_b_re.shape)

    def c_view(t):
        return jnp.transpose(t[0], (0, 2, 3, 1)).reshape(2 * grp * n_state, n_grp)

    def c_back(t):
        return jnp.transpose(t.reshape(2, grp, n_state, n_grp), (0, 3, 1, 2)).reshape(ssm_c_re.shape)

    def channel_major(t):
        return jnp.transpose(t.reshape(2 * n_state, grp, n_grp), (1, 0, 2))

    def by_group(t):
        return jnp.transpose(t.reshape(t.shape[0], 2, n_state, n_grp), (0, 1, 3, 2))

    lam_re2, lam_im2, log_step2 = lam_view(ssm_lam_re), lam_view(ssm_lam_im), ssm_log_step[0]
    b_re_t, b_im_t = channel_major(b_view(ssm_b_re)), channel_major(b_view(ssm_b_im))
    pw_r, pw_i, pq_r, pq_i, bbr, bbi = _zoh_fwd(lam_re2, lam_im2, log_step2, b_re_t, b_im_t)
    sbk = n_grp * n_state // ngb
    pw_r, pw_i, pq_r, pq_i = (by_group(t) for t in (pw_r, pw_i, pq_r, pq_i))
    bbr_g = jnp.transpose(by_group(bbr), (1, 2, 0, 3))
    bbi_g = jnp.transpose(by_group(bbi), (1, 2, 0, 3))

    def power_rows(pw, r, first):
        full = jnp.concatenate([jnp.full((1, n_grp, n_state), first, F32), pw[:, r]], axis=0)
        return jnp.transpose(full.reshape(CHUNK + 1, ngb, sbk), (1, 0, 2))

    s5 = []
    for r in range(2):
        prm = dict(bre=_block_diag(bbr_g[r], ngb), bim=_block_diag(bbi_g[r], ngb),
                   cre=_block_diag(ssm_c_re[0, r], ngb), cim=_block_diag(ssm_c_im[0, r], ngb),
                   wr=power_rows(pw_r, r, 1.0), wi=power_rows(pw_i, r, 0.0))
        half_rows = wc_in_own[r * (d_model // 2):(r + 1) * (d_model // 2)]
        t_op, bp_op, cp_op, (wc_in_half,) = _s5_ops(
            prm["bre"], prm["bim"], prm["cre"], prm["cim"], prm["wr"], prm["wi"], reverse=(r == 1),
            name=f"l1_s5_ops{r}", hosted=_Hosted([(half_rows, chips, False)]))
        s5.append(dict(
            prm, t=t_op, bp=bp_op, cp=cp_op, wc_in_half=wc_in_half,
            tab=_scan_tables(pq_r[:, r], pq_i[:, r], ngb, reverse=(r == 1)),
            tab_adj=_scan_tables(pq_r[:, r], -pq_i[:, r], ngb, reverse=(r == 0))))
    wc_in = jnp.concatenate([s5[0]["wc_in_half"], s5[1]["wc_in_half"]], axis=1)

    p0, h0, gathered = _inproj(x2, sc0, sh0, wc_in, tb=min(1024, n_lat), name="l0_inproj", hosted=later_weights)
    wc_out, ws_in, w_glu, ws_out = gathered
    wc_out, w_glu, ws_out = wc_out.reshape(e, d_model), w_glu.reshape(e, e), ws_out.reshape(e, d_model)
    wc_in_t, ws_in_t, wc_out_t = jnp.transpose(wc_in, (0, 2, 1)), jnp.transpose(ws_in, (0, 2, 1)), wc_out.T
    pc0, hc0 = _inproj(ctx2, scc0, shc0, wc_in, tb=tb_tok, name="l0_inproj_ctx")
    xhat0, rstd0, g0, yc0, f0 = _convgate(p0, x2, gt0, conv_w_full, wc_out, *ln0, grid_mode=True, tb=tb_tok, name="l0_conv")
    chat0, crstd0, gc0, ycc0, fc0 = _convgate(pc0, ctx2, gtc0, conv_w_full, wc_out, *ln0, grid_mode=False, tb=tb_tok,
                                              name="l0_conv_ctx")

    seq_rows = [(n_seq, n_ctx), (n_seq, 0)]
    useq_f, useq_b, h1 = _inproj_seq(xhat0, sc1, sh1, ws_in[0:2], ln0, tb=min(1024, n_lat), seq_rows=seq_rows,
                                     name="l1_inproj_u")
    z1, _ = _inproj(xhat0, sc1, sh1, ws_in[2:4], lnaff=ln0, tb=min(1024, n_lat), name="l1_inproj_z")
    uc, hc1 = _inproj(chat0, scc1, shc1, ws_in[0:2], lnaff=ln0, tb=tb_tok, gb_rows=[(n_ctx, 0)], name="l1_inproj_ctx")
    useq = [useq_f.at[:, 0:n_ctx].set(uc), useq_b.at[:, n_lat:].set(uc)]
    y_dir, hp_dir = [], []
    for r in range(2):
        yr, hcr = _s5_fwd(useq[r], s5[r]["t"], s5[r]["bp"], s5[r]["cp"], s5[r]["tab"], reverse=(r == 1),
                          name=f"l1_s5_fwd{r}")
        y_dir.append(yr)
        hp_dir.append(hcr)

    (loss, dxres, do1, gz1, gg1, dq1, dz1, dy_f, dy_b, dg1, db1, dgt1, dbglu, dd) = _glu_loss(
        useq[0], y_dir[0], y_dir[1], z1, xhat0, ln0, gt1, d_full, w_glu, b_glu_full, ws_out, ln1, tgt2,
        offs=(n_ctx, n_ctx, 0), dy_rows=seq_rows, tb=tb_glu, name="l1_glu_loss")
    no_dy = jnp.zeros((ngb, n_ctx, LANES), BF16)
    dy_dir = [dy_f.at[:, 0:n_ctx].set(no_dy), dy_b.at[:, n_lat:].set(no_dy)]

    tl = min(1024, n_lat)

    def owner_slices(name, full):
        w = weights[name]
        return full.reshape(8, math.prod(w.shape[:-1]) // 2, w.shape[-1])

    def scatter(named):
        return _Hosted([(owner_slices(name, full), everyone, True) for name, full in named])

    def siblings(names):
        return _Hosted([(_sum_parts(rs_parts[name], "sum_" + name), pair, False) for name in names])

    rs_parts, both_halves = {}, {}

    gw_glu = _wgrad(gg1, dq1, n_chunks=1, tm=e // 2, tl=min(2 * tl, n_lat), name="wg_glu")
    gw_ssm_out = _wgrad(gz1, do1, n_chunks=1, tm=e, tl=min(2 * tl, n_lat), name="wg_ssm_out")
    du_dir, s5_grads = [], []
    for r in range(2):
        if r == 0:
            hosted = scatter([("ssm_w_glu", gw_glu), ("ssm_w_out", gw_ssm_out)])
        else:
            hosted = siblings(["ssm_w_glu", "ssm_w_out"])
        dur, dt_op, dbp_op, dcp_op, da8, extra = _s5_bwd(useq[r], dy_dir[r], hp_dir[r], s5[r]["t"], s5[r]["bp"],
                                                         s5[r]["cp"], s5[r]["tab_adj"], reverse=(r == 1),
                                                         name=f"l1_s5_bwd{r}", hosted=hosted)
        if r == 0:
            rs_parts["ssm_w_glu"], rs_parts["ssm_w_out"] = extra
        else:
            both_halves["ssm_w_glu"], both_halves["ssm_w_out"] = extra
        du_dir.append(dur)
        prm = s5[r]
        s5_grads.append(functools.partial(
            _s5_ops_bwd, prm["bre"], prm["bim"], prm["cre"], prm["cim"], prm["wr"], prm["wi"], prm["wr"][:, 1:2],
            prm["wi"][:, 1:2], dt_op, dbp_op, dcp_op, da8, reverse=(r == 1), name=f"l1_s5_ops_bwd{r}"))
    dp1, dr0, df0, dsc1, dsh1, dg0, db0, dgt0 = _ssm_inbwd(
        du_dir[0], du_dir[1], ws_in_t, xhat0, rstd0, ln0, sc1, gt0, f0, lat=(dy_dir[1], dz1, d_full, dxres),
        row_f=n_ctx, row_b=0, tb=tb_glu, name="l1_inbwd")
    dpc1, drc0, dfc0, dscc1, dshc1, dgc0, dbc0, dgtc0 = _ssm_inbwd(
        du_dir[0], du_dir[1], ws_in_t, chat0, crstd0, ln0, scc1, gtc0, fc0, lat=None,
        row_f=0, row_b=n_lat, tb=n_ctx, name="l1_inbwd_ctx")

    def conv_backward(df, p, yc, dr, xin, sc, grid_mode, tag, hosted_b=None, hosted_in=None):
        dbg, dz, dyc = _conv_bwd_a(df, wc_out_t, p, yc, tb=tb_tok, name="l0_bwd_a" + tag)
        dp, dcw, extra_b = _conv_bwd_b(dyc, p, dbg, dz, conv_w_full, grid_mode=grid_mode, tb=tb_glu,
                                       name="l0_bwd_b" + tag, hosted=hosted_b)
        gx, dsc, dsh, extra_in = _conv_inbwd(dp, wc_in_t, dr, xin, sc, tb=tb_tok, name="l0_inbwd" + tag,
                                             hosted=None if hosted_in is None else hosted_in(dp, extra_b))
        return dp, dcw, gx, dsc, dsh, extra_b, extra_in

    dpc0, dcwc0, _, dscc0, dshc0, _, _ = conv_backward(dfc0, pc0, ycc0, drc0, ctx2, scc0, False, "_ctx")
    gw_conv_out = _wgrad(g0, df0, n_chunks=1, tm=e, tl=tl, name="wg_conv_out",
                         init=_wgrad(gc0, dfc0, n_chunks=1, tm=e, tl=tl, name="wg_conv_out_ctx"))
    gw_ssm_in = _wgrad(h1, dp1, n_chunks=4, tm=d_model, tl=tl, name="wg_ssm_in",
                       init=_wgrad(hc1, dpc1, n_chunks=4, tm=d_model, tl=tl, name="wg_ssm_in_ctx"))
    gw_conv_in_ctx = _wgrad(hc0, dpc0, n_chunks=4, tm=d_model, tl=tl, name="wg_conv_in_ctx")

    def behind_inbwd(dp, arrived):
        rs_parts["ssm_w_in"], rs_parts["conv_w_out"] = arrived
        gw_conv_in = _wgrad(h0, dp, n_chunks=4, tm=d_model, tl=tl, name="wg_conv_in", init=gw_conv_in_ctx)
        both = siblings(["ssm_w_in", "conv_w_out"])
        return _Hosted(scatter([("conv_w_in", gw_conv_in)]).items + both.items)

    dp0, dcw0, grad_x, dsc0, dsh0, _, extra_in = conv_backward(
        df0, p0, yc0, dr0, x2, sc0, True, "", hosted_b=scatter([("ssm_w_in", gw_ssm_in), ("conv_w_out", gw_conv_out)]),
        hosted_in=behind_inbwd)
    rs_parts["conv_w_in"], both_halves["ssm_w_in"], both_halves["conv_w_out"] = extra_in
    *grads_r0, _ = s5_grads[0]()
    *grads_r1, (both_halves["conv_w_in"],) = s5_grads[1](hosted=siblings(["conv_w_in"]))
    s5_grads = [grads_r0, grads_r1]

    grads, deltas, new_m, new_v = {}, {}, {}, {}
    for name in ("ssm_w_glu", "ssm_w_out", "ssm_w_in", "conv_w_out", "conv_w_in"):
        w = weights[name]
        rows, cols = math.prod(w.shape[:-1]), w.shape[-1]
        both = both_halves[name].reshape(rows, cols)
        dlt, nm, nv = _adamw(w.reshape(rows, cols), both, mom_m[name].reshape(rows, cols),
                             mom_v[name].reshape(rows, cols), "adamw_" + name)
        grads[name], deltas[name] = both.reshape(w.shape), dlt.reshape(w.shape)
        new_m[name], new_v[name] = nm.reshape(w.shape), nv.reshape(w.shape)

    gpn = (n_grp, grp, n_state)
    small_parts = [
        jnp.concatenate([dg0 + dgc0, dg1], axis=0), jnp.concatenate([db0 + dbc0, db1], axis=0),
        dcw0 + dcwc0, dd, dbglu,
        jnp.stack([s5_grads[r][4] for r in range(2)]),
    ] + [jnp.stack([_block_diag_t(s5_grads[r][k], *gpn) for r in range(2)]) for k in range(4)]
    small_shapes = [p.shape for p in small_parts]
    flat = _flat(small_parts)
    quantum = 8 * SUBLANES * LANES
    n_flat = -(-flat.shape[0] // quantum) * quantum
    flat = jnp.pad(flat, (0, n_flat - flat.shape[0])).reshape(8, n_flat // (8 * LANES), LANES)
    red = _sum_parts(_exchange(flat, everyone, True, "rs_small"), "sum_small")
    red = _exchange(red, everyone, False, "ag_small_grads").reshape(-1)
    g_ln_g, g_ln_b, g_conv_w, g_d, g_bglu, g_a, g_bbr, g_bbi, g_cre, g_cim = _unflat(red, small_shapes)

    def groups_minor(t, lead):
        return jnp.moveaxis(t, 1, -1).reshape(lead, n_grp)

    g_a = g_a.reshape(2, ngb, 2, sbk)
    dar = groups_minor(g_a[:, :, 0].reshape(2, n_grp, n_state), 2 * n_state)
    dai = groups_minor(g_a[:, :, 1].reshape(2, n_grp, n_state), 2 * n_state)
    dbbr_t = jnp.transpose(g_bbr, (2, 0, 3, 1)).reshape(grp, 2 * n_state, n_grp)
    dbbi_t = jnp.transpose(g_bbi, (2, 0, 3, 1)).reshape(grp, 2 * n_state, n_grp)
    z_lre, z_lim, z_ls, z_bre, z_bim = _zoh_bwd(lam_re2, lam_im2, log_step2, b_re_t, b_im_t, dar, dai, dbbr_t, dbbi_t)

    zero = jnp.zeros((1, d_model), F32)
    dm_rows = jnp.stack([
        jnp.stack([jnp.concatenate([dsh0, dsc0, dgt0], axis=1), jnp.concatenate([dshc0, dscc0, dgtc0], axis=1)]),
        jnp.stack([jnp.concatenate([dsh1, dsc1, dgt1], axis=1), jnp.concatenate([dshc1, dscc1, zero], axis=1)]),
    ]).reshape(DEPTH, 2, 3 * d_model)
    dm_all = _exchange(dm_rows, everyone, False, "ag_dmod")
    dm_sh = lax.dynamic_slice_in_dim(dm_all, chip * ws, ws, axis=3)
    g_ada_w, g_ada_b, ds_part = _ada_bwd(c_all, cc2, ada_w, dm_all, dm_sh)
    g_cctx = _cctx_grad(_exchange(ds_part, chips, False, "ag_dsctx"), cc2)

    grads["ada_w"] = g_ada_w
    dlt, nm, nv = _adamw(ada_w.reshape(-1, ws), g_ada_w.reshape(-1, ws), m_ada_w.reshape(-1, ws),
                         v_ada_w.reshape(-1, ws), "adamw_ada_w")
    deltas["ada_w"], new_m["ada_w"], new_v["ada_w"] = dlt.reshape(ada_w.shape), nm.reshape(ada_w.shape), nv.reshape(ada_w.shape)

    def chip_cols(full, rows):
        return lax.dynamic_slice_in_dim(full.reshape(rows, e), chip * es, es, axis=1)

    def same(t):
        return t

    def channel_minor_back(t):
        return jnp.transpose(t, (1, 0, 2)).reshape(2 * n_state * grp, n_grp)

    small = dict(
        c_ctx=(g_cctx, lambda t: t.reshape(1, d_model), lambda t: t.reshape(c_ctx.shape)),
        ada_b=(g_ada_b.reshape(ada_b.shape), same, same),
        ln_g=(g_ln_g, same, same), ln_b=(g_ln_b, same, same),
        conv_w=(chip_cols(g_conv_w, 3), lambda t: t[0], lambda t: t.reshape(conv_w.shape)),
        ssm_lam_re=(z_lre, lam_view, lam_back), ssm_lam_im=(z_lim, lam_view, lam_back),
        ssm_log_step=(z_ls, lambda t: t[0], lambda t: t.reshape(ssm_log_step.shape)),
        ssm_b_re=(channel_minor_back(z_bre), b_view, b_back), ssm_b_im=(channel_minor_back(z_bim), b_view, b_back),
        ssm_c_re=(groups_minor(g_cre, 2 * grp * n_state), c_view, c_back),
        ssm_c_im=(groups_minor(g_cim, 2 * grp * n_state), c_view, c_back),
        ssm_d=(chip_cols(g_d, 1), same, same), ssm_b_glu=(chip_cols(g_bglu, 1), same, same))
    for n, (g_view, view, back) in small.items():
        dlt, nm, nv = _adamw(view(weights[n]), g_view, view(mom_m[n]), view(mom_v[n]), "adamw_" + n)
        grads[n], deltas[n], new_m[n], new_v[n] = back(g_view), back(dlt), back(nm), back(nv)

    loss_total = lax.psum(loss[0, 0], MESH_AXES)
    return (loss_total, grad_x.reshape(x.shape), *[grads[n] for n in names], *[deltas[n] for n in names],
            *[new_m[n] for n in names], *[new_v[n] for n in names])
```

```python
import functools
import math

import jax
import jax.numpy as jnp
from jax import lax
from jax.experimental import pallas as pl
from jax.experimental.pallas import tpu as pltpu

F32 = jnp.float32
BF16 = jnp.bfloat16
LANES = 128
SUBLANES = 8
VMEM_LIMIT = 56 * 1024 * 1024
MESH_AXES = ("x", "y", "c")
HIGHEST = lax.Precision.HIGHEST

GRID_W = 64
LN_EPS = 1e-5
DEPTH = 2
DN_ALPHA = (2 * DEPTH) ** 0.25
ADAM_LR, ADAM_B1, ADAM_B2, ADAM_EPS, ADAM_WD, ADAM_STEP = 0.001, 0.9, 0.999, 1e-08, 0.01, 10
GELU_K = math.sqrt(2.0 / math.pi)
GELU_C = 0.044715


def _params(n_grid_axes):
    return pltpu.CompilerParams(dimension_semantics=("arbitrary",) * n_grid_axes, vmem_limit_bytes=VMEM_LIMIT)


def _dot(a, b):
    return jnp.dot(a, b, preferred_element_type=F32)


def _dot_nt(a, b):
    return lax.dot_general(a, b, (((1,), (1,)), ((), ())), preferred_element_type=F32)


def _dot_tn(a, b):
    return lax.dot_general(a, b, (((0,), (0,)), ((), ())), preferred_element_type=F32)


def _sigmoid(x):
    return 0.5 * jnp.tanh(0.5 * x) + 0.5


def _colsum(x):
    return jnp.sum(x, axis=0, keepdims=True)


def _rowmean(x):
    return jnp.mean(x, axis=-1, keepdims=True)


def _zero_first(first, *refs):
    @pl.when(first)
    def _():
        for ref in refs:
            ref[...] = jnp.zeros_like(ref)


def _acc(ref, value):
    ref[...] += value


def _exchange_copies(src_ref, out_ref, send_sems, recv_sems, own_sem, axes, all_to_all, sem0=0):
    n_peers = 2 ** len(axes)
    pos = {a: lax.axis_index(a) for a in MESH_AXES}

    def index(p):
        return sum(p[a] * (2 ** (len(axes) - 1 - i)) for i, a in enumerate(axes))

    me = index(pos)
    own = pltpu.make_async_copy(src_ref.at[me] if all_to_all else src_ref, out_ref.at[me], own_sem)
    copies = []
    for k in range(1, n_peers):
        peer = dict(pos)
        for i, a in enumerate(axes):
            if (k >> (len(axes) - 1 - i)) & 1:
                peer[a] = 1 - pos[a]
        copies.append(pltpu.make_async_remote_copy(
            src_ref=src_ref.at[index(peer)] if all_to_all else src_ref,
            dst_ref=out_ref.at[me],
            send_sem=send_sems.at[sem0 + k - 1],
            recv_sem=recv_sems.at[sem0 + k - 1],
            device_id=tuple(peer[a] for a in MESH_AXES),
            device_id_type=pl.DeviceIdType.MESH,
        ))
    return copies, own


def _exchange_shape(src, axes, all_to_all):
    block = tuple(src.shape[1:] if all_to_all else src.shape)
    return jax.ShapeDtypeStruct((2 ** len(axes),) + block, src.dtype)


def _exchange(src, axes, all_to_all, name):
    n_peers = 2 ** len(axes)

    def body(src_ref, out_ref, send_sems, recv_sems, own_sem):
        copies, own = _exchange_copies(src_ref, out_ref, send_sems, recv_sems, own_sem, axes, all_to_all)
        own.start()
        for cp in copies:
            cp.start()
        for cp in copies:
            cp.wait()
        own.wait()

    return pl.pallas_call(
        body,
        name=name,
        out_shape=_exchange_shape(src, axes, all_to_all),
        in_specs=[pl.BlockSpec(memory_space=pltpu.HBM)],
        out_specs=pl.BlockSpec(memory_space=pltpu.HBM),
        scratch_shapes=[
            pltpu.SemaphoreType.DMA((n_peers - 1,)),
            pltpu.SemaphoreType.DMA((n_peers - 1,)),
            pltpu.SemaphoreType.DMA,
        ],
    )(src)


class _Hosted:
    def __init__(self, items):
        self.items = items
        self.args = [src for src, _, _ in items]
        self.in_specs = [pl.BlockSpec(memory_space=pltpu.HBM)] * len(items)
        self.out_specs = [pl.BlockSpec(memory_space=pltpu.HBM)] * len(items)
        self.out_shapes = [_exchange_shape(*item) for item in items]
        n_remote = sum(2 ** len(axes) - 1 for _, axes, _ in items)
        self.scratch = [pltpu.SemaphoreType.DMA((n_remote,)), pltpu.SemaphoreType.DMA((n_remote,)),
                        pltpu.SemaphoreType.DMA((len(items),))]

    def _copies(self, src_refs, out_refs, send_sems, recv_sems, own_sems):
        out, sem0 = [], 0
        for n, (_, axes, all_to_all) in enumerate(self.items):
            copies, own = _exchange_copies(src_refs[n], out_refs[n], send_sems, recv_sems, own_sems.at[n], axes,
                                           all_to_all, sem0)
            out += [own] + copies
            sem0 += len(copies)
        return out

    def start(self, *refs):
        for cp in self._copies(*refs):
            cp.start()

    def wait(self, *refs):
        for cp in self._copies(*refs):
            cp.wait()


def _call(body, *, name, grid, in_specs, out_specs, out_shape, scratch_shapes, args, hosted=None):
    params = _params(len(grid))
    if hosted is None:
        outs = pl.pallas_call(body, name=name, grid=grid, in_specs=in_specs, out_specs=tuple(out_specs),
                              out_shape=tuple(out_shape), scratch_shapes=list(scratch_shapes), compiler_params=params)(*args)
        return list(outs), []
    n_in, n_out, n_scr, n_h = len(in_specs), len(out_shape), len(scratch_shapes), len(hosted.items)

    def wrapped(*refs):
        ins, h_in = refs[:n_in], refs[n_in:n_in + n_h]
        outs, h_out = refs[n_in + n_h:n_in + n_h + n_out], refs[n_in + n_h + n_out:n_in + 2 * n_h + n_out]
        scr = refs[n_in + 2 * n_h + n_out:]
        first = functools.reduce(jnp.logical_and, [pl.program_id(k) == 0 for k in range(len(grid))])
        last = functools.reduce(jnp.logical_and, [pl.program_id(k) == grid[k] - 1 for k in range(len(grid))])

        @pl.when(first)
        def _():
            hosted.start(h_in, h_out, *scr[n_scr:])

        body(*ins, *outs, *scr[:n_scr])

        @pl.when(last)
        def _():
            hosted.wait(h_in, h_out, *scr[n_scr:])

    outs = pl.pallas_call(
        wrapped, name=name, grid=grid, in_specs=[*in_specs, *hosted.in_specs],
        out_specs=(*out_specs, *hosted.out_specs), out_shape=(*out_shape, *hosted.out_shapes),
        scratch_shapes=[*scratch_shapes, *hosted.scratch], compiler_params=params)(*args, *hosted.args)
    return list(outs[:n_out]), list(outs[n_out:])


def _sum_parts(parts, name):
    n_parts, rows, cols = parts.shape
    tr = rows
    while n_parts * tr * cols * 4 > 8 * 1024 * 1024 and tr % 16 == 0:
        tr //= 2

    def body(p_ref, o_ref):
        total = p_ref[0]
        for k in range(1, n_parts):
            total = total + p_ref[k]
        o_ref[...] = total

    return pl.pallas_call(
        body,
        name=name,
        grid=(rows // tr,),
        out_shape=jax.ShapeDtypeStruct((rows, cols), F32),
        in_specs=[pl.BlockSpec((n_parts, tr, cols), lambda i: (0, i, 0))],
        out_specs=pl.BlockSpec((tr, cols), lambda i: (i, 0)),
        compiler_params=_params(1),
    )(parts)


def _adamw(w, g, m, v, name):
    rows, cols = w.shape
    tr = rows
    while tr * cols * 4 > 2 * 1024 * 1024 and tr % 16 == 0:
        tr //= 2

    def body(w_ref, g_ref, m_ref, v_ref, d_ref, nm_ref, nv_ref):
        gv = g_ref[...]
        nm = ADAM_B1 * m_ref[...] + (1.0 - ADAM_B1) * gv
        nv = ADAM_B2 * v_ref[...] + (1.0 - ADAM_B2) * (gv * gv)
        m_hat = nm / (1.0 - ADAM_B1 ** ADAM_STEP)
        v_hat = nv / (1.0 - ADAM_B2 ** ADAM_STEP)
        d_ref[...] = -ADAM_LR * (m_hat / (jnp.sqrt(v_hat) + ADAM_EPS) + ADAM_WD * w_ref[...])
        nm_ref[...] = nm
        nv_ref[...] = nv

    spec = pl.BlockSpec((tr, cols), lambda i: (i, 0))
    shape = jax.ShapeDtypeStruct((rows, cols), F32)
    return pl.pallas_call(
        body, name=name, grid=(rows // tr,), out_shape=(shape, shape, shape),
        in_specs=[spec] * 4, out_specs=(spec, spec, spec), compiler_params=_params(1),
    )(w, g, m, v)


def _ada_rows(c_ref, cc_ref):
    rows = jnp.concatenate([c_ref[...], jnp.broadcast_to(cc_ref[...], c_ref.shape)], axis=0)
    return rows


def _ada_fwd(c_all, c_ctx, w_sh, b_sh):
    n_layers, _, ws = w_sh.shape

    def body(c_ref, cc_ref, w_ref, b_ref, o_ref):
        rows = _ada_rows(c_ref, cc_ref)
        s = rows * _sigmoid(rows)
        for i in range(n_layers):
            o_ref[i] = jnp.dot(s, w_ref[i], precision=HIGHEST, preferred_element_type=F32) + b_ref[i]

    return pl.pallas_call(
        body, name="ada_fwd", out_shape=jax.ShapeDtypeStruct((n_layers, 16, ws), F32),
        compiler_params=pltpu.CompilerParams(vmem_limit_bytes=VMEM_LIMIT),
    )(c_all, c_ctx, w_sh, b_sh)


def _ada_bwd(c_all, c_ctx, w_sh, dm_full, dm_sh):
    n_layers, d_model, ws = w_sh.shape
    n_dev = dm_full.shape[0]
    cols = dm_full.shape[-1]

    def body(c_ref, cc_ref, w_ref, dmf_ref, dms_ref, gw_ref, gb_ref, ds_ref):
        rows = _ada_rows(c_ref, cc_ref)
        s = rows * _sigmoid(rows)
        ds = jnp.zeros((8, d_model), F32)
        for i in range(n_layers):
            ctx_s = dms_ref[0, i, 1:2, :]
            ctx_f = dmf_ref[0, i, 1:2, :]
            ex_f = dmf_ref[0, i, 0:1, :]
            for k in range(1, n_dev):
                ctx_s = ctx_s + dms_ref[k, i, 1:2, :]
                ctx_f = ctx_f + dmf_ref[k, i, 1:2, :]
                ex_f = ex_f + dmf_ref[k, i, 0:1, :]
            gb_ref[i] = ex_f + ctx_f
            r = jnp.concatenate([dms_ref[k, i, 0:1, :] for k in range(n_dev)] + [ctx_s, jnp.zeros((7, ws), F32)], axis=0)
            gw_ref[i] = lax.dot_general(s, r, (((0,), (0,)), ((), ())), precision=HIGHEST, preferred_element_type=F32)
            ds = ds + lax.dot_general(jnp.broadcast_to(ctx_s, (8, ws)), w_ref[i], (((1,), (1,)), ((), ())),
                                      precision=HIGHEST, preferred_element_type=F32)
        ds_ref[...] = ds

    return pl.pallas_call(
        body, name="ada_bwd",
        out_shape=(jax.ShapeDtypeStruct((n_layers, d_model, ws), F32), jax.ShapeDtypeStruct((n_layers, 1, cols), F32),
                   jax.ShapeDtypeStruct((8, d_model), F32)),
        compiler_params=pltpu.CompilerParams(vmem_limit_bytes=VMEM_LIMIT),
    )(c_all, c_ctx, w_sh, dm_full, dm_sh)


def _cctx_grad(ds_parts, c_ctx):
    def body(p_ref, c_ref, o_ref):
        tot = p_ref[0, 0:1, :]
        for k in range(1, ds_parts.shape[0]):
            tot = tot + p_ref[k, 0:1, :]
        cv = c_ref[...]
        sg = _sigmoid(cv)
        o_ref[...] = tot * (sg * (1.0 + cv * (1.0 - sg)))

    return pl.pallas_call(body, name="cctx_grad", out_shape=jax.ShapeDtypeStruct(c_ctx.shape, F32))(ds_parts, c_ctx)


def _zoh_math(lam_re, lam_im, log_step, b_re, b_im):
    n_state = lam_re.shape[0] // 2
    dt = jnp.exp(jnp.concatenate([jnp.broadcast_to(log_step[r:r + 1], (n_state, log_step.shape[1])) for r in range(2)],
                                 axis=0))
    mag = jnp.exp(lam_re * dt)
    ar = mag * jnp.cos(lam_im * dt)
    ai = mag * jnp.sin(lam_im * dt)
    qr, qi = ar - 1.0, ai
    den = lam_re * lam_re + lam_im * lam_im
    fr = (qr * lam_re + qi * lam_im) / den
    fi = (qi * lam_re - qr * lam_im) / den
    bbr = fr[None] * b_re - fi[None] * b_im
    bbi = fr[None] * b_im + fi[None] * b_re
    return ar, ai, bbr, bbi


def _zoh_fwd(lam_re, lam_im, log_step, b_re, b_im):
    rg, n = lam_re.shape

    def body(lr_ref, li_ref, ls_ref, br_ref, bi_ref, pr_ref, pi_ref, qr_ref, qi_ref, bbr_ref, bbi_ref):
        ar, ai, bbr, bbi = _zoh_math(lr_ref[...], li_ref[...], ls_ref[...], br_ref[...], bi_ref[...])
        bbr_ref[...] = bbr
        bbi_ref[...] = bbi

        def powers(base_r, base_i, r_ref, i_ref):
            pr, pi_ = base_r, base_i
            for k in range(8):
                r_ref[k] = pr
                i_ref[k] = pi_
                pr, pi_ = pr * base_r - pi_ * base_i, pr * base_i + pi_ * base_r

        powers(ar, ai, pr_ref, pi_ref)
        powers(pr_ref[7], pi_ref[7], qr_ref, qi_ref)

    pw = jax.ShapeDtypeStruct((8, rg, n), F32)
    bb = jax.ShapeDtypeStruct(b_re.shape, F32)
    return pl.pallas_call(body, name="zoh_fwd", out_shape=(pw, pw, pw, pw, bb, bb))(lam_re, lam_im, log_step, b_re, b_im)


def _zoh_bwd(lam_re, lam_im, log_step, b_re, b_im, dar, dai, dbbr, dbbi):
    def body(lr_ref, li_ref, ls_ref, br_ref, bi_ref, dar_ref, dai_ref, dbr_ref, dbi_ref, *outs):
        _, vjp = jax.vjp(_zoh_math, lr_ref[...], li_ref[...], ls_ref[...], br_ref[...], bi_ref[...])
        grads = vjp((dar_ref[...], dai_ref[...], dbr_ref[...], dbi_ref[...]))
        for o_ref, gval in zip(outs, grads):
            o_ref[...] = gval

    shapes = tuple(jax.ShapeDtypeStruct(a.shape, F32) for a in (lam_re, lam_im, log_step, b_re, b_im))
    return pl.pallas_call(body, name="zoh_bwd", out_shape=shapes)(lam_re, lam_im, log_step, b_re, b_im, dar, dai, dbbr, dbbi)


def _inproj(xin, sc, sh, w, *, lnaff=None, tb, gb_rows=None, name, hosted=None):
    n_tok, d_model = xin.shape
    n_chunks, _, cw = w.shape
    tb = min(tb, n_tok)
    nq = cw // LANES
    has_ln = lnaff is not None
    n_out = 1 if gb_rows is None else len(gb_rows)

    def body(*refs):
        if has_ln:
            x_ref, g_ref, b_ref, sc_ref, sh_ref, w_ref = refs[:6]
        else:
            x_ref, sc_ref, sh_ref, w_ref = refs[:4]
        p_refs, h_ref = refs[-1 - n_out:-1], refs[-1]

        @pl.when(pl.program_id(1) == 0)
        def _():
            xv = x_ref[...]
            if has_ln:
                xv = xv * g_ref[...] + b_ref[...]
            h_ref[...] = (xv * (1.0 + sc_ref[...]) + sh_ref[...]).astype(BF16)

        acc = _dot(h_ref[...], w_ref[0]).astype(BF16)
        if gb_rows is None:
            p_refs[0][0] = acc
        else:
            for p_ref in p_refs:
                for q in range(nq):
                    p_ref[q] = acc[:, q * LANES:(q + 1) * LANES]

    vec = pl.BlockSpec((1, d_model), lambda i, j: (0, 0))
    in_specs = [pl.BlockSpec((tb, d_model), lambda i, j: (i, 0))] + ([vec, vec] if has_ln else []) + [
        vec, vec, pl.BlockSpec((1, d_model, cw), lambda i, j: (j, 0, 0))]
    if gb_rows is None:
        p_shapes = [jax.ShapeDtypeStruct((n_chunks, n_tok, cw), BF16)]
        p_specs = [pl.BlockSpec((1, tb, cw), lambda i, j: (j, i, 0))]
    else:
        p_shapes, p_specs = [], []
        for total, off in gb_rows:
            assert off % tb == 0
            p_shapes.append(jax.ShapeDtypeStruct((n_chunks * nq, total, LANES), BF16))
            p_specs.append(pl.BlockSpec((nq, tb, LANES), functools.partial(lambda i, j, ob: (j, i + ob, 0), ob=off // tb)))
    args = (xin,) + (tuple(lnaff) if has_ln else ()) + (sc, sh, w)
    outs, extra = _call(
        body, name=name, grid=(n_tok // tb, n_chunks), in_specs=in_specs,
        out_specs=[*p_specs, pl.BlockSpec((tb, d_model), lambda i, j: (i, 0))],
        out_shape=[*p_shapes, jax.ShapeDtypeStruct((n_tok, d_model), BF16)], scratch_shapes=[], args=args, hosted=hosted)
    return (*outs, extra) if hosted is not None else tuple(outs)


def _inproj_seq(xin, sc, sh, w, lnaff, *, tb, seq_rows, name):
    n_tok, d_model = xin.shape
    n_chunks, _, cw = w.shape
    tb = min(tb, n_tok)
    nq = cw // LANES
    n_out = len(seq_rows)
    steps = (n_tok // tb) * n_chunks

    def body(x_ref, g_ref, b_ref, sc_ref, sh_ref, w_ref, *rest):
        p_refs, h_ref, stage, sems = rest[:n_out], rest[n_out], rest[n_out + 1], rest[n_out + 2]
        i, j = pl.program_id(0), pl.program_id(1)
        step = i * n_chunks + j
        slot = step % 2

        def copies(from_slot):
            return [pltpu.make_async_copy(stage.at[from_slot],
                                          p_ref.at[pl.ds(j * nq, nq), pl.ds(off + i * tb, tb), :], sems.at[from_slot, k])
                    for k, (p_ref, (_, off)) in enumerate(zip(p_refs, seq_rows))]

        @pl.when(step >= 2)
        def _():
            for cp in copies(slot):
                cp.wait()

        @pl.when(j == 0)
        def _():
            xv = x_ref[...] * g_ref[...] + b_ref[...]
            h_ref[...] = (xv * (1.0 + sc_ref[...]) + sh_ref[...]).astype(BF16)

        acc = _dot(h_ref[...], w_ref[0]).astype(BF16)
        for q in range(nq):
            stage[slot, q] = acc[:, q * LANES:(q + 1) * LANES]
        for cp in copies(slot):
            cp.start()

        @pl.when(step == steps - 1)
        def _():
            for cp in copies(slot):
                cp.wait()
            if steps > 1:
                for cp in copies(1 - slot):
                    cp.wait()

    vec = pl.BlockSpec((1, d_model), lambda i, j: (0, 0))
    tok = pl.BlockSpec((tb, d_model), lambda i, j: (i, 0))
    return pl.pallas_call(
        body, name=name, grid=(n_tok // tb, n_chunks),
        in_specs=[tok, vec, vec, vec, vec, pl.BlockSpec((1, d_model, cw), lambda i, j: (j, 0, 0))],
        out_specs=(*[pl.BlockSpec(memory_space=pltpu.HBM)] * n_out, tok),
        out_shape=(*[jax.ShapeDtypeStruct((n_chunks * nq, total, LANES), BF16) for total, _ in seq_rows],
                   jax.ShapeDtypeStruct((n_tok, d_model), BF16)),
        scratch_shapes=[pltpu.VMEM((2, nq, tb, LANES), BF16), pltpu.SemaphoreType.DMA((2, n_out))],
        compiler_params=_params(2),
    )(xin, lnaff[0], lnaff[1], sc, sh, w)


def _shifted(u, rows, width, tb):
    col = rows % width
    um = jnp.where(col == 0, 0.0, pltpu.roll(u, 1, 0))
    up = jnp.where(col == width - 1, 0.0, pltpu.roll(u, tb - 1, 0))
    return um, up


def _slab_width(e):
    return min(512, e // 2)


def _convgate(p, x, gt, conv_w, w_out, ln_g, ln_b, *, grid_mode, tb, name):
    _, n_tok, e = p.shape
    d_model = x.shape[1]
    eh = e // 2
    if not grid_mode:
        tb = n_tok
    tb = min(tb, n_tok)
    nb = n_tok // tb
    hb = tb // GRID_W
    cs = _slab_width(e)

    def body(*refs):
        if grid_mode:
            (bg_ref, cg_ref, v_ref, z_ref, cgp_ref, vp_ref, cgn_ref, vn_ref, x_ref, gt_ref, cw_ref, wo_ref, lg_ref,
             lb_ref, xh_ref, rs_ref, g_ref, yc_ref, f_ref) = refs
        else:
            (bg_ref, cg_ref, v_ref, z_ref, x_ref, gt_ref, cw_ref, wo_ref, lg_ref, lb_ref, xh_ref, rs_ref, g_ref,
             yc_ref, f_ref) = refs
        i = pl.program_id(0)
        rows = lax.broadcasted_iota(jnp.int32, (tb, 1), 0)
        for c0 in range(0, e, cs):
            sl = slice(c0, c0 + cs)
            u = cg_ref[0, :, sl].astype(F32) * v_ref[0, :, sl].astype(F32)
            w = cw_ref[:, sl]
            if grid_mode and c0 >= eh:
                hs = slice(c0 - eh, c0 - eh + cs)
                uprev = cgp_ref[0, :, hs].astype(F32) * vp_ref[0, :, hs].astype(F32)
                unext = cgn_ref[0, :, hs].astype(F32) * vn_ref[0, :, hs].astype(F32)
                uprev = jnp.where(i > 0, uprev, 0.0)
                unext = jnp.where(i < nb - 1, unext, 0.0)
                if tb > GRID_W:
                    um = jnp.concatenate([uprev, u[:tb - GRID_W]], axis=0)
                    up = jnp.concatenate([u[GRID_W:], unext], axis=0)
                else:
                    um, up = uprev, unext
            else:
                um, up = _shifted(u, rows, GRID_W if grid_mode else tb, tb)
            yc = um * w[0:1] + u * w[1:2] + up * w[2:3]
            zf = z_ref[0, :, sl].astype(F32)
            gval = bg_ref[0, :, sl].astype(F32) * yc * (zf * _sigmoid(zf))
            yc_ref[:, sl] = yc.astype(BF16)
            g_ref[:, sl] = gval.astype(BF16)
        f = _dot(g_ref[...], wo_ref[...])
        f_ref[...] = f.astype(BF16)
        r = DN_ALPHA * x_ref[...] + gt_ref[...] * f
        rc = r - _rowmean(r)
        rstd = lax.rsqrt(_rowmean(rc * rc) + LN_EPS)
        xh_ref[...] = rc * rstd
        rs_ref[...] = rstd

    def chunk(k):
        return pl.BlockSpec((1, tb, e), lambda i: (k, i, 0))

    n_hrows = n_tok // GRID_W

    def halo_prev(k):
        return pl.BlockSpec((1, GRID_W, eh), lambda i: (k, jnp.maximum(i * hb - 1, 0), 1))

    def halo_next(k):
        return pl.BlockSpec((1, GRID_W, eh), lambda i: (k, jnp.minimum((i + 1) * hb, n_hrows - 1), 1))

    vec = pl.BlockSpec((1, d_model), lambda i: (0, 0))
    tok = pl.BlockSpec((tb, d_model), lambda i: (i, 0))
    wide = pl.BlockSpec((tb, e), lambda i: (i, 0))
    in_specs = [chunk(0), chunk(1), chunk(2), chunk(3)]
    args = [p, p, p, p]
    if grid_mode:
        in_specs += [halo_prev(1), halo_prev(2), halo_next(1), halo_next(2)]
        args += [p, p, p, p]
    in_specs += [tok, vec, pl.BlockSpec((3, e), lambda i: (0, 0)), pl.BlockSpec((e, d_model), lambda i: (0, 0)), vec, vec]
    args += [x, gt, conv_w, w_out, ln_g, ln_b]
    return pl.pallas_call(
        body, name=name, grid=(nb,),
        out_shape=(jax.ShapeDtypeStruct((n_tok, d_model), F32), jax.ShapeDtypeStruct((n_tok, 1), F32),
                   jax.ShapeDtypeStruct((n_tok, e), BF16), jax.ShapeDtypeStruct((n_tok, e), BF16),
                   jax.ShapeDtypeStruct((n_tok, d_model), BF16)),
        in_specs=in_specs, out_specs=(tok, pl.BlockSpec((tb, 1), lambda i: (i, 0)), wide, wide, tok),
        compiler_params=_params(1),
    )(*args)


def _scan_block(buf_ref, tab_ref, cr, ci, *, reverse, tb, sb):
    n_slabs = tb // SUBLANES
    unrolled = n_slabs <= 64

    def slab(s, carry):
        cr, ci = carry
        idx = (n_slabs - 1 - s) if reverse else s
        r0 = idx * SUBLANES if unrolled else pl.multiple_of(idx * SUBLANES, SUBLANES)
        xr = buf_ref[pl.ds(r0, SUBLANES), 0:sb]
        xi = buf_ref[pl.ds(r0, SUBLANES), sb:2 * sb]
        for k, step in enumerate((1, 2, 4)):
            ar = tab_ref[2 * k]
            ai = tab_ref[2 * k + 1]
            shift = (SUBLANES - step) if reverse else step
            rr = pltpu.roll(xr, shift, 0)
            ri = pltpu.roll(xi, shift, 0)
            xr, xi = xr + ar * rr - ai * ri, xi + ar * ri + ai * rr
        pr = tab_ref[6]
        pi_ = tab_ref[7]
        xr, xi = xr + pr * cr - pi_ * ci, xi + pr * ci + pi_ * cr
        buf_ref[pl.ds(r0, SUBLANES), 0:sb] = xr
        buf_ref[pl.ds(r0, SUBLANES), sb:2 * sb] = xi
        last = 0 if reverse else SUBLANES - 1
        return (jnp.broadcast_to(xr[last:last + 1, :], (SUBLANES, sb)),
                jnp.broadcast_to(xi[last:last + 1, :], (SUBLANES, sb)))

    if unrolled:
        carry = (cr, ci)
        for s in range(n_slabs):
            carry = slab(s, carry)
        return carry
    return lax.fori_loop(0, n_slabs, slab, (cr, ci))


CHUNK = SUBLANES


def _group_mask():
    r = lax.broadcasted_iota(jnp.int32, (LANES, LANES), 0)
    c = lax.broadcasted_iota(jnp.int32, (LANES, LANES), 1)
    return r // 16 == c // 16


def _s5_ops(bre, bim, cre, cim, wr, wi, *, reverse, name, hosted=None):
    ngb, _, sb = bre.shape
    n_rows = CHUNK * LANES

    def body(bre_ref, bim_ref, cre_ref, cim_ref, wr_ref, wi_ref, t_ref, bp_ref, cp_ref):
        b_re, b_im, c_re, c_im = bre_ref[0], bim_ref[0], cre_ref[0], cim_ref[0]
        mask = _group_mask()
        er, ei = [], []
        for tau in range(CHUNK + 1):
            w_r, w_i = wr_ref[0, tau:tau + 1, :], wi_ref[0, tau:tau + 1, :]
            er.append(c_re * w_r - c_im * w_i)
            ei.append(c_re * w_i + c_im * w_r)
        kt = []
        for tau in range(CHUNK):
            k = (lax.dot_general(b_re, er[tau], (((1,), (1,)), ((), ())), precision=HIGHEST, preferred_element_type=F32)
                 - lax.dot_general(b_im, ei[tau], (((1,), (1,)), ((), ())), precision=HIGHEST, preferred_element_type=F32))
            kt.append(jnp.where(mask, k, 0.0).astype(BF16))
        zero = jnp.zeros((LANES, LANES), BF16)
        for i in range(CHUNK):
            rows = slice(i * LANES, (i + 1) * LANES)
            for j in range(CHUNK):
                lag = (i - j) if reverse else (j - i)
                t_ref[0, rows, j * LANES:(j + 1) * LANES] = kt[lag] if lag >= 0 else zero
            tau = i if reverse else CHUNK - 1 - i
            w_r, w_i = wr_ref[0, tau:tau + 1, :], wi_ref[0, tau:tau + 1, :]
            bp_ref[0, rows, 0:sb] = (b_re * w_r - b_im * w_i).astype(BF16)
            bp_ref[0, rows, sb:2 * sb] = (b_re * w_i + b_im * w_r).astype(BF16)
            tau = CHUNK - i if reverse else i + 1
            cp_ref[0, rows, 0:sb] = er[tau].astype(BF16)
            cp_ref[0, rows, sb:2 * sb] = (-ei[tau]).astype(BF16)

    mat = pl.BlockSpec((1, LANES, sb), lambda g: (g, 0, 0))
    pw = pl.BlockSpec((1, CHUNK + 1, sb), lambda g: (g, 0, 0))
    outs, extra = _call(
        body, name=name, grid=(ngb,),
        out_shape=[jax.ShapeDtypeStruct((ngb, n_rows, n_rows), BF16), jax.ShapeDtypeStruct((ngb, n_rows, 2 * sb), BF16),
                   jax.ShapeDtypeStruct((ngb, n_rows, 2 * sb), BF16)],
        in_specs=[mat, mat, mat, mat, pw, pw],
        out_specs=[pl.BlockSpec((1, n_rows, n_rows), lambda g: (g, 0, 0)),
                   pl.BlockSpec((1, n_rows, 2 * sb), lambda g: (g, 0, 0)),
                   pl.BlockSpec((1, n_rows, 2 * sb), lambda g: (g, 0, 0))],
        scratch_shapes=[], args=(bre, bim, cre, cim, wr, wi), hosted=hosted)
    return (*outs, extra)


def _s5_ops_bwd(bre, bim, cre, cim, wr, wi, ar, ai, dt, dbp, dcp, da8, *, reverse, name, hosted=None):
    ngb, _, sb = bre.shape
    n_rows = CHUNK * LANES

    def dot_hi(a, b, dims):
        return lax.dot_general(a.astype(BF16), b.astype(BF16), (dims, ((), ())), preferred_element_type=F32)

    def body(bre_ref, bim_ref, cre_ref, cim_ref, wr_ref, wi_ref, ar_ref, ai_ref, dt_ref, dbp_ref, dcp_ref, da8_ref,
             dbre_ref, dbim_ref, dcre_ref, dcim_ref, da_ref):
        b_re, b_im, c_re, c_im = bre_ref[0], bim_ref[0], cre_ref[0], cim_ref[0]
        mask = _group_mask()
        w_r = [wr_ref[0, tau:tau + 1, :] for tau in range(CHUNK + 1)]
        w_i = [wi_ref[0, tau:tau + 1, :] for tau in range(CHUNK + 1)]
        der = [jnp.zeros((LANES, sb), F32) for _ in range(CHUNK + 1)]
        dei = [jnp.zeros((LANES, sb), F32) for _ in range(CHUNK + 1)]
        dwr = [jnp.zeros((1, sb), F32) for _ in range(CHUNK + 1)]
        dwi = [jnp.zeros((1, sb), F32) for _ in range(CHUNK + 1)]
        dwr[CHUNK] = da8_ref[0, :, 0:sb]
        dwi[CHUNK] = da8_ref[0, :, sb:2 * sb]
        d_bre = jnp.zeros((LANES, sb), F32)
        d_bim = jnp.zeros((LANES, sb), F32)
        dkt = [jnp.zeros((LANES, LANES), F32) for _ in range(CHUNK)]
        for i in range(CHUNK):
            rows = slice(i * LANES, (i + 1) * LANES)
            for j in range(CHUNK):
                lag = (i - j) if reverse else (j - i)
                if lag >= 0:
                    dkt[lag] = dkt[lag] + dt_ref[0, rows, j * LANES:(j + 1) * LANES]
            tau = i if reverse else CHUNK - 1 - i
            g_r, g_i = dbp_ref[0, rows, 0:sb], dbp_ref[0, rows, sb:2 * sb]
            d_bre = d_bre + g_r * w_r[tau] + g_i * w_i[tau]
            d_bim = d_bim - g_r * w_i[tau] + g_i * w_r[tau]
            dwr[tau] = dwr[tau] + _colsum(g_r * b_re + g_i * b_im)
            dwi[tau] = dwi[tau] + _colsum(g_i * b_re - g_r * b_im)
            tau = CHUNK - i if reverse else i + 1
            der[tau] = der[tau] + dcp_ref[0, rows, 0:sb]
            dei[tau] = dei[tau] - dcp_ref[0, rows, sb:2 * sb]
        d_cre = jnp.zeros((LANES, sb), F32)
        d_cim = jnp.zeros((LANES, sb), F32)
        for tau in range(CHUNK + 1):
            if tau < CHUNK:
                e_r = c_re * w_r[tau] - c_im * w_i[tau]
                e_i = c_re * w_i[tau] + c_im * w_r[tau]
                dk = jnp.where(mask, dkt[tau], 0.0)
                d_bre = d_bre + dot_hi(dk, e_r, ((1,), (0,)))
                d_bim = d_bim - dot_hi(dk, e_i, ((1,), (0,)))
                der[tau] = der[tau] + dot_hi(dk, b_re, ((0,), (0,)))
                dei[tau] = dei[tau] - dot_hi(dk, b_im, ((0,), (0,)))
            d_cre = d_cre + der[tau] * w_r[tau] + dei[tau] * w_i[tau]
            d_cim = d_cim - der[tau] * w_i[tau] + dei[tau] * w_r[tau]
            dwr[tau] = dwr[tau] + _colsum(der[tau] * c_re + dei[tau] * c_im)
            dwi[tau] = dwi[tau] + _colsum(dei[tau] * c_re - der[tau] * c_im)
        a_r, a_i = ar_ref[0], ai_ref[0]
        d_ar = jnp.zeros((1, sb), F32)
        d_ai = jnp.zeros((1, sb), F32)
        for tau in range(CHUNK, 0, -1):
            d_ar = d_ar + dwr[tau] * w_r[tau - 1] + dwi[tau] * w_i[tau - 1]
            d_ai = d_ai - dwr[tau] * w_i[tau - 1] + dwi[tau] * w_r[tau - 1]
            dwr[tau - 1], dwi[tau - 1] = (dwr[tau - 1] + dwr[tau] * a_r + dwi[tau] * a_i,
                                          dwi[tau - 1] - dwr[tau] * a_i + dwi[tau] * a_r)
        dbre_ref[0] = d_bre
        dbim_ref[0] = d_bim
        dcre_ref[0] = d_cre
        dcim_ref[0] = d_cim
        da_ref[0, :, 0:sb] = d_ar
        da_ref[0, :, sb:2 * sb] = d_ai

    mat = pl.BlockSpec((1, LANES, sb), lambda g: (g, 0, 0))
    pw = pl.BlockSpec((1, CHUNK + 1, sb), lambda g: (g, 0, 0))
    one = pl.BlockSpec((1, 1, sb), lambda g: (g, 0, 0))
    two = pl.BlockSpec((1, 1, 2 * sb), lambda g: (g, 0, 0))
    big = pl.BlockSpec((1, n_rows, n_rows), lambda g: (g, 0, 0))
    big2 = pl.BlockSpec((1, n_rows, 2 * sb), lambda g: (g, 0, 0))
    mshape = jax.ShapeDtypeStruct((ngb, LANES, sb), F32)
    outs, extra = _call(
        body, name=name, grid=(ngb,),
        out_shape=[mshape, mshape, mshape, mshape, jax.ShapeDtypeStruct((ngb, 1, 2 * sb), F32)],
        in_specs=[mat, mat, mat, mat, pw, pw, one, one, big, big2, big2, two],
        out_specs=[mat, mat, mat, mat, two],
        scratch_shapes=[], args=(bre, bim, cre, cim, wr, wi, ar, ai, dt, dbp, dcp, da8), hosted=hosted)
    return (*outs, extra)


MXU_TILE = 256


def _causal_span(tile, n_tiles, reverse, of_output):
    upto, onward = slice(0, (tile + 1) * MXU_TILE), slice(tile * MXU_TILE, n_tiles * MXU_TILE)
    return (onward if reverse else upto) if of_output else (upto if reverse else onward)


def _apply_causal(uv, t_ref, reverse):
    n_tiles = uv.shape[1] // MXU_TILE
    cols = []
    for tj in range(n_tiles):
        span = _causal_span(tj, n_tiles, reverse, True)
        cols.append(_dot(uv[:, span], t_ref[0, span, tj * MXU_TILE:(tj + 1) * MXU_TILE]))
    return jnp.concatenate(cols, axis=1)


def _apply_causal_t(dyv, t_ref, reverse):
    n_tiles = dyv.shape[1] // MXU_TILE
    cols = []
    for ti in range(n_tiles):
        span = _causal_span(ti, n_tiles, reverse, False)
        cols.append(_dot_nt(dyv[:, span], t_ref[0, ti * MXU_TILE:(ti + 1) * MXU_TILE, span]))
    return jnp.concatenate(cols, axis=1)


def _shift_rows(xv, edge, rows, n_rows, down):
    if down:
        return jnp.where(rows == 0, edge, pltpu.roll(xv, 1, 0))
    return jnp.where(rows == n_rows - 1, edge, pltpu.roll(xv, n_rows - 1, 0))


def _rows_of_tokens(tok_ref, conv_scr, rb):
    conv_scr[...] = tok_ref[0].astype(F32)
    return jnp.concatenate([conv_scr[pl.ds(j, rb, stride=CHUNK), :] for j in range(CHUNK)], axis=1).astype(BF16)


def _tokens_of_rows(val, tok_ref, conv_scr, rb):
    for j in range(CHUNK):
        conv_scr[pl.ds(j, rb, stride=CHUNK), :] = val[:, j * LANES:(j + 1) * LANES]
    tok_ref[0] = conv_scr[...].astype(BF16)


def _s5_row_block(n_seq, target=416):
    n_rows = n_seq // CHUNK
    best = 16
    for rb in range(16, min(target, n_rows) + 1, 16):
        if n_rows % rb == 0:
            best = rb
    assert n_rows % best == 0
    return best


def _s5_fwd(useq, t_op, bp, cp, tab, *, reverse, name):
    ngb, n_seq, _ = useq.shape
    sb = bp.shape[2] // 2
    width = CHUNK * LANES
    rb = _s5_row_block(n_seq)
    tbk = rb * CHUNK
    steps = n_seq // tbk

    def blk(i):
        return (steps - 1 - i) if reverse else i

    def body(u_ref, t_ref, b_ref, c_ref, tab_ref, y_ref, hp_ref, h_scr, conv_scr, carry_scr):
        i = pl.program_id(1)

        @pl.when(i == 0)
        def _():
            carry_scr[...] = jnp.zeros_like(carry_scr)

        enter = carry_scr[0:1, :]
        uv = _rows_of_tokens(u_ref, conv_scr, rb)
        h_scr[...] = _dot(uv, b_ref[0])
        cr, ci = _scan_block(h_scr, tab_ref.at[0], carry_scr[:, 0:sb], carry_scr[:, sb:2 * sb],
                             reverse=reverse, tb=rb, sb=sb)
        carry_scr[:, 0:sb] = cr
        carry_scr[:, sb:2 * sb] = ci
        rows = lax.broadcasted_iota(jnp.int32, (rb, 1), 0)
        hprev = _shift_rows(h_scr[...], enter, rows, rb, down=not reverse)
        hp_ref[0] = hprev
        _tokens_of_rows(_apply_causal(uv, t_ref, reverse) + _dot_nt(hprev.astype(BF16), c_ref[0]), y_ref, conv_scr, rb)

    op = pl.BlockSpec((1, width, width), lambda g, i: (g, 0, 0))
    op2 = pl.BlockSpec((1, width, 2 * sb), lambda g, i: (g, 0, 0))
    tok = pl.BlockSpec((1, tbk, LANES), lambda g, i: (g, blk(i), 0))
    return pl.pallas_call(
        body, name=name, grid=(ngb, steps),
        out_shape=(jax.ShapeDtypeStruct((ngb, n_seq, LANES), BF16),
                   jax.ShapeDtypeStruct((ngb, n_seq // CHUNK, 2 * sb), F32)),
        in_specs=[tok, op, op2, op2, pl.BlockSpec((1, 8, SUBLANES, sb), lambda g, i: (g, 0, 0, 0))],
        out_specs=(tok, pl.BlockSpec((1, rb, 2 * sb), lambda g, i: (g, blk(i), 0))),
        scratch_shapes=[pltpu.VMEM((rb, 2 * sb), F32), pltpu.VMEM((tbk, LANES), F32),
                        pltpu.VMEM((SUBLANES, 2 * sb), F32)],
        compiler_params=_params(2),
    )(useq, t_op, bp, cp, tab)


def _s5_bwd(useq, dy, hprev, t_op, bp, cp, tab_adj, *, reverse, name, hosted=None):
    ngb, n_seq, _ = useq.shape
    sb = bp.shape[2] // 2
    width = CHUNK * LANES
    rb = _s5_row_block(n_seq)
    tbk = rb * CHUNK
    steps = n_seq // tbk

    def blk(i):
        return i if reverse else steps - 1 - i

    def body(u_ref, dy_ref, hp_ref, t_ref, b_ref, c_ref, taba_ref, du_ref, dt_ref, db_ref, dc_ref, da_ref,
             lam_scr, conv_scr, lcarry_scr, gedge_scr, da_scr):
        i = pl.program_id(1)
        first = i == 0

        _zero_first(first, lcarry_scr, gedge_scr, da_scr, dt_ref, db_ref, dc_ref)
        rows = lax.broadcasted_iota(jnp.int32, (rb, 1), 0)
        uv = _rows_of_tokens(u_ref, conv_scr, rb)
        dyv = _rows_of_tokens(dy_ref, conv_scr, rb)
        gy = _dot(dyv, c_ref[0])
        edge = gy[rb - 1:rb, :] if reverse else gy[0:1, :]
        lam_scr[...] = _shift_rows(gy, gedge_scr[...], rows, rb, down=reverse)
        gedge_scr[...] = edge
        lr, li = _scan_block(lam_scr, taba_ref.at[0], lcarry_scr[:, 0:sb], lcarry_scr[:, sb:2 * sb],
                             reverse=not reverse, tb=rb, sb=sb)
        lcarry_scr[:, 0:sb] = lr
        lcarry_scr[:, sb:2 * sb] = li

        lam = lam_scr[...]
        lam_bf = lam.astype(BF16)
        _tokens_of_rows(_apply_causal_t(dyv, t_ref, reverse) + _dot_nt(lam_bf, b_ref[0]), du_ref, conv_scr, rb)
        for tj in range(width // MXU_TILE):
            span, cols = _causal_span(tj, width // MXU_TILE, reverse, True), slice(tj * MXU_TILE, (tj + 1) * MXU_TILE)
            _acc(dt_ref.at[0, span, cols], _dot_tn(uv[:, span], dyv[:, cols]))
        _acc(db_ref.at[0], _dot_tn(uv, lam_bf))
        _acc(dc_ref.at[0], _dot_tn(dyv, hp_ref[0].astype(BF16)))
        lam_r, lam_i = lam[:, 0:sb], lam[:, sb:2 * sb]
        hp_r, hp_i = hp_ref[0, :, 0:sb], hp_ref[0, :, sb:2 * sb]
        da_scr[:, 0:sb] += _colsum(lam_r * hp_r + lam_i * hp_i)
        da_scr[:, sb:2 * sb] += _colsum(lam_i * hp_r - lam_r * hp_i)

        @pl.when(i == steps - 1)
        def _():
            da_ref[0] = da_scr[...]

    op = pl.BlockSpec((1, width, width), lambda g, i: (g, 0, 0))
    op2 = pl.BlockSpec((1, width, 2 * sb), lambda g, i: (g, 0, 0))
    tabs = pl.BlockSpec((1, 8, SUBLANES, sb), lambda g, i: (g, 0, 0, 0))
    tok = pl.BlockSpec((1, tbk, LANES), lambda g, i: (g, blk(i), 0))
    outs, extra = _call(
        body, name=name, grid=(ngb, steps),
        out_shape=[jax.ShapeDtypeStruct((ngb, n_seq, LANES), BF16),
                   jax.ShapeDtypeStruct((ngb, width, width), F32),
                   jax.ShapeDtypeStruct((ngb, width, 2 * sb), F32),
                   jax.ShapeDtypeStruct((ngb, width, 2 * sb), F32),
                   jax.ShapeDtypeStruct((ngb, 1, 2 * sb), F32)],
        in_specs=[tok, tok, pl.BlockSpec((1, rb, 2 * sb), lambda g, i: (g, blk(i), 0)), op, op2, op2, tabs],
        out_specs=[tok, op, op2, op2, pl.BlockSpec((1, 1, 2 * sb), lambda g, i: (g, 0, 0))],
        scratch_shapes=[pltpu.VMEM((rb, 2 * sb), F32), pltpu.VMEM((tbk, LANES), F32),
                        pltpu.VMEM((SUBLANES, 2 * sb), F32), pltpu.VMEM((1, 2 * sb), F32), pltpu.VMEM((1, 2 * sb), F32)],
        args=(useq, dy, hprev, t_op, bp, cp, tab_adj), hosted=hosted)
    return (*outs, extra)


def _glu_loss(useq, yf, yb, z, xhat0, ln0, gt, d_vec, w_glu, b_glu, w_out, ln1, target, *, offs, dy_rows, tb, name):
    ngb = useq.shape[0]
    n_tok, d_model = xhat0.shape
    e = ngb * LANES
    tb = min(tb, n_tok)
    assert all(off % tb == 0 for off in offs) and all(off % tb == 0 for _, off in dy_rows)
    nz = z.shape[0]

    def body(u_ref, yf_ref, yb_ref, z_ref, xh0_ref, g0_ref, b0_ref, gt_ref, d_ref, wg_ref, bg_ref, wo_ref, g1_ref,
             b1_ref, t_ref, loss_ref, dxr_ref, do_ref, gz_ref, gg_ref, dq_ref, dz_ref, dyf_ref, dyb_ref, dg1_ref, db1_ref,
             dgt_ref, dbg_ref, dd_ref, loss_scr, yl_scr, th_scr, s_scr, dg_scr):
        i = pl.program_id(0)
        _zero_first(i == 0, loss_scr, dg1_ref, db1_ref, dgt_ref, dbg_ref, dd_ref)
        zw = e // nz
        cs = min(512, zw)

        def z_slab(c0):
            return z_ref[c0 // zw, :, c0 % zw:c0 % zw + cs].astype(F32)

        for q in range(ngb):
            sl = slice(q * LANES, (q + 1) * LANES)
            yl = d_ref[:, sl] * u_ref[q].astype(F32) + yf_ref[q].astype(F32) + yb_ref[q].astype(F32)
            th = jnp.tanh(GELU_K * (yl + GELU_C * yl * yl * yl))
            yl_scr[:, sl] = yl
            th_scr[:, sl] = th
            gg_ref[:, sl] = (0.5 * yl * (1.0 + th)).astype(BF16)
        g_all = gg_ref[...]
        for c0 in range(0, e, cs):
            sl = slice(c0, c0 + cs)
            s = _sigmoid(_dot(g_all, wg_ref[:, sl]) + bg_ref[:, sl])
            s_scr[:, sl] = s
            zf = z_slab(c0)
            g2 = 0.5 * yl_scr[:, sl] * (1.0 + th_scr[:, sl]) * s
            gz_ref[:, sl] = (g2 * (zf * _sigmoid(zf))).astype(BF16)
        o = _dot(gz_ref[...], wo_ref[...])
        x1 = xh0_ref[...] * g0_ref[...] + b0_ref[...]
        r = DN_ALPHA * x1 + gt_ref[...] * o
        rc = r - _rowmean(r)
        rstd = lax.rsqrt(_rowmean(rc * rc) + LN_EPS)
        xh = rc * rstd
        err = xh * g1_ref[...] + b1_ref[...] - t_ref[...]
        _acc(loss_scr, _colsum(err * err))
        dy = err * (1.0 / d_model)
        _acc(dg1_ref, _colsum(dy * xh))
        _acc(db1_ref, _colsum(dy))
        dxh = dy * g1_ref[...]
        dr = rstd * (dxh - _rowmean(dxh) - xh * _rowmean(dxh * xh))
        dxr_ref[...] = DN_ALPHA * dr
        _acc(dgt_ref, _colsum(dr * o))
        do_bf = (dr * gt_ref[...]).astype(BF16)
        do_ref[...] = do_bf
        for c0 in range(0, e, cs):
            sl = slice(c0, c0 + cs)
            dgz = _dot_nt(do_bf, wo_ref[sl, :])
            zf = z_slab(c0)
            sz = _sigmoid(zf)
            g = 0.5 * yl_scr[:, sl] * (1.0 + th_scr[:, sl])
            s = s_scr[:, sl]
            dg2 = dgz * (zf * sz)
            dz_ref[:, sl] = (dgz * (g * s) * (sz * (1.0 + zf * (1.0 - sz)))).astype(BF16)
            dq = dg2 * g * s * (1.0 - s)
            _acc(dbg_ref.at[:, sl], _colsum(dq))
            dq_ref[:, sl] = dq.astype(BF16)
            dg_scr[:, sl] = dg2 * s
        dq_all = dq_ref[...]
        for c0 in range(0, e, cs):
            dg_blk = dg_scr[:, c0:c0 + cs] + _dot_nt(dq_all, wg_ref[c0:c0 + cs, :])
            for q in range(c0 // LANES, (c0 + cs) // LANES):
                sl = slice(q * LANES, (q + 1) * LANES)
                yl = yl_scr[:, sl]
                th = th_scr[:, sl]
                dgelu = 0.5 * (1.0 + th) + 0.5 * yl * (1.0 - th * th) * (GELU_K * (1.0 + 3.0 * GELU_C * yl * yl))
                dyl = dg_blk[:, q * LANES - c0:(q + 1) * LANES - c0] * dgelu
                _acc(dd_ref.at[:, sl], _colsum(dyl * u_ref[q].astype(F32)))
                dyf_ref[q] = dyl.astype(BF16)
                dyb_ref[q] = dyl.astype(BF16)

        @pl.when(i == pl.num_programs(0) - 1)
        def _():
            loss_ref[...] = (0.5 / d_model) * jnp.sum(loss_scr[...], axis=1, keepdims=True)

    vec = pl.BlockSpec((1, d_model), lambda i: (0, 0))
    evec = pl.BlockSpec((1, e), lambda i: (0, 0))
    tok = pl.BlockSpec((tb, d_model), lambda i: (i, 0))
    wide = pl.BlockSpec((tb, e), lambda i: (i, 0))
    def gblk(off):
        return pl.BlockSpec((ngb, tb, LANES), functools.partial(lambda i, ob: (0, i + ob, 0), ob=off // tb))

    once = dict(pipeline_mode=pl.Buffered(1))
    tok_f = jax.ShapeDtypeStruct((n_tok, d_model), F32)
    tok_b = jax.ShapeDtypeStruct((n_tok, d_model), BF16)
    wide_b = jax.ShapeDtypeStruct((n_tok, e), BF16)
    vec_f = jax.ShapeDtypeStruct((1, d_model), F32)
    evec_f = jax.ShapeDtypeStruct((1, e), F32)
    return pl.pallas_call(
        body, name=name, grid=(n_tok // tb,),
        out_shape=(jax.ShapeDtypeStruct((1, 1), F32), tok_f, tok_b, wide_b, wide_b, wide_b, wide_b,
                   *[jax.ShapeDtypeStruct((ngb, total, LANES), BF16) for total, _ in dy_rows],
                   vec_f, vec_f, vec_f, evec_f, evec_f),
        in_specs=[gblk(offs[0]), gblk(offs[1]), gblk(offs[2]),
                  pl.BlockSpec((nz, tb, e // nz), lambda i: (0, i, 0)), tok, vec, vec, vec, evec,
                  pl.BlockSpec((e, e), lambda i: (0, 0), **once), evec,
                  pl.BlockSpec((e, d_model), lambda i: (0, 0), **once), vec, vec, tok],
        out_specs=(pl.BlockSpec((1, 1), lambda i: (0, 0)), tok, tok, wide, wide, wide, wide,
                   *[gblk(off) for _, off in dy_rows], vec, vec, vec, evec, evec),
        scratch_shapes=[pltpu.VMEM((1, d_model), F32)] + [pltpu.VMEM((tb, e), F32)] * 4,
        compiler_params=_params(1),
    )(useq, yf, yb, z, xhat0, ln0[0], ln0[1], gt, d_vec, w_glu, b_glu, w_out, ln1[0], ln1[1], target)


def _ssm_inbwd(duf, dub, w, xhat, rstd, ln, sc, gt_prev, f_prev, *, lat, row_f, row_b, tb, name):
    ngb = duf.shape[0]
    e = ngb * LANES
    n_tok, d_model = xhat.shape
    tb = min(tb, n_tok)
    obf, obb = row_f // tb, row_b // tb
    has_lat = lat is not None
    n_w = w.shape[0] if has_lat else w.shape[0] // 2

    def body(*refs):
        if has_lat:
            (duf_ref, dub_ref, dyl_ref, dz_ref, d_ref, dxr_ref, w_ref, xh_ref, rs_ref, g_ref, b_ref, sc_ref, gt_ref,
             f_ref, dp_ref, dr_ref, df_ref, dsc_ref, dsh_ref, dg_ref, db_ref, dgt_ref) = refs
        else:
            (duf_ref, dub_ref, w_ref, xh_ref, rs_ref, g_ref, b_ref, sc_ref, gt_ref, f_ref, dp_ref, dr_ref, df_ref,
             dsc_ref, dsh_ref, dg_ref, db_ref, dgt_ref) = refs
        _zero_first(pl.program_id(0) == 0, dsc_ref, dsh_ref, dg_ref, db_ref, dgt_ref)
        du = (jnp.concatenate([duf_ref[q] for q in range(ngb)], axis=1).astype(F32)
              + jnp.concatenate([dub_ref[q] for q in range(ngb)], axis=1).astype(F32))
        if has_lat:
            du = du + d_ref[...] * jnp.concatenate([dyl_ref[q] for q in range(ngb)], axis=1).astype(F32)
            dp_ref[:, e:2 * e] = dz_ref[...]
        else:
            dp_ref[:, e:2 * e] = jnp.zeros((tb, e), BF16)
        dp_ref[:, 0:e] = du.astype(BF16)
        dh = jnp.zeros((tb, d_model), F32)
        for j in range(n_w):
            dh = dh + _dot(dp_ref[:, j * d_model:(j + 1) * d_model], w_ref[j])
        xh = xh_ref[...]
        x1 = xh * g_ref[...] + b_ref[...]
        dx1 = dh * (1.0 + sc_ref[...])
        if has_lat:
            dx1 = dx1 + dxr_ref[...]
        _acc(dsc_ref, _colsum(dh * x1))
        _acc(dsh_ref, _colsum(dh))
        _acc(dg_ref, _colsum(dx1 * xh))
        _acc(db_ref, _colsum(dx1))
        dxh = dx1 * g_ref[...]
        dr = rs_ref[...] * (dxh - _rowmean(dxh) - xh * _rowmean(dxh * xh))
        dr_ref[...] = dr
        df_ref[...] = (dr * gt_ref[...]).astype(BF16)
        _acc(dgt_ref, _colsum(dr * f_ref[...].astype(F32)))

    vec = pl.BlockSpec((1, d_model), lambda i: (0, 0))
    tok = pl.BlockSpec((tb, d_model), lambda i: (i, 0))
    gblk = pl.BlockSpec((ngb, tb, LANES), lambda i: (0, i, 0))
    in_specs = [pl.BlockSpec((ngb, tb, LANES), lambda i: (0, i + obf, 0)),
                pl.BlockSpec((ngb, tb, LANES), lambda i: (0, i + obb, 0))]
    args = [duf, dub]
    if has_lat:
        in_specs += [gblk, pl.BlockSpec((tb, e), lambda i: (i, 0)), pl.BlockSpec((1, e), lambda i: (0, 0)), tok]
        args += list(lat)
    in_specs += [pl.BlockSpec(w.shape, lambda i: (0, 0, 0)), tok, pl.BlockSpec((tb, 1), lambda i: (i, 0)), vec, vec, vec,
                 vec, tok]
    args += [w, xhat, rstd, ln[0], ln[1], sc, gt_prev, f_prev]
    vec_f = jax.ShapeDtypeStruct((1, d_model), F32)
    return pl.pallas_call(
        body, name=name, grid=(n_tok // tb,),
        out_shape=(jax.ShapeDtypeStruct((n_tok, 2 * e), BF16), jax.ShapeDtypeStruct((n_tok, d_model), F32),
                   jax.ShapeDtypeStruct((n_tok, d_model), BF16), vec_f, vec_f, vec_f, vec_f, vec_f),
        in_specs=in_specs,
        out_specs=(pl.BlockSpec((tb, 2 * e), lambda i: (i, 0)), tok, tok, vec, vec, vec, vec, vec),
        compiler_params=_params(1),
    )(*args)


def _conv_bwd_a(df, w_out_t, p, yc, *, tb, name):
    _, n_tok, e = p.shape
    d_model = df.shape[1]
    tb = min(tb, n_tok)
    cs = _slab_width(e)

    def body(df_ref, wo_ref, bg_ref, z_ref, yc_ref, dbg_ref, dz_ref, dyc_ref):
        dfv = df_ref[...]
        for c0 in range(0, e, cs):
            sl = slice(c0, c0 + cs)
            dgv = _dot(dfv, wo_ref[:, sl])
            zf = z_ref[0, :, sl].astype(F32)
            sz = _sigmoid(zf)
            silu_z = zf * sz
            bg = bg_ref[0, :, sl].astype(F32)
            yc = yc_ref[:, sl].astype(F32)
            dbg_ref[:, sl] = (dgv * yc * silu_z).astype(BF16)
            dyc_ref[:, sl] = (dgv * bg * silu_z).astype(BF16)
            dz_ref[:, sl] = (dgv * bg * yc * (sz * (1.0 + zf * (1.0 - sz)))).astype(BF16)

    wide = pl.BlockSpec((tb, e), lambda i: (i, 0))
    shape = jax.ShapeDtypeStruct((n_tok, e), BF16)
    return pl.pallas_call(
        body, name=name, grid=(n_tok // tb,), out_shape=(shape, shape, shape),
        in_specs=[pl.BlockSpec((tb, d_model), lambda i: (i, 0)), pl.BlockSpec((d_model, e), lambda i: (0, 0)),
                  pl.BlockSpec((1, tb, e), lambda i: (0, i, 0)), pl.BlockSpec((1, tb, e), lambda i: (3, i, 0)), wide],
        out_specs=(wide, wide, wide), compiler_params=_params(1),
    )(df, w_out_t, p, p, yc)


def _conv_bwd_b(dyc, p, dbg, dz, conv_w, *, grid_mode, tb, name, hosted=None):
    _, n_tok, e = p.shape
    eh = e // 2
    if not grid_mode:
        tb = n_tok
    tb = min(tb, n_tok)
    nb = n_tok // tb
    hb = tb // GRID_W
    cs = _slab_width(e)

    def body(*refs):
        if grid_mode:
            dyc_ref, dycp_ref, dycn_ref, cg_ref, v_ref, dbg_ref, dz_ref, cw_ref, dp_ref, dcw_ref = refs
        else:
            dyc_ref, cg_ref, v_ref, dbg_ref, dz_ref, cw_ref, dp_ref, dcw_ref = refs
        i = pl.program_id(0)
        _zero_first(i == 0, dcw_ref)
        rows = lax.broadcasted_iota(jnp.int32, (tb, 1), 0)
        dp_ref[0] = dbg_ref[...]
        dp_ref[3] = dz_ref[...]
        for c0 in range(0, e, cs):
            sl = slice(c0, c0 + cs)
            dyc = dyc_ref[:, sl].astype(F32)
            w = cw_ref[:, sl]
            if grid_mode and c0 >= eh:
                hs = slice(c0 - eh, c0 - eh + cs)
                dprev = jnp.where(i > 0, dycp_ref[:, hs].astype(F32), 0.0)
                dnext = jnp.where(i < nb - 1, dycn_ref[:, hs].astype(F32), 0.0)
                if tb > GRID_W:
                    dm = jnp.concatenate([dprev, dyc[:tb - GRID_W]], axis=0)
                    dpl = jnp.concatenate([dyc[GRID_W:], dnext], axis=0)
                else:
                    dm, dpl = dprev, dnext
            else:
                dm, dpl = _shifted(dyc, rows, GRID_W if grid_mode else tb, tb)
            cg = cg_ref[0, :, sl].astype(F32)
            v = v_ref[0, :, sl].astype(F32)
            u = cg * v
            du = w[0:1] * dpl + w[1:2] * dyc + w[2:3] * dm
            dp_ref[1, :, sl] = (du * v).astype(BF16)
            dp_ref[2, :, sl] = (du * cg).astype(BF16)
            _acc(dcw_ref.at[:, sl], jnp.concatenate([_colsum(u * dpl), _colsum(u * dyc), _colsum(u * dm)], axis=0))

    n_hrows = n_tok // GRID_W
    wide = pl.BlockSpec((tb, e), lambda i: (i, 0))
    in_specs = [wide]
    args = [dyc]
    if grid_mode:
        in_specs += [pl.BlockSpec((GRID_W, eh), lambda i: (jnp.maximum(i * hb - 1, 0), 1)),
                     pl.BlockSpec((GRID_W, eh), lambda i: (jnp.minimum((i + 1) * hb, n_hrows - 1), 1))]
        args += [dyc, dyc]
    in_specs += [pl.BlockSpec((1, tb, e), lambda i: (1, i, 0)), pl.BlockSpec((1, tb, e), lambda i: (2, i, 0)), wide, wide,
                 pl.BlockSpec((3, e), lambda i: (0, 0))]
    args += [p, p, dbg, dz, conv_w]
    outs, extra = _call(
        body, name=name, grid=(nb,),
        out_shape=[jax.ShapeDtypeStruct((4, n_tok, e), BF16), jax.ShapeDtypeStruct((3, e), F32)],
        in_specs=in_specs,
        out_specs=[pl.BlockSpec((4, tb, e), lambda i: (0, i, 0)), pl.BlockSpec((3, e), lambda i: (0, 0))],
        scratch_shapes=[], args=args, hosted=hosted)
    return (*outs, extra)


def _conv_inbwd(dp, w, dr, x, sc, *, tb, name, hosted=None):
    n_chunks, n_tok, e = dp.shape
    d_model = x.shape[1]
    tb = min(tb, n_tok)

    def body(dp_ref, w_ref, dr_ref, x_ref, sc_ref, gx_ref, dsc_ref, dsh_ref):
        _zero_first(pl.program_id(0) == 0, dsc_ref, dsh_ref)
        dh = _dot(dp_ref[0], w_ref[0])
        for k in range(1, n_chunks):
            dh = dh + _dot(dp_ref[k], w_ref[k])
        gx_ref[...] = DN_ALPHA * dr_ref[...] + dh * (1.0 + sc_ref[...])
        _acc(dsc_ref, _colsum(dh * x_ref[...]))
        _acc(dsh_ref, _colsum(dh))

    vec = pl.BlockSpec((1, d_model), lambda i: (0, 0))
    tok = pl.BlockSpec((tb, d_model), lambda i: (i, 0))
    vec_f = jax.ShapeDtypeStruct((1, d_model), F32)
    outs, extra = _call(
        body, name=name, grid=(n_tok // tb,),
        out_shape=[jax.ShapeDtypeStruct((n_tok, d_model), F32), vec_f, vec_f],
        in_specs=[pl.BlockSpec((n_chunks, tb, e), lambda i: (0, i, 0)),
                  pl.BlockSpec((n_chunks, e, d_model), lambda i: (0, 0, 0), pipeline_mode=pl.Buffered(1)),
                  tok, tok, vec],
        out_specs=[tok, vec, vec],
        scratch_shapes=[], args=(dp, w, dr, x, sc), hosted=hosted)
    return (*outs, extra)


def _wgrad(a, b, *, n_chunks, tm, tl, init=None, name):
    n_tok, m = a.shape
    tl = min(tl, n_tok)
    chunked = b.ndim == 3
    cw = b.shape[2] if chunked else b.shape[1] // n_chunks
    has_init = init is not None

    def body(*refs):
        if has_init:
            a_ref, b_ref, init_ref, o_ref = refs
        else:
            a_ref, b_ref, o_ref = refs
        @pl.when(pl.program_id(2) == 0)
        def _():
            o_ref[0] = init_ref[0] if has_init else jnp.zeros_like(o_ref[0])

        o_ref[0] += _dot_tn(a_ref[...], b_ref[0] if chunked else b_ref[...])

    o_spec = pl.BlockSpec((1, tm, cw), lambda jm, jc, l: (jc, jm, 0))
    b_spec = (pl.BlockSpec((1, tl, cw), lambda jm, jc, l: (jc, l, 0)) if chunked
              else pl.BlockSpec((tl, cw), lambda jm, jc, l: (l, jc)))
    init_spec = pl.BlockSpec((1, tm, cw), lambda jm, jc, l: (jc, jm, 0), pipeline_mode=pl.Buffered(1))
    in_specs = [pl.BlockSpec((tl, tm), lambda jm, jc, l: (l, jm)), b_spec] + ([init_spec] if has_init else [])
    args = (a, b) + ((init,) if has_init else ())
    return pl.pallas_call(
        body, name=name, grid=(m // tm, n_chunks, n_tok // tl),
        out_shape=jax.ShapeDtypeStruct((n_chunks, m, cw), F32),
        in_specs=in_specs, out_specs=o_spec, compiler_params=_params(3),
    )(*args)


def _block_diag(t, ngb):
    g, p, n = t.shape
    gpb = g // ngb
    eye = jnp.eye(gpb, dtype=t.dtype)
    return jnp.einsum("bgpn,gh->bgphn", t.reshape(ngb, gpb, p, n), eye).reshape(ngb, gpb * p, gpb * n)


def _block_diag_t(mat, g, p, n):
    ngb = mat.shape[0]
    gpb = g // ngb
    eye = jnp.eye(gpb, dtype=mat.dtype)
    return jnp.einsum("bgphn,gh->bgpn", mat.reshape(ngb, gpb, p, gpb, n), eye).reshape(g, p, n)


def _scan_tables(pw_r, pw_i, ngb, reverse):
    _, g, n = pw_r.shape
    sb = g * n // ngb
    rows = jnp.arange(SUBLANES)
    kinds = []
    for step in (1, 2, 4):
        mask = ((rows < SUBLANES - step) if reverse else (rows >= step)).astype(F32)
        for part in (pw_r[step - 1], pw_i[step - 1]):
            kinds.append(part.reshape(ngb, 1, sb) * mask[None, :, None])
    for part in (pw_r, pw_i):
        pw = part[::-1] if reverse else part
        kinds.append(jnp.transpose(pw.reshape(SUBLANES, ngb, sb), (1, 0, 2)))
    return jnp.stack(kinds, axis=1)


def _flat(parts):
    return jnp.concatenate([p.reshape(-1) for p in parts])


def _unflat(vec, shapes):
    out, off = [], 0
    for s in shapes:
        size = math.prod(s)
        out.append(vec[off:off + size].reshape(s))
        off += size
    return out


def kernel(x, c, ctx, c_ctx, ada_w, ada_b, ln_g, ln_b, conv_w_in, conv_w, conv_w_out, ssm_w_in, ssm_lam_re, ssm_lam_im, ssm_log_step, ssm_b_re, ssm_b_im, ssm_c_re, ssm_c_im, ssm_d, ssm_w_glu, ssm_b_glu, ssm_w_out, loss_target, m_c_ctx, m_ada_w, m_ada_b, m_ln_g, m_ln_b, m_conv_w_in, m_conv_w, m_conv_w_out, m_ssm_w_in, m_ssm_lam_re, m_ssm_lam_im, m_ssm_log_step, m_ssm_b_re, m_ssm_b_im, m_ssm_c_re, m_ssm_c_im, m_ssm_d, m_ssm_w_glu, m_ssm_b_glu, m_ssm_w_out, v_c_ctx, v_ada_w, v_ada_b, v_ln_g, v_ln_b, v_conv_w_in, v_conv_w, v_conv_w_out, v_ssm_w_in, v_ssm_lam_re, v_ssm_lam_im, v_ssm_log_step, v_ssm_b_re, v_ssm_b_im, v_ssm_c_re, v_ssm_c_im, v_ssm_d, v_ssm_w_glu, v_ssm_b_glu, v_ssm_w_out):
    weights = dict(c_ctx=c_ctx, ada_w=ada_w, ada_b=ada_b, ln_g=ln_g, ln_b=ln_b, conv_w_in=conv_w_in, conv_w=conv_w,
                   conv_w_out=conv_w_out, ssm_w_in=ssm_w_in, ssm_lam_re=ssm_lam_re, ssm_lam_im=ssm_lam_im,
                   ssm_log_step=ssm_log_step, ssm_b_re=ssm_b_re, ssm_b_im=ssm_b_im, ssm_c_re=ssm_c_re,
                   ssm_c_im=ssm_c_im, ssm_d=ssm_d, ssm_w_glu=ssm_w_glu, ssm_b_glu=ssm_b_glu, ssm_w_out=ssm_w_out)
    mom_m = dict(c_ctx=m_c_ctx, ada_w=m_ada_w, ada_b=m_ada_b, ln_g=m_ln_g, ln_b=m_ln_b, conv_w_in=m_conv_w_in,
                 conv_w=m_conv_w, conv_w_out=m_conv_w_out, ssm_w_in=m_ssm_w_in, ssm_lam_re=m_ssm_lam_re,
                 ssm_lam_im=m_ssm_lam_im, ssm_log_step=m_ssm_log_step, ssm_b_re=m_ssm_b_re, ssm_b_im=m_ssm_b_im,
                 ssm_c_re=m_ssm_c_re, ssm_c_im=m_ssm_c_im, ssm_d=m_ssm_d, ssm_w_glu=m_ssm_w_glu,
                 ssm_b_glu=m_ssm_b_glu, ssm_w_out=m_ssm_w_out)
    mom_v = dict(c_ctx=v_c_ctx, ada_w=v_ada_w, ada_b=v_ada_b, ln_g=v_ln_g, ln_b=v_ln_b, conv_w_in=v_conv_w_in,
                 conv_w=v_conv_w, conv_w_out=v_conv_w_out, ssm_w_in=v_ssm_w_in, ssm_lam_re=v_ssm_lam_re,
                 ssm_lam_im=v_ssm_lam_im, ssm_log_step=v_ssm_log_step, ssm_b_re=v_ssm_b_re, ssm_b_im=v_ssm_b_im,
                 ssm_c_re=v_ssm_c_re, ssm_c_im=v_ssm_c_im, ssm_d=v_ssm_d, ssm_w_glu=v_ssm_w_glu,
                 ssm_b_glu=v_ssm_b_glu, ssm_w_out=v_ssm_w_out)
    names = list(weights)

    n_lat, d_model = x.shape[1], x.shape[2]
    n_ctx = ctx.shape[1]
    e = 2 * d_model
    n_grp, n_state, grp = ssm_lam_re.shape[2], ssm_lam_re.shape[3], ssm_b_re.shape[4]
    ngb = e // LANES
    ws = ada_w.shape[2]
    tb_tok = min(512, n_lat)
    n_seq = n_ctx + n_lat
    tb_glu = math.gcd(256, n_ctx)
    chip = 2 * lax.axis_index("x") + lax.axis_index("y")
    me = 2 * chip + lax.axis_index("c")
    chips, everyone, pair = ("x", "y"), MESH_AXES, ("c",)

    x2, ctx2, tgt2 = x[0], ctx[0], loss_target[0]

    wc_in_own = conv_w_in[0].astype(BF16)
    later_weights = _Hosted([(w[0].astype(BF16), chips, False) for w in (conv_w_out, ssm_w_in, ssm_w_glu, ssm_w_out)])
    small_full = _exchange(_flat([conv_w[0], ssm_d[0], ssm_b_glu[0]]).reshape(1, -1), chips, False, "ag_small")
    es = conv_w.shape[2]
    conv_w_full = jnp.transpose(small_full[:, 0, :3 * es].reshape(4, 3, es), (1, 0, 2)).reshape(3, e)
    d_full = small_full[:, 0, 3 * es:4 * es].reshape(1, e)
    b_glu_full = small_full[:, 0, 4 * es:5 * es].reshape(1, e)

    c_all = _exchange(c, everyone, False, "ag_c").reshape(8, d_model)
    cc2 = c_ctx.reshape(1, d_model)
    b_sh = lax.dynamic_slice_in_dim(ada_b, chip * ws, ws, axis=1).reshape(DEPTH, 1, ws)
    m_sh = _ada_fwd(c_all, cc2, ada_w, b_sh)
    m_all = _exchange(m_sh, chips, False, "ag_mod")
    m_full = jnp.transpose(m_all, (1, 2, 0, 3)).reshape(DEPTH, 16, 3 * d_model)
    m_lat = lax.dynamic_slice_in_dim(m_full, me, 1, axis=1)
    m_ctx = m_full[:, 8:9]

    def mods(m, i):
        return m[i, :, 0:d_model], m[i, :, d_model:2 * d_model], m[i, :, 2 * d_model:3 * d_model]

    sh0, sc0, gt0 = mods(m_lat, 0)
    sh1, sc1, gt1 = mods(m_lat, 1)
    shc0, scc0, gtc0 = mods(m_ctx, 0)
    shc1, scc1, _ = mods(m_ctx, 1)
    ln0 = (ln_g[0:1], ln_b[0:1])
    ln1 = (ln_g[1:2], ln_b[1:2])

    def lam_view(t):
        return jnp.transpose(t[0], (0, 2, 1)).reshape(2 * n_state, n_grp)

    def lam_back(t):
        return jnp.transpose(t.reshape(2, n_state, n_grp), (0, 2, 1)).reshape(ssm_lam_re.shape)

    def b_view(t):
        return jnp.transpose(t[0], (0, 2, 3, 1)).reshape(2 * n_state * grp, n_grp)

    def b_back(t):
        return jnp.transpose(t.reshape(2, n_state, grp, n_grp), (0, 3, 1, 2)).reshape(ssm_b_re.shape)

    def c_view(t):
        return jnp.transpose(t[0], (0, 2, 3, 1)).reshape(2 * grp * n_state, n_grp)

    def c_back(t):
        return jnp.transpose(t.reshape(2, grp, n_state, n_grp), (0, 3, 1, 2)).reshape(ssm_c_re.shape)

    def channel_major(t):
        return jnp.transpose(t.reshape(2 * n_state, grp, n_grp), (1, 0, 2))

    def by_group(t):
        return jnp.transpose(t.reshape(t.shape[0], 2, n_state, n_grp), (0, 1, 3, 2))

    lam_re2, lam_im2, log_step2 = lam_view(ssm_lam_re), lam_view(ssm_lam_im), ssm_log_step[0]
    b_re_t, b_im_t = channel_major(b_view(ssm_b_re)), channel_major(b_view(ssm_b_im))
    pw_r, pw_i, pq_r, pq_i, bbr, bbi = _zoh_fwd(lam_re2, lam_im2, log_step2, b_re_t, b_im_t)
    sbk = n_grp * n_state // ngb
    pw_r, pw_i, pq_r, pq_i = (by_group(t) for t in (pw_r, pw_i, pq_r, pq_i))
    bbr_g = jnp.transpose(by_group(bbr), (1, 2, 0, 3))
    bbi_g = jnp.transpose(by_group(bbi), (1, 2, 0, 3))

    def power_rows(pw, r, first):
        full = jnp.concatenate([jnp.full((1, n_grp, n_state), first, F32), pw[:, r]], axis=0)
        return jnp.transpose(full.reshape(CHUNK + 1, ngb, sbk), (1, 0, 2))

    s5 = []
    for r in range(2):
        prm = dict(bre=_block_diag(bbr_g[r], ngb), bim=_block_diag(bbi_g[r], ngb),
                   cre=_block_diag(ssm_c_re[0, r], ngb), cim=_block_diag(ssm_c_im[0, r], ngb),
                   wr=power_rows(pw_r, r, 1.0), wi=power_rows(pw_i, r, 0.0))
        half_rows = wc_in_own[r * (d_model // 2):(r + 1) * (d_model // 2)]
        t_op, bp_op, cp_op, (wc_in_half,) = _s5_ops(
            prm["bre"], prm["bim"], prm["cre"], prm["cim"], prm["wr"], prm["wi"], reverse=(r == 1),
            name=f"l1_s5_ops{r}", hosted=_Hosted([(half_rows, chips, False)]))
        s5.append(dict(
            prm, t=t_op, bp=bp_op, cp=cp_op, wc_in_half=wc_in_half,
            tab=_scan_tables(pq_r[:, r], pq_i[:, r], ngb, reverse=(r == 1)),
            tab_adj=_scan_tables(pq_r[:, r], -pq_i[:, r], ngb, reverse=(r == 0))))
    wc_in = jnp.concatenate([s5[0]["wc_in_half"], s5[1]["wc_in_half"]], axis=1)

    p0, h0, gathered = _inproj(x2, sc0, sh0, wc_in, tb=min(1024, n_lat), name="l0_inproj", hosted=later_weights)
    wc_out, ws_in, w_glu, ws_out = gathered
    wc_out, w_glu, ws_out = wc_out.reshape(e, d_model), w_glu.reshape(e, e), ws_out.reshape(e, d_model)
    wc_in_t, ws_in_t, wc_out_t = jnp.transpose(wc_in, (0, 2, 1)), jnp.transpose(ws_in, (0, 2, 1)), wc_out.T
    pc0, hc0 = _inproj(ctx2, scc0, shc0, wc_in, tb=tb_tok, name="l0_inproj_ctx")
    xhat0, rstd0, g0, yc0, f0 = _convgate(p0, x2, gt0, conv_w_full, wc_out, *ln0, grid_mode=True, tb=tb_tok, name="l0_conv")
    chat0, crstd0, gc0, ycc0, fc0 = _convgate(pc0, ctx2, gtc0, conv_w_full, wc_out, *ln0, grid_mode=False, tb=tb_tok,
                                              name="l0_conv_ctx")

    seq_rows = [(n_seq, n_ctx), (n_seq, 0)]
    useq_f, useq_b, h1 = _inproj_seq(xhat0, sc1, sh1, ws_in[0:2], ln0, tb=min(1024, n_lat), seq_rows=seq_rows,
                                     name="l1_inproj_u")
    z1, _ = _inproj(xhat0, sc1, sh1, ws_in[2:4], lnaff=ln0, tb=min(1024, n_lat), name="l1_inproj_z")
    uc, hc1 = _inproj(chat0, scc1, shc1, ws_in[0:2], lnaff=ln0, tb=tb_tok, gb_rows=[(n_ctx, 0)], name="l1_inproj_ctx")
    useq = [useq_f.at[:, 0:n_ctx].set(uc), useq_b.at[:, n_lat:].set(uc)]
    y_dir, hp_dir = [], []
    for r in range(2):
        yr, hcr = _s5_fwd(useq[r], s5[r]["t"], s5[r]["bp"], s5[r]["cp"], s5[r]["tab"], reverse=(r == 1),
                          name=f"l1_s5_fwd{r}")
        y_dir.append(yr)
        hp_dir.append(hcr)

    (loss, dxres, do1, gz1, gg1, dq1, dz1, dy_f, dy_b, dg1, db1, dgt1, dbglu, dd) = _glu_loss(
        useq[0], y_dir[0], y_dir[1], z1, xhat0, ln0, gt1, d_full, w_glu, b_glu_full, ws_out, ln1, tgt2,
        offs=(n_ctx, n_ctx, 0), dy_rows=seq_rows, tb=tb_glu, name="l1_glu_loss")
    no_dy = jnp.zeros((ngb, n_ctx, LANES), BF16)
    dy_dir = [dy_f.at[:, 0:n_ctx].set(no_dy), dy_b.at[:, n_lat:].set(no_dy)]

    tl = min(1024, n_lat)

    def owner_slices(name, full):
        w = weights[name]
        return full.reshape(8, math.prod(w.shape[:-1]) // 2, w.shape[-1])

    def scatter(named):
        return _Hosted([(owner_slices(name, full), everyone, True) for name, full in named])

    def siblings(names):
        return _Hosted([(_sum_parts(rs_parts[name], "sum_" + name), pair, False) for name in names])

    rs_parts, both_halves = {}, {}

    gw_glu = _wgrad(gg1, dq1, n_chunks=1, tm=e // 2, tl=min(2 * tl, n_lat), name="wg_glu")
    gw_ssm_out = _wgrad(gz1, do1, n_chunks=1, tm=e, tl=min(2 * tl, n_lat), name="wg_ssm_out")
    du_dir, s5_grads = [], []
    for r in range(2):
        if r == 0:
            hosted = scatter([("ssm_w_glu", gw_glu), ("ssm_w_out", gw_ssm_out)])
        else:
            hosted = siblings(["ssm_w_glu", "ssm_w_out"])
        dur, dt_op, dbp_op, dcp_op, da8, extra = _s5_bwd(useq[r], dy_dir[r], hp_dir[r], s5[r]["t"], s5[r]["bp"],
                                                         s5[r]["cp"], s5[r]["tab_adj"], reverse=(r == 1),
                                                         name=f"l1_s5_bwd{r}", hosted=hosted)
        if r == 0:
            rs_parts["ssm_w_glu"], rs_parts["ssm_w_out"] = extra
        else:
            both_halves["ssm_w_glu"], both_halves["ssm_w_out"] = extra
        du_dir.append(dur)
        prm = s5[r]
        s5_grads.append(functools.partial(
            _s5_ops_bwd, prm["bre"], prm["bim"], prm["cre"], prm["cim"], prm["wr"], prm["wi"], prm["wr"][:, 1:2],
            prm["wi"][:, 1:2], dt_op, dbp_op, dcp_op, da8, reverse=(r == 1), name=f"l1_s5_ops_bwd{r}"))
    dp1, dr0, df0, dsc1, dsh1, dg0, db0, dgt0 = _ssm_inbwd(
        du_dir[0], du_dir[1], ws_in_t, xhat0, rstd0, ln0, sc1, gt0, f0, lat=(dy_dir[1], dz1, d_full, dxres),
        row_f=n_ctx, row_b=0, tb=tb_glu, name="l1_inbwd")
    dpc1, drc0, dfc0, dscc1, dshc1, dgc0, dbc0, dgtc0 = _ssm_inbwd(
        du_dir[0], du_dir[1], ws_in_t, chat0, crstd0, ln0, scc1, gtc0, fc0, lat=None,
        row_f=0, row_b=n_lat, tb=n_ctx, name="l1_inbwd_ctx")

    def conv_backward(df, p, yc, dr, xin, sc, grid_mode, tag, hosted_b=None, hosted_in=None):
        dbg, dz, dyc = _conv_bwd_a(df, wc_out_t, p, yc, tb=tb_tok, name="l0_bwd_a" + tag)
        dp, dcw, extra_b = _conv_bwd_b(dyc, p, dbg, dz, conv_w_full, grid_mode=grid_mode, tb=tb_glu,
                                       name="l0_bwd_b" + tag, hosted=hosted_b)
        gx, dsc, dsh, extra_in = _conv_inbwd(dp, wc_in_t, dr, xin, sc, tb=tb_tok, name="l0_inbwd" + tag,
                                             hosted=None if hosted_in is None else hosted_in(dp, extra_b))
        return dp, dcw, gx, dsc, dsh, extra_b, extra_in

    dpc0, dcwc0, _, dscc0, dshc0, _, _ = conv_backward(dfc0, pc0, ycc0, drc0, ctx2, scc0, False, "_ctx")
    gw_conv_out = _wgrad(g0, df0, n_chunks=1, tm=e, tl=tl, name="wg_conv_out",
                         init=_wgrad(gc0, dfc0, n_chunks=1, tm=e, tl=tl, name="wg_conv_out_ctx"))
    gw_ssm_in = _wgrad(h1, dp1, n_chunks=4, tm=d_model, tl=tl, name="wg_ssm_in",
                       init=_wgrad(hc1, dpc1, n_chunks=4, tm=d_model, tl=tl, name="wg_ssm_in_ctx"))
    gw_conv_in_ctx = _wgrad(hc0, dpc0, n_chunks=4, tm=d_model, tl=tl, name="wg_conv_in_ctx")

    def behind_inbwd(dp, arrived):
        rs_parts["ssm_w_in"], rs_parts["conv_w_out"] = arrived
        gw_conv_in = _wgrad(h0, dp, n_chunks=4, tm=d_model, tl=tl, name="wg_conv_in", init=gw_conv_in_ctx)
        both = siblings(["ssm_w_in", "conv_w_out"])
        return _Hosted(scatter([("conv_w_in", gw_conv_in)]).items + both.items)

    dp0, dcw0, grad_x, dsc0, dsh0, _, extra_in = conv_backward(
        df0, p0, yc0, dr0, x2, sc0, True, "", hosted_b=scatter([("ssm_w_in", gw_ssm_in), ("conv_w_out", gw_conv_out)]),
        hosted_in=behind_inbwd)
    rs_parts["conv_w_in"], both_halves["ssm_w_in"], both_halves["conv_w_out"] = extra_in
    *grads_r0, _ = s5_grads[0]()
    *grads_r1, (both_halves["conv_w_in"],) = s5_grads[1](hosted=siblings(["conv_w_in"]))
    s5_grads = [grads_r0, grads_r1]

    grads, deltas, new_m, new_v = {}, {}, {}, {}
    for name in ("ssm_w_glu", "ssm_w_out", "ssm_w_in", "conv_w_out", "conv_w_in"):
        w = weights[name]
        rows, cols = math.prod(w.shape[:-1]), w.shape[-1]
        both = both_halves[name].reshape(rows, cols)
        dlt, nm, nv = _adamw(w.reshape(rows, cols), both, mom_m[name].reshape(rows, cols),
                             mom_v[name].reshape(rows, cols), "adamw_" + name)
        grads[name], deltas[name] = both.reshape(w.shape), dlt.reshape(w.shape)
        new_m[name], new_v[name] = nm.reshape(w.shape), nv.reshape(w.shape)

    gpn = (n_grp, grp, n_state)
    small_parts = [
        jnp.concatenate([dg0 + dgc0, dg1], axis=0), jnp.concatenate([db0 + dbc0, db1], axis=0),
        dcw0 + dcwc0, dd, dbglu,
        jnp.stack([s5_grads[r][4] for r in range(2)]),
    ] + [jnp.stack([_block_diag_t(s5_grads[r][k], *gpn) for r in range(2)]) for k in range(4)] + [loss]
    small_shapes = [p.shape for p in small_parts]
    flat = _flat(small_parts)
    quantum = 8 * SUBLANES * LANES
    n_flat = -(-flat.shape[0] // quantum) * quantum
    flat = jnp.pad(flat, (0, n_flat - flat.shape[0])).reshape(8, n_flat // (8 * LANES), LANES)
    red = _sum_parts(_exchange(flat, everyone, True, "rs_small"), "sum_small")
    red = _exchange(red, everyone, False, "ag_small_grads").reshape(-1)
    g_ln_g, g_ln_b, g_conv_w, g_d, g_bglu, g_a, g_bbr, g_bbi, g_cre, g_cim, loss_sum = _unflat(red, small_shapes)

    def groups_minor(t, lead):
        return jnp.moveaxis(t, 1, -1).reshape(lead, n_grp)

    g_a = g_a.reshape(2, ngb, 2, sbk)
    dar = groups_minor(g_a[:, :, 0].reshape(2, n_grp, n_state), 2 * n_state)
    dai = groups_minor(g_a[:, :, 1].reshape(2, n_grp, n_state), 2 * n_state)
    dbbr_t = jnp.transpose(g_bbr, (2, 0, 3, 1)).reshape(grp, 2 * n_state, n_grp)
    dbbi_t = jnp.transpose(g_bbi, (2, 0, 3, 1)).reshape(grp, 2 * n_state, n_grp)
    z_lre, z_lim, z_ls, z_bre, z_bim = _zoh_bwd(lam_re2, lam_im2, log_step2, b_re_t, b_im_t, dar, dai, dbbr_t, dbbi_t)

    zero = jnp.zeros((1, d_model), F32)
    dm_rows = jnp.stack([
        jnp.stack([jnp.concatenate([dsh0, dsc0, dgt0], axis=1), jnp.concatenate([dshc0, dscc0, dgtc0], axis=1)]),
        jnp.stack([jnp.concatenate([dsh1, dsc1, dgt1], axis=1), jnp.concatenate([dshc1, dscc1, zero], axis=1)]),
    ]).reshape(DEPTH, 2, 3 * d_model)
    dm_all = _exchange(dm_rows, everyone, False, "ag_dmod")
    dm_sh = lax.dynamic_slice_in_dim(dm_all, chip * ws, ws, axis=3)
    g_ada_w, g_ada_b, ds_part = _ada_bwd(c_all, cc2, ada_w, dm_all, dm_sh)
    g_cctx = _cctx_grad(_exchange(ds_part, chips, False, "ag_dsctx"), cc2)

    grads["ada_w"] = g_ada_w
    dlt, nm, nv = _adamw(ada_w.reshape(-1, ws), g_ada_w.reshape(-1, ws), m_ada_w.reshape(-1, ws),
                         v_ada_w.reshape(-1, ws), "adamw_ada_w")
    deltas["ada_w"], new_m["ada_w"], new_v["ada_w"] = dlt.reshape(ada_w.shape), nm.reshape(ada_w.shape), nv.reshape(ada_w.shape)

    def chip_cols(full, rows):
        return lax.dynamic_slice_in_dim(full.reshape(rows, e), chip * es, es, axis=1)

    def same(t):
        return t

    def channel_minor_back(t):
        return jnp.transpose(t, (1, 0, 2)).reshape(2 * n_state * grp, n_grp)

    small = dict(
        c_ctx=(g_cctx, lambda t: t.reshape(1, d_model), lambda t: t.reshape(c_ctx.shape)),
        ada_b=(g_ada_b.reshape(ada_b.shape), same, same),
        ln_g=(g_ln_g, same, same), ln_b=(g_ln_b, same, same),
        conv_w=(chip_cols(g_conv_w, 3), lambda t: t[0], lambda t: t.reshape(conv_w.shape)),
        ssm_lam_re=(z_lre, lam_view, lam_back), ssm_lam_im=(z_lim, lam_view, lam_back),
        ssm_log_step=(z_ls, lambda t: t[0], lambda t: t.reshape(ssm_log_step.shape)),
        ssm_b_re=(channel_minor_back(z_bre), b_view, b_back), ssm_b_im=(channel_minor_back(z_bim), b_view, b_back),
        ssm_c_re=(groups_minor(g_cre, 2 * grp * n_state), c_view, c_back),
        ssm_c_im=(groups_minor(g_cim, 2 * grp * n_state), c_view, c_back),
        ssm_d=(chip_cols(g_d, 1), same, same), ssm_b_glu=(chip_cols(g_bglu, 1), same, same))
    for n, (g_view, view, back) in small.items():
        dlt, nm, nv = _adamw(view(weights[n]), g_view, view(mom_m[n]), view(mom_v[n]), "adamw_" + n)
        grads[n], deltas[n], new_m[n], new_v[n] = back(g_view), back(dlt), back(nm), back(nv)

    return (loss_sum.reshape(()), grad_x.reshape(x.shape), *[grads[n] for n in names], *[deltas[n] for n in names],
            *[new_m[n] for n in names], *[new_v[n] for n in names])
```

```python
import functools
import math

import jax
import jax.numpy as jnp
from jax import lax
from jax.experimental import pallas as pl
from jax.experimental.pallas import tpu as pltpu

F32 = jnp.float32
BF16 = jnp.bfloat16
LANES = 128
SUBLANES = 8
VMEM_LIMIT = 56 * 1024 * 1024
MESH_AXES = ("x", "y", "c")
HIGHEST = lax.Precision.HIGHEST

GRID_W = 64
LN_EPS = 1e-5
DEPTH = 2
DN_ALPHA = (2 * DEPTH) ** 0.25
ADAM_LR, ADAM_B1, ADAM_B2, ADAM_EPS, ADAM_WD, ADAM_STEP = 0.001, 0.9, 0.999, 1e-08, 0.01, 10
GELU_K = math.sqrt(2.0 / math.pi)
GELU_C = 0.044715


def _params(n_grid_axes):
    return pltpu.CompilerParams(dimension_semantics=("arbitrary",) * n_grid_axes, vmem_limit_bytes=VMEM_LIMIT)


def _dot(a, b):
    return jnp.dot(a, b, preferred_element_type=F32)


def _dot_nt(a, b):
    return lax.dot_general(a, b, (((1,), (1,)), ((), ())), preferred_element_type=F32)


def _dot_tn(a, b):
    return lax.dot_general(a, b, (((0,), (0,)), ((), ())), preferred_element_type=F32)


def _sigmoid(x):
    return 0.5 * jnp.tanh(0.5 * x) + 0.5


def _colsum(x):
    return jnp.sum(x, axis=0, keepdims=True)


def _rowmean(x):
    return jnp.mean(x, axis=-1, keepdims=True)


def _zero_first(first, *refs):
    @pl.when(first)
    def _():
        for ref in refs:
            ref[...] = jnp.zeros_like(ref)


def _acc(ref, value):
    ref[...] += value


def _exchange_copies(src_ref, out_ref, send_sems, recv_sems, own_sem, axes, all_to_all, sem0=0):
    n_peers = 2 ** len(axes)
    pos = {a: lax.axis_index(a) for a in MESH_AXES}

    def index(p):
        return sum(p[a] * (2 ** (len(axes) - 1 - i)) for i, a in enumerate(axes))

    me = index(pos)
    own = pltpu.make_async_copy(src_ref.at[me] if all_to_all else src_ref, out_ref.at[me], own_sem)
    copies = []
    for k in range(1, n_peers):
        peer = dict(pos)
        for i, a in enumerate(axes):
            if (k >> (len(axes) - 1 - i)) & 1:
                peer[a] = 1 - pos[a]
        copies.append(pltpu.make_async_remote_copy(
            src_ref=src_ref.at[index(peer)] if all_to_all else src_ref,
            dst_ref=out_ref.at[me],
            send_sem=send_sems.at[sem0 + k - 1],
            recv_sem=recv_sems.at[sem0 + k - 1],
            device_id=tuple(peer[a] for a in MESH_AXES),
            device_id_type=pl.DeviceIdType.MESH,
        ))
    return copies, own


def _exchange_shape(src, axes, all_to_all):
    block = tuple(src.shape[1:] if all_to_all else src.shape)
    return jax.ShapeDtypeStruct((2 ** len(axes),) + block, src.dtype)


def _exchange(src, axes, all_to_all, name):
    n_peers = 2 ** len(axes)

    def body(src_ref, out_ref, send_sems, recv_sems, own_sem):
        copies, own = _exchange_copies(src_ref, out_ref, send_sems, recv_sems, own_sem, axes, all_to_all)
        own.start()
        for cp in copies:
            cp.start()
        for cp in copies:
            cp.wait()
        own.wait()

    return pl.pallas_call(
        body,
        name=name,
        out_shape=_exchange_shape(src, axes, all_to_all),
        in_specs=[pl.BlockSpec(memory_space=pltpu.HBM)],
        out_specs=pl.BlockSpec(memory_space=pltpu.HBM),
        scratch_shapes=[
            pltpu.SemaphoreType.DMA((n_peers - 1,)),
            pltpu.SemaphoreType.DMA((n_peers - 1,)),
            pltpu.SemaphoreType.DMA,
        ],
    )(src)


class _Hosted:
    def __init__(self, items):
        self.items = items
        self.args = [src for src, _, _ in items]
        self.in_specs = [pl.BlockSpec(memory_space=pltpu.HBM)] * len(items)
        self.out_specs = [pl.BlockSpec(memory_space=pltpu.HBM)] * len(items)
        self.out_shapes = [_exchange_shape(*item) for item in items]
        n_remote = sum(2 ** len(axes) - 1 for _, axes, _ in items)
        self.scratch = [pltpu.SemaphoreType.DMA((n_remote,)), pltpu.SemaphoreType.DMA((n_remote,)),
                        pltpu.SemaphoreType.DMA((len(items),))]

    def _copies(self, src_refs, out_refs, send_sems, recv_sems, own_sems):
        out, sem0 = [], 0
        for n, (_, axes, all_to_all) in enumerate(self.items):
            copies, own = _exchange_copies(src_refs[n], out_refs[n], send_sems, recv_sems, own_sems.at[n], axes,
                                           all_to_all, sem0)
            out += [own] + copies
            sem0 += len(copies)
        return out

    def start(self, *refs):
        for cp in self._copies(*refs):
            cp.start()

    def wait(self, *refs):
        for cp in self._copies(*refs):
            cp.wait()


def _call(body, *, name, grid, in_specs, out_specs, out_shape, scratch_shapes, args, hosted=None):
    params = _params(len(grid))
    if hosted is None:
        outs = pl.pallas_call(body, name=name, grid=grid, in_specs=in_specs, out_specs=tuple(out_specs),
                              out_shape=tuple(out_shape), scratch_shapes=list(scratch_shapes), compiler_params=params)(*args)
        return list(outs), []
    n_in, n_out, n_scr, n_h = len(in_specs), len(out_shape), len(scratch_shapes), len(hosted.items)

    def wrapped(*refs):
        ins, h_in = refs[:n_in], refs[n_in:n_in + n_h]
        outs, h_out = refs[n_in + n_h:n_in + n_h + n_out], refs[n_in + n_h + n_out:n_in + 2 * n_h + n_out]
        scr = refs[n_in + 2 * n_h + n_out:]
        first = functools.reduce(jnp.logical_and, [pl.program_id(k) == 0 for k in range(len(grid))])
        last = functools.reduce(jnp.logical_and, [pl.program_id(k) == grid[k] - 1 for k in range(len(grid))])

        @pl.when(first)
        def _():
            hosted.start(h_in, h_out, *scr[n_scr:])

        body(*ins, *outs, *scr[:n_scr])

        @pl.when(last)
        def _():
            hosted.wait(h_in, h_out, *scr[n_scr:])

    outs = pl.pallas_call(
        wrapped, name=name, grid=grid, in_specs=[*in_specs, *hosted.in_specs],
        out_specs=(*out_specs, *hosted.out_specs), out_shape=(*out_shape, *hosted.out_shapes),
        scratch_shapes=[*scratch_shapes, *hosted.scratch], compiler_params=params)(*args, *hosted.args)
    return list(outs[:n_out]), list(outs[n_out:])


def _sum_parts(parts, name):
    n_parts, rows, cols = parts.shape
    tr = rows
    while n_parts * tr * cols * 4 > 8 * 1024 * 1024 and tr % 16 == 0:
        tr //= 2

    def body(p_ref, o_ref):
        total = p_ref[0]
        for k in range(1, n_parts):
            total = total + p_ref[k]
        o_ref[...] = total

    return pl.pallas_call(
        body,
        name=name,
        grid=(rows // tr,),
        out_shape=jax.ShapeDtypeStruct((rows, cols), F32),
        in_specs=[pl.BlockSpec((n_parts, tr, cols), lambda i: (0, i, 0))],
        out_specs=pl.BlockSpec((tr, cols), lambda i: (i, 0)),
        compiler_params=_params(1),
    )(parts)


def _adamw(w, g, m, v, name):
    rows, cols = w.shape
    tr = rows
    while tr * cols * 4 > 2 * 1024 * 1024 and tr % 16 == 0:
        tr //= 2

    def body(w_ref, g_ref, m_ref, v_ref, d_ref, nm_ref, nv_ref):
        gv = g_ref[...]
        nm = ADAM_B1 * m_ref[...] + (1.0 - ADAM_B1) * gv
        nv = ADAM_B2 * v_ref[...] + (1.0 - ADAM_B2) * (gv * gv)
        m_hat = nm / (1.0 - ADAM_B1 ** ADAM_STEP)
        v_hat = nv / (1.0 - ADAM_B2 ** ADAM_STEP)
        d_ref[...] = -ADAM_LR * (m_hat / (jnp.sqrt(v_hat) + ADAM_EPS) + ADAM_WD * w_ref[...])
        nm_ref[...] = nm
        nv_ref[...] = nv

    spec = pl.BlockSpec((tr, cols), lambda i: (i, 0))
    shape = jax.ShapeDtypeStruct((rows, cols), F32)
    return pl.pallas_call(
        body, name=name, grid=(rows // tr,), out_shape=(shape, shape, shape),
        in_specs=[spec] * 4, out_specs=(spec, spec, spec), compiler_params=_params(1),
    )(w, g, m, v)


def _ada_rows(c_ref, cc_ref):
    rows = jnp.concatenate([c_ref[...], jnp.broadcast_to(cc_ref[...], c_ref.shape)], axis=0)
    return rows


def _ada_fwd(c_all, c_ctx, w_sh, b_sh):
    n_layers, _, ws = w_sh.shape

    def body(c_ref, cc_ref, w_ref, b_ref, o_ref):
        rows = _ada_rows(c_ref, cc_ref)
        s = rows * _sigmoid(rows)
        for i in range(n_layers):
            o_ref[i] = jnp.dot(s, w_ref[i], precision=HIGHEST, preferred_element_type=F32) + b_ref[i]

    return pl.pallas_call(
        body, name="ada_fwd", out_shape=jax.ShapeDtypeStruct((n_layers, 16, ws), F32),
        compiler_params=pltpu.CompilerParams(vmem_limit_bytes=VMEM_LIMIT),
    )(c_all, c_ctx, w_sh, b_sh)


def _ada_bwd(c_all, c_ctx, w_sh, dm_full, dm_sh):
    n_layers, d_model, ws = w_sh.shape
    n_dev = dm_full.shape[0]
    cols = dm_full.shape[-1]

    def body(c_ref, cc_ref, w_ref, dmf_ref, dms_ref, gw_ref, gb_ref, ds_ref):
        rows = _ada_rows(c_ref, cc_ref)
        s = rows * _sigmoid(rows)
        ds = jnp.zeros((8, d_model), F32)
        for i in range(n_layers):
            ctx_s = dms_ref[0, i, 1:2, :]
            ctx_f = dmf_ref[0, i, 1:2, :]
            ex_f = dmf_ref[0, i, 0:1, :]
            for k in range(1, n_dev):
                ctx_s = ctx_s + dms_ref[k, i, 1:2, :]
                ctx_f = ctx_f + dmf_ref[k, i, 1:2, :]
                ex_f = ex_f + dmf_ref[k, i, 0:1, :]
            gb_ref[i] = ex_f + ctx_f
            r = jnp.concatenate([dms_ref[k, i, 0:1, :] for k in range(n_dev)] + [ctx_s, jnp.zeros((7, ws), F32)], axis=0)
            gw_ref[i] = lax.dot_general(s, r, (((0,), (0,)), ((), ())), precision=HIGHEST, preferred_element_type=F32)
            ds = ds + lax.dot_general(jnp.broadcast_to(ctx_s, (8, ws)), w_ref[i], (((1,), (1,)), ((), ())),
                                      precision=HIGHEST, preferred_element_type=F32)
        ds_ref[...] = ds

    return pl.pallas_call(
        body, name="ada_bwd",
        out_shape=(jax.ShapeDtypeStruct((n_layers, d_model, ws), F32), jax.ShapeDtypeStruct((n_layers, 1, cols), F32),
                   jax.ShapeDtypeStruct((8, d_model), F32)),
        compiler_params=pltpu.CompilerParams(vmem_limit_bytes=VMEM_LIMIT),
    )(c_all, c_ctx, w_sh, dm_full, dm_sh)


def _cctx_grad(ds_parts, c_ctx):
    def body(p_ref, c_ref, o_ref):
        tot = p_ref[0, 0:1, :]
        for k in range(1, ds_parts.shape[0]):
            tot = tot + p_ref[k, 0:1, :]
        cv = c_ref[...]
        sg = _sigmoid(cv)
        o_ref[...] = tot * (sg * (1.0 + cv * (1.0 - sg)))

    return pl.pallas_call(body, name="cctx_grad", out_shape=jax.ShapeDtypeStruct(c_ctx.shape, F32))(ds_parts, c_ctx)


def _zoh_math(lam_re, lam_im, log_step, b_re, b_im):
    n_state = lam_re.shape[0] // 2
    dt = jnp.exp(jnp.concatenate([jnp.broadcast_to(log_step[r:r + 1], (n_state, log_step.shape[1])) for r in range(2)],
                                 axis=0))
    mag = jnp.exp(lam_re * dt)
    ar = mag * jnp.cos(lam_im * dt)
    ai = mag * jnp.sin(lam_im * dt)
    qr, qi = ar - 1.0, ai
    den = lam_re * lam_re + lam_im * lam_im
    fr = (qr * lam_re + qi * lam_im) / den
    fi = (qi * lam_re - qr * lam_im) / den
    bbr = fr[None] * b_re - fi[None] * b_im
    bbi = fr[None] * b_im + fi[None] * b_re
    return ar, ai, bbr, bbi


def _zoh_fwd(lam_re, lam_im, log_step, b_re, b_im):
    rg, n = lam_re.shape

    def body(lr_ref, li_ref, ls_ref, br_ref, bi_ref, pr_ref, pi_ref, qr_ref, qi_ref, bbr_ref, bbi_ref):
        ar, ai, bbr, bbi = _zoh_math(lr_ref[...], li_ref[...], ls_ref[...], br_ref[...], bi_ref[...])
        bbr_ref[...] = bbr
        bbi_ref[...] = bbi

        def powers(base_r, base_i, r_ref, i_ref):
            pr, pi_ = base_r, base_i
            for k in range(8):
                r_ref[k] = pr
                i_ref[k] = pi_
                pr, pi_ = pr * base_r - pi_ * base_i, pr * base_i + pi_ * base_r

        powers(ar, ai, pr_ref, pi_ref)
        powers(pr_ref[7], pi_ref[7], qr_ref, qi_ref)

    pw = jax.ShapeDtypeStruct((8, rg, n), F32)
    bb = jax.ShapeDtypeStruct(b_re.shape, F32)
    return pl.pallas_call(body, name="zoh_fwd", out_shape=(pw, pw, pw, pw, bb, bb))(lam_re, lam_im, log_step, b_re, b_im)


def _zoh_bwd(lam_re, lam_im, log_step, b_re, b_im, dar, dai, dbbr, dbbi):
    def body(lr_ref, li_ref, ls_ref, br_ref, bi_ref, dar_ref, dai_ref, dbr_ref, dbi_ref, *outs):
        _, vjp = jax.vjp(_zoh_math, lr_ref[...], li_ref[...], ls_ref[...], br_ref[...], bi_ref[...])
        grads = vjp((dar_ref[...], dai_ref[...], dbr_ref[...], dbi_ref[...]))
        for o_ref, gval in zip(outs, grads):
            o_ref[...] = gval

    shapes = tuple(jax.ShapeDtypeStruct(a.shape, F32) for a in (lam_re, lam_im, log_step, b_re, b_im))
    return pl.pallas_call(body, name="zoh_bwd", out_shape=shapes)(lam_re, lam_im, log_step, b_re, b_im, dar, dai, dbbr, dbbi)


def _inproj(xin, sc, sh, w, *, lnaff=None, tb, gb_rows=None, name, hosted=None):
    n_tok, d_model = xin.shape
    n_chunks, _, cw = w.shape
    tb = min(tb, n_tok)
    nq = cw // LANES
    has_ln = lnaff is not None
    n_out = 1 if gb_rows is None else len(gb_rows)

    def body(*refs):
        if has_ln:
            x_ref, g_ref, b_ref, sc_ref, sh_ref, w_ref = refs[:6]
        else:
            x_ref, sc_ref, sh_ref, w_ref = refs[:4]
        p_refs, h_ref = refs[-1 - n_out:-1], refs[-1]

        @pl.when(pl.program_id(1) == 0)
        def _():
            xv = x_ref[...]
            if has_ln:
                xv = xv * g_ref[...] + b_ref[...]
            h_ref[...] = (xv * (1.0 + sc_ref[...]) + sh_ref[...]).astype(BF16)

        acc = _dot(h_ref[...], w_ref[0]).astype(BF16)
        if gb_rows is None:
            p_refs[0][0] = acc
        else:
            for p_ref in p_refs:
                for q in range(nq):
                    p_ref[q] = acc[:, q * LANES:(q + 1) * LANES]

    vec = pl.BlockSpec((1, d_model), lambda i, j: (0, 0))
    in_specs = [pl.BlockSpec((tb, d_model), lambda i, j: (i, 0))] + ([vec, vec] if has_ln else []) + [
        vec, vec, pl.BlockSpec((1, d_model, cw), lambda i, j: (j, 0, 0))]
    if gb_rows is None:
        p_shapes = [jax.ShapeDtypeStruct((n_chunks, n_tok, cw), BF16)]
        p_specs = [pl.BlockSpec((1, tb, cw), lambda i, j: (j, i, 0))]
    else:
        p_shapes, p_specs = [], []
        for total, off in gb_rows:
            assert off % tb == 0
            p_shapes.append(jax.ShapeDtypeStruct((n_chunks * nq, total, LANES), BF16))
            p_specs.append(pl.BlockSpec((nq, tb, LANES), functools.partial(lambda i, j, ob: (j, i + ob, 0), ob=off // tb)))
    args = (xin,) + (tuple(lnaff) if has_ln else ()) + (sc, sh, w)
    outs, extra = _call(
        body, name=name, grid=(n_tok // tb, n_chunks), in_specs=in_specs,
        out_specs=[*p_specs, pl.BlockSpec((tb, d_model), lambda i, j: (i, 0))],
        out_shape=[*p_shapes, jax.ShapeDtypeStruct((n_tok, d_model), BF16)], scratch_shapes=[], args=args, hosted=hosted)
    return (*outs, extra) if hosted is not None else tuple(outs)


def _inproj_seq(xin, sc, sh, w, lnaff, *, tb, seq_rows, name):
    n_tok, d_model = xin.shape
    n_chunks, _, cw = w.shape
    tb = min(tb, n_tok)
    nq = cw // LANES
    n_out = len(seq_rows)
    steps = (n_tok // tb) * n_chunks

    def body(x_ref, g_ref, b_ref, sc_ref, sh_ref, w_ref, *rest):
        p_refs, h_ref, stage, sems = rest[:n_out], rest[n_out], rest[n_out + 1], rest[n_out + 2]
        i, j = pl.program_id(0), pl.program_id(1)
        step = i * n_chunks + j
        slot = step % 2

        def copies(from_slot):
            return [pltpu.make_async_copy(stage.at[from_slot],
                                          p_ref.at[pl.ds(j * nq, nq), pl.ds(off + i * tb, tb), :], sems.at[from_slot, k])
                    for k, (p_ref, (_, off)) in enumerate(zip(p_refs, seq_rows))]

        @pl.when(step >= 2)
        def _():
            for cp in copies(slot):
                cp.wait()

        @pl.when(j == 0)
        def _():
            xv = x_ref[...] * g_ref[...] + b_ref[...]
            h_ref[...] = (xv * (1.0 + sc_ref[...]) + sh_ref[...]).astype(BF16)

        acc = _dot(h_ref[...], w_ref[0]).astype(BF16)
        for q in range(nq):
            stage[slot, q] = acc[:, q * LANES:(q + 1) * LANES]
        for cp in copies(slot):
            cp.start()

        @pl.when(step == steps - 1)
        def _():
            for cp in copies(slot):
                cp.wait()
            if steps > 1:
                for cp in copies(1 - slot):
                    cp.wait()

    vec = pl.BlockSpec((1, d_model), lambda i, j: (0, 0))
    tok = pl.BlockSpec((tb, d_model), lambda i, j: (i, 0))
    return pl.pallas_call(
        body, name=name, grid=(n_tok // tb, n_chunks),
        in_specs=[tok, vec, vec, vec, vec, pl.BlockSpec((1, d_model, cw), lambda i, j: (j, 0, 0))],
        out_specs=(*[pl.BlockSpec(memory_space=pltpu.HBM)] * n_out, tok),
        out_shape=(*[jax.ShapeDtypeStruct((n_chunks * nq, total, LANES), BF16) for total, _ in seq_rows],
                   jax.ShapeDtypeStruct((n_tok, d_model), BF16)),
        scratch_shapes=[pltpu.VMEM((2, nq, tb, LANES), BF16), pltpu.SemaphoreType.DMA((2, n_out))],
        compiler_params=_params(2),
    )(xin, lnaff[0], lnaff[1], sc, sh, w)


def _shifted(u, rows, width, tb):
    col = rows % width
    um = jnp.where(col == 0, 0.0, pltpu.roll(u, 1, 0))
    up = jnp.where(col == width - 1, 0.0, pltpu.roll(u, tb - 1, 0))
    return um, up


def _slab_width(e):
    return min(512, e // 2)


def _convgate(p, x, gt, conv_w, w_out, ln_g, ln_b, *, grid_mode, tb, name):
    _, n_tok, e = p.shape
    d_model = x.shape[1]
    eh = e // 2
    if not grid_mode:
        tb = n_tok
    tb = min(tb, n_tok)
    nb = n_tok // tb
    hb = tb // GRID_W
    cs = _slab_width(e)

    def body(*refs):
        if grid_mode:
            (bg_ref, cg_ref, v_ref, z_ref, cgp_ref, vp_ref, cgn_ref, vn_ref, x_ref, gt_ref, cw_ref, wo_ref, lg_ref,
             lb_ref, xh_ref, rs_ref, g_ref, yc_ref, f_ref) = refs
        else:
            (bg_ref, cg_ref, v_ref, z_ref, x_ref, gt_ref, cw_ref, wo_ref, lg_ref, lb_ref, xh_ref, rs_ref, g_ref,
             yc_ref, f_ref) = refs
        i = pl.program_id(0)
        rows = lax.broadcasted_iota(jnp.int32, (tb, 1), 0)
        for c0 in range(0, e, cs):
            sl = slice(c0, c0 + cs)
            u = cg_ref[0, :, sl].astype(F32) * v_ref[0, :, sl].astype(F32)
            w = cw_ref[:, sl]
            if grid_mode and c0 >= eh:
                hs = slice(c0 - eh, c0 - eh + cs)
                uprev = cgp_ref[0, :, hs].astype(F32) * vp_ref[0, :, hs].astype(F32)
                unext = cgn_ref[0, :, hs].astype(F32) * vn_ref[0, :, hs].astype(F32)
                uprev = jnp.where(i > 0, uprev, 0.0)
                unext = jnp.where(i < nb - 1, unext, 0.0)
                if tb > GRID_W:
                    um = jnp.concatenate([uprev, u[:tb - GRID_W]], axis=0)
                    up = jnp.concatenate([u[GRID_W:], unext], axis=0)
                else:
                    um, up = uprev, unext
            else:
                um, up = _shifted(u, rows, GRID_W if grid_mode else tb, tb)
            yc = um * w[0:1] + u * w[1:2] + up * w[2:3]
            zf = z_ref[0, :, sl].astype(F32)
            gval = bg_ref[0, :, sl].astype(F32) * yc * (zf * _sigmoid(zf))
            yc_ref[:, sl] = yc.astype(BF16)
            g_ref[:, sl] = gval.astype(BF16)
        f = _dot(g_ref[...], wo_ref[...])
        f_ref[...] = f.astype(BF16)
        r = DN_ALPHA * x_ref[...] + gt_ref[...] * f
        rc = r - _rowmean(r)
        rstd = lax.rsqrt(_rowmean(rc * rc) + LN_EPS)
        xh_ref[...] = rc * rstd
        rs_ref[...] = rstd

    def chunk(k):
        return pl.BlockSpec((1, tb, e), lambda i: (k, i, 0))

    n_hrows = n_tok // GRID_W

    def halo_prev(k):
        return pl.BlockSpec((1, GRID_W, eh), lambda i: (k, jnp.maximum(i * hb - 1, 0), 1))

    def halo_next(k):
        return pl.BlockSpec((1, GRID_W, eh), lambda i: (k, jnp.minimum((i + 1) * hb, n_hrows - 1), 1))

    vec = pl.BlockSpec((1, d_model), lambda i: (0, 0))
    tok = pl.BlockSpec((tb, d_model), lambda i: (i, 0))
    wide = pl.BlockSpec((tb, e), lambda i: (i, 0))
    in_specs = [chunk(0), chunk(1), chunk(2), chunk(3)]
    args = [p, p, p, p]
    if grid_mode:
        in_specs += [halo_prev(1), halo_prev(2), halo_next(1), halo_next(2)]
        args += [p, p, p, p]
    in_specs += [tok, vec, pl.BlockSpec((3, e), lambda i: (0, 0)), pl.BlockSpec((e, d_model), lambda i: (0, 0)), vec, vec]
    args += [x, gt, conv_w, w_out, ln_g, ln_b]
    return pl.pallas_call(
        body, name=name, grid=(nb,),
        out_shape=(jax.ShapeDtypeStruct((n_tok, d_model), F32), jax.ShapeDtypeStruct((n_tok, 1), F32),
                   jax.ShapeDtypeStruct((n_tok, e), BF16), jax.ShapeDtypeStruct((n_tok, e), BF16),
                   jax.ShapeDtypeStruct((n_tok, d_model), BF16)),
        in_specs=in_specs, out_specs=(tok, pl.BlockSpec((tb, 1), lambda i: (i, 0)), wide, wide, tok),
        compiler_params=_params(1),
    )(*args)


def _scan_block(buf_ref, tab_ref, cr, ci, *, reverse, tb, sb):
    n_slabs = tb // SUBLANES
    unrolled = n_slabs <= 64

    def slab(s, carry):
        cr, ci = carry
        idx = (n_slabs - 1 - s) if reverse else s
        r0 = idx * SUBLANES if unrolled else pl.multiple_of(idx * SUBLANES, SUBLANES)
        xr = buf_ref[pl.ds(r0, SUBLANES), 0:sb]
        xi = buf_ref[pl.ds(r0, SUBLANES), sb:2 * sb]
        for k, step in enumerate((1, 2, 4)):
            ar = tab_ref[2 * k]
            ai = tab_ref[2 * k + 1]
            shift = (SUBLANES - step) if reverse else step
            rr = pltpu.roll(xr, shift, 0)
            ri = pltpu.roll(xi, shift, 0)
            xr, xi = xr + ar * rr - ai * ri, xi + ar * ri + ai * rr
        pr = tab_ref[6]
        pi_ = tab_ref[7]
        xr, xi = xr + pr * cr - pi_ * ci, xi + pr * ci + pi_ * cr
        buf_ref[pl.ds(r0, SUBLANES), 0:sb] = xr
        buf_ref[pl.ds(r0, SUBLANES), sb:2 * sb] = xi
        last = 0 if reverse else SUBLANES - 1
        return (jnp.broadcast_to(xr[last:last + 1, :], (SUBLANES, sb)),
                jnp.broadcast_to(xi[last:last + 1, :], (SUBLANES, sb)))

    if unrolled:
        carry = (cr, ci)
        for s in range(n_slabs):
            carry = slab(s, carry)
        return carry
    return lax.fori_loop(0, n_slabs, slab, (cr, ci))


CHUNK = SUBLANES


def _group_mask():
    r = lax.broadcasted_iota(jnp.int32, (LANES, LANES), 0)
    c = lax.broadcasted_iota(jnp.int32, (LANES, LANES), 1)
    return r // 16 == c // 16


def _s5_ops(bre, bim, cre, cim, wr, wi, *, reverse, name, hosted=None):
    ngb, _, sb = bre.shape
    n_rows = CHUNK * LANES

    def body(bre_ref, bim_ref, cre_ref, cim_ref, wr_ref, wi_ref, t_ref, bp_ref, cp_ref):
        b_re, b_im, c_re, c_im = bre_ref[0], bim_ref[0], cre_ref[0], cim_ref[0]
        mask = _group_mask()
        er, ei = [], []
        for tau in range(CHUNK + 1):
            w_r, w_i = wr_ref[0, tau:tau + 1, :], wi_ref[0, tau:tau + 1, :]
            er.append(c_re * w_r - c_im * w_i)
            ei.append(c_re * w_i + c_im * w_r)
        kt = []
        for tau in range(CHUNK):
            k = (lax.dot_general(b_re, er[tau], (((1,), (1,)), ((), ())), precision=HIGHEST, preferred_element_type=F32)
                 - lax.dot_general(b_im, ei[tau], (((1,), (1,)), ((), ())), precision=HIGHEST, preferred_element_type=F32))
            kt.append(jnp.where(mask, k, 0.0).astype(BF16))
        zero = jnp.zeros((LANES, LANES), BF16)
        for i in range(CHUNK):
            rows = slice(i * LANES, (i + 1) * LANES)
            for j in range(CHUNK):
                lag = (i - j) if reverse else (j - i)
                t_ref[0, rows, j * LANES:(j + 1) * LANES] = kt[lag] if lag >= 0 else zero
            tau = i if reverse else CHUNK - 1 - i
            w_r, w_i = wr_ref[0, tau:tau + 1, :], wi_ref[0, tau:tau + 1, :]
            bp_ref[0, rows, 0:sb] = (b_re * w_r - b_im * w_i).astype(BF16)
            bp_ref[0, rows, sb:2 * sb] = (b_re * w_i + b_im * w_r).astype(BF16)
            tau = CHUNK - i if reverse else i + 1
            cp_ref[0, rows, 0:sb] = er[tau].astype(BF16)
            cp_ref[0, rows, sb:2 * sb] = (-ei[tau]).astype(BF16)

    mat = pl.BlockSpec((1, LANES, sb), lambda g: (g, 0, 0))
    pw = pl.BlockSpec((1, CHUNK + 1, sb), lambda g: (g, 0, 0))
    outs, extra = _call(
        body, name=name, grid=(ngb,),
        out_shape=[jax.ShapeDtypeStruct((ngb, n_rows, n_rows), BF16), jax.ShapeDtypeStruct((ngb, n_rows, 2 * sb), BF16),
                   jax.ShapeDtypeStruct((ngb, n_rows, 2 * sb), BF16)],
        in_specs=[mat, mat, mat, mat, pw, pw],
        out_specs=[pl.BlockSpec((1, n_rows, n_rows), lambda g: (g, 0, 0)),
                   pl.BlockSpec((1, n_rows, 2 * sb), lambda g: (g, 0, 0)),
                   pl.BlockSpec((1, n_rows, 2 * sb), lambda g: (g, 0, 0))],
        scratch_shapes=[], args=(bre, bim, cre, cim, wr, wi), hosted=hosted)
    return (*outs, extra)


def _s5_ops_bwd(bre, bim, cre, cim, wr, wi, ar, ai, dt, dbp, dcp, da8, *, reverse, name, hosted=None):
    ngb, _, sb = bre.shape
    n_rows = CHUNK * LANES

    def dot_hi(a, b, dims):
        return lax.dot_general(a.astype(BF16), b.astype(BF16), (dims, ((), ())), preferred_element_type=F32)

    def body(bre_ref, bim_ref, cre_ref, cim_ref, wr_ref, wi_ref, ar_ref, ai_ref, dt_ref, dbp_ref, dcp_ref, da8_ref,
             dbre_ref, dbim_ref, dcre_ref, dcim_ref, da_ref):
        b_re, b_im, c_re, c_im = bre_ref[0], bim_ref[0], cre_ref[0], cim_ref[0]
        mask = _group_mask()
        w_r = [wr_ref[0, tau:tau + 1, :] for tau in range(CHUNK + 1)]
        w_i = [wi_ref[0, tau:tau + 1, :] for tau in range(CHUNK + 1)]
        der = [jnp.zeros((LANES, sb), F32) for _ in range(CHUNK + 1)]
        dei = [jnp.zeros((LANES, sb), F32) for _ in range(CHUNK + 1)]
        dwr = [jnp.zeros((1, sb), F32) for _ in range(CHUNK + 1)]
        dwi = [jnp.zeros((1, sb), F32) for _ in range(CHUNK + 1)]
        dwr[CHUNK] = da8_ref[0, :, 0:sb]
        dwi[CHUNK] = da8_ref[0, :, sb:2 * sb]
        d_bre = jnp.zeros((LANES, sb), F32)
        d_bim = jnp.zeros((LANES, sb), F32)
        dkt = [jnp.zeros((LANES, LANES), F32) for _ in range(CHUNK)]
        for i in range(CHUNK):
            rows = slice(i * LANES, (i + 1) * LANES)
            for j in range(CHUNK):
                lag = (i - j) if reverse else (j - i)
                if lag >= 0:
                    dkt[lag] = dkt[lag] + dt_ref[0, rows, j * LANES:(j + 1) * LANES]
            tau = i if reverse else CHUNK - 1 - i
            g_r, g_i = dbp_ref[0, rows, 0:sb], dbp_ref[0, rows, sb:2 * sb]
            d_bre = d_bre + g_r * w_r[tau] + g_i * w_i[tau]
            d_bim = d_bim - g_r * w_i[tau] + g_i * w_r[tau]
            dwr[tau] = dwr[tau] + _colsum(g_r * b_re + g_i * b_im)
            dwi[tau] = dwi[tau] + _colsum(g_i * b_re - g_r * b_im)
            tau = CHUNK - i if reverse else i + 1
            der[tau] = der[tau] + dcp_ref[0, rows, 0:sb]
            dei[tau] = dei[tau] - dcp_ref[0, rows, sb:2 * sb]
        d_cre = jnp.zeros((LANES, sb), F32)
        d_cim = jnp.zeros((LANES, sb), F32)
        for tau in range(CHUNK + 1):
            if tau < CHUNK:
                e_r = c_re * w_r[tau] - c_im * w_i[tau]
                e_i = c_re * w_i[tau] + c_im * w_r[tau]
                dk = jnp.where(mask, dkt[tau], 0.0)
                d_bre = d_bre + dot_hi(dk, e_r, ((1,), (0,)))
                d_bim = d_bim - dot_hi(dk, e_i, ((1,), (0,)))
                der[tau] = der[tau] + dot_hi(dk, b_re, ((0,), (0,)))
                dei[tau] = dei[tau] - dot_hi(dk, b_im, ((0,), (0,)))
            d_cre = d_cre + der[tau] * w_r[tau] + dei[tau] * w_i[tau]
            d_cim = d_cim - der[tau] * w_i[tau] + dei[tau] * w_r[tau]
            dwr[tau] = dwr[tau] + _colsum(der[tau] * c_re + dei[tau] * c_im)
            dwi[tau] = dwi[tau] + _colsum(dei[tau] * c_re - der[tau] * c_im)
        a_r, a_i = ar_ref[0], ai_ref[0]
        d_ar = jnp.zeros((1, sb), F32)
        d_ai = jnp.zeros((1, sb), F32)
        for tau in range(CHUNK, 0, -1):
            d_ar = d_ar + dwr[tau] * w_r[tau - 1] + dwi[tau] * w_i[tau - 1]
            d_ai = d_ai - dwr[tau] * w_i[tau - 1] + dwi[tau] * w_r[tau - 1]
            dwr[tau - 1], dwi[tau - 1] = (dwr[tau - 1] + dwr[tau] * a_r + dwi[tau] * a_i,
                                          dwi[tau - 1] - dwr[tau] * a_i + dwi[tau] * a_r)
        dbre_ref[0] = d_bre
        dbim_ref[0] = d_bim
        dcre_ref[0] = d_cre
        dcim_ref[0] = d_cim
        da_ref[0, :, 0:sb] = d_ar
        da_ref[0, :, sb:2 * sb] = d_ai

    mat = pl.BlockSpec((1, LANES, sb), lambda g: (g, 0, 0))
    pw = pl.BlockSpec((1, CHUNK + 1, sb), lambda g: (g, 0, 0))
    one = pl.BlockSpec((1, 1, sb), lambda g: (g, 0, 0))
    two = pl.BlockSpec((1, 1, 2 * sb), lambda g: (g, 0, 0))
    big = pl.BlockSpec((1, n_rows, n_rows), lambda g: (g, 0, 0))
    big2 = pl.BlockSpec((1, n_rows, 2 * sb), lambda g: (g, 0, 0))
    mshape = jax.ShapeDtypeStruct((ngb, LANES, sb), F32)
    outs, extra = _call(
        body, name=name, grid=(ngb,),
        out_shape=[mshape, mshape, mshape, mshape, jax.ShapeDtypeStruct((ngb, 1, 2 * sb), F32)],
        in_specs=[mat, mat, mat, mat, pw, pw, one, one, big, big2, big2, two],
        out_specs=[mat, mat, mat, mat, two],
        scratch_shapes=[], args=(bre, bim, cre, cim, wr, wi, ar, ai, dt, dbp, dcp, da8), hosted=hosted)
    return (*outs, extra)


MXU_TILE = 256


def _causal_span(tile, n_tiles, reverse, of_output):
    upto, onward = slice(0, (tile + 1) * MXU_TILE), slice(tile * MXU_TILE, n_tiles * MXU_TILE)
    return (onward if reverse else upto) if of_output else (upto if reverse else onward)


def _apply_causal(uv, t_ref, reverse):
    n_tiles = uv.shape[1] // MXU_TILE
    cols = []
    for tj in range(n_tiles):
        span = _causal_span(tj, n_tiles, reverse, True)
        cols.append(_dot(uv[:, span], t_ref[0, span, tj * MXU_TILE:(tj + 1) * MXU_TILE]))
    return jnp.concatenate(cols, axis=1)


def _apply_causal_t(dyv, t_ref, reverse):
    n_tiles = dyv.shape[1] // MXU_TILE
    cols = []
    for ti in range(n_tiles):
        span = _causal_span(ti, n_tiles, reverse, False)
        cols.append(_dot_nt(dyv[:, span], t_ref[0, ti * MXU_TILE:(ti + 1) * MXU_TILE, span]))
    return jnp.concatenate(cols, axis=1)


def _shift_rows(xv, edge, rows, n_rows, down):
    if down:
        return jnp.where(rows == 0, edge, pltpu.roll(xv, 1, 0))
    return jnp.where(rows == n_rows - 1, edge, pltpu.roll(xv, n_rows - 1, 0))


def _rows_of_tokens(tok_ref, conv_scr, rb):
    conv_scr[...] = tok_ref[0].astype(F32)
    return jnp.concatenate([conv_scr[pl.ds(j, rb, stride=CHUNK), :] for j in range(CHUNK)], axis=1).astype(BF16)


def _tokens_of_rows(val, tok_ref, conv_scr, rb):
    for j in range(CHUNK):
        conv_scr[pl.ds(j, rb, stride=CHUNK), :] = val[:, j * LANES:(j + 1) * LANES]
    tok_ref[0] = conv_scr[...].astype(BF16)


def _s5_row_block(n_seq, target=416):
    n_rows = n_seq // CHUNK
    best = 16
    for rb in range(16, min(target, n_rows) + 1, 16):
        if n_rows % rb == 0:
            best = rb
    assert n_rows % best == 0
    return best


def _s5_fwd(useq, t_op, bp, cp, tab, *, reverse, name):
    ngb, n_seq, _ = useq.shape
    sb = bp.shape[2] // 2
    width = CHUNK * LANES
    rb = _s5_row_block(n_seq)
    tbk = rb * CHUNK
    steps = n_seq // tbk

    def blk(i):
        return (steps - 1 - i) if reverse else i

    def body(u_ref, t_ref, b_ref, c_ref, tab_ref, y_ref, hp_ref, h_scr, conv_scr, carry_scr):
        i = pl.program_id(1)

        @pl.when(i == 0)
        def _():
            carry_scr[...] = jnp.zeros_like(carry_scr)

        enter = carry_scr[0:1, :]
        uv = _rows_of_tokens(u_ref, conv_scr, rb)
        h_scr[...] = _dot(uv, b_ref[0])
        cr, ci = _scan_block(h_scr, tab_ref.at[0], carry_scr[:, 0:sb], carry_scr[:, sb:2 * sb],
                             reverse=reverse, tb=rb, sb=sb)
        carry_scr[:, 0:sb] = cr
        carry_scr[:, sb:2 * sb] = ci
        rows = lax.broadcasted_iota(jnp.int32, (rb, 1), 0)
        hprev = _shift_rows(h_scr[...], enter, rows, rb, down=not reverse)
        hp_ref[0] = hprev
        _tokens_of_rows(_apply_causal(uv, t_ref, reverse) + _dot_nt(hprev.astype(BF16), c_ref[0]), y_ref, conv_scr, rb)

    op = pl.BlockSpec((1, width, width), lambda g, i: (g, 0, 0))
    op2 = pl.BlockSpec((1, width, 2 * sb), lambda g, i: (g, 0, 0))
    tok = pl.BlockSpec((1, tbk, LANES), lambda g, i: (g, blk(i), 0))
    return pl.pallas_call(
        body, name=name, grid=(ngb, steps),
        out_shape=(jax.ShapeDtypeStruct((ngb, n_seq, LANES), BF16),
                   jax.ShapeDtypeStruct((ngb, n_seq // CHUNK, 2 * sb), F32)),
        in_specs=[tok, op, op2, op2, pl.BlockSpec((1, 8, SUBLANES, sb), lambda g, i: (g, 0, 0, 0))],
        out_specs=(tok, pl.BlockSpec((1, rb, 2 * sb), lambda g, i: (g, blk(i), 0))),
        scratch_shapes=[pltpu.VMEM((rb, 2 * sb), F32), pltpu.VMEM((tbk, LANES), F32),
                        pltpu.VMEM((SUBLANES, 2 * sb), F32)],
        compiler_params=_params(2),
    )(useq, t_op, bp, cp, tab)


def _s5_bwd(useq, dy, hprev, t_op, bp, cp, tab_adj, *, reverse, name, hosted=None):
    ngb, n_seq, _ = useq.shape
    sb = bp.shape[2] // 2
    width = CHUNK * LANES
    rb = _s5_row_block(n_seq)
    tbk = rb * CHUNK
    steps = n_seq // tbk

    def blk(i):
        return i if reverse else steps - 1 - i

    def body(u_ref, dy_ref, hp_ref, t_ref, b_ref, c_ref, taba_ref, du_ref, dt_ref, db_ref, dc_ref, da_ref,
             lam_scr, conv_scr, lcarry_scr, gedge_scr, da_scr):
        i = pl.program_id(1)
        first = i == 0

        _zero_first(first, lcarry_scr, gedge_scr, da_scr, dt_ref, db_ref, dc_ref)
        rows = lax.broadcasted_iota(jnp.int32, (rb, 1), 0)
        uv = _rows_of_tokens(u_ref, conv_scr, rb)
        dyv = _rows_of_tokens(dy_ref, conv_scr, rb)
        gy = _dot(dyv, c_ref[0])
        edge = gy[rb - 1:rb, :] if reverse else gy[0:1, :]
        lam_scr[...] = _shift_rows(gy, gedge_scr[...], rows, rb, down=reverse)
        gedge_scr[...] = edge
        lr, li = _scan_block(lam_scr, taba_ref.at[0], lcarry_scr[:, 0:sb], lcarry_scr[:, sb:2 * sb],
                             reverse=not reverse, tb=rb, sb=sb)
        lcarry_scr[:, 0:sb] = lr
        lcarry_scr[:, sb:2 * sb] = li

        lam = lam_scr[...]
        lam_bf = lam.astype(BF16)
        _tokens_of_rows(_apply_causal_t(dyv, t_ref, reverse) + _dot_nt(lam_bf, b_ref[0]), du_ref, conv_scr, rb)
        for tj in range(width // MXU_TILE):
            span, cols = _causal_span(tj, width // MXU_TILE, reverse, True), slice(tj * MXU_TILE, (tj + 1) * MXU_TILE)
            _acc(dt_ref.at[0, span, cols], _dot_tn(uv[:, span], dyv[:, cols]))
        _acc(db_ref.at[0], _dot_tn(uv, lam_bf))
        _acc(dc_ref.at[0], _dot_tn(dyv, hp_ref[0].astype(BF16)))
        lam_r, lam_i = lam[:, 0:sb], lam[:, sb:2 * sb]
        hp_r, hp_i = hp_ref[0, :, 0:sb], hp_ref[0, :, sb:2 * sb]
        da_scr[:, 0:sb] += _colsum(lam_r * hp_r + lam_i * hp_i)
        da_scr[:, sb:2 * sb] += _colsum(lam_i * hp_r - lam_r * hp_i)

        @pl.when(i == steps - 1)
        def _():
            da_ref[0] = da_scr[...]

    op = pl.BlockSpec((1, width, width), lambda g, i: (g, 0, 0))
    op2 = pl.BlockSpec((1, width, 2 * sb), lambda g, i: (g, 0, 0))
    tabs = pl.BlockSpec((1, 8, SUBLANES, sb), lambda g, i: (g, 0, 0, 0))
    tok = pl.BlockSpec((1, tbk, LANES), lambda g, i: (g, blk(i), 0))
    outs, extra = _call(
        body, name=name, grid=(ngb, steps),
        out_shape=[jax.ShapeDtypeStruct((ngb, n_seq, LANES), BF16),
                   jax.ShapeDtypeStruct((ngb, width, width), F32),
                   jax.ShapeDtypeStruct((ngb, width, 2 * sb), F32),
                   jax.ShapeDtypeStruct((ngb, width, 2 * sb), F32),
                   jax.ShapeDtypeStruct((ngb, 1, 2 * sb), F32)],
        in_specs=[tok, tok, pl.BlockSpec((1, rb, 2 * sb), lambda g, i: (g, blk(i), 0)), op, op2, op2, tabs],
        out_specs=[tok, op, op2, op2, pl.BlockSpec((1, 1, 2 * sb), lambda g, i: (g, 0, 0))],
        scratch_shapes=[pltpu.VMEM((rb, 2 * sb), F32), pltpu.VMEM((tbk, LANES), F32),
                        pltpu.VMEM((SUBLANES, 2 * sb), F32), pltpu.VMEM((1, 2 * sb), F32), pltpu.VMEM((1, 2 * sb), F32)],
        args=(useq, dy, hprev, t_op, bp, cp, tab_adj), hosted=hosted)
    return (*outs, extra)


def _glu_loss(useq, yf, yb, z, xhat0, ln0, gt, d_vec, w_glu, b_glu, w_out, ln1, target, *, offs, dy_rows, tb, name):
    ngb = useq.shape[0]
    n_tok, d_model = xhat0.shape
    e = ngb * LANES
    tb = min(tb, n_tok)
    assert all(off % tb == 0 for off in offs) and all(off % tb == 0 for _, off in dy_rows)
    nz = z.shape[0]

    def body(u_ref, yf_ref, yb_ref, z_ref, xh0_ref, g0_ref, b0_ref, gt_ref, d_ref, wg_ref, bg_ref, wo_ref, g1_ref,
             b1_ref, t_ref, loss_ref, dxr_ref, do_ref, gz_ref, gg_ref, dq_ref, dz_ref, dyf_ref, dyb_ref, dg1_ref, db1_ref,
             dgt_ref, dbg_ref, dd_ref, loss_scr, yl_scr, th_scr, s_scr, dg_scr):
        i = pl.program_id(0)
        _zero_first(i == 0, loss_scr, dg1_ref, db1_ref, dgt_ref, dbg_ref, dd_ref)
        zw = e // nz
        cs = min(512, zw)

        def z_slab(c0):
            return z_ref[c0 // zw, :, c0 % zw:c0 % zw + cs].astype(F32)

        for q in range(ngb):
            sl = slice(q * LANES, (q + 1) * LANES)
            yl = d_ref[:, sl] * u_ref[q].astype(F32) + yf_ref[q].astype(F32) + yb_ref[q].astype(F32)
            th = jnp.tanh(GELU_K * (yl + GELU_C * yl * yl * yl))
            yl_scr[:, sl] = yl
            th_scr[:, sl] = th
            gg_ref[:, sl] = (0.5 * yl * (1.0 + th)).astype(BF16)
        g_all = gg_ref[...]
        for c0 in range(0, e, cs):
            sl = slice(c0, c0 + cs)
            s = _sigmoid(_dot(g_all, wg_ref[:, sl]) + bg_ref[:, sl])
            s_scr[:, sl] = s
            zf = z_slab(c0)
            g2 = 0.5 * yl_scr[:, sl] * (1.0 + th_scr[:, sl]) * s
            gz_ref[:, sl] = (g2 * (zf * _sigmoid(zf))).astype(BF16)
        o = _dot(gz_ref[...], wo_ref[...])
        x1 = xh0_ref[...] * g0_ref[...] + b0_ref[...]
        r = DN_ALPHA * x1 + gt_ref[...] * o
        rc = r - _rowmean(r)
        rstd = lax.rsqrt(_rowmean(rc * rc) + LN_EPS)
        xh = rc * rstd
        err = xh * g1_ref[...] + b1_ref[...] - t_ref[...]
        _acc(loss_scr, _colsum(err * err))
        dy = err * (1.0 / d_model)
        _acc(dg1_ref, _colsum(dy * xh))
        _acc(db1_ref, _colsum(dy))
        dxh = dy * g1_ref[...]
        dr = rstd * (dxh - _rowmean(dxh) - xh * _rowmean(dxh * xh))
        dxr_ref[...] = DN_ALPHA * dr
        _acc(dgt_ref, _colsum(dr * o))
        do_bf = (dr * gt_ref[...]).astype(BF16)
        do_ref[...] = do_bf
        for c0 in range(0, e, cs):
            sl = slice(c0, c0 + cs)
            dgz = _dot_nt(do_bf, wo_ref[sl, :])
            zf = z_slab(c0)
            sz = _sigmoid(zf)
            g = 0.5 * yl_scr[:, sl] * (1.0 + th_scr[:, sl])
            s = s_scr[:, sl]
            dg2 = dgz * (zf * sz)
            dz_ref[:, sl] = (dgz * (g * s) * (sz * (1.0 + zf * (1.0 - sz)))).astype(BF16)
            dq = dg2 * g * s * (1.0 - s)
            _acc(dbg_ref.at[:, sl], _colsum(dq))
            dq_ref[:, sl] = dq.astype(BF16)
            dg_scr[:, sl] = dg2 * s
        dq_all = dq_ref[...]
        for c0 in range(0, e, cs):
            dg_blk = dg_scr[:, c0:c0 + cs] + _dot_nt(dq_all, wg_ref[c0:c0 + cs, :])
            for q in range(c0 // LANES, (c0 + cs) // LANES):
                sl = slice(q * LANES, (q + 1) * LANES)
                yl = yl_scr[:, sl]
                th = th_scr[:, sl]
                dgelu = 0.5 * (1.0 + th) + 0.5 * yl * (1.0 - th * th) * (GELU_K * (1.0 + 3.0 * GELU_C * yl * yl))
                dyl = dg_blk[:, q * LANES - c0:(q + 1) * LANES - c0] * dgelu
                _acc(dd_ref.at[:, sl], _colsum(dyl * u_ref[q].astype(F32)))
                dyf_ref[q] = dyl.astype(BF16)
                dyb_ref[q] = dyl.astype(BF16)

        @pl.when(i == pl.num_programs(0) - 1)
        def _():
            loss_ref[...] = (0.5 / d_model) * jnp.sum(loss_scr[...], axis=1, keepdims=True)

    vec = pl.BlockSpec((1, d_model), lambda i: (0, 0))
    evec = pl.BlockSpec((1, e), lambda i: (0, 0))
    tok = pl.BlockSpec((tb, d_model), lambda i: (i, 0))
    wide = pl.BlockSpec((tb, e), lambda i: (i, 0))
    def gblk(off):
        return pl.BlockSpec((ngb, tb, LANES), functools.partial(lambda i, ob: (0, i + ob, 0), ob=off // tb))

    once = dict(pipeline_mode=pl.Buffered(1))
    tok_f = jax.ShapeDtypeStruct((n_tok, d_model), F32)
    tok_b = jax.ShapeDtypeStruct((n_tok, d_model), BF16)
    wide_b = jax.ShapeDtypeStruct((n_tok, e), BF16)
    vec_f = jax.ShapeDtypeStruct((1, d_model), F32)
    evec_f = jax.ShapeDtypeStruct((1, e), F32)
    return pl.pallas_call(
        body, name=name, grid=(n_tok // tb,),
        out_shape=(jax.ShapeDtypeStruct((1, 1), F32), tok_f, tok_b, wide_b, wide_b, wide_b, wide_b,
                   *[jax.ShapeDtypeStruct((ngb, total, LANES), BF16) for total, _ in dy_rows],
                   vec_f, vec_f, vec_f, evec_f, evec_f),
        in_specs=[gblk(offs[0]), gblk(offs[1]), gblk(offs[2]),
                  pl.BlockSpec((nz, tb, e // nz), lambda i: (0, i, 0)), tok, vec, vec, vec, evec,
                  pl.BlockSpec((e, e), lambda i: (0, 0), **once), evec,
                  pl.BlockSpec((e, d_model), lambda i: (0, 0), **once), vec, vec, tok],
        out_specs=(pl.BlockSpec((1, 1), lambda i: (0, 0)), tok, tok, wide, wide, wide, wide,
                   *[gblk(off) for _, off in dy_rows], vec, vec, vec, evec, evec),
        scratch_shapes=[pltpu.VMEM((1, d_model), F32)] + [pltpu.VMEM((tb, e), F32)] * 4,
        compiler_params=_params(1),
    )(useq, yf, yb, z, xhat0, ln0[0], ln0[1], gt, d_vec, w_glu, b_glu, w_out, ln1[0], ln1[1], target)


def _ssm_inbwd(duf, dub, w, xhat, rstd, ln, sc, gt_prev, f_prev, *, lat, row_f, row_b, tb, name):
    ngb = duf.shape[0]
    e = ngb * LANES
    n_tok, d_model = xhat.shape
    tb = min(tb, n_tok)
    obf, obb = row_f // tb, row_b // tb
    has_lat = lat is not None
    n_w = w.shape[0] if has_lat else w.shape[0] // 2

    def body(*refs):
        if has_lat:
            (duf_ref, dub_ref, dyl_ref, dz_ref, d_ref, dxr_ref, w_ref, xh_ref, rs_ref, g_ref, b_ref, sc_ref, gt_ref,
             f_ref, dp_ref, dr_ref, df_ref, dsc_ref, dsh_ref, dg_ref, db_ref, dgt_ref) = refs
        else:
            (duf_ref, dub_ref, w_ref, xh_ref, rs_ref, g_ref, b_ref, sc_ref, gt_ref, f_ref, dp_ref, dr_ref, df_ref,
             dsc_ref, dsh_ref, dg_ref, db_ref, dgt_ref) = refs
        _zero_first(pl.program_id(0) == 0, dsc_ref, dsh_ref, dg_ref, db_ref, dgt_ref)
        du = (jnp.concatenate([duf_ref[q] for q in range(ngb)], axis=1).astype(F32)
              + jnp.concatenate([dub_ref[q] for q in range(ngb)], axis=1).astype(F32))
        if has_lat:
            du = du + d_ref[...] * jnp.concatenate([dyl_ref[q] for q in range(ngb)], axis=1).astype(F32)
            dp_ref[:, e:2 * e] = dz_ref[...]
        else:
            dp_ref[:, e:2 * e] = jnp.zeros((tb, e), BF16)
        dp_ref[:, 0:e] = du.astype(BF16)
        dh = jnp.zeros((tb, d_model), F32)
        for j in range(n_w):
            dh = dh + _dot(dp_ref[:, j * d_model:(j + 1) * d_model], w_ref[j])
        xh = xh_ref[...]
        x1 = xh * g_ref[...] + b_ref[...]
        dx1 = dh * (1.0 + sc_ref[...])
        if has_lat:
            dx1 = dx1 + dxr_ref[...]
        _acc(dsc_ref, _colsum(dh * x1))
        _acc(dsh_ref, _colsum(dh))
        _acc(dg_ref, _colsum(dx1 * xh))
        _acc(db_ref, _colsum(dx1))
        dxh = dx1 * g_ref[...]
        dr = rs_ref[...] * (dxh - _rowmean(dxh) - xh * _rowmean(dxh * xh))
        dr_ref[...] = dr
        df_ref[...] = (dr * gt_ref[...]).astype(BF16)
        _acc(dgt_ref, _colsum(dr * f_ref[...].astype(F32)))

    vec = pl.BlockSpec((1, d_model), lambda i: (0, 0))
    tok = pl.BlockSpec((tb, d_model), lambda i: (i, 0))
    gblk = pl.BlockSpec((ngb, tb, LANES), lambda i: (0, i, 0))
    in_specs = [pl.BlockSpec((ngb, tb, LANES), lambda i: (0, i + obf, 0)),
                pl.BlockSpec((ngb, tb, LANES), lambda i: (0, i + obb, 0))]
    args = [duf, dub]
    if has_lat:
        in_specs += [gblk, pl.BlockSpec((tb, e), lambda i: (i, 0)), pl.BlockSpec((1, e), lambda i: (0, 0)), tok]
        args += list(lat)
    in_specs += [pl.BlockSpec(w.shape, lambda i: (0, 0, 0)), tok, pl.BlockSpec((tb, 1), lambda i: (i, 0)), vec, vec, vec,
                 vec, tok]
    args += [w, xhat, rstd, ln[0], ln[1], sc, gt_prev, f_prev]
    vec_f = jax.ShapeDtypeStruct((1, d_model), F32)
    return pl.pallas_call(
        body, name=name, grid=(n_tok // tb,),
        out_shape=(jax.ShapeDtypeStruct((n_tok, 2 * e), BF16), jax.ShapeDtypeStruct((n_tok, d_model), F32),
                   jax.ShapeDtypeStruct((n_tok, d_model), BF16), vec_f, vec_f, vec_f, vec_f, vec_f),
        in_specs=in_specs,
        out_specs=(pl.BlockSpec((tb, 2 * e), lambda i: (i, 0)), tok, tok, vec, vec, vec, vec, vec),
        compiler_params=_params(1),
    )(*args)


def _conv_bwd_a(df, w_out_t, p, yc, *, tb, name, hosted=None):
    _, n_tok, e = p.shape
    d_model = df.shape[1]
    tb = min(tb, n_tok)
    cs = _slab_width(e)

    def body(df_ref, wo_ref, bg_ref, z_ref, yc_ref, dbg_ref, dz_ref, dyc_ref):
        dfv = df_ref[...]
        for c0 in range(0, e, cs):
            sl = slice(c0, c0 + cs)
            dgv = _dot(dfv, wo_ref[:, sl])
            zf = z_ref[0, :, sl].astype(F32)
            sz = _sigmoid(zf)
            silu_z = zf * sz
            bg = bg_ref[0, :, sl].astype(F32)
            yc = yc_ref[:, sl].astype(F32)
            dbg_ref[:, sl] = (dgv * yc * silu_z).astype(BF16)
            dyc_ref[:, sl] = (dgv * bg * silu_z).astype(BF16)
            dz_ref[:, sl] = (dgv * bg * yc * (sz * (1.0 + zf * (1.0 - sz)))).astype(BF16)

    wide = pl.BlockSpec((tb, e), lambda i: (i, 0))
    shape = jax.ShapeDtypeStruct((n_tok, e), BF16)
    outs, extra = _call(
        body, name=name, grid=(n_tok // tb,), out_shape=[shape, shape, shape],
        in_specs=[pl.BlockSpec((tb, d_model), lambda i: (i, 0)), pl.BlockSpec((d_model, e), lambda i: (0, 0)),
                  pl.BlockSpec((1, tb, e), lambda i: (0, i, 0)), pl.BlockSpec((1, tb, e), lambda i: (3, i, 0)), wide],
        out_specs=[wide, wide, wide], scratch_shapes=[], args=(df, w_out_t, p, p, yc), hosted=hosted)
    return (*outs, extra)


def _conv_bwd_b(dyc, p, dbg, dz, conv_w, *, grid_mode, tb, name, hosted=None):
    _, n_tok, e = p.shape
    eh = e // 2
    if not grid_mode:
        tb = n_tok
    tb = min(tb, n_tok)
    nb = n_tok // tb
    hb = tb // GRID_W
    cs = _slab_width(e)

    def body(*refs):
        if grid_mode:
            dyc_ref, dycp_ref, dycn_ref, cg_ref, v_ref, dbg_ref, dz_ref, cw_ref, dp_ref, dcw_ref = refs
        else:
            dyc_ref, cg_ref, v_ref, dbg_ref, dz_ref, cw_ref, dp_ref, dcw_ref = refs
        i = pl.program_id(0)
        _zero_first(i == 0, dcw_ref)
        rows = lax.broadcasted_iota(jnp.int32, (tb, 1), 0)
        dp_ref[0] = dbg_ref[...]
        dp_ref[3] = dz_ref[...]
        for c0 in range(0, e, cs):
            sl = slice(c0, c0 + cs)
            dyc = dyc_ref[:, sl].astype(F32)
            w = cw_ref[:, sl]
            if grid_mode and c0 >= eh:
                hs = slice(c0 - eh, c0 - eh + cs)
                dprev = jnp.where(i > 0, dycp_ref[:, hs].astype(F32), 0.0)
                dnext = jnp.where(i < nb - 1, dycn_ref[:, hs].astype(F32), 0.0)
                if tb > GRID_W:
                    dm = jnp.concatenate([dprev, dyc[:tb - GRID_W]], axis=0)
                    dpl = jnp.concatenate([dyc[GRID_W:], dnext], axis=0)
                else:
                    dm, dpl = dprev, dnext
            else:
                dm, dpl = _shifted(dyc, rows, GRID_W if grid_mode else tb, tb)
            cg = cg_ref[0, :, sl].astype(F32)
            v = v_ref[0, :, sl].astype(F32)
            u = cg * v
            du = w[0:1] * dpl + w[1:2] * dyc + w[2:3] * dm
            dp_ref[1, :, sl] = (du * v).astype(BF16)
            dp_ref[2, :, sl] = (du * cg).astype(BF16)
            _acc(dcw_ref.at[:, sl], jnp.concatenate([_colsum(u * dpl), _colsum(u * dyc), _colsum(u * dm)], axis=0))

    n_hrows = n_tok // GRID_W
    wide = pl.BlockSpec((tb, e), lambda i: (i, 0))
    in_specs = [wide]
    args = [dyc]
    if grid_mode:
        in_specs += [pl.BlockSpec((GRID_W, eh), lambda i: (jnp.maximum(i * hb - 1, 0), 1)),
                     pl.BlockSpec((GRID_W, eh), lambda i: (jnp.minimum((i + 1) * hb, n_hrows - 1), 1))]
        args += [dyc, dyc]
    in_specs += [pl.BlockSpec((1, tb, e), lambda i: (1, i, 0)), pl.BlockSpec((1, tb, e), lambda i: (2, i, 0)), wide, wide,
                 pl.BlockSpec((3, e), lambda i: (0, 0))]
    args += [p, p, dbg, dz, conv_w]
    outs, extra = _call(
        body, name=name, grid=(nb,),
        out_shape=[jax.ShapeDtypeStruct((4, n_tok, e), BF16), jax.ShapeDtypeStruct((3, e), F32)],
        in_specs=in_specs,
        out_specs=[pl.BlockSpec((4, tb, e), lambda i: (0, i, 0)), pl.BlockSpec((3, e), lambda i: (0, 0))],
        scratch_shapes=[], args=args, hosted=hosted)
    return (*outs, extra)


def _conv_inbwd(dp, w, dr, x, sc, *, tb, name, hosted=None):
    n_chunks, n_tok, e = dp.shape
    d_model = x.shape[1]
    tb = min(tb, n_tok)

    def body(dp_ref, w_ref, dr_ref, x_ref, sc_ref, gx_ref, dsc_ref, dsh_ref):
        _zero_first(pl.program_id(0) == 0, dsc_ref, dsh_ref)
        dh = _dot(dp_ref[0], w_ref[0])
        for k in range(1, n_chunks):
            dh = dh + _dot(dp_ref[k], w_ref[k])
        gx_ref[...] = DN_ALPHA * dr_ref[...] + dh * (1.0 + sc_ref[...])
        _acc(dsc_ref, _colsum(dh * x_ref[...]))
        _acc(dsh_ref, _colsum(dh))

    vec = pl.BlockSpec((1, d_model), lambda i: (0, 0))
    tok = pl.BlockSpec((tb, d_model), lambda i: (i, 0))
    vec_f = jax.ShapeDtypeStruct((1, d_model), F32)
    outs, extra = _call(
        body, name=name, grid=(n_tok // tb,),
        out_shape=[jax.ShapeDtypeStruct((n_tok, d_model), F32), vec_f, vec_f],
        in_specs=[pl.BlockSpec((n_chunks, tb, e), lambda i: (0, i, 0)),
                  pl.BlockSpec((n_chunks, e, d_model), lambda i: (0, 0, 0), pipeline_mode=pl.Buffered(1)),
                  tok, tok, vec],
        out_specs=[tok, vec, vec],
        scratch_shapes=[], args=(dp, w, dr, x, sc), hosted=hosted)
    return (*outs, extra)


def _wgrad(a, b, *, n_chunks, tm, tl, init=None, name):
    n_tok, m = a.shape
    tl = min(tl, n_tok)
    chunked = b.ndim == 3
    cw = b.shape[2] if chunked else b.shape[1] // n_chunks
    has_init = init is not None

    def body(*refs):
        if has_init:
            a_ref, b_ref, init_ref, o_ref = refs
        else:
            a_ref, b_ref, o_ref = refs
        @pl.when(pl.program_id(2) == 0)
        def _():
            o_ref[0] = init_ref[0] if has_init else jnp.zeros_like(o_ref[0])

        o_ref[0] += _dot_tn(a_ref[...], b_ref[0] if chunked else b_ref[...])

    o_spec = pl.BlockSpec((1, tm, cw), lambda jm, jc, l: (jc, jm, 0))
    b_spec = (pl.BlockSpec((1, tl, cw), lambda jm, jc, l: (jc, l, 0)) if chunked
              else pl.BlockSpec((tl, cw), lambda jm, jc, l: (l, jc)))
    init_spec = pl.BlockSpec((1, tm, cw), lambda jm, jc, l: (jc, jm, 0), pipeline_mode=pl.Buffered(1))
    in_specs = [pl.BlockSpec((tl, tm), lambda jm, jc, l: (l, jm)), b_spec] + ([init_spec] if has_init else [])
    args = (a, b) + ((init,) if has_init else ())
    return pl.pallas_call(
        body, name=name, grid=(m // tm, n_chunks, n_tok // tl),
        out_shape=jax.ShapeDtypeStruct((n_chunks, m, cw), F32),
        in_specs=in_specs, out_specs=o_spec, compiler_params=_params(3),
    )(*args)


def _block_diag(t, ngb):
    g, p, n = t.shape
    gpb = g // ngb
    eye = jnp.eye(gpb, dtype=t.dtype)
    return jnp.einsum("bgpn,gh->bgphn", t.reshape(ngb, gpb, p, n), eye).reshape(ngb, gpb * p, gpb * n)


def _block_diag_t(mat, g, p, n):
    ngb = mat.shape[0]
    gpb = g // ngb
    eye = jnp.eye(gpb, dtype=mat.dtype)
    return jnp.einsum("bgphn,gh->bgpn", mat.reshape(ngb, gpb, p, gpb, n), eye).reshape(g, p, n)


def _scan_tables(pw_r, pw_i, ngb, reverse):
    _, g, n = pw_r.shape
    sb = g * n // ngb
    rows = jnp.arange(SUBLANES)
    kinds = []
    for step in (1, 2, 4):
        mask = ((rows < SUBLANES - step) if reverse else (rows >= step)).astype(F32)
        for part in (pw_r[step - 1], pw_i[step - 1]):
            kinds.append(part.reshape(ngb, 1, sb) * mask[None, :, None])
    for part in (pw_r, pw_i):
        pw = part[::-1] if reverse else part
        kinds.append(jnp.transpose(pw.reshape(SUBLANES, ngb, sb), (1, 0, 2)))
    return jnp.stack(kinds, axis=1)


def _flat(parts):
    return jnp.concatenate([p.reshape(-1) for p in parts])


def _unflat(vec, shapes):
    out, off = [], 0
    for s in shapes:
        size = math.prod(s)
        out.append(vec[off:off + size].reshape(s))
        off += size
    return out


def kernel(x, c, ctx, c_ctx, ada_w, ada_b, ln_g, ln_b, conv_w_in, conv_w, conv_w_out, ssm_w_in, ssm_lam_re, ssm_lam_im, ssm_log_step, ssm_b_re, ssm_b_im, ssm_c_re, ssm_c_im, ssm_d, ssm_w_glu, ssm_b_glu, ssm_w_out, loss_target, m_c_ctx, m_ada_w, m_ada_b, m_ln_g, m_ln_b, m_conv_w_in, m_conv_w, m_conv_w_out, m_ssm_w_in, m_ssm_lam_re, m_ssm_lam_im, m_ssm_log_step, m_ssm_b_re, m_ssm_b_im, m_ssm_c_re, m_ssm_c_im, m_ssm_d, m_ssm_w_glu, m_ssm_b_glu, m_ssm_w_out, v_c_ctx, v_ada_w, v_ada_b, v_ln_g, v_ln_b, v_conv_w_in, v_conv_w, v_conv_w_out, v_ssm_w_in, v_ssm_lam_re, v_ssm_lam_im, v_ssm_log_step, v_ssm_b_re, v_ssm_b_im, v_ssm_c_re, v_ssm_c_im, v_ssm_d, v_ssm_w_glu, v_ssm_b_glu, v_ssm_w_out):
    weights = dict(c_ctx=c_ctx, ada_w=ada_w, ada_b=ada_b, ln_g=ln_g, ln_b=ln_b, conv_w_in=conv_w_in, conv_w=conv_w,
                   conv_w_out=conv_w_out, ssm_w_in=ssm_w_in, ssm_lam_re=ssm_lam_re, ssm_lam_im=ssm_lam_im,
                   ssm_log_step=ssm_log_step, ssm_b_re=ssm_b_re, ssm_b_im=ssm_b_im, ssm_c_re=ssm_c_re,
                   ssm_c_im=ssm_c_im, ssm_d=ssm_d, ssm_w_glu=ssm_w_glu, ssm_b_glu=ssm_b_glu, ssm_w_out=ssm_w_out)
    mom_m = dict(c_ctx=m_c_ctx, ada_w=m_ada_w, ada_b=m_ada_b, ln_g=m_ln_g, ln_b=m_ln_b, conv_w_in=m_conv_w_in,
                 conv_w=m_conv_w, conv_w_out=m_conv_w_out, ssm_w_in=m_ssm_w_in, ssm_lam_re=m_ssm_lam_re,
                 ssm_lam_im=m_ssm_lam_im, ssm_log_step=m_ssm_log_step, ssm_b_re=m_ssm_b_re, ssm_b_im=m_ssm_b_im,
                 ssm_c_re=m_ssm_c_re, ssm_c_im=m_ssm_c_im, ssm_d=m_ssm_d, ssm_w_glu=m_ssm_w_glu,
                 ssm_b_glu=m_ssm_b_glu, ssm_w_out=m_ssm_w_out)
    mom_v = dict(c_ctx=v_c_ctx, ada_w=v_ada_w, ada_b=v_ada_b, ln_g=v_ln_g, ln_b=v_ln_b, conv_w_in=v_conv_w_in,
                 conv_w=v_conv_w, conv_w_out=v_conv_w_out, ssm_w_in=v_ssm_w_in, ssm_lam_re=v_ssm_lam_re,
                 ssm_lam_im=v_ssm_lam_im, ssm_log_step=v_ssm_log_step, ssm_b_re=v_ssm_b_re, ssm_b_im=v_ssm_b_im,
                 ssm_c_re=v_ssm_c_re, ssm_c_im=v_ssm_c_im, ssm_d=v_ssm_d, ssm_w_glu=v_ssm_w_glu,
                 ssm_b_glu=v_ssm_b_glu, ssm_w_out=v_ssm_w_out)
    names = list(weights)

    n_lat, d_model = x.shape[1], x.shape[2]
    n_ctx = ctx.shape[1]
    e = 2 * d_model
    n_grp, n_state, grp = ssm_lam_re.shape[2], ssm_lam_re.shape[3], ssm_b_re.shape[4]
    ngb = e // LANES
    ws = ada_w.shape[2]
    tb_tok = min(512, n_lat)
    n_seq = n_ctx + n_lat
    tb_glu = math.gcd(256, n_ctx)
    chip = 2 * lax.axis_index("x") + lax.axis_index("y")
    me = 2 * chip + lax.axis_index("c")
    chips, everyone, pair = ("x", "y"), MESH_AXES, ("c",)

    x2, ctx2, tgt2 = x[0], ctx[0], loss_target[0]

    wc_in_own = conv_w_in[0].astype(BF16)
    later_weights = _Hosted([(w[0].astype(BF16), chips, False) for w in (conv_w_out, ssm_w_in, ssm_w_glu, ssm_w_out)])
    small_full = _exchange(_flat([conv_w[0], ssm_d[0], ssm_b_glu[0]]).reshape(1, -1), chips, False, "ag_small")
    es = conv_w.shape[2]
    conv_w_full = jnp.transpose(small_full[:, 0, :3 * es].reshape(4, 3, es), (1, 0, 2)).reshape(3, e)
    d_full = small_full[:, 0, 3 * es:4 * es].reshape(1, e)
    b_glu_full = small_full[:, 0, 4 * es:5 * es].reshape(1, e)

    c_all = _exchange(c, everyone, False, "ag_c").reshape(8, d_model)
    cc2 = c_ctx.reshape(1, d_model)
    b_sh = lax.dynamic_slice_in_dim(ada_b, chip * ws, ws, axis=1).reshape(DEPTH, 1, ws)
    m_sh = _ada_fwd(c_all, cc2, ada_w, b_sh)
    m_all = _exchange(m_sh, chips, False, "ag_mod")
    m_full = jnp.transpose(m_all, (1, 2, 0, 3)).reshape(DEPTH, 16, 3 * d_model)
    m_lat = lax.dynamic_slice_in_dim(m_full, me, 1, axis=1)
    m_ctx = m_full[:, 8:9]

    def mods(m, i):
        return m[i, :, 0:d_model], m[i, :, d_model:2 * d_model], m[i, :, 2 * d_model:3 * d_model]

    sh0, sc0, gt0 = mods(m_lat, 0)
    sh1, sc1, gt1 = mods(m_lat, 1)
    shc0, scc0, gtc0 = mods(m_ctx, 0)
    shc1, scc1, _ = mods(m_ctx, 1)
    ln0 = (ln_g[0:1], ln_b[0:1])
    ln1 = (ln_g[1:2], ln_b[1:2])

    def lam_view(t):
        return jnp.transpose(t[0], (0, 2, 1)).reshape(2 * n_state, n_grp)

    def lam_back(t):
        return jnp.transpose(t.reshape(2, n_state, n_grp), (0, 2, 1)).reshape(ssm_lam_re.shape)

    def b_view(t):
        return jnp.transpose(t[0], (0, 2, 3, 1)).reshape(2 * n_state * grp, n_grp)

    def b_back(t):
        return jnp.transpose(t.reshape(2, n_state, grp, n_grp), (0, 3, 1, 2)).reshape(ssm_b_re.shape)

    def c_view(t):
        return jnp.transpose(t[0], (0, 2, 3, 1)).reshape(2 * grp * n_state, n_grp)

    def c_back(t):
        return jnp.transpose(t.reshape(2, grp, n_state, n_grp), (0, 3, 1, 2)).reshape(ssm_c_re.shape)

    def channel_major(t):
        return jnp.transpose(t.reshape(2 * n_state, grp, n_grp), (1, 0, 2))

    def by_group(t):
        return jnp.transpose(t.reshape(t.shape[0], 2, n_state, n_grp), (0, 1, 3, 2))

    lam_re2, lam_im2, log_step2 = lam_view(ssm_lam_re), lam_view(ssm_lam_im), ssm_log_step[0]
    b_re_t, b_im_t = channel_major(b_view(ssm_b_re)), channel_major(b_view(ssm_b_im))
    pw_r, pw_i, pq_r, pq_i, bbr, bbi = _zoh_fwd(lam_re2, lam_im2, log_step2, b_re_t, b_im_t)
    sbk = n_grp * n_state // ngb
    pw_r, pw_i, pq_r, pq_i = (by_group(t) for t in (pw_r, pw_i, pq_r, pq_i))
    bbr_g = jnp.transpose(by_group(bbr), (1, 2, 0, 3))
    bbi_g = jnp.transpose(by_group(bbi), (1, 2, 0, 3))

    def power_rows(pw, r, first):
        full = jnp.concatenate([jnp.full((1, n_grp, n_state), first, F32), pw[:, r]], axis=0)
        return jnp.transpose(full.reshape(CHUNK + 1, ngb, sbk), (1, 0, 2))

    s5 = []
    for r in range(2):
        prm = dict(bre=_block_diag(bbr_g[r], ngb), bim=_block_diag(bbi_g[r], ngb),
                   cre=_block_diag(ssm_c_re[0, r], ngb), cim=_block_diag(ssm_c_im[0, r], ngb),
                   wr=power_rows(pw_r, r, 1.0), wi=power_rows(pw_i, r, 0.0))
        half_rows = wc_in_own[r * (d_model // 2):(r + 1) * (d_model // 2)]
        t_op, bp_op, cp_op, (wc_in_half,) = _s5_ops(
            prm["bre"], prm["bim"], prm["cre"], prm["cim"], prm["wr"], prm["wi"], reverse=(r == 1),
            name=f"l1_s5_ops{r}", hosted=_Hosted([(half_rows, chips, False)]))
        s5.append(dict(
            prm, t=t_op, bp=bp_op, cp=cp_op, wc_in_half=wc_in_half,
            tab=_scan_tables(pq_r[:, r], pq_i[:, r], ngb, reverse=(r == 1)),
            tab_adj=_scan_tables(pq_r[:, r], -pq_i[:, r], ngb, reverse=(r == 0))))
    wc_in = jnp.concatenate([s5[0]["wc_in_half"], s5[1]["wc_in_half"]], axis=1)

    p0, h0, gathered = _inproj(x2, sc0, sh0, wc_in, tb=min(1024, n_lat), name="l0_inproj", hosted=later_weights)
    wc_out, ws_in, w_glu, ws_out = gathered
    wc_out, w_glu, ws_out = wc_out.reshape(e, d_model), w_glu.reshape(e, e), ws_out.reshape(e, d_model)
    wc_in_t, ws_in_t, wc_out_t = jnp.transpose(wc_in, (0, 2, 1)), jnp.transpose(ws_in, (0, 2, 1)), wc_out.T
    pc0, hc0 = _inproj(ctx2, scc0, shc0, wc_in, tb=tb_tok, name="l0_inproj_ctx")
    xhat0, rstd0, g0, yc0, f0 = _convgate(p0, x2, gt0, conv_w_full, wc_out, *ln0, grid_mode=True, tb=tb_tok, name="l0_conv")
    chat0, crstd0, gc0, ycc0, fc0 = _convgate(pc0, ctx2, gtc0, conv_w_full, wc_out, *ln0, grid_mode=False, tb=tb_tok,
                                              name="l0_conv_ctx")

    seq_rows = [(n_seq, n_ctx), (n_seq, 0)]
    useq_f, useq_b, h1 = _inproj_seq(xhat0, sc1, sh1, ws_in[0:2], ln0, tb=min(1024, n_lat), seq_rows=seq_rows,
                                     name="l1_inproj_u")
    z1, _ = _inproj(xhat0, sc1, sh1, ws_in[2:4], lnaff=ln0, tb=min(1024, n_lat), name="l1_inproj_z")
    uc, hc1 = _inproj(chat0, scc1, shc1, ws_in[0:2], lnaff=ln0, tb=tb_tok, gb_rows=[(n_ctx, 0)], name="l1_inproj_ctx")
    useq = [useq_f.at[:, 0:n_ctx].set(uc), useq_b.at[:, n_lat:].set(uc)]
    y_dir, hp_dir = [], []
    for r in range(2):
        yr, hcr = _s5_fwd(useq[r], s5[r]["t"], s5[r]["bp"], s5[r]["cp"], s5[r]["tab"], reverse=(r == 1),
                          name=f"l1_s5_fwd{r}")
        y_dir.append(yr)
        hp_dir.append(hcr)

    (loss, dxres, do1, gz1, gg1, dq1, dz1, dy_f, dy_b, dg1, db1, dgt1, dbglu, dd) = _glu_loss(
        useq[0], y_dir[0], y_dir[1], z1, xhat0, ln0, gt1, d_full, w_glu, b_glu_full, ws_out, ln1, tgt2,
        offs=(n_ctx, n_ctx, 0), dy_rows=seq_rows, tb=tb_glu, name="l1_glu_loss")
    no_dy = jnp.zeros((ngb, n_ctx, LANES), BF16)
    dy_dir = [dy_f.at[:, 0:n_ctx].set(no_dy), dy_b.at[:, n_lat:].set(no_dy)]

    tl = min(1024, n_lat)

    def owner_slices(name, full):
        w = weights[name]
        return full.reshape(8, math.prod(w.shape[:-1]) // 2, w.shape[-1])

    def scatter(named):
        return _Hosted([(owner_slices(name, full), everyone, True) for name, full in named])

    def siblings(names):
        return _Hosted([(_sum_parts(rs_parts[name], "sum_" + name), pair, False) for name in names])

    rs_parts, both_halves = {}, {}

    gw_glu = _wgrad(gg1, dq1, n_chunks=1, tm=e // 2, tl=min(2 * tl, n_lat), name="wg_glu")
    gw_ssm_out = _wgrad(gz1, do1, n_chunks=1, tm=e, tl=min(2 * tl, n_lat), name="wg_ssm_out")
    du_dir, s5_grads = [], []
    for r in range(2):
        if r == 0:
            hosted = scatter([("ssm_w_glu", gw_glu), ("ssm_w_out", gw_ssm_out)])
        else:
            hosted = siblings(["ssm_w_glu", "ssm_w_out"])
        dur, dt_op, dbp_op, dcp_op, da8, extra = _s5_bwd(useq[r], dy_dir[r], hp_dir[r], s5[r]["t"], s5[r]["bp"],
                                                         s5[r]["cp"], s5[r]["tab_adj"], reverse=(r == 1),
                                                         name=f"l1_s5_bwd{r}", hosted=hosted)
        if r == 0:
            rs_parts["ssm_w_glu"], rs_parts["ssm_w_out"] = extra
        else:
            both_halves["ssm_w_glu"], both_halves["ssm_w_out"] = extra
        du_dir.append(dur)
        prm = s5[r]
        s5_grads.append(functools.partial(
            _s5_ops_bwd, prm["bre"], prm["bim"], prm["cre"], prm["cim"], prm["wr"], prm["wi"], prm["wr"][:, 1:2],
            prm["wi"][:, 1:2], dt_op, dbp_op, dcp_op, da8, reverse=(r == 1), name=f"l1_s5_ops_bwd{r}"))
    dp1, dr0, df0, dsc1, dsh1, dg0, db0, dgt0 = _ssm_inbwd(
        du_dir[0], du_dir[1], ws_in_t, xhat0, rstd0, ln0, sc1, gt0, f0, lat=(dy_dir[1], dz1, d_full, dxres),
        row_f=n_ctx, row_b=0, tb=tb_glu, name="l1_inbwd")
    dpc1, drc0, dfc0, dscc1, dshc1, dgc0, dbc0, dgtc0 = _ssm_inbwd(
        du_dir[0], du_dir[1], ws_in_t, chat0, crstd0, ln0, scc1, gtc0, fc0, lat=None,
        row_f=0, row_b=n_lat, tb=n_ctx, name="l1_inbwd_ctx")

    def conv_backward(df, p, yc, dr, xin, sc, grid_mode, tag, hosted_a=None, hosted_b=None, hosted_in=None):
        dbg, dz, dyc, extra_a = _conv_bwd_a(df, wc_out_t, p, yc, tb=tb_tok, name="l0_bwd_a" + tag, hosted=hosted_a)
        dp, dcw, extra_b = _conv_bwd_b(dyc, p, dbg, dz, conv_w_full, grid_mode=grid_mode, tb=tb_glu,
                                       name="l0_bwd_b" + tag, hosted=hosted_b)
        gx, dsc, dsh, extra_in = _conv_inbwd(dp, wc_in_t, dr, xin, sc, tb=tb_tok, name="l0_inbwd" + tag,
                                             hosted=None if hosted_in is None else hosted_in(dp, extra_a + extra_b))
        return dp, dcw, gx, dsc, dsh, extra_in

    dpc0, dcwc0, _, dscc0, dshc0, _ = conv_backward(dfc0, pc0, ycc0, drc0, ctx2, scc0, False, "_ctx")
    gw_conv_out = _wgrad(g0, df0, n_chunks=1, tm=e, tl=tl, name="wg_conv_out",
                         init=_wgrad(gc0, dfc0, n_chunks=1, tm=e, tl=tl, name="wg_conv_out_ctx"))
    gw_ssm_in = _wgrad(h1, dp1, n_chunks=4, tm=d_model, tl=tl, name="wg_ssm_in",
                       init=_wgrad(hc1, dpc1, n_chunks=4, tm=d_model, tl=tl, name="wg_ssm_in_ctx"))
    gw_conv_in_ctx = _wgrad(hc0, dpc0, n_chunks=4, tm=d_model, tl=tl, name="wg_conv_in_ctx")

    def behind_inbwd(dp, arrived):
        rs_parts["conv_w_out"], rs_parts["ssm_w_in"] = arrived
        gw_conv_in = _wgrad(h0, dp, n_chunks=4, tm=d_model, tl=tl, name="wg_conv_in", init=gw_conv_in_ctx)
        both = siblings(["ssm_w_in", "conv_w_out"])
        return _Hosted(scatter([("conv_w_in", gw_conv_in)]).items + both.items)

    dp0, dcw0, grad_x, dsc0, dsh0, extra_in = conv_backward(
        df0, p0, yc0, dr0, x2, sc0, True, "", hosted_a=scatter([("conv_w_out", gw_conv_out)]),
        hosted_b=scatter([("ssm_w_in", gw_ssm_in)]), hosted_in=behind_inbwd)
    rs_parts["conv_w_in"], both_halves["ssm_w_in"], both_halves["conv_w_out"] = extra_in
    half_in = _sum_parts(rs_parts["conv_w_in"], "sum_conv_w_in")
    cut = half_in.shape[0] // 2
    *grads_r0, (top,) = s5_grads[0](hosted=_Hosted([(half_in[:cut], pair, False)]))
    *grads_r1, (bottom,) = s5_grads[1](hosted=_Hosted([(half_in[cut:], pair, False)]))
    both_halves["conv_w_in"] = jnp.concatenate([top, bottom], axis=1)
    s5_grads = [grads_r0, grads_r1]

    grads, deltas, new_m, new_v = {}, {}, {}, {}
    for name in ("ssm_w_glu", "ssm_w_out", "ssm_w_in", "conv_w_out", "conv_w_in"):
        w = weights[name]
        rows, cols = math.prod(w.shape[:-1]), w.shape[-1]
        both = both_halves[name].reshape(rows, cols)
        dlt, nm, nv = _adamw(w.reshape(rows, cols), both, mom_m[name].reshape(rows, cols),
                             mom_v[name].reshape(rows, cols), "adamw_" + name)
        grads[name], deltas[name] = both.reshape(w.shape), dlt.reshape(w.shape)
        new_m[name], new_v[name] = nm.reshape(w.shape), nv.reshape(w.shape)

    gpn = (n_grp, grp, n_state)
    small_parts = [
        jnp.concatenate([dg0 + dgc0, dg1], axis=0), jnp.concatenate([db0 + dbc0, db1], axis=0),
        dcw0 + dcwc0, dd, dbglu,
        jnp.stack([s5_grads[r][4] for r in range(2)]),
    ] + [jnp.stack([_block_diag_t(s5_grads[r][k], *gpn) for r in range(2)]) for k in range(4)] + [loss]
    small_shapes = [p.shape for p in small_parts]
    flat = _flat(small_parts)
    quantum = 8 * SUBLANES * LANES
    n_flat = -(-flat.shape[0] // quantum) * quantum
    flat = jnp.pad(flat, (0, n_flat - flat.shape[0])).reshape(8, n_flat // (8 * LANES), LANES)
    red = _sum_parts(_exchange(flat, everyone, True, "rs_small"), "sum_small")
    red = _exchange(red, everyone, False, "ag_small_grads").reshape(-1)
    g_ln_g, g_ln_b, g_conv_w, g_d, g_bglu, g_a, g_bbr, g_bbi, g_cre, g_cim, loss_sum = _unflat(red, small_shapes)

    def groups_minor(t, lead):
        return jnp.moveaxis(t, 1, -1).reshape(lead, n_grp)

    g_a = g_a.reshape(2, ngb, 2, sbk)
    dar = groups_minor(g_a[:, :, 0].reshape(2, n_grp, n_state), 2 * n_state)
    dai = groups_minor(g_a[:, :, 1].reshape(2, n_grp, n_state), 2 * n_state)
    dbbr_t = jnp.transpose(g_bbr, (2, 0, 3, 1)).reshape(grp, 2 * n_state, n_grp)
    dbbi_t = jnp.transpose(g_bbi, (2, 0, 3, 1)).reshape(grp, 2 * n_state, n_grp)
    z_lre, z_lim, z_ls, z_bre, z_bim = _zoh_bwd(lam_re2, lam_im2, log_step2, b_re_t, b_im_t, dar, dai, dbbr_t, dbbi_t)

    zero = jnp.zeros((1, d_model), F32)
    dm_rows = jnp.stack([
        jnp.stack([jnp.concatenate([dsh0, dsc0, dgt0], axis=1), jnp.concatenate([dshc0, dscc0, dgtc0], axis=1)]),
        jnp.stack([jnp.concatenate([dsh1, dsc1, dgt1], axis=1), jnp.concatenate([dshc1, dscc1, zero], axis=1)]),
    ]).reshape(DEPTH, 2, 3 * d_model)
    dm_all = _exchange(dm_rows, everyone, False, "ag_dmod")
    dm_sh = lax.dynamic_slice_in_dim(dm_all, chip * ws, ws, axis=3)
    g_ada_w, g_ada_b, ds_part = _ada_bwd(c_all, cc2, ada_w, dm_all, dm_sh)
    g_cctx = _cctx_grad(_exchange(ds_part, chips, False, "ag_dsctx"), cc2)

    grads["ada_w"] = g_ada_w
    dlt, nm, nv = _adamw(ada_w.reshape(-1, ws), g_ada_w.reshape(-1, ws), m_ada_w.reshape(-1, ws),
                         v_ada_w.reshape(-1, ws), "adamw_ada_w")
    deltas["ada_w"], new_m["ada_w"], new_v["ada_w"] = dlt.reshape(ada_w.shape), nm.reshape(ada_w.shape), nv.reshape(ada_w.shape)

    def chip_cols(full, rows):
        return lax.dynamic_slice_in_dim(full.reshape(rows, e), chip * es, es, axis=1)

    def same(t):
        return t

    def channel_minor_back(t):
        return jnp.transpose(t, (1, 0, 2)).reshape(2 * n_state * grp, n_grp)

    small = dict(
        c_ctx=(g_cctx, lambda t: t.reshape(1, d_model), lambda t: t.reshape(c_ctx.shape)),
        ada_b=(g_ada_b.reshape(ada_b.shape), same, same),
        ln_g=(g_ln_g, same, same), ln_b=(g_ln_b, same, same),
        conv_w=(chip_cols(g_conv_w, 3), lambda t: t[0], lambda t: t.reshape(conv_w.shape)),
        ssm_lam_re=(z_lre, lam_view, lam_back), ssm_lam_im=(z_lim, lam_view, lam_back),
        ssm_log_step=(z_ls, lambda t: t[0], lambda t: t.reshape(ssm_log_step.shape)),
        ssm_b_re=(channel_minor_back(z_bre), b_view, b_back), ssm_b_im=(channel_minor_back(z_bim), b_view, b_back),
        ssm_c_re=(groups_minor(g_cre, 2 * grp * n_state), c_view, c_back),
        ssm_c_im=(groups_minor(g_cim, 2 * grp * n_state), c_view, c_back),
        ssm_d=(chip_cols(g_d, 1), same, same), ssm_b_glu=(chip_cols(g_bglu, 1), same, same))
    for n, (g_view, view, back) in small.items():
        dlt, nm, nv = _adamw(view(weights[n]), g_view, view(mom_m[n]), view(mom_v[n]), "adamw_" + n)
        grads[n], deltas[n], new_m[n], new_v[n] = back(g_view), back(dlt), back(nm), back(nv)

    return (loss_sum.reshape(()), grad_x.reshape(x.shape), *[grads[n] for n in names], *[deltas[n] for n in names],
            *[new_m[n] for n in names], *[new_v[n] for n in names])
```

```python
import functools
import math

import jax
import jax.numpy as jnp
from jax import lax
from jax.experimental import pallas as pl
from jax.experimental.pallas import tpu as pltpu

F32 = jnp.float32
BF16 = jnp.bfloat16
LANES = 128
SUBLANES = 8
VMEM_LIMIT = 56 * 1024 * 1024
MESH_AXES = ("x", "y", "c")
HIGHEST = lax.Precision.HIGHEST

GRID_W = 64
LN_EPS = 1e-5
DEPTH = 2
DN_ALPHA = (2 * DEPTH) ** 0.25
ADAM_LR, ADAM_B1, ADAM_B2, ADAM_EPS, ADAM_WD, ADAM_STEP = 0.001, 0.9, 0.999, 1e-08, 0.01, 10
GELU_K = math.sqrt(2.0 / math.pi)
GELU_C = 0.044715


def _params(n_grid_axes):
    return pltpu.CompilerParams(dimension_semantics=("arbitrary",) * n_grid_axes, vmem_limit_bytes=VMEM_LIMIT)


def _dot(a, b):
    return jnp.dot(a, b, preferred_element_type=F32)


def _dot_nt(a, b):
    return lax.dot_general(a, b, (((1,), (1,)), ((), ())), preferred_element_type=F32)


def _dot_tn(a, b):
    return lax.dot_general(a, b, (((0,), (0,)), ((), ())), preferred_element_type=F32)


def _sigmoid(x):
    return 0.5 * jnp.tanh(0.5 * x) + 0.5


def _colsum(x):
    return jnp.sum(x, axis=0, keepdims=True)


def _rowmean(x):
    return jnp.mean(x, axis=-1, keepdims=True)


def _zero_first(first, *refs):
    @pl.when(first)
    def _():
        for ref in refs:
            ref[...] = jnp.zeros_like(ref)


def _acc(ref, value):
    ref[...] += value


def _exchange_copies(src_ref, out_ref, send_sems, recv_sems, own_sem, axes, all_to_all, sem0=0):
    n_peers = 2 ** len(axes)
    pos = {a: lax.axis_index(a) for a in MESH_AXES}

    def index(p):
        return sum(p[a] * (2 ** (len(axes) - 1 - i)) for i, a in enumerate(axes))

    me = index(pos)
    own = pltpu.make_async_copy(src_ref.at[me] if all_to_all else src_ref, out_ref.at[me], own_sem)
    copies = []
    for k in range(1, n_peers):
        peer = dict(pos)
        for i, a in enumerate(axes):
            if (k >> (len(axes) - 1 - i)) & 1:
                peer[a] = 1 - pos[a]
        copies.append(pltpu.make_async_remote_copy(
            src_ref=src_ref.at[index(peer)] if all_to_all else src_ref,
            dst_ref=out_ref.at[me],
            send_sem=send_sems.at[sem0 + k - 1],
            recv_sem=recv_sems.at[sem0 + k - 1],
            device_id=tuple(peer[a] for a in MESH_AXES),
            device_id_type=pl.DeviceIdType.MESH,
        ))
    return copies, own


def _exchange_shape(src, axes, all_to_all):
    block = tuple(src.shape[1:] if all_to_all else src.shape)
    return jax.ShapeDtypeStruct((2 ** len(axes),) + block, src.dtype)


def _exchange(src, axes, all_to_all, name):
    n_peers = 2 ** len(axes)

    def body(src_ref, out_ref, send_sems, recv_sems, own_sem):
        copies, own = _exchange_copies(src_ref, out_ref, send_sems, recv_sems, own_sem, axes, all_to_all)
        own.start()
        for cp in copies:
            cp.start()
        for cp in copies:
            cp.wait()
        own.wait()

    return pl.pallas_call(
        body,
        name=name,
        out_shape=_exchange_shape(src, axes, all_to_all),
        in_specs=[pl.BlockSpec(memory_space=pltpu.HBM)],
        out_specs=pl.BlockSpec(memory_space=pltpu.HBM),
        scratch_shapes=[
            pltpu.SemaphoreType.DMA((n_peers - 1,)),
            pltpu.SemaphoreType.DMA((n_peers - 1,)),
            pltpu.SemaphoreType.DMA,
        ],
    )(src)


class _Hosted:
    def __init__(self, items):
        self.items = items
        self.args = [src for src, _, _ in items]
        self.in_specs = [pl.BlockSpec(memory_space=pltpu.HBM)] * len(items)
        self.out_specs = [pl.BlockSpec(memory_space=pltpu.HBM)] * len(items)
        self.out_shapes = [_exchange_shape(*item) for item in items]
        n_remote = sum(2 ** len(axes) - 1 for _, axes, _ in items)
        self.scratch = [pltpu.SemaphoreType.DMA((n_remote,)), pltpu.SemaphoreType.DMA((n_remote,)),
                        pltpu.SemaphoreType.DMA((len(items),))]

    def _copies(self, src_refs, out_refs, send_sems, recv_sems, own_sems):
        out, sem0 = [], 0
        for n, (_, axes, all_to_all) in enumerate(self.items):
            copies, own = _exchange_copies(src_refs[n], out_refs[n], send_sems, recv_sems, own_sems.at[n], axes,
                                           all_to_all, sem0)
            out += [own] + copies
            sem0 += len(copies)
        return out

    def start(self, *refs):
        for cp in self._copies(*refs):
            cp.start()

    def wait(self, *refs):
        for cp in self._copies(*refs):
            cp.wait()


def _call(body, *, name, grid, in_specs, out_specs, out_shape, scratch_shapes, args, hosted=None):
    params = _params(len(grid))
    if hosted is None:
        outs = pl.pallas_call(body, name=name, grid=grid, in_specs=in_specs, out_specs=tuple(out_specs),
                              out_shape=tuple(out_shape), scratch_shapes=list(scratch_shapes), compiler_params=params)(*args)
        return list(outs), []
    n_in, n_out, n_scr, n_h = len(in_specs), len(out_shape), len(scratch_shapes), len(hosted.items)

    def wrapped(*refs):
        ins, h_in = refs[:n_in], refs[n_in:n_in + n_h]
        outs, h_out = refs[n_in + n_h:n_in + n_h + n_out], refs[n_in + n_h + n_out:n_in + 2 * n_h + n_out]
        scr = refs[n_in + 2 * n_h + n_out:]
        first = functools.reduce(jnp.logical_and, [pl.program_id(k) == 0 for k in range(len(grid))])
        last = functools.reduce(jnp.logical_and, [pl.program_id(k) == grid[k] - 1 for k in range(len(grid))])

        @pl.when(first)
        def _():
            hosted.start(h_in, h_out, *scr[n_scr:])

        body(*ins, *outs, *scr[:n_scr])

        @pl.when(last)
        def _():
            hosted.wait(h_in, h_out, *scr[n_scr:])

    outs = pl.pallas_call(
        wrapped, name=name, grid=grid, in_specs=[*in_specs, *hosted.in_specs],
        out_specs=(*out_specs, *hosted.out_specs), out_shape=(*out_shape, *hosted.out_shapes),
        scratch_shapes=[*scratch_shapes, *hosted.scratch], compiler_params=params)(*args, *hosted.args)
    return list(outs[:n_out]), list(outs[n_out:])


def _sum_parts(parts, name):
    n_parts, rows, cols = parts.shape
    tr = rows
    while n_parts * tr * cols * 4 > 8 * 1024 * 1024 and tr % 16 == 0:
        tr //= 2

    def body(p_ref, o_ref):
        total = p_ref[0]
        for k in range(1, n_parts):
            total = total + p_ref[k]
        o_ref[...] = total

    return pl.pallas_call(
        body,
        name=name,
        grid=(rows // tr,),
        out_shape=jax.ShapeDtypeStruct((rows, cols), F32),
        in_specs=[pl.BlockSpec((n_parts, tr, cols), lambda i: (0, i, 0))],
        out_specs=pl.BlockSpec((tr, cols), lambda i: (i, 0)),
        compiler_params=_params(1),
    )(parts)


def _adamw(w, g, m, v, name):
    rows, cols = w.shape
    tr = rows
    while tr * cols * 4 > 2 * 1024 * 1024 and tr % 16 == 0:
        tr //= 2

    def body(w_ref, g_ref, m_ref, v_ref, d_ref, nm_ref, nv_ref):
        gv = g_ref[...]
        nm = ADAM_B1 * m_ref[...] + (1.0 - ADAM_B1) * gv
        nv = ADAM_B2 * v_ref[...] + (1.0 - ADAM_B2) * (gv * gv)
        m_hat = nm / (1.0 - ADAM_B1 ** ADAM_STEP)
        v_hat = nv / (1.0 - ADAM_B2 ** ADAM_STEP)
        d_ref[...] = -ADAM_LR * (m_hat / (jnp.sqrt(v_hat) + ADAM_EPS) + ADAM_WD * w_ref[...])
        nm_ref[...] = nm
        nv_ref[...] = nv

    spec = pl.BlockSpec((tr, cols), lambda i: (i, 0))
    shape = jax.ShapeDtypeStruct((rows, cols), F32)
    return pl.pallas_call(
        body, name=name, grid=(rows // tr,), out_shape=(shape, shape, shape),
        in_specs=[spec] * 4, out_specs=(spec, spec, spec), compiler_params=_params(1),
    )(w, g, m, v)


def _ada_rows(c_ref, cc_ref):
    rows = jnp.concatenate([c_ref[...], jnp.broadcast_to(cc_ref[...], c_ref.shape)], axis=0)
    return rows


def _ada_fwd(c_all, c_ctx, w_sh, b_sh):
    n_layers, _, ws = w_sh.shape

    def body(c_ref, cc_ref, w_ref, b_ref, o_ref):
        rows = _ada_rows(c_ref, cc_ref)
        s = rows * _sigmoid(rows)
        for i in range(n_layers):
            o_ref[i] = jnp.dot(s, w_ref[i], precision=HIGHEST, preferred_element_type=F32) + b_ref[i]

    return pl.pallas_call(
        body, name="ada_fwd", out_shape=jax.ShapeDtypeStruct((n_layers, 16, ws), F32),
        compiler_params=pltpu.CompilerParams(vmem_limit_bytes=VMEM_LIMIT),
    )(c_all, c_ctx, w_sh, b_sh)


def _ada_bwd(c_all, c_ctx, w_sh, dm_full, dm_sh):
    n_layers, d_model, ws = w_sh.shape
    n_dev = dm_full.shape[0]
    cols = dm_full.shape[-1]

    def body(c_ref, cc_ref, w_ref, dmf_ref, dms_ref, gw_ref, gb_ref, ds_ref):
        rows = _ada_rows(c_ref, cc_ref)
        s = rows * _sigmoid(rows)
        ds = jnp.zeros((8, d_model), F32)
        for i in range(n_layers):
            ctx_s = dms_ref[0, i, 1:2, :]
            ctx_f = dmf_ref[0, i, 1:2, :]
            ex_f = dmf_ref[0, i, 0:1, :]
            for k in range(1, n_dev):
                ctx_s = ctx_s + dms_ref[k, i, 1:2, :]
                ctx_f = ctx_f + dmf_ref[k, i, 1:2, :]
                ex_f = ex_f + dmf_ref[k, i, 0:1, :]
            gb_ref[i] = ex_f + ctx_f
            r = jnp.concatenate([dms_ref[k, i, 0:1, :] for k in range(n_dev)] + [ctx_s, jnp.zeros((7, ws), F32)], axis=0)
            gw_ref[i] = lax.dot_general(s, r, (((0,), (0,)), ((), ())), precision=HIGHEST, preferred_element_type=F32)
            ds = ds + lax.dot_general(jnp.broadcast_to(ctx_s, (8, ws)), w_ref[i], (((1,), (1,)), ((), ())),
                                      precision=HIGHEST, preferred_element_type=F32)
        ds_ref[...] = ds

    return pl.pallas_call(
        body, name="ada_bwd",
        out_shape=(jax.ShapeDtypeStruct((n_layers, d_model, ws), F32), jax.ShapeDtypeStruct((n_layers, 1, cols), F32),
                   jax.ShapeDtypeStruct((8, d_model), F32)),
        compiler_params=pltpu.CompilerParams(vmem_limit_bytes=VMEM_LIMIT),
    )(c_all, c_ctx, w_sh, dm_full, dm_sh)


def _cctx_grad(ds_parts, c_ctx):
    def body(p_ref, c_ref, o_ref):
        tot = p_ref[0, 0:1, :]
        for k in range(1, ds_parts.shape[0]):
            tot = tot + p_ref[k, 0:1, :]
        cv = c_ref[...]
        sg = _sigmoid(cv)
        o_ref[...] = tot * (sg * (1.0 + cv * (1.0 - sg)))

    return pl.pallas_call(body, name="cctx_grad", out_shape=jax.ShapeDtypeStruct(c_ctx.shape, F32))(ds_parts, c_ctx)


def _zoh_math(lam_re, lam_im, log_step, b_re, b_im):
    n_state = lam_re.shape[0] // 2
    dt = jnp.exp(jnp.concatenate([jnp.broadcast_to(log_step[r:r + 1], (n_state, log_step.shape[1])) for r in range(2)],
                                 axis=0))
    mag = jnp.exp(lam_re * dt)
    ar = mag * jnp.cos(lam_im * dt)
    ai = mag * jnp.sin(lam_im * dt)
    qr, qi = ar - 1.0, ai
    den = lam_re * lam_re + lam_im * lam_im
    fr = (qr * lam_re + qi * lam_im) / den
    fi = (qi * lam_re - qr * lam_im) / den
    bbr = fr[None] * b_re - fi[None] * b_im
    bbi = fr[None] * b_im + fi[None] * b_re
    return ar, ai, bbr, bbi


def _zoh_fwd(lam_re, lam_im, log_step, b_re, b_im):
    rg, n = lam_re.shape

    def body(lr_ref, li_ref, ls_ref, br_ref, bi_ref, pr_ref, pi_ref, qr_ref, qi_ref, bbr_ref, bbi_ref):
        ar, ai, bbr, bbi = _zoh_math(lr_ref[...], li_ref[...], ls_ref[...], br_ref[...], bi_ref[...])
        bbr_ref[...] = bbr
        bbi_ref[...] = bbi

        def powers(base_r, base_i, r_ref, i_ref):
            pr, pi_ = base_r, base_i
            for k in range(8):
                r_ref[k] = pr
                i_ref[k] = pi_
                pr, pi_ = pr * base_r - pi_ * base_i, pr * base_i + pi_ * base_r

        powers(ar, ai, pr_ref, pi_ref)
        powers(pr_ref[7], pi_ref[7], qr_ref, qi_ref)

    pw = jax.ShapeDtypeStruct((8, rg, n), F32)
    bb = jax.ShapeDtypeStruct(b_re.shape, F32)
    return pl.pallas_call(body, name="zoh_fwd", out_shape=(pw, pw, pw, pw, bb, bb))(lam_re, lam_im, log_step, b_re, b_im)


def _zoh_bwd(lam_re, lam_im, log_step, b_re, b_im, dar, dai, dbbr, dbbi):
    def body(lr_ref, li_ref, ls_ref, br_ref, bi_ref, dar_ref, dai_ref, dbr_ref, dbi_ref, *outs):
        _, vjp = jax.vjp(_zoh_math, lr_ref[...], li_ref[...], ls_ref[...], br_ref[...], bi_ref[...])
        grads = vjp((dar_ref[...], dai_ref[...], dbr_ref[...], dbi_ref[...]))
        for o_ref, gval in zip(outs, grads):
            o_ref[...] = gval

    shapes = tuple(jax.ShapeDtypeStruct(a.shape, F32) for a in (lam_re, lam_im, log_step, b_re, b_im))
    return pl.pallas_call(body, name="zoh_bwd", out_shape=shapes)(lam_re, lam_im, log_step, b_re, b_im, dar, dai, dbbr, dbbi)


def _inproj(xin, sc, sh, w, *, lnaff=None, tb, gb_rows=None, name, hosted=None):
    n_tok, d_model = xin.shape
    n_chunks, _, cw = w.shape
    tb = min(tb, n_tok)
    nq = cw // LANES
    has_ln = lnaff is not None
    n_out = 1 if gb_rows is None else len(gb_rows)

    def body(*refs):
        if has_ln:
            x_ref, g_ref, b_ref, sc_ref, sh_ref, w_ref = refs[:6]
        else:
            x_ref, sc_ref, sh_ref, w_ref = refs[:4]
        p_refs, h_ref = refs[-1 - n_out:-1], refs[-1]

        @pl.when(pl.program_id(1) == 0)
        def _():
            xv = x_ref[...]
            if has_ln:
                xv = xv * g_ref[...] + b_ref[...]
            h_ref[...] = (xv * (1.0 + sc_ref[...]) + sh_ref[...]).astype(BF16)

        acc = _dot(h_ref[...], w_ref[0]).astype(BF16)
        if gb_rows is None:
            p_refs[0][0] = acc
        else:
            for p_ref in p_refs:
                for q in range(nq):
                    p_ref[q] = acc[:, q * LANES:(q + 1) * LANES]

    vec = pl.BlockSpec((1, d_model), lambda i, j: (0, 0))
    in_specs = [pl.BlockSpec((tb, d_model), lambda i, j: (i, 0))] + ([vec, vec] if has_ln else []) + [
        vec, vec, pl.BlockSpec((1, d_model, cw), lambda i, j: (j, 0, 0))]
    if gb_rows is None:
        p_shapes = [jax.ShapeDtypeStruct((n_chunks, n_tok, cw), BF16)]
        p_specs = [pl.BlockSpec((1, tb, cw), lambda i, j: (j, i, 0))]
    else:
        p_shapes, p_specs = [], []
        for total, off in gb_rows:
            assert off % tb == 0
            p_shapes.append(jax.ShapeDtypeStruct((n_chunks * nq, total, LANES), BF16))
            p_specs.append(pl.BlockSpec((nq, tb, LANES), functools.partial(lambda i, j, ob: (j, i + ob, 0), ob=off // tb)))
    args = (xin,) + (tuple(lnaff) if has_ln else ()) + (sc, sh, w)
    outs, extra = _call(
        body, name=name, grid=(n_tok // tb, n_chunks), in_specs=in_specs,
        out_specs=[*p_specs, pl.BlockSpec((tb, d_model), lambda i, j: (i, 0))],
        out_shape=[*p_shapes, jax.ShapeDtypeStruct((n_tok, d_model), BF16)], scratch_shapes=[], args=args, hosted=hosted)
    return (*outs, extra) if hosted is not None else tuple(outs)


def _inproj_seq(xin, sc, sh, w, lnaff, *, tb, seq_rows, name):
    n_tok, d_model = xin.shape
    n_chunks, _, cw = w.shape
    tb = min(tb, n_tok)
    nq = cw // LANES
    n_out = len(seq_rows)
    steps = (n_tok // tb) * n_chunks

    def body(x_ref, g_ref, b_ref, sc_ref, sh_ref, w_ref, *rest):
        p_refs, h_ref, stage, sems = rest[:n_out], rest[n_out], rest[n_out + 1], rest[n_out + 2]
        i, j = pl.program_id(0), pl.program_id(1)
        step = i * n_chunks + j
        slot = step % 2

        def copies(from_slot):
            return [pltpu.make_async_copy(stage.at[from_slot],
                                          p_ref.at[pl.ds(j * nq, nq), pl.ds(off + i * tb, tb), :], sems.at[from_slot, k])
                    for k, (p_ref, (_, off)) in enumerate(zip(p_refs, seq_rows))]

        @pl.when(step >= 2)
        def _():
            for cp in copies(slot):
                cp.wait()

        @pl.when(j == 0)
        def _():
            xv = x_ref[...] * g_ref[...] + b_ref[...]
            h_ref[...] = (xv * (1.0 + sc_ref[...]) + sh_ref[...]).astype(BF16)

        acc = _dot(h_ref[...], w_ref[0]).astype(BF16)
        for q in range(nq):
            stage[slot, q] = acc[:, q * LANES:(q + 1) * LANES]
        for cp in copies(slot):
            cp.start()

        @pl.when(step == steps - 1)
        def _():
            for cp in copies(slot):
                cp.wait()
            if steps > 1:
                for cp in copies(1 - slot):
                    cp.wait()

    vec = pl.BlockSpec((1, d_model), lambda i, j: (0, 0))
    tok = pl.BlockSpec((tb, d_model), lambda i, j: (i, 0))
    return pl.pallas_call(
        body, name=name, grid=(n_tok // tb, n_chunks),
        in_specs=[tok, vec, vec, vec, vec, pl.BlockSpec((1, d_model, cw), lambda i, j: (j, 0, 0))],
        out_specs=(*[pl.BlockSpec(memory_space=pltpu.HBM)] * n_out, tok),
        out_shape=(*[jax.ShapeDtypeStruct((n_chunks * nq, total, LANES), BF16) for total, _ in seq_rows],
                   jax.ShapeDtypeStruct((n_tok, d_model), BF16)),
        scratch_shapes=[pltpu.VMEM((2, nq, tb, LANES), BF16), pltpu.SemaphoreType.DMA((2, n_out))],
        compiler_params=_params(2),
    )(xin, lnaff[0], lnaff[1], sc, sh, w)


def _shifted(u, rows, width, tb):
    col = rows % width
    um = jnp.where(col == 0, 0.0, pltpu.roll(u, 1, 0))
    up = jnp.where(col == width - 1, 0.0, pltpu.roll(u, tb - 1, 0))
    return um, up


def _slab_width(e):
    return min(512, e // 2)


def _convgate(p, x, gt, conv_w, w_out, ln_g, ln_b, *, grid_mode, tb, name):
    _, n_tok, e = p.shape
    d_model = x.shape[1]
    eh = e // 2
    if not grid_mode:
        tb = n_tok
    tb = min(tb, n_tok)
    nb = n_tok // tb
    hb = tb // GRID_W
    cs = _slab_width(e)

    def body(*refs):
        if grid_mode:
            (bg_ref, cg_ref, v_ref, z_ref, cgp_ref, vp_ref, cgn_ref, vn_ref, x_ref, gt_ref, cw_ref, wo_ref, lg_ref,
             lb_ref, xh_ref, rs_ref, g_ref, yc_ref, f_ref) = refs
        else:
            (bg_ref, cg_ref, v_ref, z_ref, x_ref, gt_ref, cw_ref, wo_ref, lg_ref, lb_ref, xh_ref, rs_ref, g_ref,
             yc_ref, f_ref) = refs
        i = pl.program_id(0)
        rows = lax.broadcasted_iota(jnp.int32, (tb, 1), 0)
        for c0 in range(0, e, cs):
            sl = slice(c0, c0 + cs)
            u = cg_ref[0, :, sl].astype(F32) * v_ref[0, :, sl].astype(F32)
            w = cw_ref[:, sl]
            if grid_mode and c0 >= eh:
                hs = slice(c0 - eh, c0 - eh + cs)
                uprev = cgp_ref[0, :, hs].astype(F32) * vp_ref[0, :, hs].astype(F32)
                unext = cgn_ref[0, :, hs].astype(F32) * vn_ref[0, :, hs].astype(F32)
                uprev = jnp.where(i > 0, uprev, 0.0)
                unext = jnp.where(i < nb - 1, unext, 0.0)
                if tb > GRID_W:
                    um = jnp.concatenate([uprev, u[:tb - GRID_W]], axis=0)
                    up = jnp.concatenate([u[GRID_W:], unext], axis=0)
                else:
                    um, up = uprev, unext
            else:
                um, up = _shifted(u, rows, GRID_W if grid_mode else tb, tb)
            yc = um * w[0:1] + u * w[1:2] + up * w[2:3]
            zf = z_ref[0, :, sl].astype(F32)
            gval = bg_ref[0, :, sl].astype(F32) * yc * (zf * _sigmoid(zf))
            yc_ref[:, sl] = yc.astype(BF16)
            g_ref[:, sl] = gval.astype(BF16)
        f = _dot(g_ref[...], wo_ref[...])
        f_ref[...] = f.astype(BF16)
        r = DN_ALPHA * x_ref[...] + gt_ref[...] * f
        rc = r - _rowmean(r)
        rstd = lax.rsqrt(_rowmean(rc * rc) + LN_EPS)
        xh_ref[...] = rc * rstd
        rs_ref[...] = rstd

    def chunk(k):
        return pl.BlockSpec((1, tb, e), lambda i: (k, i, 0))

    n_hrows = n_tok // GRID_W

    def halo_prev(k):
        return pl.BlockSpec((1, GRID_W, eh), lambda i: (k, jnp.maximum(i * hb - 1, 0), 1))

    def halo_next(k):
        return pl.BlockSpec((1, GRID_W, eh), lambda i: (k, jnp.minimum((i + 1) * hb, n_hrows - 1), 1))

    vec = pl.BlockSpec((1, d_model), lambda i: (0, 0))
    tok = pl.BlockSpec((tb, d_model), lambda i: (i, 0))
    wide = pl.BlockSpec((tb, e), lambda i: (i, 0))
    in_specs = [chunk(0), chunk(1), chunk(2), chunk(3)]
    args = [p, p, p, p]
    if grid_mode:
        in_specs += [halo_prev(1), halo_prev(2), halo_next(1), halo_next(2)]
        args += [p, p, p, p]
    in_specs += [tok, vec, pl.BlockSpec((3, e), lambda i: (0, 0)), pl.BlockSpec((e, d_model), lambda i: (0, 0)), vec, vec]
    args += [x, gt, conv_w, w_out, ln_g, ln_b]
    return pl.pallas_call(
        body, name=name, grid=(nb,),
        out_shape=(jax.ShapeDtypeStruct((n_tok, d_model), F32), jax.ShapeDtypeStruct((n_tok, 1), F32),
                   jax.ShapeDtypeStruct((n_tok, e), BF16), jax.ShapeDtypeStruct((n_tok, e), BF16),
                   jax.ShapeDtypeStruct((n_tok, d_model), BF16)),
        in_specs=in_specs, out_specs=(tok, pl.BlockSpec((tb, 1), lambda i: (i, 0)), wide, wide, tok),
        compiler_params=_params(1),
    )(*args)


def _scan_block(buf_ref, tab_ref, cr, ci, *, reverse, tb, sb):
    n_slabs = tb // SUBLANES
    unrolled = n_slabs <= 64

    def slab(s, carry):
        cr, ci = carry
        idx = (n_slabs - 1 - s) if reverse else s
        r0 = idx * SUBLANES if unrolled else pl.multiple_of(idx * SUBLANES, SUBLANES)
        xr = buf_ref[pl.ds(r0, SUBLANES), 0:sb]
        xi = buf_ref[pl.ds(r0, SUBLANES), sb:2 * sb]
        for k, step in enumerate((1, 2, 4)):
            ar = tab_ref[2 * k]
            ai = tab_ref[2 * k + 1]
            shift = (SUBLANES - step) if reverse else step
            rr = pltpu.roll(xr, shift, 0)
            ri = pltpu.roll(xi, shift, 0)
            xr, xi = xr + ar * rr - ai * ri, xi + ar * ri + ai * rr
        pr = tab_ref[6]
        pi_ = tab_ref[7]
        xr, xi = xr + pr * cr - pi_ * ci, xi + pr * ci + pi_ * cr
        buf_ref[pl.ds(r0, SUBLANES), 0:sb] = xr
        buf_ref[pl.ds(r0, SUBLANES), sb:2 * sb] = xi
        last = 0 if reverse else SUBLANES - 1
        return (jnp.broadcast_to(xr[last:last + 1, :], (SUBLANES, sb)),
                jnp.broadcast_to(xi[last:last + 1, :], (SUBLANES, sb)))

    if unrolled:
        carry = (cr, ci)
        for s in range(n_slabs):
            carry = slab(s, carry)
        return carry
    return lax.fori_loop(0, n_slabs, slab, (cr, ci))


CHUNK = SUBLANES


def _group_mask():
    r = lax.broadcasted_iota(jnp.int32, (LANES, LANES), 0)
    c = lax.broadcasted_iota(jnp.int32, (LANES, LANES), 1)
    return r // 16 == c // 16


def _s5_ops(bre, bim, cre, cim, wr, wi, *, reverse, name, hosted=None):
    ngb, _, sb = bre.shape
    n_rows = CHUNK * LANES

    def body(bre_ref, bim_ref, cre_ref, cim_ref, wr_ref, wi_ref, t_ref, bp_ref, cp_ref):
        b_re, b_im, c_re, c_im = bre_ref[0], bim_ref[0], cre_ref[0], cim_ref[0]
        mask = _group_mask()
        er, ei = [], []
        for tau in range(CHUNK + 1):
            w_r, w_i = wr_ref[0, tau:tau + 1, :], wi_ref[0, tau:tau + 1, :]
            er.append(c_re * w_r - c_im * w_i)
            ei.append(c_re * w_i + c_im * w_r)
        kt = []
        for tau in range(CHUNK):
            k = (lax.dot_general(b_re, er[tau], (((1,), (1,)), ((), ())), precision=HIGHEST, preferred_element_type=F32)
                 - lax.dot_general(b_im, ei[tau], (((1,), (1,)), ((), ())), precision=HIGHEST, preferred_element_type=F32))
            kt.append(jnp.where(mask, k, 0.0).astype(BF16))
        zero = jnp.zeros((LANES, LANES), BF16)
        for i in range(CHUNK):
            rows = slice(i * LANES, (i + 1) * LANES)
            for j in range(CHUNK):
                lag = (i - j) if reverse else (j - i)
                t_ref[0, rows, j * LANES:(j + 1) * LANES] = kt[lag] if lag >= 0 else zero
            tau = i if reverse else CHUNK - 1 - i
            w_r, w_i = wr_ref[0, tau:tau + 1, :], wi_ref[0, tau:tau + 1, :]
            bp_ref[0, rows, 0:sb] = (b_re * w_r - b_im * w_i).astype(BF16)
            bp_ref[0, rows, sb:2 * sb] = (b_re * w_i + b_im * w_r).astype(BF16)
            tau = CHUNK - i if reverse else i + 1
            cp_ref[0, rows, 0:sb] = er[tau].astype(BF16)
            cp_ref[0, rows, sb:2 * sb] = (-ei[tau]).astype(BF16)

    mat = pl.BlockSpec((1, LANES, sb), lambda g: (g, 0, 0))
    pw = pl.BlockSpec((1, CHUNK + 1, sb), lambda g: (g, 0, 0))
    outs, extra = _call(
        body, name=name, grid=(ngb,),
        out_shape=[jax.ShapeDtypeStruct((ngb, n_rows, n_rows), BF16), jax.ShapeDtypeStruct((ngb, n_rows, 2 * sb), BF16),
                   jax.ShapeDtypeStruct((ngb, n_rows, 2 * sb), BF16)],
        in_specs=[mat, mat, mat, mat, pw, pw],
        out_specs=[pl.BlockSpec((1, n_rows, n_rows), lambda g: (g, 0, 0)),
                   pl.BlockSpec((1, n_rows, 2 * sb), lambda g: (g, 0, 0)),
                   pl.BlockSpec((1, n_rows, 2 * sb), lambda g: (g, 0, 0))],
        scratch_shapes=[], args=(bre, bim, cre, cim, wr, wi), hosted=hosted)
    return (*outs, extra)


def _s5_ops_bwd(bre, bim, cre, cim, wr, wi, ar, ai, dt, dbp, dcp, da8, *, reverse, name, hosted=None):
    ngb, _, sb = bre.shape
    n_rows = CHUNK * LANES

    def dot_hi(a, b, dims):
        return lax.dot_general(a.astype(BF16), b.astype(BF16), (dims, ((), ())), preferred_element_type=F32)

    def body(bre_ref, bim_ref, cre_ref, cim_ref, wr_ref, wi_ref, ar_ref, ai_ref, dt_ref, dbp_ref, dcp_ref, da8_ref,
             dbre_ref, dbim_ref, dcre_ref, dcim_ref, da_ref):
        b_re, b_im, c_re, c_im = bre_ref[0], bim_ref[0], cre_ref[0], cim_ref[0]
        mask = _group_mask()
        w_r = [wr_ref[0, tau:tau + 1, :] for tau in range(CHUNK + 1)]
        w_i = [wi_ref[0, tau:tau + 1, :] for tau in range(CHUNK + 1)]
        der = [jnp.zeros((LANES, sb), F32) for _ in range(CHUNK + 1)]
        dei = [jnp.zeros((LANES, sb), F32) for _ in range(CHUNK + 1)]
        dwr = [jnp.zeros((1, sb), F32) for _ in range(CHUNK + 1)]
        dwi = [jnp.zeros((1, sb), F32) for _ in range(CHUNK + 1)]
        dwr[CHUNK] = da8_ref[0, :, 0:sb]
        dwi[CHUNK] = da8_ref[0, :, sb:2 * sb]
        d_bre = jnp.zeros((LANES, sb), F32)
        d_bim = jnp.zeros((LANES, sb), F32)
        dkt = [jnp.zeros((LANES, LANES), F32) for _ in range(CHUNK)]
        for i in range(CHUNK):
            rows = slice(i * LANES, (i + 1) * LANES)
            for j in range(CHUNK):
                lag = (i - j) if reverse else (j - i)
                if lag >= 0:
                    dkt[lag] = dkt[lag] + dt_ref[0, rows, j * LANES:(j + 1) * LANES]
            tau = i if reverse else CHUNK - 1 - i
            g_r, g_i = dbp_ref[0, rows, 0:sb], dbp_ref[0, rows, sb:2 * sb]
            d_bre = d_bre + g_r * w_r[tau] + g_i * w_i[tau]
            d_bim = d_bim - g_r * w_i[tau] + g_i * w_r[tau]
            dwr[tau] = dwr[tau] + _colsum(g_r * b_re + g_i * b_im)
            dwi[tau] = dwi[tau] + _colsum(g_i * b_re - g_r * b_im)
            tau = CHUNK - i if reverse else i + 1
            der[tau] = der[tau] + dcp_ref[0, rows, 0:sb]
            dei[tau] = dei[tau] - dcp_ref[0, rows, sb:2 * sb]
        d_cre = jnp.zeros((LANES, sb), F32)
        d_cim = jnp.zeros((LANES, sb), F32)
        for tau in range(CHUNK + 1):
            if tau < CHUNK:
                e_r = c_re * w_r[tau] - c_im * w_i[tau]
                e_i = c_re * w_i[tau] + c_im * w_r[tau]
                dk = jnp.where(mask, dkt[tau], 0.0)
                d_bre = d_bre + dot_hi(dk, e_r, ((1,), (0,)))
                d_bim = d_bim - dot_hi(dk, e_i, ((1,), (0,)))
                der[tau] = der[tau] + dot_hi(dk, b_re, ((0,), (0,)))
                dei[tau] = dei[tau] - dot_hi(dk, b_im, ((0,), (0,)))
            d_cre = d_cre + der[tau] * w_r[tau] + dei[tau] * w_i[tau]
            d_cim = d_cim - der[tau] * w_i[tau] + dei[tau] * w_r[tau]
            dwr[tau] = dwr[tau] + _colsum(der[tau] * c_re + dei[tau] * c_im)
            dwi[tau] = dwi[tau] + _colsum(dei[tau] * c_re - der[tau] * c_im)
        a_r, a_i = ar_ref[0], ai_ref[0]
        d_ar = jnp.zeros((1, sb), F32)
        d_ai = jnp.zeros((1, sb), F32)
        for tau in range(CHUNK, 0, -1):
            d_ar = d_ar + dwr[tau] * w_r[tau - 1] + dwi[tau] * w_i[tau - 1]
            d_ai = d_ai - dwr[tau] * w_i[tau - 1] + dwi[tau] * w_r[tau - 1]
            dwr[tau - 1], dwi[tau - 1] = (dwr[tau - 1] + dwr[tau] * a_r + dwi[tau] * a_i,
                                          dwi[tau - 1] - dwr[tau] * a_i + dwi[tau] * a_r)
        dbre_ref[0] = d_bre
        dbim_ref[0] = d_bim
        dcre_ref[0] = d_cre
        dcim_ref[0] = d_cim
        da_ref[0, :, 0:sb] = d_ar
        da_ref[0, :, sb:2 * sb] = d_ai

    mat = pl.BlockSpec((1, LANES, sb), lambda g: (g, 0, 0))
    pw = pl.BlockSpec((1, CHUNK + 1, sb), lambda g: (g, 0, 0))
    one = pl.BlockSpec((1, 1, sb), lambda g: (g, 0, 0))
    two = pl.BlockSpec((1, 1, 2 * sb), lambda g: (g, 0, 0))
    big = pl.BlockSpec((1, n_rows, n_rows), lambda g: (g, 0, 0))
    big2 = pl.BlockSpec((1, n_rows, 2 * sb), lambda g: (g, 0, 0))
    mshape = jax.ShapeDtypeStruct((ngb, LANES, sb), F32)
    outs, extra = _call(
        body, name=name, grid=(ngb,),
        out_shape=[mshape, mshape, mshape, mshape, jax.ShapeDtypeStruct((ngb, 1, 2 * sb), F32)],
        in_specs=[mat, mat, mat, mat, pw, pw, one, one, big, big2, big2, two],
        out_specs=[mat, mat, mat, mat, two],
        scratch_shapes=[], args=(bre, bim, cre, cim, wr, wi, ar, ai, dt, dbp, dcp, da8), hosted=hosted)
    return (*outs, extra)


MXU_TILE = 256


def _causal_span(tile, n_tiles, reverse, of_output):
    upto, onward = slice(0, (tile + 1) * MXU_TILE), slice(tile * MXU_TILE, n_tiles * MXU_TILE)
    return (onward if reverse else upto) if of_output else (upto if reverse else onward)


def _apply_causal(uv, t_ref, reverse):
    n_tiles = uv.shape[1] // MXU_TILE
    cols = []
    for tj in range(n_tiles):
        span = _causal_span(tj, n_tiles, reverse, True)
        cols.append(_dot(uv[:, span], t_ref[0, span, tj * MXU_TILE:(tj + 1) * MXU_TILE]))
    return jnp.concatenate(cols, axis=1)


def _apply_causal_t(dyv, t_ref, reverse):
    n_tiles = dyv.shape[1] // MXU_TILE
    cols = []
    for ti in range(n_tiles):
        span = _causal_span(ti, n_tiles, reverse, False)
        cols.append(_dot_nt(dyv[:, span], t_ref[0, ti * MXU_TILE:(ti + 1) * MXU_TILE, span]))
    return jnp.concatenate(cols, axis=1)


def _shift_rows(xv, edge, rows, n_rows, down):
    if down:
        return jnp.where(rows == 0, edge, pltpu.roll(xv, 1, 0))
    return jnp.where(rows == n_rows - 1, edge, pltpu.roll(xv, n_rows - 1, 0))


def _rows_of_tokens(tok_ref, conv_scr, rb):
    conv_scr[...] = tok_ref[0].astype(F32)
    return jnp.concatenate([conv_scr[pl.ds(j, rb, stride=CHUNK), :] for j in range(CHUNK)], axis=1).astype(BF16)


def _tokens_of_rows(val, tok_ref, conv_scr, rb):
    for j in range(CHUNK):
        conv_scr[pl.ds(j, rb, stride=CHUNK), :] = val[:, j * LANES:(j + 1) * LANES]
    tok_ref[0] = conv_scr[...].astype(BF16)


def _s5_row_block(n_seq, target=416):
    n_rows = n_seq // CHUNK
    best = 16
    for rb in range(16, min(target, n_rows) + 1, 16):
        if n_rows % rb == 0:
            best = rb
    assert n_rows % best == 0
    return best


def _s5_fwd(useq, t_op, bp, cp, tab, *, reverse, name):
    ngb, n_seq, _ = useq.shape
    sb = bp.shape[2] // 2
    width = CHUNK * LANES
    rb = _s5_row_block(n_seq)
    tbk = rb * CHUNK
    steps = n_seq // tbk

    def blk(i):
        return (steps - 1 - i) if reverse else i

    def body(u_ref, t_ref, b_ref, c_ref, tab_ref, y_ref, hp_ref, h_scr, conv_scr, carry_scr):
        i = pl.program_id(1)

        @pl.when(i == 0)
        def _():
            carry_scr[...] = jnp.zeros_like(carry_scr)

        enter = carry_scr[0:1, :]
        uv = _rows_of_tokens(u_ref, conv_scr, rb)
        h_scr[...] = _dot(uv, b_ref[0])
        cr, ci = _scan_block(h_scr, tab_ref.at[0], carry_scr[:, 0:sb], carry_scr[:, sb:2 * sb],
                             reverse=reverse, tb=rb, sb=sb)
        carry_scr[:, 0:sb] = cr
        carry_scr[:, sb:2 * sb] = ci
        rows = lax.broadcasted_iota(jnp.int32, (rb, 1), 0)
        hprev = _shift_rows(h_scr[...], enter, rows, rb, down=not reverse)
        hp_ref[0] = hprev
        _tokens_of_rows(_apply_causal(uv, t_ref, reverse) + _dot_nt(hprev.astype(BF16), c_ref[0]), y_ref, conv_scr, rb)

    op = pl.BlockSpec((1, width, width), lambda g, i: (g, 0, 0))
    op2 = pl.BlockSpec((1, width, 2 * sb), lambda g, i: (g, 0, 0))
    tok = pl.BlockSpec((1, tbk, LANES), lambda g, i: (g, blk(i), 0))
    return pl.pallas_call(
        body, name=name, grid=(ngb, steps),
        out_shape=(jax.ShapeDtypeStruct((ngb, n_seq, LANES), BF16),
                   jax.ShapeDtypeStruct((ngb, n_seq // CHUNK, 2 * sb), F32)),
        in_specs=[tok, op, op2, op2, pl.BlockSpec((1, 8, SUBLANES, sb), lambda g, i: (g, 0, 0, 0))],
        out_specs=(tok, pl.BlockSpec((1, rb, 2 * sb), lambda g, i: (g, blk(i), 0))),
        scratch_shapes=[pltpu.VMEM((rb, 2 * sb), F32), pltpu.VMEM((tbk, LANES), F32),
                        pltpu.VMEM((SUBLANES, 2 * sb), F32)],
        compiler_params=_params(2),
    )(useq, t_op, bp, cp, tab)


def _s5_bwd(useq, dy, hprev, t_op, bp, cp, tab_adj, *, reverse, name, hosted=None):
    ngb, n_seq, _ = useq.shape
    sb = bp.shape[2] // 2
    width = CHUNK * LANES
    rb = _s5_row_block(n_seq)
    tbk = rb * CHUNK
    steps = n_seq // tbk

    def blk(i):
        return i if reverse else steps - 1 - i

    def body(u_ref, dy_ref, hp_ref, t_ref, b_ref, c_ref, taba_ref, du_ref, dt_ref, db_ref, dc_ref, da_ref,
             lam_scr, conv_scr, lcarry_scr, gedge_scr, da_scr):
        i = pl.program_id(1)
        first = i == 0

        _zero_first(first, lcarry_scr, gedge_scr, da_scr, dt_ref, db_ref, dc_ref)
        rows = lax.broadcasted_iota(jnp.int32, (rb, 1), 0)
        uv = _rows_of_tokens(u_ref, conv_scr, rb)
        dyv = _rows_of_tokens(dy_ref, conv_scr, rb)
        gy = _dot(dyv, c_ref[0])
        edge = gy[rb - 1:rb, :] if reverse else gy[0:1, :]
        lam_scr[...] = _shift_rows(gy, gedge_scr[...], rows, rb, down=reverse)
        gedge_scr[...] = edge
        lr, li = _scan_block(lam_scr, taba_ref.at[0], lcarry_scr[:, 0:sb], lcarry_scr[:, sb:2 * sb],
                             reverse=not reverse, tb=rb, sb=sb)
        lcarry_scr[:, 0:sb] = lr
        lcarry_scr[:, sb:2 * sb] = li

        lam = lam_scr[...]
        lam_bf = lam.astype(BF16)
        _tokens_of_rows(_apply_causal_t(dyv, t_ref, reverse) + _dot_nt(lam_bf, b_ref[0]), du_ref, conv_scr, rb)
        for tj in range(width // MXU_TILE):
            span, cols = _causal_span(tj, width // MXU_TILE, reverse, True), slice(tj * MXU_TILE, (tj + 1) * MXU_TILE)
            _acc(dt_ref.at[0, span, cols], _dot_tn(uv[:, span], dyv[:, cols]))
        _acc(db_ref.at[0], _dot_tn(uv, lam_bf))
        _acc(dc_ref.at[0], _dot_tn(dyv, hp_ref[0].astype(BF16)))
        lam_r, lam_i = lam[:, 0:sb], lam[:, sb:2 * sb]
        hp_r, hp_i = hp_ref[0, :, 0:sb], hp_ref[0, :, sb:2 * sb]
        da_scr[:, 0:sb] += _colsum(lam_r * hp_r + lam_i * hp_i)
        da_scr[:, sb:2 * sb] += _colsum(lam_i * hp_r - lam_r * hp_i)

        @pl.when(i == steps - 1)
        def _():
            da_ref[0] = da_scr[...]

    op = pl.BlockSpec((1, width, width), lambda g, i: (g, 0, 0))
    op2 = pl.BlockSpec((1, width, 2 * sb), lambda g, i: (g, 0, 0))
    tabs = pl.BlockSpec((1, 8, SUBLANES, sb), lambda g, i: (g, 0, 0, 0))
    tok = pl.BlockSpec((1, tbk, LANES), lambda g, i: (g, blk(i), 0))
    outs, extra = _call(
        body, name=name, grid=(ngb, steps),
        out_shape=[jax.ShapeDtypeStruct((ngb, n_seq, LANES), BF16),
                   jax.ShapeDtypeStruct((ngb, width, width), F32),
                   jax.ShapeDtypeStruct((ngb, width, 2 * sb), F32),
                   jax.ShapeDtypeStruct((ngb, width, 2 * sb), F32),
                   jax.ShapeDtypeStruct((ngb, 1, 2 * sb), F32)],
        in_specs=[tok, tok, pl.BlockSpec((1, rb, 2 * sb), lambda g, i: (g, blk(i), 0)), op, op2, op2, tabs],
        out_specs=[tok, op, op2, op2, pl.BlockSpec((1, 1, 2 * sb), lambda g, i: (g, 0, 0))],
        scratch_shapes=[pltpu.VMEM((rb, 2 * sb), F32), pltpu.VMEM((tbk, LANES), F32),
                        pltpu.VMEM((SUBLANES, 2 * sb), F32), pltpu.VMEM((1, 2 * sb), F32), pltpu.VMEM((1, 2 * sb), F32)],
        args=(useq, dy, hprev, t_op, bp, cp, tab_adj), hosted=hosted)
    return (*outs, extra)


def _glu_loss(useq, yf, yb, z, xhat0, ln0, gt, d_vec, w_glu, b_glu, w_out, ln1, target, *, offs, dy_rows, tb, name):
    ngb = useq.shape[0]
    n_tok, d_model = xhat0.shape
    e = ngb * LANES
    tb = min(tb, n_tok)
    assert all(off % tb == 0 for off in offs) and all(off % tb == 0 for _, off in dy_rows)
    nz = z.shape[0]

    def body(u_ref, yf_ref, yb_ref, z_ref, xh0_ref, g0_ref, b0_ref, gt_ref, d_ref, wg_ref, bg_ref, wo_ref, g1_ref,
             b1_ref, t_ref, loss_ref, dxr_ref, do_ref, gz_ref, gg_ref, dq_ref, dz_ref, dyf_ref, dyb_ref, dg1_ref, db1_ref,
             dgt_ref, dbg_ref, dd_ref, loss_scr, yl_scr, th_scr, s_scr, dg_scr):
        i = pl.program_id(0)
        _zero_first(i == 0, loss_scr, dg1_ref, db1_ref, dgt_ref, dbg_ref, dd_ref)
        zw = e // nz
        cs = min(512, zw)

        def z_slab(c0):
            return z_ref[c0 // zw, :, c0 % zw:c0 % zw + cs].astype(F32)

        for q in range(ngb):
            sl = slice(q * LANES, (q + 1) * LANES)
            yl = d_ref[:, sl] * u_ref[q].astype(F32) + yf_ref[q].astype(F32) + yb_ref[q].astype(F32)
            th = jnp.tanh(GELU_K * (yl + GELU_C * yl * yl * yl))
            yl_scr[:, sl] = yl
            th_scr[:, sl] = th
            gg_ref[:, sl] = (0.5 * yl * (1.0 + th)).astype(BF16)
        g_all = gg_ref[...]
        for c0 in range(0, e, cs):
            sl = slice(c0, c0 + cs)
            s = _sigmoid(_dot(g_all, wg_ref[:, sl]) + bg_ref[:, sl])
            s_scr[:, sl] = s
            zf = z_slab(c0)
            g2 = 0.5 * yl_scr[:, sl] * (1.0 + th_scr[:, sl]) * s
            gz_ref[:, sl] = (g2 * (zf * _sigmoid(zf))).astype(BF16)
        o = _dot(gz_ref[...], wo_ref[...])
        x1 = xh0_ref[...] * g0_ref[...] + b0_ref[...]
        r = DN_ALPHA * x1 + gt_ref[...] * o
        rc = r - _rowmean(r)
        rstd = lax.rsqrt(_rowmean(rc * rc) + LN_EPS)
        xh = rc * rstd
        err = xh * g1_ref[...] + b1_ref[...] - t_ref[...]
        _acc(loss_scr, _colsum(err * err))
        dy = err * (1.0 / d_model)
        _acc(dg1_ref, _colsum(dy * xh))
        _acc(db1_ref, _colsum(dy))
        dxh = dy * g1_ref[...]
        dr = rstd * (dxh - _rowmean(dxh) - xh * _rowmean(dxh * xh))
        dxr_ref[...] = DN_ALPHA * dr
        _acc(dgt_ref, _colsum(dr * o))
        do_bf = (dr * gt_ref[...]).astype(BF16)
        do_ref[...] = do_bf
        for c0 in range(0, e, cs):
            sl = slice(c0, c0 + cs)
            dgz = _dot_nt(do_bf, wo_ref[sl, :])
            zf = z_slab(c0)
            sz = _sigmoid(zf)
            g = 0.5 * yl_scr[:, sl] * (1.0 + th_scr[:, sl])
            s = s_scr[:, sl]
            dg2 = dgz * (zf * sz)
            dz_ref[:, sl] = (dgz * (g * s) * (sz * (1.0 + zf * (1.0 - sz)))).astype(BF16)
            dq = dg2 * g * s * (1.0 - s)
            _acc(dbg_ref.at[:, sl], _colsum(dq))
            dq_ref[:, sl] = dq.astype(BF16)
            dg_scr[:, sl] = dg2 * s
        dq_all = dq_ref[...]
        for c0 in range(0, e, cs):
            dg_blk = dg_scr[:, c0:c0 + cs] + _dot_nt(dq_all, wg_ref[c0:c0 + cs, :])
            for q in range(c0 // LANES, (c0 + cs) // LANES):
                sl = slice(q * LANES, (q + 1) * LANES)
                yl = yl_scr[:, sl]
                th = th_scr[:, sl]
                dgelu = 0.5 * (1.0 + th) + 0.5 * yl * (1.0 - th * th) * (GELU_K * (1.0 + 3.0 * GELU_C * yl * yl))
                dyl = dg_blk[:, q * LANES - c0:(q + 1) * LANES - c0] * dgelu
                _acc(dd_ref.at[:, sl], _colsum(dyl * u_ref[q].astype(F32)))
                dyf_ref[q] = dyl.astype(BF16)
                dyb_ref[q] = dyl.astype(BF16)

        @pl.when(i == pl.num_programs(0) - 1)
        def _():
            loss_ref[...] = (0.5 / d_model) * jnp.sum(loss_scr[...], axis=1, keepdims=True)

    vec = pl.BlockSpec((1, d_model), lambda i: (0, 0))
    evec = pl.BlockSpec((1, e), lambda i: (0, 0))
    tok = pl.BlockSpec((tb, d_model), lambda i: (i, 0))
    wide = pl.BlockSpec((tb, e), lambda i: (i, 0))
    def gblk(off):
        return pl.BlockSpec((ngb, tb, LANES), functools.partial(lambda i, ob: (0, i + ob, 0), ob=off // tb))

    once = dict(pipeline_mode=pl.Buffered(1))
    tok_f = jax.ShapeDtypeStruct((n_tok, d_model), F32)
    tok_b = jax.ShapeDtypeStruct((n_tok, d_model), BF16)
    wide_b = jax.ShapeDtypeStruct((n_tok, e), BF16)
    vec_f = jax.ShapeDtypeStruct((1, d_model), F32)
    evec_f = jax.ShapeDtypeStruct((1, e), F32)
    return pl.pallas_call(
        body, name=name, grid=(n_tok // tb,),
        out_shape=(jax.ShapeDtypeStruct((1, 1), F32), tok_f, tok_b, wide_b, wide_b, wide_b, wide_b,
                   *[jax.ShapeDtypeStruct((ngb, total, LANES), BF16) for total, _ in dy_rows],
                   vec_f, vec_f, vec_f, evec_f, evec_f),
        in_specs=[gblk(offs[0]), gblk(offs[1]), gblk(offs[2]),
                  pl.BlockSpec((nz, tb, e // nz), lambda i: (0, i, 0)), tok, vec, vec, vec, evec,
                  pl.BlockSpec((e, e), lambda i: (0, 0), **once), evec,
                  pl.BlockSpec((e, d_model), lambda i: (0, 0), **once), vec, vec, tok],
        out_specs=(pl.BlockSpec((1, 1), lambda i: (0, 0)), tok, tok, wide, wide, wide, wide,
                   *[gblk(off) for _, off in dy_rows], vec, vec, vec, evec, evec),
        scratch_shapes=[pltpu.VMEM((1, d_model), F32)] + [pltpu.VMEM((tb, e), F32)] * 4,
        compiler_params=_params(1),
    )(useq, yf, yb, z, xhat0, ln0[0], ln0[1], gt, d_vec, w_glu, b_glu, w_out, ln1[0], ln1[1], target)


def _ssm_inbwd(duf, dub, w, xhat, rstd, ln, sc, gt_prev, f_prev, *, lat, row_f, row_b, tb, name):
    ngb = duf.shape[0]
    e = ngb * LANES
    n_tok, d_model = xhat.shape
    tb = min(tb, n_tok)
    obf, obb = row_f // tb, row_b // tb
    has_lat = lat is not None
    n_w = w.shape[0] if has_lat else w.shape[0] // 2

    def body(*refs):
        if has_lat:
            (duf_ref, dub_ref, dyl_ref, dz_ref, d_ref, dxr_ref, w_ref, xh_ref, rs_ref, g_ref, b_ref, sc_ref, gt_ref,
             f_ref, dp_ref, dr_ref, df_ref, dsc_ref, dsh_ref, dg_ref, db_ref, dgt_ref) = refs
        else:
            (duf_ref, dub_ref, w_ref, xh_ref, rs_ref, g_ref, b_ref, sc_ref, gt_ref, f_ref, dp_ref, dr_ref, df_ref,
             dsc_ref, dsh_ref, dg_ref, db_ref, dgt_ref) = refs
        _zero_first(pl.program_id(0) == 0, dsc_ref, dsh_ref, dg_ref, db_ref, dgt_ref)
        du = (jnp.concatenate([duf_ref[q] for q in range(ngb)], axis=1).astype(F32)
              + jnp.concatenate([dub_ref[q] for q in range(ngb)], axis=1).astype(F32))
        if has_lat:
            du = du + d_ref[...] * jnp.concatenate([dyl_ref[q] for q in range(ngb)], axis=1).astype(F32)
            dp_ref[:, e:2 * e] = dz_ref[...]
        else:
            dp_ref[:, e:2 * e] = jnp.zeros((tb, e), BF16)
        dp_ref[:, 0:e] = du.astype(BF16)
        dh = jnp.zeros((tb, d_model), F32)
        for j in range(n_w):
            dh = dh + _dot(dp_ref[:, j * d_model:(j + 1) * d_model], w_ref[j])
        xh = xh_ref[...]
        x1 = xh * g_ref[...] + b_ref[...]
        dx1 = dh * (1.0 + sc_ref[...])
        if has_lat:
            dx1 = dx1 + dxr_ref[...]
        _acc(dsc_ref, _colsum(dh * x1))
        _acc(dsh_ref, _colsum(dh))
        _acc(dg_ref, _colsum(dx1 * xh))
        _acc(db_ref, _colsum(dx1))
        dxh = dx1 * g_ref[...]
        dr = rs_ref[...] * (dxh - _rowmean(dxh) - xh * _rowmean(dxh * xh))
        dr_ref[...] = dr
        df_ref[...] = (dr * gt_ref[...]).astype(BF16)
        _acc(dgt_ref, _colsum(dr * f_ref[...].astype(F32)))

    vec = pl.BlockSpec((1, d_model), lambda i: (0, 0))
    tok = pl.BlockSpec((tb, d_model), lambda i: (i, 0))
    gblk = pl.BlockSpec((ngb, tb, LANES), lambda i: (0, i, 0))
    in_specs = [pl.BlockSpec((ngb, tb, LANES), lambda i: (0, i + obf, 0)),
                pl.BlockSpec((ngb, tb, LANES), lambda i: (0, i + obb, 0))]
    args = [duf, dub]
    if has_lat:
        in_specs += [gblk, pl.BlockSpec((tb, e), lambda i: (i, 0)), pl.BlockSpec((1, e), lambda i: (0, 0)), tok]
        args += list(lat)
    in_specs += [pl.BlockSpec(w.shape, lambda i: (0, 0, 0)), tok, pl.BlockSpec((tb, 1), lambda i: (i, 0)), vec, vec, vec,
                 vec, tok]
    args += [w, xhat, rstd, ln[0], ln[1], sc, gt_prev, f_prev]
    vec_f = jax.ShapeDtypeStruct((1, d_model), F32)
    return pl.pallas_call(
        body, name=name, grid=(n_tok // tb,),
        out_shape=(jax.ShapeDtypeStruct((n_tok, 2 * e), BF16), jax.ShapeDtypeStruct((n_tok, d_model), F32),
                   jax.ShapeDtypeStruct((n_tok, d_model), BF16), vec_f, vec_f, vec_f, vec_f, vec_f),
        in_specs=in_specs,
        out_specs=(pl.BlockSpec((tb, 2 * e), lambda i: (i, 0)), tok, tok, vec, vec, vec, vec, vec),
        compiler_params=_params(1),
    )(*args)


def _conv_bwd_a(df, w_out_t, p, yc, *, tb, name, hosted=None):
    _, n_tok, e = p.shape
    d_model = df.shape[1]
    tb = min(tb, n_tok)
    cs = _slab_width(e)

    def body(df_ref, wo_ref, bg_ref, z_ref, yc_ref, dbg_ref, dz_ref, dyc_ref):
        dfv = df_ref[...]
        for c0 in range(0, e, cs):
            sl = slice(c0, c0 + cs)
            dgv = _dot(dfv, wo_ref[:, sl])
            zf = z_ref[0, :, sl].astype(F32)
            sz = _sigmoid(zf)
            silu_z = zf * sz
            bg = bg_ref[0, :, sl].astype(F32)
            yc = yc_ref[:, sl].astype(F32)
            dbg_ref[:, sl] = (dgv * yc * silu_z).astype(BF16)
            dyc_ref[:, sl] = (dgv * bg * silu_z).astype(BF16)
            dz_ref[:, sl] = (dgv * bg * yc * (sz * (1.0 + zf * (1.0 - sz)))).astype(BF16)

    wide = pl.BlockSpec((tb, e), lambda i: (i, 0))
    shape = jax.ShapeDtypeStruct((n_tok, e), BF16)
    outs, extra = _call(
        body, name=name, grid=(n_tok // tb,), out_shape=[shape, shape, shape],
        in_specs=[pl.BlockSpec((tb, d_model), lambda i: (i, 0)), pl.BlockSpec((d_model, e), lambda i: (0, 0)),
                  pl.BlockSpec((1, tb, e), lambda i: (0, i, 0)), pl.BlockSpec((1, tb, e), lambda i: (3, i, 0)), wide],
        out_specs=[wide, wide, wide], scratch_shapes=[], args=(df, w_out_t, p, p, yc), hosted=hosted)
    return (*outs, extra)


def _conv_bwd_b(dyc, p, dbg, dz, conv_w, *, grid_mode, tb, name, hosted=None):
    _, n_tok, e = p.shape
    eh = e // 2
    if not grid_mode:
        tb = n_tok
    tb = min(tb, n_tok)
    nb = n_tok // tb
    hb = tb // GRID_W
    cs = _slab_width(e)

    def body(*refs):
        if grid_mode:
            dyc_ref, dycp_ref, dycn_ref, cg_ref, v_ref, dbg_ref, dz_ref, cw_ref, dp_ref, dcw_ref = refs
        else:
            dyc_ref, cg_ref, v_ref, dbg_ref, dz_ref, cw_ref, dp_ref, dcw_ref = refs
        i = pl.program_id(0)
        _zero_first(i == 0, dcw_ref)
        rows = lax.broadcasted_iota(jnp.int32, (tb, 1), 0)
        dp_ref[0] = dbg_ref[...]
        dp_ref[3] = dz_ref[...]
        for c0 in range(0, e, cs):
            sl = slice(c0, c0 + cs)
            dyc = dyc_ref[:, sl].astype(F32)
            w = cw_ref[:, sl]
            if grid_mode and c0 >= eh:
                hs = slice(c0 - eh, c0 - eh + cs)
                dprev = jnp.where(i > 0, dycp_ref[:, hs].astype(F32), 0.0)
                dnext = jnp.where(i < nb - 1, dycn_ref[:, hs].astype(F32), 0.0)
                if tb > GRID_W:
                    dm = jnp.concatenate([dprev, dyc[:tb - GRID_W]], axis=0)
                    dpl = jnp.concatenate([dyc[GRID_W:], dnext], axis=0)
                else:
                    dm, dpl = dprev, dnext
            else:
                dm, dpl = _shifted(dyc, rows, GRID_W if grid_mode else tb, tb)
            cg = cg_ref[0, :, sl].astype(F32)
            v = v_ref[0, :, sl].astype(F32)
            u = cg * v
            du = w[0:1] * dpl + w[1:2] * dyc + w[2:3] * dm
            dp_ref[1, :, sl] = (du * v).astype(BF16)
            dp_ref[2, :, sl] = (du * cg).astype(BF16)
            _acc(dcw_ref.at[:, sl], jnp.concatenate([_colsum(u * dpl), _colsum(u * dyc), _colsum(u * dm)], axis=0))

    n_hrows = n_tok // GRID_W
    wide = pl.BlockSpec((tb, e), lambda i: (i, 0))
    in_specs = [wide]
    args = [dyc]
    if grid_mode:
        in_specs += [pl.BlockSpec((GRID_W, eh), lambda i: (jnp.maximum(i * hb - 1, 0), 1)),
                     pl.BlockSpec((GRID_W, eh), lambda i: (jnp.minimum((i + 1) * hb, n_hrows - 1), 1))]
        args += [dyc, dyc]
    in_specs += [pl.BlockSpec((1, tb, e), lambda i: (1, i, 0)), pl.BlockSpec((1, tb, e), lambda i: (2, i, 0)), wide, wide,
                 pl.BlockSpec((3, e), lambda i: (0, 0))]
    args += [p, p, dbg, dz, conv_w]
    outs, extra = _call(
        body, name=name, grid=(nb,),
        out_shape=[jax.ShapeDtypeStruct((4, n_tok, e), BF16), jax.ShapeDtypeStruct((3, e), F32)],
        in_specs=in_specs,
        out_specs=[pl.BlockSpec((4, tb, e), lambda i: (0, i, 0)), pl.BlockSpec((3, e), lambda i: (0, 0))],
        scratch_shapes=[], args=args, hosted=hosted)
    return (*outs, extra)


def _conv_inbwd(dp, w, dr, x, sc, *, tb, name, hosted=None):
    n_chunks, n_tok, e = dp.shape
    d_model = x.shape[1]
    tb = min(tb, n_tok)

    def body(dp_ref, w_ref, dr_ref, x_ref, sc_ref, gx_ref, dsc_ref, dsh_ref):
        _zero_first(pl.program_id(0) == 0, dsc_ref, dsh_ref)
        dh = _dot(dp_ref[0], w_ref[0])
        for k in range(1, n_chunks):
            dh = dh + _dot(dp_ref[k], w_ref[k])
        gx_ref[...] = DN_ALPHA * dr_ref[...] + dh * (1.0 + sc_ref[...])
        _acc(dsc_ref, _colsum(dh * x_ref[...]))
        _acc(dsh_ref, _colsum(dh))

    vec = pl.BlockSpec((1, d_model), lambda i: (0, 0))
    tok = pl.BlockSpec((tb, d_model), lambda i: (i, 0))
    vec_f = jax.ShapeDtypeStruct((1, d_model), F32)
    outs, extra = _call(
        body, name=name, grid=(n_tok // tb,),
        out_shape=[jax.ShapeDtypeStruct((n_tok, d_model), F32), vec_f, vec_f],
        in_specs=[pl.BlockSpec((n_chunks, tb, e), lambda i: (0, i, 0)),
                  pl.BlockSpec((n_chunks, e, d_model), lambda i: (0, 0, 0), pipeline_mode=pl.Buffered(1)),
                  tok, tok, vec],
        out_specs=[tok, vec, vec],
        scratch_shapes=[], args=(dp, w, dr, x, sc), hosted=hosted)
    return (*outs, extra)


def _wgrad(a, b, *, n_chunks, tm, tl, init=None, name):
    n_tok, m = a.shape
    tl = min(tl, n_tok)
    chunked = b.ndim == 3
    cw = b.shape[2] if chunked else b.shape[1] // n_chunks
    has_init = init is not None

    def body(*refs):
        if has_init:
            a_ref, b_ref, init_ref, o_ref = refs
        else:
            a_ref, b_ref, o_ref = refs
        @pl.when(pl.program_id(2) == 0)
        def _():
            o_ref[0] = init_ref[0] if has_init else jnp.zeros_like(o_ref[0])

        o_ref[0] += _dot_tn(a_ref[...], b_ref[0] if chunked else b_ref[...])

    o_spec = pl.BlockSpec((1, tm, cw), lambda jm, jc, l: (jc, jm, 0))
    b_spec = (pl.BlockSpec((1, tl, cw), lambda jm, jc, l: (jc, l, 0)) if chunked
              else pl.BlockSpec((tl, cw), lambda jm, jc, l: (l, jc)))
    init_spec = pl.BlockSpec((1, tm, cw), lambda jm, jc, l: (jc, jm, 0), pipeline_mode=pl.Buffered(1))
    in_specs = [pl.BlockSpec((tl, tm), lambda jm, jc, l: (l, jm)), b_spec] + ([init_spec] if has_init else [])
    args = (a, b) + ((init,) if has_init else ())
    return pl.pallas_call(
        body, name=name, grid=(m // tm, n_chunks, n_tok // tl),
        out_shape=jax.ShapeDtypeStruct((n_chunks, m, cw), F32),
        in_specs=in_specs, out_specs=o_spec, compiler_params=_params(3),
    )(*args)


def _block_diag(t, ngb):
    g, p, n = t.shape
    gpb = g // ngb
    eye = jnp.eye(gpb, dtype=t.dtype)
    return jnp.einsum("bgpn,gh->bgphn", t.reshape(ngb, gpb, p, n), eye).reshape(ngb, gpb * p, gpb * n)


def _block_diag_t(mat, g, p, n):
    ngb = mat.shape[0]
    gpb = g // ngb
    eye = jnp.eye(gpb, dtype=mat.dtype)
    return jnp.einsum("bgphn,gh->bgpn", mat.reshape(ngb, gpb, p, gpb, n), eye).reshape(g, p, n)


def _scan_tables(pw_r, pw_i, ngb, reverse):
    _, g, n = pw_r.shape
    sb = g * n // ngb
    rows = jnp.arange(SUBLANES)
    kinds = []
    for step in (1, 2, 4):
        mask = ((rows < SUBLANES - step) if reverse else (rows >= step)).astype(F32)
        for part in (pw_r[step - 1], pw_i[step - 1]):
            kinds.append(part.reshape(ngb, 1, sb) * mask[None, :, None])
    for part in (pw_r, pw_i):
        pw = part[::-1] if reverse else part
        kinds.append(jnp.transpose(pw.reshape(SUBLANES, ngb, sb), (1, 0, 2)))
    return jnp.stack(kinds, axis=1)


def _flat(parts):
    return jnp.concatenate([p.reshape(-1) for p in parts])


def _unflat(vec, shapes):
    out, off = [], 0
    for s in shapes:
        size = math.prod(s)
        out.append(vec[off:off + size].reshape(s))
        off += size
    return out


def kernel(x, c, ctx, c_ctx, ada_w, ada_b, ln_g, ln_b, conv_w_in, conv_w, conv_w_out, ssm_w_in, ssm_lam_re, ssm_lam_im, ssm_log_step, ssm_b_re, ssm_b_im, ssm_c_re, ssm_c_im, ssm_d, ssm_w_glu, ssm_b_glu, ssm_w_out, loss_target, m_c_ctx, m_ada_w, m_ada_b, m_ln_g, m_ln_b, m_conv_w_in, m_conv_w, m_conv_w_out, m_ssm_w_in, m_ssm_lam_re, m_ssm_lam_im, m_ssm_log_step, m_ssm_b_re, m_ssm_b_im, m_ssm_c_re, m_ssm_c_im, m_ssm_d, m_ssm_w_glu, m_ssm_b_glu, m_ssm_w_out, v_c_ctx, v_ada_w, v_ada_b, v_ln_g, v_ln_b, v_conv_w_in, v_conv_w, v_conv_w_out, v_ssm_w_in, v_ssm_lam_re, v_ssm_lam_im, v_ssm_log_step, v_ssm_b_re, v_ssm_b_im, v_ssm_c_re, v_ssm_c_im, v_ssm_d, v_ssm_w_glu, v_ssm_b_glu, v_ssm_w_out):
    weights = dict(c_ctx=c_ctx, ada_w=ada_w, ada_b=ada_b, ln_g=ln_g, ln_b=ln_b, conv_w_in=conv_w_in, conv_w=conv_w,
                   conv_w_out=conv_w_out, ssm_w_in=ssm_w_in, ssm_lam_re=ssm_lam_re, ssm_lam_im=ssm_lam_im,
                   ssm_log_step=ssm_log_step, ssm_b_re=ssm_b_re, ssm_b_im=ssm_b_im, ssm_c_re=ssm_c_re,
                   ssm_c_im=ssm_c_im, ssm_d=ssm_d, ssm_w_glu=ssm_w_glu, ssm_b_glu=ssm_b_glu, ssm_w_out=ssm_w_out)
    mom_m = dict(c_ctx=m_c_ctx, ada_w=m_ada_w, ada_b=m_ada_b, ln_g=m_ln_g, ln_b=m_ln_b, conv_w_in=m_conv_w_in,
                 conv_w=m_conv_w, conv_w_out=m_conv_w_out, ssm_w_in=m_ssm_w_in, ssm_lam_re=m_ssm_lam_re,
                 ssm_lam_im=m_ssm_lam_im, ssm_log_step=m_ssm_log_step, ssm_b_re=m_ssm_b_re, ssm_b_im=m_ssm_b_im,
                 ssm_c_re=m_ssm_c_re, ssm_c_im=m_ssm_c_im, ssm_d=m_ssm_d, ssm_w_glu=m_ssm_w_glu,
                 ssm_b_glu=m_ssm_b_glu, ssm_w_out=m_ssm_w_out)
    mom_v = dict(c_ctx=v_c_ctx, ada_w=v_ada_w, ada_b=v_ada_b, ln_g=v_ln_g, ln_b=v_ln_b, conv_w_in=v_conv_w_in,
                 conv_w=v_conv_w, conv_w_out=v_conv_w_out, ssm_w_in=v_ssm_w_in, ssm_lam_re=v_ssm_lam_re,
                 ssm_lam_im=v_ssm_lam_im, ssm_log_step=v_ssm_log_step, ssm_b_re=v_ssm_b_re, ssm_b_im=v_ssm_b_im,
                 ssm_c_re=v_ssm_c_re, ssm_c_im=v_ssm_c_im, ssm_d=v_ssm_d, ssm_w_glu=v_ssm_w_glu,
                 ssm_b_glu=v_ssm_b_glu, ssm_w_out=v_ssm_w_out)
    names = list(weights)

    n_lat, d_model = x.shape[1], x.shape[2]
    n_ctx = ctx.shape[1]
    e = 2 * d_model
    n_grp, n_state, grp = ssm_lam_re.shape[2], ssm_lam_re.shape[3], ssm_b_re.shape[4]
    ngb = e // LANES
    ws = ada_w.shape[2]
    tb_tok = min(512, n_lat)
    n_seq = n_ctx + n_lat
    tb_glu = math.gcd(256, n_ctx)
    chip = 2 * lax.axis_index("x") + lax.axis_index("y")
    me = 2 * chip + lax.axis_index("c")
    chips, everyone, pair = ("x", "y"), MESH_AXES, ("c",)

    x2, ctx2, tgt2 = x[0], ctx[0], loss_target[0]

    wc_in_own = conv_w_in[0].astype(BF16)
    later_weights = _Hosted([(w[0].astype(BF16), chips, False) for w in (conv_w_out, ssm_w_in, ssm_w_glu, ssm_w_out)])
    small_full = _exchange(_flat([conv_w[0], ssm_d[0], ssm_b_glu[0]]).reshape(1, -1), chips, False, "ag_small")
    es = conv_w.shape[2]
    conv_w_full = jnp.transpose(small_full[:, 0, :3 * es].reshape(4, 3, es), (1, 0, 2)).reshape(3, e)
    d_full = small_full[:, 0, 3 * es:4 * es].reshape(1, e)
    b_glu_full = small_full[:, 0, 4 * es:5 * es].reshape(1, e)

    c_all = _exchange(c, everyone, False, "ag_c").reshape(8, d_model)
    cc2 = c_ctx.reshape(1, d_model)
    b_sh = lax.dynamic_slice_in_dim(ada_b, chip * ws, ws, axis=1).reshape(DEPTH, 1, ws)
    m_sh = _ada_fwd(c_all, cc2, ada_w, b_sh)
    m_all = _exchange(m_sh, chips, False, "ag_mod")
    m_full = jnp.transpose(m_all, (1, 2, 0, 3)).reshape(DEPTH, 16, 3 * d_model)
    m_lat = lax.dynamic_slice_in_dim(m_full, me, 1, axis=1)
    m_ctx = m_full[:, 8:9]

    def mods(m, i):
        return m[i, :, 0:d_model], m[i, :, d_model:2 * d_model], m[i, :, 2 * d_model:3 * d_model]

    sh0, sc0, gt0 = mods(m_lat, 0)
    sh1, sc1, gt1 = mods(m_lat, 1)
    shc0, scc0, gtc0 = mods(m_ctx, 0)
    shc1, scc1, _ = mods(m_ctx, 1)
    ln0 = (ln_g[0:1], ln_b[0:1])
    ln1 = (ln_g[1:2], ln_b[1:2])

    def lam_view(t):
        return jnp.transpose(t[0], (0, 2, 1)).reshape(2 * n_state, n_grp)

    def lam_back(t):
        return jnp.transpose(t.reshape(2, n_state, n_grp), (0, 2, 1)).reshape(ssm_lam_re.shape)

    def b_view(t):
        return jnp.transpose(t[0], (0, 2, 3, 1)).reshape(2 * n_state * grp, n_grp)

    def b_back(t):
        return jnp.transpose(t.reshape(2, n_state, grp, n_grp), (0, 3, 1, 2)).reshape(ssm_b_re.shape)

    def c_view(t):
        return jnp.transpose(t[0], (0, 2, 3, 1)).reshape(2 * grp * n_state, n_grp)

    def c_back(t):
        return jnp.transpose(t.reshape(2, grp, n_state, n_grp), (0, 3, 1, 2)).reshape(ssm_c_re.shape)

    def channel_major(t):
        return jnp.transpose(t.reshape(2 * n_state, grp, n_grp), (1, 0, 2))

    def by_group(t):
        return jnp.transpose(t.reshape(t.shape[0], 2, n_state, n_grp), (0, 1, 3, 2))

    lam_re2, lam_im2, log_step2 = lam_view(ssm_lam_re), lam_view(ssm_lam_im), ssm_log_step[0]
    b_re_t, b_im_t = channel_major(b_view(ssm_b_re)), channel_major(b_view(ssm_b_im))
    pw_r, pw_i, pq_r, pq_i, bbr, bbi = _zoh_fwd(lam_re2, lam_im2, log_step2, b_re_t, b_im_t)
    sbk = n_grp * n_state // ngb
    pw_r, pw_i, pq_r, pq_i = (by_group(t) for t in (pw_r, pw_i, pq_r, pq_i))
    bbr_g = jnp.transpose(by_group(bbr), (1, 2, 0, 3))
    bbi_g = jnp.transpose(by_group(bbi), (1, 2, 0, 3))

    def power_rows(pw, r, first):
        full = jnp.concatenate([jnp.full((1, n_grp, n_state), first, F32), pw[:, r]], axis=0)
        return jnp.transpose(full.reshape(CHUNK + 1, ngb, sbk), (1, 0, 2))

    s5 = []
    for r in range(2):
        prm = dict(bre=_block_diag(bbr_g[r], ngb), bim=_block_diag(bbi_g[r], ngb),
                   cre=_block_diag(ssm_c_re[0, r], ngb), cim=_block_diag(ssm_c_im[0, r], ngb),
                   wr=power_rows(pw_r, r, 1.0), wi=power_rows(pw_i, r, 0.0))
        half_rows = wc_in_own[r * (d_model // 2):(r + 1) * (d_model // 2)]
        t_op, bp_op, cp_op, (wc_in_half,) = _s5_ops(
            prm["bre"], prm["bim"], prm["cre"], prm["cim"], prm["wr"], prm["wi"], reverse=(r == 1),
            name=f"l1_s5_ops{r}", hosted=_Hosted([(half_rows, chips, False)]))
        s5.append(dict(
            prm, t=t_op, bp=bp_op, cp=cp_op, wc_in_half=wc_in_half,
            tab=_scan_tables(pq_r[:, r], pq_i[:, r], ngb, reverse=(r == 1)),
            tab_adj=_scan_tables(pq_r[:, r], -pq_i[:, r], ngb, reverse=(r == 0))))
    wc_in = jnp.concatenate([s5[0]["wc_in_half"], s5[1]["wc_in_half"]], axis=1)

    p0, h0, gathered = _inproj(x2, sc0, sh0, wc_in, tb=min(1024, n_lat), name="l0_inproj", hosted=later_weights)
    wc_out, ws_in, w_glu, ws_out = gathered
    wc_out, w_glu, ws_out = wc_out.reshape(e, d_model), w_glu.reshape(e, e), ws_out.reshape(e, d_model)
    wc_in_t, ws_in_t, wc_out_t = jnp.transpose(wc_in, (0, 2, 1)), jnp.transpose(ws_in, (0, 2, 1)), wc_out.T
    pc0, hc0 = _inproj(ctx2, scc0, shc0, wc_in, tb=tb_tok, name="l0_inproj_ctx")
    xhat0, rstd0, g0, yc0, f0 = _convgate(p0, x2, gt0, conv_w_full, wc_out, *ln0, grid_mode=True, tb=tb_tok, name="l0_conv")
    chat0, crstd0, gc0, ycc0, fc0 = _convgate(pc0, ctx2, gtc0, conv_w_full, wc_out, *ln0, grid_mode=False, tb=tb_tok,
                                              name="l0_conv_ctx")

    seq_rows = [(n_seq, n_ctx), (n_seq, 0)]

    def wide_chunk(w2):
        return jnp.transpose(w2, (1, 0, 2)).reshape(1, d_model, e)

    ws_u, ws_z = wide_chunk(ws_in[0:2]), wide_chunk(ws_in[2:4])
    useq_f, useq_b, h1 = _inproj_seq(xhat0, sc1, sh1, ws_u, ln0, tb=min(1024, n_lat), seq_rows=seq_rows,
                                     name="l1_inproj_u")
    z1, _ = _inproj(xhat0, sc1, sh1, ws_z, lnaff=ln0, tb=min(1024, n_lat), name="l1_inproj_z")
    uc, hc1 = _inproj(chat0, scc1, shc1, ws_u, lnaff=ln0, tb=tb_tok, gb_rows=[(n_ctx, 0)], name="l1_inproj_ctx")
    useq = [useq_f.at[:, 0:n_ctx].set(uc), useq_b.at[:, n_lat:].set(uc)]
    y_dir, hp_dir = [], []
    for r in range(2):
        yr, hcr = _s5_fwd(useq[r], s5[r]["t"], s5[r]["bp"], s5[r]["cp"], s5[r]["tab"], reverse=(r == 1),
                          name=f"l1_s5_fwd{r}")
        y_dir.append(yr)
        hp_dir.append(hcr)

    (loss, dxres, do1, gz1, gg1, dq1, dz1, dy_f, dy_b, dg1, db1, dgt1, dbglu, dd) = _glu_loss(
        useq[0], y_dir[0], y_dir[1], z1, xhat0, ln0, gt1, d_full, w_glu, b_glu_full, ws_out, ln1, tgt2,
        offs=(n_ctx, n_ctx, 0), dy_rows=seq_rows, tb=tb_glu, name="l1_glu_loss")
    no_dy = jnp.zeros((ngb, n_ctx, LANES), BF16)
    dy_dir = [dy_f.at[:, 0:n_ctx].set(no_dy), dy_b.at[:, n_lat:].set(no_dy)]

    tl = min(1024, n_lat)

    def owner_slices(name, full):
        w = weights[name]
        return full.reshape(8, math.prod(w.shape[:-1]) // 2, w.shape[-1])

    def scatter(named):
        return _Hosted([(owner_slices(name, full), everyone, True) for name, full in named])

    def siblings(names):
        return _Hosted([(_sum_parts(rs_parts[name], "sum_" + name), pair, False) for name in names])

    rs_parts, both_halves = {}, {}

    gw_glu = _wgrad(gg1, dq1, n_chunks=1, tm=e // 2, tl=min(2 * tl, n_lat), name="wg_glu")
    gw_ssm_out = _wgrad(gz1, do1, n_chunks=1, tm=e, tl=min(2 * tl, n_lat), name="wg_ssm_out")
    du_dir, s5_grads = [], []
    for r in range(2):
        if r == 0:
            hosted = scatter([("ssm_w_glu", gw_glu), ("ssm_w_out", gw_ssm_out)])
        else:
            hosted = siblings(["ssm_w_glu", "ssm_w_out"])
        dur, dt_op, dbp_op, dcp_op, da8, extra = _s5_bwd(useq[r], dy_dir[r], hp_dir[r], s5[r]["t"], s5[r]["bp"],
                                                         s5[r]["cp"], s5[r]["tab_adj"], reverse=(r == 1),
                                                         name=f"l1_s5_bwd{r}", hosted=hosted)
        if r == 0:
            rs_parts["ssm_w_glu"], rs_parts["ssm_w_out"] = extra
        else:
            both_halves["ssm_w_glu"], both_halves["ssm_w_out"] = extra
        du_dir.append(dur)
        prm = s5[r]
        s5_grads.append(functools.partial(
            _s5_ops_bwd, prm["bre"], prm["bim"], prm["cre"], prm["cim"], prm["wr"], prm["wi"], prm["wr"][:, 1:2],
            prm["wi"][:, 1:2], dt_op, dbp_op, dcp_op, da8, reverse=(r == 1), name=f"l1_s5_ops_bwd{r}"))
    dp1, dr0, df0, dsc1, dsh1, dg0, db0, dgt0 = _ssm_inbwd(
        du_dir[0], du_dir[1], ws_in_t, xhat0, rstd0, ln0, sc1, gt0, f0, lat=(dy_dir[1], dz1, d_full, dxres),
        row_f=n_ctx, row_b=0, tb=tb_glu, name="l1_inbwd")
    dpc1, drc0, dfc0, dscc1, dshc1, dgc0, dbc0, dgtc0 = _ssm_inbwd(
        du_dir[0], du_dir[1], ws_in_t, chat0, crstd0, ln0, scc1, gtc0, fc0, lat=None,
        row_f=0, row_b=n_lat, tb=n_ctx, name="l1_inbwd_ctx")

    def conv_backward(df, p, yc, dr, xin, sc, grid_mode, tag, hosted_a=None, hosted_b=None, hosted_in=None):
        dbg, dz, dyc, extra_a = _conv_bwd_a(df, wc_out_t, p, yc, tb=tb_tok, name="l0_bwd_a" + tag, hosted=hosted_a)
        dp, dcw, extra_b = _conv_bwd_b(dyc, p, dbg, dz, conv_w_full, grid_mode=grid_mode, tb=tb_glu,
                                       name="l0_bwd_b" + tag, hosted=hosted_b)
        gx, dsc, dsh, extra_in = _conv_inbwd(dp, wc_in_t, dr, xin, sc, tb=tb_tok, name="l0_inbwd" + tag,
                                             hosted=None if hosted_in is None else hosted_in(dp, extra_a + extra_b))
        return dp, dcw, gx, dsc, dsh, extra_in

    dpc0, dcwc0, _, dscc0, dshc0, _ = conv_backward(dfc0, pc0, ycc0, drc0, ctx2, scc0, False, "_ctx")
    gw_conv_out = _wgrad(g0, df0, n_chunks=1, tm=e, tl=tl, name="wg_conv_out",
                         init=_wgrad(gc0, dfc0, n_chunks=1, tm=e, tl=tl, name="wg_conv_out_ctx"))
    gw_ssm_in = _wgrad(h1, dp1, n_chunks=4, tm=d_model, tl=tl, name="wg_ssm_in",
                       init=_wgrad(hc1, dpc1, n_chunks=4, tm=d_model, tl=tl, name="wg_ssm_in_ctx"))
    gw_conv_in_ctx = _wgrad(hc0, dpc0, n_chunks=4, tm=d_model, tl=tl, name="wg_conv_in_ctx")

    def behind_inbwd(dp, arrived):
        rs_parts["conv_w_out"], rs_parts["ssm_w_in"] = arrived
        gw_conv_in = _wgrad(h0, dp, n_chunks=4, tm=d_model, tl=tl, name="wg_conv_in", init=gw_conv_in_ctx)
        both = siblings(["ssm_w_in", "conv_w_out"])
        return _Hosted(scatter([("conv_w_in", gw_conv_in)]).items + both.items)

    dp0, dcw0, grad_x, dsc0, dsh0, extra_in = conv_backward(
        df0, p0, yc0, dr0, x2, sc0, True, "", hosted_a=scatter([("conv_w_out", gw_conv_out)]),
        hosted_b=scatter([("ssm_w_in", gw_ssm_in)]), hosted_in=behind_inbwd)
    rs_parts["conv_w_in"], both_halves["ssm_w_in"], both_halves["conv_w_out"] = extra_in
    half_in = _sum_parts(rs_parts["conv_w_in"], "sum_conv_w_in")
    cut = half_in.shape[0] // 2
    *grads_r0, (top,) = s5_grads[0](hosted=_Hosted([(half_in[:cut], pair, False)]))
    *grads_r1, (bottom,) = s5_grads[1](hosted=_Hosted([(half_in[cut:], pair, False)]))
    both_halves["conv_w_in"] = jnp.concatenate([top, bottom], axis=1)
    s5_grads = [grads_r0, grads_r1]

    grads, deltas, new_m, new_v = {}, {}, {}, {}
    for name in ("ssm_w_glu", "ssm_w_out", "ssm_w_in", "conv_w_out", "conv_w_in"):
        w = weights[name]
        rows, cols = math.prod(w.shape[:-1]), w.shape[-1]
        both = both_halves[name].reshape(rows, cols)
        dlt, nm, nv = _adamw(w.reshape(rows, cols), both, mom_m[name].reshape(rows, cols),
                             mom_v[name].reshape(rows, cols), "adamw_" + name)
        grads[name], deltas[name] = both.reshape(w.shape), dlt.reshape(w.shape)
        new_m[name], new_v[name] = nm.reshape(w.shape), nv.reshape(w.shape)

    gpn = (n_grp, grp, n_state)
    small_parts = [
        jnp.concatenate([dg0 + dgc0, dg1], axis=0), jnp.concatenate([db0 + dbc0, db1], axis=0),
        dcw0 + dcwc0, dd, dbglu,
        jnp.stack([s5_grads[r][4] for r in range(2)]),
    ] + [jnp.stack([_block_diag_t(s5_grads[r][k], *gpn) for r in range(2)]) for k in range(4)] + [loss]
    small_shapes = [p.shape for p in small_parts]
    flat = _flat(small_parts)
    quantum = 8 * SUBLANES * LANES
    n_flat = -(-flat.shape[0] // quantum) * quantum
    flat = jnp.pad(flat, (0, n_flat - flat.shape[0])).reshape(8, n_flat // (8 * LANES), LANES)
    red = _sum_parts(_exchange(flat, everyone, True, "rs_small"), "sum_small")
    red = _exchange(red, everyone, False, "ag_small_grads").reshape(-1)
    g_ln_g, g_ln_b, g_conv_w, g_d, g_bglu, g_a, g_bbr, g_bbi, g_cre, g_cim, loss_sum = _unflat(red, small_shapes)

    def groups_minor(t, lead):
        return jnp.moveaxis(t, 1, -1).reshape(lead, n_grp)

    g_a = g_a.reshape(2, ngb, 2, sbk)
    dar = groups_minor(g_a[:, :, 0].reshape(2, n_grp, n_state), 2 * n_state)
    dai = groups_minor(g_a[:, :, 1].reshape(2, n_grp, n_state), 2 * n_state)
    dbbr_t = jnp.transpose(g_bbr, (2, 0, 3, 1)).reshape(grp, 2 * n_state, n_grp)
    dbbi_t = jnp.transpose(g_bbi, (2, 0, 3, 1)).reshape(grp, 2 * n_state, n_grp)
    z_lre, z_lim, z_ls, z_bre, z_bim = _zoh_bwd(lam_re2, lam_im2, log_step2, b_re_t, b_im_t, dar, dai, dbbr_t, dbbi_t)

    zero = jnp.zeros((1, d_model), F32)
    dm_rows = jnp.stack([
        jnp.stack([jnp.concatenate([dsh0, dsc0, dgt0], axis=1), jnp.concatenate([dshc0, dscc0, dgtc0], axis=1)]),
        jnp.stack([jnp.concatenate([dsh1, dsc1, dgt1], axis=1), jnp.concatenate([dshc1, dscc1, zero], axis=1)]),
    ]).reshape(DEPTH, 2, 3 * d_model)
    dm_all = _exchange(dm_rows, everyone, False, "ag_dmod")
    dm_sh = lax.dynamic_slice_in_dim(dm_all, chip * ws, ws, axis=3)
    g_ada_w, g_ada_b, ds_part = _ada_bwd(c_all, cc2, ada_w, dm_all, dm_sh)
    g_cctx = _cctx_grad(_exchange(ds_part, chips, False, "ag_dsctx"), cc2)

    grads["ada_w"] = g_ada_w
    dlt, nm, nv = _adamw(ada_w.reshape(-1, ws), g_ada_w.reshape(-1, ws), m_ada_w.reshape(-1, ws),
                         v_ada_w.reshape(-1, ws), "adamw_ada_w")
    deltas["ada_w"], new_m["ada_w"], new_v["ada_w"] = dlt.reshape(ada_w.shape), nm.reshape(ada_w.shape), nv.reshape(ada_w.shape)

    def chip_cols(full, rows):
        return lax.dynamic_slice_in_dim(full.reshape(rows, e), chip * es, es, axis=1)

    def same(t):
        return t

    def channel_minor_back(t):
        return jnp.transpose(t, (1, 0, 2)).reshape(2 * n_state * grp, n_grp)

    small = dict(
        c_ctx=(g_cctx, lambda t: t.reshape(1, d_model), lambda t: t.reshape(c_ctx.shape)),
        ada_b=(g_ada_b.reshape(ada_b.shape), same, same),
        ln_g=(g_ln_g, same, same), ln_b=(g_ln_b, same, same),
        conv_w=(chip_cols(g_conv_w, 3), lambda t: t[0], lambda t: t.reshape(conv_w.shape)),
        ssm_lam_re=(z_lre, lam_view, lam_back), ssm_lam_im=(z_lim, lam_view, lam_back),
        ssm_log_step=(z_ls, lambda t: t[0], lambda t: t.reshape(ssm_log_step.shape)),
        ssm_b_re=(channel_minor_back(z_bre), b_view, b_back), ssm_b_im=(channel_minor_back(z_bim), b_view, b_back),
        ssm_c_re=(groups_minor(g_cre, 2 * grp * n_state), c_view, c_back),
        ssm_c_im=(groups_minor(g_cim, 2 * grp * n_state), c_view, c_back),
        ssm_d=(chip_cols(g_d, 1), same, same), ssm_b_glu=(chip_cols(g_bglu, 1), same, same))
    for n, (g_view, view, back) in small.items():
        dlt, nm, nv = _adamw(view(weights[n]), g_view, view(mom_m[n]), view(mom_v[n]), "adamw_" + n)
        grads[n], deltas[n], new_m[n], new_v[n] = back(g_view), back(dlt), back(nm), back(nv)

    return (loss_sum.reshape(()), grad_x.reshape(x.shape), *[grads[n] for n in names], *[deltas[n] for n in names],
            *[new_m[n] for n in names], *[new_v[n] for n in names])
```

```python
import functools
import math

import jax
import jax.numpy as jnp
from jax import lax
from jax.experimental import pallas as pl
from jax.experimental.pallas import tpu as pltpu

F32 = jnp.float32
BF16 = jnp.bfloat16
LANES = 128
SUBLANES = 8
VMEM_LIMIT = 56 * 1024 * 1024
MESH_AXES = ("x", "y", "c")
HIGHEST = lax.Precision.HIGHEST

GRID_W = 64
LN_EPS = 1e-5
DEPTH = 2
DN_ALPHA = (2 * DEPTH) ** 0.25
ADAM_LR, ADAM_B1, ADAM_B2, ADAM_EPS, ADAM_WD, ADAM_STEP = 0.001, 0.9, 0.999, 1e-08, 0.01, 10
GELU_K = math.sqrt(2.0 / math.pi)
GELU_C = 0.044715


def _params(n_grid_axes):
    return pltpu.CompilerParams(dimension_semantics=("arbitrary",) * n_grid_axes, vmem_limit_bytes=VMEM_LIMIT)


def _dot(a, b):
    return jnp.dot(a, b, preferred_element_type=F32)


def _dot_nt(a, b):
    return lax.dot_general(a, b, (((1,), (1,)), ((), ())), preferred_element_type=F32)


def _dot_tn(a, b):
    return lax.dot_general(a, b, (((0,), (0,)), ((), ())), preferred_element_type=F32)


def _sigmoid(x):
    return 0.5 * jnp.tanh(0.5 * x) + 0.5


def _colsum(x):
    return jnp.sum(x, axis=0, keepdims=True)


def _rowmean(x):
    return jnp.mean(x, axis=-1, keepdims=True)


def _zero_first(first, *refs):
    @pl.when(first)
    def _():
        for ref in refs:
            ref[...] = jnp.zeros_like(ref)


def _acc(ref, value):
    ref[...] += value


def _exchange_copies(src_ref, out_ref, send_sems, recv_sems, own_sem, axes, all_to_all, sem0=0):
    n_peers = 2 ** len(axes)
    pos = {a: lax.axis_index(a) for a in MESH_AXES}

    def index(p):
        return sum(p[a] * (2 ** (len(axes) - 1 - i)) for i, a in enumerate(axes))

    me = index(pos)
    own = pltpu.make_async_copy(src_ref.at[me] if all_to_all else src_ref, out_ref.at[me], own_sem)
    copies = []
    for k in range(1, n_peers):
        peer = dict(pos)
        for i, a in enumerate(axes):
            if (k >> (len(axes) - 1 - i)) & 1:
                peer[a] = 1 - pos[a]
        copies.append(pltpu.make_async_remote_copy(
            src_ref=src_ref.at[index(peer)] if all_to_all else src_ref,
            dst_ref=out_ref.at[me],
            send_sem=send_sems.at[sem0 + k - 1],
            recv_sem=recv_sems.at[sem0 + k - 1],
            device_id=tuple(peer[a] for a in MESH_AXES),
            device_id_type=pl.DeviceIdType.MESH,
        ))
    return copies, own


def _exchange_shape(src, axes, all_to_all):
    block = tuple(src.shape[1:] if all_to_all else src.shape)
    return jax.ShapeDtypeStruct((2 ** len(axes),) + block, src.dtype)


def _exchange(src, axes, all_to_all, name):
    n_peers = 2 ** len(axes)

    def body(src_ref, out_ref, send_sems, recv_sems, own_sem):
        copies, own = _exchange_copies(src_ref, out_ref, send_sems, recv_sems, own_sem, axes, all_to_all)
        own.start()
        for cp in copies:
            cp.start()
        for cp in copies:
            cp.wait()
        own.wait()

    return pl.pallas_call(
        body,
        name=name,
        out_shape=_exchange_shape(src, axes, all_to_all),
        in_specs=[pl.BlockSpec(memory_space=pltpu.HBM)],
        out_specs=pl.BlockSpec(memory_space=pltpu.HBM),
        scratch_shapes=[
            pltpu.SemaphoreType.DMA((n_peers - 1,)),
            pltpu.SemaphoreType.DMA((n_peers - 1,)),
            pltpu.SemaphoreType.DMA,
        ],
    )(src)


class _Hosted:
    def __init__(self, items):
        self.items = items
        self.args = [src for src, _, _ in items]
        self.in_specs = [pl.BlockSpec(memory_space=pltpu.HBM)] * len(items)
        self.out_specs = [pl.BlockSpec(memory_space=pltpu.HBM)] * len(items)
        self.out_shapes = [_exchange_shape(*item) for item in items]
        n_remote = sum(2 ** len(axes) - 1 for _, axes, _ in items)
        self.scratch = [pltpu.SemaphoreType.DMA((n_remote,)), pltpu.SemaphoreType.DMA((n_remote,)),
                        pltpu.SemaphoreType.DMA((len(items),))]

    def _copies(self, src_refs, out_refs, send_sems, recv_sems, own_sems):
        out, sem0 = [], 0
        for n, (_, axes, all_to_all) in enumerate(self.items):
            copies, own = _exchange_copies(src_refs[n], out_refs[n], send_sems, recv_sems, own_sems.at[n], axes,
                                           all_to_all, sem0)
            out += [own] + copies
            sem0 += len(copies)
        return out

    def start(self, *refs):
        for cp in self._copies(*refs):
            cp.start()

    def wait(self, *refs):
        for cp in self._copies(*refs):
            cp.wait()


def _call(body, *, name, grid, in_specs, out_specs, out_shape, scratch_shapes, args, hosted=None):
    params = _params(len(grid))
    if hosted is None:
        outs = pl.pallas_call(body, name=name, grid=grid, in_specs=in_specs, out_specs=tuple(out_specs),
                              out_shape=tuple(out_shape), scratch_shapes=list(scratch_shapes), compiler_params=params)(*args)
        return list(outs), []
    n_in, n_out, n_scr, n_h = len(in_specs), len(out_shape), len(scratch_shapes), len(hosted.items)

    def wrapped(*refs):
        ins, h_in = refs[:n_in], refs[n_in:n_in + n_h]
        outs, h_out = refs[n_in + n_h:n_in + n_h + n_out], refs[n_in + n_h + n_out:n_in + 2 * n_h + n_out]
        scr = refs[n_in + 2 * n_h + n_out:]
        first = functools.reduce(jnp.logical_and, [pl.program_id(k) == 0 for k in range(len(grid))])
        last = functools.reduce(jnp.logical_and, [pl.program_id(k) == grid[k] - 1 for k in range(len(grid))])

        @pl.when(first)
        def _():
            hosted.start(h_in, h_out, *scr[n_scr:])

        body(*ins, *outs, *scr[:n_scr])

        @pl.when(last)
        def _():
            hosted.wait(h_in, h_out, *scr[n_scr:])

    outs = pl.pallas_call(
        wrapped, name=name, grid=grid, in_specs=[*in_specs, *hosted.in_specs],
        out_specs=(*out_specs, *hosted.out_specs), out_shape=(*out_shape, *hosted.out_shapes),
        scratch_shapes=[*scratch_shapes, *hosted.scratch], compiler_params=params)(*args, *hosted.args)
    return list(outs[:n_out]), list(outs[n_out:])


def _sum_parts(parts, name):
    n_parts, rows, cols = parts.shape
    tr = rows
    while n_parts * tr * cols * 4 > 8 * 1024 * 1024 and tr % 16 == 0:
        tr //= 2

    def body(p_ref, o_ref):
        total = p_ref[0]
        for k in range(1, n_parts):
            total = total + p_ref[k]
        o_ref[...] = total

    return pl.pallas_call(
        body,
        name=name,
        grid=(rows // tr,),
        out_shape=jax.ShapeDtypeStruct((rows, cols), F32),
        in_specs=[pl.BlockSpec((n_parts, tr, cols), lambda i: (0, i, 0))],
        out_specs=pl.BlockSpec((tr, cols), lambda i: (i, 0)),
        compiler_params=_params(1),
    )(parts)


def _adamw(w, g, m, v, name):
    rows, cols = w.shape
    tr = rows
    while tr * cols * 4 > 2 * 1024 * 1024 and tr % 16 == 0:
        tr //= 2

    def body(w_ref, g_ref, m_ref, v_ref, d_ref, nm_ref, nv_ref):
        gv = g_ref[...]
        nm = ADAM_B1 * m_ref[...] + (1.0 - ADAM_B1) * gv
        nv = ADAM_B2 * v_ref[...] + (1.0 - ADAM_B2) * (gv * gv)
        m_hat = nm / (1.0 - ADAM_B1 ** ADAM_STEP)
        v_hat = nv / (1.0 - ADAM_B2 ** ADAM_STEP)
        d_ref[...] = -ADAM_LR * (m_hat / (jnp.sqrt(v_hat) + ADAM_EPS) + ADAM_WD * w_ref[...])
        nm_ref[...] = nm
        nv_ref[...] = nv

    spec = pl.BlockSpec((tr, cols), lambda i: (i, 0))
    shape = jax.ShapeDtypeStruct((rows, cols), F32)
    return pl.pallas_call(
        body, name=name, grid=(rows // tr,), out_shape=(shape, shape, shape),
        in_specs=[spec] * 4, out_specs=(spec, spec, spec), compiler_params=_params(1),
    )(w, g, m, v)


def _ada_rows(c_ref, cc_ref):
    rows = jnp.concatenate([c_ref[...], jnp.broadcast_to(cc_ref[...], c_ref.shape)], axis=0)
    return rows


def _ada_fwd(c_all, c_ctx, w_sh, b_sh):
    n_layers, _, ws = w_sh.shape

    def body(c_ref, cc_ref, w_ref, b_ref, o_ref):
        rows = _ada_rows(c_ref, cc_ref)
        s = rows * _sigmoid(rows)
        for i in range(n_layers):
            o_ref[i] = jnp.dot(s, w_ref[i], precision=HIGHEST, preferred_element_type=F32) + b_ref[i]

    return pl.pallas_call(
        body, name="ada_fwd", out_shape=jax.ShapeDtypeStruct((n_layers, 16, ws), F32),
        compiler_params=pltpu.CompilerParams(vmem_limit_bytes=VMEM_LIMIT),
    )(c_all, c_ctx, w_sh, b_sh)


def _ada_bwd(c_all, c_ctx, w_sh, dm_full, dm_sh):
    n_layers, d_model, ws = w_sh.shape
    n_dev = dm_full.shape[0]
    cols = dm_full.shape[-1]

    def body(c_ref, cc_ref, w_ref, dmf_ref, dms_ref, gw_ref, gb_ref, ds_ref):
        rows = _ada_rows(c_ref, cc_ref)
        s = rows * _sigmoid(rows)
        ds = jnp.zeros((8, d_model), F32)
        for i in range(n_layers):
            ctx_s = dms_ref[0, i, 1:2, :]
            ctx_f = dmf_ref[0, i, 1:2, :]
            ex_f = dmf_ref[0, i, 0:1, :]
            for k in range(1, n_dev):
                ctx_s = ctx_s + dms_ref[k, i, 1:2, :]
                ctx_f = ctx_f + dmf_ref[k, i, 1:2, :]
                ex_f = ex_f + dmf_ref[k, i, 0:1, :]
            gb_ref[i] = ex_f + ctx_f
            r = jnp.concatenate([dms_ref[k, i, 0:1, :] for k in range(n_dev)] + [ctx_s, jnp.zeros((7, ws), F32)], axis=0)
            gw_ref[i] = lax.dot_general(s, r, (((0,), (0,)), ((), ())), precision=HIGHEST, preferred_element_type=F32)
            ds = ds + lax.dot_general(jnp.broadcast_to(ctx_s, (8, ws)), w_ref[i], (((1,), (1,)), ((), ())),
                                      precision=HIGHEST, preferred_element_type=F32)
        ds_ref[...] = ds

    return pl.pallas_call(
        body, name="ada_bwd",
        out_shape=(jax.ShapeDtypeStruct((n_layers, d_model, ws), F32), jax.ShapeDtypeStruct((n_layers, 1, cols), F32),
                   jax.ShapeDtypeStruct((8, d_model), F32)),
        compiler_params=pltpu.CompilerParams(vmem_limit_bytes=VMEM_LIMIT),
    )(c_all, c_ctx, w_sh, dm_full, dm_sh)


def _cctx_grad(ds_parts, c_ctx):
    def body(p_ref, c_ref, o_ref):
        tot = p_ref[0, 0:1, :]
        for k in range(1, ds_parts.shape[0]):
            tot = tot + p_ref[k, 0:1, :]
        cv = c_ref[...]
        sg = _sigmoid(cv)
        o_ref[...] = tot * (sg * (1.0 + cv * (1.0 - sg)))

    return pl.pallas_call(body, name="cctx_grad", out_shape=jax.ShapeDtypeStruct(c_ctx.shape, F32))(ds_parts, c_ctx)


def _zoh_math(lam_re, lam_im, log_step, b_re, b_im):
    n_state = lam_re.shape[0] // 2
    dt = jnp.exp(jnp.concatenate([jnp.broadcast_to(log_step[r:r + 1], (n_state, log_step.shape[1])) for r in range(2)],
                                 axis=0))
    mag = jnp.exp(lam_re * dt)
    ar = mag * jnp.cos(lam_im * dt)
    ai = mag * jnp.sin(lam_im * dt)
    qr, qi = ar - 1.0, ai
    den = lam_re * lam_re + lam_im * lam_im
    fr = (qr * lam_re + qi * lam_im) / den
    fi = (qi * lam_re - qr * lam_im) / den
    bbr = fr[None] * b_re - fi[None] * b_im
    bbi = fr[None] * b_im + fi[None] * b_re
    return ar, ai, bbr, bbi


def _zoh_fwd(lam_re, lam_im, log_step, b_re, b_im):
    rg, n = lam_re.shape

    def body(lr_ref, li_ref, ls_ref, br_ref, bi_ref, pr_ref, pi_ref, qr_ref, qi_ref, bbr_ref, bbi_ref):
        ar, ai, bbr, bbi = _zoh_math(lr_ref[...], li_ref[...], ls_ref[...], br_ref[...], bi_ref[...])
        bbr_ref[...] = bbr
        bbi_ref[...] = bbi

        def powers(base_r, base_i, r_ref, i_ref):
            pr, pi_ = base_r, base_i
            for k in range(8):
                r_ref[k] = pr
                i_ref[k] = pi_
                pr, pi_ = pr * base_r - pi_ * base_i, pr * base_i + pi_ * base_r

        powers(ar, ai, pr_ref, pi_ref)
        powers(pr_ref[7], pi_ref[7], qr_ref, qi_ref)

    pw = jax.ShapeDtypeStruct((8, rg, n), F32)
    bb = jax.ShapeDtypeStruct(b_re.shape, F32)
    return pl.pallas_call(body, name="zoh_fwd", out_shape=(pw, pw, pw, pw, bb, bb))(lam_re, lam_im, log_step, b_re, b_im)


def _zoh_bwd(lam_re, lam_im, log_step, b_re, b_im, dar, dai, dbbr, dbbi):
    def body(lr_ref, li_ref, ls_ref, br_ref, bi_ref, dar_ref, dai_ref, dbr_ref, dbi_ref, *outs):
        _, vjp = jax.vjp(_zoh_math, lr_ref[...], li_ref[...], ls_ref[...], br_ref[...], bi_ref[...])
        grads = vjp((dar_ref[...], dai_ref[...], dbr_ref[...], dbi_ref[...]))
        for o_ref, gval in zip(outs, grads):
            o_ref[...] = gval

    shapes = tuple(jax.ShapeDtypeStruct(a.shape, F32) for a in (lam_re, lam_im, log_step, b_re, b_im))
    return pl.pallas_call(body, name="zoh_bwd", out_shape=shapes)(lam_re, lam_im, log_step, b_re, b_im, dar, dai, dbbr, dbbi)


def _inproj(xin, sc, sh, w, *, lnaff=None, tb, gb_rows=None, name, hosted=None):
    n_tok, d_model = xin.shape
    n_chunks, _, cw = w.shape
    tb = min(tb, n_tok)
    nq = cw // LANES
    has_ln = lnaff is not None
    n_out = 1 if gb_rows is None else len(gb_rows)

    def body(*refs):
        if has_ln:
            x_ref, g_ref, b_ref, sc_ref, sh_ref, w_ref = refs[:6]
        else:
            x_ref, sc_ref, sh_ref, w_ref = refs[:4]
        p_refs, h_ref = refs[-1 - n_out:-1], refs[-1]

        @pl.when(pl.program_id(1) == 0)
        def _():
            xv = x_ref[...]
            if has_ln:
                xv = xv * g_ref[...] + b_ref[...]
            h_ref[...] = (xv * (1.0 + sc_ref[...]) + sh_ref[...]).astype(BF16)

        acc = _dot(h_ref[...], w_ref[0]).astype(BF16)
        if gb_rows is None:
            p_refs[0][0] = acc
        else:
            for p_ref in p_refs:
                for q in range(nq):
                    p_ref[q] = acc[:, q * LANES:(q + 1) * LANES]

    vec = pl.BlockSpec((1, d_model), lambda i, j: (0, 0))
    in_specs = [pl.BlockSpec((tb, d_model), lambda i, j: (i, 0))] + ([vec, vec] if has_ln else []) + [
        vec, vec, pl.BlockSpec((1, d_model, cw), lambda i, j: (j, 0, 0))]
    if gb_rows is None:
        p_shapes = [jax.ShapeDtypeStruct((n_chunks, n_tok, cw), BF16)]
        p_specs = [pl.BlockSpec((1, tb, cw), lambda i, j: (j, i, 0))]
    else:
        p_shapes, p_specs = [], []
        for total, off in gb_rows:
            assert off % tb == 0
            p_shapes.append(jax.ShapeDtypeStruct((n_chunks * nq, total, LANES), BF16))
            p_specs.append(pl.BlockSpec((nq, tb, LANES), functools.partial(lambda i, j, ob: (j, i + ob, 0), ob=off // tb)))
    args = (xin,) + (tuple(lnaff) if has_ln else ()) + (sc, sh, w)
    outs, extra = _call(
        body, name=name, grid=(n_tok // tb, n_chunks), in_specs=in_specs,
        out_specs=[*p_specs, pl.BlockSpec((tb, d_model), lambda i, j: (i, 0))],
        out_shape=[*p_shapes, jax.ShapeDtypeStruct((n_tok, d_model), BF16)], scratch_shapes=[], args=args, hosted=hosted)
    return (*outs, extra) if hosted is not None else tuple(outs)


def _inproj_seq(xin, sc, sh, w, lnaff, *, tb, seq_rows, name):
    n_tok, d_model = xin.shape
    n_chunks, _, cw = w.shape
    tb = min(tb, n_tok)
    nq = cw // LANES
    n_out = len(seq_rows)
    steps = (n_tok // tb) * n_chunks

    def body(x_ref, g_ref, b_ref, sc_ref, sh_ref, w_ref, *rest):
        p_refs, h_ref, stage, sems = rest[:n_out], rest[n_out], rest[n_out + 1], rest[n_out + 2]
        i, j = pl.program_id(0), pl.program_id(1)
        step = i * n_chunks + j
        slot = step % 2

        def copies(from_slot):
            return [pltpu.make_async_copy(stage.at[from_slot],
                                          p_ref.at[pl.ds(j * nq, nq), pl.ds(off + i * tb, tb), :], sems.at[from_slot, k])
                    for k, (p_ref, (_, off)) in enumerate(zip(p_refs, seq_rows))]

        @pl.when(step >= 2)
        def _():
            for cp in copies(slot):
                cp.wait()

        @pl.when(j == 0)
        def _():
            xv = x_ref[...] * g_ref[...] + b_ref[...]
            h_ref[...] = (xv * (1.0 + sc_ref[...]) + sh_ref[...]).astype(BF16)

        acc = _dot(h_ref[...], w_ref[0]).astype(BF16)
        for q in range(nq):
            stage[slot, q] = acc[:, q * LANES:(q + 1) * LANES]
        for cp in copies(slot):
            cp.start()

        @pl.when(step == steps - 1)
        def _():
            for cp in copies(slot):
                cp.wait()
            if steps > 1:
                for cp in copies(1 - slot):
                    cp.wait()

    vec = pl.BlockSpec((1, d_model), lambda i, j: (0, 0))
    tok = pl.BlockSpec((tb, d_model), lambda i, j: (i, 0))
    return pl.pallas_call(
        body, name=name, grid=(n_tok // tb, n_chunks),
        in_specs=[tok, vec, vec, vec, vec, pl.BlockSpec((1, d_model, cw), lambda i, j: (j, 0, 0))],
        out_specs=(*[pl.BlockSpec(memory_space=pltpu.HBM)] * n_out, tok),
        out_shape=(*[jax.ShapeDtypeStruct((n_chunks * nq, total, LANES), BF16) for total, _ in seq_rows],
                   jax.ShapeDtypeStruct((n_tok, d_model), BF16)),
        scratch_shapes=[pltpu.VMEM((2, nq, tb, LANES), BF16), pltpu.SemaphoreType.DMA((2, n_out))],
        compiler_params=_params(2),
    )(xin, lnaff[0], lnaff[1], sc, sh, w)


def _shifted(u, rows, width, tb):
    col = rows % width
    um = jnp.where(col == 0, 0.0, pltpu.roll(u, 1, 0))
    up = jnp.where(col == width - 1, 0.0, pltpu.roll(u, tb - 1, 0))
    return um, up


def _slab_width(e):
    return min(512, e // 2)


def _convgate(p, x, gt, conv_w, w_out, ln_g, ln_b, *, grid_mode, tb, name):
    _, n_tok, e = p.shape
    d_model = x.shape[1]
    eh = e // 2
    if not grid_mode:
        tb = n_tok
    tb = min(tb, n_tok)
    nb = n_tok // tb
    hb = tb // GRID_W
    cs = _slab_width(e)

    def body(*refs):
        if grid_mode:
            (bg_ref, cg_ref, v_ref, z_ref, cgp_ref, vp_ref, cgn_ref, vn_ref, x_ref, gt_ref, cw_ref, wo_ref, lg_ref,
             lb_ref, xh_ref, rs_ref, g_ref, yc_ref, f_ref) = refs
        else:
            (bg_ref, cg_ref, v_ref, z_ref, x_ref, gt_ref, cw_ref, wo_ref, lg_ref, lb_ref, xh_ref, rs_ref, g_ref,
             yc_ref, f_ref) = refs
        i = pl.program_id(0)
        rows = lax.broadcasted_iota(jnp.int32, (tb, 1), 0)
        for c0 in range(0, e, cs):
            sl = slice(c0, c0 + cs)
            u = cg_ref[0, :, sl].astype(F32) * v_ref[0, :, sl].astype(F32)
            w = cw_ref[:, sl]
            if grid_mode and c0 >= eh:
                hs = slice(c0 - eh, c0 - eh + cs)
                uprev = cgp_ref[0, :, hs].astype(F32) * vp_ref[0, :, hs].astype(F32)
                unext = cgn_ref[0, :, hs].astype(F32) * vn_ref[0, :, hs].astype(F32)
                uprev = jnp.where(i > 0, uprev, 0.0)
                unext = jnp.where(i < nb - 1, unext, 0.0)
                if tb > GRID_W:
                    um = jnp.concatenate([uprev, u[:tb - GRID_W]], axis=0)
                    up = jnp.concatenate([u[GRID_W:], unext], axis=0)
                else:
                    um, up = uprev, unext
            else:
                um, up = _shifted(u, rows, GRID_W if grid_mode else tb, tb)
            yc = um * w[0:1] + u * w[1:2] + up * w[2:3]
            zf = z_ref[0, :, sl].astype(F32)
            gval = bg_ref[0, :, sl].astype(F32) * yc * (zf * _sigmoid(zf))
            yc_ref[:, sl] = yc.astype(BF16)
            g_ref[:, sl] = gval.astype(BF16)
        f = _dot(g_ref[...], wo_ref[...])
        f_ref[...] = f.astype(BF16)
        r = DN_ALPHA * x_ref[...] + gt_ref[...] * f
        rc = r - _rowmean(r)
        rstd = lax.rsqrt(_rowmean(rc * rc) + LN_EPS)
        xh_ref[...] = rc * rstd
        rs_ref[...] = rstd

    def chunk(k):
        return pl.BlockSpec((1, tb, e), lambda i: (k, i, 0))

    n_hrows = n_tok // GRID_W

    def halo_prev(k):
        return pl.BlockSpec((1, GRID_W, eh), lambda i: (k, jnp.maximum(i * hb - 1, 0), 1))

    def halo_next(k):
        return pl.BlockSpec((1, GRID_W, eh), lambda i: (k, jnp.minimum((i + 1) * hb, n_hrows - 1), 1))

    vec = pl.BlockSpec((1, d_model), lambda i: (0, 0))
    tok = pl.BlockSpec((tb, d_model), lambda i: (i, 0))
    wide = pl.BlockSpec((tb, e), lambda i: (i, 0))
    in_specs = [chunk(0), chunk(1), chunk(2), chunk(3)]
    args = [p, p, p, p]
    if grid_mode:
        in_specs += [halo_prev(1), halo_prev(2), halo_next(1), halo_next(2)]
        args += [p, p, p, p]
    in_specs += [tok, vec, pl.BlockSpec((3, e), lambda i: (0, 0)), pl.BlockSpec((e, d_model), lambda i: (0, 0)), vec, vec]
    args += [x, gt, conv_w, w_out, ln_g, ln_b]
    return pl.pallas_call(
        body, name=name, grid=(nb,),
        out_shape=(jax.ShapeDtypeStruct((n_tok, d_model), F32), jax.ShapeDtypeStruct((n_tok, 1), F32),
                   jax.ShapeDtypeStruct((n_tok, e), BF16), jax.ShapeDtypeStruct((n_tok, e), BF16),
                   jax.ShapeDtypeStruct((n_tok, d_model), BF16)),
        in_specs=in_specs, out_specs=(tok, pl.BlockSpec((tb, 1), lambda i: (i, 0)), wide, wide, tok),
        compiler_params=_params(1),
    )(*args)


def _scan_block(buf_ref, tab_ref, cr, ci, *, reverse, tb, sb):
    n_slabs = tb // SUBLANES
    unrolled = n_slabs <= 64

    def slab(s, carry):
        cr, ci = carry
        idx = (n_slabs - 1 - s) if reverse else s
        r0 = idx * SUBLANES if unrolled else pl.multiple_of(idx * SUBLANES, SUBLANES)
        xr = buf_ref[pl.ds(r0, SUBLANES), 0:sb]
        xi = buf_ref[pl.ds(r0, SUBLANES), sb:2 * sb]
        for k, step in enumerate((1, 2, 4)):
            ar = tab_ref[2 * k]
            ai = tab_ref[2 * k + 1]
            shift = (SUBLANES - step) if reverse else step
            rr = pltpu.roll(xr, shift, 0)
            ri = pltpu.roll(xi, shift, 0)
            xr, xi = xr + ar * rr - ai * ri, xi + ar * ri + ai * rr
        pr = tab_ref[6]
        pi_ = tab_ref[7]
        xr, xi = xr + pr * cr - pi_ * ci, xi + pr * ci + pi_ * cr
        buf_ref[pl.ds(r0, SUBLANES), 0:sb] = xr
        buf_ref[pl.ds(r0, SUBLANES), sb:2 * sb] = xi
        last = 0 if reverse else SUBLANES - 1
        return (jnp.broadcast_to(xr[last:last + 1, :], (SUBLANES, sb)),
                jnp.broadcast_to(xi[last:last + 1, :], (SUBLANES, sb)))

    if unrolled:
        carry = (cr, ci)
        for s in range(n_slabs):
            carry = slab(s, carry)
        return carry
    return lax.fori_loop(0, n_slabs, slab, (cr, ci))


CHUNK = SUBLANES


def _group_mask():
    r = lax.broadcasted_iota(jnp.int32, (LANES, LANES), 0)
    c = lax.broadcasted_iota(jnp.int32, (LANES, LANES), 1)
    return r // 16 == c // 16


def _s5_ops(bre, bim, cre, cim, wr, wi, *, reverse, name, hosted=None):
    ngb, _, sb = bre.shape
    n_rows = CHUNK * LANES

    def body(bre_ref, bim_ref, cre_ref, cim_ref, wr_ref, wi_ref, t_ref, bp_ref, cp_ref):
        b_re, b_im, c_re, c_im = bre_ref[0], bim_ref[0], cre_ref[0], cim_ref[0]
        mask = _group_mask()
        er, ei = [], []
        for tau in range(CHUNK + 1):
            w_r, w_i = wr_ref[0, tau:tau + 1, :], wi_ref[0, tau:tau + 1, :]
            er.append(c_re * w_r - c_im * w_i)
            ei.append(c_re * w_i + c_im * w_r)
        kt = []
        for tau in range(CHUNK):
            k = (lax.dot_general(b_re, er[tau], (((1,), (1,)), ((), ())), precision=HIGHEST, preferred_element_type=F32)
                 - lax.dot_general(b_im, ei[tau], (((1,), (1,)), ((), ())), precision=HIGHEST, preferred_element_type=F32))
            kt.append(jnp.where(mask, k, 0.0).astype(BF16))
        zero = jnp.zeros((LANES, LANES), BF16)
        for i in range(CHUNK):
            rows = slice(i * LANES, (i + 1) * LANES)
            for j in range(CHUNK):
                lag = (i - j) if reverse else (j - i)
                t_ref[0, rows, j * LANES:(j + 1) * LANES] = kt[lag] if lag >= 0 else zero
            tau = i if reverse else CHUNK - 1 - i
            w_r, w_i = wr_ref[0, tau:tau + 1, :], wi_ref[0, tau:tau + 1, :]
            bp_ref[0, rows, 0:sb] = (b_re * w_r - b_im * w_i).astype(BF16)
            bp_ref[0, rows, sb:2 * sb] = (b_re * w_i + b_im * w_r).astype(BF16)
            tau = CHUNK - i if reverse else i + 1
            cp_ref[0, rows, 0:sb] = er[tau].astype(BF16)
            cp_ref[0, rows, sb:2 * sb] = (-ei[tau]).astype(BF16)

    mat = pl.BlockSpec((1, LANES, sb), lambda g: (g, 0, 0))
    pw = pl.BlockSpec((1, CHUNK + 1, sb), lambda g: (g, 0, 0))
    outs, extra = _call(
        body, name=name, grid=(ngb,),
        out_shape=[jax.ShapeDtypeStruct((ngb, n_rows, n_rows), BF16), jax.ShapeDtypeStruct((ngb, n_rows, 2 * sb), BF16),
                   jax.ShapeDtypeStruct((ngb, n_rows, 2 * sb), BF16)],
        in_specs=[mat, mat, mat, mat, pw, pw],
        out_specs=[pl.BlockSpec((1, n_rows, n_rows), lambda g: (g, 0, 0)),
                   pl.BlockSpec((1, n_rows, 2 * sb), lambda g: (g, 0, 0)),
                   pl.BlockSpec((1, n_rows, 2 * sb), lambda g: (g, 0, 0))],
        scratch_shapes=[], args=(bre, bim, cre, cim, wr, wi), hosted=hosted)
    return (*outs, extra)


def _s5_ops_bwd(bre, bim, cre, cim, wr, wi, ar, ai, dt, dbp, dcp, da8, *, reverse, name, hosted=None):
    ngb, _, sb = bre.shape
    n_rows = CHUNK * LANES

    def dot_hi(a, b, dims):
        return lax.dot_general(a.astype(BF16), b.astype(BF16), (dims, ((), ())), preferred_element_type=F32)

    def body(bre_ref, bim_ref, cre_ref, cim_ref, wr_ref, wi_ref, ar_ref, ai_ref, dt_ref, dbp_ref, dcp_ref, da8_ref,
             dbre_ref, dbim_ref, dcre_ref, dcim_ref, da_ref):
        b_re, b_im, c_re, c_im = bre_ref[0], bim_ref[0], cre_ref[0], cim_ref[0]
        mask = _group_mask()
        w_r = [wr_ref[0, tau:tau + 1, :] for tau in range(CHUNK + 1)]
        w_i = [wi_ref[0, tau:tau + 1, :] for tau in range(CHUNK + 1)]
        der = [jnp.zeros((LANES, sb), F32) for _ in range(CHUNK + 1)]
        dei = [jnp.zeros((LANES, sb), F32) for _ in range(CHUNK + 1)]
        dwr = [jnp.zeros((1, sb), F32) for _ in range(CHUNK + 1)]
        dwi = [jnp.zeros((1, sb), F32) for _ in range(CHUNK + 1)]
        dwr[CHUNK] = da8_ref[0, :, 0:sb]
        dwi[CHUNK] = da8_ref[0, :, sb:2 * sb]
        d_bre = jnp.zeros((LANES, sb), F32)
        d_bim = jnp.zeros((LANES, sb), F32)
        dkt = [jnp.zeros((LANES, LANES), F32) for _ in range(CHUNK)]
        for i in range(CHUNK):
            rows = slice(i * LANES, (i + 1) * LANES)
            for j in range(CHUNK):
                lag = (i - j) if reverse else (j - i)
                if lag >= 0:
                    dkt[lag] = dkt[lag] + dt_ref[0, rows, j * LANES:(j + 1) * LANES]
            tau = i if reverse else CHUNK - 1 - i
            g_r, g_i = dbp_ref[0, rows, 0:sb], dbp_ref[0, rows, sb:2 * sb]
            d_bre = d_bre + g_r * w_r[tau] + g_i * w_i[tau]
            d_bim = d_bim - g_r * w_i[tau] + g_i * w_r[tau]
            dwr[tau] = dwr[tau] + _colsum(g_r * b_re + g_i * b_im)
            dwi[tau] = dwi[tau] + _colsum(g_i * b_re - g_r * b_im)
            tau = CHUNK - i if reverse else i + 1
            der[tau] = der[tau] + dcp_ref[0, rows, 0:sb]
            dei[tau] = dei[tau] - dcp_ref[0, rows, sb:2 * sb]
        d_cre = jnp.zeros((LANES, sb), F32)
        d_cim = jnp.zeros((LANES, sb), F32)
        for tau in range(CHUNK + 1):
            if tau < CHUNK:
                e_r = c_re * w_r[tau] - c_im * w_i[tau]
                e_i = c_re * w_i[tau] + c_im * w_r[tau]
                dk = jnp.where(mask, dkt[tau], 0.0)
                d_bre = d_bre + dot_hi(dk, e_r, ((1,), (0,)))
                d_bim = d_bim - dot_hi(dk, e_i, ((1,), (0,)))
                der[tau] = der[tau] + dot_hi(dk, b_re, ((0,), (0,)))
                dei[tau] = dei[tau] - dot_hi(dk, b_im, ((0,), (0,)))
            d_cre = d_cre + der[tau] * w_r[tau] + dei[tau] * w_i[tau]
            d_cim = d_cim - der[tau] * w_i[tau] + dei[tau] * w_r[tau]
            dwr[tau] = dwr[tau] + _colsum(der[tau] * c_re + dei[tau] * c_im)
            dwi[tau] = dwi[tau] + _colsum(dei[tau] * c_re - der[tau] * c_im)
        a_r, a_i = ar_ref[0], ai_ref[0]
        d_ar = jnp.zeros((1, sb), F32)
        d_ai = jnp.zeros((1, sb), F32)
        for tau in range(CHUNK, 0, -1):
            d_ar = d_ar + dwr[tau] * w_r[tau - 1] + dwi[tau] * w_i[tau - 1]
            d_ai = d_ai - dwr[tau] * w_i[tau - 1] + dwi[tau] * w_r[tau - 1]
            dwr[tau - 1], dwi[tau - 1] = (dwr[tau - 1] + dwr[tau] * a_r + dwi[tau] * a_i,
                                          dwi[tau - 1] - dwr[tau] * a_i + dwi[tau] * a_r)
        dbre_ref[0] = d_bre
        dbim_ref[0] = d_bim
        dcre_ref[0] = d_cre
        dcim_ref[0] = d_cim
        da_ref[0, :, 0:sb] = d_ar
        da_ref[0, :, sb:2 * sb] = d_ai

    mat = pl.BlockSpec((1, LANES, sb), lambda g: (g, 0, 0))
    pw = pl.BlockSpec((1, CHUNK + 1, sb), lambda g: (g, 0, 0))
    one = pl.BlockSpec((1, 1, sb), lambda g: (g, 0, 0))
    two = pl.BlockSpec((1, 1, 2 * sb), lambda g: (g, 0, 0))
    big = pl.BlockSpec((1, n_rows, n_rows), lambda g: (g, 0, 0))
    big2 = pl.BlockSpec((1, n_rows, 2 * sb), lambda g: (g, 0, 0))
    mshape = jax.ShapeDtypeStruct((ngb, LANES, sb), F32)
    outs, extra = _call(
        body, name=name, grid=(ngb,),
        out_shape=[mshape, mshape, mshape, mshape, jax.ShapeDtypeStruct((ngb, 1, 2 * sb), F32)],
        in_specs=[mat, mat, mat, mat, pw, pw, one, one, big, big2, big2, two],
        out_specs=[mat, mat, mat, mat, two],
        scratch_shapes=[], args=(bre, bim, cre, cim, wr, wi, ar, ai, dt, dbp, dcp, da8), hosted=hosted)
    return (*outs, extra)


MXU_TILE = 256


def _causal_span(tile, n_tiles, reverse, of_output):
    upto, onward = slice(0, (tile + 1) * MXU_TILE), slice(tile * MXU_TILE, n_tiles * MXU_TILE)
    return (onward if reverse else upto) if of_output else (upto if reverse else onward)


def _apply_causal(uv, t_ref, reverse):
    n_tiles = uv.shape[1] // MXU_TILE
    cols = []
    for tj in range(n_tiles):
        span = _causal_span(tj, n_tiles, reverse, True)
        cols.append(_dot(uv[:, span], t_ref[0, span, tj * MXU_TILE:(tj + 1) * MXU_TILE]))
    return jnp.concatenate(cols, axis=1)


def _apply_causal_t(dyv, t_ref, reverse):
    n_tiles = dyv.shape[1] // MXU_TILE
    cols = []
    for ti in range(n_tiles):
        span = _causal_span(ti, n_tiles, reverse, False)
        cols.append(_dot_nt(dyv[:, span], t_ref[0, ti * MXU_TILE:(ti + 1) * MXU_TILE, span]))
    return jnp.concatenate(cols, axis=1)


def _shift_rows(xv, edge, rows, n_rows, down):
    if down:
        return jnp.where(rows == 0, edge, pltpu.roll(xv, 1, 0))
    return jnp.where(rows == n_rows - 1, edge, pltpu.roll(xv, n_rows - 1, 0))


def _rows_of_tokens(tok_ref, conv_scr, rb):
    conv_scr[...] = tok_ref[0].astype(F32)
    return jnp.concatenate([conv_scr[pl.ds(j, rb, stride=CHUNK), :] for j in range(CHUNK)], axis=1).astype(BF16)


def _tokens_of_rows(val, tok_ref, conv_scr, rb):
    for j in range(CHUNK):
        conv_scr[pl.ds(j, rb, stride=CHUNK), :] = val[:, j * LANES:(j + 1) * LANES]
    tok_ref[0] = conv_scr[...].astype(BF16)


def _s5_row_block(n_seq, target=416):
    n_rows = n_seq // CHUNK
    best = 16
    for rb in range(16, min(target, n_rows) + 1, 16):
        if n_rows % rb == 0:
            best = rb
    assert n_rows % best == 0
    return best


def _s5_fwd(useq, t_op, bp, cp, tab, *, reverse, name):
    ngb, n_seq, _ = useq.shape
    sb = bp.shape[2] // 2
    width = CHUNK * LANES
    rb = _s5_row_block(n_seq)
    tbk = rb * CHUNK
    steps = n_seq // tbk

    def blk(i):
        return (steps - 1 - i) if reverse else i

    def body(u_ref, t_ref, b_ref, c_ref, tab_ref, y_ref, hp_ref, h_scr, conv_scr, carry_scr):
        i = pl.program_id(1)

        @pl.when(i == 0)
        def _():
            carry_scr[...] = jnp.zeros_like(carry_scr)

        enter = carry_scr[0:1, :]
        uv = _rows_of_tokens(u_ref, conv_scr, rb)
        h_scr[...] = _dot(uv, b_ref[0])
        cr, ci = _scan_block(h_scr, tab_ref.at[0], carry_scr[:, 0:sb], carry_scr[:, sb:2 * sb],
                             reverse=reverse, tb=rb, sb=sb)
        carry_scr[:, 0:sb] = cr
        carry_scr[:, sb:2 * sb] = ci
        rows = lax.broadcasted_iota(jnp.int32, (rb, 1), 0)
        hprev = _shift_rows(h_scr[...], enter, rows, rb, down=not reverse)
        hp_ref[0] = hprev
        _tokens_of_rows(_apply_causal(uv, t_ref, reverse) + _dot_nt(hprev.astype(BF16), c_ref[0]), y_ref, conv_scr, rb)

    op = pl.BlockSpec((1, width, width), lambda g, i: (g, 0, 0))
    op2 = pl.BlockSpec((1, width, 2 * sb), lambda g, i: (g, 0, 0))
    tok = pl.BlockSpec((1, tbk, LANES), lambda g, i: (g, blk(i), 0))
    return pl.pallas_call(
        body, name=name, grid=(ngb, steps),
        out_shape=(jax.ShapeDtypeStruct((ngb, n_seq, LANES), BF16),
                   jax.ShapeDtypeStruct((ngb, n_seq // CHUNK, 2 * sb), F32)),
        in_specs=[tok, op, op2, op2, pl.BlockSpec((1, 8, SUBLANES, sb), lambda g, i: (g, 0, 0, 0))],
        out_specs=(tok, pl.BlockSpec((1, rb, 2 * sb), lambda g, i: (g, blk(i), 0))),
        scratch_shapes=[pltpu.VMEM((rb, 2 * sb), F32), pltpu.VMEM((tbk, LANES), F32),
                        pltpu.VMEM((SUBLANES, 2 * sb), F32)],
        compiler_params=_params(2),
    )(useq, t_op, bp, cp, tab)


def _s5_bwd(useq, dy, hprev, t_op, bp, cp, tab_adj, *, reverse, name, hosted=None):
    ngb, n_seq, _ = useq.shape
    sb = bp.shape[2] // 2
    width = CHUNK * LANES
    rb = _s5_row_block(n_seq)
    tbk = rb * CHUNK
    steps = n_seq // tbk

    def blk(i):
        return i if reverse else steps - 1 - i

    def body(u_ref, dy_ref, hp_ref, t_ref, b_ref, c_ref, taba_ref, du_ref, dt_ref, db_ref, dc_ref, da_ref,
             lam_scr, conv_scr, lcarry_scr, gedge_scr, da_scr):
        i = pl.program_id(1)
        first = i == 0

        _zero_first(first, lcarry_scr, gedge_scr, da_scr, dt_ref, db_ref, dc_ref)
        rows = lax.broadcasted_iota(jnp.int32, (rb, 1), 0)
        uv = _rows_of_tokens(u_ref, conv_scr, rb)
        dyv = _rows_of_tokens(dy_ref, conv_scr, rb)
        gy = _dot(dyv, c_ref[0])
        edge = gy[rb - 1:rb, :] if reverse else gy[0:1, :]
        lam_scr[...] = _shift_rows(gy, gedge_scr[...], rows, rb, down=reverse)
        gedge_scr[...] = edge
        lr, li = _scan_block(lam_scr, taba_ref.at[0], lcarry_scr[:, 0:sb], lcarry_scr[:, sb:2 * sb],
                             reverse=not reverse, tb=rb, sb=sb)
        lcarry_scr[:, 0:sb] = lr
        lcarry_scr[:, sb:2 * sb] = li

        lam = lam_scr[...]
        lam_bf = lam.astype(BF16)
        _tokens_of_rows(_apply_causal_t(dyv, t_ref, reverse) + _dot_nt(lam_bf, b_ref[0]), du_ref, conv_scr, rb)
        for tj in range(width // MXU_TILE):
            span, cols = _causal_span(tj, width // MXU_TILE, reverse, True), slice(tj * MXU_TILE, (tj + 1) * MXU_TILE)
            _acc(dt_ref.at[0, span, cols], _dot_tn(uv[:, span], dyv[:, cols]))
        _acc(db_ref.at[0], _dot_tn(uv, lam_bf))
        _acc(dc_ref.at[0], _dot_tn(dyv, hp_ref[0].astype(BF16)))
        lam_r, lam_i = lam[:, 0:sb], lam[:, sb:2 * sb]
        hp_r, hp_i = hp_ref[0, :, 0:sb], hp_ref[0, :, sb:2 * sb]
        da_scr[:, 0:sb] += _colsum(lam_r * hp_r + lam_i * hp_i)
        da_scr[:, sb:2 * sb] += _colsum(lam_i * hp_r - lam_r * hp_i)

        @pl.when(i == steps - 1)
        def _():
            da_ref[0] = da_scr[...]

    op = pl.BlockSpec((1, width, width), lambda g, i: (g, 0, 0))
    op2 = pl.BlockSpec((1, width, 2 * sb), lambda g, i: (g, 0, 0))
    tabs = pl.BlockSpec((1, 8, SUBLANES, sb), lambda g, i: (g, 0, 0, 0))
    tok = pl.BlockSpec((1, tbk, LANES), lambda g, i: (g, blk(i), 0))
    outs, extra = _call(
        body, name=name, grid=(ngb, steps),
        out_shape=[jax.ShapeDtypeStruct((ngb, n_seq, LANES), BF16),
                   jax.ShapeDtypeStruct((ngb, width, width), F32),
                   jax.ShapeDtypeStruct((ngb, width, 2 * sb), F32),
                   jax.ShapeDtypeStruct((ngb, width, 2 * sb), F32),
                   jax.ShapeDtypeStruct((ngb, 1, 2 * sb), F32)],
        in_specs=[tok, tok, pl.BlockSpec((1, rb, 2 * sb), lambda g, i: (g, blk(i), 0)), op, op2, op2, tabs],
        out_specs=[tok, op, op2, op2, pl.BlockSpec((1, 1, 2 * sb), lambda g, i: (g, 0, 0))],
        scratch_shapes=[pltpu.VMEM((rb, 2 * sb), F32), pltpu.VMEM((tbk, LANES), F32),
                        pltpu.VMEM((SUBLANES, 2 * sb), F32), pltpu.VMEM((1, 2 * sb), F32), pltpu.VMEM((1, 2 * sb), F32)],
        args=(useq, dy, hprev, t_op, bp, cp, tab_adj), hosted=hosted)
    return (*outs, extra)


def _glu_loss(useq, yf, yb, z, xhat0, ln0, gt, d_vec, w_glu, b_glu, w_out, ln1, target, *, offs, dy_rows, tb, name):
    ngb = useq.shape[0]
    n_tok, d_model = xhat0.shape
    e = ngb * LANES
    tb = min(tb, n_tok)
    assert all(off % tb == 0 for off in offs) and all(off % tb == 0 for _, off in dy_rows)
    nz = z.shape[0]

    def body(u_ref, yf_ref, yb_ref, z_ref, xh0_ref, g0_ref, b0_ref, gt_ref, d_ref, wg_ref, bg_ref, wo_ref, g1_ref,
             b1_ref, t_ref, loss_ref, dxr_ref, do_ref, gz_ref, gg_ref, dq_ref, dz_ref, dyf_ref, dyb_ref, dg1_ref, db1_ref,
             dgt_ref, dbg_ref, dd_ref, loss_scr, yl_scr, th_scr, s_scr, dg_scr):
        i = pl.program_id(0)
        _zero_first(i == 0, loss_scr, dg1_ref, db1_ref, dgt_ref, dbg_ref, dd_ref)
        zw = e // nz
        cs = min(512, zw)

        def z_slab(c0):
            return z_ref[c0 // zw, :, c0 % zw:c0 % zw + cs].astype(F32)

        for q in range(ngb):
            sl = slice(q * LANES, (q + 1) * LANES)
            yl = d_ref[:, sl] * u_ref[q].astype(F32) + yf_ref[q].astype(F32) + yb_ref[q].astype(F32)
            th = jnp.tanh(GELU_K * (yl + GELU_C * yl * yl * yl))
            yl_scr[:, sl] = yl
            th_scr[:, sl] = th
            gg_ref[:, sl] = (0.5 * yl * (1.0 + th)).astype(BF16)
        g_all = gg_ref[...]
        for c0 in range(0, e, cs):
            sl = slice(c0, c0 + cs)
            s = _sigmoid(_dot(g_all, wg_ref[:, sl]) + bg_ref[:, sl])
            s_scr[:, sl] = s
            zf = z_slab(c0)
            g2 = 0.5 * yl_scr[:, sl] * (1.0 + th_scr[:, sl]) * s
            gz_ref[:, sl] = (g2 * (zf * _sigmoid(zf))).astype(BF16)
        o = _dot(gz_ref[...], wo_ref[...])
        x1 = xh0_ref[...] * g0_ref[...] + b0_ref[...]
        r = DN_ALPHA * x1 + gt_ref[...] * o
        rc = r - _rowmean(r)
        rstd = lax.rsqrt(_rowmean(rc * rc) + LN_EPS)
        xh = rc * rstd
        err = xh * g1_ref[...] + b1_ref[...] - t_ref[...]
        _acc(loss_scr, _colsum(err * err))
        dy = err * (1.0 / d_model)
        _acc(dg1_ref, _colsum(dy * xh))
        _acc(db1_ref, _colsum(dy))
        dxh = dy * g1_ref[...]
        dr = rstd * (dxh - _rowmean(dxh) - xh * _rowmean(dxh * xh))
        dxr_ref[...] = DN_ALPHA * dr
        _acc(dgt_ref, _colsum(dr * o))
        do_bf = (dr * gt_ref[...]).astype(BF16)
        do_ref[...] = do_bf
        for c0 in range(0, e, cs):
            sl = slice(c0, c0 + cs)
            dgz = _dot_nt(do_bf, wo_ref[sl, :])
            zf = z_slab(c0)
            sz = _sigmoid(zf)
            g = 0.5 * yl_scr[:, sl] * (1.0 + th_scr[:, sl])
            s = s_scr[:, sl]
            dg2 = dgz * (zf * sz)
            dz_ref[:, sl] = (dgz * (g * s) * (sz * (1.0 + zf * (1.0 - sz)))).astype(BF16)
            dq = dg2 * g * s * (1.0 - s)
            _acc(dbg_ref.at[:, sl], _colsum(dq))
            dq_ref[:, sl] = dq.astype(BF16)
            dg_scr[:, sl] = dg2 * s
        dq_all = dq_ref[...]
        for c0 in range(0, e, cs):
            dg_blk = dg_scr[:, c0:c0 + cs] + _dot_nt(dq_all, wg_ref[c0:c0 + cs, :])
            for q in range(c0 // LANES, (c0 + cs) // LANES):
                sl = slice(q * LANES, (q + 1) * LANES)
                yl = yl_scr[:, sl]
                th = th_scr[:, sl]
                dgelu = 0.5 * (1.0 + th) + 0.5 * yl * (1.0 - th * th) * (GELU_K * (1.0 + 3.0 * GELU_C * yl * yl))
                dyl = dg_blk[:, q * LANES - c0:(q + 1) * LANES - c0] * dgelu
                _acc(dd_ref.at[:, sl], _colsum(dyl * u_ref[q].astype(F32)))
                dyf_ref[q] = dyl.astype(BF16)
                dyb_ref[q] = dyl.astype(BF16)

        @pl.when(i == pl.num_programs(0) - 1)
        def _():
            loss_ref[...] = (0.5 / d_model) * jnp.sum(loss_scr[...], axis=1, keepdims=True)

    vec = pl.BlockSpec((1, d_model), lambda i: (0, 0))
    evec = pl.BlockSpec((1, e), lambda i: (0, 0))
    tok = pl.BlockSpec((tb, d_model), lambda i: (i, 0))
    wide = pl.BlockSpec((tb, e), lambda i: (i, 0))
    def gblk(off):
        return pl.BlockSpec((ngb, tb, LANES), functools.partial(lambda i, ob: (0, i + ob, 0), ob=off // tb))

    once = dict(pipeline_mode=pl.Buffered(1))
    tok_f = jax.ShapeDtypeStruct((n_tok, d_model), F32)
    tok_b = jax.ShapeDtypeStruct((n_tok, d_model), BF16)
    wide_b = jax.ShapeDtypeStruct((n_tok, e), BF16)
    vec_f = jax.ShapeDtypeStruct((1, d_model), F32)
    evec_f = jax.ShapeDtypeStruct((1, e), F32)
    return pl.pallas_call(
        body, name=name, grid=(n_tok // tb,),
        out_shape=(jax.ShapeDtypeStruct((1, 1), F32), tok_f, tok_b, wide_b, wide_b, wide_b, wide_b,
                   *[jax.ShapeDtypeStruct((ngb, total, LANES), BF16) for total, _ in dy_rows],
                   vec_f, vec_f, vec_f, evec_f, evec_f),
        in_specs=[gblk(offs[0]), gblk(offs[1]), gblk(offs[2]),
                  pl.BlockSpec((nz, tb, e // nz), lambda i: (0, i, 0)), tok, vec, vec, vec, evec,
                  pl.BlockSpec((e, e), lambda i: (0, 0), **once), evec,
                  pl.BlockSpec((e, d_model), lambda i: (0, 0), **once), vec, vec, tok],
        out_specs=(pl.BlockSpec((1, 1), lambda i: (0, 0)), tok, tok, wide, wide, wide, wide,
                   *[gblk(off) for _, off in dy_rows], vec, vec, vec, evec, evec),
        scratch_shapes=[pltpu.VMEM((1, d_model), F32)] + [pltpu.VMEM((tb, e), F32)] * 4,
        compiler_params=_params(1),
    )(useq, yf, yb, z, xhat0, ln0[0], ln0[1], gt, d_vec, w_glu, b_glu, w_out, ln1[0], ln1[1], target)


def _ssm_inbwd(duf, dub, w, xhat, rstd, ln, sc, gt_prev, f_prev, *, lat, row_f, row_b, tb, name):
    ngb = duf.shape[0]
    e = ngb * LANES
    n_tok, d_model = xhat.shape
    tb = min(tb, n_tok)
    obf, obb = row_f // tb, row_b // tb
    has_lat = lat is not None
    n_w = w.shape[0] if has_lat else w.shape[0] // 2

    def body(*refs):
        if has_lat:
            (duf_ref, dub_ref, dyl_ref, dz_ref, d_ref, dxr_ref, w_ref, xh_ref, rs_ref, g_ref, b_ref, sc_ref, gt_ref,
             f_ref, dp_ref, dr_ref, df_ref, dsc_ref, dsh_ref, dg_ref, db_ref, dgt_ref) = refs
        else:
            (duf_ref, dub_ref, w_ref, xh_ref, rs_ref, g_ref, b_ref, sc_ref, gt_ref, f_ref, dp_ref, dr_ref, df_ref,
             dsc_ref, dsh_ref, dg_ref, db_ref, dgt_ref) = refs
        _zero_first(pl.program_id(0) == 0, dsc_ref, dsh_ref, dg_ref, db_ref, dgt_ref)
        du = (jnp.concatenate([duf_ref[q] for q in range(ngb)], axis=1).astype(F32)
              + jnp.concatenate([dub_ref[q] for q in range(ngb)], axis=1).astype(F32))
        if has_lat:
            du = du + d_ref[...] * jnp.concatenate([dyl_ref[q] for q in range(ngb)], axis=1).astype(F32)
            dp_ref[:, e:2 * e] = dz_ref[...]
        else:
            dp_ref[:, e:2 * e] = jnp.zeros((tb, e), BF16)
        dp_ref[:, 0:e] = du.astype(BF16)
        dh = jnp.zeros((tb, d_model), F32)
        for j in range(n_w):
            dh = dh + _dot(dp_ref[:, j * d_model:(j + 1) * d_model], w_ref[j])
        xh = xh_ref[...]
        x1 = xh * g_ref[...] + b_ref[...]
        dx1 = dh * (1.0 + sc_ref[...])
        if has_lat:
            dx1 = dx1 + dxr_ref[...]
        _acc(dsc_ref, _colsum(dh * x1))
        _acc(dsh_ref, _colsum(dh))
        _acc(dg_ref, _colsum(dx1 * xh))
        _acc(db_ref, _colsum(dx1))
        dxh = dx1 * g_ref[...]
        dr = rs_ref[...] * (dxh - _rowmean(dxh) - xh * _rowmean(dxh * xh))
        dr_ref[...] = dr
        df_ref[...] = (dr * gt_ref[...]).astype(BF16)
        _acc(dgt_ref, _colsum(dr * f_ref[...].astype(F32)))

    vec = pl.BlockSpec((1, d_model), lambda i: (0, 0))
    tok = pl.BlockSpec((tb, d_model), lambda i: (i, 0))
    gblk = pl.BlockSpec((ngb, tb, LANES), lambda i: (0, i, 0))
    in_specs = [pl.BlockSpec((ngb, tb, LANES), lambda i: (0, i + obf, 0)),
                pl.BlockSpec((ngb, tb, LANES), lambda i: (0, i + obb, 0))]
    args = [duf, dub]
    if has_lat:
        in_specs += [gblk, pl.BlockSpec((tb, e), lambda i: (i, 0)), pl.BlockSpec((1, e), lambda i: (0, 0)), tok]
        args += list(lat)
    in_specs += [pl.BlockSpec(w.shape, lambda i: (0, 0, 0)), tok, pl.BlockSpec((tb, 1), lambda i: (i, 0)), vec, vec, vec,
                 vec, tok]
    args += [w, xhat, rstd, ln[0], ln[1], sc, gt_prev, f_prev]
    vec_f = jax.ShapeDtypeStruct((1, d_model), F32)
    return pl.pallas_call(
        body, name=name, grid=(n_tok // tb,),
        out_shape=(jax.ShapeDtypeStruct((n_tok, 2 * e), BF16), jax.ShapeDtypeStruct((n_tok, d_model), F32),
                   jax.ShapeDtypeStruct((n_tok, d_model), BF16), vec_f, vec_f, vec_f, vec_f, vec_f),
        in_specs=in_specs,
        out_specs=(pl.BlockSpec((tb, 2 * e), lambda i: (i, 0)), tok, tok, vec, vec, vec, vec, vec),
        compiler_params=_params(1),
    )(*args)


def _conv_bwd_a(df, w_out_t, p, yc, *, tb, name, hosted=None):
    _, n_tok, e = p.shape
    d_model = df.shape[1]
    tb = min(tb, n_tok)
    cs = _slab_width(e)

    def body(df_ref, wo_ref, bg_ref, z_ref, yc_ref, dbg_ref, dz_ref, dyc_ref):
        dfv = df_ref[...]
        for c0 in range(0, e, cs):
            sl = slice(c0, c0 + cs)
            dgv = _dot(dfv, wo_ref[:, sl])
            zf = z_ref[0, :, sl].astype(F32)
            sz = _sigmoid(zf)
            silu_z = zf * sz
            bg = bg_ref[0, :, sl].astype(F32)
            yc = yc_ref[:, sl].astype(F32)
            dbg_ref[:, sl] = (dgv * yc * silu_z).astype(BF16)
            dyc_ref[:, sl] = (dgv * bg * silu_z).astype(BF16)
            dz_ref[:, sl] = (dgv * bg * yc * (sz * (1.0 + zf * (1.0 - sz)))).astype(BF16)

    wide = pl.BlockSpec((tb, e), lambda i: (i, 0))
    shape = jax.ShapeDtypeStruct((n_tok, e), BF16)
    outs, extra = _call(
        body, name=name, grid=(n_tok // tb,), out_shape=[shape, shape, shape],
        in_specs=[pl.BlockSpec((tb, d_model), lambda i: (i, 0)), pl.BlockSpec((d_model, e), lambda i: (0, 0)),
                  pl.BlockSpec((1, tb, e), lambda i: (0, i, 0)), pl.BlockSpec((1, tb, e), lambda i: (3, i, 0)), wide],
        out_specs=[wide, wide, wide], scratch_shapes=[], args=(df, w_out_t, p, p, yc), hosted=hosted)
    return (*outs, extra)


def _conv_bwd_b(dyc, p, dbg, dz, conv_w, *, grid_mode, tb, name, hosted=None):
    _, n_tok, e = p.shape
    eh = e // 2
    if not grid_mode:
        tb = n_tok
    tb = min(tb, n_tok)
    nb = n_tok // tb
    hb = tb // GRID_W
    cs = _slab_width(e)

    def body(*refs):
        if grid_mode:
            dyc_ref, dycp_ref, dycn_ref, cg_ref, v_ref, dbg_ref, dz_ref, cw_ref, dp_ref, dcw_ref = refs
        else:
            dyc_ref, cg_ref, v_ref, dbg_ref, dz_ref, cw_ref, dp_ref, dcw_ref = refs
        i = pl.program_id(0)
        _zero_first(i == 0, dcw_ref)
        rows = lax.broadcasted_iota(jnp.int32, (tb, 1), 0)
        dp_ref[0] = dbg_ref[...]
        dp_ref[3] = dz_ref[...]
        for c0 in range(0, e, cs):
            sl = slice(c0, c0 + cs)
            dyc = dyc_ref[:, sl].astype(F32)
            w = cw_ref[:, sl]
            if grid_mode and c0 >= eh:
                hs = slice(c0 - eh, c0 - eh + cs)
                dprev = jnp.where(i > 0, dycp_ref[:, hs].astype(F32), 0.0)
                dnext = jnp.where(i < nb - 1, dycn_ref[:, hs].astype(F32), 0.0)
                if tb > GRID_W:
                    dm = jnp.concatenate([dprev, dyc[:tb - GRID_W]], axis=0)
                    dpl = jnp.concatenate([dyc[GRID_W:], dnext], axis=0)
                else:
                    dm, dpl = dprev, dnext
            else:
                dm, dpl = _shifted(dyc, rows, GRID_W if grid_mode else tb, tb)
            cg = cg_ref[0, :, sl].astype(F32)
            v = v_ref[0, :, sl].astype(F32)
            u = cg * v
            du = w[0:1] * dpl + w[1:2] * dyc + w[2:3] * dm
            dp_ref[1, :, sl] = (du * v).astype(BF16)
            dp_ref[2, :, sl] = (du * cg).astype(BF16)
            _acc(dcw_ref.at[:, sl], jnp.concatenate([_colsum(u * dpl), _colsum(u * dyc), _colsum(u * dm)], axis=0))

    n_hrows = n_tok // GRID_W
    wide = pl.BlockSpec((tb, e), lambda i: (i, 0))
    in_specs = [wide]
    args = [dyc]
    if grid_mode:
        in_specs += [pl.BlockSpec((GRID_W, eh), lambda i: (jnp.maximum(i * hb - 1, 0), 1)),
                     pl.BlockSpec((GRID_W, eh), lambda i: (jnp.minimum((i + 1) * hb, n_hrows - 1), 1))]
        args += [dyc, dyc]
    in_specs += [pl.BlockSpec((1, tb, e), lambda i: (1, i, 0)), pl.BlockSpec((1, tb, e), lambda i: (2, i, 0)), wide, wide,
                 pl.BlockSpec((3, e), lambda i: (0, 0))]
    args += [p, p, dbg, dz, conv_w]
    outs, extra = _call(
        body, name=name, grid=(nb,),
        out_shape=[jax.ShapeDtypeStruct((4, n_tok, e), BF16), jax.ShapeDtypeStruct((3, e), F32)],
        in_specs=in_specs,
        out_specs=[pl.BlockSpec((4, tb, e), lambda i: (0, i, 0)), pl.BlockSpec((3, e), lambda i: (0, 0))],
        scratch_shapes=[], args=args, hosted=hosted)
    return (*outs, extra)


def _conv_inbwd(dp, w, dr, x, sc, *, tb, name, hosted=None):
    n_chunks, n_tok, e = dp.shape
    d_model = x.shape[1]
    tb = min(tb, n_tok)

    def body(dp_ref, w_ref, dr_ref, x_ref, sc_ref, gx_ref, dsc_ref, dsh_ref):
        _zero_first(pl.program_id(0) == 0, dsc_ref, dsh_ref)
        dh = _dot(dp_ref[0], w_ref[0])
        for k in range(1, n_chunks):
            dh = dh + _dot(dp_ref[k], w_ref[k])
        gx_ref[...] = DN_ALPHA * dr_ref[...] + dh * (1.0 + sc_ref[...])
        _acc(dsc_ref, _colsum(dh * x_ref[...]))
        _acc(dsh_ref, _colsum(dh))

    vec = pl.BlockSpec((1, d_model), lambda i: (0, 0))
    tok = pl.BlockSpec((tb, d_model), lambda i: (i, 0))
    vec_f = jax.ShapeDtypeStruct((1, d_model), F32)
    outs, extra = _call(
        body, name=name, grid=(n_tok // tb,),
        out_shape=[jax.ShapeDtypeStruct((n_tok, d_model), F32), vec_f, vec_f],
        in_specs=[pl.BlockSpec((n_chunks, tb, e), lambda i: (0, i, 0)),
                  pl.BlockSpec((n_chunks, e, d_model), lambda i: (0, 0, 0), pipeline_mode=pl.Buffered(1)),
                  tok, tok, vec],
        out_specs=[tok, vec, vec],
        scratch_shapes=[], args=(dp, w, dr, x, sc), hosted=hosted)
    return (*outs, extra)


def _wgrad(a, b, *, n_chunks, tm, tl, init=None, name):
    n_tok, m = a.shape
    tl = min(tl, n_tok)
    chunked = b.ndim == 3
    cw = b.shape[2] if chunked else b.shape[1] // n_chunks
    has_init = init is not None

    def body(*refs):
        if has_init:
            a_ref, b_ref, init_ref, o_ref = refs
        else:
            a_ref, b_ref, o_ref = refs
        @pl.when(pl.program_id(2) == 0)
        def _():
            o_ref[0] = init_ref[0] if has_init else jnp.zeros_like(o_ref[0])

        o_ref[0] += _dot_tn(a_ref[...], b_ref[0] if chunked else b_ref[...])

    o_spec = pl.BlockSpec((1, tm, cw), lambda jm, jc, l: (jc, jm, 0))
    b_spec = (pl.BlockSpec((1, tl, cw), lambda jm, jc, l: (jc, l, 0)) if chunked
              else pl.BlockSpec((tl, cw), lambda jm, jc, l: (l, jc)))
    init_spec = pl.BlockSpec((1, tm, cw), lambda jm, jc, l: (jc, jm, 0), pipeline_mode=pl.Buffered(1))
    in_specs = [pl.BlockSpec((tl, tm), lambda jm, jc, l: (l, jm)), b_spec] + ([init_spec] if has_init else [])
    args = (a, b) + ((init,) if has_init else ())
    return pl.pallas_call(
        body, name=name, grid=(m // tm, n_chunks, n_tok // tl),
        out_shape=jax.ShapeDtypeStruct((n_chunks, m, cw), F32),
        in_specs=in_specs, out_specs=o_spec, compiler_params=_params(3),
    )(*args)


def _block_diag(t, ngb):
    g, p, n = t.shape
    gpb = g // ngb
    eye = jnp.eye(gpb, dtype=t.dtype)
    return jnp.einsum("bgpn,gh->bgphn", t.reshape(ngb, gpb, p, n), eye).reshape(ngb, gpb * p, gpb * n)


def _block_diag_t(mat, g, p, n):
    ngb = mat.shape[0]
    gpb = g // ngb
    eye = jnp.eye(gpb, dtype=mat.dtype)
    return jnp.einsum("bgphn,gh->bgpn", mat.reshape(ngb, gpb, p, gpb, n), eye).reshape(g, p, n)


def _scan_tables(pw_r, pw_i, ngb, reverse):
    _, g, n = pw_r.shape
    sb = g * n // ngb
    rows = jnp.arange(SUBLANES)
    kinds = []
    for step in (1, 2, 4):
        mask = ((rows < SUBLANES - step) if reverse else (rows >= step)).astype(F32)
        for part in (pw_r[step - 1], pw_i[step - 1]):
            kinds.append(part.reshape(ngb, 1, sb) * mask[None, :, None])
    for part in (pw_r, pw_i):
        pw = part[::-1] if reverse else part
        kinds.append(jnp.transpose(pw.reshape(SUBLANES, ngb, sb), (1, 0, 2)))
    return jnp.stack(kinds, axis=1)


def _flat(parts):
    return jnp.concatenate([p.reshape(-1) for p in parts])


def _unflat(vec, shapes):
    out, off = [], 0
    for s in shapes:
        size = math.prod(s)
        out.append(vec[off:off + size].reshape(s))
        off += size
    return out


def kernel(x, c, ctx, c_ctx, ada_w, ada_b, ln_g, ln_b, conv_w_in, conv_w, conv_w_out, ssm_w_in, ssm_lam_re, ssm_lam_im, ssm_log_step, ssm_b_re, ssm_b_im, ssm_c_re, ssm_c_im, ssm_d, ssm_w_glu, ssm_b_glu, ssm_w_out, loss_target, m_c_ctx, m_ada_w, m_ada_b, m_ln_g, m_ln_b, m_conv_w_in, m_conv_w, m_conv_w_out, m_ssm_w_in, m_ssm_lam_re, m_ssm_lam_im, m_ssm_log_step, m_ssm_b_re, m_ssm_b_im, m_ssm_c_re, m_ssm_c_im, m_ssm_d, m_ssm_w_glu, m_ssm_b_glu, m_ssm_w_out, v_c_ctx, v_ada_w, v_ada_b, v_ln_g, v_ln_b, v_conv_w_in, v_conv_w, v_conv_w_out, v_ssm_w_in, v_ssm_lam_re, v_ssm_lam_im, v_ssm_log_step, v_ssm_b_re, v_ssm_b_im, v_ssm_c_re, v_ssm_c_im, v_ssm_d, v_ssm_w_glu, v_ssm_b_glu, v_ssm_w_out):
    weights = dict(c_ctx=c_ctx, ada_w=ada_w, ada_b=ada_b, ln_g=ln_g, ln_b=ln_b, conv_w_in=conv_w_in, conv_w=conv_w,
                   conv_w_out=conv_w_out, ssm_w_in=ssm_w_in, ssm_lam_re=ssm_lam_re, ssm_lam_im=ssm_lam_im,
                   ssm_log_step=ssm_log_step, ssm_b_re=ssm_b_re, ssm_b_im=ssm_b_im, ssm_c_re=ssm_c_re,
                   ssm_c_im=ssm_c_im, ssm_d=ssm_d, ssm_w_glu=ssm_w_glu, ssm_b_glu=ssm_b_glu, ssm_w_out=ssm_w_out)
    mom_m = dict(c_ctx=m_c_ctx, ada_w=m_ada_w, ada_b=m_ada_b, ln_g=m_ln_g, ln_b=m_ln_b, conv_w_in=m_conv_w_in,
                 conv_w=m_conv_w, conv_w_out=m_conv_w_out, ssm_w_in=m_ssm_w_in, ssm_lam_re=m_ssm_lam_re,
                 ssm_lam_im=m_ssm_lam_im, ssm_log_step=m_ssm_log_step, ssm_b_re=m_ssm_b_re, ssm_b_im=m_ssm_b_im,
                 ssm_c_re=m_ssm_c_re, ssm_c_im=m_ssm_c_im, ssm_d=m_ssm_d, ssm_w_glu=m_ssm_w_glu,
                 ssm_b_glu=m_ssm_b_glu, ssm_w_out=m_ssm_w_out)
    mom_v = dict(c_ctx=v_c_ctx, ada_w=v_ada_w, ada_b=v_ada_b, ln_g=v_ln_g, ln_b=v_ln_b, conv_w_in=v_conv_w_in,
                 conv_w=v_conv_w, conv_w_out=v_conv_w_out, ssm_w_in=v_ssm_w_in, ssm_lam_re=v_ssm_lam_re,
                 ssm_lam_im=v_ssm_lam_im, ssm_log_step=v_ssm_log_step, ssm_b_re=v_ssm_b_re, ssm_b_im=v_ssm_b_im,
                 ssm_c_re=v_ssm_c_re, ssm_c_im=v_ssm_c_im, ssm_d=v_ssm_d, ssm_w_glu=v_ssm_w_glu,
                 ssm_b_glu=v_ssm_b_glu, ssm_w_out=v_ssm_w_out)
    names = list(weights)

    n_lat, d_model = x.shape[1], x.shape[2]
    n_ctx = ctx.shape[1]
    e = 2 * d_model
    n_grp, n_state, grp = ssm_lam_re.shape[2], ssm_lam_re.shape[3], ssm_b_re.shape[4]
    ngb = e // LANES
    ws = ada_w.shape[2]
    tb_tok = min(512, n_lat)
    n_seq = n_ctx + n_lat
    tb_glu = math.gcd(256, n_ctx)
    chip = 2 * lax.axis_index("x") + lax.axis_index("y")
    me = 2 * chip + lax.axis_index("c")
    chips, everyone, pair = ("x", "y"), MESH_AXES, ("c",)

    x2, ctx2, tgt2 = x[0], ctx[0], loss_target[0]

    wc_in_own = conv_w_in[0].astype(BF16)
    later_weights = _Hosted([(w[0].astype(BF16), chips, False) for w in (conv_w_out, ssm_w_in, ssm_w_glu, ssm_w_out)])
    small_full = _exchange(_flat([conv_w[0], ssm_d[0], ssm_b_glu[0]]).reshape(1, -1), chips, False, "ag_small")
    es = conv_w.shape[2]
    conv_w_full = jnp.transpose(small_full[:, 0, :3 * es].reshape(4, 3, es), (1, 0, 2)).reshape(3, e)
    d_full = small_full[:, 0, 3 * es:4 * es].reshape(1, e)
    b_glu_full = small_full[:, 0, 4 * es:5 * es].reshape(1, e)

    c_all = _exchange(c, everyone, False, "ag_c").reshape(8, d_model)
    cc2 = c_ctx.reshape(1, d_model)
    b_sh = lax.dynamic_slice_in_dim(ada_b, chip * ws, ws, axis=1).reshape(DEPTH, 1, ws)
    m_sh = _ada_fwd(c_all, cc2, ada_w, b_sh)
    m_all = _exchange(m_sh, chips, False, "ag_mod")
    m_full = jnp.transpose(m_all, (1, 2, 0, 3)).reshape(DEPTH, 16, 3 * d_model)
    m_lat = lax.dynamic_slice_in_dim(m_full, me, 1, axis=1)
    m_ctx = m_full[:, 8:9]

    def mods(m, i):
        return m[i, :, 0:d_model], m[i, :, d_model:2 * d_model], m[i, :, 2 * d_model:3 * d_model]

    sh0, sc0, gt0 = mods(m_lat, 0)
    sh1, sc1, gt1 = mods(m_lat, 1)
    shc0, scc0, gtc0 = mods(m_ctx, 0)
    shc1, scc1, _ = mods(m_ctx, 1)
    ln0 = (ln_g[0:1], ln_b[0:1])
    ln1 = (ln_g[1:2], ln_b[1:2])

    def lam_view(t):
        return jnp.transpose(t[0], (0, 2, 1)).reshape(2 * n_state, n_grp)

    def lam_back(t):
        return jnp.transpose(t.reshape(2, n_state, n_grp), (0, 2, 1)).reshape(ssm_lam_re.shape)

    def b_view(t):
        return jnp.transpose(t[0], (0, 2, 3, 1)).reshape(2 * n_state * grp, n_grp)

    def b_back(t):
        return jnp.transpose(t.reshape(2, n_state, grp, n_grp), (0, 3, 1, 2)).reshape(ssm_b_re.shape)

    def c_view(t):
        return jnp.transpose(t[0], (0, 2, 3, 1)).reshape(2 * grp * n_state, n_grp)

    def c_back(t):
        return jnp.transpose(t.reshape(2, grp, n_state, n_grp), (0, 3, 1, 2)).reshape(ssm_c_re.shape)

    def channel_major(t):
        return jnp.transpose(t.reshape(2 * n_state, grp, n_grp), (1, 0, 2))

    def by_group(t):
        return jnp.transpose(t.reshape(t.shape[0], 2, n_state, n_grp), (0, 1, 3, 2))

    lam_re2, lam_im2, log_step2 = lam_view(ssm_lam_re), lam_view(ssm_lam_im), ssm_log_step[0]
    b_re_t, b_im_t = channel_major(b_view(ssm_b_re)), channel_major(b_view(ssm_b_im))
    pw_r, pw_i, pq_r, pq_i, bbr, bbi = _zoh_fwd(lam_re2, lam_im2, log_step2, b_re_t, b_im_t)
    sbk = n_grp * n_state // ngb
    pw_r, pw_i, pq_r, pq_i = (by_group(t) for t in (pw_r, pw_i, pq_r, pq_i))
    bbr_g = jnp.transpose(by_group(bbr), (1, 2, 0, 3))
    bbi_g = jnp.transpose(by_group(bbi), (1, 2, 0, 3))

    def power_rows(pw, r, first):
        full = jnp.concatenate([jnp.full((1, n_grp, n_state), first, F32), pw[:, r]], axis=0)
        return jnp.transpose(full.reshape(CHUNK + 1, ngb, sbk), (1, 0, 2))

    s5 = []
    for r in range(2):
        prm = dict(bre=_block_diag(bbr_g[r], ngb), bim=_block_diag(bbi_g[r], ngb),
                   cre=_block_diag(ssm_c_re[0, r], ngb), cim=_block_diag(ssm_c_im[0, r], ngb),
                   wr=power_rows(pw_r, r, 1.0), wi=power_rows(pw_i, r, 0.0))
        half_rows = wc_in_own[r * (d_model // 2):(r + 1) * (d_model // 2)]
        t_op, bp_op, cp_op, (wc_in_half,) = _s5_ops(
            prm["bre"], prm["bim"], prm["cre"], prm["cim"], prm["wr"], prm["wi"], reverse=(r == 1),
            name=f"l1_s5_ops{r}", hosted=_Hosted([(half_rows, chips, False)]))
        s5.append(dict(
            prm, t=t_op, bp=bp_op, cp=cp_op, wc_in_half=wc_in_half,
            tab=_scan_tables(pq_r[:, r], pq_i[:, r], ngb, reverse=(r == 1)),
            tab_adj=_scan_tables(pq_r[:, r], -pq_i[:, r], ngb, reverse=(r == 0))))
    wc_in = jnp.concatenate([s5[0]["wc_in_half"], s5[1]["wc_in_half"]], axis=1)

    p0, h0, gathered = _inproj(x2, sc0, sh0, wc_in, tb=min(1024, n_lat), name="l0_inproj", hosted=later_weights)
    wc_out, ws_in, w_glu, ws_out = gathered
    wc_out, w_glu, ws_out = wc_out.reshape(e, d_model), w_glu.reshape(e, e), ws_out.reshape(e, d_model)
    wc_in_t, ws_in_t, wc_out_t = jnp.transpose(wc_in, (0, 2, 1)), jnp.transpose(ws_in, (0, 2, 1)), wc_out.T
    pc0, hc0 = _inproj(ctx2, scc0, shc0, wc_in, tb=tb_tok, name="l0_inproj_ctx")
    xhat0, rstd0, g0, yc0, f0 = _convgate(p0, x2, gt0, conv_w_full, wc_out, *ln0, grid_mode=True, tb=tb_tok, name="l0_conv")
    chat0, crstd0, gc0, ycc0, fc0 = _convgate(pc0, ctx2, gtc0, conv_w_full, wc_out, *ln0, grid_mode=False, tb=tb_tok,
                                              name="l0_conv_ctx")

    seq_rows = [(n_seq, n_ctx), (n_seq, 0)]

    def wide_chunk(w2):
        return jnp.transpose(w2, (1, 0, 2)).reshape(1, d_model, e)

    ws_u, ws_z = wide_chunk(ws_in[0:2]), wide_chunk(ws_in[2:4])
    useq_f, useq_b, h1 = _inproj_seq(xhat0, sc1, sh1, ws_u, ln0, tb=min(1024, n_lat), seq_rows=seq_rows,
                                     name="l1_inproj_u")
    z1, _ = _inproj(xhat0, sc1, sh1, ws_z, lnaff=ln0, tb=min(1024, n_lat), name="l1_inproj_z")
    uc, hc1 = _inproj(chat0, scc1, shc1, ws_u, lnaff=ln0, tb=tb_tok, gb_rows=[(n_ctx, 0)], name="l1_inproj_ctx")
    useq = [useq_f.at[:, 0:n_ctx].set(uc), useq_b.at[:, n_lat:].set(uc)]
    y_dir, hp_dir = [], []
    for r in range(2):
        yr, hcr = _s5_fwd(useq[r], s5[r]["t"], s5[r]["bp"], s5[r]["cp"], s5[r]["tab"], reverse=(r == 1),
                          name=f"l1_s5_fwd{r}")
        y_dir.append(yr)
        hp_dir.append(hcr)

    (loss, dxres, do1, gz1, gg1, dq1, dz1, dy_f, dy_b, dg1, db1, dgt1, dbglu, dd) = _glu_loss(
        useq[0], y_dir[0], y_dir[1], z1, xhat0, ln0, gt1, d_full, w_glu, b_glu_full, ws_out, ln1, tgt2,
        offs=(n_ctx, n_ctx, 0), dy_rows=seq_rows, tb=tb_glu, name="l1_glu_loss")
    no_dy = jnp.zeros((ngb, n_ctx, LANES), BF16)
    dy_dir = [dy_f.at[:, 0:n_ctx].set(no_dy), dy_b.at[:, n_lat:].set(no_dy)]

    tl = min(1024, n_lat)

    def owner_slices(name, full):
        w = weights[name]
        return full.reshape(8, math.prod(w.shape[:-1]) // 2, w.shape[-1])

    def scatter(named):
        return _Hosted([(owner_slices(name, full), everyone, True) for name, full in named])

    def siblings(names):
        return _Hosted([(_sum_parts(rs_parts[name], "sum_" + name), pair, False) for name in names])

    rs_parts, both_halves = {}, {}

    gw_glu = _wgrad(gg1, dq1, n_chunks=1, tm=e // 2, tl=min(2 * tl, n_lat), name="wg_glu")
    gw_ssm_out = _wgrad(gz1, do1, n_chunks=1, tm=e, tl=min(2 * tl, n_lat), name="wg_ssm_out")
    du_dir, s5_grads = [], []
    for r in range(2):
        if r == 0:
            hosted = scatter([("ssm_w_glu", gw_glu), ("ssm_w_out", gw_ssm_out)])
        else:
            hosted = siblings(["ssm_w_glu", "ssm_w_out"])
        dur, dt_op, dbp_op, dcp_op, da8, extra = _s5_bwd(useq[r], dy_dir[r], hp_dir[r], s5[r]["t"], s5[r]["bp"],
                                                         s5[r]["cp"], s5[r]["tab_adj"], reverse=(r == 1),
                                                         name=f"l1_s5_bwd{r}", hosted=hosted)
        if r == 0:
            rs_parts["ssm_w_glu"], rs_parts["ssm_w_out"] = extra
        else:
            both_halves["ssm_w_glu"], both_halves["ssm_w_out"] = extra
        du_dir.append(dur)
        prm = s5[r]
        s5_grads.append(functools.partial(
            _s5_ops_bwd, prm["bre"], prm["bim"], prm["cre"], prm["cim"], prm["wr"], prm["wi"], prm["wr"][:, 1:2],
            prm["wi"][:, 1:2], dt_op, dbp_op, dcp_op, da8, reverse=(r == 1), name=f"l1_s5_ops_bwd{r}"))
    dp1, dr0, df0, dsc1, dsh1, dg0, db0, dgt0 = _ssm_inbwd(
        du_dir[0], du_dir[1], ws_in_t, xhat0, rstd0, ln0, sc1, gt0, f0, lat=(dy_dir[1], dz1, d_full, dxres),
        row_f=n_ctx, row_b=0, tb=tb_glu, name="l1_inbwd")
    dpc1, drc0, dfc0, dscc1, dshc1, dgc0, dbc0, dgtc0 = _ssm_inbwd(
        du_dir[0], du_dir[1], ws_in_t, chat0, crstd0, ln0, scc1, gtc0, fc0, lat=None,
        row_f=0, row_b=n_lat, tb=n_ctx, name="l1_inbwd_ctx")

    def conv_backward(df, p, yc, dr, xin, sc, grid_mode, tag, hosted_a=None, hosted_b=None, hosted_in=None):
        dbg, dz, dyc, extra_a = _conv_bwd_a(df, wc_out_t, p, yc, tb=tb_tok, name="l0_bwd_a" + tag, hosted=hosted_a)
        dp, dcw, extra_b = _conv_bwd_b(dyc, p, dbg, dz, conv_w_full, grid_mode=grid_mode, tb=tb_tok,
                                       name="l0_bwd_b" + tag, hosted=hosted_b)
        gx, dsc, dsh, extra_in = _conv_inbwd(dp, wc_in_t, dr, xin, sc, tb=tb_tok, name="l0_inbwd" + tag,
                                             hosted=None if hosted_in is None else hosted_in(dp, extra_a + extra_b))
        return dp, dcw, gx, dsc, dsh, extra_in

    dpc0, dcwc0, _, dscc0, dshc0, _ = conv_backward(dfc0, pc0, ycc0, drc0, ctx2, scc0, False, "_ctx")
    gw_conv_out = _wgrad(g0, df0, n_chunks=1, tm=e, tl=tl, name="wg_conv_out",
                         init=_wgrad(gc0, dfc0, n_chunks=1, tm=e, tl=tl, name="wg_conv_out_ctx"))
    gw_ssm_in = _wgrad(h1, dp1, n_chunks=4, tm=d_model, tl=tl, name="wg_ssm_in",
                       init=_wgrad(hc1, dpc1, n_chunks=4, tm=d_model, tl=tl, name="wg_ssm_in_ctx"))
    gw_conv_in_ctx = _wgrad(hc0, dpc0, n_chunks=4, tm=d_model, tl=tl, name="wg_conv_in_ctx")

    def behind_inbwd(dp, arrived):
        rs_parts["conv_w_out"], rs_parts["ssm_w_in"] = arrived
        gw_conv_in = _wgrad(h0, dp, n_chunks=4, tm=d_model, tl=tl, name="wg_conv_in", init=gw_conv_in_ctx)
        both = siblings(["ssm_w_in", "conv_w_out"])
        return _Hosted(scatter([("conv_w_in", gw_conv_in)]).items + both.items)

    dp0, dcw0, grad_x, dsc0, dsh0, extra_in = conv_backward(
        df0, p0, yc0, dr0, x2, sc0, True, "", hosted_a=scatter([("conv_w_out", gw_conv_out)]),
        hosted_b=scatter([("ssm_w_in", gw_ssm_in)]), hosted_in=behind_inbwd)
    rs_parts["conv_w_in"], both_halves["ssm_w_in"], both_halves["conv_w_out"] = extra_in
    half_in = _sum_parts(rs_parts["conv_w_in"], "sum_conv_w_in")
    cut = half_in.shape[0] // 2
    *grads_r0, (top,) = s5_grads[0](hosted=_Hosted([(half_in[:cut], pair, False)]))
    *grads_r1, (bottom,) = s5_grads[1](hosted=_Hosted([(half_in[cut:], pair, False)]))
    both_halves["conv_w_in"] = jnp.concatenate([top, bottom], axis=1)
    s5_grads = [grads_r0, grads_r1]

    grads, deltas, new_m, new_v = {}, {}, {}, {}
    for name in ("ssm_w_glu", "ssm_w_out", "ssm_w_in", "conv_w_out", "conv_w_in"):
        w = weights[name]
        rows, cols = math.prod(w.shape[:-1]), w.shape[-1]
        both = both_halves[name].reshape(rows, cols)
        dlt, nm, nv = _adamw(w.reshape(rows, cols), both, mom_m[name].reshape(rows, cols),
                             mom_v[name].reshape(rows, cols), "adamw_" + name)
        grads[name], deltas[name] = both.reshape(w.shape), dlt.reshape(w.shape)
        new_m[name], new_v[name] = nm.reshape(w.shape), nv.reshape(w.shape)

    gpn = (n_grp, grp, n_state)
    small_parts = [
        jnp.concatenate([dg0 + dgc0, dg1], axis=0), jnp.concatenate([db0 + dbc0, db1], axis=0),
        dcw0 + dcwc0, dd, dbglu,
        jnp.stack([s5_grads[r][4] for r in range(2)]),
    ] + [jnp.stack([_block_diag_t(s5_grads[r][k], *gpn) for r in range(2)]) for k in range(4)] + [loss]
    small_shapes = [p.shape for p in small_parts]
    flat = _flat(small_parts)
    quantum = 8 * SUBLANES * LANES
    n_flat = -(-flat.shape[0] // quantum) * quantum
    flat = jnp.pad(flat, (0, n_flat - flat.shape[0])).reshape(8, n_flat // (8 * LANES), LANES)
    red = _sum_parts(_exchange(flat, everyone, True, "rs_small"), "sum_small")
    red = _exchange(red, everyone, False, "ag_small_grads").reshape(-1)
    g_ln_g, g_ln_b, g_conv_w, g_d, g_bglu, g_a, g_bbr, g_bbi, g_cre, g_cim, loss_sum = _unflat(red, small_shapes)

    def groups_minor(t, lead):
        return jnp.moveaxis(t, 1, -1).reshape(lead, n_grp)

    g_a = g_a.reshape(2, ngb, 2, sbk)
    dar = groups_minor(g_a[:, :, 0].reshape(2, n_grp, n_state), 2 * n_state)
    dai = groups_minor(g_a[:, :, 1].reshape(2, n_grp, n_state), 2 * n_state)
    dbbr_t = jnp.transpose(g_bbr, (2, 0, 3, 1)).reshape(grp, 2 * n_state, n_grp)
    dbbi_t = jnp.transpose(g_bbi, (2, 0, 3, 1)).reshape(grp, 2 * n_state, n_grp)
    z_lre, z_lim, z_ls, z_bre, z_bim = _zoh_bwd(lam_re2, lam_im2, log_step2, b_re_t, b_im_t, dar, dai, dbbr_t, dbbi_t)

    zero = jnp.zeros((1, d_model), F32)
    dm_rows = jnp.stack([
        jnp.stack([jnp.concatenate([dsh0, dsc0, dgt0], axis=1), jnp.concatenate([dshc0, dscc0, dgtc0], axis=1)]),
        jnp.stack([jnp.concatenate([dsh1, dsc1, dgt1], axis=1), jnp.concatenate([dshc1, dscc1, zero], axis=1)]),
    ]).reshape(DEPTH, 2, 3 * d_model)
    dm_all = _exchange(dm_rows, everyone, False, "ag_dmod")
    dm_sh = lax.dynamic_slice_in_dim(dm_all, chip * ws, ws, axis=3)
    g_ada_w, g_ada_b, ds_part = _ada_bwd(c_all, cc2, ada_w, dm_all, dm_sh)
    g_cctx = _cctx_grad(_exchange(ds_part, chips, False, "ag_dsctx"), cc2)

    grads["ada_w"] = g_ada_w
    dlt, nm, nv = _adamw(ada_w.reshape(-1, ws), g_ada_w.reshape(-1, ws), m_ada_w.reshape(-1, ws),
                         v_ada_w.reshape(-1, ws), "adamw_ada_w")
    deltas["ada_w"], new_m["ada_w"], new_v["ada_w"] = dlt.reshape(ada_w.shape), nm.reshape(ada_w.shape), nv.reshape(ada_w.shape)

    def chip_cols(full, rows):
        return lax.dynamic_slice_in_dim(full.reshape(rows, e), chip * es, es, axis=1)

    def same(t):
        return t

    def channel_minor_back(t):
        return jnp.transpose(t, (1, 0, 2)).reshape(2 * n_state * grp, n_grp)

    small = dict(
        c_ctx=(g_cctx, lambda t: t.reshape(1, d_model), lambda t: t.reshape(c_ctx.shape)),
        ada_b=(g_ada_b.reshape(ada_b.shape), same, same),
        ln_g=(g_ln_g, same, same), ln_b=(g_ln_b, same, same),
        conv_w=(chip_cols(g_conv_w, 3), lambda t: t[0], lambda t: t.reshape(conv_w.shape)),
        ssm_lam_re=(z_lre, lam_view, lam_back), ssm_lam_im=(z_lim, lam_view, lam_back),
        ssm_log_step=(z_ls, lambda t: t[0], lambda t: t.reshape(ssm_log_step.shape)),
        ssm_b_re=(channel_minor_back(z_bre), b_view, b_back), ssm_b_im=(channel_minor_back(z_bim), b_view, b_back),
        ssm_c_re=(groups_minor(g_cre, 2 * grp * n_state), c_view, c_back),
        ssm_c_im=(groups_minor(g_cim, 2 * grp * n_state), c_view, c_back),
        ssm_d=(chip_cols(g_d, 1), same, same), ssm_b_glu=(chip_cols(g_bglu, 1), same, same))
    for n, (g_view, view, back) in small.items():
        dlt, nm, nv = _adamw(view(weights[n]), g_view, view(mom_m[n]), view(mom_v[n]), "adamw_" + n)
        grads[n], deltas[n], new_m[n], new_v[n] = back(g_view), back(dlt), back(nm), back(nv)

    return (loss_sum.reshape(()), grad_x.reshape(x.shape), *[grads[n] for n in names], *[deltas[n] for n in names],
            *[new_m[n] for n in names], *[new_v[n] for n in names])
```

```python
import functools
import math

import jax
import jax.numpy as jnp
from jax import lax
from jax.experimental import pallas as pl
from jax.experimental.pallas import tpu as pltpu

F32 = jnp.float32
BF16 = jnp.bfloat16
LANES = 128
SUBLANES = 8
VMEM_LIMIT = 56 * 1024 * 1024
MESH_AXES = ("x", "y", "c")
HIGHEST = lax.Precision.HIGHEST

GRID_W = 64
LN_EPS = 1e-5
DEPTH = 2
DN_ALPHA = (2 * DEPTH) ** 0.25
ADAM_LR, ADAM_B1, ADAM_B2, ADAM_EPS, ADAM_WD, ADAM_STEP = 0.001, 0.9, 0.999, 1e-08, 0.01, 10
GELU_K = math.sqrt(2.0 / math.pi)
GELU_C = 0.044715


def _params(n_grid_axes):
    return pltpu.CompilerParams(dimension_semantics=("arbitrary",) * n_grid_axes, vmem_limit_bytes=VMEM_LIMIT)


def _dot(a, b):
    return jnp.dot(a, b, preferred_element_type=F32)


def _dot_nt(a, b):
    return lax.dot_general(a, b, (((1,), (1,)), ((), ())), preferred_element_type=F32)


def _dot_tn(a, b):
    return lax.dot_general(a, b, (((0,), (0,)), ((), ())), preferred_element_type=F32)


def _sigmoid(x):
    return 0.5 * jnp.tanh(0.5 * x) + 0.5


def _colsum(x):
    return jnp.sum(x, axis=0, keepdims=True)


def _rowmean(x):
    return jnp.mean(x, axis=-1, keepdims=True)


def _zero_first(first, *refs):
    @pl.when(first)
    def _():
        for ref in refs:
            ref[...] = jnp.zeros_like(ref)


def _acc(ref, value):
    ref[...] += value


def _exchange_copies(src_ref, out_ref, send_sems, recv_sems, own_sem, axes, all_to_all, sem0=0):
    n_peers = 2 ** len(axes)
    pos = {a: lax.axis_index(a) for a in MESH_AXES}

    def index(p):
        return sum(p[a] * (2 ** (len(axes) - 1 - i)) for i, a in enumerate(axes))

    me = index(pos)
    own = pltpu.make_async_copy(src_ref.at[me] if all_to_all else src_ref, out_ref.at[me], own_sem)
    copies = []
    for k in range(1, n_peers):
        peer = dict(pos)
        for i, a in enumerate(axes):
            if (k >> (len(axes) - 1 - i)) & 1:
                peer[a] = 1 - pos[a]
        copies.append(pltpu.make_async_remote_copy(
            src_ref=src_ref.at[index(peer)] if all_to_all else src_ref,
            dst_ref=out_ref.at[me],
            send_sem=send_sems.at[sem0 + k - 1],
            recv_sem=recv_sems.at[sem0 + k - 1],
            device_id=tuple(peer[a] for a in MESH_AXES),
            device_id_type=pl.DeviceIdType.MESH,
        ))
    return copies, own


def _exchange_shape(src, axes, all_to_all):
    block = tuple(src.shape[1:] if all_to_all else src.shape)
    return jax.ShapeDtypeStruct((2 ** len(axes),) + block, src.dtype)


def _exchange(src, axes, all_to_all, name):
    n_peers = 2 ** len(axes)

    def body(src_ref, out_ref, send_sems, recv_sems, own_sem):
        copies, own = _exchange_copies(src_ref, out_ref, send_sems, recv_sems, own_sem, axes, all_to_all)
        own.start()
        for cp in copies:
            cp.start()
        for cp in copies:
            cp.wait()
        own.wait()

    return pl.pallas_call(
        body,
        name=name,
        out_shape=_exchange_shape(src, axes, all_to_all),
        in_specs=[pl.BlockSpec(memory_space=pltpu.HBM)],
        out_specs=pl.BlockSpec(memory_space=pltpu.HBM),
        scratch_shapes=[
            pltpu.SemaphoreType.DMA((n_peers - 1,)),
            pltpu.SemaphoreType.DMA((n_peers - 1,)),
            pltpu.SemaphoreType.DMA,
        ],
    )(src)


class _Hosted:
    def __init__(self, items):
        self.items = items
        self.args = [src for src, _, _ in items]
        self.in_specs = [pl.BlockSpec(memory_space=pltpu.HBM)] * len(items)
        self.out_specs = [pl.BlockSpec(memory_space=pltpu.HBM)] * len(items)
        self.out_shapes = [_exchange_shape(*item) for item in items]
        n_remote = sum(2 ** len(axes) - 1 for _, axes, _ in items)
        self.scratch = [pltpu.SemaphoreType.DMA((n_remote,)), pltpu.SemaphoreType.DMA((n_remote,)),
                        pltpu.SemaphoreType.DMA((len(items),))]

    def _copies(self, src_refs, out_refs, send_sems, recv_sems, own_sems):
        out, sem0 = [], 0
        for n, (_, axes, all_to_all) in enumerate(self.items):
            copies, own = _exchange_copies(src_refs[n], out_refs[n], send_sems, recv_sems, own_sems.at[n], axes,
                                           all_to_all, sem0)
            out += [own] + copies
            sem0 += len(copies)
        return out

    def start(self, *refs):
        for cp in self._copies(*refs):
            cp.start()

    def wait(self, *refs):
        for cp in self._copies(*refs):
            cp.wait()


def _call(body, *, name, grid, in_specs, out_specs, out_shape, scratch_shapes, args, hosted=None):
    params = _params(len(grid))
    if hosted is None:
        outs = pl.pallas_call(body, name=name, grid=grid, in_specs=in_specs, out_specs=tuple(out_specs),
                              out_shape=tuple(out_shape), scratch_shapes=list(scratch_shapes), compiler_params=params)(*args)
        return list(outs), []
    n_in, n_out, n_scr, n_h = len(in_specs), len(out_shape), len(scratch_shapes), len(hosted.items)

    def wrapped(*refs):
        ins, h_in = refs[:n_in], refs[n_in:n_in + n_h]
        outs, h_out = refs[n_in + n_h:n_in + n_h + n_out], refs[n_in + n_h + n_out:n_in + 2 * n_h + n_out]
        scr = refs[n_in + 2 * n_h + n_out:]
        first = functools.reduce(jnp.logical_and, [pl.program_id(k) == 0 for k in range(len(grid))])
        last = functools.reduce(jnp.logical_and, [pl.program_id(k) == grid[k] - 1 for k in range(len(grid))])

        @pl.when(first)
        def _():
            hosted.start(h_in, h_out, *scr[n_scr:])

        body(*ins, *outs, *scr[:n_scr])

        @pl.when(last)
        def _():
            hosted.wait(h_in, h_out, *scr[n_scr:])

    outs = pl.pallas_call(
        wrapped, name=name, grid=grid, in_specs=[*in_specs, *hosted.in_specs],
        out_specs=(*out_specs, *hosted.out_specs), out_shape=(*out_shape, *hosted.out_shapes),
        scratch_shapes=[*scratch_shapes, *hosted.scratch], compiler_params=params)(*args, *hosted.args)
    return list(outs[:n_out]), list(outs[n_out:])


def _sum_parts(parts, name):
    n_parts, rows, cols = parts.shape
    tr = rows
    while n_parts * tr * cols * 4 > 8 * 1024 * 1024 and tr % 16 == 0:
        tr //= 2

    def body(p_ref, o_ref):
        total = p_ref[0]
        for k in range(1, n_parts):
            total = total + p_ref[k]
        o_ref[...] = total

    return pl.pallas_call(
        body,
        name=name,
        grid=(rows // tr,),
        out_shape=jax.ShapeDtypeStruct((rows, cols), F32),
        in_specs=[pl.BlockSpec((n_parts, tr, cols), lambda i: (0, i, 0))],
        out_specs=pl.BlockSpec((tr, cols), lambda i: (i, 0)),
        compiler_params=_params(1),
    )(parts)


def _adamw(w, g, m, v, name):
    rows, cols = w.shape
    tr = rows
    while tr * cols * 4 > 2 * 1024 * 1024 and tr % 16 == 0:
        tr //= 2

    def body(w_ref, g_ref, m_ref, v_ref, d_ref, nm_ref, nv_ref):
        gv = g_ref[...]
        nm = ADAM_B1 * m_ref[...] + (1.0 - ADAM_B1) * gv
        nv = ADAM_B2 * v_ref[...] + (1.0 - ADAM_B2) * (gv * gv)
        m_hat = nm / (1.0 - ADAM_B1 ** ADAM_STEP)
        v_hat = nv / (1.0 - ADAM_B2 ** ADAM_STEP)
        d_ref[...] = -ADAM_LR * (m_hat / (jnp.sqrt(v_hat) + ADAM_EPS) + ADAM_WD * w_ref[...])
        nm_ref[...] = nm
        nv_ref[...] = nv

    spec = pl.BlockSpec((tr, cols), lambda i: (i, 0))
    shape = jax.ShapeDtypeStruct((rows, cols), F32)
    return pl.pallas_call(
        body, name=name, grid=(rows // tr,), out_shape=(shape, shape, shape),
        in_specs=[spec] * 4, out_specs=(spec, spec, spec), compiler_params=_params(1),
    )(w, g, m, v)


def _ada_rows(c_ref, cc_ref):
    rows = jnp.concatenate([c_ref[...], jnp.broadcast_to(cc_ref[...], c_ref.shape)], axis=0)
    return rows


def _ada_fwd(c_all, c_ctx, w_sh, b_sh):
    n_layers, _, ws = w_sh.shape

    def body(c_ref, cc_ref, w_ref, b_ref, o_ref):
        rows = _ada_rows(c_ref, cc_ref)
        s = rows * _sigmoid(rows)
        for i in range(n_layers):
            o_ref[i] = jnp.dot(s, w_ref[i], precision=HIGHEST, preferred_element_type=F32) + b_ref[i]

    return pl.pallas_call(
        body, name="ada_fwd", out_shape=jax.ShapeDtypeStruct((n_layers, 16, ws), F32),
        compiler_params=pltpu.CompilerParams(vmem_limit_bytes=VMEM_LIMIT),
    )(c_all, c_ctx, w_sh, b_sh)


def _ada_bwd(c_all, c_ctx, w_sh, dm_full, dm_sh):
    n_layers, d_model, ws = w_sh.shape
    n_dev = dm_full.shape[0]
    cols = dm_full.shape[-1]

    def body(c_ref, cc_ref, w_ref, dmf_ref, dms_ref, gw_ref, gb_ref, ds_ref):
        rows = _ada_rows(c_ref, cc_ref)
        s = rows * _sigmoid(rows)
        ds = jnp.zeros((8, d_model), F32)
        for i in range(n_layers):
            ctx_s = dms_ref[0, i, 1:2, :]
            ctx_f = dmf_ref[0, i, 1:2, :]
            ex_f = dmf_ref[0, i, 0:1, :]
            for k in range(1, n_dev):
                ctx_s = ctx_s + dms_ref[k, i, 1:2, :]
                ctx_f = ctx_f + dmf_ref[k, i, 1:2, :]
                ex_f = ex_f + dmf_ref[k, i, 0:1, :]
            gb_ref[i] = ex_f + ctx_f
            r = jnp.concatenate([dms_ref[k, i, 0:1, :] for k in range(n_dev)] + [ctx_s, jnp.zeros((7, ws), F32)], axis=0)
            gw_ref[i] = lax.dot_general(s, r, (((0,), (0,)), ((), ())), precision=HIGHEST, preferred_element_type=F32)
            ds = ds + lax.dot_general(jnp.broadcast_to(ctx_s, (8, ws)), w_ref[i], (((1,), (1,)), ((), ())),
                                      precision=HIGHEST, preferred_element_type=F32)
        ds_ref[...] = ds

    return pl.pallas_call(
        body, name="ada_bwd",
        out_shape=(jax.ShapeDtypeStruct((n_layers, d_model, ws), F32), jax.ShapeDtypeStruct((n_layers, 1, cols), F32),
                   jax.ShapeDtypeStruct((8, d_model), F32)),
        compiler_params=pltpu.CompilerParams(vmem_limit_bytes=VMEM_LIMIT),
    )(c_all, c_ctx, w_sh, dm_full, dm_sh)


def _cctx_grad(ds_parts, c_ctx):
    def body(p_ref, c_ref, o_ref):
        tot = p_ref[0, 0:1, :]
        for k in range(1, ds_parts.shape[0]):
            tot = tot + p_ref[k, 0:1, :]
        cv = c_ref[...]
        sg = _sigmoid(cv)
        o_ref[...] = tot * (sg * (1.0 + cv * (1.0 - sg)))

    return pl.pallas_call(body, name="cctx_grad", out_shape=jax.ShapeDtypeStruct(c_ctx.shape, F32))(ds_parts, c_ctx)


def _zoh_math(lam_re, lam_im, log_step, b_re, b_im):
    n_state = lam_re.shape[0] // 2
    dt = jnp.exp(jnp.concatenate([jnp.broadcast_to(log_step[r:r + 1], (n_state, log_step.shape[1])) for r in range(2)],
                                 axis=0))
    mag = jnp.exp(lam_re * dt)
    ar = mag * jnp.cos(lam_im * dt)
    ai = mag * jnp.sin(lam_im * dt)
    qr, qi = ar - 1.0, ai
    den = lam_re * lam_re + lam_im * lam_im
    fr = (qr * lam_re + qi * lam_im) / den
    fi = (qi * lam_re - qr * lam_im) / den
    bbr = fr[None] * b_re - fi[None] * b_im
    bbi = fr[None] * b_im + fi[None] * b_re
    return ar, ai, bbr, bbi


def _zoh_fwd(lam_re, lam_im, log_step, b_re, b_im):
    rg, n = lam_re.shape

    def body(lr_ref, li_ref, ls_ref, br_ref, bi_ref, pr_ref, pi_ref, qr_ref, qi_ref, bbr_ref, bbi_ref):
        ar, ai, bbr, bbi = _zoh_math(lr_ref[...], li_ref[...], ls_ref[...], br_ref[...], bi_ref[...])
        bbr_ref[...] = bbr
        bbi_ref[...] = bbi

        def powers(base_r, base_i, r_ref, i_ref):
            pr, pi_ = base_r, base_i
            for k in range(8):
                r_ref[k] = pr
                i_ref[k] = pi_
                pr, pi_ = pr * base_r - pi_ * base_i, pr * base_i + pi_ * base_r

        powers(ar, ai, pr_ref, pi_ref)
        powers(pr_ref[7], pi_ref[7], qr_ref, qi_ref)

    pw = jax.ShapeDtypeStruct((8, rg, n), F32)
    bb = jax.ShapeDtypeStruct(b_re.shape, F32)
    return pl.pallas_call(body, name="zoh_fwd", out_shape=(pw, pw, pw, pw, bb, bb))(lam_re, lam_im, log_step, b_re, b_im)


def _zoh_bwd(lam_re, lam_im, log_step, b_re, b_im, dar, dai, dbbr, dbbi):
    def body(lr_ref, li_ref, ls_ref, br_ref, bi_ref, dar_ref, dai_ref, dbr_ref, dbi_ref, *outs):
        _, vjp = jax.vjp(_zoh_math, lr_ref[...], li_ref[...], ls_ref[...], br_ref[...], bi_ref[...])
        grads = vjp((dar_ref[...], dai_ref[...], dbr_ref[...], dbi_ref[...]))
        for o_ref, gval in zip(outs, grads):
            o_ref[...] = gval

    shapes = tuple(jax.ShapeDtypeStruct(a.shape, F32) for a in (lam_re, lam_im, log_step, b_re, b_im))
    return pl.pallas_call(body, name="zoh_bwd", out_shape=shapes)(lam_re, lam_im, log_step, b_re, b_im, dar, dai, dbbr, dbbi)


def _inproj(xin, sc, sh, w, *, lnaff=None, tb, gb_rows=None, name, hosted=None):
    n_tok, d_model = xin.shape
    n_chunks, _, cw = w.shape
    tb = min(tb, n_tok)
    nq = cw // LANES
    has_ln = lnaff is not None
    n_out = 1 if gb_rows is None else len(gb_rows)

    def body(*refs):
        if has_ln:
            x_ref, g_ref, b_ref, sc_ref, sh_ref, w_ref = refs[:6]
        else:
            x_ref, sc_ref, sh_ref, w_ref = refs[:4]
        p_refs, h_ref = refs[-1 - n_out:-1], refs[-1]

        @pl.when(pl.program_id(1) == 0)
        def _():
            xv = x_ref[...]
            if has_ln:
                xv = xv * g_ref[...] + b_ref[...]
            h_ref[...] = (xv * (1.0 + sc_ref[...]) + sh_ref[...]).astype(BF16)

        acc = _dot(h_ref[...], w_ref[0]).astype(BF16)
        if gb_rows is None:
            p_refs[0][0] = acc
        else:
            for p_ref in p_refs:
                for q in range(nq):
                    p_ref[q] = acc[:, q * LANES:(q + 1) * LANES]

    vec = pl.BlockSpec((1, d_model), lambda i, j: (0, 0))
    in_specs = [pl.BlockSpec((tb, d_model), lambda i, j: (i, 0))] + ([vec, vec] if has_ln else []) + [
        vec, vec, pl.BlockSpec((1, d_model, cw), lambda i, j: (j, 0, 0))]
    if gb_rows is None:
        p_shapes = [jax.ShapeDtypeStruct((n_chunks, n_tok, cw), BF16)]
        p_specs = [pl.BlockSpec((1, tb, cw), lambda i, j: (j, i, 0))]
    else:
        p_shapes, p_specs = [], []
        for total, off in gb_rows:
            assert off % tb == 0
            p_shapes.append(jax.ShapeDtypeStruct((n_chunks * nq, total, LANES), BF16))
            p_specs.append(pl.BlockSpec((nq, tb, LANES), functools.partial(lambda i, j, ob: (j, i + ob, 0), ob=off // tb)))
    args = (xin,) + (tuple(lnaff) if has_ln else ()) + (sc, sh, w)
    outs, extra = _call(
        body, name=name, grid=(n_tok // tb, n_chunks), in_specs=in_specs,
        out_specs=[*p_specs, pl.BlockSpec((tb, d_model), lambda i, j: (i, 0))],
        out_shape=[*p_shapes, jax.ShapeDtypeStruct((n_tok, d_model), BF16)], scratch_shapes=[], args=args, hosted=hosted)
    return (*outs, extra) if hosted is not None else tuple(outs)


def _inproj_seq(xin, sc, sh, w, lnaff, *, tb, seq_rows, name):
    n_tok, d_model = xin.shape
    n_chunks, _, cw = w.shape
    tb = min(tb, n_tok)
    nq = cw // LANES
    n_out = len(seq_rows)
    steps = (n_tok // tb) * n_chunks

    def body(x_ref, g_ref, b_ref, sc_ref, sh_ref, w_ref, *rest):
        p_refs, h_ref, stage, sems = rest[:n_out], rest[n_out], rest[n_out + 1], rest[n_out + 2]
        i, j = pl.program_id(0), pl.program_id(1)
        step = i * n_chunks + j
        slot = step % 2

        def copies(from_slot):
            return [pltpu.make_async_copy(stage.at[from_slot],
                                          p_ref.at[pl.ds(j * nq, nq), pl.ds(off + i * tb, tb), :], sems.at[from_slot, k])
                    for k, (p_ref, (_, off)) in enumerate(zip(p_refs, seq_rows))]

        @pl.when(step >= 2)
        def _():
            for cp in copies(slot):
                cp.wait()

        @pl.when(j == 0)
        def _():
            xv = x_ref[...] * g_ref[...] + b_ref[...]
            h_ref[...] = (xv * (1.0 + sc_ref[...]) + sh_ref[...]).astype(BF16)

        acc = _dot(h_ref[...], w_ref[0]).astype(BF16)
        for q in range(nq):
            stage[slot, q] = acc[:, q * LANES:(q + 1) * LANES]
        for cp in copies(slot):
            cp.start()

        @pl.when(step == steps - 1)
        def _():
            for cp in copies(slot):
                cp.wait()
            if steps > 1:
                for cp in copies(1 - slot):
                    cp.wait()

    vec = pl.BlockSpec((1, d_model), lambda i, j: (0, 0))
    tok = pl.BlockSpec((tb, d_model), lambda i, j: (i, 0))
    return pl.pallas_call(
        body, name=name, grid=(n_tok // tb, n_chunks),
        in_specs=[tok, vec, vec, vec, vec, pl.BlockSpec((1, d_model, cw), lambda i, j: (j, 0, 0))],
        out_specs=(*[pl.BlockSpec(memory_space=pltpu.HBM)] * n_out, tok),
        out_shape=(*[jax.ShapeDtypeStruct((n_chunks * nq, total, LANES), BF16) for total, _ in seq_rows],
                   jax.ShapeDtypeStruct((n_tok, d_model), BF16)),
        scratch_shapes=[pltpu.VMEM((2, nq, tb, LANES), BF16), pltpu.SemaphoreType.DMA((2, n_out))],
        compiler_params=_params(2),
    )(xin, lnaff[0], lnaff[1], sc, sh, w)


def _shifted(u, rows, width, tb):
    col = rows % width
    um = jnp.where(col == 0, 0.0, pltpu.roll(u, 1, 0))
    up = jnp.where(col == width - 1, 0.0, pltpu.roll(u, tb - 1, 0))
    return um, up


def _slab_width(e):
    return min(512, e // 2)


def _convgate(p, x, gt, conv_w, w_out, ln_g, ln_b, *, grid_mode, tb, name):
    _, n_tok, e = p.shape
    d_model = x.shape[1]
    eh = e // 2
    if not grid_mode:
        tb = n_tok
    tb = min(tb, n_tok)
    nb = n_tok // tb
    hb = tb // GRID_W
    cs = _slab_width(e)

    def body(*refs):
        if grid_mode:
            (bg_ref, cg_ref, v_ref, z_ref, cgp_ref, vp_ref, cgn_ref, vn_ref, x_ref, gt_ref, cw_ref, wo_ref, lg_ref,
             lb_ref, xh_ref, rs_ref, g_ref, yc_ref, f_ref) = refs
        else:
            (bg_ref, cg_ref, v_ref, z_ref, x_ref, gt_ref, cw_ref, wo_ref, lg_ref, lb_ref, xh_ref, rs_ref, g_ref,
             yc_ref, f_ref) = refs
        i = pl.program_id(0)
        rows = lax.broadcasted_iota(jnp.int32, (tb, 1), 0)
        for c0 in range(0, e, cs):
            sl = slice(c0, c0 + cs)
            u = cg_ref[0, :, sl].astype(F32) * v_ref[0, :, sl].astype(F32)
            w = cw_ref[:, sl]
            if grid_mode and c0 >= eh:
                hs = slice(c0 - eh, c0 - eh + cs)
                uprev = cgp_ref[0, :, hs].astype(F32) * vp_ref[0, :, hs].astype(F32)
                unext = cgn_ref[0, :, hs].astype(F32) * vn_ref[0, :, hs].astype(F32)
                uprev = jnp.where(i > 0, uprev, 0.0)
                unext = jnp.where(i < nb - 1, unext, 0.0)
                if tb > GRID_W:
                    um = jnp.concatenate([uprev, u[:tb - GRID_W]], axis=0)
                    up = jnp.concatenate([u[GRID_W:], unext], axis=0)
                else:
                    um, up = uprev, unext
            else:
                um, up = _shifted(u, rows, GRID_W if grid_mode else tb, tb)
            yc = um * w[0:1] + u * w[1:2] + up * w[2:3]
            zf = z_ref[0, :, sl].astype(F32)
            gval = bg_ref[0, :, sl].astype(F32) * yc * (zf * _sigmoid(zf))
            yc_ref[:, sl] = yc.astype(BF16)
            g_ref[:, sl] = gval.astype(BF16)
        f = _dot(g_ref[...], wo_ref[...])
        f_ref[...] = f.astype(BF16)
        r = DN_ALPHA * x_ref[...] + gt_ref[...] * f
        rc = r - _rowmean(r)
        rstd = lax.rsqrt(_rowmean(rc * rc) + LN_EPS)
        xh_ref[...] = rc * rstd
        rs_ref[...] = rstd

    def chunk(k):
        return pl.BlockSpec((1, tb, e), lambda i: (k, i, 0))

    n_hrows = n_tok // GRID_W

    def halo_prev(k):
        return pl.BlockSpec((1, GRID_W, eh), lambda i: (k, jnp.maximum(i * hb - 1, 0), 1))

    def halo_next(k):
        return pl.BlockSpec((1, GRID_W, eh), lambda i: (k, jnp.minimum((i + 1) * hb, n_hrows - 1), 1))

    vec = pl.BlockSpec((1, d_model), lambda i: (0, 0))
    tok = pl.BlockSpec((tb, d_model), lambda i: (i, 0))
    wide = pl.BlockSpec((tb, e), lambda i: (i, 0))
    in_specs = [chunk(0), chunk(1), chunk(2), chunk(3)]
    args = [p, p, p, p]
    if grid_mode:
        in_specs += [halo_prev(1), halo_prev(2), halo_next(1), halo_next(2)]
        args += [p, p, p, p]
    in_specs += [tok, vec, pl.BlockSpec((3, e), lambda i: (0, 0)), pl.BlockSpec((e, d_model), lambda i: (0, 0)), vec, vec]
    args += [x, gt, conv_w, w_out, ln_g, ln_b]
    return pl.pallas_call(
        body, name=name, grid=(nb,),
        out_shape=(jax.ShapeDtypeStruct((n_tok, d_model), F32), jax.ShapeDtypeStruct((n_tok, 1), F32),
                   jax.ShapeDtypeStruct((n_tok, e), BF16), jax.ShapeDtypeStruct((n_tok, e), BF16),
                   jax.ShapeDtypeStruct((n_tok, d_model), BF16)),
        in_specs=in_specs, out_specs=(tok, pl.BlockSpec((tb, 1), lambda i: (i, 0)), wide, wide, tok),
        compiler_params=_params(1),
    )(*args)


def _scan_block(buf_ref, tab_ref, cr, ci, *, reverse, tb, sb):
    n_slabs = tb // SUBLANES
    unrolled = n_slabs <= 64

    def slab(s, carry):
        cr, ci = carry
        idx = (n_slabs - 1 - s) if reverse else s
        r0 = idx * SUBLANES if unrolled else pl.multiple_of(idx * SUBLANES, SUBLANES)
        xr = buf_ref[pl.ds(r0, SUBLANES), 0:sb]
        xi = buf_ref[pl.ds(r0, SUBLANES), sb:2 * sb]
        for k, step in enumerate((1, 2, 4)):
            ar = tab_ref[2 * k]
            ai = tab_ref[2 * k + 1]
            shift = (SUBLANES - step) if reverse else step
            rr = pltpu.roll(xr, shift, 0)
            ri = pltpu.roll(xi, shift, 0)
            xr, xi = xr + ar * rr - ai * ri, xi + ar * ri + ai * rr
        pr = tab_ref[6]
        pi_ = tab_ref[7]
        xr, xi = xr + pr * cr - pi_ * ci, xi + pr * ci + pi_ * cr
        buf_ref[pl.ds(r0, SUBLANES), 0:sb] = xr
        buf_ref[pl.ds(r0, SUBLANES), sb:2 * sb] = xi
        last = 0 if reverse else SUBLANES - 1
        return (jnp.broadcast_to(xr[last:last + 1, :], (SUBLANES, sb)),
                jnp.broadcast_to(xi[last:last + 1, :], (SUBLANES, sb)))

    if unrolled:
        carry = (cr, ci)
        for s in range(n_slabs):
            carry = slab(s, carry)
        return carry
    return lax.fori_loop(0, n_slabs, slab, (cr, ci))


CHUNK = SUBLANES


def _group_mask():
    r = lax.broadcasted_iota(jnp.int32, (LANES, LANES), 0)
    c = lax.broadcasted_iota(jnp.int32, (LANES, LANES), 1)
    return r // 16 == c // 16


def _s5_ops(bre, bim, cre, cim, wr, wi, *, reverse, name, hosted=None):
    ngb, _, sb = bre.shape
    n_rows = CHUNK * LANES

    def body(bre_ref, bim_ref, cre_ref, cim_ref, wr_ref, wi_ref, t_ref, bp_ref, cp_ref):
        b_re, b_im, c_re, c_im = bre_ref[0], bim_ref[0], cre_ref[0], cim_ref[0]
        mask = _group_mask()
        er, ei = [], []
        for tau in range(CHUNK + 1):
            w_r, w_i = wr_ref[0, tau:tau + 1, :], wi_ref[0, tau:tau + 1, :]
            er.append(c_re * w_r - c_im * w_i)
            ei.append(c_re * w_i + c_im * w_r)
        kt = []
        for tau in range(CHUNK):
            k = (lax.dot_general(b_re, er[tau], (((1,), (1,)), ((), ())), precision=HIGHEST, preferred_element_type=F32)
                 - lax.dot_general(b_im, ei[tau], (((1,), (1,)), ((), ())), precision=HIGHEST, preferred_element_type=F32))
            kt.append(jnp.where(mask, k, 0.0).astype(BF16))
        zero = jnp.zeros((LANES, LANES), BF16)
        for i in range(CHUNK):
            rows = slice(i * LANES, (i + 1) * LANES)
            for j in range(CHUNK):
                lag = (i - j) if reverse else (j - i)
                t_ref[0, rows, j * LANES:(j + 1) * LANES] = kt[lag] if lag >= 0 else zero
            tau = i if reverse else CHUNK - 1 - i
            w_r, w_i = wr_ref[0, tau:tau + 1, :], wi_ref[0, tau:tau + 1, :]
            bp_ref[0, rows, 0:sb] = (b_re * w_r - b_im * w_i).astype(BF16)
            bp_ref[0, rows, sb:2 * sb] = (b_re * w_i + b_im * w_r).astype(BF16)
            tau = CHUNK - i if reverse else i + 1
            cp_ref[0, rows, 0:sb] = er[tau].astype(BF16)
            cp_ref[0, rows, sb:2 * sb] = (-ei[tau]).astype(BF16)

    mat = pl.BlockSpec((1, LANES, sb), lambda g: (g, 0, 0))
    pw = pl.BlockSpec((1, CHUNK + 1, sb), lambda g: (g, 0, 0))
    outs, extra = _call(
        body, name=name, grid=(ngb,),
        out_shape=[jax.ShapeDtypeStruct((ngb, n_rows, n_rows), BF16), jax.ShapeDtypeStruct((ngb, n_rows, 2 * sb), BF16),
                   jax.ShapeDtypeStruct((ngb, n_rows, 2 * sb), BF16)],
        in_specs=[mat, mat, mat, mat, pw, pw],
        out_specs=[pl.BlockSpec((1, n_rows, n_rows), lambda g: (g, 0, 0)),
                   pl.BlockSpec((1, n_rows, 2 * sb), lambda g: (g, 0, 0)),
                   pl.BlockSpec((1, n_rows, 2 * sb), lambda g: (g, 0, 0))],
        scratch_shapes=[], args=(bre, bim, cre, cim, wr, wi), hosted=hosted)
    return (*outs, extra)


def _s5_ops_bwd(bre, bim, cre, cim, wr, wi, ar, ai, dt, dbp, dcp, da8, *, reverse, name, hosted=None):
    ngb, _, sb = bre.shape
    n_rows = CHUNK * LANES

    def dot_hi(a, b, dims):
        return lax.dot_general(a.astype(BF16), b.astype(BF16), (dims, ((), ())), preferred_element_type=F32)

    def body(bre_ref, bim_ref, cre_ref, cim_ref, wr_ref, wi_ref, ar_ref, ai_ref, dt_ref, dbp_ref, dcp_ref, da8_ref,
             dbre_ref, dbim_ref, dcre_ref, dcim_ref, da_ref):
        b_re, b_im, c_re, c_im = bre_ref[0], bim_ref[0], cre_ref[0], cim_ref[0]
        mask = _group_mask()
        w_r = [wr_ref[0, tau:tau + 1, :] for tau in range(CHUNK + 1)]
        w_i = [wi_ref[0, tau:tau + 1, :] for tau in range(CHUNK + 1)]
        der = [jnp.zeros((LANES, sb), F32) for _ in range(CHUNK + 1)]
        dei = [jnp.zeros((LANES, sb), F32) for _ in range(CHUNK + 1)]
        dwr = [jnp.zeros((1, sb), F32) for _ in range(CHUNK + 1)]
        dwi = [jnp.zeros((1, sb), F32) for _ in range(CHUNK + 1)]
        dwr[CHUNK] = da8_ref[0, :, 0:sb]
        dwi[CHUNK] = da8_ref[0, :, sb:2 * sb]
        d_bre = jnp.zeros((LANES, sb), F32)
        d_bim = jnp.zeros((LANES, sb), F32)
        dkt = [jnp.zeros((LANES, LANES), F32) for _ in range(CHUNK)]
        for i in range(CHUNK):
            rows = slice(i * LANES, (i + 1) * LANES)
            for j in range(CHUNK):
                lag = (i - j) if reverse else (j - i)
                if lag >= 0:
                    dkt[lag] = dkt[lag] + dt_ref[0, rows, j * LANES:(j + 1) * LANES]
            tau = i if reverse else CHUNK - 1 - i
            g_r, g_i = dbp_ref[0, rows, 0:sb], dbp_ref[0, rows, sb:2 * sb]
            d_bre = d_bre + g_r * w_r[tau] + g_i * w_i[tau]
            d_bim = d_bim - g_r * w_i[tau] + g_i * w_r[tau]
            dwr[tau] = dwr[tau] + _colsum(g_r * b_re + g_i * b_im)
            dwi[tau] = dwi[tau] + _colsum(g_i * b_re - g_r * b_im)
            tau = CHUNK - i if reverse else i + 1
            der[tau] = der[tau] + dcp_ref[0, rows, 0:sb]
            dei[tau] = dei[tau] - dcp_ref[0, rows, sb:2 * sb]
        d_cre = jnp.zeros((LANES, sb), F32)
        d_cim = jnp.zeros((LANES, sb), F32)
        for tau in range(CHUNK + 1):
            if tau < CHUNK:
                e_r = c_re * w_r[tau] - c_im * w_i[tau]
                e_i = c_re * w_i[tau] + c_im * w_r[tau]
                dk = jnp.where(mask, dkt[tau], 0.0)
                d_bre = d_bre + dot_hi(dk, e_r, ((1,), (0,)))
                d_bim = d_bim - dot_hi(dk, e_i, ((1,), (0,)))
                der[tau] = der[tau] + dot_hi(dk, b_re, ((0,), (0,)))
                dei[tau] = dei[tau] - dot_hi(dk, b_im, ((0,), (0,)))
            d_cre = d_cre + der[tau] * w_r[tau] + dei[tau] * w_i[tau]
            d_cim = d_cim - der[tau] * w_i[tau] + dei[tau] * w_r[tau]
            dwr[tau] = dwr[tau] + _colsum(der[tau] * c_re + dei[tau] * c_im)
            dwi[tau] = dwi[tau] + _colsum(dei[tau] * c_re - der[tau] * c_im)
        a_r, a_i = ar_ref[0], ai_ref[0]
        d_ar = jnp.zeros((1, sb), F32)
        d_ai = jnp.zeros((1, sb), F32)
        for tau in range(CHUNK, 0, -1):
            d_ar = d_ar + dwr[tau] * w_r[tau - 1] + dwi[tau] * w_i[tau - 1]
            d_ai = d_ai - dwr[tau] * w_i[tau - 1] + dwi[tau] * w_r[tau - 1]
            dwr[tau - 1], dwi[tau - 1] = (dwr[tau - 1] + dwr[tau] * a_r + dwi[tau] * a_i,
                                          dwi[tau - 1] - dwr[tau] * a_i + dwi[tau] * a_r)
        dbre_ref[0] = d_bre
        dbim_ref[0] = d_bim
        dcre_ref[0] = d_cre
        dcim_ref[0] = d_cim
        da_ref[0, :, 0:sb] = d_ar
        da_ref[0, :, sb:2 * sb] = d_ai

    mat = pl.BlockSpec((1, LANES, sb), lambda g: (g, 0, 0))
    pw = pl.BlockSpec((1, CHUNK + 1, sb), lambda g: (g, 0, 0))
    one = pl.BlockSpec((1, 1, sb), lambda g: (g, 0, 0))
    two = pl.BlockSpec((1, 1, 2 * sb), lambda g: (g, 0, 0))
    big = pl.BlockSpec((1, n_rows, n_rows), lambda g: (g, 0, 0))
    big2 = pl.BlockSpec((1, n_rows, 2 * sb), lambda g: (g, 0, 0))
    mshape = jax.ShapeDtypeStruct((ngb, LANES, sb), F32)
    outs, extra = _call(
        body, name=name, grid=(ngb,),
        out_shape=[mshape, mshape, mshape, mshape, jax.ShapeDtypeStruct((ngb, 1, 2 * sb), F32)],
        in_specs=[mat, mat, mat, mat, pw, pw, one, one, big, big2, big2, two],
        out_specs=[mat, mat, mat, mat, two],
        scratch_shapes=[], args=(bre, bim, cre, cim, wr, wi, ar, ai, dt, dbp, dcp, da8), hosted=hosted)
    return (*outs, extra)


MXU_TILE = 256


def _causal_span(tile, n_tiles, reverse, of_output):
    upto, onward = slice(0, (tile + 1) * MXU_TILE), slice(tile * MXU_TILE, n_tiles * MXU_TILE)
    return (onward if reverse else upto) if of_output else (upto if reverse else onward)


def _apply_causal(uv, t_ref, reverse):
    n_tiles = uv.shape[1] // MXU_TILE
    cols = []
    for tj in range(n_tiles):
        span = _causal_span(tj, n_tiles, reverse, True)
        cols.append(_dot(uv[:, span], t_ref[0, span, tj * MXU_TILE:(tj + 1) * MXU_TILE]))
    return jnp.concatenate(cols, axis=1)


def _apply_causal_t(dyv, t_ref, reverse):
    n_tiles = dyv.shape[1] // MXU_TILE
    cols = []
    for ti in range(n_tiles):
        span = _causal_span(ti, n_tiles, reverse, False)
        cols.append(_dot_nt(dyv[:, span], t_ref[0, ti * MXU_TILE:(ti + 1) * MXU_TILE, span]))
    return jnp.concatenate(cols, axis=1)


def _shift_rows(xv, edge, rows, n_rows, down):
    if down:
        return jnp.where(rows == 0, edge, pltpu.roll(xv, 1, 0))
    return jnp.where(rows == n_rows - 1, edge, pltpu.roll(xv, n_rows - 1, 0))


def _rows_of_tokens(tok_ref, conv_scr, rb):
    conv_scr[...] = tok_ref[0].astype(F32)
    return jnp.concatenate([conv_scr[pl.ds(j, rb, stride=CHUNK), :] for j in range(CHUNK)], axis=1).astype(BF16)


def _tokens_of_rows(val, tok_ref, conv_scr, rb):
    for j in range(CHUNK):
        conv_scr[pl.ds(j, rb, stride=CHUNK), :] = val[:, j * LANES:(j + 1) * LANES]
    tok_ref[0] = conv_scr[...].astype(BF16)


def _s5_row_block(n_seq, target=416):
    n_rows = n_seq // CHUNK
    best = 16
    for rb in range(16, min(target, n_rows) + 1, 16):
        if n_rows % rb == 0:
            best = rb
    assert n_rows % best == 0
    return best


def _s5_fwd(useq, t_op, bp, cp, tab, *, reverse, name):
    ngb, n_seq, _ = useq.shape
    sb = bp.shape[2] // 2
    width = CHUNK * LANES
    rb = _s5_row_block(n_seq)
    tbk = rb * CHUNK
    steps = n_seq // tbk

    def blk(i):
        return (steps - 1 - i) if reverse else i

    def body(u_ref, t_ref, b_ref, c_ref, tab_ref, y_ref, hp_ref, h_scr, conv_scr, carry_scr):
        i = pl.program_id(1)

        @pl.when(i == 0)
        def _():
            carry_scr[...] = jnp.zeros_like(carry_scr)

        enter = carry_scr[0:1, :]
        uv = _rows_of_tokens(u_ref, conv_scr, rb)
        h_scr[...] = _dot(uv, b_ref[0])
        cr, ci = _scan_block(h_scr, tab_ref.at[0], carry_scr[:, 0:sb], carry_scr[:, sb:2 * sb],
                             reverse=reverse, tb=rb, sb=sb)
        carry_scr[:, 0:sb] = cr
        carry_scr[:, sb:2 * sb] = ci
        rows = lax.broadcasted_iota(jnp.int32, (rb, 1), 0)
        hprev = _shift_rows(h_scr[...], enter, rows, rb, down=not reverse)
        hp_ref[0] = hprev
        _tokens_of_rows(_apply_causal(uv, t_ref, reverse) + _dot_nt(hprev.astype(BF16), c_ref[0]), y_ref, conv_scr, rb)

    op = pl.BlockSpec((1, width, width), lambda g, i: (g, 0, 0))
    op2 = pl.BlockSpec((1, width, 2 * sb), lambda g, i: (g, 0, 0))
    tok = pl.BlockSpec((1, tbk, LANES), lambda g, i: (g, blk(i), 0))
    return pl.pallas_call(
        body, name=name, grid=(ngb, steps),
        out_shape=(jax.ShapeDtypeStruct((ngb, n_seq, LANES), BF16),
                   jax.ShapeDtypeStruct((ngb, n_seq // CHUNK, 2 * sb), F32)),
        in_specs=[tok, op, op2, op2, pl.BlockSpec((1, 8, SUBLANES, sb), lambda g, i: (g, 0, 0, 0))],
        out_specs=(tok, pl.BlockSpec((1, rb, 2 * sb), lambda g, i: (g, blk(i), 0))),
        scratch_shapes=[pltpu.VMEM((rb, 2 * sb), F32), pltpu.VMEM((tbk, LANES), F32),
                        pltpu.VMEM((SUBLANES, 2 * sb), F32)],
        compiler_params=_params(2),
    )(useq, t_op, bp, cp, tab)


def _s5_bwd(useq, dy, hprev, t_op, bp, cp, tab_adj, *, reverse, name, hosted=None):
    ngb, n_seq, _ = useq.shape
    sb = bp.shape[2] // 2
    width = CHUNK * LANES
    rb = _s5_row_block(n_seq)
    tbk = rb * CHUNK
    steps = n_seq // tbk

    def blk(i):
        return i if reverse else steps - 1 - i

    def body(u_ref, dy_ref, hp_ref, t_ref, b_ref, c_ref, taba_ref, du_ref, dt_ref, db_ref, dc_ref, da_ref,
             lam_scr, conv_scr, lcarry_scr, gedge_scr, da_scr):
        i = pl.program_id(1)
        first = i == 0

        _zero_first(first, lcarry_scr, gedge_scr, da_scr, dt_ref, db_ref, dc_ref)
        rows = lax.broadcasted_iota(jnp.int32, (rb, 1), 0)
        uv = _rows_of_tokens(u_ref, conv_scr, rb)
        dyv = _rows_of_tokens(dy_ref, conv_scr, rb)
        gy = _dot(dyv, c_ref[0])
        edge = gy[rb - 1:rb, :] if reverse else gy[0:1, :]
        lam_scr[...] = _shift_rows(gy, gedge_scr[...], rows, rb, down=reverse)
        gedge_scr[...] = edge
        lr, li = _scan_block(lam_scr, taba_ref.at[0], lcarry_scr[:, 0:sb], lcarry_scr[:, sb:2 * sb],
                             reverse=not reverse, tb=rb, sb=sb)
        lcarry_scr[:, 0:sb] = lr
        lcarry_scr[:, sb:2 * sb] = li

        lam = lam_scr[...]
        lam_bf = lam.astype(BF16)
        _tokens_of_rows(_apply_causal_t(dyv, t_ref, reverse) + _dot_nt(lam_bf, b_ref[0]), du_ref, conv_scr, rb)
        for tj in range(width // MXU_TILE):
            span, cols = _causal_span(tj, width // MXU_TILE, reverse, True), slice(tj * MXU_TILE, (tj + 1) * MXU_TILE)
            _acc(dt_ref.at[0, span, cols], _dot_tn(uv[:, span], dyv[:, cols]))
        _acc(db_ref.at[0], _dot_tn(uv, lam_bf))
        _acc(dc_ref.at[0], _dot_tn(dyv, hp_ref[0].astype(BF16)))
        lam_r, lam_i = lam[:, 0:sb], lam[:, sb:2 * sb]
        hp_r, hp_i = hp_ref[0, :, 0:sb], hp_ref[0, :, sb:2 * sb]
        da_scr[:, 0:sb] += _colsum(lam_r * hp_r + lam_i * hp_i)
        da_scr[:, sb:2 * sb] += _colsum(lam_i * hp_r - lam_r * hp_i)

        @pl.when(i == steps - 1)
        def _():
            da_ref[0] = da_scr[...]

    op = pl.BlockSpec((1, width, width), lambda g, i: (g, 0, 0))
    op2 = pl.BlockSpec((1, width, 2 * sb), lambda g, i: (g, 0, 0))
    tabs = pl.BlockSpec((1, 8, SUBLANES, sb), lambda g, i: (g, 0, 0, 0))
    tok = pl.BlockSpec((1, tbk, LANES), lambda g, i: (g, blk(i), 0))
    outs, extra = _call(
        body, name=name, grid=(ngb, steps),
        out_shape=[jax.ShapeDtypeStruct((ngb, n_seq, LANES), BF16),
                   jax.ShapeDtypeStruct((ngb, width, width), F32),
                   jax.ShapeDtypeStruct((ngb, width, 2 * sb), F32),
                   jax.ShapeDtypeStruct((ngb, width, 2 * sb), F32),
                   jax.ShapeDtypeStruct((ngb, 1, 2 * sb), F32)],
        in_specs=[tok, tok, pl.BlockSpec((1, rb, 2 * sb), lambda g, i: (g, blk(i), 0)), op, op2, op2, tabs],
        out_specs=[tok, op, op2, op2, pl.BlockSpec((1, 1, 2 * sb), lambda g, i: (g, 0, 0))],
        scratch_shapes=[pltpu.VMEM((rb, 2 * sb), F32), pltpu.VMEM((tbk, LANES), F32),
                        pltpu.VMEM((SUBLANES, 2 * sb), F32), pltpu.VMEM((1, 2 * sb), F32), pltpu.VMEM((1, 2 * sb), F32)],
        args=(useq, dy, hprev, t_op, bp, cp, tab_adj), hosted=hosted)
    return (*outs, extra)


def _glu_loss(useq, yf, yb, z, xhat0, ln0, gt, d_vec, w_glu, b_glu, w_out, ln1, target, *, offs, dy_rows, tb, name):
    ngb = useq.shape[0]
    n_tok, d_model = xhat0.shape
    e = ngb * LANES
    tb = min(tb, n_tok)
    assert all(off % tb == 0 for off in offs) and all(off % tb == 0 for _, off in dy_rows)
    nz = z.shape[0]

    def body(u_ref, yf_ref, yb_ref, z_ref, xh0_ref, g0_ref, b0_ref, gt_ref, d_ref, wg_ref, bg_ref, wo_ref, g1_ref,
             b1_ref, t_ref, loss_ref, dxr_ref, do_ref, gz_ref, gg_ref, dq_ref, dz_ref, dyf_ref, dyb_ref, dg1_ref, db1_ref,
             dgt_ref, dbg_ref, dd_ref, loss_scr, yl_scr, th_scr, s_scr, dg_scr):
        i = pl.program_id(0)
        _zero_first(i == 0, loss_scr, dg1_ref, db1_ref, dgt_ref, dbg_ref, dd_ref)
        zw = e // nz
        cs = min(512, zw)

        def z_slab(c0):
            return z_ref[c0 // zw, :, c0 % zw:c0 % zw + cs].astype(F32)

        for q in range(ngb):
            sl = slice(q * LANES, (q + 1) * LANES)
            yl = d_ref[:, sl] * u_ref[q].astype(F32) + yf_ref[q].astype(F32) + yb_ref[q].astype(F32)
            th = jnp.tanh(yl * (GELU_K + (GELU_K * GELU_C) * (yl * yl)))
            yl_scr[:, sl] = yl
            th_scr[:, sl] = th
            gg_ref[:, sl] = (0.5 * yl * (1.0 + th)).astype(BF16)
        g_all = gg_ref[...]
        for c0 in range(0, e, cs):
            sl = slice(c0, c0 + cs)
            s = _sigmoid(_dot(g_all, wg_ref[:, sl]) + bg_ref[:, sl])
            s_scr[:, sl] = s
            zf = z_slab(c0)
            g2 = 0.5 * yl_scr[:, sl] * (1.0 + th_scr[:, sl]) * s
            gz_ref[:, sl] = (g2 * (zf * _sigmoid(zf))).astype(BF16)
        o = _dot(gz_ref[...], wo_ref[...])
        x1 = xh0_ref[...] * g0_ref[...] + b0_ref[...]
        r = DN_ALPHA * x1 + gt_ref[...] * o
        rc = r - _rowmean(r)
        rstd = lax.rsqrt(_rowmean(rc * rc) + LN_EPS)
        xh = rc * rstd
        err = xh * g1_ref[...] + b1_ref[...] - t_ref[...]
        _acc(loss_scr, _colsum(err * err))
        dy = err * (1.0 / d_model)
        _acc(dg1_ref, _colsum(dy * xh))
        _acc(db1_ref, _colsum(dy))
        dxh = dy * g1_ref[...]
        dr = rstd * (dxh - _rowmean(dxh) - xh * _rowmean(dxh * xh))
        dxr_ref[...] = DN_ALPHA * dr
        _acc(dgt_ref, _colsum(dr * o))
        do_bf = (dr * gt_ref[...]).astype(BF16)
        do_ref[...] = do_bf
        for c0 in range(0, e, cs):
            sl = slice(c0, c0 + cs)
            dgz = _dot_nt(do_bf, wo_ref[sl, :])
            zf = z_slab(c0)
            sz = _sigmoid(zf)
            g = 0.5 * yl_scr[:, sl] * (1.0 + th_scr[:, sl])
            s = s_scr[:, sl]
            dg2 = dgz * (zf * sz)
            gs = g * s
            dz_ref[:, sl] = (dgz * gs * (sz * (1.0 + zf * (1.0 - sz)))).astype(BF16)
            dq = dg2 * (gs - gs * s)
            _acc(dbg_ref.at[:, sl], _colsum(dq))
            dq_ref[:, sl] = dq.astype(BF16)
            dg_scr[:, sl] = dg2 * s
        dq_all = dq_ref[...]
        for c0 in range(0, e, cs):
            dg_blk = dg_scr[:, c0:c0 + cs] + _dot_nt(dq_all, wg_ref[c0:c0 + cs, :])
            for q in range(c0 // LANES, (c0 + cs) // LANES):
                sl = slice(q * LANES, (q + 1) * LANES)
                yl = yl_scr[:, sl]
                th = th_scr[:, sl]
                slope = GELU_K + (3.0 * GELU_C * GELU_K) * (yl * yl)
                dgelu = 0.5 * ((1.0 + th) + yl * (1.0 - th * th) * slope)
                dyl = dg_blk[:, q * LANES - c0:(q + 1) * LANES - c0] * dgelu
                _acc(dd_ref.at[:, sl], _colsum(dyl * u_ref[q].astype(F32)))
                dyf_ref[q] = dyl.astype(BF16)
                dyb_ref[q] = dyl.astype(BF16)

        @pl.when(i == pl.num_programs(0) - 1)
        def _():
            loss_ref[...] = (0.5 / d_model) * jnp.sum(loss_scr[...], axis=1, keepdims=True)

    vec = pl.BlockSpec((1, d_model), lambda i: (0, 0))
    evec = pl.BlockSpec((1, e), lambda i: (0, 0))
    tok = pl.BlockSpec((tb, d_model), lambda i: (i, 0))
    wide = pl.BlockSpec((tb, e), lambda i: (i, 0))
    def gblk(off):
        return pl.BlockSpec((ngb, tb, LANES), functools.partial(lambda i, ob: (0, i + ob, 0), ob=off // tb))

    once = dict(pipeline_mode=pl.Buffered(1))
    tok_f = jax.ShapeDtypeStruct((n_tok, d_model), F32)
    tok_b = jax.ShapeDtypeStruct((n_tok, d_model), BF16)
    wide_b = jax.ShapeDtypeStruct((n_tok, e), BF16)
    vec_f = jax.ShapeDtypeStruct((1, d_model), F32)
    evec_f = jax.ShapeDtypeStruct((1, e), F32)
    return pl.pallas_call(
        body, name=name, grid=(n_tok // tb,),
        out_shape=(jax.ShapeDtypeStruct((1, 1), F32), tok_f, tok_b, wide_b, wide_b, wide_b, wide_b,
                   *[jax.ShapeDtypeStruct((ngb, total, LANES), BF16) for total, _ in dy_rows],
                   vec_f, vec_f, vec_f, evec_f, evec_f),
        in_specs=[gblk(offs[0]), gblk(offs[1]), gblk(offs[2]),
                  pl.BlockSpec((nz, tb, e // nz), lambda i: (0, i, 0)), tok, vec, vec, vec, evec,
                  pl.BlockSpec((e, e), lambda i: (0, 0), **once), evec,
                  pl.BlockSpec((e, d_model), lambda i: (0, 0), **once), vec, vec, tok],
        out_specs=(pl.BlockSpec((1, 1), lambda i: (0, 0)), tok, tok, wide, wide, wide, wide,
                   *[gblk(off) for _, off in dy_rows], vec, vec, vec, evec, evec),
        scratch_shapes=[pltpu.VMEM((1, d_model), F32)] + [pltpu.VMEM((tb, e), F32)] * 4,
        compiler_params=_params(1),
    )(useq, yf, yb, z, xhat0, ln0[0], ln0[1], gt, d_vec, w_glu, b_glu, w_out, ln1[0], ln1[1], target)


def _ssm_inbwd(duf, dub, w, xhat, rstd, ln, sc, gt_prev, f_prev, *, lat, row_f, row_b, tb, name):
    ngb = duf.shape[0]
    e = ngb * LANES
    n_tok, d_model = xhat.shape
    tb = min(tb, n_tok)
    obf, obb = row_f // tb, row_b // tb
    has_lat = lat is not None
    n_w = w.shape[0] if has_lat else w.shape[0] // 2

    def body(*refs):
        if has_lat:
            (duf_ref, dub_ref, dyl_ref, dz_ref, d_ref, dxr_ref, w_ref, xh_ref, rs_ref, g_ref, b_ref, sc_ref, gt_ref,
             f_ref, dp_ref, dr_ref, df_ref, dsc_ref, dsh_ref, dg_ref, db_ref, dgt_ref) = refs
        else:
            (duf_ref, dub_ref, w_ref, xh_ref, rs_ref, g_ref, b_ref, sc_ref, gt_ref, f_ref, dp_ref, dr_ref, df_ref,
             dsc_ref, dsh_ref, dg_ref, db_ref, dgt_ref) = refs
        _zero_first(pl.program_id(0) == 0, dsc_ref, dsh_ref, dg_ref, db_ref, dgt_ref)
        du = (jnp.concatenate([duf_ref[q] for q in range(ngb)], axis=1).astype(F32)
              + jnp.concatenate([dub_ref[q] for q in range(ngb)], axis=1).astype(F32))
        if has_lat:
            du = du + d_ref[...] * jnp.concatenate([dyl_ref[q] for q in range(ngb)], axis=1).astype(F32)
            dp_ref[:, e:2 * e] = dz_ref[...]
        else:
            dp_ref[:, e:2 * e] = jnp.zeros((tb, e), BF16)
        dp_ref[:, 0:e] = du.astype(BF16)
        dh = jnp.zeros((tb, d_model), F32)
        for j in range(n_w):
            dh = dh + _dot(dp_ref[:, j * d_model:(j + 1) * d_model], w_ref[j])
        xh = xh_ref[...]
        x1 = xh * g_ref[...] + b_ref[...]
        dx1 = dh * (1.0 + sc_ref[...])
        if has_lat:
            dx1 = dx1 + dxr_ref[...]
        _acc(dsc_ref, _colsum(dh * x1))
        _acc(dsh_ref, _colsum(dh))
        _acc(dg_ref, _colsum(dx1 * xh))
        _acc(db_ref, _colsum(dx1))
        dxh = dx1 * g_ref[...]
        dr = rs_ref[...] * (dxh - _rowmean(dxh) - xh * _rowmean(dxh * xh))
        dr_ref[...] = dr
        df_ref[...] = (dr * gt_ref[...]).astype(BF16)
        _acc(dgt_ref, _colsum(dr * f_ref[...].astype(F32)))

    vec = pl.BlockSpec((1, d_model), lambda i: (0, 0))
    tok = pl.BlockSpec((tb, d_model), lambda i: (i, 0))
    gblk = pl.BlockSpec((ngb, tb, LANES), lambda i: (0, i, 0))
    in_specs = [pl.BlockSpec((ngb, tb, LANES), lambda i: (0, i + obf, 0)),
                pl.BlockSpec((ngb, tb, LANES), lambda i: (0, i + obb, 0))]
    args = [duf, dub]
    if has_lat:
        in_specs += [gblk, pl.BlockSpec((tb, e), lambda i: (i, 0)), pl.BlockSpec((1, e), lambda i: (0, 0)), tok]
        args += list(lat)
    in_specs += [pl.BlockSpec(w.shape, lambda i: (0, 0, 0)), tok, pl.BlockSpec((tb, 1), lambda i: (i, 0)), vec, vec, vec,
                 vec, tok]
    args += [w, xhat, rstd, ln[0], ln[1], sc, gt_prev, f_prev]
    vec_f = jax.ShapeDtypeStruct((1, d_model), F32)
    return pl.pallas_call(
        body, name=name, grid=(n_tok // tb,),
        out_shape=(jax.ShapeDtypeStruct((n_tok, 2 * e), BF16), jax.ShapeDtypeStruct((n_tok, d_model), F32),
                   jax.ShapeDtypeStruct((n_tok, d_model), BF16), vec_f, vec_f, vec_f, vec_f, vec_f),
        in_specs=in_specs,
        out_specs=(pl.BlockSpec((tb, 2 * e), lambda i: (i, 0)), tok, tok, vec, vec, vec, vec, vec),
        compiler_params=_params(1),
    )(*args)


def _conv_bwd_a(df, w_out_t, p, yc, *, tb, name, hosted=None):
    _, n_tok, e = p.shape
    d_model = df.shape[1]
    tb = min(tb, n_tok)
    cs = _slab_width(e)

    def body(df_ref, wo_ref, bg_ref, z_ref, yc_ref, dbg_ref, dz_ref, dyc_ref):
        dfv = df_ref[...]
        for c0 in range(0, e, cs):
            sl = slice(c0, c0 + cs)
            dgv = _dot(dfv, wo_ref[:, sl])
            zf = z_ref[0, :, sl].astype(F32)
            sz = _sigmoid(zf)
            silu_z = zf * sz
            bg = bg_ref[0, :, sl].astype(F32)
            yc = yc_ref[:, sl].astype(F32)
            dbg_ref[:, sl] = (dgv * yc * silu_z).astype(BF16)
            dyc_ref[:, sl] = (dgv * bg * silu_z).astype(BF16)
            dz_ref[:, sl] = (dgv * bg * yc * (sz * (1.0 + zf * (1.0 - sz)))).astype(BF16)

    wide = pl.BlockSpec((tb, e), lambda i: (i, 0))
    shape = jax.ShapeDtypeStruct((n_tok, e), BF16)
    outs, extra = _call(
        body, name=name, grid=(n_tok // tb,), out_shape=[shape, shape, shape],
        in_specs=[pl.BlockSpec((tb, d_model), lambda i: (i, 0)), pl.BlockSpec((d_model, e), lambda i: (0, 0)),
                  pl.BlockSpec((1, tb, e), lambda i: (0, i, 0)), pl.BlockSpec((1, tb, e), lambda i: (3, i, 0)), wide],
        out_specs=[wide, wide, wide], scratch_shapes=[], args=(df, w_out_t, p, p, yc), hosted=hosted)
    return (*outs, extra)


def _conv_bwd_b(dyc, p, dbg, dz, conv_w, *, grid_mode, tb, name, hosted=None):
    _, n_tok, e = p.shape
    eh = e // 2
    if not grid_mode:
        tb = n_tok
    tb = min(tb, n_tok)
    nb = n_tok // tb
    hb = tb // GRID_W
    cs = _slab_width(e)

    def body(*refs):
        if grid_mode:
            dyc_ref, dycp_ref, dycn_ref, cg_ref, v_ref, dbg_ref, dz_ref, cw_ref, dp_ref, dcw_ref = refs
        else:
            dyc_ref, cg_ref, v_ref, dbg_ref, dz_ref, cw_ref, dp_ref, dcw_ref = refs
        i = pl.program_id(0)
        _zero_first(i == 0, dcw_ref)
        rows = lax.broadcasted_iota(jnp.int32, (tb, 1), 0)
        dp_ref[0] = dbg_ref[...]
        dp_ref[3] = dz_ref[...]
        for c0 in range(0, e, cs):
            sl = slice(c0, c0 + cs)
            dyc = dyc_ref[:, sl].astype(F32)
            w = cw_ref[:, sl]
            if grid_mode and c0 >= eh:
                hs = slice(c0 - eh, c0 - eh + cs)
                dprev = jnp.where(i > 0, dycp_ref[:, hs].astype(F32), 0.0)
                dnext = jnp.where(i < nb - 1, dycn_ref[:, hs].astype(F32), 0.0)
                if tb > GRID_W:
                    dm = jnp.concatenate([dprev, dyc[:tb - GRID_W]], axis=0)
                    dpl = jnp.concatenate([dyc[GRID_W:], dnext], axis=0)
                else:
                    dm, dpl = dprev, dnext
            else:
                dm, dpl = _shifted(dyc, rows, GRID_W if grid_mode else tb, tb)
            cg = cg_ref[0, :, sl].astype(F32)
            v = v_ref[0, :, sl].astype(F32)
            u = cg * v
            du = w[0:1] * dpl + w[1:2] * dyc + w[2:3] * dm
            dp_ref[1, :, sl] = (du * v).astype(BF16)
            dp_ref[2, :, sl] = (du * cg).astype(BF16)
            _acc(dcw_ref.at[:, sl], jnp.concatenate([_colsum(u * dpl), _colsum(u * dyc), _colsum(u * dm)], axis=0))

    n_hrows = n_tok // GRID_W
    wide = pl.BlockSpec((tb, e), lambda i: (i, 0))
    in_specs = [wide]
    args = [dyc]
    if grid_mode:
        in_specs += [pl.BlockSpec((GRID_W, eh), lambda i: (jnp.maximum(i * hb - 1, 0), 1)),
                     pl.BlockSpec((GRID_W, eh), lambda i: (jnp.minimum((i + 1) * hb, n_hrows - 1), 1))]
        args += [dyc, dyc]
    in_specs += [pl.BlockSpec((1, tb, e), lambda i: (1, i, 0)), pl.BlockSpec((1, tb, e), lambda i: (2, i, 0)), wide, wide,
                 pl.BlockSpec((3, e), lambda i: (0, 0))]
    args += [p, p, dbg, dz, conv_w]
    outs, extra = _call(
        body, name=name, grid=(nb,),
        out_shape=[jax.ShapeDtypeStruct((4, n_tok, e), BF16), jax.ShapeDtypeStruct((3, e), F32)],
        in_specs=in_specs,
        out_specs=[pl.BlockSpec((4, tb, e), lambda i: (0, i, 0)), pl.BlockSpec((3, e), lambda i: (0, 0))],
        scratch_shapes=[], args=args, hosted=hosted)
    return (*outs, extra)


def _conv_inbwd(dp, w, dr, x, sc, *, tb, name, hosted=None):
    n_chunks, n_tok, e = dp.shape
    d_model = x.shape[1]
    tb = min(tb, n_tok)

    def body(dp_ref, w_ref, dr_ref, x_ref, sc_ref, gx_ref, dsc_ref, dsh_ref):
        _zero_first(pl.program_id(0) == 0, dsc_ref, dsh_ref)
        dh = _dot(dp_ref[0], w_ref[0])
        for k in range(1, n_chunks):
            dh = dh + _dot(dp_ref[k], w_ref[k])
        gx_ref[...] = DN_ALPHA * dr_ref[...] + dh * (1.0 + sc_ref[...])
        _acc(dsc_ref, _colsum(dh * x_ref[...]))
        _acc(dsh_ref, _colsum(dh))

    vec = pl.BlockSpec((1, d_model), lambda i: (0, 0))
    tok = pl.BlockSpec((tb, d_model), lambda i: (i, 0))
    vec_f = jax.ShapeDtypeStruct((1, d_model), F32)
    outs, extra = _call(
        body, name=name, grid=(n_tok // tb,),
        out_shape=[jax.ShapeDtypeStruct((n_tok, d_model), F32), vec_f, vec_f],
        in_specs=[pl.BlockSpec((n_chunks, tb, e), lambda i: (0, i, 0)),
                  pl.BlockSpec((n_chunks, e, d_model), lambda i: (0, 0, 0), pipeline_mode=pl.Buffered(1)),
                  tok, tok, vec],
        out_specs=[tok, vec, vec],
        scratch_shapes=[], args=(dp, w, dr, x, sc), hosted=hosted)
    return (*outs, extra)


def _wgrad(a, b, *, n_chunks, tm, tl, init=None, name):
    n_tok, m = a.shape
    tl = min(tl, n_tok)
    chunked = b.ndim == 3
    cw = b.shape[2] if chunked else b.shape[1] // n_chunks
    has_init = init is not None

    def body(*refs):
        if has_init:
            a_ref, b_ref, init_ref, o_ref = refs
        else:
            a_ref, b_ref, o_ref = refs
        @pl.when(pl.program_id(2) == 0)
        def _():
            o_ref[0] = init_ref[0] if has_init else jnp.zeros_like(o_ref[0])

        o_ref[0] += _dot_tn(a_ref[...], b_ref[0] if chunked else b_ref[...])

    o_spec = pl.BlockSpec((1, tm, cw), lambda jm, jc, l: (jc, jm, 0))
    b_spec = (pl.BlockSpec((1, tl, cw), lambda jm, jc, l: (jc, l, 0)) if chunked
              else pl.BlockSpec((tl, cw), lambda jm, jc, l: (l, jc)))
    init_spec = pl.BlockSpec((1, tm, cw), lambda jm, jc, l: (jc, jm, 0), pipeline_mode=pl.Buffered(1))
    in_specs = [pl.BlockSpec((tl, tm), lambda jm, jc, l: (l, jm)), b_spec] + ([init_spec] if has_init else [])
    args = (a, b) + ((init,) if has_init else ())
    return pl.pallas_call(
        body, name=name, grid=(m // tm, n_chunks, n_tok // tl),
        out_shape=jax.ShapeDtypeStruct((n_chunks, m, cw), F32),
        in_specs=in_specs, out_specs=o_spec, compiler_params=_params(3),
    )(*args)


def _block_diag(t, ngb):
    g, p, n = t.shape
    gpb = g // ngb
    eye = jnp.eye(gpb, dtype=t.dtype)
    return jnp.einsum("bgpn,gh->bgphn", t.reshape(ngb, gpb, p, n), eye).reshape(ngb, gpb * p, gpb * n)


def _block_diag_t(mat, g, p, n):
    ngb = mat.shape[0]
    gpb = g // ngb
    eye = jnp.eye(gpb, dtype=mat.dtype)
    return jnp.einsum("bgphn,gh->bgpn", mat.reshape(ngb, gpb, p, gpb, n), eye).reshape(g, p, n)


def _scan_tables(pw_r, pw_i, ngb, reverse):
    _, g, n = pw_r.shape
    sb = g * n // ngb
    rows = jnp.arange(SUBLANES)
    kinds = []
    for step in (1, 2, 4):
        mask = ((rows < SUBLANES - step) if reverse else (rows >= step)).astype(F32)
        for part in (pw_r[step - 1], pw_i[step - 1]):
            kinds.append(part.reshape(ngb, 1, sb) * mask[None, :, None])
    for part in (pw_r, pw_i):
        pw = part[::-1] if reverse else part
        kinds.append(jnp.transpose(pw.reshape(SUBLANES, ngb, sb), (1, 0, 2)))
    return jnp.stack(kinds, axis=1)


def _flat(parts):
    return jnp.concatenate([p.reshape(-1) for p in parts])


def _unflat(vec, shapes):
    out, off = [], 0
    for s in shapes:
        size = math.prod(s)
        out.append(vec[off:off + size].reshape(s))
        off += size
    return out


def kernel(x, c, ctx, c_ctx, ada_w, ada_b, ln_g, ln_b, conv_w_in, conv_w, conv_w_out, ssm_w_in, ssm_lam_re, ssm_lam_im, ssm_log_step, ssm_b_re, ssm_b_im, ssm_c_re, ssm_c_im, ssm_d, ssm_w_glu, ssm_b_glu, ssm_w_out, loss_target, m_c_ctx, m_ada_w, m_ada_b, m_ln_g, m_ln_b, m_conv_w_in, m_conv_w, m_conv_w_out, m_ssm_w_in, m_ssm_lam_re, m_ssm_lam_im, m_ssm_log_step, m_ssm_b_re, m_ssm_b_im, m_ssm_c_re, m_ssm_c_im, m_ssm_d, m_ssm_w_glu, m_ssm_b_glu, m_ssm_w_out, v_c_ctx, v_ada_w, v_ada_b, v_ln_g, v_ln_b, v_conv_w_in, v_conv_w, v_conv_w_out, v_ssm_w_in, v_ssm_lam_re, v_ssm_lam_im, v_ssm_log_step, v_ssm_b_re, v_ssm_b_im, v_ssm_c_re, v_ssm_c_im, v_ssm_d, v_ssm_w_glu, v_ssm_b_glu, v_ssm_w_out):
    weights = dict(c_ctx=c_ctx, ada_w=ada_w, ada_b=ada_b, ln_g=ln_g, ln_b=ln_b, conv_w_in=conv_w_in, conv_w=conv_w,
                   conv_w_out=conv_w_out, ssm_w_in=ssm_w_in, ssm_lam_re=ssm_lam_re, ssm_lam_im=ssm_lam_im,
                   ssm_log_step=ssm_log_step, ssm_b_re=ssm_b_re, ssm_b_im=ssm_b_im, ssm_c_re=ssm_c_re,
                   ssm_c_im=ssm_c_im, ssm_d=ssm_d, ssm_w_glu=ssm_w_glu, ssm_b_glu=ssm_b_glu, ssm_w_out=ssm_w_out)
    mom_m = dict(c_ctx=m_c_ctx, ada_w=m_ada_w, ada_b=m_ada_b, ln_g=m_ln_g, ln_b=m_ln_b, conv_w_in=m_conv_w_in,
                 conv_w=m_conv_w, conv_w_out=m_conv_w_out, ssm_w_in=m_ssm_w_in, ssm_lam_re=m_ssm_lam_re,
                 ssm_lam_im=m_ssm_lam_im, ssm_log_step=m_ssm_log_step, ssm_b_re=m_ssm_b_re, ssm_b_im=m_ssm_b_im,
                 ssm_c_re=m_ssm_c_re, ssm_c_im=m_ssm_c_im, ssm_d=m_ssm_d, ssm_w_glu=m_ssm_w_glu,
                 ssm_b_glu=m_ssm_b_glu, ssm_w_out=m_ssm_w_out)
    mom_v = dict(c_ctx=v_c_ctx, ada_w=v_ada_w, ada_b=v_ada_b, ln_g=v_ln_g, ln_b=v_ln_b, conv_w_in=v_conv_w_in,
                 conv_w=v_conv_w, conv_w_out=v_conv_w_out, ssm_w_in=v_ssm_w_in, ssm_lam_re=v_ssm_lam_re,
                 ssm_lam_im=v_ssm_lam_im, ssm_log_step=v_ssm_log_step, ssm_b_re=v_ssm_b_re, ssm_b_im=v_ssm_b_im,
                 ssm_c_re=v_ssm_c_re, ssm_c_im=v_ssm_c_im, ssm_d=v_ssm_d, ssm_w_glu=v_ssm_w_glu,
                 ssm_b_glu=v_ssm_b_glu, ssm_w_out=v_ssm_w_out)
    names = list(weights)

    n_lat, d_model = x.shape[1], x.shape[2]
    n_ctx = ctx.shape[1]
    e = 2 * d_model
    n_grp, n_state, grp = ssm_lam_re.shape[2], ssm_lam_re.shape[3], ssm_b_re.shape[4]
    ngb = e // LANES
    ws = ada_w.shape[2]
    tb_tok = min(512, n_lat)
    n_seq = n_ctx + n_lat
    tb_glu = math.gcd(256, n_ctx)
    chip = 2 * lax.axis_index("x") + lax.axis_index("y")
    me = 2 * chip + lax.axis_index("c")
    chips, everyone, pair = ("x", "y"), MESH_AXES, ("c",)

    x2, ctx2, tgt2 = x[0], ctx[0], loss_target[0]

    wc_in_own = conv_w_in[0].astype(BF16)
    later_weights = _Hosted([(w[0].astype(BF16), chips, False) for w in (conv_w_out, ssm_w_in, ssm_w_glu, ssm_w_out)])
    small_full = _exchange(_flat([conv_w[0], ssm_d[0], ssm_b_glu[0]]).reshape(1, -1), chips, False, "ag_small")
    es = conv_w.shape[2]
    conv_w_full = jnp.transpose(small_full[:, 0, :3 * es].reshape(4, 3, es), (1, 0, 2)).reshape(3, e)
    d_full = small_full[:, 0, 3 * es:4 * es].reshape(1, e)
    b_glu_full = small_full[:, 0, 4 * es:5 * es].reshape(1, e)

    c_all = _exchange(c, everyone, False, "ag_c").reshape(8, d_model)
    cc2 = c_ctx.reshape(1, d_model)
    b_sh = lax.dynamic_slice_in_dim(ada_b, chip * ws, ws, axis=1).reshape(DEPTH, 1, ws)
    m_sh = _ada_fwd(c_all, cc2, ada_w, b_sh)
    m_all = _exchange(m_sh, chips, False, "ag_mod")
    m_full = jnp.transpose(m_all, (1, 2, 0, 3)).reshape(DEPTH, 16, 3 * d_model)
    m_lat = lax.dynamic_slice_in_dim(m_full, me, 1, axis=1)
    m_ctx = m_full[:, 8:9]

    def mods(m, i):
        return m[i, :, 0:d_model], m[i, :, d_model:2 * d_model], m[i, :, 2 * d_model:3 * d_model]

    sh0, sc0, gt0 = mods(m_lat, 0)
    sh1, sc1, gt1 = mods(m_lat, 1)
    shc0, scc0, gtc0 = mods(m_ctx, 0)
    shc1, scc1, _ = mods(m_ctx, 1)
    ln0 = (ln_g[0:1], ln_b[0:1])
    ln1 = (ln_g[1:2], ln_b[1:2])

    def lam_view(t):
        return jnp.transpose(t[0], (0, 2, 1)).reshape(2 * n_state, n_grp)

    def lam_back(t):
        return jnp.transpose(t.reshape(2, n_state, n_grp), (0, 2, 1)).reshape(ssm_lam_re.shape)

    def b_view(t):
        return jnp.transpose(t[0], (0, 2, 3, 1)).reshape(2 * n_state * grp, n_grp)

    def b_back(t):
        return jnp.transpose(t.reshape(2, n_state, grp, n_grp), (0, 3, 1, 2)).reshape(ssm_b_re.shape)

    def c_view(t):
        return jnp.transpose(t[0], (0, 2, 3, 1)).reshape(2 * grp * n_state, n_grp)

    def c_back(t):
        return jnp.transpose(t.reshape(2, grp, n_state, n_grp), (0, 3, 1, 2)).reshape(ssm_c_re.shape)

    def channel_major(t):
        return jnp.transpose(t.reshape(2 * n_state, grp, n_grp), (1, 0, 2))

    def by_group(t):
        return jnp.transpose(t.reshape(t.shape[0], 2, n_state, n_grp), (0, 1, 3, 2))

    lam_re2, lam_im2, log_step2 = lam_view(ssm_lam_re), lam_view(ssm_lam_im), ssm_log_step[0]
    b_re_t, b_im_t = channel_major(b_view(ssm_b_re)), channel_major(b_view(ssm_b_im))
    pw_r, pw_i, pq_r, pq_i, bbr, bbi = _zoh_fwd(lam_re2, lam_im2, log_step2, b_re_t, b_im_t)
    sbk = n_grp * n_state // ngb
    pw_r, pw_i, pq_r, pq_i = (by_group(t) for t in (pw_r, pw_i, pq_r, pq_i))
    bbr_g = jnp.transpose(by_group(bbr), (1, 2, 0, 3))
    bbi_g = jnp.transpose(by_group(bbi), (1, 2, 0, 3))

    def power_rows(pw, r, first):
        full = jnp.concatenate([jnp.full((1, n_grp, n_state), first, F32), pw[:, r]], axis=0)
        return jnp.transpose(full.reshape(CHUNK + 1, ngb, sbk), (1, 0, 2))

    s5 = []
    for r in range(2):
        prm = dict(bre=_block_diag(bbr_g[r], ngb), bim=_block_diag(bbi_g[r], ngb),
                   cre=_block_diag(ssm_c_re[0, r], ngb), cim=_block_diag(ssm_c_im[0, r], ngb),
                   wr=power_rows(pw_r, r, 1.0), wi=power_rows(pw_i, r, 0.0))
        half_rows = wc_in_own[r * (d_model // 2):(r + 1) * (d_model // 2)]
        t_op, bp_op, cp_op, (wc_in_half,) = _s5_ops(
            prm["bre"], prm["bim"], prm["cre"], prm["cim"], prm["wr"], prm["wi"], reverse=(r == 1),
            name=f"l1_s5_ops{r}", hosted=_Hosted([(half_rows, chips, False)]))
        s5.append(dict(
            prm, t=t_op, bp=bp_op, cp=cp_op, wc_in_half=wc_in_half,
            tab=_scan_tables(pq_r[:, r], pq_i[:, r], ngb, reverse=(r == 1)),
            tab_adj=_scan_tables(pq_r[:, r], -pq_i[:, r], ngb, reverse=(r == 0))))
    wc_in = jnp.concatenate([s5[0]["wc_in_half"], s5[1]["wc_in_half"]], axis=1)

    p0, h0, gathered = _inproj(x2, sc0, sh0, wc_in, tb=min(1024, n_lat), name="l0_inproj", hosted=later_weights)
    wc_out, ws_in, w_glu, ws_out = gathered
    wc_out, w_glu, ws_out = wc_out.reshape(e, d_model), w_glu.reshape(e, e), ws_out.reshape(e, d_model)
    wc_in_t, ws_in_t, wc_out_t = jnp.transpose(wc_in, (0, 2, 1)), jnp.transpose(ws_in, (0, 2, 1)), wc_out.T
    pc0, hc0 = _inproj(ctx2, scc0, shc0, wc_in, tb=tb_tok, name="l0_inproj_ctx")
    xhat0, rstd0, g0, yc0, f0 = _convgate(p0, x2, gt0, conv_w_full, wc_out, *ln0, grid_mode=True, tb=tb_tok, name="l0_conv")
    chat0, crstd0, gc0, ycc0, fc0 = _convgate(pc0, ctx2, gtc0, conv_w_full, wc_out, *ln0, grid_mode=False, tb=tb_tok,
                                              name="l0_conv_ctx")

    seq_rows = [(n_seq, n_ctx), (n_seq, 0)]

    def wide_chunk(w2):
        return jnp.transpose(w2, (1, 0, 2)).reshape(1, d_model, e)

    ws_u, ws_z = wide_chunk(ws_in[0:2]), wide_chunk(ws_in[2:4])
    useq_f, useq_b, h1 = _inproj_seq(xhat0, sc1, sh1, ws_u, ln0, tb=min(1024, n_lat), seq_rows=seq_rows,
                                     name="l1_inproj_u")
    z1, _ = _inproj(xhat0, sc1, sh1, ws_z, lnaff=ln0, tb=min(1024, n_lat), name="l1_inproj_z")
    uc, hc1 = _inproj(chat0, scc1, shc1, ws_u, lnaff=ln0, tb=tb_tok, gb_rows=[(n_ctx, 0)], name="l1_inproj_ctx")
    useq = [useq_f.at[:, 0:n_ctx].set(uc), useq_b.at[:, n_lat:].set(uc)]
    y_dir, hp_dir = [], []
    for r in range(2):
        yr, hcr = _s5_fwd(useq[r], s5[r]["t"], s5[r]["bp"], s5[r]["cp"], s5[r]["tab"], reverse=(r == 1),
                          name=f"l1_s5_fwd{r}")
        y_dir.append(yr)
        hp_dir.append(hcr)

    (loss, dxres, do1, gz1, gg1, dq1, dz1, dy_f, dy_b, dg1, db1, dgt1, dbglu, dd) = _glu_loss(
        useq[0], y_dir[0], y_dir[1], z1, xhat0, ln0, gt1, d_full, w_glu, b_glu_full, ws_out, ln1, tgt2,
        offs=(n_ctx, n_ctx, 0), dy_rows=seq_rows, tb=tb_glu, name="l1_glu_loss")
    no_dy = jnp.zeros((ngb, n_ctx, LANES), BF16)
    dy_dir = [dy_f.at[:, 0:n_ctx].set(no_dy), dy_b.at[:, n_lat:].set(no_dy)]

    tl = min(1024, n_lat)

    def owner_slices(name, full):
        w = weights[name]
        return full.reshape(8, math.prod(w.shape[:-1]) // 2, w.shape[-1])

    def scatter(named):
        return _Hosted([(owner_slices(name, full), everyone, True) for name, full in named])

    def siblings(names):
        return _Hosted([(_sum_parts(rs_parts[name], "sum_" + name), pair, False) for name in names])

    rs_parts, both_halves = {}, {}

    gw_glu = _wgrad(gg1, dq1, n_chunks=1, tm=e // 2, tl=min(2 * tl, n_lat), name="wg_glu")
    gw_ssm_out = _wgrad(gz1, do1, n_chunks=1, tm=e, tl=min(2 * tl, n_lat), name="wg_ssm_out")
    du_dir, s5_grads = [], []
    for r in range(2):
        if r == 0:
            hosted = scatter([("ssm_w_glu", gw_glu), ("ssm_w_out", gw_ssm_out)])
        else:
            hosted = siblings(["ssm_w_glu", "ssm_w_out"])
        dur, dt_op, dbp_op, dcp_op, da8, extra = _s5_bwd(useq[r], dy_dir[r], hp_dir[r], s5[r]["t"], s5[r]["bp"],
                                                         s5[r]["cp"], s5[r]["tab_adj"], reverse=(r == 1),
                                                         name=f"l1_s5_bwd{r}", hosted=hosted)
        if r == 0:
            rs_parts["ssm_w_glu"], rs_parts["ssm_w_out"] = extra
        else:
            both_halves["ssm_w_glu"], both_halves["ssm_w_out"] = extra
        du_dir.append(dur)
        prm = s5[r]
        s5_grads.append(functools.partial(
            _s5_ops_bwd, prm["bre"], prm["bim"], prm["cre"], prm["cim"], prm["wr"], prm["wi"], prm["wr"][:, 1:2],
            prm["wi"][:, 1:2], dt_op, dbp_op, dcp_op, da8, reverse=(r == 1), name=f"l1_s5_ops_bwd{r}"))
    dp1, dr0, df0, dsc1, dsh1, dg0, db0, dgt0 = _ssm_inbwd(
        du_dir[0], du_dir[1], ws_in_t, xhat0, rstd0, ln0, sc1, gt0, f0, lat=(dy_dir[1], dz1, d_full, dxres),
        row_f=n_ctx, row_b=0, tb=tb_glu, name="l1_inbwd")
    dpc1, drc0, dfc0, dscc1, dshc1, dgc0, dbc0, dgtc0 = _ssm_inbwd(
        du_dir[0], du_dir[1], ws_in_t, chat0, crstd0, ln0, scc1, gtc0, fc0, lat=None,
        row_f=0, row_b=n_lat, tb=n_ctx, name="l1_inbwd_ctx")

    def conv_backward(df, p, yc, dr, xin, sc, grid_mode, tag, hosted_a=None, hosted_b=None, hosted_in=None):
        dbg, dz, dyc, extra_a = _conv_bwd_a(df, wc_out_t, p, yc, tb=tb_tok, name="l0_bwd_a" + tag, hosted=hosted_a)
        dp, dcw, extra_b = _conv_bwd_b(dyc, p, dbg, dz, conv_w_full, grid_mode=grid_mode, tb=tb_tok,
                                       name="l0_bwd_b" + tag, hosted=hosted_b)
        gx, dsc, dsh, extra_in = _conv_inbwd(dp, wc_in_t, dr, xin, sc, tb=tb_tok, name="l0_inbwd" + tag,
                                             hosted=None if hosted_in is None else hosted_in(dp, extra_a + extra_b))
        return dp, dcw, gx, dsc, dsh, extra_in

    dpc0, dcwc0, _, dscc0, dshc0, _ = conv_backward(dfc0, pc0, ycc0, drc0, ctx2, scc0, False, "_ctx")
    gw_conv_out = _wgrad(g0, df0, n_chunks=1, tm=e, tl=tl, name="wg_conv_out",
                         init=_wgrad(gc0, dfc0, n_chunks=1, tm=e, tl=tl, name="wg_conv_out_ctx"))
    gw_ssm_in = _wgrad(h1, dp1, n_chunks=4, tm=d_model, tl=tl, name="wg_ssm_in",
                       init=_wgrad(hc1, dpc1, n_chunks=4, tm=d_model, tl=tl, name="wg_ssm_in_ctx"))
    gw_conv_in_ctx = _wgrad(hc0, dpc0, n_chunks=4, tm=d_model, tl=tl, name="wg_conv_in_ctx")

    def behind_inbwd(dp, arrived):
        rs_parts["conv_w_out"], rs_parts["ssm_w_in"] = arrived
        gw_conv_in = _wgrad(h0, dp, n_chunks=4, tm=d_model, tl=tl, name="wg_conv_in", init=gw_conv_in_ctx)
        both = siblings(["ssm_w_in", "conv_w_out"])
        return _Hosted(scatter([("conv_w_in", gw_conv_in)]).items + both.items)

    dp0, dcw0, grad_x, dsc0, dsh0, extra_in = conv_backward(
        df0, p0, yc0, dr0, x2, sc0, True, "", hosted_a=scatter([("conv_w_out", gw_conv_out)]),
        hosted_b=scatter([("ssm_w_in", gw_ssm_in)]), hosted_in=behind_inbwd)
    rs_parts["conv_w_in"], both_halves["ssm_w_in"], both_halves["conv_w_out"] = extra_in
    half_in = _sum_parts(rs_parts["conv_w_in"], "sum_conv_w_in")
    cut = half_in.shape[0] // 2
    *grads_r0, (top,) = s5_grads[0](hosted=_Hosted([(half_in[:cut], pair, False)]))
    *grads_r1, (bottom,) = s5_grads[1](hosted=_Hosted([(half_in[cut:], pair, False)]))
    both_halves["conv_w_in"] = jnp.concatenate([top, bottom], axis=1)
    s5_grads = [grads_r0, grads_r1]

    grads, deltas, new_m, new_v = {}, {}, {}, {}
    for name in ("ssm_w_glu", "ssm_w_out", "ssm_w_in", "conv_w_out", "conv_w_in"):
        w = weights[name]
        rows, cols = math.prod(w.shape[:-1]), w.shape[-1]
        both = both_halves[name].reshape(rows, cols)
        dlt, nm, nv = _adamw(w.reshape(rows, cols), both, mom_m[name].reshape(rows, cols),
                             mom_v[name].reshape(rows, cols), "adamw_" + name)
        grads[name], deltas[name] = both.reshape(w.shape), dlt.reshape(w.shape)
        new_m[name], new_v[name] = nm.reshape(w.shape), nv.reshape(w.shape)

    gpn = (n_grp, grp, n_state)
    small_parts = [
        jnp.concatenate([dg0 + dgc0, dg1], axis=0), jnp.concatenate([db0 + dbc0, db1], axis=0),
        dcw0 + dcwc0, dd, dbglu,
        jnp.stack([s5_grads[r][4] for r in range(2)]),
    ] + [jnp.stack([_block_diag_t(s5_grads[r][k], *gpn) for r in range(2)]) for k in range(4)] + [loss]
    small_shapes = [p.shape for p in small_parts]
    flat = _flat(small_parts)
    quantum = 8 * SUBLANES * LANES
    n_flat = -(-flat.shape[0] // quantum) * quantum
    flat = jnp.pad(flat, (0, n_flat - flat.shape[0])).reshape(8, n_flat // (8 * LANES), LANES)
    red = _sum_parts(_exchange(flat, everyone, True, "rs_small"), "sum_small")
    red = _exchange(red, everyone, False, "ag_small_grads").reshape(-1)
    g_ln_g, g_ln_b, g_conv_w, g_d, g_bglu, g_a, g_bbr, g_bbi, g_cre, g_cim, loss_sum = _unflat(red, small_shapes)

    def groups_minor(t, lead):
        return jnp.moveaxis(t, 1, -1).reshape(lead, n_grp)

    g_a = g_a.reshape(2, ngb, 2, sbk)
    dar = groups_minor(g_a[:, :, 0].reshape(2, n_grp, n_state), 2 * n_state)
    dai = groups_minor(g_a[:, :, 1].reshape(2, n_grp, n_state), 2 * n_state)
    dbbr_t = jnp.transpose(g_bbr, (2, 0, 3, 1)).reshape(grp, 2 * n_state, n_grp)
    dbbi_t = jnp.transpose(g_bbi, (2, 0, 3, 1)).reshape(grp, 2 * n_state, n_grp)
    z_lre, z_lim, z_ls, z_bre, z_bim = _zoh_bwd(lam_re2, lam_im2, log_step2, b_re_t, b_im_t, dar, dai, dbbr_t, dbbi_t)

    zero = jnp.zeros((1, d_model), F32)
    dm_rows = jnp.stack([
        jnp.stack([jnp.concatenate([dsh0, dsc0, dgt0], axis=1), jnp.concatenate([dshc0, dscc0, dgtc0], axis=1)]),
        jnp.stack([jnp.concatenate([dsh1, dsc1, dgt1], axis=1), jnp.concatenate([dshc1, dscc1, zero], axis=1)]),
    ]).reshape(DEPTH, 2, 3 * d_model)
    dm_all = _exchange(dm_rows, everyone, False, "ag_dmod")
    dm_sh = lax.dynamic_slice_in_dim(dm_all, chip * ws, ws, axis=3)
    g_ada_w, g_ada_b, ds_part = _ada_bwd(c_all, cc2, ada_w, dm_all, dm_sh)
    g_cctx = _cctx_grad(_exchange(ds_part, chips, False, "ag_dsctx"), cc2)

    grads["ada_w"] = g_ada_w
    dlt, nm, nv = _adamw(ada_w.reshape(-1, ws), g_ada_w.reshape(-1, ws), m_ada_w.reshape(-1, ws),
                         v_ada_w.reshape(-1, ws), "adamw_ada_w")
    deltas["ada_w"], new_m["ada_w"], new_v["ada_w"] = dlt.reshape(ada_w.shape), nm.reshape(ada_w.shape), nv.reshape(ada_w.shape)

    def chip_cols(full, rows):
        return lax.dynamic_slice_in_dim(full.reshape(rows, e), chip * es, es, axis=1)

    def same(t):
        return t

    def channel_minor_back(t):
        return jnp.transpose(t, (1, 0, 2)).reshape(2 * n_state * grp, n_grp)

    small = dict(
        c_ctx=(g_cctx, lambda t: t.reshape(1, d_model), lambda t: t.reshape(c_ctx.shape)),
        ada_b=(g_ada_b.reshape(ada_b.shape), same, same),
        ln_g=(g_ln_g, same, same), ln_b=(g_ln_b, same, same),
        conv_w=(chip_cols(g_conv_w, 3), lambda t: t[0], lambda t: t.reshape(conv_w.shape)),
        ssm_lam_re=(z_lre, lam_view, lam_back), ssm_lam_im=(z_lim, lam_view, lam_back),
        ssm_log_step=(z_ls, lambda t: t[0], lambda t: t.reshape(ssm_log_step.shape)),
        ssm_b_re=(channel_minor_back(z_bre), b_view, b_back), ssm_b_im=(channel_minor_back(z_bim), b_view, b_back),
        ssm_c_re=(groups_minor(g_cre, 2 * grp * n_state), c_view, c_back),
        ssm_c_im=(groups_minor(g_cim, 2 * grp * n_state), c_view, c_back),
        ssm_d=(chip_cols(g_d, 1), same, same), ssm_b_glu=(chip_cols(g_bglu, 1), same, same))
    for n, (g_view, view, back) in small.items():
        dlt, nm, nv = _adamw(view(weights[n]), g_view, view(mom_m[n]), view(mom_v[n]), "adamw_" + n)
        grads[n], deltas[n], new_m[n], new_v[n] = back(g_view), back(dlt), back(nm), back(nv)

    return (loss_sum.reshape(()), grad_x.reshape(x.shape), *[grads[n] for n in names], *[deltas[n] for n in names],
            *[new_m[n] for n in names], *[new_v[n] for n in names])
```
